```python
import math
import jax, jax.numpy as jnp
from jax import lax
import numpy as np

D_MODEL = 2048
BATCH = 8
SEQ = 4096
DEPTH = 1

GRID_W = 64
CTX_LEN = 256
HEAD_DIM = 128
N_HEADS = D_MODEL // HEAD_DIM
HEADS_A = N_HEADS // 2
HEADS_B = N_HEADS - HEADS_A
KV_A = 2
KV_B = 2
WINDOW = 128
BLOCK = 128
FFN_HIDDEN = -(-8 * D_MODEL // 768) * 256
ROPE_THETA = 10000.0
EPS = 1e-6
ATTN_SCALE = HEAD_DIM ** -0.5
DN_ALPHA = (2.0 * DEPTH) ** 0.25
DN_BETA = (8.0 * DEPTH) ** -0.25

QA_W = HEADS_A * HEAD_DIM
KA_W = KV_A * HEAD_DIM
QB_W = HEADS_B * HEAD_DIM
KB_W = KV_B * HEAD_DIM
IN_WIDTH = QA_W + 2 * KA_W + QB_W + 2 * KB_W
IN_SPLITS = [QA_W, QA_W + KA_W, QA_W + 2 * KA_W, QA_W + 2 * KA_W + QB_W, QA_W + 2 * KA_W + QB_W + KB_W]
MIX_WIDTH = QA_W + QB_W

kernel_name = 'hybrid_window_sink_global_qknorm_dit_layer'


def layer_norm(x, g, b):
    xf = x.astype(jnp.float32)
    mu = jnp.mean(xf, axis=-1, keepdims=True)
    var = jnp.mean(jnp.square(xf - mu), axis=-1, keepdims=True)
    return ((xf - mu) * lax.rsqrt(var + EPS) * g + b).astype(x.dtype)


def rms_norm(x, g):
    xf = x.astype(jnp.float32)
    return (xf * lax.rsqrt(jnp.mean(jnp.square(xf), axis=-1, keepdims=True) + EPS) * g).astype(x.dtype)


def axial_rope_tables(rows):
    row_ids = jnp.repeat(jnp.arange(rows, dtype=jnp.float32), GRID_W)
    col_ids = jnp.tile(jnp.arange(GRID_W, dtype=jnp.float32), rows)
    axis_dim = HEAD_DIM // 2
    inv_freq = jnp.power(ROPE_THETA, -jnp.arange(0, axis_dim, 2, dtype=jnp.float32) / axis_dim)
    ang_r = row_ids[:, None] * inv_freq
    ang_c = col_ids[:, None] * inv_freq
    ang = jnp.concatenate([ang_r, ang_r, ang_c, ang_c], axis=-1)
    return jnp.cos(ang)[:, None, :], jnp.sin(ang)[:, None, :]


def rotate_half(x):
    x1, x2 = jnp.split(x, 2, axis=-1)
    return jnp.concatenate([-x2, x1], axis=-1)


def apply_axial_rope(x, cos, sin):
    xf = x.astype(jnp.float32)
    x_row, x_col = jnp.split(xf, 2, axis=-1)
    rot = jnp.concatenate([rotate_half(x_row), rotate_half(x_col)], axis=-1)
    return (xf * cos + rot * sin).astype(x.dtype)


def ada_mods(cond, w_ada, b_ada):
    m = jnp.einsum('bd,de->be', jax.nn.silu(cond), w_ada) + b_ada
    return [t[:, None, :] for t in jnp.split(m, 6, axis=-1)]


def modulate(x, shift, scale):
    return x * (1.0 + scale) + shift


def mixer_qkv(u, w_in, q_norm_g, k_norm_g, rope):
    B, N, _ = u.shape
    h = jnp.einsum('bnd,de->bne', u, w_in)
    qa, ka, va, qb, kb, vb = jnp.split(h, IN_SPLITS, axis=-1)
    heads = lambda t, n: t.reshape(B, N, n, HEAD_DIM)
    qa, ka, va = heads(qa, HEADS_A), heads(ka, KV_A), heads(va, KV_A)
    qb = rms_norm(heads(qb, HEADS_B), q_norm_g)
    kb = rms_norm(heads(kb, KV_B), k_norm_g)
    vb = heads(vb, KV_B)
    if rope is not None:
        cos, sin = rope
        qa, ka, qb, kb = (apply_axial_rope(t, cos, sin) for t in (qa, ka, qb, kb))
    return qa, ka, va, qb, kb, vb


def window_sink_attention(q, k, v, k_ctx, v_ctx, sink_logit):
    B, N = q.shape[0], q.shape[1]
    nb = N // BLOCK
    G = HEADS_A // KV_A
    qb = q.reshape(B, nb, BLOCK, KV_A, G, HEAD_DIM)
    pad = ((0, 0), (BLOCK, BLOCK), (0, 0), (0, 0))
    kp = jnp.pad(k, pad).reshape(B, nb + 2, BLOCK, KV_A, HEAD_DIM)
    vp = jnp.pad(v, pad).reshape(B, nb + 2, BLOCK, KV_A, HEAD_DIM)
    band = lambda t: jnp.concatenate([t[:, :-2], t[:, 1:-1], t[:, 2:]], axis=2)
    kb, vb = band(kp), band(vp)
    s_loc = jnp.einsum('bnqkgd,bnskd->bnkgqs', qb, kb, preferred_element_type=jnp.float32) * ATTN_SCALE
    blk = jnp.arange(nb)[:, None] * BLOCK
    qpos = blk + jnp.arange(BLOCK)[None, :]
    kpos = blk - BLOCK + jnp.arange(3 * BLOCK)[None, :]
    valid = ((jnp.abs(qpos[:, :, None] - kpos[:, None, :]) <= WINDOW)
             & (kpos[:, None, :] >= 0) & (kpos[:, None, :] < N))
    s_loc = jnp.where(valid[None, :, None, None, :, :], s_loc, -jnp.inf)
    s_ctx = jnp.einsum('bnqkgd,bckd->bnkgqc', qb, k_ctx, preferred_element_type=jnp.float32) * ATTN_SCALE
    sink_col = jnp.broadcast_to(sink_logit.reshape(KV_A, G)[None, None, :, :, None, None].astype(jnp.float32),
                                s_loc.shape[:-1] + (1,))
    p = jax.nn.softmax(jnp.concatenate([s_loc, s_ctx, sink_col], axis=-1), axis=-1)
    p_loc = p[..., :3 * BLOCK].astype(v.dtype)
    p_ctx = p[..., 3 * BLOCK:-1].astype(v.dtype)
    o = (jnp.einsum('bnkgqs,bnskd->bnqkgd', p_loc, vb)
         + jnp.einsum('bnkgqc,bckd->bnqkgd', p_ctx, v_ctx))
    return o.reshape(B, N, HEADS_A * HEAD_DIM)


def global_attention(q, k, v, k_ctx, v_ctx):
    B, N = q.shape[0], q.shape[1]
    nb = N // BLOCK
    G = HEADS_B // KV_B
    keys = jnp.concatenate([k, k_ctx], axis=1)
    vals = jnp.concatenate([v, v_ctx], axis=1)
    qb = q.reshape(B, nb, BLOCK, KV_B, G, HEAD_DIM).transpose(1, 0, 2, 3, 4, 5)

    def one_block(q_blk):
        s = jnp.einsum('bqkgd,bskd->bkgqs', q_blk, keys, preferred_element_type=jnp.float32) * ATTN_SCALE
        p = jax.nn.softmax(s, axis=-1).astype(vals.dtype)
        return jnp.einsum('bkgqs,bskd->bqkgd', p, vals)

    o = lax.map(one_block, qb)
    return o.transpose(1, 0, 2, 3, 4, 5).reshape(B, N, HEADS_B * HEAD_DIM)


def context_attention(q, k, v, sink_logit=None):
    B, C, H = q.shape[0], q.shape[1], q.shape[2]
    KV = k.shape[2]
    G = H // KV
    qg = q.reshape(B, C, KV, G, HEAD_DIM)
    s = jnp.einsum('bqkgd,bskd->bkgqs', qg, k, preferred_element_type=jnp.float32) * ATTN_SCALE
    if sink_logit is not None:
        col = jnp.broadcast_to(sink_logit.reshape(KV, G)[None, :, :, None, None].astype(jnp.float32),
                               s.shape[:-1] + (1,))
        s = jnp.concatenate([s, col], axis=-1)
    p = jax.nn.softmax(s, axis=-1)[..., :C].astype(v.dtype)
    o = jnp.einsum('bkgqs,bskd->bqkgd', p, v)
    return o.reshape(B, C, H * HEAD_DIM)


def swiglu(u, w_gate, w_up, w_down):
    return (jax.nn.silu(u @ w_gate) * (u @ w_up)) @ w_down


def _fwd_setup_inputs(seed: int = 0) -> dict:
    key = jax.random.key(seed)
    ks = jax.random.split(key, 20)
    f32 = jnp.float32
    nrm = lambda k, shape, s: jax.random.normal(k, shape, f32) * s
    D, F, L = D_MODEL, FFN_HIDDEN, DEPTH
    return {
        'x': nrm(ks[0], (BATCH, SEQ, D), 1.0),
        'c': nrm(ks[1], (BATCH, D), 1.0),
        'ctx': nrm(ks[2], (BATCH, CTX_LEN, D), 1.0),
        'c_ctx': nrm(ks[3], (D,), 1.0),
        'w_ada': nrm(ks[4], (L, D, 6 * D), 0.3 * D ** -0.5),
        'b_ada': nrm(ks[5], (L, 6 * D), 0.02),
        'w_in': nrm(ks[6], (L, D, IN_WIDTH), D ** -0.5),
        'q_norm_g': 1.0 + nrm(ks[7], (L, HEAD_DIM), 0.02),
        'k_norm_g': 1.0 + nrm(ks[8], (L, HEAD_DIM), 0.02),
        'sink_logit': nrm(ks[9], (L, HEADS_A), 0.5),
        'w_out': nrm(ks[10], (L, MIX_WIDTH, D), DN_BETA * MIX_WIDTH ** -0.5),
        'ln1_g': 1.0 + nrm(ks[11], (L, D), 0.02),
        'ln1_b': nrm(ks[12], (L, D), 0.02),
        'w_gate': nrm(ks[13], (L, D, F), D ** -0.5),
        'w_up': nrm(ks[14], (L, D, F), D ** -0.5),
        'w_down': nrm(ks[15], (L, F, D), DN_BETA * F ** -0.5),
        'ln2_g': 1.0 + nrm(ks[16], (L, D), 0.02),
        'ln2_b': nrm(ks[17], (L, D), 0.02),
    }


def _fwd_reference(x, c, ctx, c_ctx, w_ada, b_ada, w_in, q_norm_g, k_norm_g, sink_logit,
              w_out, ln1_g, ln1_b, w_gate, w_up, w_down, ln2_g, ln2_b):
    ROWS = x.shape[1] // GRID_W
    rope = axial_rope_tables(ROWS)
    for layer in range(DEPTH):
        sh1, sc1, g1, sh2, sc2, g2 = ada_mods(c, w_ada[layer], b_ada[layer])
        csh1, csc1, cg1, csh2, csc2, cg2 = ada_mods(c_ctx[None, :], w_ada[layer], b_ada[layer])

        qa, ka, va, qb, kb, vb = mixer_qkv(modulate(x, sh1, sc1), w_in[layer],
                                           q_norm_g[layer], k_norm_g[layer], rope)
        qac, kac, vac, qbc, kbc, vbc = mixer_qkv(modulate(ctx, csh1, csc1), w_in[layer],
                                                 q_norm_g[layer], k_norm_g[layer], None)
        heads = jnp.concatenate([window_sink_attention(qa, ka, va, kac, vac, sink_logit[layer]),
                                 global_attention(qb, kb, vb, kbc, vbc)], axis=-1)
        x = layer_norm(DN_ALPHA * x + g1 * (heads @ w_out[layer]), ln1_g[layer], ln1_b[layer])

        x = layer_norm(DN_ALPHA * x + g2 * swiglu(modulate(x, sh2, sc2), w_gate[layer], w_up[layer], w_down[layer]),
                       ln2_g[layer], ln2_b[layer])

        if layer < DEPTH - 1:
            heads_c = jnp.concatenate([context_attention(qac, kac, vac, sink_logit[layer]),
                                       context_attention(qbc, kbc, vbc)], axis=-1)
            ctx = layer_norm(DN_ALPHA * ctx + cg1 * (heads_c @ w_out[layer]), ln1_g[layer], ln1_b[layer])
            ctx = layer_norm(DN_ALPHA * ctx + cg2 * swiglu(modulate(ctx, csh2, csc2), w_gate[layer], w_up[layer], w_down[layer]),
                             ln2_g[layer], ln2_b[layer])
    return x


import jax as _jax
import jax.numpy as _jnp

TWIN_FORMAT = 'train_step'
FWD_PARAMS = ['x', 'c', 'ctx', 'c_ctx', 'w_ada', 'b_ada', 'w_in', 'q_norm_g', 'k_norm_g', 'sink_logit', 'w_out', 'ln1_g', 'ln1_b', 'w_gate', 'w_up', 'w_down', 'ln2_g', 'ln2_b']
TWIN_WEIGHTS = ['c_ctx', 'w_ada', 'b_ada', 'w_in', 'q_norm_g', 'k_norm_g', 'sink_logit', 'w_out', 'ln1_g', 'ln1_b', 'w_gate', 'w_up', 'w_down', 'ln2_g', 'ln2_b']
TWIN_DIFF_INPUT = 'x'
TWIN_INPUTS = ['x', 'c', 'ctx', 'c_ctx', 'w_ada', 'b_ada', 'w_in', 'q_norm_g', 'k_norm_g', 'sink_logit', 'w_out', 'ln1_g', 'ln1_b', 'w_gate', 'w_up', 'w_down', 'ln2_g', 'ln2_b', 'loss_target', 'm_c_ctx', 'm_w_ada', 'm_b_ada', 'm_w_in', 'm_q_norm_g', 'm_k_norm_g', 'm_sink_logit', 'm_w_out', 'm_ln1_g', 'm_ln1_b', 'm_w_gate', 'm_w_up', 'm_w_down', 'm_ln2_g', 'm_ln2_b', 'v_c_ctx', 'v_w_ada', 'v_b_ada', 'v_w_in', 'v_q_norm_g', 'v_k_norm_g', 'v_sink_logit', 'v_w_out', 'v_ln1_g', 'v_ln1_b', 'v_w_gate', 'v_w_up', 'v_w_down', 'v_ln2_g', 'v_ln2_b']
TWIN_OUTPUTS = ['loss', 'grad_x', 'grad_c_ctx', 'grad_w_ada', 'grad_b_ada', 'grad_w_in', 'grad_q_norm_g', 'grad_k_norm_g', 'grad_sink_logit', 'grad_w_out', 'grad_ln1_g', 'grad_ln1_b', 'grad_w_gate', 'grad_w_up', 'grad_w_down', 'grad_ln2_g', 'grad_ln2_b', 'delta_c_ctx', 'delta_w_ada', 'delta_b_ada', 'delta_w_in', 'delta_q_norm_g', 'delta_k_norm_g', 'delta_sink_logit', 'delta_w_out', 'delta_ln1_g', 'delta_ln1_b', 'delta_w_gate', 'delta_w_up', 'delta_w_down', 'delta_ln2_g', 'delta_ln2_b', 'new_m_c_ctx', 'new_m_w_ada', 'new_m_b_ada', 'new_m_w_in', 'new_m_q_norm_g', 'new_m_k_norm_g', 'new_m_sink_logit', 'new_m_w_out', 'new_m_ln1_g', 'new_m_ln1_b', 'new_m_w_gate', 'new_m_w_up', 'new_m_w_down', 'new_m_ln2_g', 'new_m_ln2_b', 'new_v_c_ctx', 'new_v_w_ada', 'new_v_b_ada', 'new_v_w_in', 'new_v_q_norm_g', 'new_v_k_norm_g', 'new_v_sink_logit', 'new_v_w_out', 'new_v_ln1_g', 'new_v_ln1_b', 'new_v_w_gate', 'new_v_w_up', 'new_v_w_down', 'new_v_ln2_g', 'new_v_ln2_b']
TWIN_LEAF_KINDS = {'loss': 'loss', 'grad_x': 'grad_x', 'grad_c_ctx': 'grad_w', 'grad_w_ada': 'grad_w', 'grad_b_ada': 'grad_w', 'grad_w_in': 'grad_w', 'grad_q_norm_g': 'grad_w', 'grad_k_norm_g': 'grad_w', 'grad_sink_logit': 'grad_w', 'grad_w_out': 'grad_w', 'grad_ln1_g': 'grad_w', 'grad_ln1_b': 'grad_w', 'grad_w_gate': 'grad_w', 'grad_w_up': 'grad_w', 'grad_w_down': 'grad_w', 'grad_ln2_g': 'grad_w', 'grad_ln2_b': 'grad_w', 'delta_c_ctx': 'delta_w', 'delta_w_ada': 'delta_w', 'delta_b_ada': 'delta_w', 'delta_w_in': 'delta_w', 'delta_q_norm_g': 'delta_w', 'delta_k_norm_g': 'delta_w', 'delta_sink_logit': 'delta_w', 'delta_w_out': 'delta_w', 'delta_ln1_g': 'delta_w', 'delta_ln1_b': 'delta_w', 'delta_w_gate': 'delta_w', 'delta_w_up': 'delta_w', 'delta_w_down': 'delta_w', 'delta_ln2_g': 'delta_w', 'delta_ln2_b': 'delta_w', 'new_m_c_ctx': 'new_m', 'new_m_w_ada': 'new_m', 'new_m_b_ada': 'new_m', 'new_m_w_in': 'new_m', 'new_m_q_norm_g': 'new_m', 'new_m_k_norm_g': 'new_m', 'new_m_sink_logit': 'new_m', 'new_m_w_out': 'new_m', 'new_m_ln1_g': 'new_m', 'new_m_ln1_b': 'new_m', 'new_m_w_gate': 'new_m', 'new_m_w_up': 'new_m', 'new_m_w_down': 'new_m', 'new_m_ln2_g': 'new_m', 'new_m_ln2_b': 'new_m', 'new_v_c_ctx': 'new_v', 'new_v_w_ada': 'new_v', 'new_v_b_ada': 'new_v', 'new_v_w_in': 'new_v', 'new_v_q_norm_g': 'new_v', 'new_v_k_norm_g': 'new_v', 'new_v_sink_logit': 'new_v', 'new_v_w_out': 'new_v', 'new_v_ln1_g': 'new_v', 'new_v_ln1_b': 'new_v', 'new_v_w_gate': 'new_v', 'new_v_w_up': 'new_v', 'new_v_w_down': 'new_v', 'new_v_ln2_g': 'new_v', 'new_v_ln2_b': 'new_v'}


def _forward(args):
    return _fwd_reference(*[args[k] for k in FWD_PARAMS])


def _output_shape():
    def fwd():
        inp = _fwd_setup_inputs(0)
        return _fwd_reference(*[inp[k] for k in FWD_PARAMS])
    out = _jax.eval_shape(fwd)
    return out.shape, out.dtype

N_MICROBATCH = 1
ADAM_LR = 0.001
ADAM_B1 = 0.9
ADAM_B2 = 0.999
ADAM_EPS = 1e-08
ADAM_WD = 0.01
ADAM_STEP = 10
PER_EXAMPLE_BATCH_AXIS = {'x': 0, 'c': 0, 'ctx': 0, 'loss_target': 0}
SHARED_INPUTS = []
_WEIGHT_DTYPES = {'c_ctx': _jnp.float32, 'w_ada': _jnp.float32, 'b_ada': _jnp.float32, 'w_in': _jnp.float32, 'q_norm_g': _jnp.float32, 'k_norm_g': _jnp.float32, 'sink_logit': _jnp.float32, 'w_out': _jnp.float32, 'ln1_g': _jnp.float32, 'ln1_b': _jnp.float32, 'w_gate': _jnp.float32, 'w_up': _jnp.float32, 'w_down': _jnp.float32, 'ln2_g': _jnp.float32, 'ln2_b': _jnp.float32}
MOMENT_SCALE = {'c_ctx': 8.573894e-04, 'w_ada': 7.990202e-03, 'b_ada': 1.427927e-02, 'w_in': 1.451515e-03, 'q_norm_g': 1.560728e-03, 'k_norm_g': 1.500226e-03, 'sink_logit': 4.260807e-05, 'w_out': 2.882501e-03, 'ln1_g': 5.783016e-01, 'ln1_b': 2.747483e-01, 'w_gate': 3.172712e-03, 'w_up': 3.080433e-03, 'w_down': 8.586742e-03, 'ln2_g': 1.600352e+01, 'ln2_b': 3.936852e-01}


def _to_microbatches(a, axis):
    t = _jnp.moveaxis(a, axis, 0)
    t = t.reshape((N_MICROBATCH, t.shape[0] // N_MICROBATCH) + t.shape[1:])
    return _jnp.moveaxis(t, 1, axis + 1)


def setup_inputs(seed: int = 0) -> dict:
    inp = _fwd_setup_inputs(seed)
    key = _jax.random.fold_in(_jax.random.key(seed), 7919)
    shape, _ = _output_shape()
    out = dict(inp)
    out["loss_target"] = _jax.random.normal(_jax.random.fold_in(key, 0), shape, _jnp.float32)
    for i, name in enumerate(TWIN_WEIGHTS):
        w = inp[name].astype(_jnp.float32)
        if MOMENT_SCALE is None:
            s = _jnp.sqrt(_jnp.mean(_jnp.square(w)) + 1e-30)
        else:
            s = MOMENT_SCALE[name]
        km, kv = _jax.random.split(_jax.random.fold_in(key, i + 1))
        out[name] = w
        out["m_" + name] = s * _jax.random.normal(km, w.shape, _jnp.float32)
        out["v_" + name] = (s * s) * _jax.random.uniform(kv, w.shape, _jnp.float32, 0.5, 1.5)
    if N_MICROBATCH > 1:
        for name, axis in PER_EXAMPLE_BATCH_AXIS.items():
            out[name] = _to_microbatches(out[name], axis)
    return {'x': out['x'], 'c': out['c'], 'ctx': out['ctx'], 'c_ctx': out['c_ctx'], 'w_ada': out['w_ada'], 'b_ada': out['b_ada'], 'w_in': out['w_in'], 'q_norm_g': out['q_norm_g'], 'k_norm_g': out['k_norm_g'], 'sink_logit': out['sink_logit'], 'w_out': out['w_out'], 'ln1_g': out['ln1_g'], 'ln1_b': out['ln1_b'], 'w_gate': out['w_gate'], 'w_up': out['w_up'], 'w_down': out['w_down'], 'ln2_g': out['ln2_g'], 'ln2_b': out['ln2_b'], 'loss_target': out['loss_target'], 'm_c_ctx': out['m_c_ctx'], 'm_w_ada': out['m_w_ada'], 'm_b_ada': out['m_b_ada'], 'm_w_in': out['m_w_in'], 'm_q_norm_g': out['m_q_norm_g'], 'm_k_norm_g': out['m_k_norm_g'], 'm_sink_logit': out['m_sink_logit'], 'm_w_out': out['m_w_out'], 'm_ln1_g': out['m_ln1_g'], 'm_ln1_b': out['m_ln1_b'], 'm_w_gate': out['m_w_gate'], 'm_w_up': out['m_w_up'], 'm_w_down': out['m_w_down'], 'm_ln2_g': out['m_ln2_g'], 'm_ln2_b': out['m_ln2_b'], 'v_c_ctx': out['v_c_ctx'], 'v_w_ada': out['v_w_ada'], 'v_b_ada': out['v_b_ada'], 'v_w_in': out['v_w_in'], 'v_q_norm_g': out['v_q_norm_g'], 'v_k_norm_g': out['v_k_norm_g'], 'v_sink_logit': out['v_sink_logit'], 'v_w_out': out['v_w_out'], 'v_ln1_g': out['v_ln1_g'], 'v_ln1_b': out['v_ln1_b'], 'v_w_gate': out['v_w_gate'], 'v_w_up': out['v_w_up'], 'v_w_down': out['v_w_down'], 'v_ln2_g': out['v_ln2_g'], 'v_ln2_b': out['v_ln2_b']}


def _loss(weights, diff, rest, loss_target):
    with _jax.named_scope("forward"):
        args = {**rest, TWIN_DIFF_INPUT: diff, **{k: w.astype(_WEIGHT_DTYPES[k]) for k, w in weights.items()}}
        y = _forward(args)
    with _jax.named_scope("loss_head"):
        err = _jnp.square(y.astype(_jnp.float32) - loss_target)
        return 0.5 * _jnp.sum(_jnp.mean(err, axis=-1)) if err.ndim else 0.5 * err


def _adamw(w, g, m, v):
    m = ADAM_B1 * m + (1.0 - ADAM_B1) * g
    v = ADAM_B2 * v + (1.0 - ADAM_B2) * _jnp.square(g)
    m_hat = m / (1.0 - ADAM_B1 ** ADAM_STEP)
    v_hat = v / (1.0 - ADAM_B2 ** ADAM_STEP)
    delta = -ADAM_LR * (m_hat / (_jnp.sqrt(v_hat) + ADAM_EPS) + ADAM_WD * w)
    return delta, m, v


def reference(x, c, ctx, c_ctx, w_ada, b_ada, w_in, q_norm_g, k_norm_g, sink_logit, w_out, ln1_g, ln1_b, w_gate, w_up, w_down, ln2_g, ln2_b, loss_target, m_c_ctx, m_w_ada, m_b_ada, m_w_in, m_q_norm_g, m_k_norm_g, m_sink_logit, m_w_out, m_ln1_g, m_ln1_b, m_w_gate, m_w_up, m_w_down, m_ln2_g, m_ln2_b, v_c_ctx, v_w_ada, v_b_ada, v_w_in, v_q_norm_g, v_k_norm_g, v_sink_logit, v_w_out, v_ln1_g, v_ln1_b, v_w_gate, v_w_up, v_w_down, v_ln2_g, v_ln2_b):
    given = dict(x=x, c=c, ctx=ctx, c_ctx=c_ctx, w_ada=w_ada, b_ada=b_ada, w_in=w_in, q_norm_g=q_norm_g, k_norm_g=k_norm_g, sink_logit=sink_logit, w_out=w_out, ln1_g=ln1_g, ln1_b=ln1_b, w_gate=w_gate, w_up=w_up, w_down=w_down, ln2_g=ln2_g, ln2_b=ln2_b, loss_target=loss_target, m_c_ctx=m_c_ctx, m_w_ada=m_w_ada, m_b_ada=m_b_ada, m_w_in=m_w_in, m_q_norm_g=m_q_norm_g, m_k_norm_g=m_k_norm_g, m_sink_logit=m_sink_logit, m_w_out=m_w_out, m_ln1_g=m_ln1_g, m_ln1_b=m_ln1_b, m_w_gate=m_w_gate, m_w_up=m_w_up, m_w_down=m_w_down, m_ln2_g=m_ln2_g, m_ln2_b=m_ln2_b, v_c_ctx=v_c_ctx, v_w_ada=v_w_ada, v_b_ada=v_b_ada, v_w_in=v_w_in, v_q_norm_g=v_q_norm_g, v_k_norm_g=v_k_norm_g, v_sink_logit=v_sink_logit, v_w_out=v_w_out, v_ln1_g=v_ln1_g, v_ln1_b=v_ln1_b, v_w_gate=v_w_gate, v_w_up=v_w_up, v_w_down=v_w_down, v_ln2_g=v_ln2_g, v_ln2_b=v_ln2_b)
    weights = {n: given[n] for n in TWIN_WEIGHTS}
    shared = {n: given[n] for n in SHARED_INPUTS}
    per_example = {n: given[n] for n in ['x', 'c', 'ctx']}
    grad_fn = _jax.value_and_grad(_loss, argnums=(0, 1))

    def one_microbatch(ex, loss_target):
        ex = dict(ex)
        diff = ex.pop(TWIN_DIFF_INPUT)
        return grad_fn(weights, diff, {**shared, **ex}, loss_target)

    if N_MICROBATCH == 1:
        loss, (grad_w, grad_x) = one_microbatch(per_example, given["loss_target"])
    else:
        def body(carry, xs):
            loss_sum, grad_sum = carry
            l_k, (gw_k, gx_k) = one_microbatch(xs[0], xs[1])
            with _jax.named_scope("update"):
                return (loss_sum + l_k, _jax.tree.map(_jnp.add, grad_sum, gw_k)), gx_k

        init = (_jnp.zeros((), _jnp.float32), _jax.tree.map(_jnp.zeros_like, weights))
        (loss, grad_w), grad_x = _jax.lax.scan(body, init, (per_example, given["loss_target"]))
    with _jax.named_scope("update"):
        delta_w, new_m, new_v = {}, {}, {}
        for n in TWIN_WEIGHTS:
            delta_w[n], new_m[n], new_v[n] = _adamw(weights[n], grad_w[n], given["m_" + n], given["v_" + n])
    return (loss, grad_x, *[grad_w[n] for n in TWIN_WEIGHTS], *[delta_w[n] for n in TWIN_WEIGHTS],
            *[new_m[n] for n in TWIN_WEIGHTS], *[new_v[n] for n in TWIN_WEIGHTS])
```

```python
import functools

import jax
import jax.numpy as jnp
from jax import lax
from jax.experimental import pallas as pl
from jax.experimental.pallas import tpu as pltpu

F32 = jnp.float32
BF16 = jnp.bfloat16

NDEV = 8
HEAD = 128
CTX = 256
GRID_W = 64
WINDOW = 128
ROPE_THETA = 10000.0
EPS = 1e-6
SCALE = HEAD ** -0.5
ALPHA = 2.0 ** 0.25
FFN_SHARD = 704
FFN_PAD = 768
IN_SHARD = 384
NEG = -1e30

ADAM_LR = 0.001
ADAM_B1 = 0.9
ADAM_B2 = 0.999
ADAM_EPS = 1e-08
ADAM_WD = 0.01
ADAM_STEP = 10

VMEM_CAP = 56 * 1024 * 1024

_KINDS = ["rope"] * 10 + ["none"] * 2 + ["qnorm"] * 8 + ["knorm"] * 2 + ["none"] * 2

_NT = (((1,), (1,)), ((), ()))
_TN = (((0,), (0,)), ((), ()))


def _pallas(body, **kw):
    return pl.pallas_call(body, **kw)


def _params(vmem_bytes):
    return pltpu.CompilerParams(vmem_limit_bytes=int(min(VMEM_CAP, vmem_bytes)))


def _mb(n):
    return int(n * 1024 * 1024)


def _sigmoid(x):
    return 1.0 / (1.0 + jnp.exp(-x))


def _colsum(a):
    return jnp.sum(a, axis=0, keepdims=True)


def _rowmean(a):
    return jnp.mean(a, axis=-1, keepdims=True)


def _exchange(src, scatter, name):
    blk = src.shape[1:] if scatter else src.shape

    def body(src_ref, out_ref, send_sems, recv_sems, local_sem):
        x, y, c = lax.axis_index("x"), lax.axis_index("y"), lax.axis_index("c")
        me = 4 * x + 2 * y + c
        copies = []
        for t in range(1, NDEV):
            px = 1 - x if (t >> 2) & 1 else x
            py = 1 - y if (t >> 1) & 1 else y
            pc = 1 - c if t & 1 else c
            peer = 4 * px + 2 * py + pc
            cp = pltpu.make_async_remote_copy(
                src_ref=src_ref.at[peer] if scatter else src_ref,
                dst_ref=out_ref.at[me],
                send_sem=send_sems.at[t - 1],
                recv_sem=recv_sems.at[t - 1],
                device_id=(px, py, pc),
                device_id_type=pl.DeviceIdType.MESH,
            )
            cp.start()
            copies.append(cp)
        own = pltpu.make_async_copy(src_ref.at[me] if scatter else src_ref, out_ref.at[me], local_sem)
        own.start()
        for cp in copies:
            cp.wait()
        own.wait()

    return _pallas(
        body, name=name,
        out_shape=jax.ShapeDtypeStruct((NDEV,) + tuple(blk), src.dtype),
        in_specs=[pl.BlockSpec(memory_space=pl.ANY)],
        out_specs=pl.BlockSpec(memory_space=pl.ANY),
        scratch_shapes=[pltpu.SemaphoreType.DMA((NDEV - 1,)), pltpu.SemaphoreType.DMA((NDEV - 1,)),
                        pltpu.SemaphoreType.DMA(())],
    )(src)


def _ada_fwd(c_all, w, bias):
    r, d = c_all.shape
    e = w.shape[1]
    tn = 512

    def body(c_ref, w_ref, b_ref, o_ref):
        cv = c_ref[...]
        s = (cv * _sigmoid(cv)).astype(BF16)
        o_ref[...] = jnp.dot(s, w_ref[...].astype(BF16), preferred_element_type=F32) + b_ref[...]

    return _pallas(
        body, name="ada_fwd", grid=(e // tn,),
        out_shape=jax.ShapeDtypeStruct((r, e), F32),
        in_specs=[pl.BlockSpec((r, d), lambda j: (0, 0)), pl.BlockSpec((d, tn), lambda j: (0, j)),
                  pl.BlockSpec((1, tn), lambda j: (0, j))],
        out_specs=pl.BlockSpec((r, tn), lambda j: (0, j)),
        compiler_params=_params(_mb(24)),
    )(c_all, w, bias)


def _ada_bwd(dm16, c_all, w):
    d, e = w.shape
    tn = 512

    def body(dm_ref, c_ref, w_ref, dw_ref, dr_ref):
        j = pl.program_id(0)
        dm = dm_ref[...]
        rid = lax.broadcasted_iota(jnp.int32, dm.shape, 0)
        ctx_sum = jnp.sum(jnp.where(rid >= 8, dm, 0.0), axis=0, keepdims=True)
        rows = jnp.where(rid < 8, dm, jnp.where(rid == 8, jnp.broadcast_to(ctx_sum, dm.shape), 0.0)).astype(BF16)
        cv = c_ref[...]
        s = (cv * _sigmoid(cv)).astype(BF16)
        dw_ref[...] = lax.dot_general(s, rows, _TN, preferred_element_type=F32)
        part = lax.dot_general(rows, w_ref[...].astype(BF16), _NT, preferred_element_type=F32)

        @pl.when(j == 0)
        def _():
            dr_ref[...] = part

        @pl.when(j > 0)
        def _():
            dr_ref[...] += part

    return _pallas(
        body, name="ada_bwd", grid=(e // tn,),
        out_shape=(jax.ShapeDtypeStruct((d, e), F32), jax.ShapeDtypeStruct((16, d), F32)),
        in_specs=[pl.BlockSpec((16, tn), lambda j: (0, j)), pl.BlockSpec((16, d), lambda j: (0, 0)),
                  pl.BlockSpec((d, tn), lambda j: (0, j))],
        out_specs=(pl.BlockSpec((d, tn), lambda j: (0, j)), pl.BlockSpec((16, d), lambda j: (0, 0))),
        compiler_params=_params(_mb(32)),
    )(dm16, c_all, w)


def _rope(v, cos, sa, sb):
    return v * cos + (pltpu.roll(v, 96, 1) * sa + pltpu.roll(v, 32, 1) * sb)


def _rope_t(dt, cos, sa, sb):
    return dt * cos + (pltpu.roll(dt * sa, 32, 1) + pltpu.roll(dt * sb, 96, 1))


def _qkv_fwd(x, ct, sc, sh, win_g, qg, kg, cos, sa, sb):
    n, d = x.shape
    tm = CTX
    nlat = n // tm
    na = n + CTX
    wcols = NDEV * IN_SHARD

    def body(x_ref, ct_ref, sc_ref, sh_ref, w_ref, qg_ref, kg_ref, cos_ref, sa_ref, sb_ref, u_ref, h_ref, t_ref):
        i = pl.program_id(0)
        xin = jnp.where(i == nlat, ct_ref[...], x_ref[...])
        u = (xin * (1.0 + sc_ref[0]) + sh_ref[0]).astype(BF16)
        u_ref[...] = u
        cos, sa, sb = cos_ref[...], sa_ref[...], sb_ref[...]
        for j in range(NDEV):
            h = jnp.dot(u, w_ref[j], preferred_element_type=F32)
            h_ref[:, j * IN_SHARD:(j + 1) * IN_SHARD] = h
            for hh in range(3):
                hd = 3 * j + hh
                v = h[:, hh * HEAD:(hh + 1) * HEAD]
                kind = _KINDS[hd]
                if kind == "qnorm":
                    v = v * lax.rsqrt(_rowmean(v * v) + EPS) * qg_ref[...]
                elif kind == "knorm":
                    v = v * lax.rsqrt(_rowmean(v * v) + EPS) * kg_ref[...]
                if kind != "none":
                    v = _rope(v, cos, sa, sb)
                t_ref[:, hd * HEAD:(hd + 1) * HEAD] = v.astype(BF16)

    lat = lambda i: (jnp.minimum(i, nlat - 1), 0)
    row = lambda i: (i, 0)
    const2 = lambda i: (0, 0)
    return _pallas(
        body, name="qkv_fwd", grid=(nlat + 1,),
        out_shape=(jax.ShapeDtypeStruct((na, d), BF16), jax.ShapeDtypeStruct((na, wcols), F32),
                   jax.ShapeDtypeStruct((na, wcols), BF16)),
        in_specs=[pl.BlockSpec((tm, d), lat), pl.BlockSpec((tm, d), const2),
                  pl.BlockSpec((1, 1, d), lambda i: (i // nlat, 0, 0)),
                  pl.BlockSpec((1, 1, d), lambda i: (i // nlat, 0, 0)),
                  pl.BlockSpec((NDEV, d, IN_SHARD), lambda i: (0, 0, 0)),
                  pl.BlockSpec((1, HEAD), const2), pl.BlockSpec((1, HEAD), const2),
                  pl.BlockSpec((tm, HEAD), row), pl.BlockSpec((tm, HEAD), row), pl.BlockSpec((tm, HEAD), row)],
        out_specs=(pl.BlockSpec((tm, d), row), pl.BlockSpec((tm, wcols), row), pl.BlockSpec((tm, wcols), row)),
        compiler_params=_params(_mb(56)),
    )(x, ct, sc, sh, win_g, qg, kg, cos, sa, sb)


def _qkv_bwd_prep(dqa, dka, dva, dqb, dkb, dvb, h_all, qg, kg, cos, sa, sb):
    na, wcols = h_all.shape
    n = na - CTX
    tm = CTX
    nlat = n // tm

    def body(dqa_ref, dka_ref, dva_ref, dqb_ref, dkb_ref, dvb_ref, h_ref, qg_ref, kg_ref, cos_ref, sa_ref, sb_ref,
             dh_ref, dg_ref):
        i = pl.program_id(0)

        @pl.when(i == 0)
        def _():
            dg_ref[...] = jnp.zeros_like(dg_ref)

        cos, sa, sb = cos_ref[...], sa_ref[...], sb_ref[...]
        is_lat = i < nlat
        for hd in range(24):
            kind = _KINDS[hd]
            if hd < 8:
                dt = jnp.where(is_lat, dqa_ref[:, hd * HEAD:(hd + 1) * HEAD], 0.0)
            elif hd < 10:
                dt = dka_ref[:, (hd - 8) * HEAD:(hd - 7) * HEAD]
            elif hd < 12:
                dt = dva_ref[:, (hd - 10) * HEAD:(hd - 9) * HEAD]
            elif hd < 20:
                dt = jnp.where(is_lat, dqb_ref[:, (hd - 12) * HEAD:(hd - 11) * HEAD], 0.0)
            elif hd < 22:
                dt = dkb_ref[:, (hd - 20) * HEAD:(hd - 19) * HEAD]
            else:
                dt = dvb_ref[:, (hd - 22) * HEAD:(hd - 21) * HEAD]
            if kind != "none":
                dt = _rope_t(dt, cos, sa, sb)
            if kind in ("qnorm", "knorm"):
                g_ref = qg_ref if kind == "qnorm" else kg_ref
                r0 = 0 if kind == "qnorm" else 1
                xv = h_ref[:, hd * HEAD:(hd + 1) * HEAD]
                xn = xv * lax.rsqrt(_rowmean(xv * xv) + EPS)
                dg_ref[r0:r0 + 1, :] += _colsum(dt * xn)
                dxn = dt * g_ref[...]
                dt = lax.rsqrt(_rowmean(xv * xv) + EPS) * (dxn - xn * _rowmean(dxn * xn))
            dh_ref[:, hd * HEAD:(hd + 1) * HEAD] = dt.astype(BF16)

    lat = lambda i: (jnp.minimum(i, nlat - 1), 0)
    row = lambda i: (i, 0)
    const2 = lambda i: (0, 0)
    return _pallas(
        body, name="qkv_bwd_prep", grid=(nlat + 1,),
        out_shape=(jax.ShapeDtypeStruct((na, wcols), BF16), jax.ShapeDtypeStruct((8, HEAD), F32)),
        in_specs=[pl.BlockSpec((tm, 8 * HEAD), lat), pl.BlockSpec((tm, 2 * HEAD), row), pl.BlockSpec((tm, 2 * HEAD), row),
                  pl.BlockSpec((tm, 8 * HEAD), lat), pl.BlockSpec((tm, 2 * HEAD), row), pl.BlockSpec((tm, 2 * HEAD), row),
                  pl.BlockSpec((tm, wcols), row),
                  pl.BlockSpec((1, HEAD), const2), pl.BlockSpec((1, HEAD), const2),
                  pl.BlockSpec((tm, HEAD), row), pl.BlockSpec((tm, HEAD), row), pl.BlockSpec((tm, HEAD), row)],
        out_specs=(pl.BlockSpec((tm, wcols), row), pl.BlockSpec((8, HEAD), const2)),
        compiler_params=_params(_mb(40)),
    )(dqa, dka, dva, dqb, dkb, dvb, h_all, qg, kg, cos, sa, sb)


def _attn_keys(window, k_ref, v_ref, n, na, tq):
    i = pl.program_id(1)
    if not window:
        return k_ref[...], v_ref[...], None, None
    start = pl.multiple_of(jnp.clip((i - 1) * tq, 0, n - 3 * tq), tq)
    kk = jnp.concatenate([k_ref[pl.ds(start, 3 * tq), :], k_ref[n:na, :]], axis=0)
    vv = jnp.concatenate([v_ref[pl.ds(start, 3 * tq), :], v_ref[n:na, :]], axis=0)
    nk = 3 * tq + CTX
    col = lax.broadcasted_iota(jnp.int32, (tq, nk), 1)
    rowi = lax.broadcasted_iota(jnp.int32, (tq, nk), 0)
    valid = (jnp.abs(i * tq + rowi - (start + col)) <= WINDOW) | (col >= 3 * tq)
    return kk, vv, valid, start


def _attn_fwd(t_all, sink, window, qblk, kblk, vblk, oblk, tq, name):
    na = t_all.shape[0]
    n = na - CTX

    def body(sink_ref, q_ref, k_ref, v_ref, o_ref, lse_ref):
        kv = pl.program_id(0)
        kk, vv, valid, _ = _attn_keys(window, k_ref, v_ref, n, na, tq)
        for g in range(4):
            q = q_ref[:, g * HEAD:(g + 1) * HEAD]
            s = lax.dot_general(q, kk, _NT, preferred_element_type=F32) * SCALE
            if window:
                s = jnp.where(valid, s, NEG)
            m = jnp.max(s, axis=-1, keepdims=True)
            if window:
                sk = sink_ref[0, 4 * kv + g]
                m = jnp.maximum(m, sk)
            p = jnp.exp(s - m)
            l = jnp.sum(p, axis=-1, keepdims=True)
            if window:
                l = l + jnp.exp(sk - m)
            o = jnp.dot((p / l).astype(BF16), vv, preferred_element_type=F32)
            o_ref[:, g * HEAD:(g + 1) * HEAD] = o
            lse_ref[:, g * HEAD:(g + 1) * HEAD] = jnp.broadcast_to(m + jnp.log(l), (tq, HEAD))

    qspec = pl.BlockSpec((tq, 4 * HEAD), lambda kv, i: (i, qblk + kv))
    ospec = pl.BlockSpec((tq, 4 * HEAD), lambda kv, i: (i, kv))
    return _pallas(
        body, name=name, grid=(2, n // tq),
        out_shape=(jax.ShapeDtypeStruct((n, 8 * HEAD), F32), jax.ShapeDtypeStruct((n, 8 * HEAD), F32)),
        in_specs=[pl.BlockSpec(memory_space=pltpu.SMEM), qspec,
                  pl.BlockSpec((na, HEAD), lambda kv, i: (0, kblk + kv)),
                  pl.BlockSpec((na, HEAD), lambda kv, i: (0, vblk + kv))],
        out_specs=(ospec, ospec),
        compiler_params=_params(_mb(48)),
    )(sink, t_all, t_all, t_all)


def _attn_bwd(t_all, o, do, lse, sink, window, qblk, kblk, vblk, oblk, tq, name):
    na = t_all.shape[0]
    n = na - CTX

    def body(sink_ref, q_ref, k_ref, v_ref, o_ref, do_ref, lse_ref, dq_ref, dk_ref, dv_ref, dsink_ref):
        kv = pl.program_id(0)
        i = pl.program_id(1)

        @pl.when(i == 0)
        def _():
            dk_ref[...] = jnp.zeros_like(dk_ref)
            dv_ref[...] = jnp.zeros_like(dv_ref)
            dsink_ref[...] = jnp.zeros_like(dsink_ref)

        kk, vv, valid, start = _attn_keys(window, k_ref, v_ref, n, na, tq)
        dk_acc = jnp.zeros((kk.shape[0], HEAD), F32)
        dv_acc = jnp.zeros((kk.shape[0], HEAD), F32)
        for g in range(4):
            q = q_ref[:, g * HEAD:(g + 1) * HEAD]
            s = lax.dot_general(q, kk, _NT, preferred_element_type=F32) * SCALE
            if window:
                s = jnp.where(valid, s, NEG)
            lse_g = lse_ref[:, g * HEAD:g * HEAD + 1]
            p = jnp.exp(s - lse_g)
            dof = do_ref[:, g * HEAD:(g + 1) * HEAD]
            delta = jnp.sum(dof * o_ref[:, g * HEAD:(g + 1) * HEAD], axis=-1, keepdims=True)
            dob = dof.astype(BF16)
            dv_acc = dv_acc + lax.dot_general(p.astype(BF16), dob, _TN, preferred_element_type=F32)
            dp = lax.dot_general(dob, vv, _NT, preferred_element_type=F32)
            ds = (p * (dp - delta) * SCALE).astype(BF16)
            dq_ref[:, g * HEAD:(g + 1) * HEAD] = jnp.dot(ds, kk, preferred_element_type=F32)
            dk_acc = dk_acc + lax.dot_general(ds, q, _TN, preferred_element_type=F32)
            if window:
                p_sink = jnp.exp(sink_ref[0, 4 * kv + g] - lse_g)
                dsink_ref[0, g:g + 1, :] += jnp.broadcast_to(-jnp.sum(p_sink * delta, axis=0, keepdims=True), (1, HEAD))
        if window:
            dk_ref[pl.ds(start, 3 * tq), :] += dk_acc[:3 * tq]
            dv_ref[pl.ds(start, 3 * tq), :] += dv_acc[:3 * tq]
            dk_ref[n:na, :] += dk_acc[3 * tq:]
            dv_ref[n:na, :] += dv_acc[3 * tq:]
        else:
            dk_ref[...] += dk_acc
            dv_ref[...] += dv_acc

    qspec = pl.BlockSpec((tq, 4 * HEAD), lambda kv, i: (i, qblk + kv))
    ospec = pl.BlockSpec((tq, 4 * HEAD), lambda kv, i: (i, oblk + kv))
    lspec = pl.BlockSpec((tq, 4 * HEAD), lambda kv, i: (i, kv))
    kvout = pl.BlockSpec((na, HEAD), lambda kv, i: (0, kv))
    return _pallas(
        body, name=name, grid=(2, n // tq),
        out_shape=(jax.ShapeDtypeStruct((n, 8 * HEAD), F32), jax.ShapeDtypeStruct((na, 2 * HEAD), F32),
                   jax.ShapeDtypeStruct((na, 2 * HEAD), F32), jax.ShapeDtypeStruct((2, 8, HEAD), F32)),
        in_specs=[pl.BlockSpec(memory_space=pltpu.SMEM), qspec,
                  pl.BlockSpec((na, HEAD), lambda kv, i: (0, kblk + kv)),
                  pl.BlockSpec((na, HEAD), lambda kv, i: (0, vblk + kv)),
                  ospec, ospec, lspec],
        out_specs=(lspec, kvout, kvout, pl.BlockSpec((1, 8, HEAD), lambda kv, i: (kv, 0, 0))),
        compiler_params=_params(_mb(56)),
    )(sink, t_all, t_all, t_all, o, do, lse)


def _outproj_ln1(o, wout, x, g1):
    n, d = x.shape
    tm = 256

    def body(o_ref, w_ref, x_ref, g1_ref, a_ref, xh_ref, rs_ref):
        a1 = jnp.dot(o_ref[...].astype(BF16), w_ref[...], preferred_element_type=F32)
        a_ref[...] = a1
        r = ALPHA * x_ref[...] + g1_ref[...] * a1
        dlt = r - _rowmean(r)
        rstd = lax.rsqrt(_rowmean(dlt * dlt) + EPS)
        xh_ref[...] = dlt * rstd
        rs_ref[...] = rstd

    row = lambda i: (i, 0)
    const2 = lambda i: (0, 0)
    return _pallas(
        body, name="outproj_ln1", grid=(n // tm,),
        out_shape=(jax.ShapeDtypeStruct((n, d), F32), jax.ShapeDtypeStruct((n, d), F32),
                   jax.ShapeDtypeStruct((n, 1), F32)),
        in_specs=[pl.BlockSpec((tm, d), row), pl.BlockSpec((d, d), const2), pl.BlockSpec((tm, d), row),
                  pl.BlockSpec((1, d), const2)],
        out_specs=(pl.BlockSpec((tm, d), row), pl.BlockSpec((tm, d), row), pl.BlockSpec((tm, 1), row)),
        compiler_params=_params(_mb(56)),
    )(o, wout, x, g1)


def _ffn_up(xh1, lg, lb, sc2, sh2, wg_g, wu_g):
    n, d = xh1.shape
    tm = 512
    f = NDEV * FFN_PAD

    def body(xh_ref, lg_ref, lb_ref, sc_ref, sh_ref, wg_ref, wu_ref, u_ref, g_ref, p_ref, hf_ref):
        @pl.when(pl.program_id(1) == 0)
        def _():
            x1 = xh_ref[...] * lg_ref[...] + lb_ref[...]
            u_ref[...] = (x1 * (1.0 + sc_ref[...]) + sh_ref[...]).astype(BF16)

        u = u_ref[...]
        gv = jnp.dot(u, wg_ref[0], preferred_element_type=F32)
        pv = jnp.dot(u, wu_ref[0], preferred_element_type=F32)
        g_ref[...] = gv
        p_ref[...] = pv
        hf_ref[...] = (gv * _sigmoid(gv) * pv).astype(BF16)

    row = lambda i, j: (i, 0)
    const2 = lambda i, j: (0, 0)
    tile = lambda i, j: (i, j)
    wspec = pl.BlockSpec((1, d, FFN_PAD), lambda i, j: (j, 0, 0))
    vec = pl.BlockSpec((1, d), const2)
    return _pallas(
        body, name="ffn_up", grid=(n // tm, NDEV),
        out_shape=(jax.ShapeDtypeStruct((n, d), BF16), jax.ShapeDtypeStruct((n, f), F32),
                   jax.ShapeDtypeStruct((n, f), F32), jax.ShapeDtypeStruct((n, f), BF16)),
        in_specs=[pl.BlockSpec((tm, d), row), vec, vec, vec, vec, wspec, wspec],
        out_specs=(pl.BlockSpec((tm, d), row), pl.BlockSpec((tm, FFN_PAD), tile), pl.BlockSpec((tm, FFN_PAD), tile),
                   pl.BlockSpec((tm, FFN_PAD), tile)),
        compiler_params=_params(_mb(48)),
    )(xh1, lg, lb, sc2, sh2, wg_g, wu_g)


def _ffn_down(hf, wd):
    n, f = hf.shape
    d = wd.shape[1]
    tm, tn = 512, 512

    def body(h_ref, w_ref, o_ref):
        o_ref[...] = jnp.dot(h_ref[...], w_ref[...], preferred_element_type=F32)

    return _pallas(
        body, name="ffn_down", grid=(n // tm, d // tn),
        out_shape=jax.ShapeDtypeStruct((n, d), F32),
        in_specs=[pl.BlockSpec((tm, f), lambda i, j: (i, 0)), pl.BlockSpec((f, tn), lambda i, j: (0, j))],
        out_specs=pl.BlockSpec((tm, tn), lambda i, j: (i, j)),
        compiler_params=_params(_mb(48)),
    )(hf, wd)


def _ln2_loss(xh1, ffn, tgt, lg1, lb1, g2, lg2, lb2):
    n, d = xh1.shape
    tm = 256

    def body(xh_ref, f_ref, t_ref, lg1_ref, lb1_ref, g2_ref, lg2_ref, lb2_ref, dr_ref, loss_ref, acc_ref):
        @pl.when(pl.program_id(0) == 0)
        def _():
            loss_ref[...] = jnp.zeros_like(loss_ref)
            acc_ref[...] = jnp.zeros_like(acc_ref)

        x1 = xh_ref[...] * lg1_ref[...] + lb1_ref[...]
        fv = f_ref[...]
        r = ALPHA * x1 + g2_ref[...] * fv
        dlt = r - _rowmean(r)
        rstd = lax.rsqrt(_rowmean(dlt * dlt) + EPS)
        xh2 = dlt * rstd
        err = xh2 * lg2_ref[...] + lb2_ref[...] - t_ref[...]
        loss_ref[...] += 0.5 * jnp.sum(_rowmean(err * err))
        dy = err * (1.0 / d)
        dyg = dy * lg2_ref[...]
        dr = rstd * (dyg - _rowmean(dyg) - xh2 * _rowmean(dyg * xh2))
        dr_ref[...] = dr
        acc_ref[0:1, :] += _colsum(dy * xh2)
        acc_ref[1:2, :] += _colsum(dy)
        acc_ref[2:3, :] += _colsum(dr * fv)

    row = lambda i: (i, 0)
    const2 = lambda i: (0, 0)
    vec = pl.BlockSpec((1, d), const2)
    return _pallas(
        body, name="ln2_loss", grid=(n // tm,),
        out_shape=(jax.ShapeDtypeStruct((n, d), F32), jax.ShapeDtypeStruct((8, HEAD), F32),
                   jax.ShapeDtypeStruct((8, d), F32)),
        in_specs=[pl.BlockSpec((tm, d), row), pl.BlockSpec((tm, d), row), pl.BlockSpec((tm, d), row),
                  vec, vec, vec, vec, vec],
        out_specs=(pl.BlockSpec((tm, d), row), pl.BlockSpec((8, HEAD), const2), pl.BlockSpec((8, d), const2)),
        compiler_params=_params(_mb(48)),
    )(xh1, ffn, tgt, lg1, lb1, g2, lg2, lb2)


def _ffn_dhf(dr2, g2, wd_g, gmat, pmat):
    n, d = dr2.shape
    f = gmat.shape[1]
    tm = 512

    def body(dr_ref, g2_ref, w_ref, g_ref, p_ref, df_ref, dg_ref, dp_ref):
        @pl.when(pl.program_id(1) == 0)
        def _():
            df_ref[...] = (g2_ref[...] * dr_ref[...]).astype(BF16)

        dhf = lax.dot_general(df_ref[...], w_ref[0], _NT, preferred_element_type=F32)
        gv = g_ref[...]
        sg = _sigmoid(gv)
        dp_ref[...] = (dhf * (gv * sg)).astype(BF16)
        dg_ref[...] = (dhf * p_ref[...] * (sg * (1.0 + gv * (1.0 - sg)))).astype(BF16)

    row = lambda i, j: (i, 0)
    tile = lambda i, j: (i, j)
    return _pallas(
        body, name="ffn_dhf", grid=(n // tm, NDEV),
        out_shape=(jax.ShapeDtypeStruct((n, d), BF16), jax.ShapeDtypeStruct((n, f), BF16),
                   jax.ShapeDtypeStruct((n, f), BF16)),
        in_specs=[pl.BlockSpec((tm, d), row), pl.BlockSpec((1, d), lambda i, j: (0, 0)),
                  pl.BlockSpec((1, FFN_PAD, d), lambda i, j: (j, 0, 0)),
                  pl.BlockSpec((tm, FFN_PAD), tile), pl.BlockSpec((tm, FFN_PAD), tile)],
        out_specs=(pl.BlockSpec((tm, d), row), pl.BlockSpec((tm, FFN_PAD), tile), pl.BlockSpec((tm, FFN_PAD), tile)),
        compiler_params=_params(_mb(48)),
    )(dr2, g2, wd_g, gmat, pmat)


def _ffn_du2(dg, dp, wg_g, wu_g):
    n, f = dg.shape
    d = wg_g.shape[1]
    tm = 512

    def body(dg_ref, dp_ref, wg_ref, wu_ref, o_ref):
        part = (lax.dot_general(dg_ref[...], wg_ref[0], _NT, preferred_element_type=F32)
                + lax.dot_general(dp_ref[...], wu_ref[0], _NT, preferred_element_type=F32))

        @pl.when(pl.program_id(1) == 0)
        def _():
            o_ref[...] = part

        @pl.when(pl.program_id(1) > 0)
        def _():
            o_ref[...] += part

    tile = lambda i, j: (i, j)
    wspec = pl.BlockSpec((1, d, FFN_PAD), lambda i, j: (j, 0, 0))
    return _pallas(
        body, name="ffn_du2", grid=(n // tm, NDEV),
        out_shape=jax.ShapeDtypeStruct((n, d), F32),
        in_specs=[pl.BlockSpec((tm, FFN_PAD), tile), pl.BlockSpec((tm, FFN_PAD), tile), wspec, wspec],
        out_specs=pl.BlockSpec((tm, d), lambda i, j: (i, 0)),
        compiler_params=_params(_mb(48)),
    )(dg, dp, wg_g, wu_g)


def _ln1_bwd(du2, dr2, xh1, rs1, a1, lg1, lb1, sc2, g1):
    n, d = du2.shape
    tm = 256

    def body(du_ref, dr2_ref, xh_ref, rs_ref, a_ref, lg_ref, lb_ref, sc_ref, g1_ref, dr1_ref, da_ref, acc_ref):
        @pl.when(pl.program_id(0) == 0)
        def _():
            acc_ref[...] = jnp.zeros_like(acc_ref)

        du = du_ref[...]
        xh = xh_ref[...]
        x1 = xh * lg_ref[...] + lb_ref[...]
        dx1 = ALPHA * dr2_ref[...] + du * (1.0 + sc_ref[...])
        dxg = dx1 * lg_ref[...]
        dr1 = rs_ref[...] * (dxg - _rowmean(dxg) - xh * _rowmean(dxg * xh))
        dr1_ref[...] = dr1
        da_ref[...] = (g1_ref[...] * dr1).astype(BF16)
        acc_ref[0:1, :] += _colsum(du * x1)
        acc_ref[1:2, :] += _colsum(du)
        acc_ref[2:3, :] += _colsum(dx1 * xh)
        acc_ref[3:4, :] += _colsum(dx1)
        acc_ref[4:5, :] += _colsum(dr1 * a_ref[...])

    row = lambda i: (i, 0)
    const2 = lambda i: (0, 0)
    vec = pl.BlockSpec((1, d), const2)
    big = pl.BlockSpec((tm, d), row)
    return _pallas(
        body, name="ln1_bwd", grid=(n // tm,),
        out_shape=(jax.ShapeDtypeStruct((n, d), F32), jax.ShapeDtypeStruct((n, d), BF16),
                   jax.ShapeDtypeStruct((8, d), F32)),
        in_specs=[big, big, big, pl.BlockSpec((tm, 1), row), big, vec, vec, vec, vec],
        out_specs=(big, big, pl.BlockSpec((8, d), const2)),
        compiler_params=_params(_mb(48)),
    )(du2, dr2, xh1, rs1, a1, lg1, lb1, sc2, g1)


def _dw_cols(a, b, nblk, bw, tm, name):
    m, k = a.shape

    def body(a_ref, b_ref, o_ref, acc_ref):
        part = lax.dot_general(a_ref[...], b_ref[...], _TN, preferred_element_type=F32)
        i = pl.program_id(1)

        @pl.when(i == 0)
        def _():
            acc_ref[...] = part

        @pl.when(i > 0)
        def _():
            acc_ref[...] += part

        @pl.when(i == pl.num_programs(1) - 1)
        def _():
            o_ref[0] = acc_ref[...].astype(BF16)

    return _pallas(
        body, name=name, grid=(nblk, m // tm),
        out_shape=jax.ShapeDtypeStruct((nblk, k, bw), BF16),
        in_specs=[pl.BlockSpec((tm, k), lambda j, i: (i, 0)), pl.BlockSpec((tm, bw), lambda j, i: (i, j))],
        out_specs=pl.BlockSpec((1, k, bw), lambda j, i: (j, 0, 0)),
        scratch_shapes=[pltpu.VMEM((k, bw), F32)],
        compiler_params=_params(_mb(48)),
    )(a, b)


def _dw_rows(a, b, nblk, bw, tm, name):
    m = a.shape[0]
    nn = b.shape[1]

    def body(a_ref, b_ref, o_ref, acc_ref):
        part = lax.dot_general(a_ref[...].astype(BF16), b_ref[...], _TN, preferred_element_type=F32)
        i = pl.program_id(1)

        @pl.when(i == 0)
        def _():
            acc_ref[...] = part

        @pl.when(i > 0)
        def _():
            acc_ref[...] += part

        @pl.when(i == pl.num_programs(1) - 1)
        def _():
            o_ref[0] = acc_ref[...].astype(BF16)

    return _pallas(
        body, name=name, grid=(nblk, m // tm),
        out_shape=jax.ShapeDtypeStruct((nblk, bw, nn), BF16),
        in_specs=[pl.BlockSpec((tm, bw), lambda j, i: (i, j)), pl.BlockSpec((tm, nn), lambda j, i: (i, 0))],
        out_specs=pl.BlockSpec((1, bw, nn), lambda j, i: (j, 0, 0)),
        scratch_shapes=[pltpu.VMEM((bw, nn), F32)],
        compiler_params=_params(_mb(48)),
    )(a, b)


def _outproj_bwd(da1, wout):
    n, d = da1.shape
    tm = 512

    def body(a_ref, w_ref, o_ref):
        o_ref[...] = lax.dot_general(a_ref[...], w_ref[...], _NT, preferred_element_type=F32)

    return _pallas(
        body, name="outproj_bwd", grid=(n // tm,),
        out_shape=jax.ShapeDtypeStruct((n, d), F32),
        in_specs=[pl.BlockSpec((tm, d), lambda i: (i, 0)), pl.BlockSpec((d, d), lambda i: (0, 0))],
        out_specs=pl.BlockSpec((tm, d), lambda i: (i, 0)),
        compiler_params=_params(_mb(48)),
    )(da1, wout)


def _qkv_bwd(dh, win_g, x, ct, dr1, sc):
    na, wcols = dh.shape
    n, d = x.shape
    tm = CTX
    nlat = n // tm

    def body(dh_ref, w_ref, x_ref, ct_ref, dr_ref, sc_ref, gx_ref, acc_ref):
        i = pl.program_id(0)

        @pl.when(i == 0)
        def _():
            acc_ref[...] = jnp.zeros_like(acc_ref)

        du = jnp.zeros((tm, d), F32)
        for j in range(NDEV):
            du = du + lax.dot_general(dh_ref[:, j * IN_SHARD:(j + 1) * IN_SHARD], w_ref[j], _NT,
                                      preferred_element_type=F32)

        @pl.when(i < nlat)
        def _():
            gx_ref[...] = ALPHA * dr_ref[...] + du * (1.0 + sc_ref[0])
            acc_ref[0:1, :] += _colsum(du)
            acc_ref[1:2, :] += _colsum(du * x_ref[...])

        @pl.when(i == nlat)
        def _():
            acc_ref[2:3, :] += _colsum(du)
            acc_ref[3:4, :] += _colsum(du * ct_ref[...])

    lat = lambda i: (jnp.minimum(i, nlat - 1), 0)
    const2 = lambda i: (0, 0)
    return _pallas(
        body, name="qkv_bwd", grid=(nlat + 1,),
        out_shape=(jax.ShapeDtypeStruct((n, d), F32), jax.ShapeDtypeStruct((8, d), F32)),
        in_specs=[pl.BlockSpec((tm, wcols), lambda i: (i, 0)), pl.BlockSpec((NDEV, d, IN_SHARD), lambda i: (0, 0, 0)),
                  pl.BlockSpec((tm, d), lat), pl.BlockSpec((tm, d), const2), pl.BlockSpec((tm, d), lat),
                  pl.BlockSpec((1, 1, d), lambda i: (0, 0, 0))],
        out_specs=(pl.BlockSpec((tm, d), lat), pl.BlockSpec((8, d), const2)),
        compiler_params=_params(_mb(56)),
    )(dh, win_g, x, ct, dr1, sc)


def _adam_math(w, g, m, v):
    m2 = ADAM_B1 * m + (1.0 - ADAM_B1) * g
    v2 = ADAM_B2 * v + (1.0 - ADAM_B2) * (g * g)
    m_hat = m2 / (1.0 - ADAM_B1 ** ADAM_STEP)
    v_hat = v2 / (1.0 - ADAM_B2 ** ADAM_STEP)
    delta = -ADAM_LR * (m_hat / (jnp.sqrt(v_hat) + ADAM_EPS) + ADAM_WD * w)
    return delta, m2, v2


def _adamw(w, gsrc, m, v, name):
    r, c = w.shape
    parts = gsrc.ndim == 3
    tr = r
    while tr * c * 4 > _mb(1) and tr % 16 == 0:
        tr //= 2

    def body(w_ref, g_ref, m_ref, v_ref, go_ref, d_ref, mo_ref, vo_ref):
        if parts:
            g = g_ref[0].astype(F32)
            for s in range(1, NDEV):
                g = g + g_ref[s].astype(F32)
        else:
            g = g_ref[...]
        delta, m2, v2 = _adam_math(w_ref[...], g, m_ref[...], v_ref[...])
        go_ref[...] = g
        d_ref[...] = delta
        mo_ref[...] = m2
        vo_ref[...] = v2

    tile = pl.BlockSpec((tr, c), lambda i: (i, 0))
    gspec = pl.BlockSpec((NDEV, tr, c), lambda i: (0, i, 0)) if parts else tile
    sds = jax.ShapeDtypeStruct((r, c), F32)
    return _pallas(
        body, name=name, grid=(r // tr,),
        out_shape=(sds, sds, sds, sds),
        in_specs=[tile, gspec, tile, tile],
        out_specs=(tile, tile, tile, tile),
        compiler_params=_params(_mb(48)),
    )(w, gsrc, m, v)


def _small_update(gath, dcc, cc, w_s, m_s, v_s):
    d = w_s.shape[1]

    def body(g_ref, dcc_ref, cc_ref, w_ref, m_ref, v_ref, go_ref, d_ref, mo_ref, vo_ref):
        s = g_ref[0]
        for b in range(1, NDEV):
            s = s + g_ref[b]
        dsl = dcc_ref[0, 8:9, :]
        for b in range(1, NDEV):
            dsl = dsl + dcc_ref[b, 8:9, :]
        cv = cc_ref[...]
        sg = _sigmoid(cv)
        go_ref[...] = jnp.zeros_like(go_ref)
        go_ref[0:1, :] = dsl * (sg * (1.0 + cv * (1.0 - sg)))
        go_ref[1:3, :] = s[0:2] + s[6:8]
        go_ref[3:7, :] = s[2:6]
        go_ref[7:12, :] = s[8:13]
        delta, m2, v2 = _adam_math(w_ref[...], go_ref[...], m_ref[...], v_ref[...])
        d_ref[...] = delta
        mo_ref[...] = m2
        vo_ref[...] = v2

    full = pl.BlockSpec((16, d), lambda: (0, 0))
    g3 = pl.BlockSpec((NDEV, 16, d), lambda: (0, 0, 0))
    sds = jax.ShapeDtypeStruct((16, d), F32)
    return _pallas(
        body, name="small_update",
        out_shape=(sds, sds, sds, sds),
        in_specs=[g3, g3, pl.BlockSpec((1, d), lambda: (0, 0)), full, full, full],
        out_specs=(full, full, full, full),
        compiler_params=_params(_mb(24)),
    )(gath, dcc, cc, w_s, m_s, v_s)


def _rope_tables(n):
    rows = n // GRID_W
    row_ids = jnp.repeat(jnp.arange(rows, dtype=F32), GRID_W)
    col_ids = jnp.tile(jnp.arange(GRID_W, dtype=F32), rows)
    axis_dim = HEAD // 2
    inv_freq = jnp.power(ROPE_THETA, -jnp.arange(0, axis_dim, 2, dtype=F32) / axis_dim)
    ang_r = row_ids[:, None] * inv_freq
    ang_c = col_ids[:, None] * inv_freq
    ang = jnp.concatenate([ang_r, ang_r, ang_c, ang_c], axis=-1)
    cos, sin = jnp.cos(ang), jnp.sin(ang)
    first = (jnp.arange(HEAD) % (HEAD // 2)) < HEAD // 4
    sa = jnp.where(first, -sin, 0.0)
    sb = jnp.where(first, 0.0, sin)
    ones = jnp.ones((CTX, HEAD), F32)
    zeros = jnp.zeros((CTX, HEAD), F32)
    return (jnp.concatenate([cos, ones], 0), jnp.concatenate([sa, zeros], 0), jnp.concatenate([sb, zeros], 0))


def _pad_cols(a, width):
    return jnp.pad(a, ((0, 0), (0, width - a.shape[1])))


def _pad_rows(a, rows):
    return jnp.pad(a, ((0, rows - a.shape[0]), (0, 0)))


def _pack_small(c_ctx, b_ada, ln1_g, ln1_b, ln2_g, ln2_b, qg, kg, sink, d):
    misc = _pad_cols(jnp.concatenate([qg, kg, sink], axis=1), d)
    rows = jnp.concatenate([c_ctx.reshape(1, d), b_ada.reshape(6, d), ln1_g, ln1_b, ln2_g, ln2_b, misc], axis=0)
    return _pad_rows(rows, 16)


def _unpack_small(p, d):
    return dict(c_ctx=p[0], b_ada=p[1:7].reshape(1, 6 * d), ln1_g=p[7:8], ln1_b=p[8:9], ln2_g=p[9:10], ln2_b=p[10:11],
                q_norm_g=p[11:12, 0:HEAD], k_norm_g=p[11:12, HEAD:2 * HEAD], sink_logit=p[11:12, 2 * HEAD:2 * HEAD + 8])


def kernel(x, c, ctx, c_ctx, w_ada, b_ada, w_in, q_norm_g, k_norm_g, sink_logit, w_out, ln1_g, ln1_b, w_gate, w_up, w_down, ln2_g, ln2_b, loss_target, m_c_ctx, m_w_ada, m_b_ada, m_w_in, m_q_norm_g, m_k_norm_g, m_sink_logit, m_w_out, m_ln1_g, m_ln1_b, m_w_gate, m_w_up, m_w_down, m_ln2_g, m_ln2_b, v_c_ctx, v_w_ada, v_b_ada, v_w_in, v_q_norm_g, v_k_norm_g, v_sink_logit, v_w_out, v_ln1_g, v_ln1_b, v_w_gate, v_w_up, v_w_down, v_ln2_g, v_ln2_b):
    xs, cts, tgt = x[0], ctx[0], loss_target[0]
    n, d = xs.shape
    assert cts.shape == (CTX, d) and w_in.shape[2] == IN_SHARD and w_gate.shape[2] == FFN_SHARD
    me = 4 * lax.axis_index("x") + 2 * lax.axis_index("y") + lax.axis_index("c")
    e_sh = w_ada.shape[2]

    c_g = _exchange(_pad_rows(c, 8), False, "gather_c")
    c_all = jnp.concatenate([c_g[:, 0, :], _pad_rows(c_ctx.reshape(1, d), 8)], axis=0)
    bias_sh = lax.dynamic_slice(b_ada, (0, me * e_sh), (1, e_sh))
    mods_g = _exchange(_ada_fwd(c_all, w_ada[0], bias_sh), False, "gather_mods")
    mods = jnp.transpose(mods_g, (1, 0, 2)).reshape(16, NDEV * e_sh)
    mine = lax.dynamic_slice(mods, (me, 0), (1, 6 * d))
    sh1, sc1, g1, sh2, sc2, g2 = [mine[:, k * d:(k + 1) * d] for k in range(6)]
    csh1, csc1 = mods[8:9, 0:d], mods[8:9, d:2 * d]
    sc_pair = jnp.stack([sc1, csc1])
    sh_pair = jnp.stack([sh1, csh1])

    win_g = _exchange(w_in[0].astype(BF16), False, "gather_w_in")
    wout_g = _exchange(w_out[0].astype(BF16), False, "gather_w_out").reshape(d, d)
    wg_g = _exchange(_pad_cols(w_gate[0], FFN_PAD).astype(BF16), False, "gather_w_gate")
    wu_g = _exchange(_pad_cols(w_up[0], FFN_PAD).astype(BF16), False, "gather_w_up")
    wd_g = _exchange(_pad_rows(w_down[0], FFN_PAD).astype(BF16), False, "gather_w_down")

    cos, sa, sb = _rope_tables(n)
    u_all, h_all, t_all = _qkv_fwd(xs, cts, sc_pair, sh_pair, win_g, q_norm_g, k_norm_g, cos, sa, sb)
    o_a, lse_a = _attn_fwd(t_all, sink_logit, True, 0, 8, 10, 0, WINDOW, "attn_window_fwd")
    o_b, lse_b = _attn_fwd(t_all, sink_logit, False, 3, 20, 22, 2, 256, "attn_global_fwd")
    o = jnp.concatenate([o_a, o_b], axis=1)
    a1, xh1, rs1 = _outproj_ln1(o, wout_g, xs, g1)
    u2, gmat, pmat, hf = _ffn_up(xh1, ln1_g, ln1_b, sc2, sh2, wg_g, wu_g)
    ffn = _ffn_down(hf, wd_g.reshape(NDEV * FFN_PAD, d))
    dr2, loss_p, acc2 = _ln2_loss(xh1, ffn, tgt, ln1_g, ln1_b, g2, ln2_g, ln2_b)
    loss = lax.psum(loss_p[0, 0], ("x", "y", "c"))

    df, dgm, dpm = _ffn_dhf(dr2, g2, wd_g, gmat, pmat)
    du2 = _ffn_du2(dgm, dpm, wg_g, wu_g)
    dr1, da1, acc1 = _ln1_bwd(du2, dr2, xh1, rs1, a1, ln1_g, ln1_b, sc2, g1)
    dwg_p = _dw_cols(u2, dgm, NDEV, FFN_PAD, 512, "dw_gate")
    dwu_p = _dw_cols(u2, dpm, NDEV, FFN_PAD, 512, "dw_up")
    dwd_p = _dw_rows(hf, df, NDEV, FFN_PAD, 512, "dw_down")
    dwo_p = _dw_rows(o, da1, NDEV, 2 * HEAD, 512, "dw_out")
    do = _outproj_bwd(da1, wout_g)
    dqa, dka, dva, dsink = _attn_bwd(t_all, o, do, lse_a, sink_logit, True, 0, 8, 10, 0, WINDOW, "attn_window_bwd")
    dqb, dkb, dvb, _ = _attn_bwd(t_all, o, do, lse_b, sink_logit, False, 3, 20, 22, 2, 256, "attn_global_bwd")
    dh_all, dnorm = _qkv_bwd_prep(dqa, dka, dva, dqb, dkb, dvb, h_all, q_norm_g, k_norm_g, cos, sa, sb)
    dwi_p = _dw_cols(u_all, dh_all, NDEV, IN_SHARD, CTX, "dw_in")
    grad_x, acc0 = _qkv_bwd(dh_all, win_g, xs, cts, dr1, sc_pair)

    misc = _pad_cols(jnp.concatenate([dnorm[0:1], dnorm[1:2], dsink[:, 0:4, 0].reshape(1, 8)], axis=1), d)
    part = jnp.concatenate([
        acc0[0:2], acc1[4:5], acc1[1:2], acc1[0:1], acc2[2:3],
        acc0[2:4],
        acc1[2:4], acc2[0:2],
        misc, jnp.zeros((3, d), F32)], axis=0)
    gath = _exchange(part, False, "gather_small")
    dm_batch = gath[:, 0:6, :].reshape(NDEV, 6 * d)
    dm_ctx = _pad_cols(gath[:, 6:8, :].reshape(NDEV, 2 * d), 6 * d)
    dm16 = lax.dynamic_slice(jnp.concatenate([dm_batch, dm_ctx], axis=0), (0, me * e_sh), (16, e_sh))
    dw_ada, drow = _ada_bwd(dm16, c_all, w_ada[0])
    dcc = _exchange(drow, False, "gather_dcc")

    w_s = _pack_small(c_ctx, b_ada, ln1_g, ln1_b, ln2_g, ln2_b, q_norm_g, k_norm_g, sink_logit, d)
    m_s = _pack_small(m_c_ctx, m_b_ada, m_ln1_g, m_ln1_b, m_ln2_g, m_ln2_b, m_q_norm_g, m_k_norm_g, m_sink_logit, d)
    v_s = _pack_small(v_c_ctx, v_b_ada, v_ln1_g, v_ln1_b, v_ln2_g, v_ln2_b, v_q_norm_g, v_k_norm_g, v_sink_logit, d)
    small = [_unpack_small(p, d) for p in _small_update(gath, dcc, c_ctx.reshape(1, d), w_s, m_s, v_s)]

    big = {}
    big["w_ada"] = _adamw(w_ada[0], dw_ada, m_w_ada[0], v_w_ada[0], "adamw_w_ada")
    big["w_in"] = _adamw(w_in[0], _exchange(dwi_p, True, "scatter_dw_in"), m_w_in[0], v_w_in[0], "adamw_w_in")
    big["w_out"] = _adamw(w_out[0], _exchange(dwo_p, True, "scatter_dw_out"), m_w_out[0], v_w_out[0], "adamw_w_out")
    for nm, wt, mt, vt, dp in (("w_gate", w_gate, m_w_gate, v_w_gate, dwg_p), ("w_up", w_up, m_w_up, v_w_up, dwu_p)):
        res = _adamw(_pad_cols(wt[0], FFN_PAD), _exchange(dp, True, "scatter_d" + nm), _pad_cols(mt[0], FFN_PAD),
                     _pad_cols(vt[0], FFN_PAD), "adamw_" + nm)
        big[nm] = [r[:, :FFN_SHARD] for r in res]
    res = _adamw(_pad_rows(w_down[0], FFN_PAD), _exchange(dwd_p, True, "scatter_dw_down"),
                 _pad_rows(m_w_down[0], FFN_PAD), _pad_rows(v_w_down[0], FFN_PAD), "adamw_w_down")
    big["w_down"] = [r[:FFN_SHARD] for r in res]

    names = ["c_ctx", "w_ada", "b_ada", "w_in", "q_norm_g", "k_norm_g", "sink_logit", "w_out", "ln1_g", "ln1_b",
             "w_gate", "w_up", "w_down", "ln2_g", "ln2_b"]
    outs = [loss, grad_x[None]]
    for k in range(4):
        for nm in names:
            outs.append(big[nm][k][None] if nm in big else small[k][nm])
    return tuple(outs)
```

```python
import functools

import jax
import jax.numpy as jnp
from jax import lax
from jax.experimental import pallas as pl
from jax.experimental.pallas import tpu as pltpu

F32 = jnp.float32
BF16 = jnp.bfloat16

NDEV = 8
HEAD = 128
CTX = 256
GRID_W = 64
WINDOW = 128
ROPE_THETA = 10000.0
EPS = 1e-6
SCALE = HEAD ** -0.5
ALPHA = 2.0 ** 0.25
FFN_SHARD = 704
FFN_PAD = 768
IN_SHARD = 384
NEG = -1e30

ADAM_LR = 0.001
ADAM_B1 = 0.9
ADAM_B2 = 0.999
ADAM_EPS = 1e-08
ADAM_WD = 0.01
ADAM_STEP = 10

VMEM_CAP = 56 * 1024 * 1024

_KINDS = ["rope"] * 10 + ["none"] * 2 + ["qnorm"] * 8 + ["knorm"] * 2 + ["none"] * 2

_NT = (((1,), (1,)), ((), ()))
_TN = (((0,), (0,)), ((), ()))


def _pallas(body, **kw):
    return pl.pallas_call(body, **kw)


def _params(vmem_bytes):
    return pltpu.CompilerParams(vmem_limit_bytes=int(min(VMEM_CAP, vmem_bytes)))


def _mb(n):
    return int(n * 1024 * 1024)


def _sigmoid(x):
    return 1.0 / (1.0 + jnp.exp(-x))


def _colsum(a):
    return jnp.sum(a, axis=0, keepdims=True)


def _rowmean(a):
    return jnp.mean(a, axis=-1, keepdims=True)


def _exchange(src, scatter, name):
    blk = src.shape[1:] if scatter else src.shape

    def body(src_ref, out_ref, send_sems, recv_sems, local_sem):
        x, y, c = lax.axis_index("x"), lax.axis_index("y"), lax.axis_index("c")
        me = 4 * x + 2 * y + c
        copies = []
        for t in range(1, NDEV):
            px = 1 - x if (t >> 2) & 1 else x
            py = 1 - y if (t >> 1) & 1 else y
            pc = 1 - c if t & 1 else c
            peer = 4 * px + 2 * py + pc
            cp = pltpu.make_async_remote_copy(
                src_ref=src_ref.at[peer] if scatter else src_ref,
                dst_ref=out_ref.at[me],
                send_sem=send_sems.at[t - 1],
                recv_sem=recv_sems.at[t - 1],
                device_id=(px, py, pc),
                device_id_type=pl.DeviceIdType.MESH,
            )
            cp.start()
            copies.append(cp)
        own = pltpu.make_async_copy(src_ref.at[me] if scatter else src_ref, out_ref.at[me], local_sem)
        own.start()
        for cp in copies:
            cp.wait()
        own.wait()

    return _pallas(
        body, name=name,
        out_shape=jax.ShapeDtypeStruct((NDEV,) + tuple(blk), src.dtype),
        in_specs=[pl.BlockSpec(memory_space=pl.ANY)],
        out_specs=pl.BlockSpec(memory_space=pl.ANY),
        scratch_shapes=[pltpu.SemaphoreType.DMA((NDEV - 1,)), pltpu.SemaphoreType.DMA((NDEV - 1,)),
                        pltpu.SemaphoreType.DMA(())],
    )(src)


_HBM = pl.BlockSpec(memory_space=pltpu.HBM)
_SEM = pl.BlockSpec(memory_space=pltpu.SEMAPHORE)
_ANY = pl.BlockSpec(memory_space=pl.ANY)
_EFFECT = pltpu.SideEffectType.DATAFLOW_SIDE_EFFECTING


def _exchange_copies(src_ref, land_ref, send_sems, recv_sems, scatter):
    x, y, c = lax.axis_index("x"), lax.axis_index("y"), lax.axis_index("c")
    me = 4 * x + 2 * y + c
    copies = []
    for t in range(1, NDEV):
        px = 1 - x if (t >> 2) & 1 else x
        py = 1 - y if (t >> 1) & 1 else y
        pc = 1 - c if t & 1 else c
        peer = 4 * px + 2 * py + pc
        copies.append(pltpu.make_async_remote_copy(
            src_ref=src_ref.at[peer] if scatter else src_ref,
            dst_ref=land_ref.at[me],
            send_sem=send_sems.at[t - 1],
            recv_sem=recv_sems.at[t - 1],
            device_id=(px, py, pc),
            device_id_type=pl.DeviceIdType.MESH,
        ))
    own = pltpu.make_async_copy(src_ref.at[me] if scatter else src_ref, land_ref.at[me], send_sems.at[NDEV - 1])
    return copies, own


def _exchange_start(src, scatter, after, name):
    blk = src.shape[1:] if scatter else src.shape
    land = lax.empty((NDEV,) + tuple(blk), src.dtype)

    def body(src_ref, land_ref, after_ref, send_sems, recv_sems, src_thru, land_thru):
        copies, own = _exchange_copies(src_ref, land_ref, send_sems, recv_sems, scatter)
        for cp in copies:
            cp.start()
        own.start()

    return _pallas(
        body, name=name,
        out_shape=(pltpu.SemaphoreType.DMA((NDEV,)), pltpu.SemaphoreType.DMA((NDEV,)),
                   pltpu.HBM(src.shape, src.dtype), pltpu.HBM(land.shape, land.dtype)),
        in_specs=(_HBM, _HBM, _ANY), out_specs=(_SEM, _SEM, _HBM, _HBM),
        input_output_aliases={0: 2, 1: 3},
        compiler_params=pltpu.CompilerParams(has_side_effects=_EFFECT),
    )(pltpu.with_memory_space_constraint(src, pltpu.HBM), pltpu.with_memory_space_constraint(land, pltpu.HBM), after)


def _exchange_wait(handle, scatter, after, name):
    send_sems, recv_sems, src_thru, land_thru = handle

    def body(src_ref, land_ref, send_sems, recv_sems, after_ref, src_dead, got_ref):
        copies, own = _exchange_copies(src_ref, land_ref, send_sems, recv_sems, scatter)
        for cp in copies:
            cp.wait_send()
            cp.wait_recv()
        own.wait()

    return _pallas(
        body, name=name,
        out_shape=(pltpu.HBM(src_thru.shape, src_thru.dtype), pltpu.HBM(land_thru.shape, land_thru.dtype)),
        in_specs=(_HBM, _HBM, _SEM, _SEM, _ANY), out_specs=(_HBM, _HBM),
        input_output_aliases={0: 0, 1: 1},
        compiler_params=pltpu.CompilerParams(has_side_effects=_EFFECT),
    )(src_thru, land_thru, send_sems, recv_sems, after)[1]


def _ada_fwd(c_all, w, bias):
    r, d = c_all.shape
    e = w.shape[1]
    tn = 512

    def body(c_ref, w_ref, b_ref, o_ref):
        cv = c_ref[...]
        s = (cv * _sigmoid(cv)).astype(BF16)
        o_ref[...] = jnp.dot(s, w_ref[...].astype(BF16), preferred_element_type=F32) + b_ref[...]

    return _pallas(
        body, name="ada_fwd", grid=(e // tn,),
        out_shape=jax.ShapeDtypeStruct((r, e), F32),
        in_specs=[pl.BlockSpec((r, d), lambda j: (0, 0)), pl.BlockSpec((d, tn), lambda j: (0, j)),
                  pl.BlockSpec((1, tn), lambda j: (0, j))],
        out_specs=pl.BlockSpec((r, tn), lambda j: (0, j)),
        compiler_params=_params(_mb(24)),
    )(c_all, w, bias)


def _ada_bwd(dm16, c_all, w):
    d, e = w.shape
    tn = 512

    def body(dm_ref, c_ref, w_ref, dw_ref, dr_ref):
        j = pl.program_id(0)
        dm = dm_ref[...]
        rid = lax.broadcasted_iota(jnp.int32, dm.shape, 0)
        ctx_sum = jnp.sum(jnp.where(rid >= 8, dm, 0.0), axis=0, keepdims=True)
        rows = jnp.where(rid < 8, dm, jnp.where(rid == 8, jnp.broadcast_to(ctx_sum, dm.shape), 0.0)).astype(BF16)
        cv = c_ref[...]
        s = (cv * _sigmoid(cv)).astype(BF16)
        dw_ref[...] = lax.dot_general(s, rows, _TN, preferred_element_type=F32)
        part = lax.dot_general(rows, w_ref[...].astype(BF16), _NT, preferred_element_type=F32)

        @pl.when(j == 0)
        def _():
            dr_ref[...] = part

        @pl.when(j > 0)
        def _():
            dr_ref[...] += part

    return _pallas(
        body, name="ada_bwd", grid=(e // tn,),
        out_shape=(jax.ShapeDtypeStruct((d, e), F32), jax.ShapeDtypeStruct((16, d), F32)),
        in_specs=[pl.BlockSpec((16, tn), lambda j: (0, j)), pl.BlockSpec((16, d), lambda j: (0, 0)),
                  pl.BlockSpec((d, tn), lambda j: (0, j))],
        out_specs=(pl.BlockSpec((d, tn), lambda j: (0, j)), pl.BlockSpec((16, d), lambda j: (0, 0))),
        compiler_params=_params(_mb(32)),
    )(dm16, c_all, w)


def _rope(v, cos, sa, sb):
    return v * cos + (pltpu.roll(v, 96, 1) * sa + pltpu.roll(v, 32, 1) * sb)


def _rope_t(dt, cos, sa, sb):
    return dt * cos + (pltpu.roll(dt * sa, 32, 1) + pltpu.roll(dt * sb, 96, 1))


def _qkv_fwd(x, ct, sc, sh, win_g, qg, kg, cos, sa, sb):
    n, d = x.shape
    tm = CTX
    nlat = n // tm
    na = n + CTX
    wcols = NDEV * IN_SHARD

    def body(x_ref, ct_ref, sc_ref, sh_ref, w_ref, qg_ref, kg_ref, cos_ref, sa_ref, sb_ref, u_ref, h_ref, t_ref):
        i = pl.program_id(0)
        xin = jnp.where(i == nlat, ct_ref[...], x_ref[...])
        u = (xin * (1.0 + sc_ref[0]) + sh_ref[0]).astype(BF16)
        u_ref[...] = u
        cos, sa, sb = cos_ref[...], sa_ref[...], sb_ref[...]
        for j in range(NDEV):
            h = jnp.dot(u, w_ref[j], preferred_element_type=F32)
            h_ref[:, j * IN_SHARD:(j + 1) * IN_SHARD] = h
            for hh in range(3):
                hd = 3 * j + hh
                v = h[:, hh * HEAD:(hh + 1) * HEAD]
                kind = _KINDS[hd]
                if kind == "qnorm":
                    v = v * lax.rsqrt(_rowmean(v * v) + EPS) * qg_ref[...]
                elif kind == "knorm":
                    v = v * lax.rsqrt(_rowmean(v * v) + EPS) * kg_ref[...]
                if kind != "none":
                    v = _rope(v, cos, sa, sb)
                t_ref[:, hd * HEAD:(hd + 1) * HEAD] = v.astype(BF16)

    lat = lambda i: (jnp.minimum(i, nlat - 1), 0)
    row = lambda i: (i, 0)
    const2 = lambda i: (0, 0)
    return _pallas(
        body, name="qkv_fwd", grid=(nlat + 1,),
        out_shape=(jax.ShapeDtypeStruct((na, d), BF16), jax.ShapeDtypeStruct((na, wcols), F32),
                   jax.ShapeDtypeStruct((na, wcols), BF16)),
        in_specs=[pl.BlockSpec((tm, d), lat), pl.BlockSpec((tm, d), const2),
                  pl.BlockSpec((1, 1, d), lambda i: (i // nlat, 0, 0)),
                  pl.BlockSpec((1, 1, d), lambda i: (i // nlat, 0, 0)),
                  pl.BlockSpec((NDEV, d, IN_SHARD), lambda i: (0, 0, 0)),
                  pl.BlockSpec((1, HEAD), const2), pl.BlockSpec((1, HEAD), const2),
                  pl.BlockSpec((tm, HEAD), row), pl.BlockSpec((tm, HEAD), row), pl.BlockSpec((tm, HEAD), row)],
        out_specs=(pl.BlockSpec((tm, d), row), pl.BlockSpec((tm, wcols), row), pl.BlockSpec((tm, wcols), row)),
        compiler_params=_params(_mb(56)),
    )(x, ct, sc, sh, win_g, qg, kg, cos, sa, sb)


def _qkv_bwd_prep(dqa, dka, dva, dqb, dkb, dvb, h_all, qg, kg, cos, sa, sb):
    na, wcols = h_all.shape
    n = na - CTX
    tm = CTX
    nlat = n // tm

    def body(dqa_ref, dka_ref, dva_ref, dqb_ref, dkb_ref, dvb_ref, h_ref, qg_ref, kg_ref, cos_ref, sa_ref, sb_ref,
             dh_ref, dg_ref):
        i = pl.program_id(0)

        @pl.when(i == 0)
        def _():
            dg_ref[...] = jnp.zeros_like(dg_ref)

        cos, sa, sb = cos_ref[...], sa_ref[...], sb_ref[...]
        is_lat = i < nlat
        for hd in range(24):
            kind = _KINDS[hd]
            if hd < 8:
                dt = jnp.where(is_lat, dqa_ref[:, hd * HEAD:(hd + 1) * HEAD], 0.0)
            elif hd < 10:
                dt = dka_ref[:, (hd - 8) * HEAD:(hd - 7) * HEAD]
            elif hd < 12:
                dt = dva_ref[:, (hd - 10) * HEAD:(hd - 9) * HEAD]
            elif hd < 20:
                dt = jnp.where(is_lat, dqb_ref[:, (hd - 12) * HEAD:(hd - 11) * HEAD], 0.0)
            elif hd < 22:
                dt = dkb_ref[:, (hd - 20) * HEAD:(hd - 19) * HEAD]
            else:
                dt = dvb_ref[:, (hd - 22) * HEAD:(hd - 21) * HEAD]
            if kind != "none":
                dt = _rope_t(dt, cos, sa, sb)
            if kind in ("qnorm", "knorm"):
                g_ref = qg_ref if kind == "qnorm" else kg_ref
                r0 = 0 if kind == "qnorm" else 1
                xv = h_ref[:, hd * HEAD:(hd + 1) * HEAD]
                xn = xv * lax.rsqrt(_rowmean(xv * xv) + EPS)
                dg_ref[r0:r0 + 1, :] += _colsum(dt * xn)
                dxn = dt * g_ref[...]
                dt = lax.rsqrt(_rowmean(xv * xv) + EPS) * (dxn - xn * _rowmean(dxn * xn))
            dh_ref[:, hd * HEAD:(hd + 1) * HEAD] = dt.astype(BF16)

    lat = lambda i: (jnp.minimum(i, nlat - 1), 0)
    row = lambda i: (i, 0)
    const2 = lambda i: (0, 0)
    return _pallas(
        body, name="qkv_bwd_prep", grid=(nlat + 1,),
        out_shape=(jax.ShapeDtypeStruct((na, wcols), BF16), jax.ShapeDtypeStruct((8, HEAD), F32)),
        in_specs=[pl.BlockSpec((tm, 8 * HEAD), lat), pl.BlockSpec((tm, 2 * HEAD), row), pl.BlockSpec((tm, 2 * HEAD), row),
                  pl.BlockSpec((tm, 8 * HEAD), lat), pl.BlockSpec((tm, 2 * HEAD), row), pl.BlockSpec((tm, 2 * HEAD), row),
                  pl.BlockSpec((tm, wcols), row),
                  pl.BlockSpec((1, HEAD), const2), pl.BlockSpec((1, HEAD), const2),
                  pl.BlockSpec((tm, HEAD), row), pl.BlockSpec((tm, HEAD), row), pl.BlockSpec((tm, HEAD), row)],
        out_specs=(pl.BlockSpec((tm, wcols), row), pl.BlockSpec((8, HEAD), const2)),
        compiler_params=_params(_mb(40)),
    )(dqa, dka, dva, dqb, dkb, dvb, h_all, qg, kg, cos, sa, sb)


def _attn_keys(window, k_ref, v_ref, n, na, tq):
    i = pl.program_id(1)
    if not window:
        return k_ref[...], v_ref[...], None, None
    start = pl.multiple_of(jnp.clip((i - 1) * tq, 0, n - 3 * tq), tq)
    kk = jnp.concatenate([k_ref[pl.ds(start, 3 * tq), :], k_ref[n:na, :]], axis=0)
    vv = jnp.concatenate([v_ref[pl.ds(start, 3 * tq), :], v_ref[n:na, :]], axis=0)
    nk = 3 * tq + CTX
    col = lax.broadcasted_iota(jnp.int32, (tq, nk), 1)
    rowi = lax.broadcasted_iota(jnp.int32, (tq, nk), 0)
    valid = (jnp.abs(i * tq + rowi - (start + col)) <= WINDOW) | (col >= 3 * tq)
    return kk, vv, valid, start


def _attn_fwd(t_all, sink, window, qblk, kblk, vblk, oblk, tq, name):
    na = t_all.shape[0]
    n = na - CTX

    def body(sink_ref, q_ref, k_ref, v_ref, o_ref, lse_ref):
        kv = pl.program_id(0)
        kk, vv, valid, _ = _attn_keys(window, k_ref, v_ref, n, na, tq)
        for g in range(4):
            q = q_ref[:, g * HEAD:(g + 1) * HEAD]
            s = lax.dot_general(q, kk, _NT, preferred_element_type=F32) * SCALE
            if window:
                s = jnp.where(valid, s, NEG)
            m = jnp.max(s, axis=-1, keepdims=True)
            if window:
                sk = sink_ref[0, 4 * kv + g]
                m = jnp.maximum(m, sk)
            p = jnp.exp(s - m)
            l = jnp.sum(p, axis=-1, keepdims=True)
            if window:
                l = l + jnp.exp(sk - m)
            o = jnp.dot((p / l).astype(BF16), vv, preferred_element_type=F32)
            o_ref[:, g * HEAD:(g + 1) * HEAD] = o
            lse_ref[:, g * HEAD:(g + 1) * HEAD] = jnp.broadcast_to(m + jnp.log(l), (tq, HEAD))

    qspec = pl.BlockSpec((tq, 4 * HEAD), lambda kv, i: (i, qblk + kv))
    ospec = pl.BlockSpec((tq, 4 * HEAD), lambda kv, i: (i, kv))
    return _pallas(
        body, name=name, grid=(2, n // tq),
        out_shape=(jax.ShapeDtypeStruct((n, 8 * HEAD), F32), jax.ShapeDtypeStruct((n, 8 * HEAD), F32)),
        in_specs=[pl.BlockSpec(memory_space=pltpu.SMEM), qspec,
                  pl.BlockSpec((na, HEAD), lambda kv, i: (0, kblk + kv)),
                  pl.BlockSpec((na, HEAD), lambda kv, i: (0, vblk + kv))],
        out_specs=(ospec, ospec),
        compiler_params=_params(_mb(48)),
    )(sink, t_all, t_all, t_all)


def _attn_bwd(t_all, o, do, lse, sink, window, qblk, kblk, vblk, oblk, tq, name):
    na = t_all.shape[0]
    n = na - CTX

    def body(sink_ref, q_ref, k_ref, v_ref, o_ref, do_ref, lse_ref, dq_ref, dk_ref, dv_ref, dsink_ref):
        kv = pl.program_id(0)
        i = pl.program_id(1)

        @pl.when(i == 0)
        def _():
            dk_ref[...] = jnp.zeros_like(dk_ref)
            dv_ref[...] = jnp.zeros_like(dv_ref)
            dsink_ref[...] = jnp.zeros_like(dsink_ref)

        kk, vv, valid, start = _attn_keys(window, k_ref, v_ref, n, na, tq)
        dk_acc = jnp.zeros((kk.shape[0], HEAD), F32)
        dv_acc = jnp.zeros((kk.shape[0], HEAD), F32)
        for g in range(4):
            q = q_ref[:, g * HEAD:(g + 1) * HEAD]
            s = lax.dot_general(q, kk, _NT, preferred_element_type=F32) * SCALE
            if window:
                s = jnp.where(valid, s, NEG)
            lse_g = lse_ref[:, g * HEAD:g * HEAD + 1]
            p = jnp.exp(s - lse_g)
            dof = do_ref[:, g * HEAD:(g + 1) * HEAD]
            delta = jnp.sum(dof * o_ref[:, g * HEAD:(g + 1) * HEAD], axis=-1, keepdims=True)
            dob = dof.astype(BF16)
            dv_acc = dv_acc + lax.dot_general(p.astype(BF16), dob, _TN, preferred_element_type=F32)
            dp = lax.dot_general(dob, vv, _NT, preferred_element_type=F32)
            ds = (p * (dp - delta) * SCALE).astype(BF16)
            dq_ref[:, g * HEAD:(g + 1) * HEAD] = jnp.dot(ds, kk, preferred_element_type=F32)
            dk_acc = dk_acc + lax.dot_general(ds, q, _TN, preferred_element_type=F32)
            if window:
                p_sink = jnp.exp(sink_ref[0, 4 * kv + g] - lse_g)
                dsink_ref[0, g:g + 1, :] += jnp.broadcast_to(-jnp.sum(p_sink * delta, axis=0, keepdims=True), (1, HEAD))
        if window:
            dk_ref[pl.ds(start, 3 * tq), :] += dk_acc[:3 * tq]
            dv_ref[pl.ds(start, 3 * tq), :] += dv_acc[:3 * tq]
            dk_ref[n:na, :] += dk_acc[3 * tq:]
            dv_ref[n:na, :] += dv_acc[3 * tq:]
        else:
            dk_ref[...] += dk_acc
            dv_ref[...] += dv_acc

    qspec = pl.BlockSpec((tq, 4 * HEAD), lambda kv, i: (i, qblk + kv))
    ospec = pl.BlockSpec((tq, 4 * HEAD), lambda kv, i: (i, oblk + kv))
    lspec = pl.BlockSpec((tq, 4 * HEAD), lambda kv, i: (i, kv))
    kvout = pl.BlockSpec((na, HEAD), lambda kv, i: (0, kv))
    return _pallas(
        body, name=name, grid=(2, n // tq),
        out_shape=(jax.ShapeDtypeStruct((n, 8 * HEAD), F32), jax.ShapeDtypeStruct((na, 2 * HEAD), F32),
                   jax.ShapeDtypeStruct((na, 2 * HEAD), F32), jax.ShapeDtypeStruct((2, 8, HEAD), F32)),
        in_specs=[pl.BlockSpec(memory_space=pltpu.SMEM), qspec,
                  pl.BlockSpec((na, HEAD), lambda kv, i: (0, kblk + kv)),
                  pl.BlockSpec((na, HEAD), lambda kv, i: (0, vblk + kv)),
                  ospec, ospec, lspec],
        out_specs=(lspec, kvout, kvout, pl.BlockSpec((1, 8, HEAD), lambda kv, i: (kv, 0, 0))),
        compiler_params=_params(_mb(56)),
    )(sink, t_all, t_all, t_all, o, do, lse)


def _outproj_ln1(o, wout, x, g1):
    n, d = x.shape
    tm = 256

    def body(o_ref, w_ref, x_ref, g1_ref, a_ref, xh_ref, rs_ref):
        a1 = jnp.dot(o_ref[...].astype(BF16), w_ref[...], preferred_element_type=F32)
        a_ref[...] = a1
        r = ALPHA * x_ref[...] + g1_ref[...] * a1
        dlt = r - _rowmean(r)
        rstd = lax.rsqrt(_rowmean(dlt * dlt) + EPS)
        xh_ref[...] = dlt * rstd
        rs_ref[...] = rstd

    row = lambda i: (i, 0)
    const2 = lambda i: (0, 0)
    return _pallas(
        body, name="outproj_ln1", grid=(n // tm,),
        out_shape=(jax.ShapeDtypeStruct((n, d), F32), jax.ShapeDtypeStruct((n, d), F32),
                   jax.ShapeDtypeStruct((n, 1), F32)),
        in_specs=[pl.BlockSpec((tm, d), row), pl.BlockSpec((d, d), const2), pl.BlockSpec((tm, d), row),
                  pl.BlockSpec((1, d), const2)],
        out_specs=(pl.BlockSpec((tm, d), row), pl.BlockSpec((tm, d), row), pl.BlockSpec((tm, 1), row)),
        compiler_params=_params(_mb(56)),
    )(o, wout, x, g1)


def _ffn_up(xh1, lg, lb, sc2, sh2, wg_g, wu_g):
    n, d = xh1.shape
    tm = 512
    f = NDEV * FFN_PAD

    def body(xh_ref, lg_ref, lb_ref, sc_ref, sh_ref, wg_ref, wu_ref, u_ref, g_ref, p_ref, hf_ref):
        @pl.when(pl.program_id(1) == 0)
        def _():
            x1 = xh_ref[...] * lg_ref[...] + lb_ref[...]
            u_ref[...] = (x1 * (1.0 + sc_ref[...]) + sh_ref[...]).astype(BF16)

        u = u_ref[...]
        gv = jnp.dot(u, wg_ref[0], preferred_element_type=F32)
        pv = jnp.dot(u, wu_ref[0], preferred_element_type=F32)
        g_ref[...] = gv
        p_ref[...] = pv
        hf_ref[...] = (gv * _sigmoid(gv) * pv).astype(BF16)

    row = lambda i, j: (i, 0)
    const2 = lambda i, j: (0, 0)
    tile = lambda i, j: (i, j)
    wspec = pl.BlockSpec((1, d, FFN_PAD), lambda i, j: (j, 0, 0))
    vec = pl.BlockSpec((1, d), const2)
    return _pallas(
        body, name="ffn_up", grid=(n // tm, NDEV),
        out_shape=(jax.ShapeDtypeStruct((n, d), BF16), jax.ShapeDtypeStruct((n, f), F32),
                   jax.ShapeDtypeStruct((n, f), F32), jax.ShapeDtypeStruct((n, f), BF16)),
        in_specs=[pl.BlockSpec((tm, d), row), vec, vec, vec, vec, wspec, wspec],
        out_specs=(pl.BlockSpec((tm, d), row), pl.BlockSpec((tm, FFN_PAD), tile), pl.BlockSpec((tm, FFN_PAD), tile),
                   pl.BlockSpec((tm, FFN_PAD), tile)),
        compiler_params=_params(_mb(48)),
    )(xh1, lg, lb, sc2, sh2, wg_g, wu_g)


def _ffn_down(hf, wd):
    n, f = hf.shape
    d = wd.shape[1]
    tm, tn = 512, 512

    def body(h_ref, w_ref, o_ref):
        o_ref[...] = jnp.dot(h_ref[...], w_ref[...], preferred_element_type=F32)

    return _pallas(
        body, name="ffn_down", grid=(n // tm, d // tn),
        out_shape=jax.ShapeDtypeStruct((n, d), F32),
        in_specs=[pl.BlockSpec((tm, f), lambda i, j: (i, 0)), pl.BlockSpec((f, tn), lambda i, j: (0, j))],
        out_specs=pl.BlockSpec((tm, tn), lambda i, j: (i, j)),
        compiler_params=_params(_mb(48)),
    )(hf, wd)


def _ln2_loss(xh1, ffn, tgt, lg1, lb1, g2, lg2, lb2):
    n, d = xh1.shape
    tm = 256

    def body(xh_ref, f_ref, t_ref, lg1_ref, lb1_ref, g2_ref, lg2_ref, lb2_ref, dr_ref, loss_ref, acc_ref):
        @pl.when(pl.program_id(0) == 0)
        def _():
            loss_ref[...] = jnp.zeros_like(loss_ref)
            acc_ref[...] = jnp.zeros_like(acc_ref)

        x1 = xh_ref[...] * lg1_ref[...] + lb1_ref[...]
        fv = f_ref[...]
        r = ALPHA * x1 + g2_ref[...] * fv
        dlt = r - _rowmean(r)
        rstd = lax.rsqrt(_rowmean(dlt * dlt) + EPS)
        xh2 = dlt * rstd
        err = xh2 * lg2_ref[...] + lb2_ref[...] - t_ref[...]
        loss_ref[...] += 0.5 * jnp.sum(_rowmean(err * err))
        dy = err * (1.0 / d)
        dyg = dy * lg2_ref[...]
        dr = rstd * (dyg - _rowmean(dyg) - xh2 * _rowmean(dyg * xh2))
        dr_ref[...] = dr
        acc_ref[0:1, :] += _colsum(dy * xh2)
        acc_ref[1:2, :] += _colsum(dy)
        acc_ref[2:3, :] += _colsum(dr * fv)

    row = lambda i: (i, 0)
    const2 = lambda i: (0, 0)
    vec = pl.BlockSpec((1, d), const2)
    return _pallas(
        body, name="ln2_loss", grid=(n // tm,),
        out_shape=(jax.ShapeDtypeStruct((n, d), F32), jax.ShapeDtypeStruct((8, HEAD), F32),
                   jax.ShapeDtypeStruct((8, d), F32)),
        in_specs=[pl.BlockSpec((tm, d), row), pl.BlockSpec((tm, d), row), pl.BlockSpec((tm, d), row),
                  vec, vec, vec, vec, vec],
        out_specs=(pl.BlockSpec((tm, d), row), pl.BlockSpec((8, HEAD), const2), pl.BlockSpec((8, d), const2)),
        compiler_params=_params(_mb(48)),
    )(xh1, ffn, tgt, lg1, lb1, g2, lg2, lb2)


def _ffn_dhf(dr2, g2, wd_g, gmat, pmat):
    n, d = dr2.shape
    f = gmat.shape[1]
    tm = 512

    def body(dr_ref, g2_ref, w_ref, g_ref, p_ref, df_ref, dg_ref, dp_ref):
        @pl.when(pl.program_id(1) == 0)
        def _():
            df_ref[...] = (g2_ref[...] * dr_ref[...]).astype(BF16)

        dhf = lax.dot_general(df_ref[...], w_ref[0], _NT, preferred_element_type=F32)
        gv = g_ref[...]
        sg = _sigmoid(gv)
        dp_ref[...] = (dhf * (gv * sg)).astype(BF16)
        dg_ref[...] = (dhf * p_ref[...] * (sg * (1.0 + gv * (1.0 - sg)))).astype(BF16)

    row = lambda i, j: (i, 0)
    tile = lambda i, j: (i, j)
    return _pallas(
        body, name="ffn_dhf", grid=(n // tm, NDEV),
        out_shape=(jax.ShapeDtypeStruct((n, d), BF16), jax.ShapeDtypeStruct((n, f), BF16),
                   jax.ShapeDtypeStruct((n, f), BF16)),
        in_specs=[pl.BlockSpec((tm, d), row), pl.BlockSpec((1, d), lambda i, j: (0, 0)),
                  pl.BlockSpec((1, FFN_PAD, d), lambda i, j: (j, 0, 0)),
                  pl.BlockSpec((tm, FFN_PAD), tile), pl.BlockSpec((tm, FFN_PAD), tile)],
        out_specs=(pl.BlockSpec((tm, d), row), pl.BlockSpec((tm, FFN_PAD), tile), pl.BlockSpec((tm, FFN_PAD), tile)),
        compiler_params=_params(_mb(48)),
    )(dr2, g2, wd_g, gmat, pmat)


def _ffn_du2(dg, dp, wg_g, wu_g):
    n, f = dg.shape
    d = wg_g.shape[1]
    tm = 512

    def body(dg_ref, dp_ref, wg_ref, wu_ref, o_ref):
        part = (lax.dot_general(dg_ref[...], wg_ref[0], _NT, preferred_element_type=F32)
                + lax.dot_general(dp_ref[...], wu_ref[0], _NT, preferred_element_type=F32))

        @pl.when(pl.program_id(1) == 0)
        def _():
            o_ref[...] = part

        @pl.when(pl.program_id(1) > 0)
        def _():
            o_ref[...] += part

    tile = lambda i, j: (i, j)
    wspec = pl.BlockSpec((1, d, FFN_PAD), lambda i, j: (j, 0, 0))
    return _pallas(
        body, name="ffn_du2", grid=(n // tm, NDEV),
        out_shape=jax.ShapeDtypeStruct((n, d), F32),
        in_specs=[pl.BlockSpec((tm, FFN_PAD), tile), pl.BlockSpec((tm, FFN_PAD), tile), wspec, wspec],
        out_specs=pl.BlockSpec((tm, d), lambda i, j: (i, 0)),
        compiler_params=_params(_mb(48)),
    )(dg, dp, wg_g, wu_g)


def _ln1_bwd(du2, dr2, xh1, rs1, a1, lg1, lb1, sc2, g1):
    n, d = du2.shape
    tm = 256

    def body(du_ref, dr2_ref, xh_ref, rs_ref, a_ref, lg_ref, lb_ref, sc_ref, g1_ref, dr1_ref, da_ref, acc_ref):
        @pl.when(pl.program_id(0) == 0)
        def _():
            acc_ref[...] = jnp.zeros_like(acc_ref)

        du = du_ref[...]
        xh = xh_ref[...]
        x1 = xh * lg_ref[...] + lb_ref[...]
        dx1 = ALPHA * dr2_ref[...] + du * (1.0 + sc_ref[...])
        dxg = dx1 * lg_ref[...]
        dr1 = rs_ref[...] * (dxg - _rowmean(dxg) - xh * _rowmean(dxg * xh))
        dr1_ref[...] = dr1
        da_ref[...] = (g1_ref[...] * dr1).astype(BF16)
        acc_ref[0:1, :] += _colsum(du * x1)
        acc_ref[1:2, :] += _colsum(du)
        acc_ref[2:3, :] += _colsum(dx1 * xh)
        acc_ref[3:4, :] += _colsum(dx1)
        acc_ref[4:5, :] += _colsum(dr1 * a_ref[...])

    row = lambda i: (i, 0)
    const2 = lambda i: (0, 0)
    vec = pl.BlockSpec((1, d), const2)
    big = pl.BlockSpec((tm, d), row)
    return _pallas(
        body, name="ln1_bwd", grid=(n // tm,),
        out_shape=(jax.ShapeDtypeStruct((n, d), F32), jax.ShapeDtypeStruct((n, d), BF16),
                   jax.ShapeDtypeStruct((8, d), F32)),
        in_specs=[big, big, big, pl.BlockSpec((tm, 1), row), big, vec, vec, vec, vec],
        out_specs=(big, big, pl.BlockSpec((8, d), const2)),
        compiler_params=_params(_mb(48)),
    )(du2, dr2, xh1, rs1, a1, lg1, lb1, sc2, g1)


def _dw_cols(a, b, nblk, bw, tm, name):
    m, k = a.shape

    def body(a_ref, b_ref, o_ref, acc_ref):
        part = lax.dot_general(a_ref[...], b_ref[...], _TN, preferred_element_type=F32)
        i = pl.program_id(1)

        @pl.when(i == 0)
        def _():
            acc_ref[...] = part

        @pl.when(i > 0)
        def _():
            acc_ref[...] += part

        @pl.when(i == pl.num_programs(1) - 1)
        def _():
            o_ref[0] = acc_ref[...].astype(BF16)

    return _pallas(
        body, name=name, grid=(nblk, m // tm),
        out_shape=jax.ShapeDtypeStruct((nblk, k, bw), BF16),
        in_specs=[pl.BlockSpec((tm, k), lambda j, i: (i, 0)), pl.BlockSpec((tm, bw), lambda j, i: (i, j))],
        out_specs=pl.BlockSpec((1, k, bw), lambda j, i: (j, 0, 0)),
        scratch_shapes=[pltpu.VMEM((k, bw), F32)],
        compiler_params=_params(_mb(48)),
    )(a, b)


def _dw_rows(a, b, nblk, bw, tm, name):
    m = a.shape[0]
    nn = b.shape[1]

    def body(a_ref, b_ref, o_ref, acc_ref):
        part = lax.dot_general(a_ref[...].astype(BF16), b_ref[...], _TN, preferred_element_type=F32)
        i = pl.program_id(1)

        @pl.when(i == 0)
        def _():
            acc_ref[...] = part

        @pl.when(i > 0)
        def _():
            acc_ref[...] += part

        @pl.when(i == pl.num_programs(1) - 1)
        def _():
            o_ref[0] = acc_ref[...].astype(BF16)

    return _pallas(
        body, name=name, grid=(nblk, m // tm),
        out_shape=jax.ShapeDtypeStruct((nblk, bw, nn), BF16),
        in_specs=[pl.BlockSpec((tm, bw), lambda j, i: (i, j)), pl.BlockSpec((tm, nn), lambda j, i: (i, 0))],
        out_specs=pl.BlockSpec((1, bw, nn), lambda j, i: (j, 0, 0)),
        scratch_shapes=[pltpu.VMEM((bw, nn), F32)],
        compiler_params=_params(_mb(48)),
    )(a, b)


def _outproj_bwd(da1, wout):
    n, d = da1.shape
    tm = 512

    def body(a_ref, w_ref, o_ref):
        o_ref[...] = lax.dot_general(a_ref[...], w_ref[...], _NT, preferred_element_type=F32)

    return _pallas(
        body, name="outproj_bwd", grid=(n // tm,),
        out_shape=jax.ShapeDtypeStruct((n, d), F32),
        in_specs=[pl.BlockSpec((tm, d), lambda i: (i, 0)), pl.BlockSpec((d, d), lambda i: (0, 0))],
        out_specs=pl.BlockSpec((tm, d), lambda i: (i, 0)),
        compiler_params=_params(_mb(48)),
    )(da1, wout)


def _qkv_bwd(dh, win_g, x, ct, dr1, sc):
    na, wcols = dh.shape
    n, d = x.shape
    tm = CTX
    nlat = n // tm

    def body(dh_ref, w_ref, x_ref, ct_ref, dr_ref, sc_ref, gx_ref, acc_ref):
        i = pl.program_id(0)

        @pl.when(i == 0)
        def _():
            acc_ref[...] = jnp.zeros_like(acc_ref)

        du = jnp.zeros((tm, d), F32)
        for j in range(NDEV):
            du = du + lax.dot_general(dh_ref[:, j * IN_SHARD:(j + 1) * IN_SHARD], w_ref[j], _NT,
                                      preferred_element_type=F32)

        @pl.when(i < nlat)
        def _():
            gx_ref[...] = ALPHA * dr_ref[...] + du * (1.0 + sc_ref[0])
            acc_ref[0:1, :] += _colsum(du)
            acc_ref[1:2, :] += _colsum(du * x_ref[...])

        @pl.when(i == nlat)
        def _():
            acc_ref[2:3, :] += _colsum(du)
            acc_ref[3:4, :] += _colsum(du * ct_ref[...])

    lat = lambda i: (jnp.minimum(i, nlat - 1), 0)
    const2 = lambda i: (0, 0)
    return _pallas(
        body, name="qkv_bwd", grid=(nlat + 1,),
        out_shape=(jax.ShapeDtypeStruct((n, d), F32), jax.ShapeDtypeStruct((8, d), F32)),
        in_specs=[pl.BlockSpec((tm, wcols), lambda i: (i, 0)), pl.BlockSpec((NDEV, d, IN_SHARD), lambda i: (0, 0, 0)),
                  pl.BlockSpec((tm, d), lat), pl.BlockSpec((tm, d), const2), pl.BlockSpec((tm, d), lat),
                  pl.BlockSpec((1, 1, d), lambda i: (0, 0, 0))],
        out_specs=(pl.BlockSpec((tm, d), lat), pl.BlockSpec((8, d), const2)),
        compiler_params=_params(_mb(56)),
    )(dh, win_g, x, ct, dr1, sc)


def _adam_math(w, g, m, v):
    m2 = ADAM_B1 * m + (1.0 - ADAM_B1) * g
    v2 = ADAM_B2 * v + (1.0 - ADAM_B2) * (g * g)
    m_hat = m2 / (1.0 - ADAM_B1 ** ADAM_STEP)
    v_hat = v2 / (1.0 - ADAM_B2 ** ADAM_STEP)
    delta = -ADAM_LR * (m_hat / (jnp.sqrt(v_hat) + ADAM_EPS) + ADAM_WD * w)
    return delta, m2, v2


def _adamw(w, gsrc, m, v, name):
    r, c = w.shape
    parts = gsrc.ndim == 3
    tr = r
    while tr * c * 4 > _mb(1) and tr % 16 == 0:
        tr //= 2

    def body(w_ref, g_ref, m_ref, v_ref, go_ref, d_ref, mo_ref, vo_ref):
        if parts:
            g = g_ref[0].astype(F32)
            for s in range(1, NDEV):
                g = g + g_ref[s].astype(F32)
        else:
            g = g_ref[...]
        delta, m2, v2 = _adam_math(w_ref[...], g, m_ref[...], v_ref[...])
        go_ref[...] = g
        d_ref[...] = delta
        mo_ref[...] = m2
        vo_ref[...] = v2

    tile = pl.BlockSpec((tr, c), lambda i: (i, 0))
    gspec = pl.BlockSpec((NDEV, tr, c), lambda i: (0, i, 0)) if parts else tile
    sds = jax.ShapeDtypeStruct((r, c), F32)
    return _pallas(
        body, name=name, grid=(r // tr,),
        out_shape=(sds, sds, sds, sds),
        in_specs=[tile, gspec, tile, tile],
        out_specs=(tile, tile, tile, tile),
        compiler_params=_params(_mb(48)),
    )(w, gsrc, m, v)


def _small_update(gath, dcc, cc, w_s, m_s, v_s):
    d = w_s.shape[1]

    def body(g_ref, dcc_ref, cc_ref, w_ref, m_ref, v_ref, go_ref, d_ref, mo_ref, vo_ref):
        s = g_ref[0]
        for b in range(1, NDEV):
            s = s + g_ref[b]
        dsl = dcc_ref[0, 8:9, :]
        for b in range(1, NDEV):
            dsl = dsl + dcc_ref[b, 8:9, :]
        cv = cc_ref[...]
        sg = _sigmoid(cv)
        go_ref[...] = jnp.zeros_like(go_ref)
        go_ref[0:1, :] = dsl * (sg * (1.0 + cv * (1.0 - sg)))
        go_ref[1:3, :] = s[0:2] + s[6:8]
        go_ref[3:7, :] = s[2:6]
        go_ref[7:12, :] = s[8:13]
        delta, m2, v2 = _adam_math(w_ref[...], go_ref[...], m_ref[...], v_ref[...])
        d_ref[...] = delta
        mo_ref[...] = m2
        vo_ref[...] = v2

    full = pl.BlockSpec((16, d), lambda: (0, 0))
    g3 = pl.BlockSpec((NDEV, 16, d), lambda: (0, 0, 0))
    sds = jax.ShapeDtypeStruct((16, d), F32)
    return _pallas(
        body, name="small_update",
        out_shape=(sds, sds, sds, sds),
        in_specs=[g3, g3, pl.BlockSpec((1, d), lambda: (0, 0)), full, full, full],
        out_specs=(full, full, full, full),
        compiler_params=_params(_mb(24)),
    )(gath, dcc, cc, w_s, m_s, v_s)


def _rope_tables(n):
    rows = n // GRID_W
    row_ids = jnp.repeat(jnp.arange(rows, dtype=F32), GRID_W)
    col_ids = jnp.tile(jnp.arange(GRID_W, dtype=F32), rows)
    axis_dim = HEAD // 2
    inv_freq = jnp.power(ROPE_THETA, -jnp.arange(0, axis_dim, 2, dtype=F32) / axis_dim)
    ang_r = row_ids[:, None] * inv_freq
    ang_c = col_ids[:, None] * inv_freq
    ang = jnp.concatenate([ang_r, ang_r, ang_c, ang_c], axis=-1)
    cos, sin = jnp.cos(ang), jnp.sin(ang)
    first = (jnp.arange(HEAD) % (HEAD // 2)) < HEAD // 4
    sa = jnp.where(first, -sin, 0.0)
    sb = jnp.where(first, 0.0, sin)
    ones = jnp.ones((CTX, HEAD), F32)
    zeros = jnp.zeros((CTX, HEAD), F32)
    return (jnp.concatenate([cos, ones], 0), jnp.concatenate([sa, zeros], 0), jnp.concatenate([sb, zeros], 0))


def _pad_cols(a, width):
    return jnp.pad(a, ((0, 0), (0, width - a.shape[1])))


def _pad_rows(a, rows):
    return jnp.pad(a, ((0, rows - a.shape[0]), (0, 0)))


def _pack_small(c_ctx, b_ada, ln1_g, ln1_b, ln2_g, ln2_b, qg, kg, sink, d):
    misc = _pad_cols(jnp.concatenate([qg, kg, sink], axis=1), d)
    rows = jnp.concatenate([c_ctx.reshape(1, d), b_ada.reshape(6, d), ln1_g, ln1_b, ln2_g, ln2_b, misc], axis=0)
    return _pad_rows(rows, 16)


def _unpack_small(p, d):
    return dict(c_ctx=p[0], b_ada=p[1:7].reshape(1, 6 * d), ln1_g=p[7:8], ln1_b=p[8:9], ln2_g=p[9:10], ln2_b=p[10:11],
                q_norm_g=p[11:12, 0:HEAD], k_norm_g=p[11:12, HEAD:2 * HEAD], sink_logit=p[11:12, 2 * HEAD:2 * HEAD + 8])


def kernel(x, c, ctx, c_ctx, w_ada, b_ada, w_in, q_norm_g, k_norm_g, sink_logit, w_out, ln1_g, ln1_b, w_gate, w_up, w_down, ln2_g, ln2_b, loss_target, m_c_ctx, m_w_ada, m_b_ada, m_w_in, m_q_norm_g, m_k_norm_g, m_sink_logit, m_w_out, m_ln1_g, m_ln1_b, m_w_gate, m_w_up, m_w_down, m_ln2_g, m_ln2_b, v_c_ctx, v_w_ada, v_b_ada, v_w_in, v_q_norm_g, v_k_norm_g, v_sink_logit, v_w_out, v_ln1_g, v_ln1_b, v_w_gate, v_w_up, v_w_down, v_ln2_g, v_ln2_b):
    xs, cts, tgt = x[0], ctx[0], loss_target[0]
    n, d = xs.shape
    assert cts.shape == (CTX, d) and w_in.shape[2] == IN_SHARD and w_gate.shape[2] == FFN_SHARD
    me = 4 * lax.axis_index("x") + 2 * lax.axis_index("y") + lax.axis_index("c")
    e_sh = w_ada.shape[2]

    c_g = _exchange(_pad_rows(c, 8), False, "gather_c")
    c_all = jnp.concatenate([c_g[:, 0, :], _pad_rows(c_ctx.reshape(1, d), 8)], axis=0)
    bias_sh = lax.dynamic_slice(b_ada, (0, me * e_sh), (1, e_sh))
    mods_g = _exchange(_ada_fwd(c_all, w_ada[0], bias_sh), False, "gather_mods")
    mods = jnp.transpose(mods_g, (1, 0, 2)).reshape(16, NDEV * e_sh)
    mine = lax.dynamic_slice(mods, (me, 0), (1, 6 * d))
    sh1, sc1, g1, sh2, sc2, g2 = [mine[:, k * d:(k + 1) * d] for k in range(6)]
    csh1, csc1 = mods[8:9, 0:d], mods[8:9, d:2 * d]
    sc_pair = jnp.stack([sc1, csc1])
    sh_pair = jnp.stack([sh1, csh1])

    h_win = _exchange_start(w_in[0].astype(BF16), False, mods, "gather_w_in_start")
    h_wout = _exchange_start(w_out[0].astype(BF16), False, mods, "gather_w_out_start")
    h_wg = _exchange_start(_pad_cols(w_gate[0], FFN_PAD).astype(BF16), False, mods, "gather_w_gate_start")
    h_wu = _exchange_start(_pad_cols(w_up[0], FFN_PAD).astype(BF16), False, mods, "gather_w_up_start")
    h_wd = _exchange_start(_pad_rows(w_down[0], FFN_PAD).astype(BF16), False, mods, "gather_w_down_start")

    cos, sa, sb = _rope_tables(n)
    win_g = _exchange_wait(h_win, False, sc_pair, "gather_w_in_wait")
    u_all, h_all, t_all = _qkv_fwd(xs, cts, sc_pair, sh_pair, win_g, q_norm_g, k_norm_g, cos, sa, sb)
    o_a, lse_a = _attn_fwd(t_all, sink_logit, True, 0, 8, 10, 0, WINDOW, "attn_window_fwd")
    o_b, lse_b = _attn_fwd(t_all, sink_logit, False, 3, 20, 22, 2, 256, "attn_global_fwd")
    o = jnp.concatenate([o_a, o_b], axis=1)
    wout_g = _exchange_wait(h_wout, False, o, "gather_w_out_wait").reshape(d, d)
    a1, xh1, rs1 = _outproj_ln1(o, wout_g, xs, g1)
    wg_g = _exchange_wait(h_wg, False, rs1, "gather_w_gate_wait")
    wu_g = _exchange_wait(h_wu, False, rs1, "gather_w_up_wait")
    u2, gmat, pmat, hf = _ffn_up(xh1, ln1_g, ln1_b, sc2, sh2, wg_g, wu_g)
    wd_g = _exchange_wait(h_wd, False, u2, "gather_w_down_wait")
    ffn = _ffn_down(hf, wd_g.reshape(NDEV * FFN_PAD, d))
    dr2, loss_p, acc2 = _ln2_loss(xh1, ffn, tgt, ln1_g, ln1_b, g2, ln2_g, ln2_b)
    loss = lax.psum(loss_p[0, 0], ("x", "y", "c"))

    df, dgm, dpm = _ffn_dhf(dr2, g2, wd_g, gmat, pmat)
    dwd_p = _dw_rows(hf, df, NDEV, FFN_PAD, 512, "dw_down")
    h_dwd = _exchange_start(dwd_p, True, loss_p, "scatter_dw_down_start")
    dwg_p = _dw_cols(u2, dgm, NDEV, FFN_PAD, 512, "dw_gate")
    h_dwg = _exchange_start(dwg_p, True, loss_p, "scatter_dw_gate_start")
    dwu_p = _dw_cols(u2, dpm, NDEV, FFN_PAD, 512, "dw_up")
    h_dwu = _exchange_start(dwu_p, True, loss_p, "scatter_dw_up_start")
    du2 = _ffn_du2(dgm, dpm, wg_g, wu_g)
    dr1, da1, acc1 = _ln1_bwd(du2, dr2, xh1, rs1, a1, ln1_g, ln1_b, sc2, g1)
    dwo_p = _dw_rows(o, da1, NDEV, 2 * HEAD, 512, "dw_out")
    h_dwo = _exchange_start(dwo_p, True, loss_p, "scatter_dw_out_start")
    do = _outproj_bwd(da1, wout_g)
    dqa, dka, dva, dsink = _attn_bwd(t_all, o, do, lse_a, sink_logit, True, 0, 8, 10, 0, WINDOW, "attn_window_bwd")
    dqb, dkb, dvb, _ = _attn_bwd(t_all, o, do, lse_b, sink_logit, False, 3, 20, 22, 2, 256, "attn_global_bwd")
    dh_all, dnorm = _qkv_bwd_prep(dqa, dka, dva, dqb, dkb, dvb, h_all, q_norm_g, k_norm_g, cos, sa, sb)
    grad_x, acc0 = _qkv_bwd(dh_all, win_g, xs, cts, dr1, sc_pair)
    dwi_p = _dw_cols(u_all, dh_all, NDEV, IN_SHARD, CTX, "dw_in")

    misc = _pad_cols(jnp.concatenate([dnorm[0:1], dnorm[1:2], dsink[:, 0:4, 0].reshape(1, 8)], axis=1), d)
    part = jnp.concatenate([
        acc0[0:2], acc1[4:5], acc1[1:2], acc1[0:1], acc2[2:3],
        acc0[2:4],
        acc1[2:4], acc2[0:2],
        misc, jnp.zeros((3, d), F32)], axis=0)
    gath = _exchange(part, False, "gather_small")
    dm_batch = gath[:, 0:6, :].reshape(NDEV, 6 * d)
    dm_ctx = _pad_cols(gath[:, 6:8, :].reshape(NDEV, 2 * d), 6 * d)
    dm16 = lax.dynamic_slice(jnp.concatenate([dm_batch, dm_ctx], axis=0), (0, me * e_sh), (16, e_sh))
    dw_ada, drow = _ada_bwd(dm16, c_all, w_ada[0])
    dcc = _exchange(drow, False, "gather_dcc")
    h_dwi = _exchange_start(dwi_p, True, dcc, "scatter_dw_in_start")

    w_s = _pack_small(c_ctx, b_ada, ln1_g, ln1_b, ln2_g, ln2_b, q_norm_g, k_norm_g, sink_logit, d)
    m_s = _pack_small(m_c_ctx, m_b_ada, m_ln1_g, m_ln1_b, m_ln2_g, m_ln2_b, m_q_norm_g, m_k_norm_g, m_sink_logit, d)
    v_s = _pack_small(v_c_ctx, v_b_ada, v_ln1_g, v_ln1_b, v_ln2_g, v_ln2_b, v_q_norm_g, v_k_norm_g, v_sink_logit, d)
    small = [_unpack_small(p, d) for p in _small_update(gath, dcc, c_ctx.reshape(1, d), w_s, m_s, v_s)]

    big = {}
    big["w_ada"] = _adamw(w_ada[0], dw_ada, m_w_ada[0], v_w_ada[0], "adamw_w_ada")
    late = big["w_ada"][1]
    res = _adamw(_pad_rows(w_down[0], FFN_PAD), _exchange_wait(h_dwd, True, late, "scatter_dw_down_wait"),
                 _pad_rows(m_w_down[0], FFN_PAD), _pad_rows(v_w_down[0], FFN_PAD), "adamw_w_down")
    big["w_down"] = [r[:FFN_SHARD] for r in res]
    for nm, wt, mt, vt, hd in (("w_gate", w_gate, m_w_gate, v_w_gate, h_dwg), ("w_up", w_up, m_w_up, v_w_up, h_dwu)):
        res = _adamw(_pad_cols(wt[0], FFN_PAD), _exchange_wait(hd, True, late, "scatter_d" + nm + "_wait"),
                     _pad_cols(mt[0], FFN_PAD), _pad_cols(vt[0], FFN_PAD), "adamw_" + nm)
        big[nm] = [r[:, :FFN_SHARD] for r in res]
    big["w_out"] = _adamw(w_out[0], _exchange_wait(h_dwo, True, late, "scatter_dw_out_wait"), m_w_out[0], v_w_out[0],
                          "adamw_w_out")
    big["w_in"] = _adamw(w_in[0], _exchange_wait(h_dwi, True, big["w_out"][1], "scatter_dw_in_wait"), m_w_in[0],
                         v_w_in[0], "adamw_w_in")

    names = ["c_ctx", "w_ada", "b_ada", "w_in", "q_norm_g", "k_norm_g", "sink_logit", "w_out", "ln1_g", "ln1_b",
             "w_gate", "w_up", "w_down", "ln2_g", "ln2_b"]
    outs = [loss, grad_x[None]]
    for k in range(4):
        for nm in names:
            outs.append(big[nm][k][None] if nm in big else small[k][nm])
    return tuple(outs)
```

```python
import functools

import jax
import jax.numpy as jnp
from jax import lax
from jax.experimental import pallas as pl
from jax.experimental.pallas import tpu as pltpu

F32 = jnp.float32
BF16 = jnp.bfloat16

NDEV = 8
HEAD = 128
CTX = 256
GRID_W = 64
WINDOW = 128
ROPE_THETA = 10000.0
EPS = 1e-6
SCALE = HEAD ** -0.5
ALPHA = 2.0 ** 0.25
FFN_SHARD = 704
FFN_PAD = 768
IN_SHARD = 384
NEG = -1e30

ADAM_LR = 0.001
ADAM_B1 = 0.9
ADAM_B2 = 0.999
ADAM_EPS = 1e-08
ADAM_WD = 0.01
ADAM_STEP = 10

VMEM_CAP = 56 * 1024 * 1024

_KINDS = ["rope"] * 10 + ["none"] * 2 + ["qnorm"] * 8 + ["knorm"] * 2 + ["none"] * 2

_NT = (((1,), (1,)), ((), ()))
_TN = (((0,), (0,)), ((), ()))


def _pallas(body, **kw):
    return pl.pallas_call(body, **kw)


def _params(vmem_bytes):
    return pltpu.CompilerParams(vmem_limit_bytes=int(min(VMEM_CAP, vmem_bytes)))


def _mb(n):
    return int(n * 1024 * 1024)


def _sigmoid(x):
    return 1.0 / (1.0 + jnp.exp(-x))


def _colsum(a):
    return jnp.sum(a, axis=0, keepdims=True)


def _rowmean(a):
    return jnp.mean(a, axis=-1, keepdims=True)


def _exchange(src, scatter, name):
    blk = src.shape[1:] if scatter else src.shape

    def body(src_ref, out_ref, send_sems, recv_sems, local_sem):
        x, y, c = lax.axis_index("x"), lax.axis_index("y"), lax.axis_index("c")
        me = 4 * x + 2 * y + c
        copies = []
        for t in range(1, NDEV):
            px = 1 - x if (t >> 2) & 1 else x
            py = 1 - y if (t >> 1) & 1 else y
            pc = 1 - c if t & 1 else c
            peer = 4 * px + 2 * py + pc
            cp = pltpu.make_async_remote_copy(
                src_ref=src_ref.at[peer] if scatter else src_ref,
                dst_ref=out_ref.at[me],
                send_sem=send_sems.at[t - 1],
                recv_sem=recv_sems.at[t - 1],
                device_id=(px, py, pc),
                device_id_type=pl.DeviceIdType.MESH,
            )
            cp.start()
            copies.append(cp)
        own = pltpu.make_async_copy(src_ref.at[me] if scatter else src_ref, out_ref.at[me], local_sem)
        own.start()
        for cp in copies:
            cp.wait()
        own.wait()

    return _pallas(
        body, name=name,
        out_shape=jax.ShapeDtypeStruct((NDEV,) + tuple(blk), src.dtype),
        in_specs=[pl.BlockSpec(memory_space=pl.ANY)],
        out_specs=pl.BlockSpec(memory_space=pl.ANY),
        scratch_shapes=[pltpu.SemaphoreType.DMA((NDEV - 1,)), pltpu.SemaphoreType.DMA((NDEV - 1,)),
                        pltpu.SemaphoreType.DMA(())],
    )(src)


_HBM = pl.BlockSpec(memory_space=pltpu.HBM)
_SEM = pl.BlockSpec(memory_space=pltpu.SEMAPHORE)
_ANY = pl.BlockSpec(memory_space=pl.ANY)
_EFFECT = pltpu.SideEffectType.DATAFLOW_SIDE_EFFECTING


def _exchange_copies(src_ref, land_ref, send_sems, recv_sems, scatter):
    x, y, c = lax.axis_index("x"), lax.axis_index("y"), lax.axis_index("c")
    me = 4 * x + 2 * y + c
    copies = []
    for t in range(1, NDEV):
        px = 1 - x if (t >> 2) & 1 else x
        py = 1 - y if (t >> 1) & 1 else y
        pc = 1 - c if t & 1 else c
        peer = 4 * px + 2 * py + pc
        copies.append(pltpu.make_async_remote_copy(
            src_ref=src_ref.at[peer] if scatter else src_ref,
            dst_ref=land_ref.at[me],
            send_sem=send_sems.at[t - 1],
            recv_sem=recv_sems.at[t - 1],
            device_id=(px, py, pc),
            device_id_type=pl.DeviceIdType.MESH,
        ))
    own = pltpu.make_async_copy(src_ref.at[me] if scatter else src_ref, land_ref.at[me], send_sems.at[NDEV - 1])
    return copies, own


def _exchange_start(src, scatter, after, name):
    blk = src.shape[1:] if scatter else src.shape
    land = lax.empty((NDEV,) + tuple(blk), src.dtype)

    def body(src_ref, land_ref, after_ref, send_sems, recv_sems, src_thru, land_thru, token):
        copies, own = _exchange_copies(src_ref, land_ref, send_sems, recv_sems, scatter)
        for cp in copies:
            cp.start()
        own.start()
        token[...] = jnp.zeros_like(token)

    res = _pallas(
        body, name=name,
        out_shape=(pltpu.SemaphoreType.DMA((NDEV,)), pltpu.SemaphoreType.DMA((NDEV,)),
                   pltpu.HBM(src.shape, src.dtype), pltpu.HBM(land.shape, land.dtype),
                   jax.ShapeDtypeStruct((8, HEAD), F32)),
        in_specs=(_HBM, _HBM, _ANY), out_specs=(_SEM, _SEM, _HBM, _HBM, pl.BlockSpec(memory_space=pltpu.VMEM)),
        input_output_aliases={0: 2, 1: 3},
        compiler_params=pltpu.CompilerParams(has_side_effects=_EFFECT),
    )(pltpu.with_memory_space_constraint(src, pltpu.HBM), pltpu.with_memory_space_constraint(land, pltpu.HBM), after)
    return res[:4], res[4]


def _exchange_wait(handle, scatter, after, name):
    send_sems, recv_sems, src_thru, land_thru = handle

    def body(src_ref, land_ref, send_sems, recv_sems, after_ref, src_dead, got_ref):
        copies, own = _exchange_copies(src_ref, land_ref, send_sems, recv_sems, scatter)
        for cp in copies:
            cp.wait_send()
            cp.wait_recv()
        own.wait()

    return _pallas(
        body, name=name,
        out_shape=(pltpu.HBM(src_thru.shape, src_thru.dtype), pltpu.HBM(land_thru.shape, land_thru.dtype)),
        in_specs=(_HBM, _HBM, _SEM, _SEM, _ANY), out_specs=(_HBM, _HBM),
        input_output_aliases={0: 0, 1: 1},
        compiler_params=pltpu.CompilerParams(has_side_effects=_EFFECT),
    )(src_thru, land_thru, send_sems, recv_sems, after)[1]


def _ada_fwd(c_all, w, bias):
    r, d = c_all.shape
    e = w.shape[1]
    tn = 512

    def body(c_ref, w_ref, b_ref, o_ref):
        cv = c_ref[...]
        s = (cv * _sigmoid(cv)).astype(BF16)
        o_ref[...] = jnp.dot(s, w_ref[...].astype(BF16), preferred_element_type=F32) + b_ref[...]

    return _pallas(
        body, name="ada_fwd", grid=(e // tn,),
        out_shape=jax.ShapeDtypeStruct((r, e), F32),
        in_specs=[pl.BlockSpec((r, d), lambda j: (0, 0)), pl.BlockSpec((d, tn), lambda j: (0, j)),
                  pl.BlockSpec((1, tn), lambda j: (0, j))],
        out_specs=pl.BlockSpec((r, tn), lambda j: (0, j)),
        compiler_params=_params(_mb(24)),
    )(c_all, w, bias)


def _ada_bwd(dm16, c_all, w):
    d, e = w.shape
    tn = 512

    def body(dm_ref, c_ref, w_ref, dw_ref, dr_ref):
        j = pl.program_id(0)
        dm = dm_ref[...]
        rid = lax.broadcasted_iota(jnp.int32, dm.shape, 0)
        ctx_sum = jnp.sum(jnp.where(rid >= 8, dm, 0.0), axis=0, keepdims=True)
        rows = jnp.where(rid < 8, dm, jnp.where(rid == 8, jnp.broadcast_to(ctx_sum, dm.shape), 0.0)).astype(BF16)
        cv = c_ref[...]
        s = (cv * _sigmoid(cv)).astype(BF16)
        dw_ref[...] = lax.dot_general(s, rows, _TN, preferred_element_type=F32)
        part = lax.dot_general(rows, w_ref[...].astype(BF16), _NT, preferred_element_type=F32)

        @pl.when(j == 0)
        def _():
            dr_ref[...] = part

        @pl.when(j > 0)
        def _():
            dr_ref[...] += part

    return _pallas(
        body, name="ada_bwd", grid=(e // tn,),
        out_shape=(jax.ShapeDtypeStruct((d, e), F32), jax.ShapeDtypeStruct((16, d), F32)),
        in_specs=[pl.BlockSpec((16, tn), lambda j: (0, j)), pl.BlockSpec((16, d), lambda j: (0, 0)),
                  pl.BlockSpec((d, tn), lambda j: (0, j))],
        out_specs=(pl.BlockSpec((d, tn), lambda j: (0, j)), pl.BlockSpec((16, d), lambda j: (0, 0))),
        compiler_params=_params(_mb(32)),
    )(dm16, c_all, w)


def _rope(v, cos, sa, sb):
    return v * cos + (pltpu.roll(v, 96, 1) * sa + pltpu.roll(v, 32, 1) * sb)


def _rope_t(dt, cos, sa, sb):
    return dt * cos + (pltpu.roll(dt * sa, 32, 1) + pltpu.roll(dt * sb, 96, 1))


def _qkv_fwd(x, ct, sc, sh, win_g, qg, kg, cos, sa, sb):
    n, d = x.shape
    tm = CTX
    nlat = n // tm
    na = n + CTX
    wcols = NDEV * IN_SHARD

    def body(x_ref, ct_ref, sc_ref, sh_ref, w_ref, qg_ref, kg_ref, cos_ref, sa_ref, sb_ref, u_ref, h_ref, t_ref):
        i = pl.program_id(0)
        xin = jnp.where(i == nlat, ct_ref[...], x_ref[...])
        u = (xin * (1.0 + sc_ref[0]) + sh_ref[0]).astype(BF16)
        u_ref[...] = u
        cos, sa, sb = cos_ref[...], sa_ref[...], sb_ref[...]
        for j in range(NDEV):
            h = jnp.dot(u, w_ref[j], preferred_element_type=F32)
            h_ref[:, j * IN_SHARD:(j + 1) * IN_SHARD] = h
            for hh in range(3):
                hd = 3 * j + hh
                v = h[:, hh * HEAD:(hh + 1) * HEAD]
                kind = _KINDS[hd]
                if kind == "qnorm":
                    v = v * lax.rsqrt(_rowmean(v * v) + EPS) * qg_ref[...]
                elif kind == "knorm":
                    v = v * lax.rsqrt(_rowmean(v * v) + EPS) * kg_ref[...]
                if kind != "none":
                    v = _rope(v, cos, sa, sb)
                t_ref[:, hd * HEAD:(hd + 1) * HEAD] = v.astype(BF16)

    lat = lambda i: (jnp.minimum(i, nlat - 1), 0)
    row = lambda i: (i, 0)
    const2 = lambda i: (0, 0)
    return _pallas(
        body, name="qkv_fwd", grid=(nlat + 1,),
        out_shape=(jax.ShapeDtypeStruct((na, d), BF16), jax.ShapeDtypeStruct((na, wcols), F32),
                   jax.ShapeDtypeStruct((na, wcols), BF16)),
        in_specs=[pl.BlockSpec((tm, d), lat), pl.BlockSpec((tm, d), const2),
                  pl.BlockSpec((1, 1, d), lambda i: (i // nlat, 0, 0)),
                  pl.BlockSpec((1, 1, d), lambda i: (i // nlat, 0, 0)),
                  pl.BlockSpec((NDEV, d, IN_SHARD), lambda i: (0, 0, 0)),
                  pl.BlockSpec((1, HEAD), const2), pl.BlockSpec((1, HEAD), const2),
                  pl.BlockSpec((tm, HEAD), row), pl.BlockSpec((tm, HEAD), row), pl.BlockSpec((tm, HEAD), row)],
        out_specs=(pl.BlockSpec((tm, d), row), pl.BlockSpec((tm, wcols), row), pl.BlockSpec((tm, wcols), row)),
        compiler_params=_params(_mb(56)),
    )(x, ct, sc, sh, win_g, qg, kg, cos, sa, sb)


def _qkv_bwd_prep(dqa, dka, dva, dqb, dkb, dvb, h_all, qg, kg, cos, sa, sb):
    na, wcols = h_all.shape
    n = na - CTX
    tm = CTX
    nlat = n // tm

    def body(dqa_ref, dka_ref, dva_ref, dqb_ref, dkb_ref, dvb_ref, h_ref, qg_ref, kg_ref, cos_ref, sa_ref, sb_ref,
             dh_ref, dg_ref):
        i = pl.program_id(0)

        @pl.when(i == 0)
        def _():
            dg_ref[...] = jnp.zeros_like(dg_ref)

        cos, sa, sb = cos_ref[...], sa_ref[...], sb_ref[...]
        is_lat = i < nlat
        for hd in range(24):
            kind = _KINDS[hd]
            if hd < 8:
                dt = jnp.where(is_lat, dqa_ref[:, hd * HEAD:(hd + 1) * HEAD], 0.0)
            elif hd < 10:
                dt = dka_ref[:, (hd - 8) * HEAD:(hd - 7) * HEAD]
            elif hd < 12:
                dt = dva_ref[:, (hd - 10) * HEAD:(hd - 9) * HEAD]
            elif hd < 20:
                dt = jnp.where(is_lat, dqb_ref[:, (hd - 12) * HEAD:(hd - 11) * HEAD], 0.0)
            elif hd < 22:
                dt = dkb_ref[:, (hd - 20) * HEAD:(hd - 19) * HEAD]
            else:
                dt = dvb_ref[:, (hd - 22) * HEAD:(hd - 21) * HEAD]
            if kind != "none":
                dt = _rope_t(dt, cos, sa, sb)
            if kind in ("qnorm", "knorm"):
                g_ref = qg_ref if kind == "qnorm" else kg_ref
                r0 = 0 if kind == "qnorm" else 1
                xv = h_ref[:, hd * HEAD:(hd + 1) * HEAD]
                xn = xv * lax.rsqrt(_rowmean(xv * xv) + EPS)
                dg_ref[r0:r0 + 1, :] += _colsum(dt * xn)
                dxn = dt * g_ref[...]
                dt = lax.rsqrt(_rowmean(xv * xv) + EPS) * (dxn - xn * _rowmean(dxn * xn))
            dh_ref[:, hd * HEAD:(hd + 1) * HEAD] = dt.astype(BF16)

    lat = lambda i: (jnp.minimum(i, nlat - 1), 0)
    row = lambda i: (i, 0)
    const2 = lambda i: (0, 0)
    return _pallas(
        body, name="qkv_bwd_prep", grid=(nlat + 1,),
        out_shape=(jax.ShapeDtypeStruct((na, wcols), BF16), jax.ShapeDtypeStruct((8, HEAD), F32)),
        in_specs=[pl.BlockSpec((tm, 8 * HEAD), lat), pl.BlockSpec((tm, 2 * HEAD), row), pl.BlockSpec((tm, 2 * HEAD), row),
                  pl.BlockSpec((tm, 8 * HEAD), lat), pl.BlockSpec((tm, 2 * HEAD), row), pl.BlockSpec((tm, 2 * HEAD), row),
                  pl.BlockSpec((tm, wcols), row),
                  pl.BlockSpec((1, HEAD), const2), pl.BlockSpec((1, HEAD), const2),
                  pl.BlockSpec((tm, HEAD), row), pl.BlockSpec((tm, HEAD), row), pl.BlockSpec((tm, HEAD), row)],
        out_specs=(pl.BlockSpec((tm, wcols), row), pl.BlockSpec((8, HEAD), const2)),
        compiler_params=_params(_mb(40)),
    )(dqa, dka, dva, dqb, dkb, dvb, h_all, qg, kg, cos, sa, sb)


def _attn_keys(window, k_ref, v_ref, n, na, tq):
    i = pl.program_id(1)
    if not window:
        return k_ref[...], v_ref[...], None, None
    start = pl.multiple_of(jnp.clip((i - 1) * tq, 0, n - 3 * tq), tq)
    kk = jnp.concatenate([k_ref[pl.ds(start, 3 * tq), :], k_ref[n:na, :]], axis=0)
    vv = jnp.concatenate([v_ref[pl.ds(start, 3 * tq), :], v_ref[n:na, :]], axis=0)
    nk = 3 * tq + CTX
    col = lax.broadcasted_iota(jnp.int32, (tq, nk), 1)
    rowi = lax.broadcasted_iota(jnp.int32, (tq, nk), 0)
    valid = (jnp.abs(i * tq + rowi - (start + col)) <= WINDOW) | (col >= 3 * tq)
    return kk, vv, valid, start


def _attn_fwd(t_all, sink, window, qblk, kblk, vblk, oblk, tq, name):
    na = t_all.shape[0]
    n = na - CTX

    def body(sink_ref, q_ref, k_ref, v_ref, o_ref, lse_ref):
        kv = pl.program_id(0)
        kk, vv, valid, _ = _attn_keys(window, k_ref, v_ref, n, na, tq)
        for g in range(4):
            q = q_ref[:, g * HEAD:(g + 1) * HEAD]
            s = lax.dot_general(q, kk, _NT, preferred_element_type=F32) * SCALE
            if window:
                s = jnp.where(valid, s, NEG)
            m = jnp.max(s, axis=-1, keepdims=True)
            if window:
                sk = sink_ref[0, 4 * kv + g]
                m = jnp.maximum(m, sk)
            p = jnp.exp(s - m)
            l = jnp.sum(p, axis=-1, keepdims=True)
            if window:
                l = l + jnp.exp(sk - m)
            o = jnp.dot((p / l).astype(BF16), vv, preferred_element_type=F32)
            o_ref[:, g * HEAD:(g + 1) * HEAD] = o
            lse_ref[:, g * HEAD:(g + 1) * HEAD] = jnp.broadcast_to(m + jnp.log(l), (tq, HEAD))

    qspec = pl.BlockSpec((tq, 4 * HEAD), lambda kv, i: (i, qblk + kv))
    ospec = pl.BlockSpec((tq, 4 * HEAD), lambda kv, i: (i, kv))
    return _pallas(
        body, name=name, grid=(2, n // tq),
        out_shape=(jax.ShapeDtypeStruct((n, 8 * HEAD), F32), jax.ShapeDtypeStruct((n, 8 * HEAD), F32)),
        in_specs=[pl.BlockSpec(memory_space=pltpu.SMEM), qspec,
                  pl.BlockSpec((na, HEAD), lambda kv, i: (0, kblk + kv)),
                  pl.BlockSpec((na, HEAD), lambda kv, i: (0, vblk + kv))],
        out_specs=(ospec, ospec),
        compiler_params=_params(_mb(48)),
    )(sink, t_all, t_all, t_all)


def _attn_bwd(t_all, o, do, lse, sink, window, qblk, kblk, vblk, oblk, tq, name):
    na = t_all.shape[0]
    n = na - CTX

    def body(sink_ref, q_ref, k_ref, v_ref, o_ref, do_ref, lse_ref, dq_ref, dk_ref, dv_ref, dsink_ref):
        kv = pl.program_id(0)
        i = pl.program_id(1)

        @pl.when(i == 0)
        def _():
            dk_ref[...] = jnp.zeros_like(dk_ref)
            dv_ref[...] = jnp.zeros_like(dv_ref)
            dsink_ref[...] = jnp.zeros_like(dsink_ref)

        kk, vv, valid, start = _attn_keys(window, k_ref, v_ref, n, na, tq)
        dk_acc = jnp.zeros((kk.shape[0], HEAD), F32)
        dv_acc = jnp.zeros((kk.shape[0], HEAD), F32)
        for g in range(4):
            q = q_ref[:, g * HEAD:(g + 1) * HEAD]
            s = lax.dot_general(q, kk, _NT, preferred_element_type=F32) * SCALE
            if window:
                s = jnp.where(valid, s, NEG)
            lse_g = lse_ref[:, g * HEAD:g * HEAD + 1]
            p = jnp.exp(s - lse_g)
            dof = do_ref[:, g * HEAD:(g + 1) * HEAD]
            delta = jnp.sum(dof * o_ref[:, g * HEAD:(g + 1) * HEAD], axis=-1, keepdims=True)
            dob = dof.astype(BF16)
            dv_acc = dv_acc + lax.dot_general(p.astype(BF16), dob, _TN, preferred_element_type=F32)
            dp = lax.dot_general(dob, vv, _NT, preferred_element_type=F32)
            ds = (p * (dp - delta) * SCALE).astype(BF16)
            dq_ref[:, g * HEAD:(g + 1) * HEAD] = jnp.dot(ds, kk, preferred_element_type=F32)
            dk_acc = dk_acc + lax.dot_general(ds, q, _TN, preferred_element_type=F32)
            if window:
                p_sink = jnp.exp(sink_ref[0, 4 * kv + g] - lse_g)
                dsink_ref[0, g:g + 1, :] += jnp.broadcast_to(-jnp.sum(p_sink * delta, axis=0, keepdims=True), (1, HEAD))
        if window:
            dk_ref[pl.ds(start, 3 * tq), :] += dk_acc[:3 * tq]
            dv_ref[pl.ds(start, 3 * tq), :] += dv_acc[:3 * tq]
            dk_ref[n:na, :] += dk_acc[3 * tq:]
            dv_ref[n:na, :] += dv_acc[3 * tq:]
        else:
            dk_ref[...] += dk_acc
            dv_ref[...] += dv_acc

    qspec = pl.BlockSpec((tq, 4 * HEAD), lambda kv, i: (i, qblk + kv))
    ospec = pl.BlockSpec((tq, 4 * HEAD), lambda kv, i: (i, oblk + kv))
    lspec = pl.BlockSpec((tq, 4 * HEAD), lambda kv, i: (i, kv))
    kvout = pl.BlockSpec((na, HEAD), lambda kv, i: (0, kv))
    return _pallas(
        body, name=name, grid=(2, n // tq),
        out_shape=(jax.ShapeDtypeStruct((n, 8 * HEAD), F32), jax.ShapeDtypeStruct((na, 2 * HEAD), F32),
                   jax.ShapeDtypeStruct((na, 2 * HEAD), F32), jax.ShapeDtypeStruct((2, 8, HEAD), F32)),
        in_specs=[pl.BlockSpec(memory_space=pltpu.SMEM), qspec,
                  pl.BlockSpec((na, HEAD), lambda kv, i: (0, kblk + kv)),
                  pl.BlockSpec((na, HEAD), lambda kv, i: (0, vblk + kv)),
                  ospec, ospec, lspec],
        out_specs=(lspec, kvout, kvout, pl.BlockSpec((1, 8, HEAD), lambda kv, i: (kv, 0, 0))),
        compiler_params=_params(_mb(56)),
    )(sink, t_all, t_all, t_all, o, do, lse)


def _outproj_ln1(o, wout, x, g1):
    n, d = x.shape
    tm = 256

    def body(o_ref, w_ref, x_ref, g1_ref, a_ref, xh_ref, rs_ref):
        a1 = jnp.dot(o_ref[...].astype(BF16), w_ref[...], preferred_element_type=F32)
        a_ref[...] = a1
        r = ALPHA * x_ref[...] + g1_ref[...] * a1
        dlt = r - _rowmean(r)
        rstd = lax.rsqrt(_rowmean(dlt * dlt) + EPS)
        xh_ref[...] = dlt * rstd
        rs_ref[...] = rstd

    row = lambda i: (i, 0)
    const2 = lambda i: (0, 0)
    return _pallas(
        body, name="outproj_ln1", grid=(n // tm,),
        out_shape=(jax.ShapeDtypeStruct((n, d), F32), jax.ShapeDtypeStruct((n, d), F32),
                   jax.ShapeDtypeStruct((n, 1), F32)),
        in_specs=[pl.BlockSpec((tm, d), row), pl.BlockSpec((d, d), const2), pl.BlockSpec((tm, d), row),
                  pl.BlockSpec((1, d), const2)],
        out_specs=(pl.BlockSpec((tm, d), row), pl.BlockSpec((tm, d), row), pl.BlockSpec((tm, 1), row)),
        compiler_params=_params(_mb(56)),
    )(o, wout, x, g1)


def _ffn_up(xh1, lg, lb, sc2, sh2, wg_g, wu_g):
    n, d = xh1.shape
    tm = 512
    f = NDEV * FFN_PAD

    def body(xh_ref, lg_ref, lb_ref, sc_ref, sh_ref, wg_ref, wu_ref, u_ref, g_ref, p_ref, hf_ref):
        @pl.when(pl.program_id(1) == 0)
        def _():
            x1 = xh_ref[...] * lg_ref[...] + lb_ref[...]
            u_ref[...] = (x1 * (1.0 + sc_ref[...]) + sh_ref[...]).astype(BF16)

        u = u_ref[...]
        gv = jnp.dot(u, wg_ref[0], preferred_element_type=F32)
        pv = jnp.dot(u, wu_ref[0], preferred_element_type=F32)
        g_ref[...] = gv
        p_ref[...] = pv
        hf_ref[...] = (gv * _sigmoid(gv) * pv).astype(BF16)

    row = lambda i, j: (i, 0)
    const2 = lambda i, j: (0, 0)
    tile = lambda i, j: (i, j)
    wspec = pl.BlockSpec((1, d, FFN_PAD), lambda i, j: (j, 0, 0))
    vec = pl.BlockSpec((1, d), const2)
    return _pallas(
        body, name="ffn_up", grid=(n // tm, NDEV),
        out_shape=(jax.ShapeDtypeStruct((n, d), BF16), jax.ShapeDtypeStruct((n, f), F32),
                   jax.ShapeDtypeStruct((n, f), F32), jax.ShapeDtypeStruct((n, f), BF16)),
        in_specs=[pl.BlockSpec((tm, d), row), vec, vec, vec, vec, wspec, wspec],
        out_specs=(pl.BlockSpec((tm, d), row), pl.BlockSpec((tm, FFN_PAD), tile), pl.BlockSpec((tm, FFN_PAD), tile),
                   pl.BlockSpec((tm, FFN_PAD), tile)),
        compiler_params=_params(_mb(48)),
    )(xh1, lg, lb, sc2, sh2, wg_g, wu_g)


def _ffn_down(hf, wd):
    n, f = hf.shape
    d = wd.shape[1]
    tm, tn = 512, 512

    def body(h_ref, w_ref, o_ref):
        o_ref[...] = jnp.dot(h_ref[...], w_ref[...], preferred_element_type=F32)

    return _pallas(
        body, name="ffn_down", grid=(n // tm, d // tn),
        out_shape=jax.ShapeDtypeStruct((n, d), F32),
        in_specs=[pl.BlockSpec((tm, f), lambda i, j: (i, 0)), pl.BlockSpec((f, tn), lambda i, j: (0, j))],
        out_specs=pl.BlockSpec((tm, tn), lambda i, j: (i, j)),
        compiler_params=_params(_mb(48)),
    )(hf, wd)


def _ln2_loss(xh1, ffn, tgt, lg1, lb1, g2, lg2, lb2):
    n, d = xh1.shape
    tm = 256

    def body(xh_ref, f_ref, t_ref, lg1_ref, lb1_ref, g2_ref, lg2_ref, lb2_ref, dr_ref, loss_ref, acc_ref):
        @pl.when(pl.program_id(0) == 0)
        def _():
            loss_ref[...] = jnp.zeros_like(loss_ref)
            acc_ref[...] = jnp.zeros_like(acc_ref)

        x1 = xh_ref[...] * lg1_ref[...] + lb1_ref[...]
        fv = f_ref[...]
        r = ALPHA * x1 + g2_ref[...] * fv
        dlt = r - _rowmean(r)
        rstd = lax.rsqrt(_rowmean(dlt * dlt) + EPS)
        xh2 = dlt * rstd
        err = xh2 * lg2_ref[...] + lb2_ref[...] - t_ref[...]
        loss_ref[...] += 0.5 * jnp.sum(_rowmean(err * err))
        dy = err * (1.0 / d)
        dyg = dy * lg2_ref[...]
        dr = rstd * (dyg - _rowmean(dyg) - xh2 * _rowmean(dyg * xh2))
        dr_ref[...] = dr
        acc_ref[0:1, :] += _colsum(dy * xh2)
        acc_ref[1:2, :] += _colsum(dy)
        acc_ref[2:3, :] += _colsum(dr * fv)

    row = lambda i: (i, 0)
    const2 = lambda i: (0, 0)
    vec = pl.BlockSpec((1, d), const2)
    return _pallas(
        body, name="ln2_loss", grid=(n // tm,),
        out_shape=(jax.ShapeDtypeStruct((n, d), F32), jax.ShapeDtypeStruct((8, HEAD), F32),
                   jax.ShapeDtypeStruct((8, d), F32)),
        in_specs=[pl.BlockSpec((tm, d), row), pl.BlockSpec((tm, d), row), pl.BlockSpec((tm, d), row),
                  vec, vec, vec, vec, vec],
        out_specs=(pl.BlockSpec((tm, d), row), pl.BlockSpec((8, HEAD), const2), pl.BlockSpec((8, d), const2)),
        compiler_params=_params(_mb(48)),
    )(xh1, ffn, tgt, lg1, lb1, g2, lg2, lb2)


def _ffn_dhf(dr2, g2, wd_g, gmat, pmat):
    n, d = dr2.shape
    f = gmat.shape[1]
    tm = 512

    def body(dr_ref, g2_ref, w_ref, g_ref, p_ref, df_ref, dg_ref, dp_ref):
        @pl.when(pl.program_id(1) == 0)
        def _():
            df_ref[...] = (g2_ref[...] * dr_ref[...]).astype(BF16)

        dhf = lax.dot_general(df_ref[...], w_ref[0], _NT, preferred_element_type=F32)
        gv = g_ref[...]
        sg = _sigmoid(gv)
        dp_ref[...] = (dhf * (gv * sg)).astype(BF16)
        dg_ref[...] = (dhf * p_ref[...] * (sg * (1.0 + gv * (1.0 - sg)))).astype(BF16)

    row = lambda i, j: (i, 0)
    tile = lambda i, j: (i, j)
    return _pallas(
        body, name="ffn_dhf", grid=(n // tm, NDEV),
        out_shape=(jax.ShapeDtypeStruct((n, d), BF16), jax.ShapeDtypeStruct((n, f), BF16),
                   jax.ShapeDtypeStruct((n, f), BF16)),
        in_specs=[pl.BlockSpec((tm, d), row), pl.BlockSpec((1, d), lambda i, j: (0, 0)),
                  pl.BlockSpec((1, FFN_PAD, d), lambda i, j: (j, 0, 0)),
                  pl.BlockSpec((tm, FFN_PAD), tile), pl.BlockSpec((tm, FFN_PAD), tile)],
        out_specs=(pl.BlockSpec((tm, d), row), pl.BlockSpec((tm, FFN_PAD), tile), pl.BlockSpec((tm, FFN_PAD), tile)),
        compiler_params=_params(_mb(48)),
    )(dr2, g2, wd_g, gmat, pmat)


def _ffn_du2(dg, dp, wg_g, wu_g, after):
    n, f = dg.shape
    d = wg_g.shape[1]
    tm = 512

    def body(dg_ref, dp_ref, wg_ref, wu_ref, after_ref, o_ref):
        part = (lax.dot_general(dg_ref[...], wg_ref[0], _NT, preferred_element_type=F32)
                + lax.dot_general(dp_ref[...], wu_ref[0], _NT, preferred_element_type=F32))

        @pl.when(pl.program_id(1) == 0)
        def _():
            o_ref[...] = part

        @pl.when(pl.program_id(1) > 0)
        def _():
            o_ref[...] += part

    tile = lambda i, j: (i, j)
    wspec = pl.BlockSpec((1, d, FFN_PAD), lambda i, j: (j, 0, 0))
    return _pallas(
        body, name="ffn_du2", grid=(n // tm, NDEV),
        out_shape=jax.ShapeDtypeStruct((n, d), F32),
        in_specs=[pl.BlockSpec((tm, FFN_PAD), tile), pl.BlockSpec((tm, FFN_PAD), tile), wspec, wspec, _ANY],
        out_specs=pl.BlockSpec((tm, d), lambda i, j: (i, 0)),
        compiler_params=_params(_mb(48)),
    )(dg, dp, wg_g, wu_g, after)


def _ln1_bwd(du2, dr2, xh1, rs1, a1, lg1, lb1, sc2, g1):
    n, d = du2.shape
    tm = 256

    def body(du_ref, dr2_ref, xh_ref, rs_ref, a_ref, lg_ref, lb_ref, sc_ref, g1_ref, dr1_ref, da_ref, acc_ref):
        @pl.when(pl.program_id(0) == 0)
        def _():
            acc_ref[...] = jnp.zeros_like(acc_ref)

        du = du_ref[...]
        xh = xh_ref[...]
        x1 = xh * lg_ref[...] + lb_ref[...]
        dx1 = ALPHA * dr2_ref[...] + du * (1.0 + sc_ref[...])
        dxg = dx1 * lg_ref[...]
        dr1 = rs_ref[...] * (dxg - _rowmean(dxg) - xh * _rowmean(dxg * xh))
        dr1_ref[...] = dr1
        da_ref[...] = (g1_ref[...] * dr1).astype(BF16)
        acc_ref[0:1, :] += _colsum(du * x1)
        acc_ref[1:2, :] += _colsum(du)
        acc_ref[2:3, :] += _colsum(dx1 * xh)
        acc_ref[3:4, :] += _colsum(dx1)
        acc_ref[4:5, :] += _colsum(dr1 * a_ref[...])

    row = lambda i: (i, 0)
    const2 = lambda i: (0, 0)
    vec = pl.BlockSpec((1, d), const2)
    big = pl.BlockSpec((tm, d), row)
    return _pallas(
        body, name="ln1_bwd", grid=(n // tm,),
        out_shape=(jax.ShapeDtypeStruct((n, d), F32), jax.ShapeDtypeStruct((n, d), BF16),
                   jax.ShapeDtypeStruct((8, d), F32)),
        in_specs=[big, big, big, pl.BlockSpec((tm, 1), row), big, vec, vec, vec, vec],
        out_specs=(big, big, pl.BlockSpec((8, d), const2)),
        compiler_params=_params(_mb(48)),
    )(du2, dr2, xh1, rs1, a1, lg1, lb1, sc2, g1)


def _dw_cols(a, b, nblk, bw, tm, after, name):
    m, k = a.shape

    def body(a_ref, b_ref, after_ref, o_ref, acc_ref):
        part = lax.dot_general(a_ref[...], b_ref[...], _TN, preferred_element_type=F32)
        i = pl.program_id(1)

        @pl.when(i == 0)
        def _():
            acc_ref[...] = part

        @pl.when(i > 0)
        def _():
            acc_ref[...] += part

        @pl.when(i == pl.num_programs(1) - 1)
        def _():
            o_ref[0] = acc_ref[...].astype(BF16)

    return _pallas(
        body, name=name, grid=(nblk, m // tm),
        out_shape=jax.ShapeDtypeStruct((nblk, k, bw), BF16),
        in_specs=[pl.BlockSpec((tm, k), lambda j, i: (i, 0)), pl.BlockSpec((tm, bw), lambda j, i: (i, j)), _ANY],
        out_specs=pl.BlockSpec((1, k, bw), lambda j, i: (j, 0, 0)),
        scratch_shapes=[pltpu.VMEM((k, bw), F32)],
        compiler_params=_params(_mb(48)),
    )(a, b, after)


def _dw_rows(a, b, nblk, bw, tm, name):
    m = a.shape[0]
    nn = b.shape[1]

    def body(a_ref, b_ref, o_ref, acc_ref):
        part = lax.dot_general(a_ref[...].astype(BF16), b_ref[...], _TN, preferred_element_type=F32)
        i = pl.program_id(1)

        @pl.when(i == 0)
        def _():
            acc_ref[...] = part

        @pl.when(i > 0)
        def _():
            acc_ref[...] += part

        @pl.when(i == pl.num_programs(1) - 1)
        def _():
            o_ref[0] = acc_ref[...].astype(BF16)

    return _pallas(
        body, name=name, grid=(nblk, m // tm),
        out_shape=jax.ShapeDtypeStruct((nblk, bw, nn), BF16),
        in_specs=[pl.BlockSpec((tm, bw), lambda j, i: (i, j)), pl.BlockSpec((tm, nn), lambda j, i: (i, 0))],
        out_specs=pl.BlockSpec((1, bw, nn), lambda j, i: (j, 0, 0)),
        scratch_shapes=[pltpu.VMEM((bw, nn), F32)],
        compiler_params=_params(_mb(48)),
    )(a, b)


def _outproj_bwd(da1, wout, after):
    n, d = da1.shape
    tm = 512

    def body(a_ref, w_ref, after_ref, o_ref):
        o_ref[...] = lax.dot_general(a_ref[...], w_ref[...], _NT, preferred_element_type=F32)

    return _pallas(
        body, name="outproj_bwd", grid=(n // tm,),
        out_shape=jax.ShapeDtypeStruct((n, d), F32),
        in_specs=[pl.BlockSpec((tm, d), lambda i: (i, 0)), pl.BlockSpec((d, d), lambda i: (0, 0)), _ANY],
        out_specs=pl.BlockSpec((tm, d), lambda i: (i, 0)),
        compiler_params=_params(_mb(48)),
    )(da1, wout, after)


def _qkv_bwd(dh, win_g, x, ct, dr1, sc):
    na, wcols = dh.shape
    n, d = x.shape
    tm = CTX
    nlat = n // tm

    def body(dh_ref, w_ref, x_ref, ct_ref, dr_ref, sc_ref, gx_ref, acc_ref):
        i = pl.program_id(0)

        @pl.when(i == 0)
        def _():
            acc_ref[...] = jnp.zeros_like(acc_ref)

        du = jnp.zeros((tm, d), F32)
        for j in range(NDEV):
            du = du + lax.dot_general(dh_ref[:, j * IN_SHARD:(j + 1) * IN_SHARD], w_ref[j], _NT,
                                      preferred_element_type=F32)

        @pl.when(i < nlat)
        def _():
            gx_ref[...] = ALPHA * dr_ref[...] + du * (1.0 + sc_ref[0])
            acc_ref[0:1, :] += _colsum(du)
            acc_ref[1:2, :] += _colsum(du * x_ref[...])

        @pl.when(i == nlat)
        def _():
            acc_ref[2:3, :] += _colsum(du)
            acc_ref[3:4, :] += _colsum(du * ct_ref[...])

    lat = lambda i: (jnp.minimum(i, nlat - 1), 0)
    const2 = lambda i: (0, 0)
    return _pallas(
        body, name="qkv_bwd", grid=(nlat + 1,),
        out_shape=(jax.ShapeDtypeStruct((n, d), F32), jax.ShapeDtypeStruct((8, d), F32)),
        in_specs=[pl.BlockSpec((tm, wcols), lambda i: (i, 0)), pl.BlockSpec((NDEV, d, IN_SHARD), lambda i: (0, 0, 0)),
                  pl.BlockSpec((tm, d), lat), pl.BlockSpec((tm, d), const2), pl.BlockSpec((tm, d), lat),
                  pl.BlockSpec((1, 1, d), lambda i: (0, 0, 0))],
        out_specs=(pl.BlockSpec((tm, d), lat), pl.BlockSpec((8, d), const2)),
        compiler_params=_params(_mb(56)),
    )(dh, win_g, x, ct, dr1, sc)


def _adam_math(w, g, m, v):
    m2 = ADAM_B1 * m + (1.0 - ADAM_B1) * g
    v2 = ADAM_B2 * v + (1.0 - ADAM_B2) * (g * g)
    m_hat = m2 / (1.0 - ADAM_B1 ** ADAM_STEP)
    v_hat = v2 / (1.0 - ADAM_B2 ** ADAM_STEP)
    delta = -ADAM_LR * (m_hat / (jnp.sqrt(v_hat) + ADAM_EPS) + ADAM_WD * w)
    return delta, m2, v2


def _adamw(w, gsrc, m, v, name):
    r, c = w.shape
    parts = gsrc.ndim == 3
    tr = r
    while tr * c * 4 > _mb(1) and tr % 16 == 0:
        tr //= 2

    def body(w_ref, g_ref, m_ref, v_ref, go_ref, d_ref, mo_ref, vo_ref):
        if parts:
            g = g_ref[0].astype(F32)
            for s in range(1, NDEV):
                g = g + g_ref[s].astype(F32)
        else:
            g = g_ref[...]
        delta, m2, v2 = _adam_math(w_ref[...], g, m_ref[...], v_ref[...])
        go_ref[...] = g
        d_ref[...] = delta
        mo_ref[...] = m2
        vo_ref[...] = v2

    tile = pl.BlockSpec((tr, c), lambda i: (i, 0))
    gspec = pl.BlockSpec((NDEV, tr, c), lambda i: (0, i, 0)) if parts else tile
    sds = jax.ShapeDtypeStruct((r, c), F32)
    return _pallas(
        body, name=name, grid=(r // tr,),
        out_shape=(sds, sds, sds, sds),
        in_specs=[tile, gspec, tile, tile],
        out_specs=(tile, tile, tile, tile),
        compiler_params=_params(_mb(48)),
    )(w, gsrc, m, v)


def _small_update(gath, dcc, cc, w_s, m_s, v_s):
    d = w_s.shape[1]

    def body(g_ref, dcc_ref, cc_ref, w_ref, m_ref, v_ref, go_ref, d_ref, mo_ref, vo_ref):
        s = g_ref[0]
        for b in range(1, NDEV):
            s = s + g_ref[b]
        dsl = dcc_ref[0, 8:9, :]
        for b in range(1, NDEV):
            dsl = dsl + dcc_ref[b, 8:9, :]
        cv = cc_ref[...]
        sg = _sigmoid(cv)
        go_ref[...] = jnp.zeros_like(go_ref)
        go_ref[0:1, :] = dsl * (sg * (1.0 + cv * (1.0 - sg)))
        go_ref[1:3, :] = s[0:2] + s[6:8]
        go_ref[3:7, :] = s[2:6]
        go_ref[7:12, :] = s[8:13]
        delta, m2, v2 = _adam_math(w_ref[...], go_ref[...], m_ref[...], v_ref[...])
        d_ref[...] = delta
        mo_ref[...] = m2
        vo_ref[...] = v2

    full = pl.BlockSpec((16, d), lambda: (0, 0))
    g3 = pl.BlockSpec((NDEV, 16, d), lambda: (0, 0, 0))
    sds = jax.ShapeDtypeStruct((16, d), F32)
    return _pallas(
        body, name="small_update",
        out_shape=(sds, sds, sds, sds),
        in_specs=[g3, g3, pl.BlockSpec((1, d), lambda: (0, 0)), full, full, full],
        out_specs=(full, full, full, full),
        compiler_params=_params(_mb(24)),
    )(gath, dcc, cc, w_s, m_s, v_s)


def _rope_tables(n):
    rows = n // GRID_W
    row_ids = jnp.repeat(jnp.arange(rows, dtype=F32), GRID_W)
    col_ids = jnp.tile(jnp.arange(GRID_W, dtype=F32), rows)
    axis_dim = HEAD // 2
    inv_freq = jnp.power(ROPE_THETA, -jnp.arange(0, axis_dim, 2, dtype=F32) / axis_dim)
    ang_r = row_ids[:, None] * inv_freq
    ang_c = col_ids[:, None] * inv_freq
    ang = jnp.concatenate([ang_r, ang_r, ang_c, ang_c], axis=-1)
    cos, sin = jnp.cos(ang), jnp.sin(ang)
    first = (jnp.arange(HEAD) % (HEAD // 2)) < HEAD // 4
    sa = jnp.where(first, -sin, 0.0)
    sb = jnp.where(first, 0.0, sin)
    ones = jnp.ones((CTX, HEAD), F32)
    zeros = jnp.zeros((CTX, HEAD), F32)
    return (jnp.concatenate([cos, ones], 0), jnp.concatenate([sa, zeros], 0), jnp.concatenate([sb, zeros], 0))


def _pad_cols(a, width):
    return jnp.pad(a, ((0, 0), (0, width - a.shape[1])))


def _pad_rows(a, rows):
    return jnp.pad(a, ((0, rows - a.shape[0]), (0, 0)))


def _pack_small(c_ctx, b_ada, ln1_g, ln1_b, ln2_g, ln2_b, qg, kg, sink, d):
    misc = _pad_cols(jnp.concatenate([qg, kg, sink], axis=1), d)
    rows = jnp.concatenate([c_ctx.reshape(1, d), b_ada.reshape(6, d), ln1_g, ln1_b, ln2_g, ln2_b, misc], axis=0)
    return _pad_rows(rows, 16)


def _unpack_small(p, d):
    return dict(c_ctx=p[0], b_ada=p[1:7].reshape(1, 6 * d), ln1_g=p[7:8], ln1_b=p[8:9], ln2_g=p[9:10], ln2_b=p[10:11],
                q_norm_g=p[11:12, 0:HEAD], k_norm_g=p[11:12, HEAD:2 * HEAD], sink_logit=p[11:12, 2 * HEAD:2 * HEAD + 8])


def kernel(x, c, ctx, c_ctx, w_ada, b_ada, w_in, q_norm_g, k_norm_g, sink_logit, w_out, ln1_g, ln1_b, w_gate, w_up, w_down, ln2_g, ln2_b, loss_target, m_c_ctx, m_w_ada, m_b_ada, m_w_in, m_q_norm_g, m_k_norm_g, m_sink_logit, m_w_out, m_ln1_g, m_ln1_b, m_w_gate, m_w_up, m_w_down, m_ln2_g, m_ln2_b, v_c_ctx, v_w_ada, v_b_ada, v_w_in, v_q_norm_g, v_k_norm_g, v_sink_logit, v_w_out, v_ln1_g, v_ln1_b, v_w_gate, v_w_up, v_w_down, v_ln2_g, v_ln2_b):
    xs, cts, tgt = x[0], ctx[0], loss_target[0]
    n, d = xs.shape
    assert cts.shape == (CTX, d) and w_in.shape[2] == IN_SHARD and w_gate.shape[2] == FFN_SHARD
    me = 4 * lax.axis_index("x") + 2 * lax.axis_index("y") + lax.axis_index("c")
    e_sh = w_ada.shape[2]

    c_g = _exchange(_pad_rows(c, 8), False, "gather_c")
    c_all = jnp.concatenate([c_g[:, 0, :], _pad_rows(c_ctx.reshape(1, d), 8)], axis=0)
    bias_sh = lax.dynamic_slice(b_ada, (0, me * e_sh), (1, e_sh))
    mods_g = _exchange(_ada_fwd(c_all, w_ada[0], bias_sh), False, "gather_mods")
    mods = jnp.transpose(mods_g, (1, 0, 2)).reshape(16, NDEV * e_sh)
    mine = lax.dynamic_slice(mods, (me, 0), (1, 6 * d))
    sh1, sc1, g1, sh2, sc2, g2 = [mine[:, k * d:(k + 1) * d] for k in range(6)]
    csh1, csc1 = mods[8:9, 0:d], mods[8:9, d:2 * d]
    sc_pair = jnp.stack([sc1, csc1])
    sh_pair = jnp.stack([sh1, csh1])

    h_win, tok = _exchange_start(w_in[0].astype(BF16), False, mods, "gather_w_in_start")
    h_wout, tok = _exchange_start(w_out[0].astype(BF16), False, tok, "gather_w_out_start")
    h_wg, tok = _exchange_start(_pad_cols(w_gate[0], FFN_PAD).astype(BF16), False, tok, "gather_w_gate_start")
    h_wu, tok = _exchange_start(_pad_cols(w_up[0], FFN_PAD).astype(BF16), False, tok, "gather_w_up_start")
    h_wd, tok = _exchange_start(_pad_rows(w_down[0], FFN_PAD).astype(BF16), False, tok, "gather_w_down_start")

    cos, sa, sb = _rope_tables(n)
    win_g = _exchange_wait(h_win, False, tok, "gather_w_in_wait")
    u_all, h_all, t_all = _qkv_fwd(xs, cts, sc_pair, sh_pair, win_g, q_norm_g, k_norm_g, cos, sa, sb)
    o_a, lse_a = _attn_fwd(t_all, sink_logit, True, 0, 8, 10, 0, WINDOW, "attn_window_fwd")
    o_b, lse_b = _attn_fwd(t_all, sink_logit, False, 3, 20, 22, 2, 256, "attn_global_fwd")
    o = jnp.concatenate([o_a, o_b], axis=1)
    wout_g = _exchange_wait(h_wout, False, o, "gather_w_out_wait").reshape(d, d)
    a1, xh1, rs1 = _outproj_ln1(o, wout_g, xs, g1)
    wg_g = _exchange_wait(h_wg, False, rs1, "gather_w_gate_wait")
    wu_g = _exchange_wait(h_wu, False, rs1, "gather_w_up_wait")
    u2, gmat, pmat, hf = _ffn_up(xh1, ln1_g, ln1_b, sc2, sh2, wg_g, wu_g)
    wd_g = _exchange_wait(h_wd, False, u2, "gather_w_down_wait")
    ffn = _ffn_down(hf, wd_g.reshape(NDEV * FFN_PAD, d))
    dr2, loss_p, acc2 = _ln2_loss(xh1, ffn, tgt, ln1_g, ln1_b, g2, ln2_g, ln2_b)
    loss = lax.psum(loss_p[0, 0], ("x", "y", "c"))

    df, dgm, dpm = _ffn_dhf(dr2, g2, wd_g, gmat, pmat)
    dwd_p = _dw_rows(hf, df, NDEV, FFN_PAD, 512, "dw_down")
    h_dwd, tok = _exchange_start(dwd_p, True, loss_p, "scatter_dw_down_start")
    dwg_p = _dw_cols(u2, dgm, NDEV, FFN_PAD, 512, tok, "dw_gate")
    h_dwg, tok = _exchange_start(dwg_p, True, tok, "scatter_dw_gate_start")
    dwu_p = _dw_cols(u2, dpm, NDEV, FFN_PAD, 512, tok, "dw_up")
    h_dwu, tok = _exchange_start(dwu_p, True, tok, "scatter_dw_up_start")
    du2 = _ffn_du2(dgm, dpm, wg_g, wu_g, tok)
    dr1, da1, acc1 = _ln1_bwd(du2, dr2, xh1, rs1, a1, ln1_g, ln1_b, sc2, g1)
    dwo_p = _dw_rows(o, da1, NDEV, 2 * HEAD, 512, "dw_out")
    h_dwo, tok = _exchange_start(dwo_p, True, loss_p, "scatter_dw_out_start")
    do = _outproj_bwd(da1, wout_g, tok)
    dqa, dka, dva, dsink = _attn_bwd(t_all, o, do, lse_a, sink_logit, True, 0, 8, 10, 0, WINDOW, "attn_window_bwd")
    dqb, dkb, dvb, _ = _attn_bwd(t_all, o, do, lse_b, sink_logit, False, 3, 20, 22, 2, 256, "attn_global_bwd")
    dh_all, dnorm = _qkv_bwd_prep(dqa, dka, dva, dqb, dkb, dvb, h_all, q_norm_g, k_norm_g, cos, sa, sb)
    grad_x, acc0 = _qkv_bwd(dh_all, win_g, xs, cts, dr1, sc_pair)

    misc = _pad_cols(jnp.concatenate([dnorm[0:1], dnorm[1:2], dsink[:, 0:4, 0].reshape(1, 8)], axis=1), d)
    part = jnp.concatenate([
        acc0[0:2], acc1[4:5], acc1[1:2], acc1[0:1], acc2[2:3],
        acc0[2:4],
        acc1[2:4], acc2[0:2],
        misc, jnp.zeros((3, d), F32)], axis=0)
    gath = _exchange(part, False, "gather_small")
    dm_batch = gath[:, 0:6, :].reshape(NDEV, 6 * d)
    dm_ctx = _pad_cols(gath[:, 6:8, :].reshape(NDEV, 2 * d), 6 * d)
    dm16 = lax.dynamic_slice(jnp.concatenate([dm_batch, dm_ctx], axis=0), (0, me * e_sh), (16, e_sh))
    dw_ada, drow = _ada_bwd(dm16, c_all, w_ada[0])
    dcc = _exchange(drow, False, "gather_dcc")
    dwi_p = _dw_cols(u_all, dh_all, NDEV, IN_SHARD, CTX, dcc, "dw_in")
    h_dwi, tok = _exchange_start(dwi_p, True, dcc, "scatter_dw_in_start")

    w_s = _pack_small(c_ctx, b_ada, ln1_g, ln1_b, ln2_g, ln2_b, q_norm_g, k_norm_g, sink_logit, d)
    m_s = _pack_small(m_c_ctx, m_b_ada, m_ln1_g, m_ln1_b, m_ln2_g, m_ln2_b, m_q_norm_g, m_k_norm_g, m_sink_logit, d)
    v_s = _pack_small(v_c_ctx, v_b_ada, v_ln1_g, v_ln1_b, v_ln2_g, v_ln2_b, v_q_norm_g, v_k_norm_g, v_sink_logit, d)
    small = [_unpack_small(p, d) for p in _small_update(gath, dcc, c_ctx.reshape(1, d), w_s, m_s, v_s)]

    big = {}
    big["w_ada"] = _adamw(w_ada[0], dw_ada, m_w_ada[0], v_w_ada[0], "adamw_w_ada")
    late = tok
    res = _adamw(_pad_rows(w_down[0], FFN_PAD), _exchange_wait(h_dwd, True, late, "scatter_dw_down_wait"),
                 _pad_rows(m_w_down[0], FFN_PAD), _pad_rows(v_w_down[0], FFN_PAD), "adamw_w_down")
    big["w_down"] = [r[:FFN_SHARD] for r in res]
    for nm, wt, mt, vt, hd in (("w_gate", w_gate, m_w_gate, v_w_gate, h_dwg), ("w_up", w_up, m_w_up, v_w_up, h_dwu)):
        res = _adamw(_pad_cols(wt[0], FFN_PAD), _exchange_wait(hd, True, late, "scatter_d" + nm + "_wait"),
                     _pad_cols(mt[0], FFN_PAD), _pad_cols(vt[0], FFN_PAD), "adamw_" + nm)
        big[nm] = [r[:, :FFN_SHARD] for r in res]
    big["w_out"] = _adamw(w_out[0], _exchange_wait(h_dwo, True, late, "scatter_dw_out_wait"), m_w_out[0], v_w_out[0],
                          "adamw_w_out")
    big["w_in"] = _adamw(w_in[0], _exchange_wait(h_dwi, True, big["w_out"][1], "scatter_dw_in_wait"), m_w_in[0],
                         v_w_in[0], "adamw_w_in")

    names = ["c_ctx", "w_ada", "b_ada", "w_in", "q_norm_g", "k_norm_g", "sink_logit", "w_out", "ln1_g", "ln1_b",
             "w_gate", "w_up", "w_down", "ln2_g", "ln2_b"]
    outs = [loss, grad_x[None]]
    for k in range(4):
        for nm in names:
            outs.append(big[nm][k][None] if nm in big else small[k][nm])
    return tuple(outs)
```

```python
import functools

import jax
import jax.numpy as jnp
from jax import lax
from jax.experimental import pallas as pl
from jax.experimental.pallas import tpu as pltpu

F32 = jnp.float32
BF16 = jnp.bfloat16

NDEV = 8
HEAD = 128
CTX = 256
GRID_W = 64
WINDOW = 128
ROPE_THETA = 10000.0
EPS = 1e-6
SCALE = HEAD ** -0.5
ALPHA = 2.0 ** 0.25
FFN_SHARD = 704
FFN_PAD = 768
IN_SHARD = 384
NEG = -1e30

ADAM_LR = 0.001
ADAM_B1 = 0.9
ADAM_B2 = 0.999
ADAM_EPS = 1e-08
ADAM_WD = 0.01
ADAM_STEP = 10

VMEM_CAP = 56 * 1024 * 1024

_KINDS = ["rope"] * 10 + ["none"] * 2 + ["qnorm"] * 8 + ["knorm"] * 2 + ["none"] * 2

_NT = (((1,), (1,)), ((), ()))
_TN = (((0,), (0,)), ((), ()))


def _pallas(body, **kw):
    return pl.pallas_call(body, **kw)


def _params(vmem_bytes):
    return pltpu.CompilerParams(vmem_limit_bytes=int(min(VMEM_CAP, vmem_bytes)))


def _mb(n):
    return int(n * 1024 * 1024)


def _sigmoid(x):
    return 1.0 / (1.0 + jnp.exp(-x))


def _colsum(a):
    return jnp.sum(a, axis=0, keepdims=True)


def _rowmean(a):
    return jnp.mean(a, axis=-1, keepdims=True)


def _exchange(src, scatter, name, after=None):
    blk = src.shape[1:] if scatter else src.shape
    after = src if after is None else after

    def body(src_ref, after_ref, out_ref, send_sems, recv_sems, local_sem):
        x, y, c = lax.axis_index("x"), lax.axis_index("y"), lax.axis_index("c")
        me = 4 * x + 2 * y + c
        copies = []
        for t in range(1, NDEV):
            px = 1 - x if (t >> 2) & 1 else x
            py = 1 - y if (t >> 1) & 1 else y
            pc = 1 - c if t & 1 else c
            peer = 4 * px + 2 * py + pc
            cp = pltpu.make_async_remote_copy(
                src_ref=src_ref.at[peer] if scatter else src_ref,
                dst_ref=out_ref.at[me],
                send_sem=send_sems.at[t - 1],
                recv_sem=recv_sems.at[t - 1],
                device_id=(px, py, pc),
                device_id_type=pl.DeviceIdType.MESH,
            )
            cp.start()
            copies.append(cp)
        own = pltpu.make_async_copy(src_ref.at[me] if scatter else src_ref, out_ref.at[me], local_sem)
        own.start()
        for cp in copies:
            cp.wait()
        own.wait()

    return _pallas(
        body, name=name,
        out_shape=jax.ShapeDtypeStruct((NDEV,) + tuple(blk), src.dtype),
        in_specs=[pl.BlockSpec(memory_space=pl.ANY), pl.BlockSpec(memory_space=pl.ANY)],
        out_specs=pl.BlockSpec(memory_space=pl.ANY),
        scratch_shapes=[pltpu.SemaphoreType.DMA((NDEV - 1,)), pltpu.SemaphoreType.DMA((NDEV - 1,)),
                        pltpu.SemaphoreType.DMA(())],
    )(src, after)


_HBM = pl.BlockSpec(memory_space=pltpu.HBM)
_SEM = pl.BlockSpec(memory_space=pltpu.SEMAPHORE)
_ANY = pl.BlockSpec(memory_space=pl.ANY)
_EFFECT = pltpu.SideEffectType.DATAFLOW_SIDE_EFFECTING


def _exchange_copies(src_ref, land_ref, send_sems, recv_sems, scatter):
    x, y, c = lax.axis_index("x"), lax.axis_index("y"), lax.axis_index("c")
    me = 4 * x + 2 * y + c
    copies = []
    for t in range(1, NDEV):
        px = 1 - x if (t >> 2) & 1 else x
        py = 1 - y if (t >> 1) & 1 else y
        pc = 1 - c if t & 1 else c
        peer = 4 * px + 2 * py + pc
        copies.append(pltpu.make_async_remote_copy(
            src_ref=src_ref.at[peer] if scatter else src_ref,
            dst_ref=land_ref.at[me],
            send_sem=send_sems.at[t - 1],
            recv_sem=recv_sems.at[t - 1],
            device_id=(px, py, pc),
            device_id_type=pl.DeviceIdType.MESH,
        ))
    own = pltpu.make_async_copy(src_ref.at[me] if scatter else src_ref, land_ref.at[me], send_sems.at[NDEV - 1])
    return copies, own


def _exchange_start(src, scatter, after, name):
    blk = src.shape[1:] if scatter else src.shape
    land = lax.empty((NDEV,) + tuple(blk), src.dtype)

    def body(src_ref, land_ref, after_ref, send_sems, recv_sems, src_thru, land_thru, token):
        copies, own = _exchange_copies(src_ref, land_ref, send_sems, recv_sems, scatter)
        for cp in copies:
            cp.start()
        own.start()
        token[...] = jnp.zeros_like(token)

    res = _pallas(
        body, name=name,
        out_shape=(pltpu.SemaphoreType.DMA((NDEV,)), pltpu.SemaphoreType.DMA((NDEV,)),
                   pltpu.HBM(src.shape, src.dtype), pltpu.HBM(land.shape, land.dtype),
                   jax.ShapeDtypeStruct((8, HEAD), F32)),
        in_specs=(_HBM, _HBM, _ANY), out_specs=(_SEM, _SEM, _HBM, _HBM, pl.BlockSpec(memory_space=pltpu.VMEM)),
        input_output_aliases={0: 2, 1: 3},
        compiler_params=pltpu.CompilerParams(has_side_effects=_EFFECT),
    )(pltpu.with_memory_space_constraint(src, pltpu.HBM), pltpu.with_memory_space_constraint(land, pltpu.HBM), after)
    return res[:4], res[4]


def _exchange_wait(handle, scatter, after, name):
    send_sems, recv_sems, src_thru, land_thru = handle

    def body(src_ref, land_ref, send_sems, recv_sems, after_ref, src_dead, got_ref):
        copies, own = _exchange_copies(src_ref, land_ref, send_sems, recv_sems, scatter)
        for cp in copies:
            cp.wait_send()
            cp.wait_recv()
        own.wait()

    return _pallas(
        body, name=name,
        out_shape=(pltpu.HBM(src_thru.shape, src_thru.dtype), pltpu.HBM(land_thru.shape, land_thru.dtype)),
        in_specs=(_HBM, _HBM, _SEM, _SEM, _ANY), out_specs=(_HBM, _HBM),
        input_output_aliases={0: 0, 1: 1},
        compiler_params=pltpu.CompilerParams(has_side_effects=_EFFECT),
    )(src_thru, land_thru, send_sems, recv_sems, after)[1]


def _ada_fwd(c_all, w, bias):
    r, d = c_all.shape
    e = w.shape[1]
    tn = 512

    def body(c_ref, w_ref, b_ref, o_ref):
        cv = c_ref[...]
        s = (cv * _sigmoid(cv)).astype(BF16)
        o_ref[...] = jnp.dot(s, w_ref[...].astype(BF16), preferred_element_type=F32) + b_ref[...]

    return _pallas(
        body, name="ada_fwd", grid=(e // tn,),
        out_shape=jax.ShapeDtypeStruct((r, e), F32),
        in_specs=[pl.BlockSpec((r, d), lambda j: (0, 0)), pl.BlockSpec((d, tn), lambda j: (0, j)),
                  pl.BlockSpec((1, tn), lambda j: (0, j))],
        out_specs=pl.BlockSpec((r, tn), lambda j: (0, j)),
        compiler_params=_params(_mb(24)),
    )(c_all, w, bias)


def _ada_bwd(dm16, c_all, w):
    d, e = w.shape
    tn = 512

    def body(dm_ref, c_ref, w_ref, dw_ref, dr_ref):
        j = pl.program_id(0)
        dm = dm_ref[...]
        rid = lax.broadcasted_iota(jnp.int32, dm.shape, 0)
        ctx_sum = jnp.sum(jnp.where(rid >= 8, dm, 0.0), axis=0, keepdims=True)
        rows = jnp.where(rid < 8, dm, jnp.where(rid == 8, jnp.broadcast_to(ctx_sum, dm.shape), 0.0)).astype(BF16)
        cv = c_ref[...]
        s = (cv * _sigmoid(cv)).astype(BF16)
        dw_ref[...] = lax.dot_general(s, rows, _TN, preferred_element_type=F32)
        part = lax.dot_general(rows, w_ref[...].astype(BF16), _NT, preferred_element_type=F32)

        @pl.when(j == 0)
        def _():
            dr_ref[...] = part

        @pl.when(j > 0)
        def _():
            dr_ref[...] += part

    return _pallas(
        body, name="ada_bwd", grid=(e // tn,),
        out_shape=(jax.ShapeDtypeStruct((d, e), F32), jax.ShapeDtypeStruct((16, d), F32)),
        in_specs=[pl.BlockSpec((16, tn), lambda j: (0, j)), pl.BlockSpec((16, d), lambda j: (0, 0)),
                  pl.BlockSpec((d, tn), lambda j: (0, j))],
        out_specs=(pl.BlockSpec((d, tn), lambda j: (0, j)), pl.BlockSpec((16, d), lambda j: (0, 0))),
        compiler_params=_params(_mb(32)),
    )(dm16, c_all, w)


def _rope(v, cos, sa, sb):
    return v * cos + (pltpu.roll(v, 96, 1) * sa + pltpu.roll(v, 32, 1) * sb)


def _rope_t(dt, cos, sa, sb):
    return dt * cos + (pltpu.roll(dt * sa, 32, 1) + pltpu.roll(dt * sb, 96, 1))


def _qkv_fwd(x, ct, sc, sh, win_g, qg, kg, cos, sa, sb):
    n, d = x.shape
    tm = CTX
    nlat = n // tm
    na = n + CTX
    wcols = NDEV * IN_SHARD

    def body(x_ref, ct_ref, sc_ref, sh_ref, w_ref, qg_ref, kg_ref, cos_ref, sa_ref, sb_ref, u_ref, h_ref, t_ref):
        i = pl.program_id(0)
        xin = jnp.where(i == nlat, ct_ref[...], x_ref[...])
        u = (xin * (1.0 + sc_ref[0]) + sh_ref[0]).astype(BF16)
        u_ref[...] = u
        cos, sa, sb = cos_ref[...], sa_ref[...], sb_ref[...]
        for j in range(NDEV):
            h = jnp.dot(u, w_ref[j], preferred_element_type=F32)
            h_ref[:, j * IN_SHARD:(j + 1) * IN_SHARD] = h
            for hh in range(3):
                hd = 3 * j + hh
                v = h[:, hh * HEAD:(hh + 1) * HEAD]
                kind = _KINDS[hd]
                if kind == "qnorm":
                    v = v * lax.rsqrt(_rowmean(v * v) + EPS) * qg_ref[...]
                elif kind == "knorm":
                    v = v * lax.rsqrt(_rowmean(v * v) + EPS) * kg_ref[...]
                if kind != "none":
                    v = _rope(v, cos, sa, sb)
                t_ref[:, hd * HEAD:(hd + 1) * HEAD] = v.astype(BF16)

    lat = lambda i: (jnp.minimum(i, nlat - 1), 0)
    row = lambda i: (i, 0)
    const2 = lambda i: (0, 0)
    return _pallas(
        body, name="qkv_fwd", grid=(nlat + 1,),
        out_shape=(jax.ShapeDtypeStruct((na, d), BF16), jax.ShapeDtypeStruct((na, wcols), F32),
                   jax.ShapeDtypeStruct((na, wcols), BF16)),
        in_specs=[pl.BlockSpec((tm, d), lat), pl.BlockSpec((tm, d), const2),
                  pl.BlockSpec((1, 1, d), lambda i: (i // nlat, 0, 0)),
                  pl.BlockSpec((1, 1, d), lambda i: (i // nlat, 0, 0)),
                  pl.BlockSpec((NDEV, d, IN_SHARD), lambda i: (0, 0, 0)),
                  pl.BlockSpec((1, HEAD), const2), pl.BlockSpec((1, HEAD), const2),
                  pl.BlockSpec((tm, HEAD), row), pl.BlockSpec((tm, HEAD), row), pl.BlockSpec((tm, HEAD), row)],
        out_specs=(pl.BlockSpec((tm, d), row), pl.BlockSpec((tm, wcols), row), pl.BlockSpec((tm, wcols), row)),
        compiler_params=_params(_mb(56)),
    )(x, ct, sc, sh, win_g, qg, kg, cos, sa, sb)


def _qkv_bwd_prep(dqa, dka, dva, dqb, dkb, dvb, h_all, qg, kg, cos, sa, sb):
    na, wcols = h_all.shape
    n = na - CTX
    tm = CTX
    nlat = n // tm

    def body(dqa_ref, dka_ref, dva_ref, dqb_ref, dkb_ref, dvb_ref, h_ref, qg_ref, kg_ref, cos_ref, sa_ref, sb_ref,
             dh_ref, dg_ref):
        i = pl.program_id(0)

        @pl.when(i == 0)
        def _():
            dg_ref[...] = jnp.zeros_like(dg_ref)

        cos, sa, sb = cos_ref[...], sa_ref[...], sb_ref[...]
        is_lat = i < nlat
        for hd in range(24):
            kind = _KINDS[hd]
            if hd < 8:
                dt = jnp.where(is_lat, dqa_ref[:, hd * HEAD:(hd + 1) * HEAD], 0.0)
            elif hd < 10:
                dt = dka_ref[:, (hd - 8) * HEAD:(hd - 7) * HEAD]
            elif hd < 12:
                dt = dva_ref[:, (hd - 10) * HEAD:(hd - 9) * HEAD]
            elif hd < 20:
                dt = jnp.where(is_lat, dqb_ref[:, (hd - 12) * HEAD:(hd - 11) * HEAD], 0.0)
            elif hd < 22:
                dt = dkb_ref[:, (hd - 20) * HEAD:(hd - 19) * HEAD]
            else:
                dt = dvb_ref[:, (hd - 22) * HEAD:(hd - 21) * HEAD]
            if kind != "none":
                dt = _rope_t(dt, cos, sa, sb)
            if kind in ("qnorm", "knorm"):
                g_ref = qg_ref if kind == "qnorm" else kg_ref
                r0 = 0 if kind == "qnorm" else 1
                xv = h_ref[:, hd * HEAD:(hd + 1) * HEAD]
                xn = xv * lax.rsqrt(_rowmean(xv * xv) + EPS)
                dg_ref[r0:r0 + 1, :] += _colsum(dt * xn)
                dxn = dt * g_ref[...]
                dt = lax.rsqrt(_rowmean(xv * xv) + EPS) * (dxn - xn * _rowmean(dxn * xn))
            dh_ref[:, hd * HEAD:(hd + 1) * HEAD] = dt.astype(BF16)

    lat = lambda i: (jnp.minimum(i, nlat - 1), 0)
    row = lambda i: (i, 0)
    const2 = lambda i: (0, 0)
    return _pallas(
        body, name="qkv_bwd_prep", grid=(nlat + 1,),
        out_shape=(jax.ShapeDtypeStruct((na, wcols), BF16), jax.ShapeDtypeStruct((8, HEAD), F32)),
        in_specs=[pl.BlockSpec((tm, 8 * HEAD), lat), pl.BlockSpec((tm, 2 * HEAD), row), pl.BlockSpec((tm, 2 * HEAD), row),
                  pl.BlockSpec((tm, 8 * HEAD), lat), pl.BlockSpec((tm, 2 * HEAD), row), pl.BlockSpec((tm, 2 * HEAD), row),
                  pl.BlockSpec((tm, wcols), row),
                  pl.BlockSpec((1, HEAD), const2), pl.BlockSpec((1, HEAD), const2),
                  pl.BlockSpec((tm, HEAD), row), pl.BlockSpec((tm, HEAD), row), pl.BlockSpec((tm, HEAD), row)],
        out_specs=(pl.BlockSpec((tm, wcols), row), pl.BlockSpec((8, HEAD), const2)),
        compiler_params=_params(_mb(40)),
    )(dqa, dka, dva, dqb, dkb, dvb, h_all, qg, kg, cos, sa, sb)


def _attn_keys(window, k_ref, v_ref, n, na, tq):
    i = pl.program_id(1)
    if not window:
        return k_ref[...], v_ref[...], None, None
    start = pl.multiple_of(jnp.clip((i - 1) * tq, 0, n - 3 * tq), tq)
    kk = jnp.concatenate([k_ref[pl.ds(start, 3 * tq), :], k_ref[n:na, :]], axis=0)
    vv = jnp.concatenate([v_ref[pl.ds(start, 3 * tq), :], v_ref[n:na, :]], axis=0)
    nk = 3 * tq + CTX
    col = lax.broadcasted_iota(jnp.int32, (tq, nk), 1)
    rowi = lax.broadcasted_iota(jnp.int32, (tq, nk), 0)
    valid = (jnp.abs(i * tq + rowi - (start + col)) <= WINDOW) | (col >= 3 * tq)
    return kk, vv, valid, start


def _attn_fwd(t_all, sink, window, qblk, kblk, vblk, oblk, tq, name):
    na = t_all.shape[0]
    n = na - CTX

    def body(sink_ref, q_ref, k_ref, v_ref, o_ref, lse_ref):
        kv = pl.program_id(0)
        kk, vv, valid, _ = _attn_keys(window, k_ref, v_ref, n, na, tq)
        for g in range(4):
            q = q_ref[:, g * HEAD:(g + 1) * HEAD]
            s = lax.dot_general(q, kk, _NT, preferred_element_type=F32) * SCALE
            if window:
                s = jnp.where(valid, s, NEG)
            m = jnp.max(s, axis=-1, keepdims=True)
            if window:
                sk = sink_ref[0, 4 * kv + g]
                m = jnp.maximum(m, sk)
            p = jnp.exp(s - m)
            l = jnp.sum(p, axis=-1, keepdims=True)
            if window:
                l = l + jnp.exp(sk - m)
            o = jnp.dot((p / l).astype(BF16), vv, preferred_element_type=F32)
            o_ref[:, g * HEAD:(g + 1) * HEAD] = o
            lse_ref[:, g * HEAD:(g + 1) * HEAD] = jnp.broadcast_to(m + jnp.log(l), (tq, HEAD))

    qspec = pl.BlockSpec((tq, 4 * HEAD), lambda kv, i: (i, qblk + kv))
    ospec = pl.BlockSpec((tq, 4 * HEAD), lambda kv, i: (i, kv))
    return _pallas(
        body, name=name, grid=(2, n // tq),
        out_shape=(jax.ShapeDtypeStruct((n, 8 * HEAD), F32), jax.ShapeDtypeStruct((n, 8 * HEAD), F32)),
        in_specs=[pl.BlockSpec(memory_space=pltpu.SMEM), qspec,
                  pl.BlockSpec((na, HEAD), lambda kv, i: (0, kblk + kv)),
                  pl.BlockSpec((na, HEAD), lambda kv, i: (0, vblk + kv))],
        out_specs=(ospec, ospec),
        compiler_params=_params(_mb(48)),
    )(sink, t_all, t_all, t_all)


def _attn_bwd(t_all, o, do, lse, sink, window, qblk, kblk, vblk, oblk, tq, name):
    na = t_all.shape[0]
    n = na - CTX

    def body(sink_ref, q_ref, k_ref, v_ref, o_ref, do_ref, lse_ref, dq_ref, dk_ref, dv_ref, dsink_ref):
        kv = pl.program_id(0)
        i = pl.program_id(1)

        @pl.when(i == 0)
        def _():
            dk_ref[...] = jnp.zeros_like(dk_ref)
            dv_ref[...] = jnp.zeros_like(dv_ref)
            dsink_ref[...] = jnp.zeros_like(dsink_ref)

        kk, vv, valid, start = _attn_keys(window, k_ref, v_ref, n, na, tq)
        dk_acc = jnp.zeros((kk.shape[0], HEAD), F32)
        dv_acc = jnp.zeros((kk.shape[0], HEAD), F32)
        for g in range(4):
            q = q_ref[:, g * HEAD:(g + 1) * HEAD]
            s = lax.dot_general(q, kk, _NT, preferred_element_type=F32) * SCALE
            if window:
                s = jnp.where(valid, s, NEG)
            lse_g = lse_ref[:, g * HEAD:g * HEAD + 1]
            p = jnp.exp(s - lse_g)
            dof = do_ref[:, g * HEAD:(g + 1) * HEAD]
            delta = jnp.sum(dof * o_ref[:, g * HEAD:(g + 1) * HEAD], axis=-1, keepdims=True)
            dob = dof.astype(BF16)
            dv_acc = dv_acc + lax.dot_general(p.astype(BF16), dob, _TN, preferred_element_type=F32)
            dp = lax.dot_general(dob, vv, _NT, preferred_element_type=F32)
            ds = (p * (dp - delta) * SCALE).astype(BF16)
            dq_ref[:, g * HEAD:(g + 1) * HEAD] = jnp.dot(ds, kk, preferred_element_type=F32)
            dk_acc = dk_acc + lax.dot_general(ds, q, _TN, preferred_element_type=F32)
            if window:
                p_sink = jnp.exp(sink_ref[0, 4 * kv + g] - lse_g)
                dsink_ref[0, g:g + 1, :] += jnp.broadcast_to(-jnp.sum(p_sink * delta, axis=0, keepdims=True), (1, HEAD))
        if window:
            dk_ref[pl.ds(start, 3 * tq), :] += dk_acc[:3 * tq]
            dv_ref[pl.ds(start, 3 * tq), :] += dv_acc[:3 * tq]
            dk_ref[n:na, :] += dk_acc[3 * tq:]
            dv_ref[n:na, :] += dv_acc[3 * tq:]
        else:
            dk_ref[...] += dk_acc
            dv_ref[...] += dv_acc

    qspec = pl.BlockSpec((tq, 4 * HEAD), lambda kv, i: (i, qblk + kv))
    ospec = pl.BlockSpec((tq, 4 * HEAD), lambda kv, i: (i, oblk + kv))
    lspec = pl.BlockSpec((tq, 4 * HEAD), lambda kv, i: (i, kv))
    kvout = pl.BlockSpec((na, HEAD), lambda kv, i: (0, kv))
    return _pallas(
        body, name=name, grid=(2, n // tq),
        out_shape=(jax.ShapeDtypeStruct((n, 8 * HEAD), F32), jax.ShapeDtypeStruct((na, 2 * HEAD), F32),
                   jax.ShapeDtypeStruct((na, 2 * HEAD), F32), jax.ShapeDtypeStruct((2, 8, HEAD), F32)),
        in_specs=[pl.BlockSpec(memory_space=pltpu.SMEM), qspec,
                  pl.BlockSpec((na, HEAD), lambda kv, i: (0, kblk + kv)),
                  pl.BlockSpec((na, HEAD), lambda kv, i: (0, vblk + kv)),
                  ospec, ospec, lspec],
        out_specs=(lspec, kvout, kvout, pl.BlockSpec((1, 8, HEAD), lambda kv, i: (kv, 0, 0))),
        compiler_params=_params(_mb(56)),
    )(sink, t_all, t_all, t_all, o, do, lse)


def _outproj_ln1(o, wout, x, g1, lg, lb, sc2, sh2):
    n, d = x.shape
    tm = 256

    def body(o_ref, w_ref, x_ref, g1_ref, lg_ref, lb_ref, sc_ref, sh_ref, a_ref, xh_ref, rs_ref, u_ref):
        a1 = jnp.dot(o_ref[...].astype(BF16), w_ref[...], preferred_element_type=F32)
        a_ref[...] = a1
        r = ALPHA * x_ref[...] + g1_ref[...] * a1
        dlt = r - _rowmean(r)
        rstd = lax.rsqrt(_rowmean(dlt * dlt) + EPS)
        xh = dlt * rstd
        xh_ref[...] = xh
        rs_ref[...] = rstd
        x1 = xh * lg_ref[...] + lb_ref[...]
        u_ref[...] = (x1 * (1.0 + sc_ref[...]) + sh_ref[...]).astype(BF16)

    row = lambda i: (i, 0)
    const2 = lambda i: (0, 0)
    vec = pl.BlockSpec((1, d), const2)
    big = pl.BlockSpec((tm, d), row)
    return _pallas(
        body, name="outproj_ln1", grid=(n // tm,),
        out_shape=(jax.ShapeDtypeStruct((n, d), F32), jax.ShapeDtypeStruct((n, d), F32),
                   jax.ShapeDtypeStruct((n, 1), F32), jax.ShapeDtypeStruct((n, d), BF16)),
        in_specs=[big, pl.BlockSpec((d, d), const2), big, vec, vec, vec, vec, vec],
        out_specs=(big, big, pl.BlockSpec((tm, 1), row), big),
        compiler_params=_params(_mb(56)),
    )(o, wout, x, g1, lg, lb, sc2, sh2)


def _ffn_up(u2, wg_g, wu_g):
    n, d = u2.shape
    tm = min(1024, n)
    f = NDEV * FFN_PAD

    def body(u_ref, wg_ref, wu_ref, g_ref, p_ref, hf_ref):
        u = u_ref[...]
        gv = jnp.dot(u, wg_ref[0], preferred_element_type=F32)
        pv = jnp.dot(u, wu_ref[0], preferred_element_type=F32)
        g_ref[...] = gv.astype(BF16)
        p_ref[...] = pv.astype(BF16)
        hf_ref[...] = (gv * _sigmoid(gv) * pv).astype(BF16)

    tile = pl.BlockSpec((tm, FFN_PAD), lambda i, j: (i, j))
    wspec = pl.BlockSpec((1, d, FFN_PAD), lambda i, j: (j, 0, 0))
    sds = jax.ShapeDtypeStruct((n, f), BF16)
    return _pallas(
        body, name="ffn_up", grid=(n // tm, NDEV),
        out_shape=(sds, sds, sds),
        in_specs=[pl.BlockSpec((tm, d), lambda i, j: (i, 0)), wspec, wspec],
        out_specs=(tile, tile, tile),
        compiler_params=_params(_mb(48)),
    )(u2, wg_g, wu_g)


def _ffn_down(hf, wd):
    n, f = hf.shape
    d = wd.shape[1]
    tm, tn = min(1024, n), 512

    def body(h_ref, w_ref, o_ref):
        o_ref[...] = jnp.dot(h_ref[...], w_ref[...], preferred_element_type=F32)

    return _pallas(
        body, name="ffn_down", grid=(n // tm, d // tn),
        out_shape=jax.ShapeDtypeStruct((n, d), F32),
        in_specs=[pl.BlockSpec((tm, f), lambda i, j: (i, 0)), pl.BlockSpec((f, tn), lambda i, j: (0, j))],
        out_specs=pl.BlockSpec((tm, tn), lambda i, j: (i, j)),
        compiler_params=_params(_mb(56)),
    )(hf, wd)


def _ln2_loss(xh1, ffn, tgt, lg1, lb1, g2, lg2, lb2):
    n, d = xh1.shape
    tm = 256

    def body(xh_ref, f_ref, t_ref, lg1_ref, lb1_ref, g2_ref, lg2_ref, lb2_ref, dr_ref, df_ref, loss_ref, acc_ref):
        @pl.when(pl.program_id(0) == 0)
        def _():
            loss_ref[...] = jnp.zeros_like(loss_ref)
            acc_ref[...] = jnp.zeros_like(acc_ref)

        x1 = xh_ref[...] * lg1_ref[...] + lb1_ref[...]
        fv = f_ref[...]
        r = ALPHA * x1 + g2_ref[...] * fv
        dlt = r - _rowmean(r)
        rstd = lax.rsqrt(_rowmean(dlt * dlt) + EPS)
        xh2 = dlt * rstd
        err = xh2 * lg2_ref[...] + lb2_ref[...] - t_ref[...]
        loss_ref[...] += 0.5 * jnp.sum(_rowmean(err * err))
        dy = err * (1.0 / d)
        dyg = dy * lg2_ref[...]
        dr = rstd * (dyg - _rowmean(dyg) - xh2 * _rowmean(dyg * xh2))
        dr_ref[...] = dr
        df_ref[...] = (g2_ref[...] * dr).astype(BF16)
        acc_ref[0:1, :] += _colsum(dy * xh2)
        acc_ref[1:2, :] += _colsum(dy)
        acc_ref[2:3, :] += _colsum(dr * fv)

    row = lambda i: (i, 0)
    const2 = lambda i: (0, 0)
    vec = pl.BlockSpec((1, d), const2)
    big = pl.BlockSpec((tm, d), row)
    return _pallas(
        body, name="ln2_loss", grid=(n // tm,),
        out_shape=(jax.ShapeDtypeStruct((n, d), F32), jax.ShapeDtypeStruct((n, d), BF16),
                   jax.ShapeDtypeStruct((8, HEAD), F32), jax.ShapeDtypeStruct((8, d), F32)),
        in_specs=[big, big, big, vec, vec, vec, vec, vec],
        out_specs=(big, big, pl.BlockSpec((8, HEAD), const2), pl.BlockSpec((8, d), const2)),
        compiler_params=_params(_mb(48)),
    )(xh1, ffn, tgt, lg1, lb1, g2, lg2, lb2)


def _ffn_dhf(df, wd_g, gmat, pmat):
    n, d = df.shape
    f = gmat.shape[1]
    tm = min(1024, n)

    def body(df_ref, w_ref, g_ref, p_ref, dg_ref, dp_ref):
        dhf = lax.dot_general(df_ref[...], w_ref[0], _NT, preferred_element_type=F32)
        gv = g_ref[...].astype(F32)
        sg = _sigmoid(gv)
        dp_ref[...] = (dhf * (gv * sg)).astype(BF16)
        dg_ref[...] = (dhf * p_ref[...].astype(F32) * (sg * (1.0 + gv * (1.0 - sg)))).astype(BF16)

    tile = pl.BlockSpec((tm, FFN_PAD), lambda i, j: (i, j))
    sds = jax.ShapeDtypeStruct((n, f), BF16)
    return _pallas(
        body, name="ffn_dhf", grid=(n // tm, NDEV),
        out_shape=(sds, sds),
        in_specs=[pl.BlockSpec((tm, d), lambda i, j: (i, 0)), pl.BlockSpec((1, FFN_PAD, d), lambda i, j: (j, 0, 0)),
                  tile, tile],
        out_specs=(tile, tile),
        compiler_params=_params(_mb(48)),
    )(df, wd_g, gmat, pmat)


def _ffn_du2(dg, dp, wg_g, wu_g, after):
    n, f = dg.shape
    d = wg_g.shape[1]
    tm = min(1024, n)

    def body(dg_ref, dp_ref, wg_ref, wu_ref, after_ref, o_ref):
        part = (lax.dot_general(dg_ref[...], wg_ref[0], _NT, preferred_element_type=F32)
                + lax.dot_general(dp_ref[...], wu_ref[0], _NT, preferred_element_type=F32))

        @pl.when(pl.program_id(1) == 0)
        def _():
            o_ref[...] = part

        @pl.when(pl.program_id(1) > 0)
        def _():
            o_ref[...] += part

    tile = lambda i, j: (i, j)
    wspec = pl.BlockSpec((1, d, FFN_PAD), lambda i, j: (j, 0, 0))
    return _pallas(
        body, name="ffn_du2", grid=(n // tm, NDEV),
        out_shape=jax.ShapeDtypeStruct((n, d), F32),
        in_specs=[pl.BlockSpec((tm, FFN_PAD), tile), pl.BlockSpec((tm, FFN_PAD), tile), wspec, wspec, _ANY],
        out_specs=pl.BlockSpec((tm, d), lambda i, j: (i, 0)),
        compiler_params=_params(_mb(48)),
    )(dg, dp, wg_g, wu_g, after)


def _ln1_bwd(du2, dr2, xh1, rs1, a1, lg1, lb1, sc2, g1):
    n, d = du2.shape
    tm = 256

    def body(du_ref, dr2_ref, xh_ref, rs_ref, a_ref, lg_ref, lb_ref, sc_ref, g1_ref, dr1_ref, da_ref, acc_ref):
        @pl.when(pl.program_id(0) == 0)
        def _():
            acc_ref[...] = jnp.zeros_like(acc_ref)

        du = du_ref[...]
        xh = xh_ref[...]
        x1 = xh * lg_ref[...] + lb_ref[...]
        dx1 = ALPHA * dr2_ref[...] + du * (1.0 + sc_ref[...])
        dxg = dx1 * lg_ref[...]
        dr1 = rs_ref[...] * (dxg - _rowmean(dxg) - xh * _rowmean(dxg * xh))
        dr1_ref[...] = dr1
        da_ref[...] = (g1_ref[...] * dr1).astype(BF16)
        acc_ref[0:1, :] += _colsum(du * x1)
        acc_ref[1:2, :] += _colsum(du)
        acc_ref[2:3, :] += _colsum(dx1 * xh)
        acc_ref[3:4, :] += _colsum(dx1)
        acc_ref[4:5, :] += _colsum(dr1 * a_ref[...])

    row = lambda i: (i, 0)
    const2 = lambda i: (0, 0)
    vec = pl.BlockSpec((1, d), const2)
    big = pl.BlockSpec((tm, d), row)
    return _pallas(
        body, name="ln1_bwd", grid=(n // tm,),
        out_shape=(jax.ShapeDtypeStruct((n, d), F32), jax.ShapeDtypeStruct((n, d), BF16),
                   jax.ShapeDtypeStruct((8, d), F32)),
        in_specs=[big, big, big, pl.BlockSpec((tm, 1), row), big, vec, vec, vec, vec],
        out_specs=(big, big, pl.BlockSpec((8, d), const2)),
        compiler_params=_params(_mb(48)),
    )(du2, dr2, xh1, rs1, a1, lg1, lb1, sc2, g1)


def _dw_cols(a, b, nblk, bw, tm, after, name):
    m, k = a.shape

    def body(a_ref, b_ref, after_ref, o_ref, acc_ref):
        part = lax.dot_general(a_ref[...], b_ref[...], _TN, preferred_element_type=F32)
        i = pl.program_id(1)

        @pl.when(i == 0)
        def _():
            acc_ref[...] = part

        @pl.when(i > 0)
        def _():
            acc_ref[...] += part

        @pl.when(i == pl.num_programs(1) - 1)
        def _():
            o_ref[0] = acc_ref[...].astype(BF16)

    return _pallas(
        body, name=name, grid=(nblk, m // tm),
        out_shape=jax.ShapeDtypeStruct((nblk, k, bw), BF16),
        in_specs=[pl.BlockSpec((tm, k), lambda j, i: (i, 0)), pl.BlockSpec((tm, bw), lambda j, i: (i, j)), _ANY],
        out_specs=pl.BlockSpec((1, k, bw), lambda j, i: (j, 0, 0)),
        scratch_shapes=[pltpu.VMEM((k, bw), F32)],
        compiler_params=_params(_mb(48)),
    )(a, b, after)


def _dw_in(u, dh, after):
    m, k = u.shape
    tm = CTX
    half = 4 * IN_SHARD

    def body(u_ref, dh_ref, after_ref, o_ref, acc_ref):
        part = lax.dot_general(u_ref[...], dh_ref[...], _TN, preferred_element_type=F32)
        i = pl.program_id(1)

        @pl.when(i == 0)
        def _():
            acc_ref[...] = part

        @pl.when(i > 0)
        def _():
            acc_ref[...] += part

        @pl.when(i == pl.num_programs(1) - 1)
        def _():
            for jj in range(4):
                o_ref[jj] = acc_ref[:, jj * IN_SHARD:(jj + 1) * IN_SHARD].astype(BF16)

    return _pallas(
        body, name="dw_in", grid=(2, m // tm),
        out_shape=jax.ShapeDtypeStruct((NDEV, k, IN_SHARD), BF16),
        in_specs=[pl.BlockSpec((tm, k), lambda jh, i: (i, 0)), pl.BlockSpec((tm, half), lambda jh, i: (i, jh)), _ANY],
        out_specs=pl.BlockSpec((4, k, IN_SHARD), lambda jh, i: (jh, 0, 0)),
        scratch_shapes=[pltpu.VMEM((k, half), F32)],
        compiler_params=_params(_mb(48)),
    )(u, dh, after)


def _dw_rows(a, b, nblk, bw, tm, name):
    m = a.shape[0]
    nn = b.shape[1]

    def body(a_ref, b_ref, o_ref, acc_ref):
        part = lax.dot_general(a_ref[...].astype(BF16), b_ref[...], _TN, preferred_element_type=F32)
        i = pl.program_id(1)

        @pl.when(i == 0)
        def _():
            acc_ref[...] = part

        @pl.when(i > 0)
        def _():
            acc_ref[...] += part

        @pl.when(i == pl.num_programs(1) - 1)
        def _():
            o_ref[0] = acc_ref[...].astype(BF16)

    return _pallas(
        body, name=name, grid=(nblk, m // tm),
        out_shape=jax.ShapeDtypeStruct((nblk, bw, nn), BF16),
        in_specs=[pl.BlockSpec((tm, bw), lambda j, i: (i, j)), pl.BlockSpec((tm, nn), lambda j, i: (i, 0))],
        out_specs=pl.BlockSpec((1, bw, nn), lambda j, i: (j, 0, 0)),
        scratch_shapes=[pltpu.VMEM((bw, nn), F32)],
        compiler_params=_params(_mb(48)),
    )(a, b)


def _outproj_bwd(da1, wout, after):
    n, d = da1.shape
    tm = 512

    def body(a_ref, w_ref, after_ref, o_ref):
        o_ref[...] = lax.dot_general(a_ref[...], w_ref[...], _NT, preferred_element_type=F32)

    return _pallas(
        body, name="outproj_bwd", grid=(n // tm,),
        out_shape=jax.ShapeDtypeStruct((n, d), F32),
        in_specs=[pl.BlockSpec((tm, d), lambda i: (i, 0)), pl.BlockSpec((d, d), lambda i: (0, 0)), _ANY],
        out_specs=pl.BlockSpec((tm, d), lambda i: (i, 0)),
        compiler_params=_params(_mb(48)),
    )(da1, wout, after)


def _qkv_bwd(dh, win_g, x, ct, dr1, sc):
    na, wcols = dh.shape
    n, d = x.shape
    tm = CTX
    nlat = n // tm

    def body(dh_ref, w_ref, x_ref, ct_ref, dr_ref, sc_ref, gx_ref, acc_ref):
        i = pl.program_id(0)

        @pl.when(i == 0)
        def _():
            acc_ref[...] = jnp.zeros_like(acc_ref)

        du = jnp.zeros((tm, d), F32)
        for j in range(NDEV):
            du = du + lax.dot_general(dh_ref[:, j * IN_SHARD:(j + 1) * IN_SHARD], w_ref[j], _NT,
                                      preferred_element_type=F32)

        @pl.when(i < nlat)
        def _():
            gx_ref[...] = ALPHA * dr_ref[...] + du * (1.0 + sc_ref[0])
            acc_ref[0:1, :] += _colsum(du)
            acc_ref[1:2, :] += _colsum(du * x_ref[...])

        @pl.when(i == nlat)
        def _():
            acc_ref[2:3, :] += _colsum(du)
            acc_ref[3:4, :] += _colsum(du * ct_ref[...])

    lat = lambda i: (jnp.minimum(i, nlat - 1), 0)
    const2 = lambda i: (0, 0)
    return _pallas(
        body, name="qkv_bwd", grid=(nlat + 1,),
        out_shape=(jax.ShapeDtypeStruct((n, d), F32), jax.ShapeDtypeStruct((8, d), F32)),
        in_specs=[pl.BlockSpec((tm, wcols), lambda i: (i, 0)), pl.BlockSpec((NDEV, d, IN_SHARD), lambda i: (0, 0, 0)),
                  pl.BlockSpec((tm, d), lat), pl.BlockSpec((tm, d), const2), pl.BlockSpec((tm, d), lat),
                  pl.BlockSpec((1, 1, d), lambda i: (0, 0, 0))],
        out_specs=(pl.BlockSpec((tm, d), lat), pl.BlockSpec((8, d), const2)),
        compiler_params=_params(_mb(56)),
    )(dh, win_g, x, ct, dr1, sc)


def _adam_math(w, g, m, v):
    m2 = ADAM_B1 * m + (1.0 - ADAM_B1) * g
    v2 = ADAM_B2 * v + (1.0 - ADAM_B2) * (g * g)
    m_hat = m2 / (1.0 - ADAM_B1 ** ADAM_STEP)
    v_hat = v2 / (1.0 - ADAM_B2 ** ADAM_STEP)
    delta = -ADAM_LR * (m_hat / (jnp.sqrt(v_hat) + ADAM_EPS) + ADAM_WD * w)
    return delta, m2, v2


def _adamw(w, gsrc, m, v, name):
    r, c = w.shape
    parts = gsrc.ndim == 3
    cg = gsrc.shape[-1]
    tr = r
    while tr * c * 4 > _mb(1) and tr % 32 == 0:
        tr //= 2

    def body(w_ref, g_ref, m_ref, v_ref, go_ref, d_ref, mo_ref, vo_ref):
        if parts:
            g = g_ref[0].astype(F32)
            for s in range(1, NDEV):
                g = g + g_ref[s].astype(F32)
            g = g[:, :c]
        else:
            g = g_ref[...]
        delta, m2, v2 = _adam_math(w_ref[...], g, m_ref[...], v_ref[...])
        go_ref[...] = g
        d_ref[...] = delta
        mo_ref[...] = m2
        vo_ref[...] = v2

    tile = pl.BlockSpec((tr, c), lambda i: (i, 0))
    gspec = pl.BlockSpec((NDEV, tr, cg), lambda i: (0, i, 0)) if parts else tile
    sds = jax.ShapeDtypeStruct((r, c), F32)
    return _pallas(
        body, name=name, grid=(r // tr,),
        out_shape=(sds, sds, sds, sds),
        in_specs=[tile, gspec, tile, tile],
        out_specs=(tile, tile, tile, tile),
        compiler_params=_params(_mb(48)),
    )(w, gsrc, m, v)


def _small_update(gath, dcc, cc, w_s, m_s, v_s):
    d = w_s.shape[1]

    def body(g_ref, dcc_ref, cc_ref, w_ref, m_ref, v_ref, go_ref, d_ref, mo_ref, vo_ref):
        s = g_ref[0]
        for b in range(1, NDEV):
            s = s + g_ref[b]
        dsl = dcc_ref[0, 8:9, :]
        for b in range(1, NDEV):
            dsl = dsl + dcc_ref[b, 8:9, :]
        cv = cc_ref[...]
        sg = _sigmoid(cv)
        go_ref[...] = jnp.zeros_like(go_ref)
        go_ref[0:1, :] = dsl * (sg * (1.0 + cv * (1.0 - sg)))
        go_ref[1:3, :] = s[0:2] + s[6:8]
        go_ref[3:7, :] = s[2:6]
        go_ref[7:12, :] = s[8:13]
        delta, m2, v2 = _adam_math(w_ref[...], go_ref[...], m_ref[...], v_ref[...])
        d_ref[...] = delta
        mo_ref[...] = m2
        vo_ref[...] = v2

    full = pl.BlockSpec((16, d), lambda: (0, 0))
    g3 = pl.BlockSpec((NDEV, 16, d), lambda: (0, 0, 0))
    sds = jax.ShapeDtypeStruct((16, d), F32)
    return _pallas(
        body, name="small_update",
        out_shape=(sds, sds, sds, sds),
        in_specs=[g3, g3, pl.BlockSpec((1, d), lambda: (0, 0)), full, full, full],
        out_specs=(full, full, full, full),
        compiler_params=_params(_mb(24)),
    )(gath, dcc, cc, w_s, m_s, v_s)


def _rope_tables(n):
    rows = n // GRID_W
    row_ids = jnp.repeat(jnp.arange(rows, dtype=F32), GRID_W)
    col_ids = jnp.tile(jnp.arange(GRID_W, dtype=F32), rows)
    axis_dim = HEAD // 2
    inv_freq = jnp.power(ROPE_THETA, -jnp.arange(0, axis_dim, 2, dtype=F32) / axis_dim)
    ang_r = row_ids[:, None] * inv_freq
    ang_c = col_ids[:, None] * inv_freq
    ang = jnp.concatenate([ang_r, ang_r, ang_c, ang_c], axis=-1)
    cos, sin = jnp.cos(ang), jnp.sin(ang)
    first = (jnp.arange(HEAD) % (HEAD // 2)) < HEAD // 4
    sa = jnp.where(first, -sin, 0.0)
    sb = jnp.where(first, 0.0, sin)
    ones = jnp.ones((CTX, HEAD), F32)
    zeros = jnp.zeros((CTX, HEAD), F32)
    return (jnp.concatenate([cos, ones], 0), jnp.concatenate([sa, zeros], 0), jnp.concatenate([sb, zeros], 0))


def _pad_cols(a, width):
    return jnp.pad(a, ((0, 0), (0, width - a.shape[1])))


def _pad_rows(a, rows):
    return jnp.pad(a, ((0, rows - a.shape[0]), (0, 0)))


def _pack_small(c_ctx, b_ada, ln1_g, ln1_b, ln2_g, ln2_b, qg, kg, sink, d):
    misc = _pad_cols(jnp.concatenate([qg, kg, sink], axis=1), d)
    rows = jnp.concatenate([c_ctx.reshape(1, d), b_ada.reshape(6, d), ln1_g, ln1_b, ln2_g, ln2_b, misc], axis=0)
    return _pad_rows(rows, 16)


def _unpack_small(p, d):
    return dict(c_ctx=p[0], b_ada=p[1:7].reshape(1, 6 * d), ln1_g=p[7:8], ln1_b=p[8:9], ln2_g=p[9:10], ln2_b=p[10:11],
                q_norm_g=p[11:12, 0:HEAD], k_norm_g=p[11:12, HEAD:2 * HEAD], sink_logit=p[11:12, 2 * HEAD:2 * HEAD + 8])


def kernel(x, c, ctx, c_ctx, w_ada, b_ada, w_in, q_norm_g, k_norm_g, sink_logit, w_out, ln1_g, ln1_b, w_gate, w_up, w_down, ln2_g, ln2_b, loss_target, m_c_ctx, m_w_ada, m_b_ada, m_w_in, m_q_norm_g, m_k_norm_g, m_sink_logit, m_w_out, m_ln1_g, m_ln1_b, m_w_gate, m_w_up, m_w_down, m_ln2_g, m_ln2_b, v_c_ctx, v_w_ada, v_b_ada, v_w_in, v_q_norm_g, v_k_norm_g, v_sink_logit, v_w_out, v_ln1_g, v_ln1_b, v_w_gate, v_w_up, v_w_down, v_ln2_g, v_ln2_b):
    xs, cts, tgt = x[0], ctx[0], loss_target[0]
    n, d = xs.shape
    assert cts.shape == (CTX, d) and w_in.shape[2] == IN_SHARD and w_gate.shape[2] == FFN_SHARD
    me = 4 * lax.axis_index("x") + 2 * lax.axis_index("y") + lax.axis_index("c")
    e_sh = w_ada.shape[2]

    h_win, tok = _exchange_start(w_in[0].astype(BF16), False, c, "gather_w_in_start")
    c_g = _exchange(_pad_rows(c, 8), False, "gather_c", after=tok)
    c_all = jnp.concatenate([c_g[:, 0, :], _pad_rows(c_ctx.reshape(1, d), 8)], axis=0)
    bias_sh = lax.dynamic_slice(b_ada, (0, me * e_sh), (1, e_sh))
    mods_g = _exchange(_ada_fwd(c_all, w_ada[0], bias_sh), False, "gather_mods")
    mods = jnp.transpose(mods_g, (1, 0, 2)).reshape(16, NDEV * e_sh)
    mine = lax.dynamic_slice(mods, (me, 0), (1, 6 * d))
    sh1, sc1, g1, sh2, sc2, g2 = [mine[:, k * d:(k + 1) * d] for k in range(6)]
    csh1, csc1 = mods[8:9, 0:d], mods[8:9, d:2 * d]
    sc_pair = jnp.stack([sc1, csc1])
    sh_pair = jnp.stack([sh1, csh1])

    h_wout, tok = _exchange_start(w_out[0].astype(BF16), False, mods, "gather_w_out_start")
    h_wg, tok = _exchange_start(_pad_cols(w_gate[0], FFN_PAD).astype(BF16), False, tok, "gather_w_gate_start")
    h_wu, tok = _exchange_start(_pad_cols(w_up[0], FFN_PAD).astype(BF16), False, tok, "gather_w_up_start")
    h_wd, tok = _exchange_start(_pad_rows(w_down[0], FFN_PAD).astype(BF16), False, tok, "gather_w_down_start")

    cos, sa, sb = _rope_tables(n)
    win_g = _exchange_wait(h_win, False, tok, "gather_w_in_wait")
    u_all, h_all, t_all = _qkv_fwd(xs, cts, sc_pair, sh_pair, win_g, q_norm_g, k_norm_g, cos, sa, sb)
    o_a, lse_a = _attn_fwd(t_all, sink_logit, True, 0, 8, 10, 0, WINDOW, "attn_window_fwd")
    o_b, lse_b = _attn_fwd(t_all, sink_logit, False, 3, 20, 22, 2, 256, "attn_global_fwd")
    o = jnp.concatenate([o_a, o_b], axis=1)
    wout_g = _exchange_wait(h_wout, False, o, "gather_w_out_wait").reshape(d, d)
    a1, xh1, rs1, u2 = _outproj_ln1(o, wout_g, xs, g1, ln1_g, ln1_b, sc2, sh2)
    wg_g = _exchange_wait(h_wg, False, rs1, "gather_w_gate_wait")
    wu_g = _exchange_wait(h_wu, False, rs1, "gather_w_up_wait")
    gmat, pmat, hf = _ffn_up(u2, wg_g, wu_g)
    wd_g = _exchange_wait(h_wd, False, hf, "gather_w_down_wait")
    ffn = _ffn_down(hf, wd_g.reshape(NDEV * FFN_PAD, d))
    dr2, df, loss_p, acc2 = _ln2_loss(xh1, ffn, tgt, ln1_g, ln1_b, g2, ln2_g, ln2_b)
    loss = lax.psum(loss_p[0, 0], ("x", "y", "c"))

    tk = min(n, 2048)
    dgm, dpm = _ffn_dhf(df, wd_g, gmat, pmat)
    dwd_p = _dw_rows(hf, df, NDEV, FFN_PAD, tk, "dw_down")
    h_dwd, tok = _exchange_start(dwd_p, True, loss_p, "scatter_dw_down_start")
    dwg_p = _dw_cols(u2, dgm, NDEV, FFN_PAD, tk, tok, "dw_gate")
    h_dwg, tok = _exchange_start(dwg_p, True, tok, "scatter_dw_gate_start")
    dwu_p = _dw_cols(u2, dpm, NDEV, FFN_PAD, tk, tok, "dw_up")
    h_dwu, tok = _exchange_start(dwu_p, True, tok, "scatter_dw_up_start")
    du2 = _ffn_du2(dgm, dpm, wg_g, wu_g, tok)
    dr1, da1, acc1 = _ln1_bwd(du2, dr2, xh1, rs1, a1, ln1_g, ln1_b, sc2, g1)
    dwo_p = _dw_rows(o, da1, NDEV, 2 * HEAD, tk, "dw_out")
    h_dwo, tok = _exchange_start(dwo_p, True, loss_p, "scatter_dw_out_start")
    do = _outproj_bwd(da1, wout_g, tok)
    dqa, dka, dva, dsink = _attn_bwd(t_all, o, do, lse_a, sink_logit, True, 0, 8, 10, 0, WINDOW, "attn_window_bwd")
    dqb, dkb, dvb, _ = _attn_bwd(t_all, o, do, lse_b, sink_logit, False, 3, 20, 22, 2, 256, "attn_global_bwd")
    dh_all, dnorm = _qkv_bwd_prep(dqa, dka, dva, dqb, dkb, dvb, h_all, q_norm_g, k_norm_g, cos, sa, sb)
    grad_x, acc0 = _qkv_bwd(dh_all, win_g, xs, cts, dr1, sc_pair)

    misc = _pad_cols(jnp.concatenate([dnorm[0:1], dnorm[1:2], dsink[:, 0:4, 0].reshape(1, 8)], axis=1), d)
    part = jnp.concatenate([
        acc0[0:2], acc1[4:5], acc1[1:2], acc1[0:1], acc2[2:3],
        acc0[2:4],
        acc1[2:4], acc2[0:2],
        misc, jnp.zeros((3, d), F32)], axis=0)
    gath = _exchange(part, False, "gather_small")
    dm_batch = gath[:, 0:6, :].reshape(NDEV, 6 * d)
    dm_ctx = _pad_cols(gath[:, 6:8, :].reshape(NDEV, 2 * d), 6 * d)
    dm16 = lax.dynamic_slice(jnp.concatenate([dm_batch, dm_ctx], axis=0), (0, me * e_sh), (16, e_sh))
    dw_ada, drow = _ada_bwd(dm16, c_all, w_ada[0])
    dcc = _exchange(drow, False, "gather_dcc")
    dwi_p = _dw_in(u_all, dh_all, dcc)
    h_dwi, tok = _exchange_start(dwi_p, True, dcc, "scatter_dw_in_start")

    w_s = _pack_small(c_ctx, b_ada, ln1_g, ln1_b, ln2_g, ln2_b, q_norm_g, k_norm_g, sink_logit, d)
    m_s = _pack_small(m_c_ctx, m_b_ada, m_ln1_g, m_ln1_b, m_ln2_g, m_ln2_b, m_q_norm_g, m_k_norm_g, m_sink_logit, d)
    v_s = _pack_small(v_c_ctx, v_b_ada, v_ln1_g, v_ln1_b, v_ln2_g, v_ln2_b, v_q_norm_g, v_k_norm_g, v_sink_logit, d)
    small = [_unpack_small(p, d) for p in _small_update(gath, dcc, c_ctx.reshape(1, d), w_s, m_s, v_s)]

    big = {}
    big["w_ada"] = _adamw(w_ada[0], dw_ada, m_w_ada[0], v_w_ada[0], "adamw_w_ada")
    late = tok
    big["w_down"] = _adamw(w_down[0], _exchange_wait(h_dwd, True, late, "scatter_dw_down_wait"), m_w_down[0],
                           v_w_down[0], "adamw_w_down")
    for nm, wt, mt, vt, hd in (("w_gate", w_gate, m_w_gate, v_w_gate, h_dwg), ("w_up", w_up, m_w_up, v_w_up, h_dwu)):
        big[nm] = _adamw(wt[0], _exchange_wait(hd, True, late, "scatter_d" + nm + "_wait"), mt[0], vt[0], "adamw_" + nm)
    big["w_out"] = _adamw(w_out[0], _exchange_wait(h_dwo, True, late, "scatter_dw_out_wait"), m_w_out[0], v_w_out[0],
                          "adamw_w_out")
    big["w_in"] = _adamw(w_in[0], _exchange_wait(h_dwi, True, big["w_out"][1], "scatter_dw_in_wait"), m_w_in[0],
                         v_w_in[0], "adamw_w_in")

    names = ["c_ctx", "w_ada", "b_ada", "w_in", "q_norm_g", "k_norm_g", "sink_logit", "w_out", "ln1_g", "ln1_b",
             "w_gate", "w_up", "w_down", "ln2_g", "ln2_b"]
    outs = [loss, grad_x[None]]
    for k in range(4):
        for nm in names:
            outs.append(big[nm][k][None] if nm in big else small[k][nm])
    return tuple(outs)
```

```python
import functools

import jax
import jax.numpy as jnp
from jax import lax
from jax.experimental import pallas as pl
from jax.experimental.pallas import tpu as pltpu

F32 = jnp.float32
BF16 = jnp.bfloat16

NDEV = 8
HEAD = 128
CTX = 256
GRID_W = 64
WINDOW = 128
ROPE_THETA = 10000.0
EPS = 1e-6
SCALE = HEAD ** -0.5
LOG2E = 1.4426950408889634
QK_LOG2 = SCALE * LOG2E
ALPHA = 2.0 ** 0.25
FFN_SHARD = 704
FFN_PAD = 768
IN_SHARD = 384
NEG = -1e30

ADAM_LR = 0.001
ADAM_B1 = 0.9
ADAM_B2 = 0.999
ADAM_EPS = 1e-08
ADAM_WD = 0.01
ADAM_STEP = 10

VMEM_CAP = 56 * 1024 * 1024

_KINDS = ["rope"] * 10 + ["none"] * 2 + ["qnorm"] * 8 + ["knorm"] * 2 + ["none"] * 2

_NT = (((1,), (1,)), ((), ()))
_TN = (((0,), (0,)), ((), ()))


def _pallas(body, **kw):
    return pl.pallas_call(body, **kw)


def _params(vmem_bytes):
    return pltpu.CompilerParams(vmem_limit_bytes=int(min(VMEM_CAP, vmem_bytes)))


def _mb(n):
    return int(n * 1024 * 1024)


def _sigmoid(x):
    return 1.0 / (1.0 + jnp.exp(-x))


def _colsum(a):
    return jnp.sum(a, axis=0, keepdims=True)


def _rowmean(a):
    return jnp.mean(a, axis=-1, keepdims=True)


def _exchange(src, scatter, name, after=None):
    blk = src.shape[1:] if scatter else src.shape
    after = src if after is None else after

    def body(src_ref, after_ref, out_ref, send_sems, recv_sems, local_sem):
        x, y, c = lax.axis_index("x"), lax.axis_index("y"), lax.axis_index("c")
        me = 4 * x + 2 * y + c
        copies = []
        for t in range(1, NDEV):
            px = 1 - x if (t >> 2) & 1 else x
            py = 1 - y if (t >> 1) & 1 else y
            pc = 1 - c if t & 1 else c
            peer = 4 * px + 2 * py + pc
            cp = pltpu.make_async_remote_copy(
                src_ref=src_ref.at[peer] if scatter else src_ref,
                dst_ref=out_ref.at[me],
                send_sem=send_sems.at[t - 1],
                recv_sem=recv_sems.at[t - 1],
                device_id=(px, py, pc),
                device_id_type=pl.DeviceIdType.MESH,
            )
            cp.start()
            copies.append(cp)
        own = pltpu.make_async_copy(src_ref.at[me] if scatter else src_ref, out_ref.at[me], local_sem)
        own.start()
        for cp in copies:
            cp.wait()
        own.wait()

    return _pallas(
        body, name=name,
        out_shape=jax.ShapeDtypeStruct((NDEV,) + tuple(blk), src.dtype),
        in_specs=[pl.BlockSpec(memory_space=pl.ANY), pl.BlockSpec(memory_space=pl.ANY)],
        out_specs=pl.BlockSpec(memory_space=pl.ANY),
        scratch_shapes=[pltpu.SemaphoreType.DMA((NDEV - 1,)), pltpu.SemaphoreType.DMA((NDEV - 1,)),
                        pltpu.SemaphoreType.DMA(())],
    )(src, after)


_HBM = pl.BlockSpec(memory_space=pltpu.HBM)
_SEM = pl.BlockSpec(memory_space=pltpu.SEMAPHORE)
_ANY = pl.BlockSpec(memory_space=pl.ANY)
_EFFECT = pltpu.SideEffectType.DATAFLOW_SIDE_EFFECTING


def _exchange_copies(src_ref, land_ref, send_sems, recv_sems, scatter):
    x, y, c = lax.axis_index("x"), lax.axis_index("y"), lax.axis_index("c")
    me = 4 * x + 2 * y + c
    copies = []
    for t in range(1, NDEV):
        px = 1 - x if (t >> 2) & 1 else x
        py = 1 - y if (t >> 1) & 1 else y
        pc = 1 - c if t & 1 else c
        peer = 4 * px + 2 * py + pc
        copies.append(pltpu.make_async_remote_copy(
            src_ref=src_ref.at[peer] if scatter else src_ref,
            dst_ref=land_ref.at[me],
            send_sem=send_sems.at[t - 1],
            recv_sem=recv_sems.at[t - 1],
            device_id=(px, py, pc),
            device_id_type=pl.DeviceIdType.MESH,
        ))
    own = pltpu.make_async_copy(src_ref.at[me] if scatter else src_ref, land_ref.at[me], send_sems.at[NDEV - 1])
    return copies, own


def _exchange_start(src, scatter, after, name):
    blk = src.shape[1:] if scatter else src.shape
    land = lax.empty((NDEV,) + tuple(blk), src.dtype)

    def body(src_ref, land_ref, after_ref, send_sems, recv_sems, src_thru, land_thru, token):
        copies, own = _exchange_copies(src_ref, land_ref, send_sems, recv_sems, scatter)
        for cp in copies:
            cp.start()
        own.start()
        token[...] = jnp.zeros_like(token)

    res = _pallas(
        body, name=name,
        out_shape=(pltpu.SemaphoreType.DMA((NDEV,)), pltpu.SemaphoreType.DMA((NDEV,)),
                   pltpu.HBM(src.shape, src.dtype), pltpu.HBM(land.shape, land.dtype),
                   jax.ShapeDtypeStruct((8, HEAD), F32)),
        in_specs=(_HBM, _HBM, _ANY), out_specs=(_SEM, _SEM, _HBM, _HBM, pl.BlockSpec(memory_space=pltpu.VMEM)),
        input_output_aliases={0: 2, 1: 3},
        compiler_params=pltpu.CompilerParams(has_side_effects=_EFFECT),
    )(pltpu.with_memory_space_constraint(src, pltpu.HBM), pltpu.with_memory_space_constraint(land, pltpu.HBM), after)
    return res[:4], res[4]


def _exchange_wait(handle, scatter, after, name):
    send_sems, recv_sems, src_thru, land_thru = handle

    def body(src_ref, land_ref, send_sems, recv_sems, after_ref, src_dead, got_ref):
        copies, own = _exchange_copies(src_ref, land_ref, send_sems, recv_sems, scatter)
        for cp in copies:
            cp.wait_send()
            cp.wait_recv()
        own.wait()

    return _pallas(
        body, name=name,
        out_shape=(pltpu.HBM(src_thru.shape, src_thru.dtype), pltpu.HBM(land_thru.shape, land_thru.dtype)),
        in_specs=(_HBM, _HBM, _SEM, _SEM, _ANY), out_specs=(_HBM, _HBM),
        input_output_aliases={0: 0, 1: 1},
        compiler_params=pltpu.CompilerParams(has_side_effects=_EFFECT),
    )(src_thru, land_thru, send_sems, recv_sems, after)[1]


def _ada_fwd(c_all, w, bias):
    r, d = c_all.shape
    e = w.shape[1]
    tn = 512

    def body(c_ref, w_ref, b_ref, o_ref):
        cv = c_ref[...]
        s = (cv * _sigmoid(cv)).astype(BF16)
        o_ref[...] = jnp.dot(s, w_ref[...].astype(BF16), preferred_element_type=F32) + b_ref[...]

    return _pallas(
        body, name="ada_fwd", grid=(e // tn,),
        out_shape=jax.ShapeDtypeStruct((r, e), F32),
        in_specs=[pl.BlockSpec((r, d), lambda j: (0, 0)), pl.BlockSpec((d, tn), lambda j: (0, j)),
                  pl.BlockSpec((1, tn), lambda j: (0, j))],
        out_specs=pl.BlockSpec((r, tn), lambda j: (0, j)),
        compiler_params=_params(_mb(24)),
    )(c_all, w, bias)


def _ada_bwd(dm16, c_all, w):
    d, e = w.shape
    tn = 512

    def body(dm_ref, c_ref, w_ref, dw_ref, dr_ref):
        j = pl.program_id(0)
        dm = dm_ref[...]
        rid = lax.broadcasted_iota(jnp.int32, dm.shape, 0)
        ctx_sum = jnp.sum(jnp.where(rid >= 8, dm, 0.0), axis=0, keepdims=True)
        rows = jnp.where(rid < 8, dm, jnp.where(rid == 8, jnp.broadcast_to(ctx_sum, dm.shape), 0.0)).astype(BF16)
        cv = c_ref[...]
        s = (cv * _sigmoid(cv)).astype(BF16)
        dw_ref[...] = lax.dot_general(s, rows, _TN, preferred_element_type=F32)
        part = lax.dot_general(rows, w_ref[...].astype(BF16), _NT, preferred_element_type=F32)

        @pl.when(j == 0)
        def _():
            dr_ref[...] = part

        @pl.when(j > 0)
        def _():
            dr_ref[...] += part

    return _pallas(
        body, name="ada_bwd", grid=(e // tn,),
        out_shape=(jax.ShapeDtypeStruct((d, e), F32), jax.ShapeDtypeStruct((16, d), F32)),
        in_specs=[pl.BlockSpec((16, tn), lambda j: (0, j)), pl.BlockSpec((16, d), lambda j: (0, 0)),
                  pl.BlockSpec((d, tn), lambda j: (0, j))],
        out_specs=(pl.BlockSpec((d, tn), lambda j: (0, j)), pl.BlockSpec((16, d), lambda j: (0, 0))),
        compiler_params=_params(_mb(32)),
    )(dm16, c_all, w)


def _rope(v, cos, sa, sb):
    return v * cos + (pltpu.roll(v, 96, 1) * sa + pltpu.roll(v, 32, 1) * sb)


def _rope_t(dt, cos, sa, sb):
    return dt * cos + (pltpu.roll(dt * sa, 32, 1) + pltpu.roll(dt * sb, 96, 1))


def _qkv_fwd(x, ct, sc, sh, win_g, qg, kg, cos, sa, sb):
    n, d = x.shape
    tm = CTX
    nlat = n // tm
    na = n + CTX
    wcols = NDEV * IN_SHARD

    def body(x_ref, ct_ref, sc_ref, sh_ref, w_ref, qg_ref, kg_ref, cos_ref, sa_ref, sb_ref, u_ref, h_ref, t_ref):
        i = pl.program_id(0)
        xin = jnp.where(i == nlat, ct_ref[...], x_ref[...])
        u = (xin * (1.0 + sc_ref[0]) + sh_ref[0]).astype(BF16)
        u_ref[...] = u
        cos, sa, sb = cos_ref[...], sa_ref[...], sb_ref[...]
        for j in range(NDEV):
            h = jnp.dot(u, w_ref[j], preferred_element_type=F32)
            h_ref[:, j * IN_SHARD:(j + 1) * IN_SHARD] = h
            for hh in range(3):
                hd = 3 * j + hh
                v = h[:, hh * HEAD:(hh + 1) * HEAD]
                kind = _KINDS[hd]
                if kind == "qnorm":
                    v = v * lax.rsqrt(_rowmean(v * v) + EPS) * qg_ref[...]
                elif kind == "knorm":
                    v = v * lax.rsqrt(_rowmean(v * v) + EPS) * kg_ref[...]
                if kind != "none":
                    v = _rope(v, cos, sa, sb)
                t_ref[:, hd * HEAD:(hd + 1) * HEAD] = v.astype(BF16)

    lat = lambda i: (jnp.minimum(i, nlat - 1), 0)
    row = lambda i: (i, 0)
    const2 = lambda i: (0, 0)
    return _pallas(
        body, name="qkv_fwd", grid=(nlat + 1,),
        out_shape=(jax.ShapeDtypeStruct((na, d), BF16), jax.ShapeDtypeStruct((na, wcols), F32),
                   jax.ShapeDtypeStruct((na, wcols), BF16)),
        in_specs=[pl.BlockSpec((tm, d), lat), pl.BlockSpec((tm, d), const2),
                  pl.BlockSpec((1, 1, d), lambda i: (i // nlat, 0, 0)),
                  pl.BlockSpec((1, 1, d), lambda i: (i // nlat, 0, 0)),
                  pl.BlockSpec((NDEV, d, IN_SHARD), lambda i: (0, 0, 0)),
                  pl.BlockSpec((1, HEAD), const2), pl.BlockSpec((1, HEAD), const2),
                  pl.BlockSpec((tm, HEAD), row), pl.BlockSpec((tm, HEAD), row), pl.BlockSpec((tm, HEAD), row)],
        out_specs=(pl.BlockSpec((tm, d), row), pl.BlockSpec((tm, wcols), row), pl.BlockSpec((tm, wcols), row)),
        compiler_params=_params(_mb(56)),
    )(x, ct, sc, sh, win_g, qg, kg, cos, sa, sb)


def _qkv_bwd_prep(dqa, dka, dva, dqb, dkb, dvb, h_all, qg, kg, cos, sa, sb):
    na, wcols = h_all.shape
    n = na - CTX
    tm = CTX
    nlat = n // tm

    def body(dqa_ref, dka_ref, dva_ref, dqb_ref, dkb_ref, dvb_ref, h_ref, qg_ref, kg_ref, cos_ref, sa_ref, sb_ref,
             dh_ref, dg_ref):
        i = pl.program_id(0)

        @pl.when(i == 0)
        def _():
            dg_ref[...] = jnp.zeros_like(dg_ref)

        cos, sa, sb = cos_ref[...], sa_ref[...], sb_ref[...]
        is_lat = i < nlat
        for hd in range(24):
            kind = _KINDS[hd]
            if hd < 8:
                dt = jnp.where(is_lat, dqa_ref[:, hd * HEAD:(hd + 1) * HEAD], 0.0)
            elif hd < 10:
                dt = dka_ref[:, (hd - 8) * HEAD:(hd - 7) * HEAD]
            elif hd < 12:
                dt = dva_ref[:, (hd - 10) * HEAD:(hd - 9) * HEAD]
            elif hd < 20:
                dt = jnp.where(is_lat, dqb_ref[:, (hd - 12) * HEAD:(hd - 11) * HEAD], 0.0)
            elif hd < 22:
                dt = dkb_ref[:, (hd - 20) * HEAD:(hd - 19) * HEAD]
            else:
                dt = dvb_ref[:, (hd - 22) * HEAD:(hd - 21) * HEAD]
            if kind != "none":
                dt = _rope_t(dt, cos, sa, sb)
            if kind in ("qnorm", "knorm"):
                g_ref = qg_ref if kind == "qnorm" else kg_ref
                r0 = 0 if kind == "qnorm" else 1
                xv = h_ref[:, hd * HEAD:(hd + 1) * HEAD]
                xn = xv * lax.rsqrt(_rowmean(xv * xv) + EPS)
                dg_ref[r0:r0 + 1, :] += _colsum(dt * xn)
                dxn = dt * g_ref[...]
                dt = lax.rsqrt(_rowmean(xv * xv) + EPS) * (dxn - xn * _rowmean(dxn * xn))
            dh_ref[:, hd * HEAD:(hd + 1) * HEAD] = dt.astype(BF16)

    lat = lambda i: (jnp.minimum(i, nlat - 1), 0)
    row = lambda i: (i, 0)
    const2 = lambda i: (0, 0)
    return _pallas(
        body, name="qkv_bwd_prep", grid=(nlat + 1,),
        out_shape=(jax.ShapeDtypeStruct((na, wcols), BF16), jax.ShapeDtypeStruct((8, HEAD), F32)),
        in_specs=[pl.BlockSpec((tm, 8 * HEAD), lat), pl.BlockSpec((tm, 2 * HEAD), row), pl.BlockSpec((tm, 2 * HEAD), row),
                  pl.BlockSpec((tm, 8 * HEAD), lat), pl.BlockSpec((tm, 2 * HEAD), row), pl.BlockSpec((tm, 2 * HEAD), row),
                  pl.BlockSpec((tm, wcols), row),
                  pl.BlockSpec((1, HEAD), const2), pl.BlockSpec((1, HEAD), const2),
                  pl.BlockSpec((tm, HEAD), row), pl.BlockSpec((tm, HEAD), row), pl.BlockSpec((tm, HEAD), row)],
        out_specs=(pl.BlockSpec((tm, wcols), row), pl.BlockSpec((8, HEAD), const2)),
        compiler_params=_params(_mb(40)),
    )(dqa, dka, dva, dqb, dkb, dvb, h_all, qg, kg, cos, sa, sb)


def _window_keys(k_ref, v_ref, n, na):
    i = pl.program_id(1)
    tq = WINDOW
    start = pl.multiple_of(jnp.clip((i - 1) * tq, 0, n - 3 * tq), tq)
    kk = jnp.concatenate([k_ref[pl.ds(start, 3 * tq), :], k_ref[n:na, :]], axis=0)
    vv = jnp.concatenate([v_ref[pl.ds(start, 3 * tq), :], v_ref[n:na, :]], axis=0)
    nk = 3 * tq + CTX
    col = lax.broadcasted_iota(jnp.int32, (4 * tq, nk), 1)
    rowi = lax.broadcasted_iota(jnp.int32, (4 * tq, nk), 0)
    qpos = i * tq + (rowi & (tq - 1))
    valid = (jnp.abs(qpos - (start + col)) <= WINDOW) | (col >= 3 * tq)
    return kk, vv, valid, start


def _stack_heads(ref, width=HEAD):
    return jnp.concatenate([ref[:, g * HEAD:g * HEAD + width] for g in range(4)], axis=0)


def _sink_column(sink_ref, kv, tq):
    grp = lax.broadcasted_iota(jnp.int32, (4 * tq, 1), 0) // tq
    col = jnp.zeros((4 * tq, 1), F32)
    for g in range(4):
        col = jnp.where(grp == g, sink_ref[0, 4 * kv + g] * LOG2E, col)
    return col


def _attn_window_fwd(t_all, sink):
    na = t_all.shape[0]
    n = na - CTX
    tq = WINDOW

    def body(sink_ref, q_ref, k_ref, v_ref, o_ref, lse_ref):
        kv = pl.program_id(0)
        kk, vv, valid, _ = _window_keys(k_ref, v_ref, n, na)
        t = lax.dot_general(_stack_heads(q_ref), kk, _NT, preferred_element_type=F32) * QK_LOG2
        t = jnp.where(valid, t, NEG)
        sk = _sink_column(sink_ref, kv, tq)
        m = jnp.maximum(jnp.max(t, axis=-1, keepdims=True), sk)
        p = jnp.exp2(t - m)
        l = jnp.sum(p, axis=-1, keepdims=True) + jnp.exp2(sk - m)
        o = jnp.dot(p.astype(BF16), vv, preferred_element_type=F32) * (1.0 / l)
        lse = m + jnp.log2(l)
        for g in range(4):
            o_ref[:, g * HEAD:(g + 1) * HEAD] = o[g * tq:(g + 1) * tq]
            lse_ref[:, g * HEAD:(g + 1) * HEAD] = jnp.broadcast_to(lse[g * tq:(g + 1) * tq], (tq, HEAD))

    blk = pl.BlockSpec((tq, 4 * HEAD), lambda kv, i: (i, kv))
    return _pallas(
        body, name="attn_window_fwd", grid=(2, n // tq),
        out_shape=(jax.ShapeDtypeStruct((n, 16 * HEAD), F32), jax.ShapeDtypeStruct((n, 8 * HEAD), F32)),
        in_specs=[pl.BlockSpec(memory_space=pltpu.SMEM), blk,
                  pl.BlockSpec((na, HEAD), lambda kv, i: (0, 8 + kv)),
                  pl.BlockSpec((na, HEAD), lambda kv, i: (0, 10 + kv))],
        out_specs=(blk, blk),
        compiler_params=_params(_mb(32)),
    )(sink, t_all, t_all, t_all)


def _attn_global_fwd(t_all, o_part):
    na = t_all.shape[0]
    n = na - CTX
    tq = 256

    def body(q_ref, k_ref, v_ref, o_in_ref, o_ref, lse_ref):
        kk, vv = k_ref[...], v_ref[...]
        for g in range(4):
            q = q_ref[:, g * HEAD:(g + 1) * HEAD]
            t = lax.dot_general(q, kk, _NT, preferred_element_type=F32) * QK_LOG2
            m = jnp.max(t, axis=-1, keepdims=True)
            p = jnp.exp2(t - m)
            l = jnp.sum(p, axis=-1, keepdims=True)
            o_ref[:, g * HEAD:(g + 1) * HEAD] = jnp.dot(p.astype(BF16), vv, preferred_element_type=F32) * (1.0 / l)
            lse_ref[:, g * HEAD:(g + 1) * HEAD] = jnp.broadcast_to(m + jnp.log2(l), (tq, HEAD))

    return _pallas(
        body, name="attn_global_fwd", grid=(2, n // tq),
        out_shape=(jax.ShapeDtypeStruct((n, 16 * HEAD), F32), jax.ShapeDtypeStruct((n, 8 * HEAD), F32)),
        in_specs=[pl.BlockSpec((tq, 4 * HEAD), lambda kv, i: (i, 3 + kv)),
                  pl.BlockSpec((na, HEAD), lambda kv, i: (0, 20 + kv)),
                  pl.BlockSpec((na, HEAD), lambda kv, i: (0, 22 + kv)), _ANY],
        out_specs=(pl.BlockSpec((tq, 4 * HEAD), lambda kv, i: (i, 2 + kv)),
                   pl.BlockSpec((tq, 4 * HEAD), lambda kv, i: (i, kv))),
        input_output_aliases={3: 0},
        compiler_params=_params(_mb(48)),
    )(t_all, t_all, t_all, o_part)


def _attn_window_bwd(t_all, o, do, lse, sink):
    na = t_all.shape[0]
    n = na - CTX
    tq = WINDOW

    def body(sink_ref, q_ref, k_ref, v_ref, o_ref, do_ref, lse_ref, dq_ref, dk_ref, dv_ref, dsink_ref):
        kv = pl.program_id(0)

        @pl.when(pl.program_id(1) == 0)
        def _():
            dk_ref[...] = jnp.zeros_like(dk_ref)
            dv_ref[...] = jnp.zeros_like(dv_ref)
            dsink_ref[...] = jnp.zeros_like(dsink_ref)

        kk, vv, valid, start = _window_keys(k_ref, v_ref, n, na)
        q = _stack_heads(q_ref)
        t = lax.dot_general(q, kk, _NT, preferred_element_type=F32) * QK_LOG2
        t = jnp.where(valid, t, NEG)
        lse = _stack_heads(lse_ref, 1)
        p = jnp.exp2(t - lse)
        dof = _stack_heads(do_ref)
        delta = jnp.sum(dof * _stack_heads(o_ref), axis=-1, keepdims=True)
        dob = dof.astype(BF16)
        dv_acc = lax.dot_general(p.astype(BF16), dob, _TN, preferred_element_type=F32)
        dp = lax.dot_general(dob, vv, _NT, preferred_element_type=F32)
        ds = (p * (dp - delta) * SCALE).astype(BF16)
        dq = jnp.dot(ds, kk, preferred_element_type=F32)
        dk_acc = lax.dot_general(ds, q, _TN, preferred_element_type=F32)
        dsk = -(jnp.exp2(_sink_column(sink_ref, kv, tq) - lse) * delta)
        for g in range(4):
            dq_ref[:, g * HEAD:(g + 1) * HEAD] = dq[g * tq:(g + 1) * tq]
            dsink_ref[0, g:g + 1, :] += jnp.broadcast_to(_colsum(dsk[g * tq:(g + 1) * tq]), (1, HEAD))
        dk_ref[pl.ds(start, 3 * tq), :] += dk_acc[:3 * tq]
        dv_ref[pl.ds(start, 3 * tq), :] += dv_acc[:3 * tq]
        dk_ref[n:na, :] += dk_acc[3 * tq:]
        dv_ref[n:na, :] += dv_acc[3 * tq:]

    blk = pl.BlockSpec((tq, 4 * HEAD), lambda kv, i: (i, kv))
    kvout = pl.BlockSpec((na, HEAD), lambda kv, i: (0, kv))
    return _pallas(
        body, name="attn_window_bwd", grid=(2, n // tq),
        out_shape=(jax.ShapeDtypeStruct((n, 8 * HEAD), F32), jax.ShapeDtypeStruct((na, 2 * HEAD), F32),
                   jax.ShapeDtypeStruct((na, 2 * HEAD), F32), jax.ShapeDtypeStruct((2, 8, HEAD), F32)),
        in_specs=[pl.BlockSpec(memory_space=pltpu.SMEM), blk,
                  pl.BlockSpec((na, HEAD), lambda kv, i: (0, 8 + kv)),
                  pl.BlockSpec((na, HEAD), lambda kv, i: (0, 10 + kv)),
                  blk, blk, blk],
        out_specs=(blk, kvout, kvout, pl.BlockSpec((1, 8, HEAD), lambda kv, i: (kv, 0, 0))),
        compiler_params=_params(_mb(40)),
    )(sink, t_all, t_all, t_all, o, do, lse)


def _attn_global_bwd(t_all, o, do, lse):
    na = t_all.shape[0]
    n = na - CTX
    tq = 256

    def body(q_ref, k_ref, v_ref, o_ref, do_ref, lse_ref, dq_ref, dk_ref, dv_ref):
        @pl.when(pl.program_id(1) == 0)
        def _():
            dk_ref[...] = jnp.zeros_like(dk_ref)
            dv_ref[...] = jnp.zeros_like(dv_ref)

        kk, vv = k_ref[...], v_ref[...]
        dk_acc = jnp.zeros((na, HEAD), F32)
        dv_acc = jnp.zeros((na, HEAD), F32)
        for g in range(4):
            q = q_ref[:, g * HEAD:(g + 1) * HEAD]
            t = lax.dot_general(q, kk, _NT, preferred_element_type=F32) * QK_LOG2
            p = jnp.exp2(t - lse_ref[:, g * HEAD:g * HEAD + 1])
            dof = do_ref[:, g * HEAD:(g + 1) * HEAD]
            delta = jnp.sum(dof * o_ref[:, g * HEAD:(g + 1) * HEAD], axis=-1, keepdims=True)
            dob = dof.astype(BF16)
            dv_acc = dv_acc + lax.dot_general(p.astype(BF16), dob, _TN, preferred_element_type=F32)
            dp = lax.dot_general(dob, vv, _NT, preferred_element_type=F32)
            ds = (p * (dp - delta) * SCALE).astype(BF16)
            dq_ref[:, g * HEAD:(g + 1) * HEAD] = jnp.dot(ds, kk, preferred_element_type=F32)
            dk_acc = dk_acc + lax.dot_general(ds, q, _TN, preferred_element_type=F32)
        dk_ref[...] += dk_acc
        dv_ref[...] += dv_acc

    ospec = pl.BlockSpec((tq, 4 * HEAD), lambda kv, i: (i, 2 + kv))
    lspec = pl.BlockSpec((tq, 4 * HEAD), lambda kv, i: (i, kv))
    kvout = pl.BlockSpec((na, HEAD), lambda kv, i: (0, kv))
    return _pallas(
        body, name="attn_global_bwd", grid=(2, n // tq),
        out_shape=(jax.ShapeDtypeStruct((n, 8 * HEAD), F32), jax.ShapeDtypeStruct((na, 2 * HEAD), F32),
                   jax.ShapeDtypeStruct((na, 2 * HEAD), F32)),
        in_specs=[pl.BlockSpec((tq, 4 * HEAD), lambda kv, i: (i, 3 + kv)),
                  pl.BlockSpec((na, HEAD), lambda kv, i: (0, 20 + kv)),
                  pl.BlockSpec((na, HEAD), lambda kv, i: (0, 22 + kv)),
                  ospec, ospec, lspec],
        out_specs=(lspec, kvout, kvout),
        compiler_params=_params(_mb(56)),
    )(t_all, t_all, t_all, o, do, lse)


def _outproj_ln1(o, wout, x, g1, lg, lb, sc2, sh2):
    n, d = x.shape
    tm = 256

    def body(o_ref, w_ref, x_ref, g1_ref, lg_ref, lb_ref, sc_ref, sh_ref, a_ref, xh_ref, rs_ref, u_ref):
        a1 = jnp.dot(o_ref[...].astype(BF16), w_ref[...], preferred_element_type=F32)
        a_ref[...] = a1
        r = ALPHA * x_ref[...] + g1_ref[...] * a1
        dlt = r - _rowmean(r)
        rstd = lax.rsqrt(_rowmean(dlt * dlt) + EPS)
        xh = dlt * rstd
        xh_ref[...] = xh
        rs_ref[...] = rstd
        x1 = xh * lg_ref[...] + lb_ref[...]
        u_ref[...] = (x1 * (1.0 + sc_ref[...]) + sh_ref[...]).astype(BF16)

    row = lambda i: (i, 0)
    const2 = lambda i: (0, 0)
    vec = pl.BlockSpec((1, d), const2)
    big = pl.BlockSpec((tm, d), row)
    return _pallas(
        body, name="outproj_ln1", grid=(n // tm,),
        out_shape=(jax.ShapeDtypeStruct((n, d), F32), jax.ShapeDtypeStruct((n, d), F32),
                   jax.ShapeDtypeStruct((n, 1), F32), jax.ShapeDtypeStruct((n, d), BF16)),
        in_specs=[big, pl.BlockSpec((d, d), const2), big, vec, vec, vec, vec, vec],
        out_specs=(big, big, pl.BlockSpec((tm, 1), row), big),
        compiler_params=_params(_mb(56)),
    )(o, wout, x, g1, lg, lb, sc2, sh2)


def _ffn_up(u2, wg_g, wu_g):
    n, d = u2.shape
    tm = min(1024, n)
    f = NDEV * FFN_PAD

    def body(u_ref, wg_ref, wu_ref, g_ref, p_ref, hf_ref):
        u = u_ref[...]
        gv = jnp.dot(u, wg_ref[0], preferred_element_type=F32)
        pv = jnp.dot(u, wu_ref[0], preferred_element_type=F32)
        g_ref[...] = gv.astype(BF16)
        p_ref[...] = pv.astype(BF16)
        hf_ref[...] = (gv * _sigmoid(gv) * pv).astype(BF16)

    tile = pl.BlockSpec((tm, FFN_PAD), lambda i, j: (i, j))
    wspec = pl.BlockSpec((1, d, FFN_PAD), lambda i, j: (j, 0, 0))
    sds = jax.ShapeDtypeStruct((n, f), BF16)
    return _pallas(
        body, name="ffn_up", grid=(n // tm, NDEV),
        out_shape=(sds, sds, sds),
        in_specs=[pl.BlockSpec((tm, d), lambda i, j: (i, 0)), wspec, wspec],
        out_specs=(tile, tile, tile),
        compiler_params=_params(_mb(48)),
    )(u2, wg_g, wu_g)


def _ffn_down(hf, wd):
    n, f = hf.shape
    d = wd.shape[1]
    tm, tn = min(1024, n), 512

    def body(h_ref, w_ref, o_ref):
        o_ref[...] = jnp.dot(h_ref[...], w_ref[...], preferred_element_type=F32)

    return _pallas(
        body, name="ffn_down", grid=(n // tm, d // tn),
        out_shape=jax.ShapeDtypeStruct((n, d), F32),
        in_specs=[pl.BlockSpec((tm, f), lambda i, j: (i, 0)), pl.BlockSpec((f, tn), lambda i, j: (0, j))],
        out_specs=pl.BlockSpec((tm, tn), lambda i, j: (i, j)),
        compiler_params=_params(_mb(56)),
    )(hf, wd)


def _ln2_loss(xh1, ffn, tgt, lg1, lb1, g2, lg2, lb2):
    n, d = xh1.shape
    tm = 256

    def body(xh_ref, f_ref, t_ref, lg1_ref, lb1_ref, g2_ref, lg2_ref, lb2_ref, dr_ref, df_ref, loss_ref, acc_ref):
        @pl.when(pl.program_id(0) == 0)
        def _():
            loss_ref[...] = jnp.zeros_like(loss_ref)
            acc_ref[...] = jnp.zeros_like(acc_ref)

        x1 = xh_ref[...] * lg1_ref[...] + lb1_ref[...]
        fv = f_ref[...]
        r = ALPHA * x1 + g2_ref[...] * fv
        dlt = r - _rowmean(r)
        rstd = lax.rsqrt(_rowmean(dlt * dlt) + EPS)
        xh2 = dlt * rstd
        err = xh2 * lg2_ref[...] + lb2_ref[...] - t_ref[...]
        loss_ref[...] += 0.5 * jnp.sum(_rowmean(err * err))
        dy = err * (1.0 / d)
        dyg = dy * lg2_ref[...]
        dr = rstd * (dyg - _rowmean(dyg) - xh2 * _rowmean(dyg * xh2))
        dr_ref[...] = dr
        df_ref[...] = (g2_ref[...] * dr).astype(BF16)
        acc_ref[0:1, :] += _colsum(dy * xh2)
        acc_ref[1:2, :] += _colsum(dy)
        acc_ref[2:3, :] += _colsum(dr * fv)

    row = lambda i: (i, 0)
    const2 = lambda i: (0, 0)
    vec = pl.BlockSpec((1, d), const2)
    big = pl.BlockSpec((tm, d), row)
    return _pallas(
        body, name="ln2_loss", grid=(n // tm,),
        out_shape=(jax.ShapeDtypeStruct((n, d), F32), jax.ShapeDtypeStruct((n, d), BF16),
                   jax.ShapeDtypeStruct((8, HEAD), F32), jax.ShapeDtypeStruct((8, d), F32)),
        in_specs=[big, big, big, vec, vec, vec, vec, vec],
        out_specs=(big, big, pl.BlockSpec((8, HEAD), const2), pl.BlockSpec((8, d), const2)),
        compiler_params=_params(_mb(48)),
    )(xh1, ffn, tgt, lg1, lb1, g2, lg2, lb2)


def _ffn_dhf(df, wd_g, gmat, pmat):
    n, d = df.shape
    f = gmat.shape[1]
    tm = min(1024, n)

    def body(df_ref, w_ref, g_ref, p_ref, dg_ref, dp_ref):
        dhf = lax.dot_general(df_ref[...], w_ref[0], _NT, preferred_element_type=F32)
        gv = g_ref[...].astype(F32)
        sg = _sigmoid(gv)
        dp_ref[...] = (dhf * (gv * sg)).astype(BF16)
        dg_ref[...] = (dhf * p_ref[...].astype(F32) * (sg * (1.0 + gv * (1.0 - sg)))).astype(BF16)

    tile = pl.BlockSpec((tm, FFN_PAD), lambda i, j: (i, j))
    sds = jax.ShapeDtypeStruct((n, f), BF16)
    return _pallas(
        body, name="ffn_dhf", grid=(n // tm, NDEV),
        out_shape=(sds, sds),
        in_specs=[pl.BlockSpec((tm, d), lambda i, j: (i, 0)), pl.BlockSpec((1, FFN_PAD, d), lambda i, j: (j, 0, 0)),
                  tile, tile],
        out_specs=(tile, tile),
        compiler_params=_params(_mb(48)),
    )(df, wd_g, gmat, pmat)


def _ffn_du2(dg, dp, wg_g, wu_g, after):
    n, f = dg.shape
    d = wg_g.shape[1]
    tm = min(1024, n)

    def body(dg_ref, dp_ref, wg_ref, wu_ref, after_ref, o_ref):
        part = (lax.dot_general(dg_ref[...], wg_ref[0], _NT, preferred_element_type=F32)
                + lax.dot_general(dp_ref[...], wu_ref[0], _NT, preferred_element_type=F32))

        @pl.when(pl.program_id(1) == 0)
        def _():
            o_ref[...] = part

        @pl.when(pl.program_id(1) > 0)
        def _():
            o_ref[...] += part

    tile = lambda i, j: (i, j)
    wspec = pl.BlockSpec((1, d, FFN_PAD), lambda i, j: (j, 0, 0))
    return _pallas(
        body, name="ffn_du2", grid=(n // tm, NDEV),
        out_shape=jax.ShapeDtypeStruct((n, d), F32),
        in_specs=[pl.BlockSpec((tm, FFN_PAD), tile), pl.BlockSpec((tm, FFN_PAD), tile), wspec, wspec, _ANY],
        out_specs=pl.BlockSpec((tm, d), lambda i, j: (i, 0)),
        compiler_params=_params(_mb(48)),
    )(dg, dp, wg_g, wu_g, after)


def _ln1_bwd(du2, dr2, xh1, rs1, a1, lg1, lb1, sc2, g1):
    n, d = du2.shape
    tm = 256

    def body(du_ref, dr2_ref, xh_ref, rs_ref, a_ref, lg_ref, lb_ref, sc_ref, g1_ref, dr1_ref, da_ref, acc_ref):
        @pl.when(pl.program_id(0) == 0)
        def _():
            acc_ref[...] = jnp.zeros_like(acc_ref)

        du = du_ref[...]
        xh = xh_ref[...]
        x1 = xh * lg_ref[...] + lb_ref[...]
        dx1 = ALPHA * dr2_ref[...] + du * (1.0 + sc_ref[...])
        dxg = dx1 * lg_ref[...]
        dr1 = rs_ref[...] * (dxg - _rowmean(dxg) - xh * _rowmean(dxg * xh))
        dr1_ref[...] = dr1
        da_ref[...] = (g1_ref[...] * dr1).astype(BF16)
        acc_ref[0:1, :] += _colsum(du * x1)
        acc_ref[1:2, :] += _colsum(du)
        acc_ref[2:3, :] += _colsum(dx1 * xh)
        acc_ref[3:4, :] += _colsum(dx1)
        acc_ref[4:5, :] += _colsum(dr1 * a_ref[...])

    row = lambda i: (i, 0)
    const2 = lambda i: (0, 0)
    vec = pl.BlockSpec((1, d), const2)
    big = pl.BlockSpec((tm, d), row)
    return _pallas(
        body, name="ln1_bwd", grid=(n // tm,),
        out_shape=(jax.ShapeDtypeStruct((n, d), F32), jax.ShapeDtypeStruct((n, d), BF16),
                   jax.ShapeDtypeStruct((8, d), F32)),
        in_specs=[big, big, big, pl.BlockSpec((tm, 1), row), big, vec, vec, vec, vec],
        out_specs=(big, big, pl.BlockSpec((8, d), const2)),
        compiler_params=_params(_mb(48)),
    )(du2, dr2, xh1, rs1, a1, lg1, lb1, sc2, g1)


def _dw_cols(a, b, nblk, bw, tm, after, name):
    m, k = a.shape

    def body(a_ref, b_ref, after_ref, o_ref, acc_ref):
        part = lax.dot_general(a_ref[...], b_ref[...], _TN, preferred_element_type=F32)
        i = pl.program_id(1)

        @pl.when(i == 0)
        def _():
            acc_ref[...] = part

        @pl.when(i > 0)
        def _():
            acc_ref[...] += part

        @pl.when(i == pl.num_programs(1) - 1)
        def _():
            o_ref[0] = acc_ref[...].astype(BF16)

    return _pallas(
        body, name=name, grid=(nblk, m // tm),
        out_shape=jax.ShapeDtypeStruct((nblk, k, bw), BF16),
        in_specs=[pl.BlockSpec((tm, k), lambda j, i: (i, 0)), pl.BlockSpec((tm, bw), lambda j, i: (i, j)), _ANY],
        out_specs=pl.BlockSpec((1, k, bw), lambda j, i: (j, 0, 0)),
        scratch_shapes=[pltpu.VMEM((k, bw), F32)],
        compiler_params=_params(_mb(48)),
    )(a, b, after)


def _dw_in(u, dh, after):
    m, k = u.shape
    tm = CTX
    half = 4 * IN_SHARD

    def body(u_ref, dh_ref, after_ref, o_ref, acc_ref):
        part = lax.dot_general(u_ref[...], dh_ref[...], _TN, preferred_element_type=F32)
        i = pl.program_id(1)

        @pl.when(i == 0)
        def _():
            acc_ref[...] = part

        @pl.when(i > 0)
        def _():
            acc_ref[...] += part

        @pl.when(i == pl.num_programs(1) - 1)
        def _():
            for jj in range(4):
                o_ref[jj] = acc_ref[:, jj * IN_SHARD:(jj + 1) * IN_SHARD].astype(BF16)

    return _pallas(
        body, name="dw_in", grid=(2, m // tm),
        out_shape=jax.ShapeDtypeStruct((NDEV, k, IN_SHARD), BF16),
        in_specs=[pl.BlockSpec((tm, k), lambda jh, i: (i, 0)), pl.BlockSpec((tm, half), lambda jh, i: (i, jh)), _ANY],
        out_specs=pl.BlockSpec((4, k, IN_SHARD), lambda jh, i: (jh, 0, 0)),
        scratch_shapes=[pltpu.VMEM((k, half), F32)],
        compiler_params=_params(_mb(48)),
    )(u, dh, after)


def _dw_rows(a, b, nblk, bw, tm, name):
    m = a.shape[0]
    nn = b.shape[1]

    def body(a_ref, b_ref, o_ref, acc_ref):
        part = lax.dot_general(a_ref[...].astype(BF16), b_ref[...], _TN, preferred_element_type=F32)
        i = pl.program_id(1)

        @pl.when(i == 0)
        def _():
            acc_ref[...] = part

        @pl.when(i > 0)
        def _():
            acc_ref[...] += part

        @pl.when(i == pl.num_programs(1) - 1)
        def _():
            o_ref[0] = acc_ref[...].astype(BF16)

    return _pallas(
        body, name=name, grid=(nblk, m // tm),
        out_shape=jax.ShapeDtypeStruct((nblk, bw, nn), BF16),
        in_specs=[pl.BlockSpec((tm, bw), lambda j, i: (i, j)), pl.BlockSpec((tm, nn), lambda j, i: (i, 0))],
        out_specs=pl.BlockSpec((1, bw, nn), lambda j, i: (j, 0, 0)),
        scratch_shapes=[pltpu.VMEM((bw, nn), F32)],
        compiler_params=_params(_mb(48)),
    )(a, b)


def _outproj_bwd(da1, wout, after):
    n, d = da1.shape
    tm = 512

    def body(a_ref, w_ref, after_ref, o_ref):
        o_ref[...] = lax.dot_general(a_ref[...], w_ref[...], _NT, preferred_element_type=F32)

    return _pallas(
        body, name="outproj_bwd", grid=(n // tm,),
        out_shape=jax.ShapeDtypeStruct((n, d), F32),
        in_specs=[pl.BlockSpec((tm, d), lambda i: (i, 0)), pl.BlockSpec((d, d), lambda i: (0, 0)), _ANY],
        out_specs=pl.BlockSpec((tm, d), lambda i: (i, 0)),
        compiler_params=_params(_mb(48)),
    )(da1, wout, after)


def _qkv_bwd(dh, win_g, x, ct, dr1, sc):
    na, wcols = dh.shape
    n, d = x.shape
    tm = CTX
    nlat = n // tm

    def body(dh_ref, w_ref, x_ref, ct_ref, dr_ref, sc_ref, gx_ref, acc_ref):
        i = pl.program_id(0)

        @pl.when(i == 0)
        def _():
            acc_ref[...] = jnp.zeros_like(acc_ref)

        du = jnp.zeros((tm, d), F32)
        for j in range(NDEV):
            du = du + lax.dot_general(dh_ref[:, j * IN_SHARD:(j + 1) * IN_SHARD], w_ref[j], _NT,
                                      preferred_element_type=F32)

        @pl.when(i < nlat)
        def _():
            gx_ref[...] = ALPHA * dr_ref[...] + du * (1.0 + sc_ref[0])
            acc_ref[0:1, :] += _colsum(du)
            acc_ref[1:2, :] += _colsum(du * x_ref[...])

        @pl.when(i == nlat)
        def _():
            acc_ref[2:3, :] += _colsum(du)
            acc_ref[3:4, :] += _colsum(du * ct_ref[...])

    lat = lambda i: (jnp.minimum(i, nlat - 1), 0)
    const2 = lambda i: (0, 0)
    return _pallas(
        body, name="qkv_bwd", grid=(nlat + 1,),
        out_shape=(jax.ShapeDtypeStruct((n, d), F32), jax.ShapeDtypeStruct((8, d), F32)),
        in_specs=[pl.BlockSpec((tm, wcols), lambda i: (i, 0)), pl.BlockSpec((NDEV, d, IN_SHARD), lambda i: (0, 0, 0)),
                  pl.BlockSpec((tm, d), lat), pl.BlockSpec((tm, d), const2), pl.BlockSpec((tm, d), lat),
                  pl.BlockSpec((1, 1, d), lambda i: (0, 0, 0))],
        out_specs=(pl.BlockSpec((tm, d), lat), pl.BlockSpec((8, d), const2)),
        compiler_params=_params(_mb(56)),
    )(dh, win_g, x, ct, dr1, sc)


def _adam_math(w, g, m, v):
    m2 = ADAM_B1 * m + (1.0 - ADAM_B1) * g
    v2 = ADAM_B2 * v + (1.0 - ADAM_B2) * (g * g)
    m_hat = m2 / (1.0 - ADAM_B1 ** ADAM_STEP)
    v_hat = v2 / (1.0 - ADAM_B2 ** ADAM_STEP)
    delta = -ADAM_LR * (m_hat / (jnp.sqrt(v_hat) + ADAM_EPS) + ADAM_WD * w)
    return delta, m2, v2


def _adamw(w, gsrc, m, v, name):
    r, c = w.shape
    parts = gsrc.ndim == 3
    cg = gsrc.shape[-1]
    tr = r
    while tr * c * 4 > _mb(1) and tr % 32 == 0:
        tr //= 2

    def body(w_ref, g_ref, m_ref, v_ref, go_ref, d_ref, mo_ref, vo_ref):
        if parts:
            g = g_ref[0].astype(F32)
            for s in range(1, NDEV):
                g = g + g_ref[s].astype(F32)
            g = g[:, :c]
        else:
            g = g_ref[...]
        delta, m2, v2 = _adam_math(w_ref[...], g, m_ref[...], v_ref[...])
        go_ref[...] = g
        d_ref[...] = delta
        mo_ref[...] = m2
        vo_ref[...] = v2

    tile = pl.BlockSpec((tr, c), lambda i: (i, 0))
    gspec = pl.BlockSpec((NDEV, tr, cg), lambda i: (0, i, 0)) if parts else tile
    sds = jax.ShapeDtypeStruct((r, c), F32)
    return _pallas(
        body, name=name, grid=(r // tr,),
        out_shape=(sds, sds, sds, sds),
        in_specs=[tile, gspec, tile, tile],
        out_specs=(tile, tile, tile, tile),
        compiler_params=_params(_mb(48)),
    )(w, gsrc, m, v)


def _small_update(gath, dcc, cc, w_s, m_s, v_s):
    d = w_s.shape[1]

    def body(g_ref, dcc_ref, cc_ref, w_ref, m_ref, v_ref, go_ref, d_ref, mo_ref, vo_ref):
        s = g_ref[0]
        for b in range(1, NDEV):
            s = s + g_ref[b]
        dsl = dcc_ref[0, 8:9, :]
        for b in range(1, NDEV):
            dsl = dsl + dcc_ref[b, 8:9, :]
        cv = cc_ref[...]
        sg = _sigmoid(cv)
        go_ref[...] = jnp.zeros_like(go_ref)
        go_ref[0:1, :] = dsl * (sg * (1.0 + cv * (1.0 - sg)))
        go_ref[1:3, :] = s[0:2] + s[6:8]
        go_ref[3:7, :] = s[2:6]
        go_ref[7:12, :] = s[8:13]
        delta, m2, v2 = _adam_math(w_ref[...], go_ref[...], m_ref[...], v_ref[...])
        d_ref[...] = delta
        mo_ref[...] = m2
        vo_ref[...] = v2

    full = pl.BlockSpec((16, d), lambda: (0, 0))
    g3 = pl.BlockSpec((NDEV, 16, d), lambda: (0, 0, 0))
    sds = jax.ShapeDtypeStruct((16, d), F32)
    return _pallas(
        body, name="small_update",
        out_shape=(sds, sds, sds, sds),
        in_specs=[g3, g3, pl.BlockSpec((1, d), lambda: (0, 0)), full, full, full],
        out_specs=(full, full, full, full),
        compiler_params=_params(_mb(24)),
    )(gath, dcc, cc, w_s, m_s, v_s)


def _rope_tables(n):
    rows = n // GRID_W
    row_ids = jnp.repeat(jnp.arange(rows, dtype=F32), GRID_W)
    col_ids = jnp.tile(jnp.arange(GRID_W, dtype=F32), rows)
    axis_dim = HEAD // 2
    inv_freq = jnp.power(ROPE_THETA, -jnp.arange(0, axis_dim, 2, dtype=F32) / axis_dim)
    ang_r = row_ids[:, None] * inv_freq
    ang_c = col_ids[:, None] * inv_freq
    ang = jnp.concatenate([ang_r, ang_r, ang_c, ang_c], axis=-1)
    cos, sin = jnp.cos(ang), jnp.sin(ang)
    first = (jnp.arange(HEAD) % (HEAD // 2)) < HEAD // 4
    sa = jnp.where(first, -sin, 0.0)
    sb = jnp.where(first, 0.0, sin)
    ones = jnp.ones((CTX, HEAD), F32)
    zeros = jnp.zeros((CTX, HEAD), F32)
    return (jnp.concatenate([cos, ones], 0), jnp.concatenate([sa, zeros], 0), jnp.concatenate([sb, zeros], 0))


def _pad_cols(a, width):
    return jnp.pad(a, ((0, 0), (0, width - a.shape[1])))


def _pad_rows(a, rows):
    return jnp.pad(a, ((0, rows - a.shape[0]), (0, 0)))


def _pack_small(c_ctx, b_ada, ln1_g, ln1_b, ln2_g, ln2_b, qg, kg, sink, d):
    misc = _pad_cols(jnp.concatenate([qg, kg, sink], axis=1), d)
    rows = jnp.concatenate([c_ctx.reshape(1, d), b_ada.reshape(6, d), ln1_g, ln1_b, ln2_g, ln2_b, misc], axis=0)
    return _pad_rows(rows, 16)


def _unpack_small(p, d):
    return dict(c_ctx=p[0], b_ada=p[1:7].reshape(1, 6 * d), ln1_g=p[7:8], ln1_b=p[8:9], ln2_g=p[9:10], ln2_b=p[10:11],
                q_norm_g=p[11:12, 0:HEAD], k_norm_g=p[11:12, HEAD:2 * HEAD], sink_logit=p[11:12, 2 * HEAD:2 * HEAD + 8])


def kernel(x, c, ctx, c_ctx, w_ada, b_ada, w_in, q_norm_g, k_norm_g, sink_logit, w_out, ln1_g, ln1_b, w_gate, w_up, w_down, ln2_g, ln2_b, loss_target, m_c_ctx, m_w_ada, m_b_ada, m_w_in, m_q_norm_g, m_k_norm_g, m_sink_logit, m_w_out, m_ln1_g, m_ln1_b, m_w_gate, m_w_up, m_w_down, m_ln2_g, m_ln2_b, v_c_ctx, v_w_ada, v_b_ada, v_w_in, v_q_norm_g, v_k_norm_g, v_sink_logit, v_w_out, v_ln1_g, v_ln1_b, v_w_gate, v_w_up, v_w_down, v_ln2_g, v_ln2_b):
    xs, cts, tgt = x[0], ctx[0], loss_target[0]
    n, d = xs.shape
    assert cts.shape == (CTX, d) and w_in.shape[2] == IN_SHARD and w_gate.shape[2] == FFN_SHARD
    me = 4 * lax.axis_index("x") + 2 * lax.axis_index("y") + lax.axis_index("c")
    e_sh = w_ada.shape[2]

    c_g = _exchange(_pad_rows(c, 8), False, "gather_c")
    h_win, tok = _exchange_start(w_in[0].astype(BF16), False, c_g, "gather_w_in_start")
    c_all = jnp.concatenate([c_g[:, 0, :], _pad_rows(c_ctx.reshape(1, d), 8)], axis=0)
    bias_sh = lax.dynamic_slice(b_ada, (0, me * e_sh), (1, e_sh))
    mods_g = _exchange(_ada_fwd(c_all, w_ada[0], bias_sh), False, "gather_mods", after=tok)
    mods = jnp.transpose(mods_g, (1, 0, 2)).reshape(16, NDEV * e_sh)
    mine = lax.dynamic_slice(mods, (me, 0), (1, 6 * d))
    sh1, sc1, g1, sh2, sc2, g2 = [mine[:, k * d:(k + 1) * d] for k in range(6)]
    csh1, csc1 = mods[8:9, 0:d], mods[8:9, d:2 * d]
    sc_pair = jnp.stack([sc1, csc1])
    sh_pair = jnp.stack([sh1, csh1])

    h_wout, tok = _exchange_start(w_out[0].astype(BF16), False, mods, "gather_w_out_start")
    h_wg, tok = _exchange_start(_pad_cols(w_gate[0], FFN_PAD).astype(BF16), False, tok, "gather_w_gate_start")
    h_wu, tok = _exchange_start(_pad_cols(w_up[0], FFN_PAD).astype(BF16), False, tok, "gather_w_up_start")
    h_wd, tok = _exchange_start(_pad_rows(w_down[0], FFN_PAD).astype(BF16), False, tok, "gather_w_down_start")

    cos, sa, sb = _rope_tables(n)
    win_g = _exchange_wait(h_win, False, tok, "gather_w_in_wait")
    u_all, h_all, t_all = _qkv_fwd(xs, cts, sc_pair, sh_pair, win_g, q_norm_g, k_norm_g, cos, sa, sb)
    o_a, lse_a = _attn_window_fwd(t_all, sink_logit)
    o, lse_b = _attn_global_fwd(t_all, o_a)
    wout_g = _exchange_wait(h_wout, False, o, "gather_w_out_wait").reshape(d, d)
    a1, xh1, rs1, u2 = _outproj_ln1(o, wout_g, xs, g1, ln1_g, ln1_b, sc2, sh2)
    wg_g = _exchange_wait(h_wg, False, rs1, "gather_w_gate_wait")
    wu_g = _exchange_wait(h_wu, False, rs1, "gather_w_up_wait")
    gmat, pmat, hf = _ffn_up(u2, wg_g, wu_g)
    wd_g = _exchange_wait(h_wd, False, hf, "gather_w_down_wait")
    ffn = _ffn_down(hf, wd_g.reshape(NDEV * FFN_PAD, d))
    dr2, df, loss_p, acc2 = _ln2_loss(xh1, ffn, tgt, ln1_g, ln1_b, g2, ln2_g, ln2_b)
    loss = lax.psum(loss_p[0, 0], ("x", "y", "c"))

    tk = min(n, 2048)
    dgm, dpm = _ffn_dhf(df, wd_g, gmat, pmat)
    dwd_p = _dw_rows(hf, df, NDEV, FFN_PAD, tk, "dw_down")
    h_dwd, tok = _exchange_start(dwd_p, True, loss.reshape(1, 1), "scatter_dw_down_start")
    dwg_p = _dw_cols(u2, dgm, NDEV, FFN_PAD, tk, tok, "dw_gate")
    h_dwg, tok = _exchange_start(dwg_p, True, tok, "scatter_dw_gate_start")
    dwu_p = _dw_cols(u2, dpm, NDEV, FFN_PAD, tk, tok, "dw_up")
    h_dwu, tok = _exchange_start(dwu_p, True, tok, "scatter_dw_up_start")
    du2 = _ffn_du2(dgm, dpm, wg_g, wu_g, tok)
    dr1, da1, acc1 = _ln1_bwd(du2, dr2, xh1, rs1, a1, ln1_g, ln1_b, sc2, g1)
    dwo_p = _dw_rows(o, da1, NDEV, 2 * HEAD, tk, "dw_out")
    h_dwo, tok = _exchange_start(dwo_p, True, loss_p, "scatter_dw_out_start")
    do = _outproj_bwd(da1, wout_g, tok)
    dqa, dka, dva, dsink = _attn_window_bwd(t_all, o, do, lse_a, sink_logit)
    dqb, dkb, dvb = _attn_global_bwd(t_all, o, do, lse_b)
    dh_all, dnorm = _qkv_bwd_prep(dqa, dka, dva, dqb, dkb, dvb, h_all, q_norm_g, k_norm_g, cos, sa, sb)
    grad_x, acc0 = _qkv_bwd(dh_all, win_g, xs, cts, dr1, sc_pair)

    misc = _pad_cols(jnp.concatenate([dnorm[0:1], dnorm[1:2], dsink[:, 0:4, 0].reshape(1, 8)], axis=1), d)
    part = jnp.concatenate([
        acc0[0:2], acc1[4:5], acc1[1:2], acc1[0:1], acc2[2:3],
        acc0[2:4],
        acc1[2:4], acc2[0:2],
        misc, jnp.zeros((3, d), F32)], axis=0)
    gath = _exchange(part, False, "gather_small")
    dm_batch = gath[:, 0:6, :].reshape(NDEV, 6 * d)
    dm_ctx = _pad_cols(gath[:, 6:8, :].reshape(NDEV, 2 * d), 6 * d)
    dm16 = lax.dynamic_slice(jnp.concatenate([dm_batch, dm_ctx], axis=0), (0, me * e_sh), (16, e_sh))
    dw_ada, drow = _ada_bwd(dm16, c_all, w_ada[0])
    dcc = _exchange(drow, False, "gather_dcc")
    dwi_p = _dw_in(u_all, dh_all, dcc)
    h_dwi, tok = _exchange_start(dwi_p, True, dcc, "scatter_dw_in_start")

    w_s = _pack_small(c_ctx, b_ada, ln1_g, ln1_b, ln2_g, ln2_b, q_norm_g, k_norm_g, sink_logit, d)
    m_s = _pack_small(m_c_ctx, m_b_ada, m_ln1_g, m_ln1_b, m_ln2_g, m_ln2_b, m_q_norm_g, m_k_norm_g, m_sink_logit, d)
    v_s = _pack_small(v_c_ctx, v_b_ada, v_ln1_g, v_ln1_b, v_ln2_g, v_ln2_b, v_q_norm_g, v_k_norm_g, v_sink_logit, d)
    small = [_unpack_small(p, d) for p in _small_update(gath, dcc, c_ctx.reshape(1, d), w_s, m_s, v_s)]

    big = {}
    big["w_ada"] = _adamw(w_ada[0], dw_ada, m_w_ada[0], v_w_ada[0], "adamw_w_ada")
    late = tok
    big["w_down"] = _adamw(w_down[0], _exchange_wait(h_dwd, True, late, "scatter_dw_down_wait"), m_w_down[0],
                           v_w_down[0], "adamw_w_down")
    for nm, wt, mt, vt, hd in (("w_gate", w_gate, m_w_gate, v_w_gate, h_dwg), ("w_up", w_up, m_w_up, v_w_up, h_dwu)):
        big[nm] = _adamw(wt[0], _exchange_wait(hd, True, late, "scatter_d" + nm + "_wait"), mt[0], vt[0], "adamw_" + nm)
    big["w_out"] = _adamw(w_out[0], _exchange_wait(h_dwo, True, late, "scatter_dw_out_wait"), m_w_out[0], v_w_out[0],
                          "adamw_w_out")
    big["w_in"] = _adamw(w_in[0], _exchange_wait(h_dwi, True, big["w_out"][1], "scatter_dw_in_wait"), m_w_in[0],
                         v_w_in[0], "adamw_w_in")

    names = ["c_ctx", "w_ada", "b_ada", "w_in", "q_norm_g", "k_norm_g", "sink_logit", "w_out", "ln1_g", "ln1_b",
             "w_gate", "w_up", "w_down", "ln2_g", "ln2_b"]
    outs = [loss, grad_x[None]]
    for k in range(4):
        for nm in names:
            outs.append(big[nm][k][None] if nm in big else small[k][nm])
    return tuple(outs)
```

```python
import functools

import jax
import jax.numpy as jnp
from jax import lax
from jax.experimental import pallas as pl
from jax.experimental.pallas import tpu as pltpu

F32 = jnp.float32
BF16 = jnp.bfloat16

NDEV = 8
HEAD = 128
CTX = 256
GRID_W = 64
WINDOW = 128
ROPE_THETA = 10000.0
EPS = 1e-6
SCALE = HEAD ** -0.5
LOG2E = 1.4426950408889634
QK_LOG2 = SCALE * LOG2E
ALPHA = 2.0 ** 0.25
FFN_SHARD = 704
FFN_PAD = 768
IN_SHARD = 384
NEG = -1e30

ADAM_LR = 0.001
ADAM_B1 = 0.9
ADAM_B2 = 0.999
ADAM_EPS = 1e-08
ADAM_WD = 0.01
ADAM_STEP = 10

VMEM_CAP = 56 * 1024 * 1024

_KINDS = ["rope"] * 10 + ["none"] * 2 + ["qnorm"] * 8 + ["knorm"] * 2 + ["none"] * 2

_NT = (((1,), (1,)), ((), ()))
_TN = (((0,), (0,)), ((), ()))


def _pallas(body, **kw):
    return pl.pallas_call(body, **kw)


def _params(vmem_bytes):
    return pltpu.CompilerParams(vmem_limit_bytes=int(min(VMEM_CAP, vmem_bytes)))


def _mb(n):
    return int(n * 1024 * 1024)


def _sigmoid(x):
    return 1.0 / (1.0 + jnp.exp(-x))


def _colsum(a):
    return jnp.sum(a, axis=0, keepdims=True)


def _rowmean(a):
    return jnp.mean(a, axis=-1, keepdims=True)


def _exchange(src, scatter, name, after=None):
    blk = src.shape[1:] if scatter else src.shape
    after = src if after is None else after

    def body(src_ref, after_ref, out_ref, send_sems, recv_sems, local_sem):
        x, y, c = lax.axis_index("x"), lax.axis_index("y"), lax.axis_index("c")
        me = 4 * x + 2 * y + c
        copies = []
        for t in range(1, NDEV):
            px = 1 - x if (t >> 2) & 1 else x
            py = 1 - y if (t >> 1) & 1 else y
            pc = 1 - c if t & 1 else c
            peer = 4 * px + 2 * py + pc
            cp = pltpu.make_async_remote_copy(
                src_ref=src_ref.at[peer] if scatter else src_ref,
                dst_ref=out_ref.at[me],
                send_sem=send_sems.at[t - 1],
                recv_sem=recv_sems.at[t - 1],
                device_id=(px, py, pc),
                device_id_type=pl.DeviceIdType.MESH,
            )
            cp.start()
            copies.append(cp)
        own = pltpu.make_async_copy(src_ref.at[me] if scatter else src_ref, out_ref.at[me], local_sem)
        own.start()
        for cp in copies:
            cp.wait()
        own.wait()

    return _pallas(
        body, name=name,
        out_shape=jax.ShapeDtypeStruct((NDEV,) + tuple(blk), src.dtype),
        in_specs=[pl.BlockSpec(memory_space=pl.ANY), pl.BlockSpec(memory_space=pl.ANY)],
        out_specs=pl.BlockSpec(memory_space=pl.ANY),
        scratch_shapes=[pltpu.SemaphoreType.DMA((NDEV - 1,)), pltpu.SemaphoreType.DMA((NDEV - 1,)),
                        pltpu.SemaphoreType.DMA(())],
    )(src, after)


_HBM = pl.BlockSpec(memory_space=pltpu.HBM)
_SEM = pl.BlockSpec(memory_space=pltpu.SEMAPHORE)
_ANY = pl.BlockSpec(memory_space=pl.ANY)
_EFFECT = pltpu.SideEffectType.DATAFLOW_SIDE_EFFECTING


def _exchange_copies(src_ref, land_ref, send_sems, recv_sems, mode):
    x, y, c = lax.axis_index("x"), lax.axis_index("y"), lax.axis_index("c")
    me = 4 * x + 2 * y + c
    scatter = mode == "scatter"
    copies = []
    for t in ((1, 2, 4, 6) if mode == "chip" else range(1, NDEV)):
        px = 1 - x if (t >> 2) & 1 else x
        py = 1 - y if (t >> 1) & 1 else y
        pc = 1 - c if t & 1 else c
        peer = 4 * px + 2 * py + pc
        copies.append(pltpu.make_async_remote_copy(
            src_ref=src_ref.at[peer] if scatter else src_ref,
            dst_ref=land_ref.at[me],
            send_sem=send_sems.at[t - 1],
            recv_sem=recv_sems.at[t - 1],
            device_id=(px, py, pc),
            device_id_type=pl.DeviceIdType.MESH,
        ))
    own = pltpu.make_async_copy(src_ref.at[me] if scatter else src_ref, land_ref.at[me], send_sems.at[NDEV - 1])
    return copies, own


def _forward_copies(land_ref, send_sems, recv_sems):
    x, y, c = lax.axis_index("x"), lax.axis_index("y"), lax.axis_index("c")
    copies = []
    for k, t in enumerate((2, 4, 6)):
        px = 1 - x if (t >> 2) & 1 else x
        py = 1 - y if (t >> 1) & 1 else y
        mine, theirs = 4 * px + 2 * py + c, 4 * px + 2 * py + (1 - c)
        send = pltpu.make_async_remote_copy(
            src_ref=land_ref.at[mine], dst_ref=land_ref.at[mine], send_sem=send_sems.at[k], recv_sem=recv_sems.at[k],
            device_id=(x, y, 1 - c), device_id_type=pl.DeviceIdType.MESH)
        recv = pltpu.make_async_remote_copy(
            src_ref=land_ref.at[theirs], dst_ref=land_ref.at[theirs], send_sem=send_sems.at[k], recv_sem=recv_sems.at[k],
            device_id=(x, y, 1 - c), device_id_type=pl.DeviceIdType.MESH)
        copies.append((send, recv))
    return copies


def _forward_start(land, after, name):
    def body(land_ref, after_ref, send_sems, recv_sems, land_thru, token):
        for send, _ in _forward_copies(land_ref, send_sems, recv_sems):
            send.start()
        token[...] = jnp.zeros_like(token)

    res = _pallas(
        body, name=name,
        out_shape=(pltpu.SemaphoreType.DMA((3,)), pltpu.SemaphoreType.DMA((3,)), pltpu.HBM(land.shape, land.dtype),
                   jax.ShapeDtypeStruct((8, HEAD), F32)),
        in_specs=(_HBM, _ANY), out_specs=(_SEM, _SEM, _HBM, pl.BlockSpec(memory_space=pltpu.VMEM)),
        input_output_aliases={0: 2},
        compiler_params=pltpu.CompilerParams(has_side_effects=_EFFECT),
    )(land, after)
    return res[:3], res[3]


def _forward_wait(handle, after, name):
    send_sems, recv_sems, land_thru = handle

    def body(land_ref, send_sems, recv_sems, after_ref, got_ref):
        for send, recv in _forward_copies(land_ref, send_sems, recv_sems):
            send.wait_send()
            recv.wait_recv()

    return _pallas(
        body, name=name,
        out_shape=pltpu.HBM(land_thru.shape, land_thru.dtype),
        in_specs=(_HBM, _SEM, _SEM, _ANY), out_specs=_HBM,
        input_output_aliases={0: 0},
        compiler_params=pltpu.CompilerParams(has_side_effects=_EFFECT),
    )(land_thru, send_sems, recv_sems, after)


def _exchange_start(src, mode, after, name):
    blk = src.shape[1:] if mode == "scatter" else src.shape
    land = lax.empty((NDEV,) + tuple(blk), src.dtype)

    def body(src_ref, land_ref, after_ref, send_sems, recv_sems, src_thru, land_thru, token):
        copies, own = _exchange_copies(src_ref, land_ref, send_sems, recv_sems, mode)
        for cp in copies:
            cp.start()
        own.start()
        token[...] = jnp.zeros_like(token)

    res = _pallas(
        body, name=name,
        out_shape=(pltpu.SemaphoreType.DMA((NDEV,)), pltpu.SemaphoreType.DMA((NDEV,)),
                   pltpu.HBM(src.shape, src.dtype), pltpu.HBM(land.shape, land.dtype),
                   jax.ShapeDtypeStruct((8, HEAD), F32)),
        in_specs=(_HBM, _HBM, _ANY), out_specs=(_SEM, _SEM, _HBM, _HBM, pl.BlockSpec(memory_space=pltpu.VMEM)),
        input_output_aliases={0: 2, 1: 3},
        compiler_params=pltpu.CompilerParams(has_side_effects=_EFFECT),
    )(pltpu.with_memory_space_constraint(src, pltpu.HBM), pltpu.with_memory_space_constraint(land, pltpu.HBM), after)
    return res[:4], res[4]


def _exchange_wait(handle, mode, after, name):
    send_sems, recv_sems, src_thru, land_thru = handle

    def body(src_ref, land_ref, send_sems, recv_sems, after_ref, src_dead, got_ref):
        copies, own = _exchange_copies(src_ref, land_ref, send_sems, recv_sems, mode)
        for cp in copies:
            cp.wait_send()
            cp.wait_recv()
        own.wait()

    return _pallas(
        body, name=name,
        out_shape=(pltpu.HBM(src_thru.shape, src_thru.dtype), pltpu.HBM(land_thru.shape, land_thru.dtype)),
        in_specs=(_HBM, _HBM, _SEM, _SEM, _ANY), out_specs=(_HBM, _HBM),
        input_output_aliases={0: 0, 1: 1},
        compiler_params=pltpu.CompilerParams(has_side_effects=_EFFECT),
    )(src_thru, land_thru, send_sems, recv_sems, after)[1]


def _ada_fwd(c_all, w, bias):
    r, d = c_all.shape
    e = w.shape[1]
    tn = 512

    def body(c_ref, w_ref, b_ref, o_ref):
        cv = c_ref[...]
        s = (cv * _sigmoid(cv)).astype(BF16)
        o_ref[...] = jnp.dot(s, w_ref[...].astype(BF16), preferred_element_type=F32) + b_ref[...]

    return _pallas(
        body, name="ada_fwd", grid=(e // tn,),
        out_shape=jax.ShapeDtypeStruct((r, e), F32),
        in_specs=[pl.BlockSpec((r, d), lambda j: (0, 0)), pl.BlockSpec((d, tn), lambda j: (0, j)),
                  pl.BlockSpec((1, tn), lambda j: (0, j))],
        out_specs=pl.BlockSpec((r, tn), lambda j: (0, j)),
        compiler_params=_params(_mb(24)),
    )(c_all, w, bias)


def _ada_bwd(dm16, c_all, w):
    d, e = w.shape
    tn = 512

    def body(dm_ref, c_ref, w_ref, dw_ref, dr_ref):
        j = pl.program_id(0)
        dm = dm_ref[...]
        rid = lax.broadcasted_iota(jnp.int32, dm.shape, 0)
        ctx_sum = jnp.sum(jnp.where(rid >= 8, dm, 0.0), axis=0, keepdims=True)
        rows = jnp.where(rid < 8, dm, jnp.where(rid == 8, jnp.broadcast_to(ctx_sum, dm.shape), 0.0)).astype(BF16)
        cv = c_ref[...]
        s = (cv * _sigmoid(cv)).astype(BF16)
        dw_ref[...] = lax.dot_general(s, rows, _TN, preferred_element_type=F32)
        part = lax.dot_general(rows, w_ref[...].astype(BF16), _NT, preferred_element_type=F32)

        @pl.when(j == 0)
        def _():
            dr_ref[...] = part

        @pl.when(j > 0)
        def _():
            dr_ref[...] += part

    return _pallas(
        body, name="ada_bwd", grid=(e // tn,),
        out_shape=(jax.ShapeDtypeStruct((d, e), F32), jax.ShapeDtypeStruct((16, d), F32)),
        in_specs=[pl.BlockSpec((16, tn), lambda j: (0, j)), pl.BlockSpec((16, d), lambda j: (0, 0)),
                  pl.BlockSpec((d, tn), lambda j: (0, j))],
        out_specs=(pl.BlockSpec((d, tn), lambda j: (0, j)), pl.BlockSpec((16, d), lambda j: (0, 0))),
        compiler_params=_params(_mb(32)),
    )(dm16, c_all, w)


def _rope(v, cos, sa, sb):
    return v * cos + (pltpu.roll(v, 96, 1) * sa + pltpu.roll(v, 32, 1) * sb)


def _rope_t(dt, cos, sa, sb):
    return dt * cos + (pltpu.roll(dt * sa, 32, 1) + pltpu.roll(dt * sb, 96, 1))


def _qkv_fwd(x, ct, sc, sh, win_g, qg, kg, cos, sa, sb):
    n, d = x.shape
    tm = CTX
    nlat = n // tm
    na = n + CTX
    wcols = NDEV * IN_SHARD

    def body(x_ref, ct_ref, sc_ref, sh_ref, w_ref, qg_ref, kg_ref, cos_ref, sa_ref, sb_ref, u_ref, h_ref, t_ref):
        i = pl.program_id(0)
        xin = jnp.where(i == nlat, ct_ref[...], x_ref[...])
        u = (xin * (1.0 + sc_ref[0]) + sh_ref[0]).astype(BF16)
        u_ref[...] = u
        cos, sa, sb = cos_ref[...], sa_ref[...], sb_ref[...]
        for j in range(NDEV):
            h = jnp.dot(u, w_ref[j], preferred_element_type=F32)
            h_ref[:, j * IN_SHARD:(j + 1) * IN_SHARD] = h
            for hh in range(3):
                hd = 3 * j + hh
                v = h[:, hh * HEAD:(hh + 1) * HEAD]
                kind = _KINDS[hd]
                if kind == "qnorm":
                    v = v * lax.rsqrt(_rowmean(v * v) + EPS) * qg_ref[...]
                elif kind == "knorm":
                    v = v * lax.rsqrt(_rowmean(v * v) + EPS) * kg_ref[...]
                if kind != "none":
                    v = _rope(v, cos, sa, sb)
                t_ref[:, hd * HEAD:(hd + 1) * HEAD] = v.astype(BF16)

    lat = lambda i: (jnp.minimum(i, nlat - 1), 0)
    row = lambda i: (i, 0)
    const2 = lambda i: (0, 0)
    return _pallas(
        body, name="qkv_fwd", grid=(nlat + 1,),
        out_shape=(jax.ShapeDtypeStruct((na, d), BF16), jax.ShapeDtypeStruct((na, wcols), F32),
                   jax.ShapeDtypeStruct((na, wcols), BF16)),
        in_specs=[pl.BlockSpec((tm, d), lat), pl.BlockSpec((tm, d), const2),
                  pl.BlockSpec((1, 1, d), lambda i: (i // nlat, 0, 0)),
                  pl.BlockSpec((1, 1, d), lambda i: (i // nlat, 0, 0)),
                  pl.BlockSpec((NDEV, d, IN_SHARD), lambda i: (0, 0, 0)),
                  pl.BlockSpec((1, HEAD), const2), pl.BlockSpec((1, HEAD), const2),
                  pl.BlockSpec((tm, HEAD), row), pl.BlockSpec((tm, HEAD), row), pl.BlockSpec((tm, HEAD), row)],
        out_specs=(pl.BlockSpec((tm, d), row), pl.BlockSpec((tm, wcols), row), pl.BlockSpec((tm, wcols), row)),
        compiler_params=_params(_mb(56)),
    )(x, ct, sc, sh, win_g, qg, kg, cos, sa, sb)


def _qkv_bwd_prep(dqa, dka, dva, dqb, dkb, dvb, h_all, qg, kg, cos, sa, sb):
    na, wcols = h_all.shape
    n = na - CTX
    tm = CTX
    nlat = n // tm

    def body(dqa_ref, dka_ref, dva_ref, dqb_ref, dkb_ref, dvb_ref, h_ref, qg_ref, kg_ref, cos_ref, sa_ref, sb_ref,
             dh_ref, dg_ref):
        i = pl.program_id(0)

        @pl.when(i == 0)
        def _():
            dg_ref[...] = jnp.zeros_like(dg_ref)

        cos, sa, sb = cos_ref[...], sa_ref[...], sb_ref[...]
        is_lat = i < nlat
        for hd in range(24):
            kind = _KINDS[hd]
            if hd < 8:
                dt = jnp.where(is_lat, dqa_ref[:, hd * HEAD:(hd + 1) * HEAD], 0.0)
            elif hd < 10:
                dt = dka_ref[:, (hd - 8) * HEAD:(hd - 7) * HEAD]
            elif hd < 12:
                dt = dva_ref[:, (hd - 10) * HEAD:(hd - 9) * HEAD]
            elif hd < 20:
                dt = jnp.where(is_lat, dqb_ref[:, (hd - 12) * HEAD:(hd - 11) * HEAD], 0.0)
            elif hd < 22:
                dt = dkb_ref[:, (hd - 20) * HEAD:(hd - 19) * HEAD]
            else:
                dt = dvb_ref[:, (hd - 22) * HEAD:(hd - 21) * HEAD]
            if kind != "none":
                dt = _rope_t(dt, cos, sa, sb)
            if kind in ("qnorm", "knorm"):
                g_ref = qg_ref if kind == "qnorm" else kg_ref
                r0 = 0 if kind == "qnorm" else 1
                xv = h_ref[:, hd * HEAD:(hd + 1) * HEAD]
                xn = xv * lax.rsqrt(_rowmean(xv * xv) + EPS)
                dg_ref[r0:r0 + 1, :] += _colsum(dt * xn)
                dxn = dt * g_ref[...]
                dt = lax.rsqrt(_rowmean(xv * xv) + EPS) * (dxn - xn * _rowmean(dxn * xn))
            dh_ref[:, hd * HEAD:(hd + 1) * HEAD] = dt.astype(BF16)

    lat = lambda i: (jnp.minimum(i, nlat - 1), 0)
    row = lambda i: (i, 0)
    const2 = lambda i: (0, 0)
    return _pallas(
        body, name="qkv_bwd_prep", grid=(nlat + 1,),
        out_shape=(jax.ShapeDtypeStruct((na, wcols), BF16), jax.ShapeDtypeStruct((8, HEAD), F32)),
        in_specs=[pl.BlockSpec((tm, 8 * HEAD), lat), pl.BlockSpec((tm, 2 * HEAD), row), pl.BlockSpec((tm, 2 * HEAD), row),
                  pl.BlockSpec((tm, 8 * HEAD), lat), pl.BlockSpec((tm, 2 * HEAD), row), pl.BlockSpec((tm, 2 * HEAD), row),
                  pl.BlockSpec((tm, wcols), row),
                  pl.BlockSpec((1, HEAD), const2), pl.BlockSpec((1, HEAD), const2),
                  pl.BlockSpec((tm, HEAD), row), pl.BlockSpec((tm, HEAD), row), pl.BlockSpec((tm, HEAD), row)],
        out_specs=(pl.BlockSpec((tm, wcols), row), pl.BlockSpec((8, HEAD), const2)),
        compiler_params=_params(_mb(40)),
    )(dqa, dka, dva, dqb, dkb, dvb, h_all, qg, kg, cos, sa, sb)


def _window_keys(k_ref, v_ref, n, na):
    i = pl.program_id(1)
    tq = WINDOW
    start = pl.multiple_of(jnp.clip((i - 1) * tq, 0, n - 3 * tq), tq)
    kk = jnp.concatenate([k_ref[pl.ds(start, 3 * tq), :], k_ref[n:na, :]], axis=0)
    vv = jnp.concatenate([v_ref[pl.ds(start, 3 * tq), :], v_ref[n:na, :]], axis=0)
    nk = 3 * tq + CTX
    col = lax.broadcasted_iota(jnp.int32, (4 * tq, nk), 1)
    rowi = lax.broadcasted_iota(jnp.int32, (4 * tq, nk), 0)
    qpos = i * tq + (rowi & (tq - 1))
    valid = (jnp.abs(qpos - (start + col)) <= WINDOW) | (col >= 3 * tq)
    return kk, vv, valid, start


def _stack_heads(ref, width=HEAD):
    return jnp.concatenate([ref[:, g * HEAD:g * HEAD + width] for g in range(4)], axis=0)


def _sink_column(sink_ref, kv, tq):
    grp = lax.broadcasted_iota(jnp.int32, (4 * tq, 1), 0) // tq
    col = jnp.zeros((4 * tq, 1), F32)
    for g in range(4):
        col = jnp.where(grp == g, sink_ref[0, 4 * kv + g] * LOG2E, col)
    return col


def _attn_window_fwd(t_all, sink, after):
    na = t_all.shape[0]
    n = na - CTX
    tq = WINDOW

    def body(sink_ref, q_ref, k_ref, v_ref, after_ref, o_ref, lse_ref):
        kv = pl.program_id(0)
        kk, vv, valid, _ = _window_keys(k_ref, v_ref, n, na)
        t = lax.dot_general(_stack_heads(q_ref), kk, _NT, preferred_element_type=F32) * QK_LOG2
        t = jnp.where(valid, t, NEG)
        sk = _sink_column(sink_ref, kv, tq)
        m = jnp.maximum(jnp.max(t, axis=-1, keepdims=True), sk)
        p = jnp.exp2(t - m)
        l = jnp.sum(p, axis=-1, keepdims=True) + jnp.exp2(sk - m)
        o = jnp.dot(p.astype(BF16), vv, preferred_element_type=F32) * (1.0 / l)
        lse = m + jnp.log2(l)
        for g in range(4):
            o_ref[:, g * HEAD:(g + 1) * HEAD] = o[g * tq:(g + 1) * tq]
            lse_ref[:, g * HEAD:(g + 1) * HEAD] = jnp.broadcast_to(lse[g * tq:(g + 1) * tq], (tq, HEAD))

    blk = pl.BlockSpec((tq, 4 * HEAD), lambda kv, i: (i, kv))
    return _pallas(
        body, name="attn_window_fwd", grid=(2, n // tq),
        out_shape=(jax.ShapeDtypeStruct((n, 16 * HEAD), F32), jax.ShapeDtypeStruct((n, 8 * HEAD), F32)),
        in_specs=[pl.BlockSpec(memory_space=pltpu.SMEM), blk,
                  pl.BlockSpec((na, HEAD), lambda kv, i: (0, 8 + kv)),
                  pl.BlockSpec((na, HEAD), lambda kv, i: (0, 10 + kv)), _ANY],
        out_specs=(blk, blk),
        compiler_params=_params(_mb(32)),
    )(sink, t_all, t_all, t_all, after)


def _attn_global_fwd(t_all, o_part):
    na = t_all.shape[0]
    n = na - CTX
    tq = 256

    def body(q_ref, k_ref, v_ref, o_in_ref, o_ref, lse_ref):
        kk, vv = k_ref[...], v_ref[...]
        for g in range(4):
            q = q_ref[:, g * HEAD:(g + 1) * HEAD]
            t = lax.dot_general(q, kk, _NT, preferred_element_type=F32) * QK_LOG2
            m = jnp.max(t, axis=-1, keepdims=True)
            p = jnp.exp2(t - m)
            l = jnp.sum(p, axis=-1, keepdims=True)
            o_ref[:, g * HEAD:(g + 1) * HEAD] = jnp.dot(p.astype(BF16), vv, preferred_element_type=F32) * (1.0 / l)
            lse_ref[:, g * HEAD:(g + 1) * HEAD] = jnp.broadcast_to(m + jnp.log2(l), (tq, HEAD))

    return _pallas(
        body, name="attn_global_fwd", grid=(2, n // tq),
        out_shape=(jax.ShapeDtypeStruct((n, 16 * HEAD), F32), jax.ShapeDtypeStruct((n, 8 * HEAD), F32)),
        in_specs=[pl.BlockSpec((tq, 4 * HEAD), lambda kv, i: (i, 3 + kv)),
                  pl.BlockSpec((na, HEAD), lambda kv, i: (0, 20 + kv)),
                  pl.BlockSpec((na, HEAD), lambda kv, i: (0, 22 + kv)), _ANY],
        out_specs=(pl.BlockSpec((tq, 4 * HEAD), lambda kv, i: (i, 2 + kv)),
                   pl.BlockSpec((tq, 4 * HEAD), lambda kv, i: (i, kv))),
        input_output_aliases={3: 0},
        compiler_params=_params(_mb(48)),
    )(t_all, t_all, t_all, o_part)


def _attn_window_bwd(t_all, o, do, lse, sink):
    na = t_all.shape[0]
    n = na - CTX
    tq = WINDOW

    def body(sink_ref, q_ref, k_ref, v_ref, o_ref, do_ref, lse_ref, dq_ref, dk_ref, dv_ref, dsink_ref):
        kv = pl.program_id(0)

        @pl.when(pl.program_id(1) == 0)
        def _():
            dk_ref[...] = jnp.zeros_like(dk_ref)
            dv_ref[...] = jnp.zeros_like(dv_ref)
            dsink_ref[...] = jnp.zeros_like(dsink_ref)

        kk, vv, valid, start = _window_keys(k_ref, v_ref, n, na)
        q = _stack_heads(q_ref)
        t = lax.dot_general(q, kk, _NT, preferred_element_type=F32) * QK_LOG2
        t = jnp.where(valid, t, NEG)
        lse = _stack_heads(lse_ref, 1)
        p = jnp.exp2(t - lse)
        dof = _stack_heads(do_ref)
        delta = jnp.sum(dof * _stack_heads(o_ref), axis=-1, keepdims=True)
        dob = dof.astype(BF16)
        dv_acc = lax.dot_general(p.astype(BF16), dob, _TN, preferred_element_type=F32)
        dp = lax.dot_general(dob, vv, _NT, preferred_element_type=F32)
        ds = (p * (dp - delta) * SCALE).astype(BF16)
        dq = jnp.dot(ds, kk, preferred_element_type=F32)
        dk_acc = lax.dot_general(ds, q, _TN, preferred_element_type=F32)
        dsk = -(jnp.exp2(_sink_column(sink_ref, kv, tq) - lse) * delta)
        for g in range(4):
            dq_ref[:, g * HEAD:(g + 1) * HEAD] = dq[g * tq:(g + 1) * tq]
            dsink_ref[0, g:g + 1, :] += jnp.broadcast_to(_colsum(dsk[g * tq:(g + 1) * tq]), (1, HEAD))
        dk_ref[pl.ds(start, 3 * tq), :] += dk_acc[:3 * tq]
        dv_ref[pl.ds(start, 3 * tq), :] += dv_acc[:3 * tq]
        dk_ref[n:na, :] += dk_acc[3 * tq:]
        dv_ref[n:na, :] += dv_acc[3 * tq:]

    blk = pl.BlockSpec((tq, 4 * HEAD), lambda kv, i: (i, kv))
    kvout = pl.BlockSpec((na, HEAD), lambda kv, i: (0, kv))
    return _pallas(
        body, name="attn_window_bwd", grid=(2, n // tq),
        out_shape=(jax.ShapeDtypeStruct((n, 8 * HEAD), F32), jax.ShapeDtypeStruct((na, 2 * HEAD), F32),
                   jax.ShapeDtypeStruct((na, 2 * HEAD), F32), jax.ShapeDtypeStruct((2, 8, HEAD), F32)),
        in_specs=[pl.BlockSpec(memory_space=pltpu.SMEM), blk,
                  pl.BlockSpec((na, HEAD), lambda kv, i: (0, 8 + kv)),
                  pl.BlockSpec((na, HEAD), lambda kv, i: (0, 10 + kv)),
                  blk, blk, blk],
        out_specs=(blk, kvout, kvout, pl.BlockSpec((1, 8, HEAD), lambda kv, i: (kv, 0, 0))),
        compiler_params=_params(_mb(40)),
    )(sink, t_all, t_all, t_all, o, do, lse)


def _attn_global_bwd(t_all, o, do, lse):
    na = t_all.shape[0]
    n = na - CTX
    tq = 256

    def body(q_ref, k_ref, v_ref, o_ref, do_ref, lse_ref, dq_ref, dk_ref, dv_ref):
        @pl.when(pl.program_id(1) == 0)
        def _():
            dk_ref[...] = jnp.zeros_like(dk_ref)
            dv_ref[...] = jnp.zeros_like(dv_ref)

        kk, vv = k_ref[...], v_ref[...]
        dk_acc = jnp.zeros((na, HEAD), F32)
        dv_acc = jnp.zeros((na, HEAD), F32)
        for g in range(4):
            q = q_ref[:, g * HEAD:(g + 1) * HEAD]
            t = lax.dot_general(q, kk, _NT, preferred_element_type=F32) * QK_LOG2
            p = jnp.exp2(t - lse_ref[:, g * HEAD:g * HEAD + 1])
            dof = do_ref[:, g * HEAD:(g + 1) * HEAD]
            delta = jnp.sum(dof * o_ref[:, g * HEAD:(g + 1) * HEAD], axis=-1, keepdims=True)
            dob = dof.astype(BF16)
            dv_acc = dv_acc + lax.dot_general(p.astype(BF16), dob, _TN, preferred_element_type=F32)
            dp = lax.dot_general(dob, vv, _NT, preferred_element_type=F32)
            ds = (p * (dp - delta) * SCALE).astype(BF16)
            dq_ref[:, g * HEAD:(g + 1) * HEAD] = jnp.dot(ds, kk, preferred_element_type=F32)
            dk_acc = dk_acc + lax.dot_general(ds, q, _TN, preferred_element_type=F32)
        dk_ref[...] += dk_acc
        dv_ref[...] += dv_acc

    ospec = pl.BlockSpec((tq, 4 * HEAD), lambda kv, i: (i, 2 + kv))
    lspec = pl.BlockSpec((tq, 4 * HEAD), lambda kv, i: (i, kv))
    kvout = pl.BlockSpec((na, HEAD), lambda kv, i: (0, kv))
    return _pallas(
        body, name="attn_global_bwd", grid=(2, n // tq),
        out_shape=(jax.ShapeDtypeStruct((n, 8 * HEAD), F32), jax.ShapeDtypeStruct((na, 2 * HEAD), F32),
                   jax.ShapeDtypeStruct((na, 2 * HEAD), F32)),
        in_specs=[pl.BlockSpec((tq, 4 * HEAD), lambda kv, i: (i, 3 + kv)),
                  pl.BlockSpec((na, HEAD), lambda kv, i: (0, 20 + kv)),
                  pl.BlockSpec((na, HEAD), lambda kv, i: (0, 22 + kv)),
                  ospec, ospec, lspec],
        out_specs=(lspec, kvout, kvout),
        compiler_params=_params(_mb(56)),
    )(t_all, t_all, t_all, o, do, lse)


def _outproj_ln1(o, wout, x, g1, lg, lb, sc2, sh2, after):
    n, d = x.shape
    tm = 256

    def body(o_ref, w_ref, x_ref, g1_ref, lg_ref, lb_ref, sc_ref, sh_ref, after_ref, a_ref, xh_ref, rs_ref, u_ref):
        a1 = jnp.dot(o_ref[...].astype(BF16), w_ref[...], preferred_element_type=F32)
        a_ref[...] = a1
        r = ALPHA * x_ref[...] + g1_ref[...] * a1
        dlt = r - _rowmean(r)
        rstd = lax.rsqrt(_rowmean(dlt * dlt) + EPS)
        xh = dlt * rstd
        xh_ref[...] = xh
        rs_ref[...] = rstd
        x1 = xh * lg_ref[...] + lb_ref[...]
        u_ref[...] = (x1 * (1.0 + sc_ref[...]) + sh_ref[...]).astype(BF16)

    row = lambda i: (i, 0)
    const2 = lambda i: (0, 0)
    vec = pl.BlockSpec((1, d), const2)
    big = pl.BlockSpec((tm, d), row)
    return _pallas(
        body, name="outproj_ln1", grid=(n // tm,),
        out_shape=(jax.ShapeDtypeStruct((n, d), F32), jax.ShapeDtypeStruct((n, d), F32),
                   jax.ShapeDtypeStruct((n, 1), F32), jax.ShapeDtypeStruct((n, d), BF16)),
        in_specs=[big, pl.BlockSpec((d, d), const2), big, vec, vec, vec, vec, vec, _ANY],
        out_specs=(big, big, pl.BlockSpec((tm, 1), row), big),
        compiler_params=_params(_mb(56)),
    )(o, wout, x, g1, lg, lb, sc2, sh2, after)


def _ffn_up(u2, wg_g, wu_g, after):
    n, d = u2.shape
    tm = min(1024, n)
    f = NDEV * FFN_PAD

    def body(u_ref, wg_ref, wu_ref, after_ref, g_ref, p_ref, hf_ref):
        u = u_ref[...]
        gv = jnp.dot(u, wg_ref[0], preferred_element_type=F32)
        pv = jnp.dot(u, wu_ref[0], preferred_element_type=F32)
        g_ref[...] = gv.astype(BF16)
        p_ref[...] = pv.astype(BF16)
        hf_ref[...] = (gv * _sigmoid(gv) * pv).astype(BF16)

    tile = pl.BlockSpec((tm, FFN_PAD), lambda i, j: (i, j))
    wspec = pl.BlockSpec((1, d, FFN_PAD), lambda i, j: (j, 0, 0))
    sds = jax.ShapeDtypeStruct((n, f), BF16)
    return _pallas(
        body, name="ffn_up", grid=(n // tm, NDEV),
        out_shape=(sds, sds, sds),
        in_specs=[pl.BlockSpec((tm, d), lambda i, j: (i, 0)), wspec, wspec, _ANY],
        out_specs=(tile, tile, tile),
        compiler_params=_params(_mb(48)),
    )(u2, wg_g, wu_g, after)


def _ffn_down(hf, wd):
    n, f = hf.shape
    d = wd.shape[1]
    tm, tn = min(1024, n), 512

    def body(h_ref, w_ref, o_ref):
        o_ref[...] = jnp.dot(h_ref[...], w_ref[...], preferred_element_type=F32)

    return _pallas(
        body, name="ffn_down", grid=(n // tm, d // tn),
        out_shape=jax.ShapeDtypeStruct((n, d), F32),
        in_specs=[pl.BlockSpec((tm, f), lambda i, j: (i, 0)), pl.BlockSpec((f, tn), lambda i, j: (0, j))],
        out_specs=pl.BlockSpec((tm, tn), lambda i, j: (i, j)),
        compiler_params=_params(_mb(56)),
    )(hf, wd)


def _ln2_loss(xh1, ffn, tgt, lg1, lb1, g2, lg2, lb2):
    n, d = xh1.shape
    tm = 256

    def body(xh_ref, f_ref, t_ref, lg1_ref, lb1_ref, g2_ref, lg2_ref, lb2_ref, dr_ref, df_ref, loss_ref, acc_ref):
        @pl.when(pl.program_id(0) == 0)
        def _():
            loss_ref[...] = jnp.zeros_like(loss_ref)
            acc_ref[...] = jnp.zeros_like(acc_ref)

        x1 = xh_ref[...] * lg1_ref[...] + lb1_ref[...]
        fv = f_ref[...]
        r = ALPHA * x1 + g2_ref[...] * fv
        dlt = r - _rowmean(r)
        rstd = lax.rsqrt(_rowmean(dlt * dlt) + EPS)
        xh2 = dlt * rstd
        err = xh2 * lg2_ref[...] + lb2_ref[...] - t_ref[...]
        loss_ref[...] += 0.5 * jnp.sum(_rowmean(err * err))
        dy = err * (1.0 / d)
        dyg = dy * lg2_ref[...]
        dr = rstd * (dyg - _rowmean(dyg) - xh2 * _rowmean(dyg * xh2))
        dr_ref[...] = dr
        df_ref[...] = (g2_ref[...] * dr).astype(BF16)
        acc_ref[0:1, :] += _colsum(dy * xh2)
        acc_ref[1:2, :] += _colsum(dy)
        acc_ref[2:3, :] += _colsum(dr * fv)

    row = lambda i: (i, 0)
    const2 = lambda i: (0, 0)
    vec = pl.BlockSpec((1, d), const2)
    big = pl.BlockSpec((tm, d), row)
    return _pallas(
        body, name="ln2_loss", grid=(n // tm,),
        out_shape=(jax.ShapeDtypeStruct((n, d), F32), jax.ShapeDtypeStruct((n, d), BF16),
                   jax.ShapeDtypeStruct((8, HEAD), F32), jax.ShapeDtypeStruct((8, d), F32)),
        in_specs=[big, big, big, vec, vec, vec, vec, vec],
        out_specs=(big, big, pl.BlockSpec((8, HEAD), const2), pl.BlockSpec((8, d), const2)),
        compiler_params=_params(_mb(48)),
    )(xh1, ffn, tgt, lg1, lb1, g2, lg2, lb2)


def _ffn_dhf(df, wd_g, gmat, pmat):
    n, d = df.shape
    f = gmat.shape[1]
    tm = min(1024, n)

    def body(df_ref, w_ref, g_ref, p_ref, dg_ref, dp_ref):
        dhf = lax.dot_general(df_ref[...], w_ref[0], _NT, preferred_element_type=F32)
        gv = g_ref[...].astype(F32)
        sg = _sigmoid(gv)
        dp_ref[...] = (dhf * (gv * sg)).astype(BF16)
        dg_ref[...] = (dhf * p_ref[...].astype(F32) * (sg * (1.0 + gv * (1.0 - sg)))).astype(BF16)

    tile = pl.BlockSpec((tm, FFN_PAD), lambda i, j: (i, j))
    sds = jax.ShapeDtypeStruct((n, f), BF16)
    return _pallas(
        body, name="ffn_dhf", grid=(n // tm, NDEV),
        out_shape=(sds, sds),
        in_specs=[pl.BlockSpec((tm, d), lambda i, j: (i, 0)), pl.BlockSpec((1, FFN_PAD, d), lambda i, j: (j, 0, 0)),
                  tile, tile],
        out_specs=(tile, tile),
        compiler_params=_params(_mb(48)),
    )(df, wd_g, gmat, pmat)


def _ffn_du2(dg, dp, wg_g, wu_g, after):
    n, f = dg.shape
    d = wg_g.shape[1]
    tm = min(1024, n)

    def body(dg_ref, dp_ref, wg_ref, wu_ref, after_ref, o_ref):
        part = (lax.dot_general(dg_ref[...], wg_ref[0], _NT, preferred_element_type=F32)
                + lax.dot_general(dp_ref[...], wu_ref[0], _NT, preferred_element_type=F32))

        @pl.when(pl.program_id(1) == 0)
        def _():
            o_ref[...] = part

        @pl.when(pl.program_id(1) > 0)
        def _():
            o_ref[...] += part

    tile = lambda i, j: (i, j)
    wspec = pl.BlockSpec((1, d, FFN_PAD), lambda i, j: (j, 0, 0))
    return _pallas(
        body, name="ffn_du2", grid=(n // tm, NDEV),
        out_shape=jax.ShapeDtypeStruct((n, d), F32),
        in_specs=[pl.BlockSpec((tm, FFN_PAD), tile), pl.BlockSpec((tm, FFN_PAD), tile), wspec, wspec, _ANY],
        out_specs=pl.BlockSpec((tm, d), lambda i, j: (i, 0)),
        compiler_params=_params(_mb(48)),
    )(dg, dp, wg_g, wu_g, after)


def _ln1_bwd(du2, dr2, xh1, rs1, a1, lg1, lb1, sc2, g1):
    n, d = du2.shape
    tm = 256

    def body(du_ref, dr2_ref, xh_ref, rs_ref, a_ref, lg_ref, lb_ref, sc_ref, g1_ref, dr1_ref, da_ref, acc_ref):
        @pl.when(pl.program_id(0) == 0)
        def _():
            acc_ref[...] = jnp.zeros_like(acc_ref)

        du = du_ref[...]
        xh = xh_ref[...]
        x1 = xh * lg_ref[...] + lb_ref[...]
        dx1 = ALPHA * dr2_ref[...] + du * (1.0 + sc_ref[...])
        dxg = dx1 * lg_ref[...]
        dr1 = rs_ref[...] * (dxg - _rowmean(dxg) - xh * _rowmean(dxg * xh))
        dr1_ref[...] = dr1
        da_ref[...] = (g1_ref[...] * dr1).astype(BF16)
        acc_ref[0:1, :] += _colsum(du * x1)
        acc_ref[1:2, :] += _colsum(du)
        acc_ref[2:3, :] += _colsum(dx1 * xh)
        acc_ref[3:4, :] += _colsum(dx1)
        acc_ref[4:5, :] += _colsum(dr1 * a_ref[...])

    row = lambda i: (i, 0)
    const2 = lambda i: (0, 0)
    vec = pl.BlockSpec((1, d), const2)
    big = pl.BlockSpec((tm, d), row)
    return _pallas(
        body, name="ln1_bwd", grid=(n // tm,),
        out_shape=(jax.ShapeDtypeStruct((n, d), F32), jax.ShapeDtypeStruct((n, d), BF16),
                   jax.ShapeDtypeStruct((8, d), F32)),
        in_specs=[big, big, big, pl.BlockSpec((tm, 1), row), big, vec, vec, vec, vec],
        out_specs=(big, big, pl.BlockSpec((8, d), const2)),
        compiler_params=_params(_mb(48)),
    )(du2, dr2, xh1, rs1, a1, lg1, lb1, sc2, g1)


def _dw_cols(a, b, nblk, bw, tm, after, name):
    m, k = a.shape

    def body(a_ref, b_ref, after_ref, o_ref, acc_ref):
        part = lax.dot_general(a_ref[...], b_ref[...], _TN, preferred_element_type=F32)
        i = pl.program_id(1)

        @pl.when(i == 0)
        def _():
            acc_ref[...] = part

        @pl.when(i > 0)
        def _():
            acc_ref[...] += part

        @pl.when(i == pl.num_programs(1) - 1)
        def _():
            o_ref[0] = acc_ref[...].astype(BF16)

    return _pallas(
        body, name=name, grid=(nblk, m // tm),
        out_shape=jax.ShapeDtypeStruct((nblk, k, bw), BF16),
        in_specs=[pl.BlockSpec((tm, k), lambda j, i: (i, 0)), pl.BlockSpec((tm, bw), lambda j, i: (i, j)), _ANY],
        out_specs=pl.BlockSpec((1, k, bw), lambda j, i: (j, 0, 0)),
        scratch_shapes=[pltpu.VMEM((k, bw), F32)],
        compiler_params=_params(_mb(48)),
    )(a, b, after)


def _dw_in(u, dh, after):
    m, k = u.shape
    tm = CTX
    half = 4 * IN_SHARD

    def body(u_ref, dh_ref, after_ref, o_ref, acc_ref):
        part = lax.dot_general(u_ref[...], dh_ref[...], _TN, preferred_element_type=F32)
        i = pl.program_id(1)

        @pl.when(i == 0)
        def _():
            acc_ref[...] = part

        @pl.when(i > 0)
        def _():
            acc_ref[...] += part

        @pl.when(i == pl.num_programs(1) - 1)
        def _():
            for jj in range(4):
                o_ref[jj] = acc_ref[:, jj * IN_SHARD:(jj + 1) * IN_SHARD].astype(BF16)

    return _pallas(
        body, name="dw_in", grid=(2, m // tm),
        out_shape=jax.ShapeDtypeStruct((NDEV, k, IN_SHARD), BF16),
        in_specs=[pl.BlockSpec((tm, k), lambda jh, i: (i, 0)), pl.BlockSpec((tm, half), lambda jh, i: (i, jh)), _ANY],
        out_specs=pl.BlockSpec((4, k, IN_SHARD), lambda jh, i: (jh, 0, 0)),
        scratch_shapes=[pltpu.VMEM((k, half), F32)],
        compiler_params=_params(_mb(48)),
    )(u, dh, after)


def _dw_rows(a, b, nblk, bw, tm, name):
    m = a.shape[0]
    nn = b.shape[1]

    def body(a_ref, b_ref, o_ref, acc_ref):
        part = lax.dot_general(a_ref[...].astype(BF16), b_ref[...], _TN, preferred_element_type=F32)
        i = pl.program_id(1)

        @pl.when(i == 0)
        def _():
            acc_ref[...] = part

        @pl.when(i > 0)
        def _():
            acc_ref[...] += part

        @pl.when(i == pl.num_programs(1) - 1)
        def _():
            o_ref[0] = acc_ref[...].astype(BF16)

    return _pallas(
        body, name=name, grid=(nblk, m // tm),
        out_shape=jax.ShapeDtypeStruct((nblk, bw, nn), BF16),
        in_specs=[pl.BlockSpec((tm, bw), lambda j, i: (i, j)), pl.BlockSpec((tm, nn), lambda j, i: (i, 0))],
        out_specs=pl.BlockSpec((1, bw, nn), lambda j, i: (j, 0, 0)),
        scratch_shapes=[pltpu.VMEM((bw, nn), F32)],
        compiler_params=_params(_mb(48)),
    )(a, b)


def _outproj_bwd(da1, wout, after):
    n, d = da1.shape
    tm = 512

    def body(a_ref, w_ref, after_ref, o_ref):
        o_ref[...] = lax.dot_general(a_ref[...], w_ref[...], _NT, preferred_element_type=F32)

    return _pallas(
        body, name="outproj_bwd", grid=(n // tm,),
        out_shape=jax.ShapeDtypeStruct((n, d), F32),
        in_specs=[pl.BlockSpec((tm, d), lambda i: (i, 0)), pl.BlockSpec((d, d), lambda i: (0, 0)), _ANY],
        out_specs=pl.BlockSpec((tm, d), lambda i: (i, 0)),
        compiler_params=_params(_mb(48)),
    )(da1, wout, after)


def _qkv_bwd(dh, win_g, x, ct, dr1, sc):
    na, wcols = dh.shape
    n, d = x.shape
    tm = CTX
    nlat = n // tm

    def body(dh_ref, w_ref, x_ref, ct_ref, dr_ref, sc_ref, gx_ref, acc_ref):
        i = pl.program_id(0)

        @pl.when(i == 0)
        def _():
            acc_ref[...] = jnp.zeros_like(acc_ref)

        du = jnp.zeros((tm, d), F32)
        for j in range(NDEV):
            du = du + lax.dot_general(dh_ref[:, j * IN_SHARD:(j + 1) * IN_SHARD], w_ref[j], _NT,
                                      preferred_element_type=F32)

        @pl.when(i < nlat)
        def _():
            gx_ref[...] = ALPHA * dr_ref[...] + du * (1.0 + sc_ref[0])
            acc_ref[0:1, :] += _colsum(du)
            acc_ref[1:2, :] += _colsum(du * x_ref[...])

        @pl.when(i == nlat)
        def _():
            acc_ref[2:3, :] += _colsum(du)
            acc_ref[3:4, :] += _colsum(du * ct_ref[...])

    lat = lambda i: (jnp.minimum(i, nlat - 1), 0)
    const2 = lambda i: (0, 0)
    return _pallas(
        body, name="qkv_bwd", grid=(nlat + 1,),
        out_shape=(jax.ShapeDtypeStruct((n, d), F32), jax.ShapeDtypeStruct((8, d), F32)),
        in_specs=[pl.BlockSpec((tm, wcols), lambda i: (i, 0)), pl.BlockSpec((NDEV, d, IN_SHARD), lambda i: (0, 0, 0)),
                  pl.BlockSpec((tm, d), lat), pl.BlockSpec((tm, d), const2), pl.BlockSpec((tm, d), lat),
                  pl.BlockSpec((1, 1, d), lambda i: (0, 0, 0))],
        out_specs=(pl.BlockSpec((tm, d), lat), pl.BlockSpec((8, d), const2)),
        compiler_params=_params(_mb(56)),
    )(dh, win_g, x, ct, dr1, sc)


def _adam_math(w, g, m, v):
    m2 = ADAM_B1 * m + (1.0 - ADAM_B1) * g
    v2 = ADAM_B2 * v + (1.0 - ADAM_B2) * (g * g)
    m_hat = m2 / (1.0 - ADAM_B1 ** ADAM_STEP)
    v_hat = v2 / (1.0 - ADAM_B2 ** ADAM_STEP)
    delta = -ADAM_LR * (m_hat / (jnp.sqrt(v_hat) + ADAM_EPS) + ADAM_WD * w)
    return delta, m2, v2


def _adamw(w, gsrc, m, v, name):
    r, c = w.shape
    parts = gsrc.ndim == 3
    cg = gsrc.shape[-1]
    tr = r
    while tr * c * 4 > _mb(1) and tr % 32 == 0:
        tr //= 2

    def body(w_ref, g_ref, m_ref, v_ref, go_ref, d_ref, mo_ref, vo_ref):
        if parts:
            g = g_ref[0].astype(F32)
            for s in range(1, NDEV):
                g = g + g_ref[s].astype(F32)
            g = g[:, :c]
        else:
            g = g_ref[...]
        delta, m2, v2 = _adam_math(w_ref[...], g, m_ref[...], v_ref[...])
        go_ref[...] = g
        d_ref[...] = delta
        mo_ref[...] = m2
        vo_ref[...] = v2

    tile = pl.BlockSpec((tr, c), lambda i: (i, 0))
    gspec = pl.BlockSpec((NDEV, tr, cg), lambda i: (0, i, 0)) if parts else tile
    sds = jax.ShapeDtypeStruct((r, c), F32)
    return _pallas(
        body, name=name, grid=(r // tr,),
        out_shape=(sds, sds, sds, sds),
        in_specs=[tile, gspec, tile, tile],
        out_specs=(tile, tile, tile, tile),
        compiler_params=_params(_mb(48)),
    )(w, gsrc, m, v)


def _small_update(gath, dcc, cc, w_s, m_s, v_s):
    d = w_s.shape[1]

    def body(g_ref, dcc_ref, cc_ref, w_ref, m_ref, v_ref, go_ref, d_ref, mo_ref, vo_ref):
        s = g_ref[0]
        for b in range(1, NDEV):
            s = s + g_ref[b]
        dsl = dcc_ref[0, 8:9, :]
        for b in range(1, NDEV):
            dsl = dsl + dcc_ref[b, 8:9, :]
        cv = cc_ref[...]
        sg = _sigmoid(cv)
        go_ref[...] = jnp.zeros_like(go_ref)
        go_ref[0:1, :] = dsl * (sg * (1.0 + cv * (1.0 - sg)))
        go_ref[1:3, :] = s[0:2] + s[6:8]
        go_ref[3:7, :] = s[2:6]
        go_ref[7:12, :] = s[8:13]
        delta, m2, v2 = _adam_math(w_ref[...], go_ref[...], m_ref[...], v_ref[...])
        d_ref[...] = delta
        mo_ref[...] = m2
        vo_ref[...] = v2

    full = pl.BlockSpec((16, d), lambda: (0, 0))
    g3 = pl.BlockSpec((NDEV, 16, d), lambda: (0, 0, 0))
    sds = jax.ShapeDtypeStruct((16, d), F32)
    return _pallas(
        body, name="small_update",
        out_shape=(sds, sds, sds, sds),
        in_specs=[g3, g3, pl.BlockSpec((1, d), lambda: (0, 0)), full, full, full],
        out_specs=(full, full, full, full),
        compiler_params=_params(_mb(24)),
    )(gath, dcc, cc, w_s, m_s, v_s)


def _rope_tables(n):
    rows = n // GRID_W
    row_ids = jnp.repeat(jnp.arange(rows, dtype=F32), GRID_W)
    col_ids = jnp.tile(jnp.arange(GRID_W, dtype=F32), rows)
    axis_dim = HEAD // 2
    inv_freq = jnp.power(ROPE_THETA, -jnp.arange(0, axis_dim, 2, dtype=F32) / axis_dim)
    ang_r = row_ids[:, None] * inv_freq
    ang_c = col_ids[:, None] * inv_freq
    ang = jnp.concatenate([ang_r, ang_r, ang_c, ang_c], axis=-1)
    cos, sin = jnp.cos(ang), jnp.sin(ang)
    first = (jnp.arange(HEAD) % (HEAD // 2)) < HEAD // 4
    sa = jnp.where(first, -sin, 0.0)
    sb = jnp.where(first, 0.0, sin)
    ones = jnp.ones((CTX, HEAD), F32)
    zeros = jnp.zeros((CTX, HEAD), F32)
    return (jnp.concatenate([cos, ones], 0), jnp.concatenate([sa, zeros], 0), jnp.concatenate([sb, zeros], 0))


def _pad_cols(a, width):
    return jnp.pad(a, ((0, 0), (0, width - a.shape[1])))


def _pad_rows(a, rows):
    return jnp.pad(a, ((0, rows - a.shape[0]), (0, 0)))


def _pack_small(c_ctx, b_ada, ln1_g, ln1_b, ln2_g, ln2_b, qg, kg, sink, d):
    misc = _pad_cols(jnp.concatenate([qg, kg, sink], axis=1), d)
    rows = jnp.concatenate([c_ctx.reshape(1, d), b_ada.reshape(6, d), ln1_g, ln1_b, ln2_g, ln2_b, misc], axis=0)
    return _pad_rows(rows, 16)


def _unpack_small(p, d):
    return dict(c_ctx=p[0], b_ada=p[1:7].reshape(1, 6 * d), ln1_g=p[7:8], ln1_b=p[8:9], ln2_g=p[9:10], ln2_b=p[10:11],
                q_norm_g=p[11:12, 0:HEAD], k_norm_g=p[11:12, HEAD:2 * HEAD], sink_logit=p[11:12, 2 * HEAD:2 * HEAD + 8])


def kernel(x, c, ctx, c_ctx, w_ada, b_ada, w_in, q_norm_g, k_norm_g, sink_logit, w_out, ln1_g, ln1_b, w_gate, w_up, w_down, ln2_g, ln2_b, loss_target, m_c_ctx, m_w_ada, m_b_ada, m_w_in, m_q_norm_g, m_k_norm_g, m_sink_logit, m_w_out, m_ln1_g, m_ln1_b, m_w_gate, m_w_up, m_w_down, m_ln2_g, m_ln2_b, v_c_ctx, v_w_ada, v_b_ada, v_w_in, v_q_norm_g, v_k_norm_g, v_sink_logit, v_w_out, v_ln1_g, v_ln1_b, v_w_gate, v_w_up, v_w_down, v_ln2_g, v_ln2_b):
    xs, cts, tgt = x[0], ctx[0], loss_target[0]
    n, d = xs.shape
    assert cts.shape == (CTX, d) and w_in.shape[2] == IN_SHARD and w_gate.shape[2] == FFN_SHARD
    me = 4 * lax.axis_index("x") + 2 * lax.axis_index("y") + lax.axis_index("c")
    e_sh = w_ada.shape[2]

    c_g = _exchange(_pad_rows(c, 8), False, "gather_c")
    c_all = jnp.concatenate([c_g[:, 0, :], _pad_rows(c_ctx.reshape(1, d), 8)], axis=0)
    bias_sh = lax.dynamic_slice(b_ada, (0, me * e_sh), (1, e_sh))
    mods_g = _exchange(_ada_fwd(c_all, w_ada[0], bias_sh), False, "gather_mods")
    mods = jnp.transpose(mods_g, (1, 0, 2)).reshape(16, NDEV * e_sh)
    mine = lax.dynamic_slice(mods, (me, 0), (1, 6 * d))
    sh1, sc1, g1, sh2, sc2, g2 = [mine[:, k * d:(k + 1) * d] for k in range(6)]
    csh1, csc1 = mods[8:9, 0:d], mods[8:9, d:2 * d]
    sc_pair = jnp.stack([sc1, csc1])
    sh_pair = jnp.stack([sh1, csh1])

    h_win, tok = _exchange_start(w_in[0].astype(BF16), "gather", mods, "gather_w_in_start")
    h_wout, tok = _exchange_start(w_out[0].astype(BF16), "chip", tok, "gather_w_out_start")
    h_wg, tok = _exchange_start(_pad_cols(w_gate[0], FFN_PAD).astype(BF16), "chip", tok, "gather_w_gate_start")
    h_wu, tok = _exchange_start(_pad_cols(w_up[0], FFN_PAD).astype(BF16), "chip", tok, "gather_w_up_start")
    h_wd, tok = _exchange_start(_pad_rows(w_down[0], FFN_PAD).astype(BF16), "chip", tok, "gather_w_down_start")

    cos, sa, sb = _rope_tables(n)
    win_g = _exchange_wait(h_win, "gather", tok, "gather_w_in_wait")
    u_all, h_all, t_all = _qkv_fwd(xs, cts, sc_pair, sh_pair, win_g, q_norm_g, k_norm_g, cos, sa, sb)
    f_wout, tok = _forward_start(_exchange_wait(h_wout, "chip", t_all, "gather_w_out_wait"), t_all, "forward_w_out_start")
    o_a, lse_a = _attn_window_fwd(t_all, sink_logit, tok)
    o, lse_b = _attn_global_fwd(t_all, o_a)
    f_wg, tok = _forward_start(_exchange_wait(h_wg, "chip", o, "gather_w_gate_wait"), o, "forward_w_gate_start")
    f_wu, tok = _forward_start(_exchange_wait(h_wu, "chip", tok, "gather_w_up_wait"), tok, "forward_w_up_start")
    wout_g = _forward_wait(f_wout, tok, "forward_w_out_wait").reshape(d, d)
    a1, xh1, rs1, u2 = _outproj_ln1(o, wout_g, xs, g1, ln1_g, ln1_b, sc2, sh2, tok)
    f_wd, tok = _forward_start(_exchange_wait(h_wd, "chip", rs1, "gather_w_down_wait"), rs1, "forward_w_down_start")
    wg_g = _forward_wait(f_wg, tok, "forward_w_gate_wait")
    wu_g = _forward_wait(f_wu, tok, "forward_w_up_wait")
    gmat, pmat, hf = _ffn_up(u2, wg_g, wu_g, tok)
    wd_g = _forward_wait(f_wd, hf, "forward_w_down_wait")
    ffn = _ffn_down(hf, wd_g.reshape(NDEV * FFN_PAD, d))
    dr2, df, loss_p, acc2 = _ln2_loss(xh1, ffn, tgt, ln1_g, ln1_b, g2, ln2_g, ln2_b)
    loss = lax.psum(loss_p[0, 0], ("x", "y", "c"))

    tk = min(n, 2048)
    dgm, dpm = _ffn_dhf(df, wd_g, gmat, pmat)
    dwd_p = _dw_rows(hf, df, NDEV, FFN_PAD, tk, "dw_down")
    h_dwd, tok = _exchange_start(dwd_p, "scatter", loss.reshape(1, 1), "scatter_dw_down_start")
    dwg_p = _dw_cols(u2, dgm, NDEV, FFN_PAD, tk, tok, "dw_gate")
    h_dwg, tok = _exchange_start(dwg_p, "scatter", tok, "scatter_dw_gate_start")
    dwu_p = _dw_cols(u2, dpm, NDEV, FFN_PAD, tk, tok, "dw_up")
    h_dwu, tok = _exchange_start(dwu_p, "scatter", tok, "scatter_dw_up_start")
    du2 = _ffn_du2(dgm, dpm, wg_g, wu_g, tok)
    dr1, da1, acc1 = _ln1_bwd(du2, dr2, xh1, rs1, a1, ln1_g, ln1_b, sc2, g1)
    dwo_p = _dw_rows(o, da1, NDEV, 2 * HEAD, tk, "dw_out")
    h_dwo, tok = _exchange_start(dwo_p, "scatter", loss_p, "scatter_dw_out_start")
    do = _outproj_bwd(da1, wout_g, tok)
    dqa, dka, dva, dsink = _attn_window_bwd(t_all, o, do, lse_a, sink_logit)
    dqb, dkb, dvb = _attn_global_bwd(t_all, o, do, lse_b)
    dh_all, dnorm = _qkv_bwd_prep(dqa, dka, dva, dqb, dkb, dvb, h_all, q_norm_g, k_norm_g, cos, sa, sb)
    grad_x, acc0 = _qkv_bwd(dh_all, win_g, xs, cts, dr1, sc_pair)

    misc = _pad_cols(jnp.concatenate([dnorm[0:1], dnorm[1:2], dsink[:, 0:4, 0].reshape(1, 8)], axis=1), d)
    part = jnp.concatenate([
        acc0[0:2], acc1[4:5], acc1[1:2], acc1[0:1], acc2[2:3],
        acc0[2:4],
        acc1[2:4], acc2[0:2],
        misc, jnp.zeros((3, d), F32)], axis=0)
    gath = _exchange(part, False, "gather_small")
    dm_batch = gath[:, 0:6, :].reshape(NDEV, 6 * d)
    dm_ctx = _pad_cols(gath[:, 6:8, :].reshape(NDEV, 2 * d), 6 * d)
    dm16 = lax.dynamic_slice(jnp.concatenate([dm_batch, dm_ctx], axis=0), (0, me * e_sh), (16, e_sh))
    dw_ada, drow = _ada_bwd(dm16, c_all, w_ada[0])
    dcc = _exchange(drow, False, "gather_dcc")
    dwi_p = _dw_in(u_all, dh_all, dcc)
    h_dwi, tok = _exchange_start(dwi_p, "scatter", dcc, "scatter_dw_in_start")

    w_s = _pack_small(c_ctx, b_ada, ln1_g, ln1_b, ln2_g, ln2_b, q_norm_g, k_norm_g, sink_logit, d)
    m_s = _pack_small(m_c_ctx, m_b_ada, m_ln1_g, m_ln1_b, m_ln2_g, m_ln2_b, m_q_norm_g, m_k_norm_g, m_sink_logit, d)
    v_s = _pack_small(v_c_ctx, v_b_ada, v_ln1_g, v_ln1_b, v_ln2_g, v_ln2_b, v_q_norm_g, v_k_norm_g, v_sink_logit, d)
    small = [_unpack_small(p, d) for p in _small_update(gath, dcc, c_ctx.reshape(1, d), w_s, m_s, v_s)]

    big = {}
    big["w_ada"] = _adamw(w_ada[0], dw_ada, m_w_ada[0], v_w_ada[0], "adamw_w_ada")
    late = tok
    big["w_down"] = _adamw(w_down[0], _exchange_wait(h_dwd, "scatter", late, "scatter_dw_down_wait"), m_w_down[0],
                           v_w_down[0], "adamw_w_down")
    for nm, wt, mt, vt, hd in (("w_gate", w_gate, m_w_gate, v_w_gate, h_dwg), ("w_up", w_up, m_w_up, v_w_up, h_dwu)):
        big[nm] = _adamw(wt[0], _exchange_wait(hd, "scatter", late, "scatter_d" + nm + "_wait"), mt[0], vt[0],
                         "adamw_" + nm)
    big["w_out"] = _adamw(w_out[0], _exchange_wait(h_dwo, "scatter", late, "scatter_dw_out_wait"), m_w_out[0], v_w_out[0],
                          "adamw_w_out")
    big["w_in"] = _adamw(w_in[0], _exchange_wait(h_dwi, "scatter", big["w_out"][1], "scatter_dw_in_wait"), m_w_in[0],
                         v_w_in[0], "adamw_w_in")

    names = ["c_ctx", "w_ada", "b_ada", "w_in", "q_norm_g", "k_norm_g", "sink_logit", "w_out", "ln1_g", "ln1_b",
             "w_gate", "w_up", "w_down", "ln2_g", "ln2_b"]
    outs = [loss, grad_x[None]]
    for k in range(4):
        for nm in names:
            outs.append(big[nm][k][None] if nm in big else small[k][nm])
    return tuple(outs)
```

```python
import functools

import jax
import jax.numpy as jnp
from jax import lax
from jax.experimental import pallas as pl
from jax.experimental.pallas import tpu as pltpu

F32 = jnp.float32
BF16 = jnp.bfloat16

NDEV = 8
HEAD = 128
CTX = 256
GRID_W = 64
WINDOW = 128
ROPE_THETA = 10000.0
EPS = 1e-6
SCALE = HEAD ** -0.5
LOG2E = 1.4426950408889634
QK_LOG2 = SCALE * LOG2E
ALPHA = 2.0 ** 0.25
FFN_SHARD = 704
FFN_TILE = 512
FFN_PAIR = 2 * FFN_SHARD
IN_SHARD = 384
NEG = -1e30

ADAM_LR = 0.001
ADAM_B1 = 0.9
ADAM_B2 = 0.999
ADAM_EPS = 1e-08
ADAM_WD = 0.01
ADAM_STEP = 10

VMEM_CAP = 56 * 1024 * 1024

_KINDS = ["rope"] * 10 + ["none"] * 2 + ["qnorm"] * 8 + ["knorm"] * 2 + ["none"] * 2

_NT = (((1,), (1,)), ((), ()))
_TN = (((0,), (0,)), ((), ()))


def _pallas(body, **kw):
    return pl.pallas_call(body, **kw)


def _params(vmem_bytes):
    return pltpu.CompilerParams(vmem_limit_bytes=int(min(VMEM_CAP, vmem_bytes)))


def _mb(n):
    return int(n * 1024 * 1024)


def _sigmoid(x):
    return 1.0 / (1.0 + jnp.exp(-x))


def _colsum(a):
    return jnp.sum(a, axis=0, keepdims=True)


def _rowmean(a):
    return jnp.mean(a, axis=-1, keepdims=True)


def _exchange(src, scatter, name, after=None):
    blk = src.shape[1:] if scatter else src.shape
    after = src if after is None else after

    def body(src_ref, after_ref, out_ref, send_sems, recv_sems, local_sem):
        x, y, c = lax.axis_index("x"), lax.axis_index("y"), lax.axis_index("c")
        me = 4 * x + 2 * y + c
        copies = []
        for t in range(1, NDEV):
            px = 1 - x if (t >> 2) & 1 else x
            py = 1 - y if (t >> 1) & 1 else y
            pc = 1 - c if t & 1 else c
            peer = 4 * px + 2 * py + pc
            cp = pltpu.make_async_remote_copy(
                src_ref=src_ref.at[peer] if scatter else src_ref,
                dst_ref=out_ref.at[me],
                send_sem=send_sems.at[t - 1],
                recv_sem=recv_sems.at[t - 1],
                device_id=(px, py, pc),
                device_id_type=pl.DeviceIdType.MESH,
            )
            cp.start()
            copies.append(cp)
        own = pltpu.make_async_copy(src_ref.at[me] if scatter else src_ref, out_ref.at[me], local_sem)
        own.start()
        for cp in copies:
            cp.wait()
        own.wait()

    return _pallas(
        body, name=name,
        out_shape=jax.ShapeDtypeStruct((NDEV,) + tuple(blk), src.dtype),
        in_specs=[pl.BlockSpec(memory_space=pl.ANY), pl.BlockSpec(memory_space=pl.ANY)],
        out_specs=pl.BlockSpec(memory_space=pl.ANY),
        scratch_shapes=[pltpu.SemaphoreType.DMA((NDEV - 1,)), pltpu.SemaphoreType.DMA((NDEV - 1,)),
                        pltpu.SemaphoreType.DMA(())],
    )(src, after)


_HBM = pl.BlockSpec(memory_space=pltpu.HBM)
_SEM = pl.BlockSpec(memory_space=pltpu.SEMAPHORE)
_ANY = pl.BlockSpec(memory_space=pl.ANY)
_EFFECT = pltpu.SideEffectType.DATAFLOW_SIDE_EFFECTING


def _exchange_copies(src_ref, land_ref, send_sems, recv_sems, mode):
    x, y, c = lax.axis_index("x"), lax.axis_index("y"), lax.axis_index("c")
    me = 4 * x + 2 * y + c
    scatter = mode == "scatter"
    copies = []
    for t in ((1, 2, 4, 6) if mode == "chip" else range(1, NDEV)):
        px = 1 - x if (t >> 2) & 1 else x
        py = 1 - y if (t >> 1) & 1 else y
        pc = 1 - c if t & 1 else c
        peer = 4 * px + 2 * py + pc
        copies.append(pltpu.make_async_remote_copy(
            src_ref=src_ref.at[peer] if scatter else src_ref,
            dst_ref=land_ref.at[me],
            send_sem=send_sems.at[t - 1],
            recv_sem=recv_sems.at[t - 1],
            device_id=(px, py, pc),
            device_id_type=pl.DeviceIdType.MESH,
        ))
    own = pltpu.make_async_copy(src_ref.at[me] if scatter else src_ref, land_ref.at[me], send_sems.at[NDEV - 1])
    return copies, own


def _forward_copies(land_ref, send_sems, recv_sems):
    x, y, c = lax.axis_index("x"), lax.axis_index("y"), lax.axis_index("c")
    copies = []
    for k, t in enumerate((2, 4, 6)):
        px = 1 - x if (t >> 2) & 1 else x
        py = 1 - y if (t >> 1) & 1 else y
        mine, theirs = 4 * px + 2 * py + c, 4 * px + 2 * py + (1 - c)
        send = pltpu.make_async_remote_copy(
            src_ref=land_ref.at[mine], dst_ref=land_ref.at[mine], send_sem=send_sems.at[k], recv_sem=recv_sems.at[k],
            device_id=(x, y, 1 - c), device_id_type=pl.DeviceIdType.MESH)
        recv = pltpu.make_async_remote_copy(
            src_ref=land_ref.at[theirs], dst_ref=land_ref.at[theirs], send_sem=send_sems.at[k], recv_sem=recv_sems.at[k],
            device_id=(x, y, 1 - c), device_id_type=pl.DeviceIdType.MESH)
        copies.append((send, recv))
    return copies


def _forward_start(land, after, name):
    def body(land_ref, after_ref, send_sems, recv_sems, land_thru, token):
        for send, _ in _forward_copies(land_ref, send_sems, recv_sems):
            send.start()
        token[...] = jnp.zeros_like(token)

    res = _pallas(
        body, name=name,
        out_shape=(pltpu.SemaphoreType.DMA((3,)), pltpu.SemaphoreType.DMA((3,)), pltpu.HBM(land.shape, land.dtype),
                   jax.ShapeDtypeStruct((8, HEAD), F32)),
        in_specs=(_HBM, _ANY), out_specs=(_SEM, _SEM, _HBM, pl.BlockSpec(memory_space=pltpu.VMEM)),
        input_output_aliases={0: 2},
        compiler_params=pltpu.CompilerParams(has_side_effects=_EFFECT),
    )(land, after)
    return res[:3], res[3]


def _forward_wait(handle, after, name):
    send_sems, recv_sems, land_thru = handle

    def body(land_ref, send_sems, recv_sems, after_ref, got_ref):
        for send, recv in _forward_copies(land_ref, send_sems, recv_sems):
            send.wait_send()
            recv.wait_recv()

    return _pallas(
        body, name=name,
        out_shape=pltpu.HBM(land_thru.shape, land_thru.dtype),
        in_specs=(_HBM, _SEM, _SEM, _ANY), out_specs=_HBM,
        input_output_aliases={0: 0},
        compiler_params=pltpu.CompilerParams(has_side_effects=_EFFECT),
    )(land_thru, send_sems, recv_sems, after)


def _exchange_start(src, mode, after, name):
    blk = src.shape[1:] if mode == "scatter" else src.shape
    land = lax.empty((NDEV,) + tuple(blk), src.dtype)

    def body(src_ref, land_ref, after_ref, send_sems, recv_sems, src_thru, land_thru, token):
        copies, own = _exchange_copies(src_ref, land_ref, send_sems, recv_sems, mode)
        for cp in copies:
            cp.start()
        own.start()
        token[...] = jnp.zeros_like(token)

    res = _pallas(
        body, name=name,
        out_shape=(pltpu.SemaphoreType.DMA((NDEV,)), pltpu.SemaphoreType.DMA((NDEV,)),
                   pltpu.HBM(src.shape, src.dtype), pltpu.HBM(land.shape, land.dtype),
                   jax.ShapeDtypeStruct((8, HEAD), F32)),
        in_specs=(_HBM, _HBM, _ANY), out_specs=(_SEM, _SEM, _HBM, _HBM, pl.BlockSpec(memory_space=pltpu.VMEM)),
        input_output_aliases={0: 2, 1: 3},
        compiler_params=pltpu.CompilerParams(has_side_effects=_EFFECT),
    )(pltpu.with_memory_space_constraint(src, pltpu.HBM), pltpu.with_memory_space_constraint(land, pltpu.HBM), after)
    return res[:4], res[4]


def _exchange_wait(handle, mode, after, name):
    send_sems, recv_sems, src_thru, land_thru = handle

    def body(src_ref, land_ref, send_sems, recv_sems, after_ref, src_dead, got_ref):
        copies, own = _exchange_copies(src_ref, land_ref, send_sems, recv_sems, mode)
        for cp in copies:
            cp.wait_send()
            cp.wait_recv()
        own.wait()

    return _pallas(
        body, name=name,
        out_shape=(pltpu.HBM(src_thru.shape, src_thru.dtype), pltpu.HBM(land_thru.shape, land_thru.dtype)),
        in_specs=(_HBM, _HBM, _SEM, _SEM, _ANY), out_specs=(_HBM, _HBM),
        input_output_aliases={0: 0, 1: 1},
        compiler_params=pltpu.CompilerParams(has_side_effects=_EFFECT),
    )(src_thru, land_thru, send_sems, recv_sems, after)[1]


def _ada_fwd(c_all, w, bias):
    r, d = c_all.shape
    e = w.shape[1]
    tn = 512

    def body(c_ref, w_ref, b_ref, o_ref):
        cv = c_ref[...]
        s = (cv * _sigmoid(cv)).astype(BF16)
        o_ref[...] = jnp.dot(s, w_ref[...].astype(BF16), preferred_element_type=F32) + b_ref[...]

    return _pallas(
        body, name="ada_fwd", grid=(e // tn,),
        out_shape=jax.ShapeDtypeStruct((r, e), F32),
        in_specs=[pl.BlockSpec((r, d), lambda j: (0, 0)), pl.BlockSpec((d, tn), lambda j: (0, j)),
                  pl.BlockSpec((1, tn), lambda j: (0, j))],
        out_specs=pl.BlockSpec((r, tn), lambda j: (0, j)),
        compiler_params=_params(_mb(24)),
    )(c_all, w, bias)


def _ada_bwd(dm16, c_all, w):
    d, e = w.shape
    tn = 512

    def body(dm_ref, c_ref, w_ref, dw_ref, dr_ref):
        j = pl.program_id(0)
        dm = dm_ref[...]
        rid = lax.broadcasted_iota(jnp.int32, dm.shape, 0)
        ctx_sum = jnp.sum(jnp.where(rid >= 8, dm, 0.0), axis=0, keepdims=True)
        rows = jnp.where(rid < 8, dm, jnp.where(rid == 8, jnp.broadcast_to(ctx_sum, dm.shape), 0.0)).astype(BF16)
        cv = c_ref[...]
        s = (cv * _sigmoid(cv)).astype(BF16)
        dw_ref[...] = lax.dot_general(s, rows, _TN, preferred_element_type=F32)
        part = lax.dot_general(rows, w_ref[...].astype(BF16), _NT, preferred_element_type=F32)

        @pl.when(j == 0)
        def _():
            dr_ref[...] = part

        @pl.when(j > 0)
        def _():
            dr_ref[...] += part

    return _pallas(
        body, name="ada_bwd", grid=(e // tn,),
        out_shape=(jax.ShapeDtypeStruct((d, e), F32), jax.ShapeDtypeStruct((16, d), F32)),
        in_specs=[pl.BlockSpec((16, tn), lambda j: (0, j)), pl.BlockSpec((16, d), lambda j: (0, 0)),
                  pl.BlockSpec((d, tn), lambda j: (0, j))],
        out_specs=(pl.BlockSpec((d, tn), lambda j: (0, j)), pl.BlockSpec((16, d), lambda j: (0, 0))),
        compiler_params=_params(_mb(32)),
    )(dm16, c_all, w)


def _rope(v, cos, sa, sb):
    return v * cos + (pltpu.roll(v, 96, 1) * sa + pltpu.roll(v, 32, 1) * sb)


def _rope_t(dt, cos, sa, sb):
    return dt * cos + (pltpu.roll(dt * sa, 32, 1) + pltpu.roll(dt * sb, 96, 1))


def _qkv_fwd(x, ct, sc, sh, win_g, qg, kg, cos, sa, sb):
    n, d = x.shape
    tm = CTX
    nlat = n // tm
    na = n + CTX
    wcols = NDEV * IN_SHARD

    def body(x_ref, ct_ref, sc_ref, sh_ref, w_ref, qg_ref, kg_ref, cos_ref, sa_ref, sb_ref, u_ref, h_ref, t_ref):
        i = pl.program_id(0)
        xin = jnp.where(i == nlat, ct_ref[...], x_ref[...])
        u = (xin * (1.0 + sc_ref[0]) + sh_ref[0]).astype(BF16)
        u_ref[...] = u
        cos, sa, sb = cos_ref[...], sa_ref[...], sb_ref[...]
        for j in range(NDEV):
            h = jnp.dot(u, w_ref[j], preferred_element_type=F32)
            h_ref[:, j * IN_SHARD:(j + 1) * IN_SHARD] = h
            for hh in range(3):
                hd = 3 * j + hh
                v = h[:, hh * HEAD:(hh + 1) * HEAD]
                kind = _KINDS[hd]
                if kind == "qnorm":
                    v = v * lax.rsqrt(_rowmean(v * v) + EPS) * qg_ref[...]
                elif kind == "knorm":
                    v = v * lax.rsqrt(_rowmean(v * v) + EPS) * kg_ref[...]
                if kind != "none":
                    v = _rope(v, cos, sa, sb)
                t_ref[:, hd * HEAD:(hd + 1) * HEAD] = v.astype(BF16)

    lat = lambda i: (jnp.minimum(i, nlat - 1), 0)
    row = lambda i: (i, 0)
    const2 = lambda i: (0, 0)
    return _pallas(
        body, name="qkv_fwd", grid=(nlat + 1,),
        out_shape=(jax.ShapeDtypeStruct((na, d), BF16), jax.ShapeDtypeStruct((na, wcols), F32),
                   jax.ShapeDtypeStruct((na, wcols), BF16)),
        in_specs=[pl.BlockSpec((tm, d), lat), pl.BlockSpec((tm, d), const2),
                  pl.BlockSpec((1, 1, d), lambda i: (i // nlat, 0, 0)),
                  pl.BlockSpec((1, 1, d), lambda i: (i // nlat, 0, 0)),
                  pl.BlockSpec((NDEV, d, IN_SHARD), lambda i: (0, 0, 0)),
                  pl.BlockSpec((1, HEAD), const2), pl.BlockSpec((1, HEAD), const2),
                  pl.BlockSpec((tm, HEAD), row), pl.BlockSpec((tm, HEAD), row), pl.BlockSpec((tm, HEAD), row)],
        out_specs=(pl.BlockSpec((tm, d), row), pl.BlockSpec((tm, wcols), row), pl.BlockSpec((tm, wcols), row)),
        compiler_params=_params(_mb(56)),
    )(x, ct, sc, sh, win_g, qg, kg, cos, sa, sb)


def _qkv_bwd_prep(dqa, dka, dva, dqb, dkb, dvb, h_all, qg, kg, cos, sa, sb):
    na, wcols = h_all.shape
    n = na - CTX
    tm = CTX
    nlat = n // tm

    def body(dqa_ref, dka_ref, dva_ref, dqb_ref, dkb_ref, dvb_ref, h_ref, qg_ref, kg_ref, cos_ref, sa_ref, sb_ref,
             dh_ref, dg_ref):
        i = pl.program_id(0)

        @pl.when(i == 0)
        def _():
            dg_ref[...] = jnp.zeros_like(dg_ref)

        cos, sa, sb = cos_ref[...], sa_ref[...], sb_ref[...]
        is_lat = i < nlat
        for hd in range(24):
            kind = _KINDS[hd]
            if hd < 8:
                dt = jnp.where(is_lat, dqa_ref[:, hd * HEAD:(hd + 1) * HEAD], 0.0)
            elif hd < 10:
                dt = dka_ref[:, (hd - 8) * HEAD:(hd - 7) * HEAD]
            elif hd < 12:
                dt = dva_ref[:, (hd - 10) * HEAD:(hd - 9) * HEAD]
            elif hd < 20:
                dt = jnp.where(is_lat, dqb_ref[:, (hd - 12) * HEAD:(hd - 11) * HEAD], 0.0)
            elif hd < 22:
                dt = dkb_ref[:, (hd - 20) * HEAD:(hd - 19) * HEAD]
            else:
                dt = dvb_ref[:, (hd - 22) * HEAD:(hd - 21) * HEAD]
            if kind != "none":
                dt = _rope_t(dt, cos, sa, sb)
            if kind in ("qnorm", "knorm"):
                g_ref = qg_ref if kind == "qnorm" else kg_ref
                r0 = 0 if kind == "qnorm" else 1
                xv = h_ref[:, hd * HEAD:(hd + 1) * HEAD]
                xn = xv * lax.rsqrt(_rowmean(xv * xv) + EPS)
                dg_ref[r0:r0 + 1, :] += _colsum(dt * xn)
                dxn = dt * g_ref[...]
                dt = lax.rsqrt(_rowmean(xv * xv) + EPS) * (dxn - xn * _rowmean(dxn * xn))
            dh_ref[:, hd * HEAD:(hd + 1) * HEAD] = dt.astype(BF16)

    lat = lambda i: (jnp.minimum(i, nlat - 1), 0)
    row = lambda i: (i, 0)
    const2 = lambda i: (0, 0)
    return _pallas(
        body, name="qkv_bwd_prep", grid=(nlat + 1,),
        out_shape=(jax.ShapeDtypeStruct((na, wcols), BF16), jax.ShapeDtypeStruct((8, HEAD), F32)),
        in_specs=[pl.BlockSpec((tm, 8 * HEAD), lat), pl.BlockSpec((tm, 2 * HEAD), row), pl.BlockSpec((tm, 2 * HEAD), row),
                  pl.BlockSpec((tm, 8 * HEAD), lat), pl.BlockSpec((tm, 2 * HEAD), row), pl.BlockSpec((tm, 2 * HEAD), row),
                  pl.BlockSpec((tm, wcols), row),
                  pl.BlockSpec((1, HEAD), const2), pl.BlockSpec((1, HEAD), const2),
                  pl.BlockSpec((tm, HEAD), row), pl.BlockSpec((tm, HEAD), row), pl.BlockSpec((tm, HEAD), row)],
        out_specs=(pl.BlockSpec((tm, wcols), row), pl.BlockSpec((8, HEAD), const2)),
        compiler_params=_params(_mb(40)),
    )(dqa, dka, dva, dqb, dkb, dvb, h_all, qg, kg, cos, sa, sb)


def _window_keys(k_ref, v_ref, n, na):
    i = pl.program_id(1)
    tq = WINDOW
    start = pl.multiple_of(jnp.clip((i - 1) * tq, 0, n - 3 * tq), tq)
    kk = jnp.concatenate([k_ref[pl.ds(start, 3 * tq), :], k_ref[n:na, :]], axis=0)
    vv = jnp.concatenate([v_ref[pl.ds(start, 3 * tq), :], v_ref[n:na, :]], axis=0)
    nk = 3 * tq + CTX
    col = lax.broadcasted_iota(jnp.int32, (4 * tq, nk), 1)
    rowi = lax.broadcasted_iota(jnp.int32, (4 * tq, nk), 0)
    qpos = i * tq + (rowi & (tq - 1))
    valid = (jnp.abs(qpos - (start + col)) <= WINDOW) | (col >= 3 * tq)
    return kk, vv, valid, start


def _stack_heads(ref, width=HEAD):
    return jnp.concatenate([ref[:, g * HEAD:g * HEAD + width] for g in range(4)], axis=0)


def _sink_column(sink_ref, kv, tq):
    grp = lax.broadcasted_iota(jnp.int32, (4 * tq, 1), 0) // tq
    col = jnp.zeros((4 * tq, 1), F32)
    for g in range(4):
        col = jnp.where(grp == g, sink_ref[0, 4 * kv + g] * LOG2E, col)
    return col


def _attn_window_fwd(t_all, sink, after):
    na = t_all.shape[0]
    n = na - CTX
    tq = WINDOW

    def body(sink_ref, q_ref, k_ref, v_ref, after_ref, o_ref, lse_ref):
        kv = pl.program_id(0)
        kk, vv, valid, _ = _window_keys(k_ref, v_ref, n, na)
        t = lax.dot_general(_stack_heads(q_ref), kk, _NT, preferred_element_type=F32) * QK_LOG2
        t = jnp.where(valid, t, NEG)
        sk = _sink_column(sink_ref, kv, tq)
        m = jnp.maximum(jnp.max(t, axis=-1, keepdims=True), sk)
        p = jnp.exp2(t - m)
        l = jnp.sum(p, axis=-1, keepdims=True) + jnp.exp2(sk - m)
        o = jnp.dot(p.astype(BF16), vv, preferred_element_type=F32) * (1.0 / l)
        lse = m + jnp.log2(l)
        for g in range(4):
            o_ref[:, g * HEAD:(g + 1) * HEAD] = o[g * tq:(g + 1) * tq]
            lse_ref[:, g * HEAD:(g + 1) * HEAD] = jnp.broadcast_to(lse[g * tq:(g + 1) * tq], (tq, HEAD))

    blk = pl.BlockSpec((tq, 4 * HEAD), lambda kv, i: (i, kv))
    return _pallas(
        body, name="attn_window_fwd", grid=(2, n // tq),
        out_shape=(jax.ShapeDtypeStruct((n, 16 * HEAD), F32), jax.ShapeDtypeStruct((n, 8 * HEAD), F32)),
        in_specs=[pl.BlockSpec(memory_space=pltpu.SMEM), blk,
                  pl.BlockSpec((na, HEAD), lambda kv, i: (0, 8 + kv)),
                  pl.BlockSpec((na, HEAD), lambda kv, i: (0, 10 + kv)), _ANY],
        out_specs=(blk, blk),
        compiler_params=_params(_mb(32)),
    )(sink, t_all, t_all, t_all, after)


def _attn_global_fwd(t_all, o_part):
    na = t_all.shape[0]
    n = na - CTX
    tq = 256

    def body(q_ref, k_ref, v_ref, o_in_ref, o_ref, lse_ref):
        kk, vv = k_ref[...], v_ref[...]
        for g in range(4):
            q = q_ref[:, g * HEAD:(g + 1) * HEAD]
            t = lax.dot_general(q, kk, _NT, preferred_element_type=F32) * QK_LOG2
            m = jnp.max(t, axis=-1, keepdims=True)
            p = jnp.exp2(t - m)
            l = jnp.sum(p, axis=-1, keepdims=True)
            o_ref[:, g * HEAD:(g + 1) * HEAD] = jnp.dot(p.astype(BF16), vv, preferred_element_type=F32) * (1.0 / l)
            lse_ref[:, g * HEAD:(g + 1) * HEAD] = jnp.broadcast_to(m + jnp.log2(l), (tq, HEAD))

    return _pallas(
        body, name="attn_global_fwd", grid=(2, n // tq),
        out_shape=(jax.ShapeDtypeStruct((n, 16 * HEAD), F32), jax.ShapeDtypeStruct((n, 8 * HEAD), F32)),
        in_specs=[pl.BlockSpec((tq, 4 * HEAD), lambda kv, i: (i, 3 + kv)),
                  pl.BlockSpec((na, HEAD), lambda kv, i: (0, 20 + kv)),
                  pl.BlockSpec((na, HEAD), lambda kv, i: (0, 22 + kv)), _ANY],
        out_specs=(pl.BlockSpec((tq, 4 * HEAD), lambda kv, i: (i, 2 + kv)),
                   pl.BlockSpec((tq, 4 * HEAD), lambda kv, i: (i, kv))),
        input_output_aliases={3: 0},
        compiler_params=_params(_mb(48)),
    )(t_all, t_all, t_all, o_part)


def _attn_window_bwd(t_all, o, do, lse, sink):
    na = t_all.shape[0]
    n = na - CTX
    tq = WINDOW

    def body(sink_ref, q_ref, k_ref, v_ref, o_ref, do_ref, lse_ref, dq_ref, dk_ref, dv_ref, dsink_ref):
        kv = pl.program_id(0)

        @pl.when(pl.program_id(1) == 0)
        def _():
            dk_ref[...] = jnp.zeros_like(dk_ref)
            dv_ref[...] = jnp.zeros_like(dv_ref)
            dsink_ref[...] = jnp.zeros_like(dsink_ref)

        kk, vv, valid, start = _window_keys(k_ref, v_ref, n, na)
        q = _stack_heads(q_ref)
        t = lax.dot_general(q, kk, _NT, preferred_element_type=F32) * QK_LOG2
        t = jnp.where(valid, t, NEG)
        lse = _stack_heads(lse_ref, 1)
        p = jnp.exp2(t - lse)
        dof = _stack_heads(do_ref)
        delta = jnp.sum(dof * _stack_heads(o_ref), axis=-1, keepdims=True)
        dob = dof.astype(BF16)
        dv_acc = lax.dot_general(p.astype(BF16), dob, _TN, preferred_element_type=F32)
        dp = lax.dot_general(dob, vv, _NT, preferred_element_type=F32)
        ds = (p * (dp - delta) * SCALE).astype(BF16)
        dq = jnp.dot(ds, kk, preferred_element_type=F32)
        dk_acc = lax.dot_general(ds, q, _TN, preferred_element_type=F32)
        dsk = -(jnp.exp2(_sink_column(sink_ref, kv, tq) - lse) * delta)
        for g in range(4):
            dq_ref[:, g * HEAD:(g + 1) * HEAD] = dq[g * tq:(g + 1) * tq]
            dsink_ref[0, g:g + 1, :] += jnp.broadcast_to(_colsum(dsk[g * tq:(g + 1) * tq]), (1, HEAD))
        dk_ref[pl.ds(start, 3 * tq), :] += dk_acc[:3 * tq]
        dv_ref[pl.ds(start, 3 * tq), :] += dv_acc[:3 * tq]
        dk_ref[n:na, :] += dk_acc[3 * tq:]
        dv_ref[n:na, :] += dv_acc[3 * tq:]

    blk = pl.BlockSpec((tq, 4 * HEAD), lambda kv, i: (i, kv))
    kvout = pl.BlockSpec((na, HEAD), lambda kv, i: (0, kv))
    return _pallas(
        body, name="attn_window_bwd", grid=(2, n // tq),
        out_shape=(jax.ShapeDtypeStruct((n, 8 * HEAD), F32), jax.ShapeDtypeStruct((na, 2 * HEAD), F32),
                   jax.ShapeDtypeStruct((na, 2 * HEAD), F32), jax.ShapeDtypeStruct((2, 8, HEAD), F32)),
        in_specs=[pl.BlockSpec(memory_space=pltpu.SMEM), blk,
                  pl.BlockSpec((na, HEAD), lambda kv, i: (0, 8 + kv)),
                  pl.BlockSpec((na, HEAD), lambda kv, i: (0, 10 + kv)),
                  blk, blk, blk],
        out_specs=(blk, kvout, kvout, pl.BlockSpec((1, 8, HEAD), lambda kv, i: (kv, 0, 0))),
        compiler_params=_params(_mb(40)),
    )(sink, t_all, t_all, t_all, o, do, lse)


def _attn_global_bwd(t_all, o, do, lse):
    na = t_all.shape[0]
    n = na - CTX
    tq = 256

    def body(q_ref, k_ref, v_ref, o_ref, do_ref, lse_ref, dq_ref, dk_ref, dv_ref):
        @pl.when(pl.program_id(1) == 0)
        def _():
            dk_ref[...] = jnp.zeros_like(dk_ref)
            dv_ref[...] = jnp.zeros_like(dv_ref)

        kk, vv = k_ref[...], v_ref[...]
        dk_acc = jnp.zeros((na, HEAD), F32)
        dv_acc = jnp.zeros((na, HEAD), F32)
        for g in range(4):
            q = q_ref[:, g * HEAD:(g + 1) * HEAD]
            t = lax.dot_general(q, kk, _NT, preferred_element_type=F32) * QK_LOG2
            p = jnp.exp2(t - lse_ref[:, g * HEAD:g * HEAD + 1])
            dof = do_ref[:, g * HEAD:(g + 1) * HEAD]
            delta = jnp.sum(dof * o_ref[:, g * HEAD:(g + 1) * HEAD], axis=-1, keepdims=True)
            dob = dof.astype(BF16)
            dv_acc = dv_acc + lax.dot_general(p.astype(BF16), dob, _TN, preferred_element_type=F32)
            dp = lax.dot_general(dob, vv, _NT, preferred_element_type=F32)
            ds = (p * (dp - delta) * SCALE).astype(BF16)
            dq_ref[:, g * HEAD:(g + 1) * HEAD] = jnp.dot(ds, kk, preferred_element_type=F32)
            dk_acc = dk_acc + lax.dot_general(ds, q, _TN, preferred_element_type=F32)
        dk_ref[...] += dk_acc
        dv_ref[...] += dv_acc

    ospec = pl.BlockSpec((tq, 4 * HEAD), lambda kv, i: (i, 2 + kv))
    lspec = pl.BlockSpec((tq, 4 * HEAD), lambda kv, i: (i, kv))
    kvout = pl.BlockSpec((na, HEAD), lambda kv, i: (0, kv))
    return _pallas(
        body, name="attn_global_bwd", grid=(2, n // tq),
        out_shape=(jax.ShapeDtypeStruct((n, 8 * HEAD), F32), jax.ShapeDtypeStruct((na, 2 * HEAD), F32),
                   jax.ShapeDtypeStruct((na, 2 * HEAD), F32)),
        in_specs=[pl.BlockSpec((tq, 4 * HEAD), lambda kv, i: (i, 3 + kv)),
                  pl.BlockSpec((na, HEAD), lambda kv, i: (0, 20 + kv)),
                  pl.BlockSpec((na, HEAD), lambda kv, i: (0, 22 + kv)),
                  ospec, ospec, lspec],
        out_specs=(lspec, kvout, kvout),
        compiler_params=_params(_mb(56)),
    )(t_all, t_all, t_all, o, do, lse)


def _outproj_ln1(o, wout, x, g1, lg, lb, sc2, sh2, after):
    n, d = x.shape
    tm = 256

    def body(o_ref, w_ref, x_ref, g1_ref, lg_ref, lb_ref, sc_ref, sh_ref, after_ref, a_ref, xh_ref, rs_ref, u_ref):
        a1 = jnp.dot(o_ref[...].astype(BF16), w_ref[...], preferred_element_type=F32)
        a_ref[...] = a1
        r = ALPHA * x_ref[...] + g1_ref[...] * a1
        dlt = r - _rowmean(r)
        rstd = lax.rsqrt(_rowmean(dlt * dlt) + EPS)
        xh = dlt * rstd
        xh_ref[...] = xh
        rs_ref[...] = rstd
        x1 = xh * lg_ref[...] + lb_ref[...]
        u_ref[...] = (x1 * (1.0 + sc_ref[...]) + sh_ref[...]).astype(BF16)

    row = lambda i: (i, 0)
    const2 = lambda i: (0, 0)
    vec = pl.BlockSpec((1, d), const2)
    big = pl.BlockSpec((tm, d), row)
    return _pallas(
        body, name="outproj_ln1", grid=(n // tm,),
        out_shape=(jax.ShapeDtypeStruct((n, d), F32), jax.ShapeDtypeStruct((n, d), F32),
                   jax.ShapeDtypeStruct((n, 1), F32), jax.ShapeDtypeStruct((n, d), BF16)),
        in_specs=[big, pl.BlockSpec((d, d), const2), big, vec, vec, vec, vec, vec, _ANY],
        out_specs=(big, big, pl.BlockSpec((tm, 1), row), big),
        compiler_params=_params(_mb(56)),
    )(o, wout, x, g1, lg, lb, sc2, sh2, after)


def _ffn_up(u2, wgt, wut, after):
    n, d = u2.shape
    f = wgt.shape[0]
    tm = min(1024, n)

    def body(u_ref, wg_ref, wu_ref, after_ref, g_ref, p_ref, hf_ref):
        u = u_ref[...]
        gv = lax.dot_general(u, wg_ref[...], _NT, preferred_element_type=F32)
        pv = lax.dot_general(u, wu_ref[...], _NT, preferred_element_type=F32)
        g_ref[...] = gv.astype(BF16)
        p_ref[...] = pv.astype(BF16)
        hf_ref[...] = (gv * _sigmoid(gv) * pv).astype(BF16)

    tile = pl.BlockSpec((tm, FFN_TILE), lambda i, j: (i, j))
    wspec = pl.BlockSpec((FFN_TILE, d), lambda i, j: (j, 0))
    sds = jax.ShapeDtypeStruct((n, f), BF16)
    return _pallas(
        body, name="ffn_up", grid=(n // tm, f // FFN_TILE),
        out_shape=(sds, sds, sds),
        in_specs=[pl.BlockSpec((tm, d), lambda i, j: (i, 0)), wspec, wspec, _ANY],
        out_specs=(tile, tile, tile),
        compiler_params=_params(_mb(48)),
    )(u2, wgt, wut, after)


def _ffn_down(hf, wd):
    n, f = hf.shape
    d = wd.shape[1]
    tm, tn = min(1024, n), 512

    def body(h_ref, w_ref, o_ref):
        o_ref[...] = jnp.dot(h_ref[...], w_ref[...], preferred_element_type=F32)

    return _pallas(
        body, name="ffn_down", grid=(n // tm, d // tn),
        out_shape=jax.ShapeDtypeStruct((n, d), F32),
        in_specs=[pl.BlockSpec((tm, f), lambda i, j: (i, 0)), pl.BlockSpec((f, tn), lambda i, j: (0, j))],
        out_specs=pl.BlockSpec((tm, tn), lambda i, j: (i, j)),
        compiler_params=_params(_mb(56)),
    )(hf, wd)


def _ln2_loss(xh1, ffn, tgt, lg1, lb1, g2, lg2, lb2):
    n, d = xh1.shape
    tm = 256

    def body(xh_ref, f_ref, t_ref, lg1_ref, lb1_ref, g2_ref, lg2_ref, lb2_ref, dr_ref, df_ref, loss_ref, acc_ref):
        @pl.when(pl.program_id(0) == 0)
        def _():
            loss_ref[...] = jnp.zeros_like(loss_ref)
            acc_ref[...] = jnp.zeros_like(acc_ref)

        x1 = xh_ref[...] * lg1_ref[...] + lb1_ref[...]
        fv = f_ref[...]
        r = ALPHA * x1 + g2_ref[...] * fv
        dlt = r - _rowmean(r)
        rstd = lax.rsqrt(_rowmean(dlt * dlt) + EPS)
        xh2 = dlt * rstd
        err = xh2 * lg2_ref[...] + lb2_ref[...] - t_ref[...]
        loss_ref[...] += 0.5 * jnp.sum(_rowmean(err * err))
        dy = err * (1.0 / d)
        dyg = dy * lg2_ref[...]
        dr = rstd * (dyg - _rowmean(dyg) - xh2 * _rowmean(dyg * xh2))
        dr_ref[...] = dr
        df_ref[...] = (g2_ref[...] * dr).astype(BF16)
        acc_ref[0:1, :] += _colsum(dy * xh2)
        acc_ref[1:2, :] += _colsum(dy)
        acc_ref[2:3, :] += _colsum(dr * fv)

    row = lambda i: (i, 0)
    const2 = lambda i: (0, 0)
    vec = pl.BlockSpec((1, d), const2)
    big = pl.BlockSpec((tm, d), row)
    return _pallas(
        body, name="ln2_loss", grid=(n // tm,),
        out_shape=(jax.ShapeDtypeStruct((n, d), F32), jax.ShapeDtypeStruct((n, d), BF16),
                   jax.ShapeDtypeStruct((8, HEAD), F32), jax.ShapeDtypeStruct((8, d), F32)),
        in_specs=[big, big, big, vec, vec, vec, vec, vec],
        out_specs=(big, big, pl.BlockSpec((8, HEAD), const2), pl.BlockSpec((8, d), const2)),
        compiler_params=_params(_mb(48)),
    )(xh1, ffn, tgt, lg1, lb1, g2, lg2, lb2)


def _ffn_dhf(df, wd, gmat, pmat):
    n, d = df.shape
    f = gmat.shape[1]
    tm = min(1024, n)

    def body(df_ref, w_ref, g_ref, p_ref, dg_ref, dp_ref):
        dhf = lax.dot_general(df_ref[...], w_ref[...], _NT, preferred_element_type=F32)
        gv = g_ref[...].astype(F32)
        sg = _sigmoid(gv)
        dp_ref[...] = (dhf * (gv * sg)).astype(BF16)
        dg_ref[...] = (dhf * p_ref[...].astype(F32) * (sg * (1.0 + gv * (1.0 - sg)))).astype(BF16)

    tile = pl.BlockSpec((tm, FFN_TILE), lambda i, j: (i, j))
    sds = jax.ShapeDtypeStruct((n, f), BF16)
    return _pallas(
        body, name="ffn_dhf", grid=(n // tm, f // FFN_TILE),
        out_shape=(sds, sds),
        in_specs=[pl.BlockSpec((tm, d), lambda i, j: (i, 0)), pl.BlockSpec((FFN_TILE, d), lambda i, j: (j, 0)),
                  tile, tile],
        out_specs=(tile, tile),
        compiler_params=_params(_mb(48)),
    )(df, wd, gmat, pmat)


def _ffn_du2(dg, dp, wgt, wut, after):
    n, f = dg.shape
    d = wgt.shape[1]
    tm = min(1024, n)

    def body(dg_ref, dp_ref, wg_ref, wu_ref, after_ref, o_ref):
        part = (jnp.dot(dg_ref[...], wg_ref[...], preferred_element_type=F32)
                + jnp.dot(dp_ref[...], wu_ref[...], preferred_element_type=F32))

        @pl.when(pl.program_id(1) == 0)
        def _():
            o_ref[...] = part

        @pl.when(pl.program_id(1) > 0)
        def _():
            o_ref[...] += part

    tile = pl.BlockSpec((tm, FFN_TILE), lambda i, j: (i, j))
    wspec = pl.BlockSpec((FFN_TILE, d), lambda i, j: (j, 0))
    return _pallas(
        body, name="ffn_du2", grid=(n // tm, f // FFN_TILE),
        out_shape=jax.ShapeDtypeStruct((n, d), F32),
        in_specs=[tile, tile, wspec, wspec, _ANY],
        out_specs=pl.BlockSpec((tm, d), lambda i, j: (i, 0)),
        compiler_params=_params(_mb(48)),
    )(dg, dp, wgt, wut, after)


def _ln1_bwd(du2, dr2, xh1, rs1, a1, lg1, lb1, sc2, g1):
    n, d = du2.shape
    tm = 256

    def body(du_ref, dr2_ref, xh_ref, rs_ref, a_ref, lg_ref, lb_ref, sc_ref, g1_ref, dr1_ref, da_ref, acc_ref):
        @pl.when(pl.program_id(0) == 0)
        def _():
            acc_ref[...] = jnp.zeros_like(acc_ref)

        du = du_ref[...]
        xh = xh_ref[...]
        x1 = xh * lg_ref[...] + lb_ref[...]
        dx1 = ALPHA * dr2_ref[...] + du * (1.0 + sc_ref[...])
        dxg = dx1 * lg_ref[...]
        dr1 = rs_ref[...] * (dxg - _rowmean(dxg) - xh * _rowmean(dxg * xh))
        dr1_ref[...] = dr1
        da_ref[...] = (g1_ref[...] * dr1).astype(BF16)
        acc_ref[0:1, :] += _colsum(du * x1)
        acc_ref[1:2, :] += _colsum(du)
        acc_ref[2:3, :] += _colsum(dx1 * xh)
        acc_ref[3:4, :] += _colsum(dx1)
        acc_ref[4:5, :] += _colsum(dr1 * a_ref[...])

    row = lambda i: (i, 0)
    const2 = lambda i: (0, 0)
    vec = pl.BlockSpec((1, d), const2)
    big = pl.BlockSpec((tm, d), row)
    return _pallas(
        body, name="ln1_bwd", grid=(n // tm,),
        out_shape=(jax.ShapeDtypeStruct((n, d), F32), jax.ShapeDtypeStruct((n, d), BF16),
                   jax.ShapeDtypeStruct((8, d), F32)),
        in_specs=[big, big, big, pl.BlockSpec((tm, 1), row), big, vec, vec, vec, vec],
        out_specs=(big, big, pl.BlockSpec((8, d), const2)),
        compiler_params=_params(_mb(48)),
    )(du2, dr2, xh1, rs1, a1, lg1, lb1, sc2, g1)


def _dw_in(u, dh, after):
    m, k = u.shape
    tm = CTX
    half = 4 * IN_SHARD

    def body(u_ref, dh_ref, after_ref, o_ref, acc_ref):
        part = lax.dot_general(u_ref[...], dh_ref[...], _TN, preferred_element_type=F32)
        i = pl.program_id(1)

        @pl.when(i == 0)
        def _():
            acc_ref[...] = part

        @pl.when(i > 0)
        def _():
            acc_ref[...] += part

        @pl.when(i == pl.num_programs(1) - 1)
        def _():
            for jj in range(4):
                o_ref[jj] = acc_ref[:, jj * IN_SHARD:(jj + 1) * IN_SHARD].astype(BF16)

    return _pallas(
        body, name="dw_in", grid=(2, m // tm),
        out_shape=jax.ShapeDtypeStruct((NDEV, k, IN_SHARD), BF16),
        in_specs=[pl.BlockSpec((tm, k), lambda jh, i: (i, 0)), pl.BlockSpec((tm, half), lambda jh, i: (i, jh)), _ANY],
        out_specs=pl.BlockSpec((4, k, IN_SHARD), lambda jh, i: (jh, 0, 0)),
        scratch_shapes=[pltpu.VMEM((k, half), F32)],
        compiler_params=_params(_mb(48)),
    )(u, dh, after)


def _dw_rows(a, b, nblk, bw, tm, after, name):
    m = a.shape[0]
    nn = b.shape[1]

    def body(a_ref, b_ref, after_ref, o_ref, acc_ref):
        part = lax.dot_general(a_ref[...].astype(BF16), b_ref[...], _TN, preferred_element_type=F32)
        i = pl.program_id(1)

        @pl.when(i == 0)
        def _():
            acc_ref[...] = part

        @pl.when(i > 0)
        def _():
            acc_ref[...] += part

        @pl.when(i == pl.num_programs(1) - 1)
        def _():
            o_ref[0] = acc_ref[...].astype(BF16)

    return _pallas(
        body, name=name, grid=(nblk, m // tm),
        out_shape=jax.ShapeDtypeStruct((nblk, bw, nn), BF16),
        in_specs=[pl.BlockSpec((tm, bw), lambda j, i: (i, j)), pl.BlockSpec((tm, nn), lambda j, i: (i, 0)), _ANY],
        out_specs=pl.BlockSpec((1, bw, nn), lambda j, i: (j, 0, 0)),
        scratch_shapes=[pltpu.VMEM((bw, nn), F32)],
        compiler_params=_params(_mb(56)),
    )(a, b, after)


def _outproj_bwd(da1, wout, after):
    n, d = da1.shape
    tm = 512

    def body(a_ref, w_ref, after_ref, o_ref):
        o_ref[...] = lax.dot_general(a_ref[...], w_ref[...], _NT, preferred_element_type=F32)

    return _pallas(
        body, name="outproj_bwd", grid=(n // tm,),
        out_shape=jax.ShapeDtypeStruct((n, d), F32),
        in_specs=[pl.BlockSpec((tm, d), lambda i: (i, 0)), pl.BlockSpec((d, d), lambda i: (0, 0)), _ANY],
        out_specs=pl.BlockSpec((tm, d), lambda i: (i, 0)),
        compiler_params=_params(_mb(48)),
    )(da1, wout, after)


def _qkv_bwd(dh, win_g, x, ct, dr1, sc):
    na, wcols = dh.shape
    n, d = x.shape
    tm = CTX
    nlat = n // tm

    def body(dh_ref, w_ref, x_ref, ct_ref, dr_ref, sc_ref, gx_ref, acc_ref):
        i = pl.program_id(0)

        @pl.when(i == 0)
        def _():
            acc_ref[...] = jnp.zeros_like(acc_ref)

        du = jnp.zeros((tm, d), F32)
        for j in range(NDEV):
            du = du + lax.dot_general(dh_ref[:, j * IN_SHARD:(j + 1) * IN_SHARD], w_ref[j], _NT,
                                      preferred_element_type=F32)

        @pl.when(i < nlat)
        def _():
            gx_ref[...] = ALPHA * dr_ref[...] + du * (1.0 + sc_ref[0])
            acc_ref[0:1, :] += _colsum(du)
            acc_ref[1:2, :] += _colsum(du * x_ref[...])

        @pl.when(i == nlat)
        def _():
            acc_ref[2:3, :] += _colsum(du)
            acc_ref[3:4, :] += _colsum(du * ct_ref[...])

    lat = lambda i: (jnp.minimum(i, nlat - 1), 0)
    const2 = lambda i: (0, 0)
    return _pallas(
        body, name="qkv_bwd", grid=(nlat + 1,),
        out_shape=(jax.ShapeDtypeStruct((n, d), F32), jax.ShapeDtypeStruct((8, d), F32)),
        in_specs=[pl.BlockSpec((tm, wcols), lambda i: (i, 0)), pl.BlockSpec((NDEV, d, IN_SHARD), lambda i: (0, 0, 0)),
                  pl.BlockSpec((tm, d), lat), pl.BlockSpec((tm, d), const2), pl.BlockSpec((tm, d), lat),
                  pl.BlockSpec((1, 1, d), lambda i: (0, 0, 0))],
        out_specs=(pl.BlockSpec((tm, d), lat), pl.BlockSpec((8, d), const2)),
        compiler_params=_params(_mb(56)),
    )(dh, win_g, x, ct, dr1, sc)


def _adam_math(w, g, m, v):
    m2 = ADAM_B1 * m + (1.0 - ADAM_B1) * g
    v2 = ADAM_B2 * v + (1.0 - ADAM_B2) * (g * g)
    m_hat = m2 / (1.0 - ADAM_B1 ** ADAM_STEP)
    v_hat = v2 / (1.0 - ADAM_B2 ** ADAM_STEP)
    delta = -ADAM_LR * (m_hat / (jnp.sqrt(v_hat) + ADAM_EPS) + ADAM_WD * w)
    return delta, m2, v2


def _adamw(w, gsrc, m, v, name):
    r, c = w.shape
    parts = gsrc.ndim == 3
    cg = gsrc.shape[-1]
    tr = r
    while tr * c * 4 > _mb(1) and tr % 32 == 0:
        tr //= 2

    def body(w_ref, g_ref, m_ref, v_ref, go_ref, d_ref, mo_ref, vo_ref):
        if parts:
            g = g_ref[0].astype(F32)
            for s in range(1, NDEV):
                g = g + g_ref[s].astype(F32)
            g = g[:, :c]
        else:
            g = g_ref[...]
        delta, m2, v2 = _adam_math(w_ref[...], g, m_ref[...], v_ref[...])
        go_ref[...] = g
        d_ref[...] = delta
        mo_ref[...] = m2
        vo_ref[...] = v2

    tile = pl.BlockSpec((tr, c), lambda i: (i, 0))
    gspec = pl.BlockSpec((NDEV, tr, cg), lambda i: (0, i, 0)) if parts else tile
    sds = jax.ShapeDtypeStruct((r, c), F32)
    return _pallas(
        body, name=name, grid=(r // tr,),
        out_shape=(sds, sds, sds, sds),
        in_specs=[tile, gspec, tile, tile],
        out_specs=(tile, tile, tile, tile),
        compiler_params=_params(_mb(48)),
    )(w, gsrc, m, v)


def _adamw_t(w, gsrc_t, m, v, name):
    r, c = w.shape
    tr = 256

    def body(w_ref, g_ref, m_ref, v_ref, go_ref, d_ref, mo_ref, vo_ref):
        gt = g_ref[0].astype(F32)
        for s in range(1, NDEV):
            gt = gt + g_ref[s].astype(F32)
        g = gt.T
        delta, m2, v2 = _adam_math(w_ref[...], g, m_ref[...], v_ref[...])
        go_ref[...] = g
        d_ref[...] = delta
        mo_ref[...] = m2
        vo_ref[...] = v2

    tile = pl.BlockSpec((tr, c), lambda i: (i, 0))
    sds = jax.ShapeDtypeStruct((r, c), F32)
    return _pallas(
        body, name=name, grid=(r // tr,),
        out_shape=(sds, sds, sds, sds),
        in_specs=[tile, pl.BlockSpec((NDEV, c, tr), lambda i: (0, 0, i)), tile, tile],
        out_specs=(tile, tile, tile, tile),
        compiler_params=_params(_mb(48)),
    )(w, gsrc_t, m, v)


def _small_update(gath, dcc, cc, w_s, m_s, v_s):
    d = w_s.shape[1]

    def body(g_ref, dcc_ref, cc_ref, w_ref, m_ref, v_ref, go_ref, d_ref, mo_ref, vo_ref):
        s = g_ref[0]
        for b in range(1, NDEV):
            s = s + g_ref[b]
        dsl = dcc_ref[0, 8:9, :]
        for b in range(1, NDEV):
            dsl = dsl + dcc_ref[b, 8:9, :]
        cv = cc_ref[...]
        sg = _sigmoid(cv)
        go_ref[...] = jnp.zeros_like(go_ref)
        go_ref[0:1, :] = dsl * (sg * (1.0 + cv * (1.0 - sg)))
        go_ref[1:3, :] = s[0:2] + s[6:8]
        go_ref[3:7, :] = s[2:6]
        go_ref[7:12, :] = s[8:13]
        delta, m2, v2 = _adam_math(w_ref[...], go_ref[...], m_ref[...], v_ref[...])
        d_ref[...] = delta
        mo_ref[...] = m2
        vo_ref[...] = v2

    full = pl.BlockSpec((16, d), lambda: (0, 0))
    g3 = pl.BlockSpec((NDEV, 16, d), lambda: (0, 0, 0))
    sds = jax.ShapeDtypeStruct((16, d), F32)
    return _pallas(
        body, name="small_update",
        out_shape=(sds, sds, sds, sds),
        in_specs=[g3, g3, pl.BlockSpec((1, d), lambda: (0, 0)), full, full, full],
        out_specs=(full, full, full, full),
        compiler_params=_params(_mb(24)),
    )(gath, dcc, cc, w_s, m_s, v_s)


def _rope_tables(n):
    rows = n // GRID_W
    row_ids = jnp.repeat(jnp.arange(rows, dtype=F32), GRID_W)
    col_ids = jnp.tile(jnp.arange(GRID_W, dtype=F32), rows)
    axis_dim = HEAD // 2
    inv_freq = jnp.power(ROPE_THETA, -jnp.arange(0, axis_dim, 2, dtype=F32) / axis_dim)
    ang_r = row_ids[:, None] * inv_freq
    ang_c = col_ids[:, None] * inv_freq
    ang = jnp.concatenate([ang_r, ang_r, ang_c, ang_c], axis=-1)
    cos, sin = jnp.cos(ang), jnp.sin(ang)
    first = (jnp.arange(HEAD) % (HEAD // 2)) < HEAD // 4
    sa = jnp.where(first, -sin, 0.0)
    sb = jnp.where(first, 0.0, sin)
    ones = jnp.ones((CTX, HEAD), F32)
    zeros = jnp.zeros((CTX, HEAD), F32)
    return (jnp.concatenate([cos, ones], 0), jnp.concatenate([sa, zeros], 0), jnp.concatenate([sb, zeros], 0))


def _pad_cols(a, width):
    return jnp.pad(a, ((0, 0), (0, width - a.shape[1])))


def _pad_rows(a, rows):
    return jnp.pad(a, ((0, rows - a.shape[0]), (0, 0)))


def _pack_small(c_ctx, b_ada, ln1_g, ln1_b, ln2_g, ln2_b, qg, kg, sink, d):
    misc = _pad_cols(jnp.concatenate([qg, kg, sink], axis=1), d)
    rows = jnp.concatenate([c_ctx.reshape(1, d), b_ada.reshape(6, d), ln1_g, ln1_b, ln2_g, ln2_b, misc], axis=0)
    return _pad_rows(rows, 16)


def _unpack_small(p, d):
    return dict(c_ctx=p[0], b_ada=p[1:7].reshape(1, 6 * d), ln1_g=p[7:8], ln1_b=p[8:9], ln2_g=p[9:10], ln2_b=p[10:11],
                q_norm_g=p[11:12, 0:HEAD], k_norm_g=p[11:12, HEAD:2 * HEAD], sink_logit=p[11:12, 2 * HEAD:2 * HEAD + 8])


def kernel(x, c, ctx, c_ctx, w_ada, b_ada, w_in, q_norm_g, k_norm_g, sink_logit, w_out, ln1_g, ln1_b, w_gate, w_up, w_down, ln2_g, ln2_b, loss_target, m_c_ctx, m_w_ada, m_b_ada, m_w_in, m_q_norm_g, m_k_norm_g, m_sink_logit, m_w_out, m_ln1_g, m_ln1_b, m_w_gate, m_w_up, m_w_down, m_ln2_g, m_ln2_b, v_c_ctx, v_w_ada, v_b_ada, v_w_in, v_q_norm_g, v_k_norm_g, v_sink_logit, v_w_out, v_ln1_g, v_ln1_b, v_w_gate, v_w_up, v_w_down, v_ln2_g, v_ln2_b):
    xs, cts, tgt = x[0], ctx[0], loss_target[0]
    n, d = xs.shape
    assert cts.shape == (CTX, d) and w_in.shape[2] == IN_SHARD and w_gate.shape[2] == FFN_SHARD
    me = 4 * lax.axis_index("x") + 2 * lax.axis_index("y") + lax.axis_index("c")
    e_sh = w_ada.shape[2]

    c_g = _exchange(_pad_rows(c, 8), False, "gather_c")
    c_all = jnp.concatenate([c_g[:, 0, :], _pad_rows(c_ctx.reshape(1, d), 8)], axis=0)
    bias_sh = lax.dynamic_slice(b_ada, (0, me * e_sh), (1, e_sh))
    mods_g = _exchange(_ada_fwd(c_all, w_ada[0], bias_sh), False, "gather_mods")
    mods = jnp.transpose(mods_g, (1, 0, 2)).reshape(16, NDEV * e_sh)
    mine = lax.dynamic_slice(mods, (me, 0), (1, 6 * d))
    sh1, sc1, g1, sh2, sc2, g2 = [mine[:, k * d:(k + 1) * d] for k in range(6)]
    csh1, csc1 = mods[8:9, 0:d], mods[8:9, d:2 * d]
    sc_pair = jnp.stack([sc1, csc1])
    sh_pair = jnp.stack([sh1, csh1])

    h_win, tok = _exchange_start(w_in[0].astype(BF16), "gather", mods, "gather_w_in_start")
    h_wout, tok = _exchange_start(w_out[0].astype(BF16), "chip", tok, "gather_w_out_start")
    h_wg, tok = _exchange_start(w_gate[0].T.astype(BF16), "chip", tok, "gather_w_gate_start")
    h_wu, tok = _exchange_start(w_up[0].T.astype(BF16), "chip", tok, "gather_w_up_start")
    h_wd, tok = _exchange_start(w_down[0].astype(BF16), "chip", tok, "gather_w_down_start")

    cos, sa, sb = _rope_tables(n)
    win_g = _exchange_wait(h_win, "gather", tok, "gather_w_in_wait")
    u_all, h_all, t_all = _qkv_fwd(xs, cts, sc_pair, sh_pair, win_g, q_norm_g, k_norm_g, cos, sa, sb)
    f_wout, tok = _forward_start(_exchange_wait(h_wout, "chip", t_all, "gather_w_out_wait"), t_all, "forward_w_out_start")
    o_a, lse_a = _attn_window_fwd(t_all, sink_logit, tok)
    o, lse_b = _attn_global_fwd(t_all, o_a)
    f_wg, tok = _forward_start(_exchange_wait(h_wg, "chip", o, "gather_w_gate_wait"), o, "forward_w_gate_start")
    f_wu, tok = _forward_start(_exchange_wait(h_wu, "chip", tok, "gather_w_up_wait"), tok, "forward_w_up_start")
    wout_g = _forward_wait(f_wout, tok, "forward_w_out_wait").reshape(d, d)
    a1, xh1, rs1, u2 = _outproj_ln1(o, wout_g, xs, g1, ln1_g, ln1_b, sc2, sh2, tok)
    f_wd, tok = _forward_start(_exchange_wait(h_wd, "chip", rs1, "gather_w_down_wait"), rs1, "forward_w_down_start")
    ffn_w = (NDEV * FFN_SHARD, d)
    wg_g = _forward_wait(f_wg, tok, "forward_w_gate_wait").reshape(ffn_w)
    wu_g = _forward_wait(f_wu, tok, "forward_w_up_wait").reshape(ffn_w)
    gmat, pmat, hf = _ffn_up(u2, wg_g, wu_g, tok)
    wd_g = _forward_wait(f_wd, hf, "forward_w_down_wait").reshape(ffn_w)
    ffn = _ffn_down(hf, wd_g)
    dr2, df, loss_p, acc2 = _ln2_loss(xh1, ffn, tgt, ln1_g, ln1_b, g2, ln2_g, ln2_b)
    loss = lax.psum(loss_p[0, 0], ("x", "y", "c"))

    tk = min(n, 2048)
    parts = (NDEV, FFN_SHARD, d)
    dgm, dpm = _ffn_dhf(df, wd_g, gmat, pmat)
    dwd_p = _dw_rows(hf, df, NDEV // 2, FFN_PAIR, min(n, 1024), loss_p, "dw_down").reshape(parts)
    h_dwd, tok = _exchange_start(dwd_p, "scatter", loss.reshape(1, 1), "scatter_dw_down_start")
    dwg_p = _dw_rows(dgm, u2, NDEV // 2, FFN_PAIR, min(n, 1024), tok, "dw_gate").reshape(parts)
    h_dwg, tok = _exchange_start(dwg_p, "scatter", tok, "scatter_dw_gate_start")
    dwu_p = _dw_rows(dpm, u2, NDEV // 2, FFN_PAIR, min(n, 1024), tok, "dw_up").reshape(parts)
    h_dwu, tok = _exchange_start(dwu_p, "scatter", tok, "scatter_dw_up_start")
    du2 = _ffn_du2(dgm, dpm, wg_g, wu_g, tok)
    dr1, da1, acc1 = _ln1_bwd(du2, dr2, xh1, rs1, a1, ln1_g, ln1_b, sc2, g1)
    dwo_p = _dw_rows(o, da1, NDEV, 2 * HEAD, tk, loss_p, "dw_out")
    h_dwo, tok = _exchange_start(dwo_p, "scatter", loss_p, "scatter_dw_out_start")
    do = _outproj_bwd(da1, wout_g, tok)
    dqa, dka, dva, dsink = _attn_window_bwd(t_all, o, do, lse_a, sink_logit)
    dqb, dkb, dvb = _attn_global_bwd(t_all, o, do, lse_b)
    dh_all, dnorm = _qkv_bwd_prep(dqa, dka, dva, dqb, dkb, dvb, h_all, q_norm_g, k_norm_g, cos, sa, sb)
    grad_x, acc0 = _qkv_bwd(dh_all, win_g, xs, cts, dr1, sc_pair)

    misc = _pad_cols(jnp.concatenate([dnorm[0:1], dnorm[1:2], dsink[:, 0:4, 0].reshape(1, 8)], axis=1), d)
    part = jnp.concatenate([
        acc0[0:2], acc1[4:5], acc1[1:2], acc1[0:1], acc2[2:3],
        acc0[2:4],
        acc1[2:4], acc2[0:2],
        misc, jnp.zeros((3, d), F32)], axis=0)
    gath = _exchange(part, False, "gather_small")
    dm_batch = gath[:, 0:6, :].reshape(NDEV, 6 * d)
    dm_ctx = _pad_cols(gath[:, 6:8, :].reshape(NDEV, 2 * d), 6 * d)
    dm16 = lax.dynamic_slice(jnp.concatenate([dm_batch, dm_ctx], axis=0), (0, me * e_sh), (16, e_sh))
    dw_ada, drow = _ada_bwd(dm16, c_all, w_ada[0])
    dcc = _exchange(drow, False, "gather_dcc")
    dwi_p = _dw_in(u_all, dh_all, dcc)
    h_dwi, tok = _exchange_start(dwi_p, "scatter", dcc, "scatter_dw_in_start")

    w_s = _pack_small(c_ctx, b_ada, ln1_g, ln1_b, ln2_g, ln2_b, q_norm_g, k_norm_g, sink_logit, d)
    m_s = _pack_small(m_c_ctx, m_b_ada, m_ln1_g, m_ln1_b, m_ln2_g, m_ln2_b, m_q_norm_g, m_k_norm_g, m_sink_logit, d)
    v_s = _pack_small(v_c_ctx, v_b_ada, v_ln1_g, v_ln1_b, v_ln2_g, v_ln2_b, v_q_norm_g, v_k_norm_g, v_sink_logit, d)
    small = [_unpack_small(p, d) for p in _small_update(gath, dcc, c_ctx.reshape(1, d), w_s, m_s, v_s)]

    big = {}
    big["w_ada"] = _adamw(w_ada[0], dw_ada, m_w_ada[0], v_w_ada[0], "adamw_w_ada")
    late = tok
    big["w_down"] = _adamw(w_down[0], _exchange_wait(h_dwd, "scatter", late, "scatter_dw_down_wait"), m_w_down[0],
                           v_w_down[0], "adamw_w_down")
    for nm, wt, mt, vt, hd in (("w_gate", w_gate, m_w_gate, v_w_gate, h_dwg), ("w_up", w_up, m_w_up, v_w_up, h_dwu)):
        big[nm] = _adamw_t(wt[0], _exchange_wait(hd, "scatter", late, "scatter_d" + nm + "_wait"), mt[0], vt[0],
                           "adamw_" + nm)
    big["w_out"] = _adamw(w_out[0], _exchange_wait(h_dwo, "scatter", late, "scatter_dw_out_wait"), m_w_out[0], v_w_out[0],
                          "adamw_w_out")
    big["w_in"] = _adamw(w_in[0], _exchange_wait(h_dwi, "scatter", big["w_out"][1], "scatter_dw_in_wait"), m_w_in[0],
                         v_w_in[0], "adamw_w_in")

    names = ["c_ctx", "w_ada", "b_ada", "w_in", "q_norm_g", "k_norm_g", "sink_logit", "w_out", "ln1_g", "ln1_b",
             "w_gate", "w_up", "w_down", "ln2_g", "ln2_b"]
    outs = [loss, grad_x[None]]
    for k in range(4):
        for nm in names:
            outs.append(big[nm][k][None] if nm in big else small[k][nm])
    return tuple(outs)
```

```python
import functools

import jax
import jax.numpy as jnp
from jax import lax
from jax.experimental import pallas as pl
from jax.experimental.pallas import tpu as pltpu

F32 = jnp.float32
BF16 = jnp.bfloat16

NDEV = 8
HEAD = 128
CTX = 256
GRID_W = 64
WINDOW = 128
ROPE_THETA = 10000.0
EPS = 1e-6
SCALE = HEAD ** -0.5
LOG2E = 1.4426950408889634
QK_LOG2 = SCALE * LOG2E
ALPHA = 2.0 ** 0.25
FFN_SHARD = 704
FFN_TILE = 512
FFN_PAIR = 2 * FFN_SHARD
IN_SHARD = 384
NEG = -1e30

ADAM_LR = 0.001
ADAM_B1 = 0.9
ADAM_B2 = 0.999
ADAM_EPS = 1e-08
ADAM_WD = 0.01
ADAM_STEP = 10

VMEM_CAP = 56 * 1024 * 1024

_KINDS = ["rope"] * 10 + ["none"] * 2 + ["qnorm"] * 8 + ["knorm"] * 2 + ["none"] * 2

_NT = (((1,), (1,)), ((), ()))
_TN = (((0,), (0,)), ((), ()))


def _pallas(body, **kw):
    return pl.pallas_call(body, **kw)


def _params(vmem_bytes):
    return pltpu.CompilerParams(vmem_limit_bytes=int(min(VMEM_CAP, vmem_bytes)))


def _mb(n):
    return int(n * 1024 * 1024)


def _sigmoid(x):
    return 1.0 / (1.0 + jnp.exp(-x))


def _colsum(a):
    return jnp.sum(a, axis=0, keepdims=True)


def _rowmean(a):
    return jnp.mean(a, axis=-1, keepdims=True)


def _exchange(src, scatter, name, after=None):
    blk = src.shape[1:] if scatter else src.shape
    after = src if after is None else after

    def body(src_ref, after_ref, out_ref, send_sems, recv_sems, local_sem):
        x, y, c = lax.axis_index("x"), lax.axis_index("y"), lax.axis_index("c")
        me = 4 * x + 2 * y + c
        copies = []
        for t in range(1, NDEV):
            px = 1 - x if (t >> 2) & 1 else x
            py = 1 - y if (t >> 1) & 1 else y
            pc = 1 - c if t & 1 else c
            peer = 4 * px + 2 * py + pc
            cp = pltpu.make_async_remote_copy(
                src_ref=src_ref.at[peer] if scatter else src_ref,
                dst_ref=out_ref.at[me],
                send_sem=send_sems.at[t - 1],
                recv_sem=recv_sems.at[t - 1],
                device_id=(px, py, pc),
                device_id_type=pl.DeviceIdType.MESH,
            )
            cp.start()
            copies.append(cp)
        own = pltpu.make_async_copy(src_ref.at[me] if scatter else src_ref, out_ref.at[me], local_sem)
        own.start()
        for cp in copies:
            cp.wait()
        own.wait()

    return _pallas(
        body, name=name,
        out_shape=jax.ShapeDtypeStruct((NDEV,) + tuple(blk), src.dtype),
        in_specs=[pl.BlockSpec(memory_space=pl.ANY), pl.BlockSpec(memory_space=pl.ANY)],
        out_specs=pl.BlockSpec(memory_space=pl.ANY),
        scratch_shapes=[pltpu.SemaphoreType.DMA((NDEV - 1,)), pltpu.SemaphoreType.DMA((NDEV - 1,)),
                        pltpu.SemaphoreType.DMA(())],
    )(src, after)


_HBM = pl.BlockSpec(memory_space=pltpu.HBM)
_SEM = pl.BlockSpec(memory_space=pltpu.SEMAPHORE)
_ANY = pl.BlockSpec(memory_space=pl.ANY)
_EFFECT = pltpu.SideEffectType.DATAFLOW_SIDE_EFFECTING


def _exchange_copies(src_ref, land_ref, send_sems, recv_sems, mode):
    x, y, c = lax.axis_index("x"), lax.axis_index("y"), lax.axis_index("c")
    me = 4 * x + 2 * y + c
    scatter = mode == "scatter"
    copies = []
    for t in ((1, 2, 4, 6) if mode == "chip" else range(1, NDEV)):
        px = 1 - x if (t >> 2) & 1 else x
        py = 1 - y if (t >> 1) & 1 else y
        pc = 1 - c if t & 1 else c
        peer = 4 * px + 2 * py + pc
        copies.append(pltpu.make_async_remote_copy(
            src_ref=src_ref.at[peer] if scatter else src_ref,
            dst_ref=land_ref.at[me],
            send_sem=send_sems.at[t - 1],
            recv_sem=recv_sems.at[t - 1],
            device_id=(px, py, pc),
            device_id_type=pl.DeviceIdType.MESH,
        ))
    own = pltpu.make_async_copy(src_ref.at[me] if scatter else src_ref, land_ref.at[me], send_sems.at[NDEV - 1])
    return copies, own


def _forward_copies(land_ref, send_sems, recv_sems):
    x, y, c = lax.axis_index("x"), lax.axis_index("y"), lax.axis_index("c")
    copies = []
    for k, t in enumerate((2, 4, 6)):
        px = 1 - x if (t >> 2) & 1 else x
        py = 1 - y if (t >> 1) & 1 else y
        mine, theirs = 4 * px + 2 * py + c, 4 * px + 2 * py + (1 - c)
        send = pltpu.make_async_remote_copy(
            src_ref=land_ref.at[mine], dst_ref=land_ref.at[mine], send_sem=send_sems.at[k], recv_sem=recv_sems.at[k],
            device_id=(x, y, 1 - c), device_id_type=pl.DeviceIdType.MESH)
        recv = pltpu.make_async_remote_copy(
            src_ref=land_ref.at[theirs], dst_ref=land_ref.at[theirs], send_sem=send_sems.at[k], recv_sem=recv_sems.at[k],
            device_id=(x, y, 1 - c), device_id_type=pl.DeviceIdType.MESH)
        copies.append((send, recv))
    return copies


def _forward_start(land, after, name):
    def body(land_ref, after_ref, send_sems, recv_sems, land_thru, token):
        for send, _ in _forward_copies(land_ref, send_sems, recv_sems):
            send.start()
        token[...] = jnp.zeros_like(token)

    res = _pallas(
        body, name=name,
        out_shape=(pltpu.SemaphoreType.DMA((3,)), pltpu.SemaphoreType.DMA((3,)), pltpu.HBM(land.shape, land.dtype),
                   jax.ShapeDtypeStruct((8, HEAD), F32)),
        in_specs=(_HBM, _ANY), out_specs=(_SEM, _SEM, _HBM, pl.BlockSpec(memory_space=pltpu.VMEM)),
        input_output_aliases={0: 2},
        compiler_params=pltpu.CompilerParams(has_side_effects=_EFFECT),
    )(land, after)
    return res[:3], res[3]


def _forward_wait(handle, after, name):
    send_sems, recv_sems, land_thru = handle

    def body(land_ref, send_sems, recv_sems, after_ref, got_ref):
        for send, recv in _forward_copies(land_ref, send_sems, recv_sems):
            send.wait_send()
            recv.wait_recv()

    return _pallas(
        body, name=name,
        out_shape=pltpu.HBM(land_thru.shape, land_thru.dtype),
        in_specs=(_HBM, _SEM, _SEM, _ANY), out_specs=_HBM,
        input_output_aliases={0: 0},
        compiler_params=pltpu.CompilerParams(has_side_effects=_EFFECT),
    )(land_thru, send_sems, recv_sems, after)


def _exchange_start(src, mode, after, name):
    blk = src.shape[1:] if mode == "scatter" else src.shape
    land = lax.empty((NDEV,) + tuple(blk), src.dtype)

    def body(src_ref, land_ref, after_ref, send_sems, recv_sems, src_thru, land_thru, token):
        copies, own = _exchange_copies(src_ref, land_ref, send_sems, recv_sems, mode)
        for cp in copies:
            cp.start()
        own.start()
        token[...] = jnp.zeros_like(token)

    res = _pallas(
        body, name=name,
        out_shape=(pltpu.SemaphoreType.DMA((NDEV,)), pltpu.SemaphoreType.DMA((NDEV,)),
                   pltpu.HBM(src.shape, src.dtype), pltpu.HBM(land.shape, land.dtype),
                   jax.ShapeDtypeStruct((8, HEAD), F32)),
        in_specs=(_HBM, _HBM, _ANY), out_specs=(_SEM, _SEM, _HBM, _HBM, pl.BlockSpec(memory_space=pltpu.VMEM)),
        input_output_aliases={0: 2, 1: 3},
        compiler_params=pltpu.CompilerParams(has_side_effects=_EFFECT),
    )(pltpu.with_memory_space_constraint(src, pltpu.HBM), pltpu.with_memory_space_constraint(land, pltpu.HBM), after)
    return res[:4], res[4]


def _exchange_wait(handle, mode, after, name):
    send_sems, recv_sems, src_thru, land_thru = handle

    def body(src_ref, land_ref, send_sems, recv_sems, after_ref, src_dead, got_ref):
        copies, own = _exchange_copies(src_ref, land_ref, send_sems, recv_sems, mode)
        for cp in copies:
            cp.wait_send()
            cp.wait_recv()
        own.wait()

    return _pallas(
        body, name=name,
        out_shape=(pltpu.HBM(src_thru.shape, src_thru.dtype), pltpu.HBM(land_thru.shape, land_thru.dtype)),
        in_specs=(_HBM, _HBM, _SEM, _SEM, _ANY), out_specs=(_HBM, _HBM),
        input_output_aliases={0: 0, 1: 1},
        compiler_params=pltpu.CompilerParams(has_side_effects=_EFFECT),
    )(src_thru, land_thru, send_sems, recv_sems, after)[1]


def _ada_fwd(c_all, w, bias):
    r, d = c_all.shape
    e = w.shape[1]
    tn = 512

    def body(c_ref, w_ref, b_ref, o_ref):
        cv = c_ref[...]
        s = (cv * _sigmoid(cv)).astype(BF16)
        o_ref[...] = jnp.dot(s, w_ref[...].astype(BF16), preferred_element_type=F32) + b_ref[...]

    return _pallas(
        body, name="ada_fwd", grid=(e // tn,),
        out_shape=jax.ShapeDtypeStruct((r, e), F32),
        in_specs=[pl.BlockSpec((r, d), lambda j: (0, 0)), pl.BlockSpec((d, tn), lambda j: (0, j)),
                  pl.BlockSpec((1, tn), lambda j: (0, j))],
        out_specs=pl.BlockSpec((r, tn), lambda j: (0, j)),
        compiler_params=_params(_mb(24)),
    )(c_all, w, bias)


def _ada_bwd(dm16, c_all, w):
    d, e = w.shape
    tn = 512

    def body(dm_ref, c_ref, w_ref, dw_ref, dr_ref):
        j = pl.program_id(0)
        dm = dm_ref[...]
        rid = lax.broadcasted_iota(jnp.int32, dm.shape, 0)
        ctx_sum = jnp.sum(jnp.where(rid >= 8, dm, 0.0), axis=0, keepdims=True)
        rows = jnp.where(rid < 8, dm, jnp.where(rid == 8, jnp.broadcast_to(ctx_sum, dm.shape), 0.0)).astype(BF16)
        cv = c_ref[...]
        s = (cv * _sigmoid(cv)).astype(BF16)
        dw_ref[...] = lax.dot_general(s, rows, _TN, preferred_element_type=F32)
        part = lax.dot_general(rows, w_ref[...].astype(BF16), _NT, preferred_element_type=F32)

        @pl.when(j == 0)
        def _():
            dr_ref[...] = part

        @pl.when(j > 0)
        def _():
            dr_ref[...] += part

    return _pallas(
        body, name="ada_bwd", grid=(e // tn,),
        out_shape=(jax.ShapeDtypeStruct((d, e), F32), jax.ShapeDtypeStruct((16, d), F32)),
        in_specs=[pl.BlockSpec((16, tn), lambda j: (0, j)), pl.BlockSpec((16, d), lambda j: (0, 0)),
                  pl.BlockSpec((d, tn), lambda j: (0, j))],
        out_specs=(pl.BlockSpec((d, tn), lambda j: (0, j)), pl.BlockSpec((16, d), lambda j: (0, 0))),
        compiler_params=_params(_mb(32)),
    )(dm16, c_all, w)


def _rope(v, cos, sa, sb):
    return v * cos + (pltpu.roll(v, 96, 1) * sa + pltpu.roll(v, 32, 1) * sb)


def _rope_t(dt, cos, sa, sb):
    return dt * cos + (pltpu.roll(dt * sa, 32, 1) + pltpu.roll(dt * sb, 96, 1))


def _qkv_fwd(x, ct, sc, sh, win_g, qg, kg, cos, sa, sb):
    n, d = x.shape
    tm = CTX
    nlat = n // tm
    na = n + CTX
    wcols = NDEV * IN_SHARD

    def body(x_ref, ct_ref, sc_ref, sh_ref, w_ref, qg_ref, kg_ref, cos_ref, sa_ref, sb_ref, u_ref, h_ref, t_ref):
        i = pl.program_id(0)
        xin = jnp.where(i == nlat, ct_ref[...], x_ref[...])
        u = (xin * (1.0 + sc_ref[0]) + sh_ref[0]).astype(BF16)
        u_ref[...] = u
        cos, sa, sb = cos_ref[...], sa_ref[...], sb_ref[...]
        for j in range(NDEV):
            h = jnp.dot(u, w_ref[j], preferred_element_type=F32)
            h_ref[:, j * IN_SHARD:(j + 1) * IN_SHARD] = h
            for hh in range(3):
                hd = 3 * j + hh
                v = h[:, hh * HEAD:(hh + 1) * HEAD]
                kind = _KINDS[hd]
                if kind == "qnorm":
                    v = v * lax.rsqrt(_rowmean(v * v) + EPS) * qg_ref[...]
                elif kind == "knorm":
                    v = v * lax.rsqrt(_rowmean(v * v) + EPS) * kg_ref[...]
                if kind != "none":
                    v = _rope(v, cos, sa, sb)
                t_ref[:, hd * HEAD:(hd + 1) * HEAD] = v.astype(BF16)

    lat = lambda i: (jnp.minimum(i, nlat - 1), 0)
    row = lambda i: (i, 0)
    const2 = lambda i: (0, 0)
    return _pallas(
        body, name="qkv_fwd", grid=(nlat + 1,),
        out_shape=(jax.ShapeDtypeStruct((na, d), BF16), jax.ShapeDtypeStruct((na, wcols), F32),
                   jax.ShapeDtypeStruct((na, wcols), BF16)),
        in_specs=[pl.BlockSpec((tm, d), lat), pl.BlockSpec((tm, d), const2),
                  pl.BlockSpec((1, 1, d), lambda i: (i // nlat, 0, 0)),
                  pl.BlockSpec((1, 1, d), lambda i: (i // nlat, 0, 0)),
                  pl.BlockSpec((NDEV, d, IN_SHARD), lambda i: (0, 0, 0)),
                  pl.BlockSpec((1, HEAD), const2), pl.BlockSpec((1, HEAD), const2),
                  pl.BlockSpec((tm, HEAD), row), pl.BlockSpec((tm, HEAD), row), pl.BlockSpec((tm, HEAD), row)],
        out_specs=(pl.BlockSpec((tm, d), row), pl.BlockSpec((tm, wcols), row), pl.BlockSpec((tm, wcols), row)),
        compiler_params=_params(_mb(56)),
    )(x, ct, sc, sh, win_g, qg, kg, cos, sa, sb)


def _qkv_bwd_prep(dqa, dka, dva, dqb, dkb, dvb, h_all, qg, kg, cos, sa, sb):
    na, wcols = h_all.shape
    n = na - CTX
    tm = CTX
    nlat = n // tm

    def body(dqa_ref, dka_ref, dva_ref, dqb_ref, dkb_ref, dvb_ref, h_ref, qg_ref, kg_ref, cos_ref, sa_ref, sb_ref,
             dh_ref, dg_ref):
        i = pl.program_id(0)

        @pl.when(i == 0)
        def _():
            dg_ref[...] = jnp.zeros_like(dg_ref)

        cos, sa, sb = cos_ref[...], sa_ref[...], sb_ref[...]
        is_lat = i < nlat
        for hd in range(24):
            kind = _KINDS[hd]
            if hd < 8:
                dt = jnp.where(is_lat, dqa_ref[:, hd * HEAD:(hd + 1) * HEAD], 0.0)
            elif hd < 10:
                dt = dka_ref[:, (hd - 8) * HEAD:(hd - 7) * HEAD]
            elif hd < 12:
                dt = dva_ref[:, (hd - 10) * HEAD:(hd - 9) * HEAD]
            elif hd < 20:
                dt = jnp.where(is_lat, dqb_ref[:, (hd - 12) * HEAD:(hd - 11) * HEAD], 0.0)
            elif hd < 22:
                dt = dkb_ref[:, (hd - 20) * HEAD:(hd - 19) * HEAD]
            else:
                dt = dvb_ref[:, (hd - 22) * HEAD:(hd - 21) * HEAD]
            if kind != "none":
                dt = _rope_t(dt, cos, sa, sb)
            if kind in ("qnorm", "knorm"):
                g_ref = qg_ref if kind == "qnorm" else kg_ref
                r0 = 0 if kind == "qnorm" else 1
                xv = h_ref[:, hd * HEAD:(hd + 1) * HEAD]
                xn = xv * lax.rsqrt(_rowmean(xv * xv) + EPS)
                dg_ref[r0:r0 + 1, :] += _colsum(dt * xn)
                dxn = dt * g_ref[...]
                dt = lax.rsqrt(_rowmean(xv * xv) + EPS) * (dxn - xn * _rowmean(dxn * xn))
            dh_ref[:, hd * HEAD:(hd + 1) * HEAD] = dt.astype(BF16)

    lat = lambda i: (jnp.minimum(i, nlat - 1), 0)
    row = lambda i: (i, 0)
    const2 = lambda i: (0, 0)
    return _pallas(
        body, name="qkv_bwd_prep", grid=(nlat + 1,),
        out_shape=(jax.ShapeDtypeStruct((na, wcols), BF16), jax.ShapeDtypeStruct((8, HEAD), F32)),
        in_specs=[pl.BlockSpec((tm, 8 * HEAD), lat), pl.BlockSpec((tm, 2 * HEAD), row), pl.BlockSpec((tm, 2 * HEAD), row),
                  pl.BlockSpec((tm, 8 * HEAD), lat), pl.BlockSpec((tm, 2 * HEAD), row), pl.BlockSpec((tm, 2 * HEAD), row),
                  pl.BlockSpec((tm, wcols), row),
                  pl.BlockSpec((1, HEAD), const2), pl.BlockSpec((1, HEAD), const2),
                  pl.BlockSpec((tm, HEAD), row), pl.BlockSpec((tm, HEAD), row), pl.BlockSpec((tm, HEAD), row)],
        out_specs=(pl.BlockSpec((tm, wcols), row), pl.BlockSpec((8, HEAD), const2)),
        compiler_params=_params(_mb(40)),
    )(dqa, dka, dva, dqb, dkb, dvb, h_all, qg, kg, cos, sa, sb)


def _window_keys(k_ref, v_ref, n, na):
    i = pl.program_id(1)
    tq = WINDOW
    start = pl.multiple_of(jnp.clip((i - 1) * tq, 0, n - 3 * tq), tq)
    kk = jnp.concatenate([k_ref[pl.ds(start, 3 * tq), :], k_ref[n:na, :]], axis=0)
    vv = jnp.concatenate([v_ref[pl.ds(start, 3 * tq), :], v_ref[n:na, :]], axis=0)
    nk = 3 * tq + CTX
    col = lax.broadcasted_iota(jnp.int32, (4 * tq, nk), 1)
    rowi = lax.broadcasted_iota(jnp.int32, (4 * tq, nk), 0)
    qpos = i * tq + (rowi & (tq - 1))
    valid = (jnp.abs(qpos - (start + col)) <= WINDOW) | (col >= 3 * tq)
    return kk, vv, valid, start


def _stack_heads(ref, width=HEAD):
    return jnp.concatenate([ref[:, g * HEAD:g * HEAD + width] for g in range(4)], axis=0)


def _sink_column(sink_ref, kv, tq):
    grp = lax.broadcasted_iota(jnp.int32, (4 * tq, 1), 0) // tq
    col = jnp.zeros((4 * tq, 1), F32)
    for g in range(4):
        col = jnp.where(grp == g, sink_ref[0, 4 * kv + g] * LOG2E, col)
    return col


def _attn_window_fwd(t_all, sink, after):
    na = t_all.shape[0]
    n = na - CTX
    tq = WINDOW

    def body(sink_ref, q_ref, k_ref, v_ref, after_ref, o_ref, lse_ref):
        kv = pl.program_id(0)
        kk, vv, valid, _ = _window_keys(k_ref, v_ref, n, na)
        t = lax.dot_general(_stack_heads(q_ref), kk, _NT, preferred_element_type=F32) * QK_LOG2
        t = jnp.where(valid, t, NEG)
        sk = _sink_column(sink_ref, kv, tq)
        m = jnp.maximum(jnp.max(t, axis=-1, keepdims=True), sk)
        p = jnp.exp2(t - m)
        l = jnp.sum(p, axis=-1, keepdims=True) + jnp.exp2(sk - m)
        o = jnp.dot(p.astype(BF16), vv, preferred_element_type=F32) * (1.0 / l)
        lse = m + jnp.log2(l)
        for g in range(4):
            o_ref[:, g * HEAD:(g + 1) * HEAD] = o[g * tq:(g + 1) * tq]
            lse_ref[:, g * HEAD:(g + 1) * HEAD] = jnp.broadcast_to(lse[g * tq:(g + 1) * tq], (tq, HEAD))

    blk = pl.BlockSpec((tq, 4 * HEAD), lambda kv, i: (i, kv))
    return _pallas(
        body, name="attn_window_fwd", grid=(2, n // tq),
        out_shape=(jax.ShapeDtypeStruct((n, 16 * HEAD), F32), jax.ShapeDtypeStruct((n, 8 * HEAD), F32)),
        in_specs=[pl.BlockSpec(memory_space=pltpu.SMEM), blk,
                  pl.BlockSpec((na, HEAD), lambda kv, i: (0, 8 + kv)),
                  pl.BlockSpec((na, HEAD), lambda kv, i: (0, 10 + kv)), _ANY],
        out_specs=(blk, blk),
        compiler_params=_params(_mb(32)),
    )(sink, t_all, t_all, t_all, after)


def _attn_global_fwd(t_all, o_part):
    na = t_all.shape[0]
    n = na - CTX
    tq = 256

    def body(q_ref, k_ref, v_ref, o_in_ref, o_ref, lse_ref):
        kk, vv = k_ref[...], v_ref[...]
        for g in range(4):
            q = q_ref[:, g * HEAD:(g + 1) * HEAD]
            t = lax.dot_general(q, kk, _NT, preferred_element_type=F32) * QK_LOG2
            m = jnp.max(t, axis=-1, keepdims=True)
            p = jnp.exp2(t - m)
            l = jnp.sum(p, axis=-1, keepdims=True)
            o_ref[:, g * HEAD:(g + 1) * HEAD] = jnp.dot(p.astype(BF16), vv, preferred_element_type=F32) * (1.0 / l)
            lse_ref[:, g * HEAD:(g + 1) * HEAD] = jnp.broadcast_to(m + jnp.log2(l), (tq, HEAD))

    return _pallas(
        body, name="attn_global_fwd", grid=(2, n // tq),
        out_shape=(jax.ShapeDtypeStruct((n, 16 * HEAD), F32), jax.ShapeDtypeStruct((n, 8 * HEAD), F32)),
        in_specs=[pl.BlockSpec((tq, 4 * HEAD), lambda kv, i: (i, 3 + kv)),
                  pl.BlockSpec((na, HEAD), lambda kv, i: (0, 20 + kv)),
                  pl.BlockSpec((na, HEAD), lambda kv, i: (0, 22 + kv)), _ANY],
        out_specs=(pl.BlockSpec((tq, 4 * HEAD), lambda kv, i: (i, 2 + kv)),
                   pl.BlockSpec((tq, 4 * HEAD), lambda kv, i: (i, kv))),
        input_output_aliases={3: 0},
        compiler_params=_params(_mb(48)),
    )(t_all, t_all, t_all, o_part)


def _attn_window_bwd(t_all, o, do, lse, sink):
    na = t_all.shape[0]
    n = na - CTX
    tq = WINDOW

    def body(sink_ref, q_ref, k_ref, v_ref, o_ref, do_ref, lse_ref, dq_ref, dk_ref, dv_ref, dsink_ref):
        kv = pl.program_id(0)

        @pl.when(pl.program_id(1) == 0)
        def _():
            dk_ref[...] = jnp.zeros_like(dk_ref)
            dv_ref[...] = jnp.zeros_like(dv_ref)
            dsink_ref[...] = jnp.zeros_like(dsink_ref)

        kk, vv, valid, start = _window_keys(k_ref, v_ref, n, na)
        q = _stack_heads(q_ref)
        t = lax.dot_general(q, kk, _NT, preferred_element_type=F32) * QK_LOG2
        t = jnp.where(valid, t, NEG)
        lse = _stack_heads(lse_ref, 1)
        p = jnp.exp2(t - lse)
        dof = _stack_heads(do_ref)
        delta = jnp.sum(dof * _stack_heads(o_ref), axis=-1, keepdims=True)
        dob = dof.astype(BF16)
        dv_acc = lax.dot_general(p.astype(BF16), dob, _TN, preferred_element_type=F32)
        dp = lax.dot_general(dob, vv, _NT, preferred_element_type=F32)
        ds = (p * (dp - delta) * SCALE).astype(BF16)
        dq = jnp.dot(ds, kk, preferred_element_type=F32)
        dk_acc = lax.dot_general(ds, q, _TN, preferred_element_type=F32)
        dsk = -(jnp.exp2(_sink_column(sink_ref, kv, tq) - lse) * delta)
        for g in range(4):
            dq_ref[:, g * HEAD:(g + 1) * HEAD] = dq[g * tq:(g + 1) * tq]
            dsink_ref[0, g:g + 1, :] += jnp.broadcast_to(_colsum(dsk[g * tq:(g + 1) * tq]), (1, HEAD))
        dk_ref[pl.ds(start, 3 * tq), :] += dk_acc[:3 * tq]
        dv_ref[pl.ds(start, 3 * tq), :] += dv_acc[:3 * tq]
        dk_ref[n:na, :] += dk_acc[3 * tq:]
        dv_ref[n:na, :] += dv_acc[3 * tq:]

    blk = pl.BlockSpec((tq, 4 * HEAD), lambda kv, i: (i, kv))
    kvout = pl.BlockSpec((na, HEAD), lambda kv, i: (0, kv))
    return _pallas(
        body, name="attn_window_bwd", grid=(2, n // tq),
        out_shape=(jax.ShapeDtypeStruct((n, 8 * HEAD), F32), jax.ShapeDtypeStruct((na, 2 * HEAD), F32),
                   jax.ShapeDtypeStruct((na, 2 * HEAD), F32), jax.ShapeDtypeStruct((2, 8, HEAD), F32)),
        in_specs=[pl.BlockSpec(memory_space=pltpu.SMEM), blk,
                  pl.BlockSpec((na, HEAD), lambda kv, i: (0, 8 + kv)),
                  pl.BlockSpec((na, HEAD), lambda kv, i: (0, 10 + kv)),
                  blk, blk, blk],
        out_specs=(blk, kvout, kvout, pl.BlockSpec((1, 8, HEAD), lambda kv, i: (kv, 0, 0))),
        compiler_params=_params(_mb(40)),
    )(sink, t_all, t_all, t_all, o, do, lse)


def _attn_global_bwd(t_all, kt, o, do, lse):
    na = t_all.shape[0]
    n = na - CTX
    tq = 256

    def body(q_ref, k_ref, v_ref, kt_ref, o_ref, do_ref, lse_ref, dq_ref, dk_ref, dv_ref, dkt_acc, dvt_acc):
        i = pl.program_id(1)

        @pl.when(i == 0)
        def _():
            dkt_acc[...] = jnp.zeros_like(dkt_acc)
            dvt_acc[...] = jnp.zeros_like(dvt_acc)

        kk, vv, kt_v = k_ref[...], v_ref[...], kt_ref[...]
        dkt = jnp.zeros((HEAD, na), F32)
        dvt = jnp.zeros((HEAD, na), F32)
        for g in range(4):
            q = q_ref[:, g * HEAD:(g + 1) * HEAD]
            t = lax.dot_general(q, kk, _NT, preferred_element_type=F32) * QK_LOG2
            p = jnp.exp2(t - lse_ref[:, g * HEAD:g * HEAD + 1])
            dof = do_ref[:, g * HEAD:(g + 1) * HEAD]
            delta = jnp.sum(dof * o_ref[:, g * HEAD:(g + 1) * HEAD], axis=-1, keepdims=True)
            dob = dof.astype(BF16)
            dvt = dvt + lax.dot_general(dob, p.astype(BF16), _TN, preferred_element_type=F32)
            dp = lax.dot_general(dob, vv, _NT, preferred_element_type=F32)
            ds = (p * (dp - delta) * SCALE).astype(BF16)
            dq_ref[:, g * HEAD:(g + 1) * HEAD] = lax.dot_general(kt_v, ds, _NT, preferred_element_type=F32).T
            dkt = dkt + lax.dot_general(q, ds, _TN, preferred_element_type=F32)
        dkt_acc[...] += dkt
        dvt_acc[...] += dvt

        @pl.when(i == pl.num_programs(1) - 1)
        def _():
            dk_ref[...] = dkt_acc[...].T
            dv_ref[...] = dvt_acc[...].T

    ospec = pl.BlockSpec((tq, 4 * HEAD), lambda kv, i: (i, 2 + kv))
    lspec = pl.BlockSpec((tq, 4 * HEAD), lambda kv, i: (i, kv))
    kvout = pl.BlockSpec((na, HEAD), lambda kv, i: (0, kv))
    return _pallas(
        body, name="attn_global_bwd", grid=(2, n // tq),
        out_shape=(jax.ShapeDtypeStruct((n, 8 * HEAD), F32), jax.ShapeDtypeStruct((na, 2 * HEAD), F32),
                   jax.ShapeDtypeStruct((na, 2 * HEAD), F32)),
        in_specs=[pl.BlockSpec((tq, 4 * HEAD), lambda kv, i: (i, 3 + kv)),
                  pl.BlockSpec((na, HEAD), lambda kv, i: (0, 20 + kv)),
                  pl.BlockSpec((na, HEAD), lambda kv, i: (0, 22 + kv)),
                  pl.BlockSpec((HEAD, na), lambda kv, i: (kv, 0)),
                  ospec, ospec, lspec],
        out_specs=(lspec, kvout, kvout),
        scratch_shapes=[pltpu.VMEM((HEAD, na), F32), pltpu.VMEM((HEAD, na), F32)],
        compiler_params=_params(_mb(56)),
    )(t_all, t_all, t_all, kt, o, do, lse)


def _outproj_ln1(o, wout, x, g1, lg, lb, sc2, sh2, after):
    n, d = x.shape
    tm = 256

    def body(o_ref, w_ref, x_ref, g1_ref, lg_ref, lb_ref, sc_ref, sh_ref, after_ref, a_ref, xh_ref, rs_ref, u_ref):
        a1 = jnp.dot(o_ref[...].astype(BF16), w_ref[...], preferred_element_type=F32)
        a_ref[...] = a1
        r = ALPHA * x_ref[...] + g1_ref[...] * a1
        dlt = r - _rowmean(r)
        rstd = lax.rsqrt(_rowmean(dlt * dlt) + EPS)
        xh = dlt * rstd
        xh_ref[...] = xh
        rs_ref[...] = rstd
        x1 = xh * lg_ref[...] + lb_ref[...]
        u_ref[...] = (x1 * (1.0 + sc_ref[...]) + sh_ref[...]).astype(BF16)

    row = lambda i: (i, 0)
    const2 = lambda i: (0, 0)
    vec = pl.BlockSpec((1, d), const2)
    big = pl.BlockSpec((tm, d), row)
    return _pallas(
        body, name="outproj_ln1", grid=(n // tm,),
        out_shape=(jax.ShapeDtypeStruct((n, d), F32), jax.ShapeDtypeStruct((n, d), F32),
                   jax.ShapeDtypeStruct((n, 1), F32), jax.ShapeDtypeStruct((n, d), BF16)),
        in_specs=[big, pl.BlockSpec((d, d), const2), big, vec, vec, vec, vec, vec, _ANY],
        out_specs=(big, big, pl.BlockSpec((tm, 1), row), big),
        compiler_params=_params(_mb(56)),
    )(o, wout, x, g1, lg, lb, sc2, sh2, after)


def _ffn_up(u2, wgt, wut, after):
    n, d = u2.shape
    f = wgt.shape[0]
    tm = min(1024, n)

    def body(u_ref, wg_ref, wu_ref, after_ref, sa_ref, sb_ref, hf_ref):
        u = u_ref[...]
        gv = lax.dot_general(u, wg_ref[...], _NT, preferred_element_type=F32)
        pv = lax.dot_general(u, wu_ref[...], _NT, preferred_element_type=F32)
        sg = _sigmoid(gv)
        silu = gv * sg
        sa_ref[...] = silu.astype(BF16)
        sb_ref[...] = (pv * (sg * (1.0 + gv * (1.0 - sg)))).astype(BF16)
        hf_ref[...] = (silu * pv).astype(BF16)

    tile = pl.BlockSpec((tm, FFN_TILE), lambda i, j: (i, j))
    wspec = pl.BlockSpec((FFN_TILE, d), lambda i, j: (j, 0))
    sds = jax.ShapeDtypeStruct((n, f), BF16)
    return _pallas(
        body, name="ffn_up", grid=(n // tm, f // FFN_TILE),
        out_shape=(sds, sds, sds),
        in_specs=[pl.BlockSpec((tm, d), lambda i, j: (i, 0)), wspec, wspec, _ANY],
        out_specs=(tile, tile, tile),
        compiler_params=_params(_mb(48)),
    )(u2, wgt, wut, after)


def _ffn_down(hf, wd):
    n, f = hf.shape
    d = wd.shape[1]
    tm, tn = min(1024, n), 512

    def body(h_ref, w_ref, o_ref):
        o_ref[...] = jnp.dot(h_ref[...], w_ref[...], preferred_element_type=F32)

    return _pallas(
        body, name="ffn_down", grid=(n // tm, d // tn),
        out_shape=jax.ShapeDtypeStruct((n, d), F32),
        in_specs=[pl.BlockSpec((tm, f), lambda i, j: (i, 0)), pl.BlockSpec((f, tn), lambda i, j: (0, j))],
        out_specs=pl.BlockSpec((tm, tn), lambda i, j: (i, j)),
        compiler_params=_params(_mb(56)),
    )(hf, wd)


def _ln2_loss(xh1, ffn, tgt, lg1, lb1, g2, lg2, lb2):
    n, d = xh1.shape
    tm = 256

    def body(xh_ref, f_ref, t_ref, lg1_ref, lb1_ref, g2_ref, lg2_ref, lb2_ref, dr_ref, df_ref, loss_ref, acc_ref):
        @pl.when(pl.program_id(0) == 0)
        def _():
            loss_ref[...] = jnp.zeros_like(loss_ref)
            acc_ref[...] = jnp.zeros_like(acc_ref)

        x1 = xh_ref[...] * lg1_ref[...] + lb1_ref[...]
        fv = f_ref[...]
        r = ALPHA * x1 + g2_ref[...] * fv
        dlt = r - _rowmean(r)
        rstd = lax.rsqrt(_rowmean(dlt * dlt) + EPS)
        xh2 = dlt * rstd
        err = xh2 * lg2_ref[...] + lb2_ref[...] - t_ref[...]
        loss_ref[...] += 0.5 * jnp.sum(_rowmean(err * err))
        dy = err * (1.0 / d)
        dyg = dy * lg2_ref[...]
        dr = rstd * (dyg - _rowmean(dyg) - xh2 * _rowmean(dyg * xh2))
        dr_ref[...] = dr
        df_ref[...] = (g2_ref[...] * dr).astype(BF16)
        acc_ref[0:1, :] += _colsum(dy * xh2)
        acc_ref[1:2, :] += _colsum(dy)
        acc_ref[2:3, :] += _colsum(dr * fv)

    row = lambda i: (i, 0)
    const2 = lambda i: (0, 0)
    vec = pl.BlockSpec((1, d), const2)
    big = pl.BlockSpec((tm, d), row)
    return _pallas(
        body, name="ln2_loss", grid=(n // tm,),
        out_shape=(jax.ShapeDtypeStruct((n, d), F32), jax.ShapeDtypeStruct((n, d), BF16),
                   jax.ShapeDtypeStruct((8, HEAD), F32), jax.ShapeDtypeStruct((8, d), F32)),
        in_specs=[big, big, big, vec, vec, vec, vec, vec],
        out_specs=(big, big, pl.BlockSpec((8, HEAD), const2), pl.BlockSpec((8, d), const2)),
        compiler_params=_params(_mb(48)),
    )(xh1, ffn, tgt, lg1, lb1, g2, lg2, lb2)


def _ffn_dhf(df, wd, sa, sb):
    n, d = df.shape
    f = sa.shape[1]
    tm = min(1024, n)

    def body(df_ref, w_ref, sa_ref, sb_ref, dg_ref, dp_ref):
        dhf = lax.dot_general(df_ref[...], w_ref[...], _NT, preferred_element_type=F32)
        dp_ref[...] = (dhf * sa_ref[...].astype(F32)).astype(BF16)
        dg_ref[...] = (dhf * sb_ref[...].astype(F32)).astype(BF16)

    tile = pl.BlockSpec((tm, FFN_TILE), lambda i, j: (i, j))
    sds = jax.ShapeDtypeStruct((n, f), BF16)
    return _pallas(
        body, name="ffn_dhf", grid=(n // tm, f // FFN_TILE),
        out_shape=(sds, sds),
        in_specs=[pl.BlockSpec((tm, d), lambda i, j: (i, 0)), pl.BlockSpec((FFN_TILE, d), lambda i, j: (j, 0)),
                  tile, tile],
        out_specs=(tile, tile),
        compiler_params=_params(_mb(48)),
    )(df, wd, sa, sb)


def _ffn_du2(dg, dp, wgt, wut, after):
    n, f = dg.shape
    d = wgt.shape[1]
    tm = min(1024, n)

    def body(dg_ref, dp_ref, wg_ref, wu_ref, after_ref, o_ref):
        part = (jnp.dot(dg_ref[...], wg_ref[...], preferred_element_type=F32)
                + jnp.dot(dp_ref[...], wu_ref[...], preferred_element_type=F32))

        @pl.when(pl.program_id(1) == 0)
        def _():
            o_ref[...] = part

        @pl.when(pl.program_id(1) > 0)
        def _():
            o_ref[...] += part

    tile = pl.BlockSpec((tm, FFN_TILE), lambda i, j: (i, j))
    wspec = pl.BlockSpec((FFN_TILE, d), lambda i, j: (j, 0))
    return _pallas(
        body, name="ffn_du2", grid=(n // tm, f // FFN_TILE),
        out_shape=jax.ShapeDtypeStruct((n, d), F32),
        in_specs=[tile, tile, wspec, wspec, _ANY],
        out_specs=pl.BlockSpec((tm, d), lambda i, j: (i, 0)),
        compiler_params=_params(_mb(48)),
    )(dg, dp, wgt, wut, after)


def _ln1_bwd(du2, dr2, xh1, rs1, a1, lg1, lb1, sc2, g1):
    n, d = du2.shape
    tm = 256

    def body(du_ref, dr2_ref, xh_ref, rs_ref, a_ref, lg_ref, lb_ref, sc_ref, g1_ref, dr1_ref, da_ref, acc_ref):
        @pl.when(pl.program_id(0) == 0)
        def _():
            acc_ref[...] = jnp.zeros_like(acc_ref)

        du = du_ref[...]
        xh = xh_ref[...]
        x1 = xh * lg_ref[...] + lb_ref[...]
        dx1 = ALPHA * dr2_ref[...] + du * (1.0 + sc_ref[...])
        dxg = dx1 * lg_ref[...]
        dr1 = rs_ref[...] * (dxg - _rowmean(dxg) - xh * _rowmean(dxg * xh))
        dr1_ref[...] = dr1
        da_ref[...] = (g1_ref[...] * dr1).astype(BF16)
        acc_ref[0:1, :] += _colsum(du * x1)
        acc_ref[1:2, :] += _colsum(du)
        acc_ref[2:3, :] += _colsum(dx1 * xh)
        acc_ref[3:4, :] += _colsum(dx1)
        acc_ref[4:5, :] += _colsum(dr1 * a_ref[...])

    row = lambda i: (i, 0)
    const2 = lambda i: (0, 0)
    vec = pl.BlockSpec((1, d), const2)
    big = pl.BlockSpec((tm, d), row)
    return _pallas(
        body, name="ln1_bwd", grid=(n // tm,),
        out_shape=(jax.ShapeDtypeStruct((n, d), F32), jax.ShapeDtypeStruct((n, d), BF16),
                   jax.ShapeDtypeStruct((8, d), F32)),
        in_specs=[big, big, big, pl.BlockSpec((tm, 1), row), big, vec, vec, vec, vec],
        out_specs=(big, big, pl.BlockSpec((8, d), const2)),
        compiler_params=_params(_mb(48)),
    )(du2, dr2, xh1, rs1, a1, lg1, lb1, sc2, g1)


def _dw_in(u, dh, after):
    m, k = u.shape
    tm = CTX
    half = 4 * IN_SHARD

    def body(u_ref, dh_ref, after_ref, o_ref, acc_ref):
        part = lax.dot_general(u_ref[...], dh_ref[...], _TN, preferred_element_type=F32)
        i = pl.program_id(1)

        @pl.when(i == 0)
        def _():
            acc_ref[...] = part

        @pl.when(i > 0)
        def _():
            acc_ref[...] += part

        @pl.when(i == pl.num_programs(1) - 1)
        def _():
            for jj in range(4):
                o_ref[jj] = acc_ref[:, jj * IN_SHARD:(jj + 1) * IN_SHARD].astype(BF16)

    return _pallas(
        body, name="dw_in", grid=(2, m // tm),
        out_shape=jax.ShapeDtypeStruct((NDEV, k, IN_SHARD), BF16),
        in_specs=[pl.BlockSpec((tm, k), lambda jh, i: (i, 0)), pl.BlockSpec((tm, half), lambda jh, i: (i, jh)), _ANY],
        out_specs=pl.BlockSpec((4, k, IN_SHARD), lambda jh, i: (jh, 0, 0)),
        scratch_shapes=[pltpu.VMEM((k, half), F32)],
        compiler_params=_params(_mb(48)),
    )(u, dh, after)


def _dw_rows(a, b, nblk, bw, tm, after, name):
    m = a.shape[0]
    nn = b.shape[1]

    def body(a_ref, b_ref, after_ref, o_ref, acc_ref):
        part = lax.dot_general(a_ref[...].astype(BF16), b_ref[...], _TN, preferred_element_type=F32)
        i = pl.program_id(1)

        @pl.when(i == 0)
        def _():
            acc_ref[...] = part

        @pl.when(i > 0)
        def _():
            acc_ref[...] += part

        @pl.when(i == pl.num_programs(1) - 1)
        def _():
            o_ref[0] = acc_ref[...].astype(BF16)

    return _pallas(
        body, name=name, grid=(nblk, m // tm),
        out_shape=jax.ShapeDtypeStruct((nblk, bw, nn), BF16),
        in_specs=[pl.BlockSpec((tm, bw), lambda j, i: (i, j)), pl.BlockSpec((tm, nn), lambda j, i: (i, 0)), _ANY],
        out_specs=pl.BlockSpec((1, bw, nn), lambda j, i: (j, 0, 0)),
        scratch_shapes=[pltpu.VMEM((bw, nn), F32)],
        compiler_params=_params(_mb(56)),
    )(a, b, after)


def _outproj_bwd(da1, wout, after):
    n, d = da1.shape
    tm = 512

    def body(a_ref, w_ref, after_ref, o_ref):
        o_ref[...] = lax.dot_general(a_ref[...], w_ref[...], _NT, preferred_element_type=F32)

    return _pallas(
        body, name="outproj_bwd", grid=(n // tm,),
        out_shape=jax.ShapeDtypeStruct((n, d), F32),
        in_specs=[pl.BlockSpec((tm, d), lambda i: (i, 0)), pl.BlockSpec((d, d), lambda i: (0, 0)), _ANY],
        out_specs=pl.BlockSpec((tm, d), lambda i: (i, 0)),
        compiler_params=_params(_mb(48)),
    )(da1, wout, after)


def _qkv_bwd(dh, win_g, x, ct, dr1, sc):
    na, wcols = dh.shape
    n, d = x.shape
    tm = CTX
    nlat = n // tm

    def body(dh_ref, w_ref, x_ref, ct_ref, dr_ref, sc_ref, gx_ref, acc_ref):
        i = pl.program_id(0)

        @pl.when(i == 0)
        def _():
            acc_ref[...] = jnp.zeros_like(acc_ref)

        du = jnp.zeros((tm, d), F32)
        for j in range(NDEV):
            du = du + lax.dot_general(dh_ref[:, j * IN_SHARD:(j + 1) * IN_SHARD], w_ref[j], _NT,
                                      preferred_element_type=F32)

        @pl.when(i < nlat)
        def _():
            gx_ref[...] = ALPHA * dr_ref[...] + du * (1.0 + sc_ref[0])
            acc_ref[0:1, :] += _colsum(du)
            acc_ref[1:2, :] += _colsum(du * x_ref[...])

        @pl.when(i == nlat)
        def _():
            acc_ref[2:3, :] += _colsum(du)
            acc_ref[3:4, :] += _colsum(du * ct_ref[...])

    lat = lambda i: (jnp.minimum(i, nlat - 1), 0)
    const2 = lambda i: (0, 0)
    return _pallas(
        body, name="qkv_bwd", grid=(nlat + 1,),
        out_shape=(jax.ShapeDtypeStruct((n, d), F32), jax.ShapeDtypeStruct((8, d), F32)),
        in_specs=[pl.BlockSpec((tm, wcols), lambda i: (i, 0)), pl.BlockSpec((NDEV, d, IN_SHARD), lambda i: (0, 0, 0)),
                  pl.BlockSpec((tm, d), lat), pl.BlockSpec((tm, d), const2), pl.BlockSpec((tm, d), lat),
                  pl.BlockSpec((1, 1, d), lambda i: (0, 0, 0))],
        out_specs=(pl.BlockSpec((tm, d), lat), pl.BlockSpec((8, d), const2)),
        compiler_params=_params(_mb(56)),
    )(dh, win_g, x, ct, dr1, sc)


def _adam_math(w, g, m, v):
    m2 = ADAM_B1 * m + (1.0 - ADAM_B1) * g
    v2 = ADAM_B2 * v + (1.0 - ADAM_B2) * (g * g)
    m_hat = m2 * (1.0 / (1.0 - ADAM_B1 ** ADAM_STEP))
    v_hat = v2 * (1.0 / (1.0 - ADAM_B2 ** ADAM_STEP))
    delta = -ADAM_LR * (m_hat / (jnp.sqrt(v_hat) + ADAM_EPS) + ADAM_WD * w)
    return delta, m2, v2


def _adamw(w, gsrc, m, v, name):
    r, c = w.shape
    parts = gsrc.ndim == 3
    cg = gsrc.shape[-1]
    tr = r
    while tr * c * 4 > _mb(1) and tr % 32 == 0:
        tr //= 2

    def body(w_ref, g_ref, m_ref, v_ref, go_ref, d_ref, mo_ref, vo_ref):
        if parts:
            g = g_ref[0].astype(F32)
            for s in range(1, NDEV):
                g = g + g_ref[s].astype(F32)
            g = g[:, :c]
        else:
            g = g_ref[...]
        delta, m2, v2 = _adam_math(w_ref[...], g, m_ref[...], v_ref[...])
        go_ref[...] = g
        d_ref[...] = delta
        mo_ref[...] = m2
        vo_ref[...] = v2

    tile = pl.BlockSpec((tr, c), lambda i: (i, 0))
    gspec = pl.BlockSpec((NDEV, tr, cg), lambda i: (0, i, 0)) if parts else tile
    sds = jax.ShapeDtypeStruct((r, c), F32)
    return _pallas(
        body, name=name, grid=(r // tr,),
        out_shape=(sds, sds, sds, sds),
        in_specs=[tile, gspec, tile, tile],
        out_specs=(tile, tile, tile, tile),
        compiler_params=_params(_mb(48)),
    )(w, gsrc, m, v)


def _adamw_t(w, gsrc_t, m, v, name):
    r, c = w.shape
    tr = 256

    def body(w_ref, g_ref, m_ref, v_ref, go_ref, d_ref, mo_ref, vo_ref):
        gt = g_ref[0].astype(F32)
        for s in range(1, NDEV):
            gt = gt + g_ref[s].astype(F32)
        g = gt.T
        delta, m2, v2 = _adam_math(w_ref[...], g, m_ref[...], v_ref[...])
        go_ref[...] = g
        d_ref[...] = delta
        mo_ref[...] = m2
        vo_ref[...] = v2

    tile = pl.BlockSpec((tr, c), lambda i: (i, 0))
    sds = jax.ShapeDtypeStruct((r, c), F32)
    return _pallas(
        body, name=name, grid=(r // tr,),
        out_shape=(sds, sds, sds, sds),
        in_specs=[tile, pl.BlockSpec((NDEV, c, tr), lambda i: (0, 0, i)), tile, tile],
        out_specs=(tile, tile, tile, tile),
        compiler_params=_params(_mb(48)),
    )(w, gsrc_t, m, v)


def _small_update(gath, dcc, cc, w_s, m_s, v_s):
    d = w_s.shape[1]

    def body(g_ref, dcc_ref, cc_ref, w_ref, m_ref, v_ref, go_ref, d_ref, mo_ref, vo_ref):
        s = g_ref[0]
        for b in range(1, NDEV):
            s = s + g_ref[b]
        dsl = dcc_ref[0, 8:9, :]
        for b in range(1, NDEV):
            dsl = dsl + dcc_ref[b, 8:9, :]
        cv = cc_ref[...]
        sg = _sigmoid(cv)
        go_ref[...] = jnp.zeros_like(go_ref)
        go_ref[0:1, :] = dsl * (sg * (1.0 + cv * (1.0 - sg)))
        go_ref[1:3, :] = s[0:2] + s[6:8]
        go_ref[3:7, :] = s[2:6]
        go_ref[7:12, :] = s[8:13]
        delta, m2, v2 = _adam_math(w_ref[...], go_ref[...], m_ref[...], v_ref[...])
        d_ref[...] = delta
        mo_ref[...] = m2
        vo_ref[...] = v2

    full = pl.BlockSpec((16, d), lambda: (0, 0))
    g3 = pl.BlockSpec((NDEV, 16, d), lambda: (0, 0, 0))
    sds = jax.ShapeDtypeStruct((16, d), F32)
    return _pallas(
        body, name="small_update",
        out_shape=(sds, sds, sds, sds),
        in_specs=[g3, g3, pl.BlockSpec((1, d), lambda: (0, 0)), full, full, full],
        out_specs=(full, full, full, full),
        compiler_params=_params(_mb(24)),
    )(gath, dcc, cc, w_s, m_s, v_s)


def _rope_tables(n):
    rows = n // GRID_W
    row_ids = jnp.repeat(jnp.arange(rows, dtype=F32), GRID_W)
    col_ids = jnp.tile(jnp.arange(GRID_W, dtype=F32), rows)
    axis_dim = HEAD // 2
    inv_freq = jnp.power(ROPE_THETA, -jnp.arange(0, axis_dim, 2, dtype=F32) / axis_dim)
    ang_r = row_ids[:, None] * inv_freq
    ang_c = col_ids[:, None] * inv_freq
    ang = jnp.concatenate([ang_r, ang_r, ang_c, ang_c], axis=-1)
    cos, sin = jnp.cos(ang), jnp.sin(ang)
    first = (jnp.arange(HEAD) % (HEAD // 2)) < HEAD // 4
    sa = jnp.where(first, -sin, 0.0)
    sb = jnp.where(first, 0.0, sin)
    ones = jnp.ones((CTX, HEAD), F32)
    zeros = jnp.zeros((CTX, HEAD), F32)
    return (jnp.concatenate([cos, ones], 0), jnp.concatenate([sa, zeros], 0), jnp.concatenate([sb, zeros], 0))


def _pad_cols(a, width):
    return jnp.pad(a, ((0, 0), (0, width - a.shape[1])))


def _pad_rows(a, rows):
    return jnp.pad(a, ((0, rows - a.shape[0]), (0, 0)))


def _pack_small(c_ctx, b_ada, ln1_g, ln1_b, ln2_g, ln2_b, qg, kg, sink, d):
    misc = _pad_cols(jnp.concatenate([qg, kg, sink], axis=1), d)
    rows = jnp.concatenate([c_ctx.reshape(1, d), b_ada.reshape(6, d), ln1_g, ln1_b, ln2_g, ln2_b, misc], axis=0)
    return _pad_rows(rows, 16)


def _unpack_small(p, d):
    return dict(c_ctx=p[0], b_ada=p[1:7].reshape(1, 6 * d), ln1_g=p[7:8], ln1_b=p[8:9], ln2_g=p[9:10], ln2_b=p[10:11],
                q_norm_g=p[11:12, 0:HEAD], k_norm_g=p[11:12, HEAD:2 * HEAD], sink_logit=p[11:12, 2 * HEAD:2 * HEAD + 8])


def kernel(x, c, ctx, c_ctx, w_ada, b_ada, w_in, q_norm_g, k_norm_g, sink_logit, w_out, ln1_g, ln1_b, w_gate, w_up, w_down, ln2_g, ln2_b, loss_target, m_c_ctx, m_w_ada, m_b_ada, m_w_in, m_q_norm_g, m_k_norm_g, m_sink_logit, m_w_out, m_ln1_g, m_ln1_b, m_w_gate, m_w_up, m_w_down, m_ln2_g, m_ln2_b, v_c_ctx, v_w_ada, v_b_ada, v_w_in, v_q_norm_g, v_k_norm_g, v_sink_logit, v_w_out, v_ln1_g, v_ln1_b, v_w_gate, v_w_up, v_w_down, v_ln2_g, v_ln2_b):
    xs, cts, tgt = x[0], ctx[0], loss_target[0]
    n, d = xs.shape
    assert cts.shape == (CTX, d) and w_in.shape[2] == IN_SHARD and w_gate.shape[2] == FFN_SHARD
    me = 4 * lax.axis_index("x") + 2 * lax.axis_index("y") + lax.axis_index("c")
    e_sh = w_ada.shape[2]

    c_g = _exchange(_pad_rows(c, 8), False, "gather_c")
    c_all = jnp.concatenate([c_g[:, 0, :], _pad_rows(c_ctx.reshape(1, d), 8)], axis=0)
    bias_sh = lax.dynamic_slice(b_ada, (0, me * e_sh), (1, e_sh))
    mods_g = _exchange(_ada_fwd(c_all, w_ada[0], bias_sh), False, "gather_mods")
    mods = jnp.transpose(mods_g, (1, 0, 2)).reshape(16, NDEV * e_sh)
    mine = lax.dynamic_slice(mods, (me, 0), (1, 6 * d))
    sh1, sc1, g1, sh2, sc2, g2 = [mine[:, k * d:(k + 1) * d] for k in range(6)]
    csh1, csc1 = mods[8:9, 0:d], mods[8:9, d:2 * d]
    sc_pair = jnp.stack([sc1, csc1])
    sh_pair = jnp.stack([sh1, csh1])

    h_win, tok = _exchange_start(w_in[0].astype(BF16), "chip", mods, "gather_w_in_start")
    h_wout, tok = _exchange_start(w_out[0].astype(BF16), "chip", tok, "gather_w_out_start")
    h_wg, tok = _exchange_start(w_gate[0].T.astype(BF16), "chip", tok, "gather_w_gate_start")
    h_wu, tok = _exchange_start(w_up[0].T.astype(BF16), "chip", tok, "gather_w_up_start")
    h_wd, tok = _exchange_start(w_down[0].astype(BF16), "chip", tok, "gather_w_down_start")

    cos, sa, sb = _rope_tables(n)
    f_win, tok = _forward_start(_exchange_wait(h_win, "chip", tok, "gather_w_in_wait"), tok, "forward_w_in_start")
    win_g = _forward_wait(f_win, tok, "forward_w_in_wait")
    u_all, h_all, t_all = _qkv_fwd(xs, cts, sc_pair, sh_pair, win_g, q_norm_g, k_norm_g, cos, sa, sb)
    f_wout, tok = _forward_start(_exchange_wait(h_wout, "chip", t_all, "gather_w_out_wait"), t_all, "forward_w_out_start")
    o_a, lse_a = _attn_window_fwd(t_all, sink_logit, tok)
    o, lse_b = _attn_global_fwd(t_all, o_a)
    f_wg, tok = _forward_start(_exchange_wait(h_wg, "chip", o, "gather_w_gate_wait"), o, "forward_w_gate_start")
    f_wu, tok = _forward_start(_exchange_wait(h_wu, "chip", tok, "gather_w_up_wait"), tok, "forward_w_up_start")
    wout_g = _forward_wait(f_wout, tok, "forward_w_out_wait").reshape(d, d)
    a1, xh1, rs1, u2 = _outproj_ln1(o, wout_g, xs, g1, ln1_g, ln1_b, sc2, sh2, tok)
    f_wd, tok = _forward_start(_exchange_wait(h_wd, "chip", rs1, "gather_w_down_wait"), rs1, "forward_w_down_start")
    ffn_w = (NDEV * FFN_SHARD, d)
    wg_g = _forward_wait(f_wg, tok, "forward_w_gate_wait").reshape(ffn_w)
    wu_g = _forward_wait(f_wu, tok, "forward_w_up_wait").reshape(ffn_w)
    sa_f, sb_f, hf = _ffn_up(u2, wg_g, wu_g, tok)
    wd_g = _forward_wait(f_wd, hf, "forward_w_down_wait").reshape(ffn_w)
    ffn = _ffn_down(hf, wd_g)
    dr2, df, loss_p, acc2 = _ln2_loss(xh1, ffn, tgt, ln1_g, ln1_b, g2, ln2_g, ln2_b)
    loss = lax.psum(loss_p[0, 0], ("x", "y", "c"))

    tk = min(n, 2048)
    parts = (NDEV, FFN_SHARD, d)
    dgm, dpm = _ffn_dhf(df, wd_g, sa_f, sb_f)
    dwd_p = _dw_rows(hf, df, NDEV // 2, FFN_PAIR, min(n, 1024), loss_p, "dw_down").reshape(parts)
    h_dwd, tok = _exchange_start(dwd_p, "scatter", loss.reshape(1, 1), "scatter_dw_down_start")
    dwg_p = _dw_rows(dgm, u2, NDEV // 2, FFN_PAIR, min(n, 1024), tok, "dw_gate").reshape(parts)
    h_dwg, tok = _exchange_start(dwg_p, "scatter", tok, "scatter_dw_gate_start")
    dwu_p = _dw_rows(dpm, u2, NDEV // 2, FFN_PAIR, min(n, 1024), tok, "dw_up").reshape(parts)
    h_dwu, tok = _exchange_start(dwu_p, "scatter", tok, "scatter_dw_up_start")
    du2 = _ffn_du2(dgm, dpm, wg_g, wu_g, tok)
    dr1, da1, acc1 = _ln1_bwd(du2, dr2, xh1, rs1, a1, ln1_g, ln1_b, sc2, g1)
    dwo_p = _dw_rows(o, da1, NDEV, 2 * HEAD, tk, loss_p, "dw_out")
    h_dwo, tok = _exchange_start(dwo_p, "scatter", loss_p, "scatter_dw_out_start")
    do = _outproj_bwd(da1, wout_g, tok)
    dqa, dka, dva, dsink = _attn_window_bwd(t_all, o, do, lse_a, sink_logit)
    kt_b = t_all[:, 20 * HEAD:22 * HEAD].T
    dqb, dkb, dvb = _attn_global_bwd(t_all, kt_b, o, do, lse_b)
    dh_all, dnorm = _qkv_bwd_prep(dqa, dka, dva, dqb, dkb, dvb, h_all, q_norm_g, k_norm_g, cos, sa, sb)
    grad_x, acc0 = _qkv_bwd(dh_all, win_g, xs, cts, dr1, sc_pair)

    misc = _pad_cols(jnp.concatenate([dnorm[0:1], dnorm[1:2], dsink[:, 0:4, 0].reshape(1, 8)], axis=1), d)
    part = jnp.concatenate([
        acc0[0:2], acc1[4:5], acc1[1:2], acc1[0:1], acc2[2:3],
        acc0[2:4],
        acc1[2:4], acc2[0:2],
        misc, jnp.zeros((3, d), F32)], axis=0)
    gath = _exchange(part, False, "gather_small")
    dm_batch = gath[:, 0:6, :].reshape(NDEV, 6 * d)
    dm_ctx = _pad_cols(gath[:, 6:8, :].reshape(NDEV, 2 * d), 6 * d)
    dm16 = lax.dynamic_slice(jnp.concatenate([dm_batch, dm_ctx], axis=0), (0, me * e_sh), (16, e_sh))
    dw_ada, drow = _ada_bwd(dm16, c_all, w_ada[0])
    dcc = _exchange(drow, False, "gather_dcc")
    dwi_p = _dw_in(u_all, dh_all, dcc)
    h_dwi, tok = _exchange_start(dwi_p, "scatter", dcc, "scatter_dw_in_start")

    w_s = _pack_small(c_ctx, b_ada, ln1_g, ln1_b, ln2_g, ln2_b, q_norm_g, k_norm_g, sink_logit, d)
    m_s = _pack_small(m_c_ctx, m_b_ada, m_ln1_g, m_ln1_b, m_ln2_g, m_ln2_b, m_q_norm_g, m_k_norm_g, m_sink_logit, d)
    v_s = _pack_small(v_c_ctx, v_b_ada, v_ln1_g, v_ln1_b, v_ln2_g, v_ln2_b, v_q_norm_g, v_k_norm_g, v_sink_logit, d)
    small = [_unpack_small(p, d) for p in _small_update(gath, dcc, c_ctx.reshape(1, d), w_s, m_s, v_s)]

    big = {}
    big["w_ada"] = _adamw(w_ada[0], dw_ada, m_w_ada[0], v_w_ada[0], "adamw_w_ada")
    late = tok
    big["w_down"] = _adamw(w_down[0], _exchange_wait(h_dwd, "scatter", late, "scatter_dw_down_wait"), m_w_down[0],
                           v_w_down[0], "adamw_w_down")
    for nm, wt, mt, vt, hd in (("w_gate", w_gate, m_w_gate, v_w_gate, h_dwg), ("w_up", w_up, m_w_up, v_w_up, h_dwu)):
        big[nm] = _adamw_t(wt[0], _exchange_wait(hd, "scatter", late, "scatter_d" + nm + "_wait"), mt[0], vt[0],
                           "adamw_" + nm)
    big["w_out"] = _adamw(w_out[0], _exchange_wait(h_dwo, "scatter", late, "scatter_dw_out_wait"), m_w_out[0], v_w_out[0],
                          "adamw_w_out")
    big["w_in"] = _adamw(w_in[0], _exchange_wait(h_dwi, "scatter", big["w_out"][1], "scatter_dw_in_wait"), m_w_in[0],
                         v_w_in[0], "adamw_w_in")

    names = ["c_ctx", "w_ada", "b_ada", "w_in", "q_norm_g", "k_norm_g", "sink_logit", "w_out", "ln1_g", "ln1_b",
             "w_gate", "w_up", "w_down", "ln2_g", "ln2_b"]
    outs = [loss, grad_x[None]]
    for k in range(4):
        for nm in names:
            outs.append(big[nm][k][None] if nm in big else small[k][nm])
    return tuple(outs)
```

```python
import functools

import jax
import jax.numpy as jnp
from jax import lax
from jax.experimental import pallas as pl
from jax.experimental.pallas import tpu as pltpu

F32 = jnp.float32
BF16 = jnp.bfloat16

NDEV = 8
HEAD = 128
CTX = 256
GRID_W = 64
WINDOW = 128
ROPE_THETA = 10000.0
EPS = 1e-6
SCALE = HEAD ** -0.5
LOG2E = 1.4426950408889634
QK_LOG2 = SCALE * LOG2E
ALPHA = 2.0 ** 0.25
FFN_SHARD = 704
FFN_TILE = 512
FFN_PAIR = 2 * FFN_SHARD
IN_SHARD = 384
NEG = -1e30

ADAM_LR = 0.001
ADAM_B1 = 0.9
ADAM_B2 = 0.999
ADAM_EPS = 1e-08
ADAM_WD = 0.01
ADAM_STEP = 10

VMEM_CAP = 56 * 1024 * 1024

_KINDS = ["rope"] * 10 + ["none"] * 2 + ["qnorm"] * 8 + ["knorm"] * 2 + ["none"] * 2

_NT = (((1,), (1,)), ((), ()))
_TN = (((0,), (0,)), ((), ()))


def _pallas(body, **kw):
    return pl.pallas_call(body, **kw)


def _params(vmem_bytes):
    return pltpu.CompilerParams(vmem_limit_bytes=int(min(VMEM_CAP, vmem_bytes)))


def _mb(n):
    return int(n * 1024 * 1024)


def _sigmoid(x):
    return 1.0 / (1.0 + jnp.exp(-x))


def _colsum(a):
    return jnp.sum(a, axis=0, keepdims=True)


def _rowmean(a):
    return jnp.mean(a, axis=-1, keepdims=True)


def _exchange(src, scatter, name, after=None):
    blk = src.shape[1:] if scatter else src.shape
    after = src if after is None else after

    def body(src_ref, after_ref, out_ref, send_sems, recv_sems, local_sem):
        x, y, c = lax.axis_index("x"), lax.axis_index("y"), lax.axis_index("c")
        me = 4 * x + 2 * y + c
        copies = []
        for t in range(1, NDEV):
            px = 1 - x if (t >> 2) & 1 else x
            py = 1 - y if (t >> 1) & 1 else y
            pc = 1 - c if t & 1 else c
            peer = 4 * px + 2 * py + pc
            cp = pltpu.make_async_remote_copy(
                src_ref=src_ref.at[peer] if scatter else src_ref,
                dst_ref=out_ref.at[me],
                send_sem=send_sems.at[t - 1],
                recv_sem=recv_sems.at[t - 1],
                device_id=(px, py, pc),
                device_id_type=pl.DeviceIdType.MESH,
            )
            cp.start()
            copies.append(cp)
        own = pltpu.make_async_copy(src_ref.at[me] if scatter else src_ref, out_ref.at[me], local_sem)
        own.start()
        for cp in copies:
            cp.wait()
        own.wait()

    return _pallas(
        body, name=name,
        out_shape=jax.ShapeDtypeStruct((NDEV,) + tuple(blk), src.dtype),
        in_specs=[pl.BlockSpec(memory_space=pl.ANY), pl.BlockSpec(memory_space=pl.ANY)],
        out_specs=pl.BlockSpec(memory_space=pl.ANY),
        scratch_shapes=[pltpu.SemaphoreType.DMA((NDEV - 1,)), pltpu.SemaphoreType.DMA((NDEV - 1,)),
                        pltpu.SemaphoreType.DMA(())],
    )(src, after)


_HBM = pl.BlockSpec(memory_space=pltpu.HBM)
_SEM = pl.BlockSpec(memory_space=pltpu.SEMAPHORE)
_ANY = pl.BlockSpec(memory_space=pl.ANY)
_EFFECT = pltpu.SideEffectType.DATAFLOW_SIDE_EFFECTING


def _exchange_copies(src_ref, land_ref, send_sems, recv_sems, mode):
    x, y, c = lax.axis_index("x"), lax.axis_index("y"), lax.axis_index("c")
    me = 4 * x + 2 * y + c
    scatter = mode == "scatter"
    copies = []
    for t in ((1, 2, 4, 6) if mode == "chip" else range(1, NDEV)):
        px = 1 - x if (t >> 2) & 1 else x
        py = 1 - y if (t >> 1) & 1 else y
        pc = 1 - c if t & 1 else c
        peer = 4 * px + 2 * py + pc
        copies.append(pltpu.make_async_remote_copy(
            src_ref=src_ref.at[peer] if scatter else src_ref,
            dst_ref=land_ref.at[me],
            send_sem=send_sems.at[t - 1],
            recv_sem=recv_sems.at[t - 1],
            device_id=(px, py, pc),
            device_id_type=pl.DeviceIdType.MESH,
        ))
    own = pltpu.make_async_copy(src_ref.at[me] if scatter else src_ref, land_ref.at[me], send_sems.at[NDEV - 1])
    return copies, own


def _forward_copies(land_ref, send_sems, recv_sems):
    x, y, c = lax.axis_index("x"), lax.axis_index("y"), lax.axis_index("c")
    copies = []
    for k, t in enumerate((2, 4, 6)):
        px = 1 - x if (t >> 2) & 1 else x
        py = 1 - y if (t >> 1) & 1 else y
        mine, theirs = 4 * px + 2 * py + c, 4 * px + 2 * py + (1 - c)
        send = pltpu.make_async_remote_copy(
            src_ref=land_ref.at[mine], dst_ref=land_ref.at[mine], send_sem=send_sems.at[k], recv_sem=recv_sems.at[k],
            device_id=(x, y, 1 - c), device_id_type=pl.DeviceIdType.MESH)
        recv = pltpu.make_async_remote_copy(
            src_ref=land_ref.at[theirs], dst_ref=land_ref.at[theirs], send_sem=send_sems.at[k], recv_sem=recv_sems.at[k],
            device_id=(x, y, 1 - c), device_id_type=pl.DeviceIdType.MESH)
        copies.append((send, recv))
    return copies


def _forward_start(land, after, name):
    def body(land_ref, after_ref, send_sems, recv_sems, land_thru, token):
        for send, _ in _forward_copies(land_ref, send_sems, recv_sems):
            send.start()
        token[...] = jnp.zeros_like(token)

    res = _pallas(
        body, name=name,
        out_shape=(pltpu.SemaphoreType.DMA((3,)), pltpu.SemaphoreType.DMA((3,)), pltpu.HBM(land.shape, land.dtype),
                   jax.ShapeDtypeStruct((8, HEAD), F32)),
        in_specs=(_HBM, _ANY), out_specs=(_SEM, _SEM, _HBM, pl.BlockSpec(memory_space=pltpu.VMEM)),
        input_output_aliases={0: 2},
        compiler_params=pltpu.CompilerParams(has_side_effects=_EFFECT),
    )(land, after)
    return res[:3], res[3]


def _forward_wait(handle, after, name):
    send_sems, recv_sems, land_thru = handle

    def body(land_ref, send_sems, recv_sems, after_ref, got_ref):
        for send, recv in _forward_copies(land_ref, send_sems, recv_sems):
            send.wait_send()
            recv.wait_recv()

    return _pallas(
        body, name=name,
        out_shape=pltpu.HBM(land_thru.shape, land_thru.dtype),
        in_specs=(_HBM, _SEM, _SEM, _ANY), out_specs=_HBM,
        input_output_aliases={0: 0},
        compiler_params=pltpu.CompilerParams(has_side_effects=_EFFECT),
    )(land_thru, send_sems, recv_sems, after)


def _exchange_start(src, mode, after, name):
    blk = src.shape[1:] if mode == "scatter" else src.shape
    land = lax.empty((NDEV,) + tuple(blk), src.dtype)

    def body(src_ref, land_ref, after_ref, send_sems, recv_sems, src_thru, land_thru, token):
        copies, own = _exchange_copies(src_ref, land_ref, send_sems, recv_sems, mode)
        for cp in copies:
            cp.start()
        own.start()
        token[...] = jnp.zeros_like(token)

    res = _pallas(
        body, name=name,
        out_shape=(pltpu.SemaphoreType.DMA((NDEV,)), pltpu.SemaphoreType.DMA((NDEV,)),
                   pltpu.HBM(src.shape, src.dtype), pltpu.HBM(land.shape, land.dtype),
                   jax.ShapeDtypeStruct((8, HEAD), F32)),
        in_specs=(_HBM, _HBM, _ANY), out_specs=(_SEM, _SEM, _HBM, _HBM, pl.BlockSpec(memory_space=pltpu.VMEM)),
        input_output_aliases={0: 2, 1: 3},
        compiler_params=pltpu.CompilerParams(has_side_effects=_EFFECT),
    )(pltpu.with_memory_space_constraint(src, pltpu.HBM), pltpu.with_memory_space_constraint(land, pltpu.HBM), after)
    return res[:4], res[4]


def _exchange_wait(handle, mode, after, name):
    send_sems, recv_sems, src_thru, land_thru = handle

    def body(src_ref, land_ref, send_sems, recv_sems, after_ref, src_dead, got_ref):
        copies, own = _exchange_copies(src_ref, land_ref, send_sems, recv_sems, mode)
        for cp in copies:
            cp.wait_send()
            cp.wait_recv()
        own.wait()

    return _pallas(
        body, name=name,
        out_shape=(pltpu.HBM(src_thru.shape, src_thru.dtype), pltpu.HBM(land_thru.shape, land_thru.dtype)),
        in_specs=(_HBM, _HBM, _SEM, _SEM, _ANY), out_specs=(_HBM, _HBM),
        input_output_aliases={0: 0, 1: 1},
        compiler_params=pltpu.CompilerParams(has_side_effects=_EFFECT),
    )(src_thru, land_thru, send_sems, recv_sems, after)[1]


def _ada_fwd(c_all, w, bias):
    r, d = c_all.shape
    e = w.shape[1]
    tn = 512

    def body(c_ref, w_ref, b_ref, o_ref):
        cv = c_ref[...]
        s = (cv * _sigmoid(cv)).astype(BF16)
        o_ref[...] = jnp.dot(s, w_ref[...].astype(BF16), preferred_element_type=F32) + b_ref[...]

    return _pallas(
        body, name="ada_fwd", grid=(e // tn,),
        out_shape=jax.ShapeDtypeStruct((r, e), F32),
        in_specs=[pl.BlockSpec((r, d), lambda j: (0, 0)), pl.BlockSpec((d, tn), lambda j: (0, j)),
                  pl.BlockSpec((1, tn), lambda j: (0, j))],
        out_specs=pl.BlockSpec((r, tn), lambda j: (0, j)),
        compiler_params=_params(_mb(24)),
    )(c_all, w, bias)


def _ada_bwd(dm16, c_all, w):
    d, e = w.shape
    tn = 512

    def body(dm_ref, c_ref, w_ref, dw_ref, dr_ref):
        j = pl.program_id(0)
        dm = dm_ref[...]
        rid = lax.broadcasted_iota(jnp.int32, dm.shape, 0)
        ctx_sum = jnp.sum(jnp.where(rid >= 8, dm, 0.0), axis=0, keepdims=True)
        rows = jnp.where(rid < 8, dm, jnp.where(rid == 8, jnp.broadcast_to(ctx_sum, dm.shape), 0.0)).astype(BF16)
        cv = c_ref[...]
        s = (cv * _sigmoid(cv)).astype(BF16)
        dw_ref[...] = lax.dot_general(s, rows, _TN, preferred_element_type=F32)
        part = lax.dot_general(rows, w_ref[...].astype(BF16), _NT, preferred_element_type=F32)

        @pl.when(j == 0)
        def _():
            dr_ref[...] = part

        @pl.when(j > 0)
        def _():
            dr_ref[...] += part

    return _pallas(
        body, name="ada_bwd", grid=(e // tn,),
        out_shape=(jax.ShapeDtypeStruct((d, e), F32), jax.ShapeDtypeStruct((16, d), F32)),
        in_specs=[pl.BlockSpec((16, tn), lambda j: (0, j)), pl.BlockSpec((16, d), lambda j: (0, 0)),
                  pl.BlockSpec((d, tn), lambda j: (0, j))],
        out_specs=(pl.BlockSpec((d, tn), lambda j: (0, j)), pl.BlockSpec((16, d), lambda j: (0, 0))),
        compiler_params=_params(_mb(32)),
    )(dm16, c_all, w)


def _rope(v, cos, sa, sb):
    return v * cos + (pltpu.roll(v, 96, 1) * sa + pltpu.roll(v, 32, 1) * sb)


def _rope_t(dt, cos, sa, sb):
    return dt * cos + (pltpu.roll(dt * sa, 32, 1) + pltpu.roll(dt * sb, 96, 1))


def _qkv_fwd(x, ct, sc, sh, wint, qg, kg, cos, sa, sb):
    n, d = x.shape
    tm = CTX
    nlat = n // tm
    na = n + CTX
    wcols = wint.shape[0]

    def body(x_ref, ct_ref, sc_ref, sh_ref, w_ref, qg_ref, kg_ref, cos_ref, sa_ref, sb_ref, u_ref, h_ref, t_ref, kvt_ref):
        i = pl.program_id(0)
        xin = jnp.where(i == nlat, ct_ref[...], x_ref[...])
        u = (xin * (1.0 + sc_ref[0]) + sh_ref[0]).astype(BF16)
        u_ref[...] = u
        cos, sa, sb = cos_ref[...], sa_ref[...], sb_ref[...]
        h = lax.dot_general(u, w_ref[...], _NT, preferred_element_type=F32)
        h_ref[...] = h
        for hd in range(24):
            v = h[:, hd * HEAD:(hd + 1) * HEAD]
            kind = _KINDS[hd]
            if kind == "qnorm":
                v = v * lax.rsqrt(_rowmean(v * v) + EPS) * qg_ref[...]
            elif kind == "knorm":
                v = v * lax.rsqrt(_rowmean(v * v) + EPS) * kg_ref[...]
            if kind != "none":
                v = _rope(v, cos, sa, sb)
            t_ref[:, hd * HEAD:(hd + 1) * HEAD] = v.astype(BF16)
            if hd >= 20:
                kvt_ref[(hd - 20) * HEAD:(hd - 19) * HEAD, :] = v.T.astype(BF16)

    lat = lambda i: (jnp.minimum(i, nlat - 1), 0)
    row = lambda i: (i, 0)
    const2 = lambda i: (0, 0)
    return _pallas(
        body, name="qkv_fwd", grid=(nlat + 1,),
        out_shape=(jax.ShapeDtypeStruct((na, d), BF16), jax.ShapeDtypeStruct((na, wcols), F32),
                   jax.ShapeDtypeStruct((na, wcols), BF16), jax.ShapeDtypeStruct((4 * HEAD, na), BF16)),
        in_specs=[pl.BlockSpec((tm, d), lat), pl.BlockSpec((tm, d), const2),
                  pl.BlockSpec((1, 1, d), lambda i: (i // nlat, 0, 0)),
                  pl.BlockSpec((1, 1, d), lambda i: (i // nlat, 0, 0)),
                  pl.BlockSpec((wcols, d), const2),
                  pl.BlockSpec((1, HEAD), const2), pl.BlockSpec((1, HEAD), const2),
                  pl.BlockSpec((tm, HEAD), row), pl.BlockSpec((tm, HEAD), row), pl.BlockSpec((tm, HEAD), row)],
        out_specs=(pl.BlockSpec((tm, d), row), pl.BlockSpec((tm, wcols), row), pl.BlockSpec((tm, wcols), row),
                   pl.BlockSpec((4 * HEAD, tm), lambda i: (0, i))),
        compiler_params=_params(_mb(56)),
    )(x, ct, sc, sh, wint, qg, kg, cos, sa, sb)


def _qkv_bwd_prep(dqa, dka, dva, dqb, dkb, dvb, h_all, qg, kg, cos, sa, sb):
    na, wcols = h_all.shape
    n = na - CTX
    tm = CTX
    nlat = n // tm

    def body(dqa_ref, dka_ref, dva_ref, dqb_ref, dkb_ref, dvb_ref, h_ref, qg_ref, kg_ref, cos_ref, sa_ref, sb_ref,
             dh_ref, dg_ref):
        i = pl.program_id(0)

        @pl.when(i == 0)
        def _():
            dg_ref[...] = jnp.zeros_like(dg_ref)

        cos, sa, sb = cos_ref[...], sa_ref[...], sb_ref[...]
        is_lat = i < nlat
        for hd in range(24):
            kind = _KINDS[hd]
            if hd < 8:
                dt = jnp.where(is_lat, dqa_ref[:, hd * HEAD:(hd + 1) * HEAD], 0.0)
            elif hd < 10:
                dt = dka_ref[:, (hd - 8) * HEAD:(hd - 7) * HEAD]
            elif hd < 12:
                dt = dva_ref[:, (hd - 10) * HEAD:(hd - 9) * HEAD]
            elif hd < 20:
                dt = jnp.where(is_lat, dqb_ref[:, (hd - 12) * HEAD:(hd - 11) * HEAD], 0.0)
            elif hd < 22:
                dt = dkb_ref[:, (hd - 20) * HEAD:(hd - 19) * HEAD]
            else:
                dt = dvb_ref[:, (hd - 22) * HEAD:(hd - 21) * HEAD]
            if kind != "none":
                dt = _rope_t(dt, cos, sa, sb)
            if kind in ("qnorm", "knorm"):
                g_ref = qg_ref if kind == "qnorm" else kg_ref
                r0 = 0 if kind == "qnorm" else 1
                xv = h_ref[:, hd * HEAD:(hd + 1) * HEAD]
                xn = xv * lax.rsqrt(_rowmean(xv * xv) + EPS)
                dg_ref[r0:r0 + 1, :] += _colsum(dt * xn)
                dxn = dt * g_ref[...]
                dt = lax.rsqrt(_rowmean(xv * xv) + EPS) * (dxn - xn * _rowmean(dxn * xn))
            dh_ref[:, hd * HEAD:(hd + 1) * HEAD] = dt.astype(BF16)

    lat = lambda i: (jnp.minimum(i, nlat - 1), 0)
    row = lambda i: (i, 0)
    const2 = lambda i: (0, 0)
    return _pallas(
        body, name="qkv_bwd_prep", grid=(nlat + 1,),
        out_shape=(jax.ShapeDtypeStruct((na, wcols), BF16), jax.ShapeDtypeStruct((8, HEAD), F32)),
        in_specs=[pl.BlockSpec((tm, 8 * HEAD), lat), pl.BlockSpec((tm, 2 * HEAD), row), pl.BlockSpec((tm, 2 * HEAD), row),
                  pl.BlockSpec((tm, 8 * HEAD), lat), pl.BlockSpec((tm, 2 * HEAD), row), pl.BlockSpec((tm, 2 * HEAD), row),
                  pl.BlockSpec((tm, wcols), row),
                  pl.BlockSpec((1, HEAD), const2), pl.BlockSpec((1, HEAD), const2),
                  pl.BlockSpec((tm, HEAD), row), pl.BlockSpec((tm, HEAD), row), pl.BlockSpec((tm, HEAD), row)],
        out_specs=(pl.BlockSpec((tm, wcols), row), pl.BlockSpec((8, HEAD), const2)),
        compiler_params=_params(_mb(40)),
    )(dqa, dka, dva, dqb, dkb, dvb, h_all, qg, kg, cos, sa, sb)


def _window_keys(k_ref, v_ref, n, na):
    i = pl.program_id(1)
    tq = WINDOW
    start = pl.multiple_of(jnp.clip((i - 1) * tq, 0, n - 3 * tq), tq)
    kk = jnp.concatenate([k_ref[pl.ds(start, 3 * tq), :], k_ref[n:na, :]], axis=0)
    vv = jnp.concatenate([v_ref[pl.ds(start, 3 * tq), :], v_ref[n:na, :]], axis=0)
    nk = 3 * tq + CTX
    col = lax.broadcasted_iota(jnp.int32, (4 * tq, nk), 1)
    rowi = lax.broadcasted_iota(jnp.int32, (4 * tq, nk), 0)
    qpos = i * tq + (rowi & (tq - 1))
    valid = (jnp.abs(qpos - (start + col)) <= WINDOW) | (col >= 3 * tq)
    return kk, vv, valid, start


def _stack_heads(ref, width=HEAD):
    return jnp.concatenate([ref[:, g * HEAD:g * HEAD + width] for g in range(4)], axis=0)


def _sink_column(sink_ref, kv, tq):
    grp = lax.broadcasted_iota(jnp.int32, (4 * tq, 1), 0) // tq
    col = jnp.zeros((4 * tq, 1), F32)
    for g in range(4):
        col = jnp.where(grp == g, sink_ref[0, 4 * kv + g] * LOG2E, col)
    return col


def _attn_window_fwd(t_all, sink, after):
    na = t_all.shape[0]
    n = na - CTX
    tq = WINDOW

    def body(sink_ref, q_ref, k_ref, v_ref, after_ref, o_ref, lse_ref):
        kv = pl.program_id(0)
        kk, vv, valid, _ = _window_keys(k_ref, v_ref, n, na)
        t = lax.dot_general(_stack_heads(q_ref), kk, _NT, preferred_element_type=F32) * QK_LOG2
        t = jnp.where(valid, t, NEG)
        sk = _sink_column(sink_ref, kv, tq)
        m = jnp.maximum(jnp.max(t, axis=-1, keepdims=True), sk)
        p = jnp.exp2(t - m)
        l = jnp.sum(p, axis=-1, keepdims=True) + jnp.exp2(sk - m)
        o = jnp.dot(p.astype(BF16), vv, preferred_element_type=F32) * (1.0 / l)
        lse = m + jnp.log2(l)
        for g in range(4):
            o_ref[:, g * HEAD:(g + 1) * HEAD] = o[g * tq:(g + 1) * tq]
            lse_ref[:, g * HEAD:(g + 1) * HEAD] = jnp.broadcast_to(lse[g * tq:(g + 1) * tq], (tq, HEAD))

    blk = pl.BlockSpec((tq, 4 * HEAD), lambda kv, i: (i, kv))
    return _pallas(
        body, name="attn_window_fwd", grid=(2, n // tq),
        out_shape=(jax.ShapeDtypeStruct((n, 16 * HEAD), F32), jax.ShapeDtypeStruct((n, 8 * HEAD), F32)),
        in_specs=[pl.BlockSpec(memory_space=pltpu.SMEM), blk,
                  pl.BlockSpec((na, HEAD), lambda kv, i: (0, 8 + kv)),
                  pl.BlockSpec((na, HEAD), lambda kv, i: (0, 10 + kv)), _ANY],
        out_specs=(blk, blk),
        compiler_params=_params(_mb(32)),
    )(sink, t_all, t_all, t_all, after)


def _attn_global_fwd(t_all, kvt, o_part):
    na = t_all.shape[0]
    n = na - CTX
    tq = 256

    def body(q_ref, k_ref, vt_ref, o_in_ref, o_ref, lse_ref):
        kk, vt = k_ref[...], vt_ref[...]
        for g in range(4):
            q = q_ref[:, g * HEAD:(g + 1) * HEAD]
            t = lax.dot_general(q, kk, _NT, preferred_element_type=F32) * QK_LOG2
            m = jnp.max(t, axis=-1, keepdims=True)
            p = jnp.exp2(t - m)
            l = jnp.sum(p, axis=-1, keepdims=True)
            ot = lax.dot_general(vt, p.astype(BF16), _NT, preferred_element_type=F32)
            o_ref[:, g * HEAD:(g + 1) * HEAD] = ot.T * (1.0 / l)
            lse_ref[:, g * HEAD:(g + 1) * HEAD] = jnp.broadcast_to(m + jnp.log2(l), (tq, HEAD))

    return _pallas(
        body, name="attn_global_fwd", grid=(2, n // tq),
        out_shape=(jax.ShapeDtypeStruct((n, 16 * HEAD), F32), jax.ShapeDtypeStruct((n, 8 * HEAD), F32)),
        in_specs=[pl.BlockSpec((tq, 4 * HEAD), lambda kv, i: (i, 3 + kv)),
                  pl.BlockSpec((na, HEAD), lambda kv, i: (0, 20 + kv)),
                  pl.BlockSpec((HEAD, na), lambda kv, i: (2 + kv, 0)), _ANY],
        out_specs=(pl.BlockSpec((tq, 4 * HEAD), lambda kv, i: (i, 2 + kv)),
                   pl.BlockSpec((tq, 4 * HEAD), lambda kv, i: (i, kv))),
        input_output_aliases={3: 0},
        compiler_params=_params(_mb(48)),
    )(t_all, t_all, kvt, o_part)


def _attn_window_bwd(t_all, o, do, lse, sink):
    na = t_all.shape[0]
    n = na - CTX
    tq = WINDOW

    def body(sink_ref, q_ref, k_ref, v_ref, o_ref, do_ref, lse_ref, dq_ref, dk_ref, dv_ref, dsink_ref):
        kv = pl.program_id(0)

        @pl.when(pl.program_id(1) == 0)
        def _():
            dk_ref[...] = jnp.zeros_like(dk_ref)
            dv_ref[...] = jnp.zeros_like(dv_ref)
            dsink_ref[...] = jnp.zeros_like(dsink_ref)

        kk, vv, valid, start = _window_keys(k_ref, v_ref, n, na)
        q = _stack_heads(q_ref)
        t = lax.dot_general(q, kk, _NT, preferred_element_type=F32) * QK_LOG2
        t = jnp.where(valid, t, NEG)
        lse = _stack_heads(lse_ref, 1)
        p = jnp.exp2(t - lse)
        dof = _stack_heads(do_ref)
        delta = jnp.sum(dof * _stack_heads(o_ref), axis=-1, keepdims=True)
        dob = dof.astype(BF16)
        dv_acc = lax.dot_general(p.astype(BF16), dob, _TN, preferred_element_type=F32)
        dp = lax.dot_general(dob, vv, _NT, preferred_element_type=F32)
        ds = (p * (dp - delta) * SCALE).astype(BF16)
        dq = jnp.dot(ds, kk, preferred_element_type=F32)
        dk_acc = lax.dot_general(ds, q, _TN, preferred_element_type=F32)
        dsk = -(jnp.exp2(_sink_column(sink_ref, kv, tq) - lse) * delta)
        for g in range(4):
            dq_ref[:, g * HEAD:(g + 1) * HEAD] = dq[g * tq:(g + 1) * tq]
            dsink_ref[0, g:g + 1, :] += jnp.broadcast_to(_colsum(dsk[g * tq:(g + 1) * tq]), (1, HEAD))
        dk_ref[pl.ds(start, 3 * tq), :] += dk_acc[:3 * tq]
        dv_ref[pl.ds(start, 3 * tq), :] += dv_acc[:3 * tq]
        dk_ref[n:na, :] += dk_acc[3 * tq:]
        dv_ref[n:na, :] += dv_acc[3 * tq:]

    blk = pl.BlockSpec((tq, 4 * HEAD), lambda kv, i: (i, kv))
    kvout = pl.BlockSpec((na, HEAD), lambda kv, i: (0, kv))
    return _pallas(
        body, name="attn_window_bwd", grid=(2, n // tq),
        out_shape=(jax.ShapeDtypeStruct((n, 8 * HEAD), F32), jax.ShapeDtypeStruct((na, 2 * HEAD), F32),
                   jax.ShapeDtypeStruct((na, 2 * HEAD), F32), jax.ShapeDtypeStruct((2, 8, HEAD), F32)),
        in_specs=[pl.BlockSpec(memory_space=pltpu.SMEM), blk,
                  pl.BlockSpec((na, HEAD), lambda kv, i: (0, 8 + kv)),
                  pl.BlockSpec((na, HEAD), lambda kv, i: (0, 10 + kv)),
                  blk, blk, blk],
        out_specs=(blk, kvout, kvout, pl.BlockSpec((1, 8, HEAD), lambda kv, i: (kv, 0, 0))),
        compiler_params=_params(_mb(40)),
    )(sink, t_all, t_all, t_all, o, do, lse)


def _attn_global_bwd(t_all, kt, o, do, lse):
    na = t_all.shape[0]
    n = na - CTX
    tq = 256

    def body(q_ref, k_ref, v_ref, kt_ref, o_ref, do_ref, lse_ref, dq_ref, dk_ref, dv_ref, dkt_acc, dvt_acc):
        i = pl.program_id(1)

        @pl.when(i == 0)
        def _():
            dkt_acc[...] = jnp.zeros_like(dkt_acc)
            dvt_acc[...] = jnp.zeros_like(dvt_acc)

        kk, vv, kt_v = k_ref[...], v_ref[...], kt_ref[...]
        dkt = jnp.zeros((HEAD, na), F32)
        dvt = jnp.zeros((HEAD, na), F32)
        for g in range(4):
            q = q_ref[:, g * HEAD:(g + 1) * HEAD]
            t = lax.dot_general(q, kk, _NT, preferred_element_type=F32) * QK_LOG2
            p = jnp.exp2(t - lse_ref[:, g * HEAD:g * HEAD + 1])
            dof = do_ref[:, g * HEAD:(g + 1) * HEAD]
            delta = jnp.sum(dof * o_ref[:, g * HEAD:(g + 1) * HEAD], axis=-1, keepdims=True)
            dob = dof.astype(BF16)
            dvt = dvt + lax.dot_general(dob, p.astype(BF16), _TN, preferred_element_type=F32)
            dp = lax.dot_general(dob, vv, _NT, preferred_element_type=F32)
            ds = (p * (dp - delta) * SCALE).astype(BF16)
            dq_ref[:, g * HEAD:(g + 1) * HEAD] = lax.dot_general(kt_v, ds, _NT, preferred_element_type=F32).T
            dkt = dkt + lax.dot_general(q, ds, _TN, preferred_element_type=F32)
        dkt_acc[...] += dkt
        dvt_acc[...] += dvt

        @pl.when(i == pl.num_programs(1) - 1)
        def _():
            dk_ref[...] = dkt_acc[...].T
            dv_ref[...] = dvt_acc[...].T

    ospec = pl.BlockSpec((tq, 4 * HEAD), lambda kv, i: (i, 2 + kv))
    lspec = pl.BlockSpec((tq, 4 * HEAD), lambda kv, i: (i, kv))
    kvout = pl.BlockSpec((na, HEAD), lambda kv, i: (0, kv))
    return _pallas(
        body, name="attn_global_bwd", grid=(2, n // tq),
        out_shape=(jax.ShapeDtypeStruct((n, 8 * HEAD), F32), jax.ShapeDtypeStruct((na, 2 * HEAD), F32),
                   jax.ShapeDtypeStruct((na, 2 * HEAD), F32)),
        in_specs=[pl.BlockSpec((tq, 4 * HEAD), lambda kv, i: (i, 3 + kv)),
                  pl.BlockSpec((na, HEAD), lambda kv, i: (0, 20 + kv)),
                  pl.BlockSpec((na, HEAD), lambda kv, i: (0, 22 + kv)),
                  pl.BlockSpec((HEAD, na), lambda kv, i: (kv, 0)),
                  ospec, ospec, lspec],
        out_specs=(lspec, kvout, kvout),
        scratch_shapes=[pltpu.VMEM((HEAD, na), F32), pltpu.VMEM((HEAD, na), F32)],
        compiler_params=_params(_mb(56)),
    )(t_all, t_all, t_all, kt, o, do, lse)


def _outproj_ln1(o, wout, x, g1, lg, lb, sc2, sh2, after):
    n, d = x.shape
    tm = 256

    def body(o_ref, w_ref, x_ref, g1_ref, lg_ref, lb_ref, sc_ref, sh_ref, after_ref, a_ref, xh_ref, rs_ref, u_ref):
        a1 = jnp.dot(o_ref[...].astype(BF16), w_ref[...], preferred_element_type=F32)
        a_ref[...] = a1
        r = ALPHA * x_ref[...] + g1_ref[...] * a1
        dlt = r - _rowmean(r)
        rstd = lax.rsqrt(_rowmean(dlt * dlt) + EPS)
        xh = dlt * rstd
        xh_ref[...] = xh
        rs_ref[...] = rstd
        x1 = xh * lg_ref[...] + lb_ref[...]
        u_ref[...] = (x1 * (1.0 + sc_ref[...]) + sh_ref[...]).astype(BF16)

    row = lambda i: (i, 0)
    const2 = lambda i: (0, 0)
    vec = pl.BlockSpec((1, d), const2)
    big = pl.BlockSpec((tm, d), row)
    return _pallas(
        body, name="outproj_ln1", grid=(n // tm,),
        out_shape=(jax.ShapeDtypeStruct((n, d), F32), jax.ShapeDtypeStruct((n, d), F32),
                   jax.ShapeDtypeStruct((n, 1), F32), jax.ShapeDtypeStruct((n, d), BF16)),
        in_specs=[big, pl.BlockSpec((d, d), const2), big, vec, vec, vec, vec, vec, _ANY],
        out_specs=(big, big, pl.BlockSpec((tm, 1), row), big),
        compiler_params=_params(_mb(56)),
    )(o, wout, x, g1, lg, lb, sc2, sh2, after)


def _ffn_up(u2, wgt, wut, after):
    n, d = u2.shape
    f = wgt.shape[0]
    tm = min(1024, n)

    def body(u_ref, wg_ref, wu_ref, after_ref, sa_ref, sb_ref, hf_ref):
        u = u_ref[...]
        gv = lax.dot_general(u, wg_ref[...], _NT, preferred_element_type=F32)
        pv = lax.dot_general(u, wu_ref[...], _NT, preferred_element_type=F32)
        sg = _sigmoid(gv)
        silu = gv * sg
        sa_ref[...] = silu.astype(BF16)
        sb_ref[...] = (pv * (sg * (1.0 + gv * (1.0 - sg)))).astype(BF16)
        hf_ref[...] = (silu * pv).astype(BF16)

    tile = pl.BlockSpec((tm, FFN_TILE), lambda i, j: (i, j))
    wspec = pl.BlockSpec((FFN_TILE, d), lambda i, j: (j, 0))
    sds = jax.ShapeDtypeStruct((n, f), BF16)
    return _pallas(
        body, name="ffn_up", grid=(n // tm, f // FFN_TILE),
        out_shape=(sds, sds, sds),
        in_specs=[pl.BlockSpec((tm, d), lambda i, j: (i, 0)), wspec, wspec, _ANY],
        out_specs=(tile, tile, tile),
        compiler_params=_params(_mb(48)),
    )(u2, wgt, wut, after)


def _ffn_down(hf, wd):
    n, f = hf.shape
    d = wd.shape[1]
    tm, tn = min(1024, n), 512

    def body(h_ref, w_ref, o_ref):
        o_ref[...] = jnp.dot(h_ref[...], w_ref[...], preferred_element_type=F32)

    return _pallas(
        body, name="ffn_down", grid=(n // tm, d // tn),
        out_shape=jax.ShapeDtypeStruct((n, d), F32),
        in_specs=[pl.BlockSpec((tm, f), lambda i, j: (i, 0)), pl.BlockSpec((f, tn), lambda i, j: (0, j))],
        out_specs=pl.BlockSpec((tm, tn), lambda i, j: (i, j)),
        compiler_params=_params(_mb(56)),
    )(hf, wd)


def _ln2_loss(xh1, ffn, tgt, lg1, lb1, g2, lg2, lb2):
    n, d = xh1.shape
    tm = 256

    def body(xh_ref, f_ref, t_ref, lg1_ref, lb1_ref, g2_ref, lg2_ref, lb2_ref, dr_ref, df_ref, loss_ref, acc_ref):
        @pl.when(pl.program_id(0) == 0)
        def _():
            loss_ref[...] = jnp.zeros_like(loss_ref)
            acc_ref[...] = jnp.zeros_like(acc_ref)

        x1 = xh_ref[...] * lg1_ref[...] + lb1_ref[...]
        fv = f_ref[...]
        r = ALPHA * x1 + g2_ref[...] * fv
        dlt = r - _rowmean(r)
        rstd = lax.rsqrt(_rowmean(dlt * dlt) + EPS)
        xh2 = dlt * rstd
        err = xh2 * lg2_ref[...] + lb2_ref[...] - t_ref[...]
        loss_ref[...] += 0.5 * jnp.sum(_rowmean(err * err))
        dy = err * (1.0 / d)
        dyg = dy * lg2_ref[...]
        dr = rstd * (dyg - _rowmean(dyg) - xh2 * _rowmean(dyg * xh2))
        dr_ref[...] = dr
        df_ref[...] = (g2_ref[...] * dr).astype(BF16)
        acc_ref[0:1, :] += _colsum(dy * xh2)
        acc_ref[1:2, :] += _colsum(dy)
        acc_ref[2:3, :] += _colsum(dr * fv)

    row = lambda i: (i, 0)
    const2 = lambda i: (0, 0)
    vec = pl.BlockSpec((1, d), const2)
    big = pl.BlockSpec((tm, d), row)
    return _pallas(
        body, name="ln2_loss", grid=(n // tm,),
        out_shape=(jax.ShapeDtypeStruct((n, d), F32), jax.ShapeDtypeStruct((n, d), BF16),
                   jax.ShapeDtypeStruct((8, HEAD), F32), jax.ShapeDtypeStruct((8, d), F32)),
        in_specs=[big, big, big, vec, vec, vec, vec, vec],
        out_specs=(big, big, pl.BlockSpec((8, HEAD), const2), pl.BlockSpec((8, d), const2)),
        compiler_params=_params(_mb(48)),
    )(xh1, ffn, tgt, lg1, lb1, g2, lg2, lb2)


def _ffn_dhf(df, wd, sa, sb):
    n, d = df.shape
    f = sa.shape[1]
    tm = min(1024, n)

    def body(df_ref, w_ref, sa_ref, sb_ref, dg_ref, dp_ref):
        dhf = lax.dot_general(df_ref[...], w_ref[...], _NT, preferred_element_type=F32)
        dp_ref[...] = (dhf * sa_ref[...].astype(F32)).astype(BF16)
        dg_ref[...] = (dhf * sb_ref[...].astype(F32)).astype(BF16)

    tile = pl.BlockSpec((tm, FFN_TILE), lambda i, j: (i, j))
    sds = jax.ShapeDtypeStruct((n, f), BF16)
    return _pallas(
        body, name="ffn_dhf", grid=(n // tm, f // FFN_TILE),
        out_shape=(sds, sds),
        in_specs=[pl.BlockSpec((tm, d), lambda i, j: (i, 0)), pl.BlockSpec((FFN_TILE, d), lambda i, j: (j, 0)),
                  tile, tile],
        out_specs=(tile, tile),
        compiler_params=_params(_mb(48)),
    )(df, wd, sa, sb)


def _ffn_du2(dg, dp, wgt, wut, after):
    n, f = dg.shape
    d = wgt.shape[1]
    tm = min(1024, n)

    def body(dg_ref, dp_ref, wg_ref, wu_ref, after_ref, o_ref):
        part = (jnp.dot(dg_ref[...], wg_ref[...], preferred_element_type=F32)
                + jnp.dot(dp_ref[...], wu_ref[...], preferred_element_type=F32))

        @pl.when(pl.program_id(1) == 0)
        def _():
            o_ref[...] = part

        @pl.when(pl.program_id(1) > 0)
        def _():
            o_ref[...] += part

    tile = pl.BlockSpec((tm, FFN_TILE), lambda i, j: (i, j))
    wspec = pl.BlockSpec((FFN_TILE, d), lambda i, j: (j, 0))
    return _pallas(
        body, name="ffn_du2", grid=(n // tm, f // FFN_TILE),
        out_shape=jax.ShapeDtypeStruct((n, d), F32),
        in_specs=[tile, tile, wspec, wspec, _ANY],
        out_specs=pl.BlockSpec((tm, d), lambda i, j: (i, 0)),
        compiler_params=_params(_mb(48)),
    )(dg, dp, wgt, wut, after)


def _ln1_bwd(du2, dr2, xh1, rs1, a1, lg1, lb1, sc2, g1):
    n, d = du2.shape
    tm = 256

    def body(du_ref, dr2_ref, xh_ref, rs_ref, a_ref, lg_ref, lb_ref, sc_ref, g1_ref, dr1_ref, da_ref, acc_ref):
        @pl.when(pl.program_id(0) == 0)
        def _():
            acc_ref[...] = jnp.zeros_like(acc_ref)

        du = du_ref[...]
        xh = xh_ref[...]
        x1 = xh * lg_ref[...] + lb_ref[...]
        dx1 = ALPHA * dr2_ref[...] + du * (1.0 + sc_ref[...])
        dxg = dx1 * lg_ref[...]
        dr1 = rs_ref[...] * (dxg - _rowmean(dxg) - xh * _rowmean(dxg * xh))
        dr1_ref[...] = dr1
        da_ref[...] = (g1_ref[...] * dr1).astype(BF16)
        acc_ref[0:1, :] += _colsum(du * x1)
        acc_ref[1:2, :] += _colsum(du)
        acc_ref[2:3, :] += _colsum(dx1 * xh)
        acc_ref[3:4, :] += _colsum(dx1)
        acc_ref[4:5, :] += _colsum(dr1 * a_ref[...])

    row = lambda i: (i, 0)
    const2 = lambda i: (0, 0)
    vec = pl.BlockSpec((1, d), const2)
    big = pl.BlockSpec((tm, d), row)
    return _pallas(
        body, name="ln1_bwd", grid=(n // tm,),
        out_shape=(jax.ShapeDtypeStruct((n, d), F32), jax.ShapeDtypeStruct((n, d), BF16),
                   jax.ShapeDtypeStruct((8, d), F32)),
        in_specs=[big, big, big, pl.BlockSpec((tm, 1), row), big, vec, vec, vec, vec],
        out_specs=(big, big, pl.BlockSpec((8, d), const2)),
        compiler_params=_params(_mb(48)),
    )(du2, dr2, xh1, rs1, a1, lg1, lb1, sc2, g1)


def _dw_rows(a, b, nblk, bw, tm, after, name):
    m = a.shape[0]
    nn = b.shape[1]

    def body(a_ref, b_ref, after_ref, o_ref, acc_ref):
        part = lax.dot_general(a_ref[...].astype(BF16), b_ref[...], _TN, preferred_element_type=F32)
        i = pl.program_id(1)

        @pl.when(i == 0)
        def _():
            acc_ref[...] = part

        @pl.when(i > 0)
        def _():
            acc_ref[...] += part

        @pl.when(i == pl.num_programs(1) - 1)
        def _():
            o_ref[0] = acc_ref[...].astype(BF16)

    return _pallas(
        body, name=name, grid=(nblk, m // tm),
        out_shape=jax.ShapeDtypeStruct((nblk, bw, nn), BF16),
        in_specs=[pl.BlockSpec((tm, bw), lambda j, i: (i, j)), pl.BlockSpec((tm, nn), lambda j, i: (i, 0)), _ANY],
        out_specs=pl.BlockSpec((1, bw, nn), lambda j, i: (j, 0, 0)),
        scratch_shapes=[pltpu.VMEM((bw, nn), F32)],
        compiler_params=_params(_mb(56)),
    )(a, b, after)


def _outproj_bwd(da1, wout, after):
    n, d = da1.shape
    tm = 512

    def body(a_ref, w_ref, after_ref, o_ref):
        o_ref[...] = lax.dot_general(a_ref[...], w_ref[...], _NT, preferred_element_type=F32)

    return _pallas(
        body, name="outproj_bwd", grid=(n // tm,),
        out_shape=jax.ShapeDtypeStruct((n, d), F32),
        in_specs=[pl.BlockSpec((tm, d), lambda i: (i, 0)), pl.BlockSpec((d, d), lambda i: (0, 0)), _ANY],
        out_specs=pl.BlockSpec((tm, d), lambda i: (i, 0)),
        compiler_params=_params(_mb(48)),
    )(da1, wout, after)


def _qkv_bwd(dh, wint, x, ct, dr1, sc):
    na, wcols = dh.shape
    n, d = x.shape
    tm = CTX
    nlat = n // tm

    def body(dh_ref, w_ref, x_ref, ct_ref, dr_ref, sc_ref, gx_ref, acc_ref):
        i = pl.program_id(0)

        @pl.when(i == 0)
        def _():
            acc_ref[...] = jnp.zeros_like(acc_ref)

        du = jnp.dot(dh_ref[...], w_ref[...], preferred_element_type=F32)

        @pl.when(i < nlat)
        def _():
            gx_ref[...] = ALPHA * dr_ref[...] + du * (1.0 + sc_ref[0])
            acc_ref[0:1, :] += _colsum(du)
            acc_ref[1:2, :] += _colsum(du * x_ref[...])

        @pl.when(i == nlat)
        def _():
            acc_ref[2:3, :] += _colsum(du)
            acc_ref[3:4, :] += _colsum(du * ct_ref[...])

    lat = lambda i: (jnp.minimum(i, nlat - 1), 0)
    const2 = lambda i: (0, 0)
    return _pallas(
        body, name="qkv_bwd", grid=(nlat + 1,),
        out_shape=(jax.ShapeDtypeStruct((n, d), F32), jax.ShapeDtypeStruct((8, d), F32)),
        in_specs=[pl.BlockSpec((tm, wcols), lambda i: (i, 0)), pl.BlockSpec((wcols, d), const2),
                  pl.BlockSpec((tm, d), lat), pl.BlockSpec((tm, d), const2), pl.BlockSpec((tm, d), lat),
                  pl.BlockSpec((1, 1, d), lambda i: (0, 0, 0))],
        out_specs=(pl.BlockSpec((tm, d), lat), pl.BlockSpec((8, d), const2)),
        compiler_params=_params(_mb(56)),
    )(dh, wint, x, ct, dr1, sc)


def _adam_math(w, g, m, v):
    m2 = ADAM_B1 * m + (1.0 - ADAM_B1) * g
    v2 = ADAM_B2 * v + (1.0 - ADAM_B2) * (g * g)
    m_hat = m2 * (1.0 / (1.0 - ADAM_B1 ** ADAM_STEP))
    v_hat = v2 * (1.0 / (1.0 - ADAM_B2 ** ADAM_STEP))
    delta = -ADAM_LR * (m_hat / (jnp.sqrt(v_hat) + ADAM_EPS) + ADAM_WD * w)
    return delta, m2, v2


def _adamw(w, gsrc, m, v, name):
    r, c = w.shape
    parts = gsrc.ndim == 3
    cg = gsrc.shape[-1]
    tr = r
    while tr * c * 4 > _mb(1) and tr % 32 == 0:
        tr //= 2

    def body(w_ref, g_ref, m_ref, v_ref, go_ref, d_ref, mo_ref, vo_ref):
        if parts:
            g = g_ref[0].astype(F32)
            for s in range(1, NDEV):
                g = g + g_ref[s].astype(F32)
            g = g[:, :c]
        else:
            g = g_ref[...]
        delta, m2, v2 = _adam_math(w_ref[...], g, m_ref[...], v_ref[...])
        go_ref[...] = g
        d_ref[...] = delta
        mo_ref[...] = m2
        vo_ref[...] = v2

    tile = pl.BlockSpec((tr, c), lambda i: (i, 0))
    gspec = pl.BlockSpec((NDEV, tr, cg), lambda i: (0, i, 0)) if parts else tile
    sds = jax.ShapeDtypeStruct((r, c), F32)
    return _pallas(
        body, name=name, grid=(r // tr,),
        out_shape=(sds, sds, sds, sds),
        in_specs=[tile, gspec, tile, tile],
        out_specs=(tile, tile, tile, tile),
        compiler_params=_params(_mb(48)),
    )(w, gsrc, m, v)


def _adamw_t(w, gsrc_t, m, v, name):
    r, c = w.shape
    tr = 256

    def body(w_ref, g_ref, m_ref, v_ref, go_ref, d_ref, mo_ref, vo_ref):
        gt = g_ref[0].astype(F32)
        for s in range(1, NDEV):
            gt = gt + g_ref[s].astype(F32)
        g = gt.T
        delta, m2, v2 = _adam_math(w_ref[...], g, m_ref[...], v_ref[...])
        go_ref[...] = g
        d_ref[...] = delta
        mo_ref[...] = m2
        vo_ref[...] = v2

    tile = pl.BlockSpec((tr, c), lambda i: (i, 0))
    sds = jax.ShapeDtypeStruct((r, c), F32)
    return _pallas(
        body, name=name, grid=(r // tr,),
        out_shape=(sds, sds, sds, sds),
        in_specs=[tile, pl.BlockSpec((NDEV, c, tr), lambda i: (0, 0, i)), tile, tile],
        out_specs=(tile, tile, tile, tile),
        compiler_params=_params(_mb(48)),
    )(w, gsrc_t, m, v)


def _small_update(gath, dcc, cc, w_s, m_s, v_s):
    d = w_s.shape[1]

    def body(g_ref, dcc_ref, cc_ref, w_ref, m_ref, v_ref, go_ref, d_ref, mo_ref, vo_ref):
        s = g_ref[0]
        for b in range(1, NDEV):
            s = s + g_ref[b]
        dsl = dcc_ref[0, 8:9, :]
        for b in range(1, NDEV):
            dsl = dsl + dcc_ref[b, 8:9, :]
        cv = cc_ref[...]
        sg = _sigmoid(cv)
        go_ref[...] = jnp.zeros_like(go_ref)
        go_ref[0:1, :] = dsl * (sg * (1.0 + cv * (1.0 - sg)))
        go_ref[1:3, :] = s[0:2] + s[6:8]
        go_ref[3:7, :] = s[2:6]
        go_ref[7:12, :] = s[8:13]
        delta, m2, v2 = _adam_math(w_ref[...], go_ref[...], m_ref[...], v_ref[...])
        d_ref[...] = delta
        mo_ref[...] = m2
        vo_ref[...] = v2

    full = pl.BlockSpec((16, d), lambda: (0, 0))
    g3 = pl.BlockSpec((NDEV, 16, d), lambda: (0, 0, 0))
    sds = jax.ShapeDtypeStruct((16, d), F32)
    return _pallas(
        body, name="small_update",
        out_shape=(sds, sds, sds, sds),
        in_specs=[g3, g3, pl.BlockSpec((1, d), lambda: (0, 0)), full, full, full],
        out_specs=(full, full, full, full),
        compiler_params=_params(_mb(24)),
    )(gath, dcc, cc, w_s, m_s, v_s)


def _rope_tables(n):
    rows = n // GRID_W
    row_ids = jnp.repeat(jnp.arange(rows, dtype=F32), GRID_W)
    col_ids = jnp.tile(jnp.arange(GRID_W, dtype=F32), rows)
    axis_dim = HEAD // 2
    inv_freq = jnp.power(ROPE_THETA, -jnp.arange(0, axis_dim, 2, dtype=F32) / axis_dim)
    ang_r = row_ids[:, None] * inv_freq
    ang_c = col_ids[:, None] * inv_freq
    ang = jnp.concatenate([ang_r, ang_r, ang_c, ang_c], axis=-1)
    cos, sin = jnp.cos(ang), jnp.sin(ang)
    first = (jnp.arange(HEAD) % (HEAD // 2)) < HEAD // 4
    sa = jnp.where(first, -sin, 0.0)
    sb = jnp.where(first, 0.0, sin)
    ones = jnp.ones((CTX, HEAD), F32)
    zeros = jnp.zeros((CTX, HEAD), F32)
    return (jnp.concatenate([cos, ones], 0), jnp.concatenate([sa, zeros], 0), jnp.concatenate([sb, zeros], 0))


def _pad_cols(a, width):
    return jnp.pad(a, ((0, 0), (0, width - a.shape[1])))


def _pad_rows(a, rows):
    return jnp.pad(a, ((0, rows - a.shape[0]), (0, 0)))


def _pack_small(c_ctx, b_ada, ln1_g, ln1_b, ln2_g, ln2_b, qg, kg, sink, d):
    misc = _pad_cols(jnp.concatenate([qg, kg, sink], axis=1), d)
    rows = jnp.concatenate([c_ctx.reshape(1, d), b_ada.reshape(6, d), ln1_g, ln1_b, ln2_g, ln2_b, misc], axis=0)
    return _pad_rows(rows, 16)


def _unpack_small(p, d):
    return dict(c_ctx=p[0], b_ada=p[1:7].reshape(1, 6 * d), ln1_g=p[7:8], ln1_b=p[8:9], ln2_g=p[9:10], ln2_b=p[10:11],
                q_norm_g=p[11:12, 0:HEAD], k_norm_g=p[11:12, HEAD:2 * HEAD], sink_logit=p[11:12, 2 * HEAD:2 * HEAD + 8])


def kernel(x, c, ctx, c_ctx, w_ada, b_ada, w_in, q_norm_g, k_norm_g, sink_logit, w_out, ln1_g, ln1_b, w_gate, w_up, w_down, ln2_g, ln2_b, loss_target, m_c_ctx, m_w_ada, m_b_ada, m_w_in, m_q_norm_g, m_k_norm_g, m_sink_logit, m_w_out, m_ln1_g, m_ln1_b, m_w_gate, m_w_up, m_w_down, m_ln2_g, m_ln2_b, v_c_ctx, v_w_ada, v_b_ada, v_w_in, v_q_norm_g, v_k_norm_g, v_sink_logit, v_w_out, v_ln1_g, v_ln1_b, v_w_gate, v_w_up, v_w_down, v_ln2_g, v_ln2_b):
    xs, cts, tgt = x[0], ctx[0], loss_target[0]
    n, d = xs.shape
    assert cts.shape == (CTX, d) and w_in.shape[2] == IN_SHARD and w_gate.shape[2] == FFN_SHARD
    me = 4 * lax.axis_index("x") + 2 * lax.axis_index("y") + lax.axis_index("c")
    e_sh = w_ada.shape[2]

    c_g = _exchange(_pad_rows(c, 8), False, "gather_c")
    c_all = jnp.concatenate([c_g[:, 0, :], _pad_rows(c_ctx.reshape(1, d), 8)], axis=0)
    bias_sh = lax.dynamic_slice(b_ada, (0, me * e_sh), (1, e_sh))
    mods_g = _exchange(_ada_fwd(c_all, w_ada[0], bias_sh), False, "gather_mods")
    mods = jnp.transpose(mods_g, (1, 0, 2)).reshape(16, NDEV * e_sh)
    mine = lax.dynamic_slice(mods, (me, 0), (1, 6 * d))
    sh1, sc1, g1, sh2, sc2, g2 = [mine[:, k * d:(k + 1) * d] for k in range(6)]
    csh1, csc1 = mods[8:9, 0:d], mods[8:9, d:2 * d]
    sc_pair = jnp.stack([sc1, csc1])
    sh_pair = jnp.stack([sh1, csh1])

    h_win, tok = _exchange_start(w_in[0].T.astype(BF16), "chip", mods, "gather_w_in_start")
    h_wout, tok = _exchange_start(w_out[0].astype(BF16), "chip", tok, "gather_w_out_start")
    h_wg, tok = _exchange_start(w_gate[0].T.astype(BF16), "chip", tok, "gather_w_gate_start")
    h_wu, tok = _exchange_start(w_up[0].T.astype(BF16), "chip", tok, "gather_w_up_start")
    h_wd, tok = _exchange_start(w_down[0].astype(BF16), "chip", tok, "gather_w_down_start")

    cos, sa, sb = _rope_tables(n)
    f_win, tok = _forward_start(_exchange_wait(h_win, "chip", tok, "gather_w_in_wait"), tok, "forward_w_in_start")
    win_g = _forward_wait(f_win, tok, "forward_w_in_wait").reshape(NDEV * IN_SHARD, d)
    u_all, h_all, t_all, kvt = _qkv_fwd(xs, cts, sc_pair, sh_pair, win_g, q_norm_g, k_norm_g, cos, sa, sb)
    f_wout, tok = _forward_start(_exchange_wait(h_wout, "chip", t_all, "gather_w_out_wait"), t_all, "forward_w_out_start")
    o_a, lse_a = _attn_window_fwd(t_all, sink_logit, tok)
    o, lse_b = _attn_global_fwd(t_all, kvt, o_a)
    f_wg, tok = _forward_start(_exchange_wait(h_wg, "chip", o, "gather_w_gate_wait"), o, "forward_w_gate_start")
    f_wu, tok = _forward_start(_exchange_wait(h_wu, "chip", tok, "gather_w_up_wait"), tok, "forward_w_up_start")
    wout_g = _forward_wait(f_wout, tok, "forward_w_out_wait").reshape(d, d)
    a1, xh1, rs1, u2 = _outproj_ln1(o, wout_g, xs, g1, ln1_g, ln1_b, sc2, sh2, tok)
    f_wd, tok = _forward_start(_exchange_wait(h_wd, "chip", rs1, "gather_w_down_wait"), rs1, "forward_w_down_start")
    ffn_w = (NDEV * FFN_SHARD, d)
    wg_g = _forward_wait(f_wg, tok, "forward_w_gate_wait").reshape(ffn_w)
    wu_g = _forward_wait(f_wu, tok, "forward_w_up_wait").reshape(ffn_w)
    sa_f, sb_f, hf = _ffn_up(u2, wg_g, wu_g, tok)
    wd_g = _forward_wait(f_wd, hf, "forward_w_down_wait").reshape(ffn_w)
    ffn = _ffn_down(hf, wd_g)
    dr2, df, loss_p, acc2 = _ln2_loss(xh1, ffn, tgt, ln1_g, ln1_b, g2, ln2_g, ln2_b)
    loss = lax.psum(loss_p[0, 0], ("x", "y", "c"))

    tk = min(n, 2048)
    parts = (NDEV, FFN_SHARD, d)
    dgm, dpm = _ffn_dhf(df, wd_g, sa_f, sb_f)
    dwd_p = _dw_rows(hf, df, NDEV // 2, FFN_PAIR, min(n, 1024), loss_p, "dw_down").reshape(parts)
    h_dwd, tok = _exchange_start(dwd_p, "scatter", loss.reshape(1, 1), "scatter_dw_down_start")
    dwg_p = _dw_rows(dgm, u2, NDEV // 2, FFN_PAIR, min(n, 1024), tok, "dw_gate").reshape(parts)
    h_dwg, tok = _exchange_start(dwg_p, "scatter", tok, "scatter_dw_gate_start")
    dwu_p = _dw_rows(dpm, u2, NDEV // 2, FFN_PAIR, min(n, 1024), tok, "dw_up").reshape(parts)
    h_dwu, tok = _exchange_start(dwu_p, "scatter", tok, "scatter_dw_up_start")
    du2 = _ffn_du2(dgm, dpm, wg_g, wu_g, tok)
    dr1, da1, acc1 = _ln1_bwd(du2, dr2, xh1, rs1, a1, ln1_g, ln1_b, sc2, g1)
    dwo_p = _dw_rows(o, da1, NDEV, 2 * HEAD, tk, loss_p, "dw_out")
    h_dwo, tok = _exchange_start(dwo_p, "scatter", loss_p, "scatter_dw_out_start")
    do = _outproj_bwd(da1, wout_g, tok)
    dqa, dka, dva, dsink = _attn_window_bwd(t_all, o, do, lse_a, sink_logit)
    dqb, dkb, dvb = _attn_global_bwd(t_all, kvt, o, do, lse_b)
    dh_all, dnorm = _qkv_bwd_prep(dqa, dka, dva, dqb, dkb, dvb, h_all, q_norm_g, k_norm_g, cos, sa, sb)
    grad_x, acc0 = _qkv_bwd(dh_all, win_g, xs, cts, dr1, sc_pair)

    misc = _pad_cols(jnp.concatenate([dnorm[0:1], dnorm[1:2], dsink[:, 0:4, 0].reshape(1, 8)], axis=1), d)
    part = jnp.concatenate([
        acc0[0:2], acc1[4:5], acc1[1:2], acc1[0:1], acc2[2:3],
        acc0[2:4],
        acc1[2:4], acc2[0:2],
        misc, jnp.zeros((3, d), F32)], axis=0)
    gath = _exchange(part, False, "gather_small")
    dm_batch = gath[:, 0:6, :].reshape(NDEV, 6 * d)
    dm_ctx = _pad_cols(gath[:, 6:8, :].reshape(NDEV, 2 * d), 6 * d)
    dm16 = lax.dynamic_slice(jnp.concatenate([dm_batch, dm_ctx], axis=0), (0, me * e_sh), (16, e_sh))
    dw_ada, drow = _ada_bwd(dm16, c_all, w_ada[0])
    dcc = _exchange(drow, False, "gather_dcc")
    dwi_p = _dw_rows(dh_all, u_all, NDEV, IN_SHARD, (n + CTX) // 2, dcc, "dw_in")
    h_dwi, tok = _exchange_start(dwi_p, "scatter", dcc, "scatter_dw_in_start")

    w_s = _pack_small(c_ctx, b_ada, ln1_g, ln1_b, ln2_g, ln2_b, q_norm_g, k_norm_g, sink_logit, d)
    m_s = _pack_small(m_c_ctx, m_b_ada, m_ln1_g, m_ln1_b, m_ln2_g, m_ln2_b, m_q_norm_g, m_k_norm_g, m_sink_logit, d)
    v_s = _pack_small(v_c_ctx, v_b_ada, v_ln1_g, v_ln1_b, v_ln2_g, v_ln2_b, v_q_norm_g, v_k_norm_g, v_sink_logit, d)
    small = [_unpack_small(p, d) for p in _small_update(gath, dcc, c_ctx.reshape(1, d), w_s, m_s, v_s)]

    big = {}
    big["w_ada"] = _adamw(w_ada[0], dw_ada, m_w_ada[0], v_w_ada[0], "adamw_w_ada")
    late = tok
    big["w_down"] = _adamw(w_down[0], _exchange_wait(h_dwd, "scatter", late, "scatter_dw_down_wait"), m_w_down[0],
                           v_w_down[0], "adamw_w_down")
    for nm, wt, mt, vt, hd in (("w_gate", w_gate, m_w_gate, v_w_gate, h_dwg), ("w_up", w_up, m_w_up, v_w_up, h_dwu)):
        big[nm] = _adamw_t(wt[0], _exchange_wait(hd, "scatter", late, "scatter_d" + nm + "_wait"), mt[0], vt[0],
                           "adamw_" + nm)
    big["w_out"] = _adamw(w_out[0], _exchange_wait(h_dwo, "scatter", late, "scatter_dw_out_wait"), m_w_out[0], v_w_out[0],
                          "adamw_w_out")
    big["w_in"] = _adamw_t(w_in[0], _exchange_wait(h_dwi, "scatter", big["w_out"][1], "scatter_dw_in_wait"), m_w_in[0],
                           v_w_in[0], "adamw_w_in")

    names = ["c_ctx", "w_ada", "b_ada", "w_in", "q_norm_g", "k_norm_g", "sink_logit", "w_out", "ln1_g", "ln1_b",
             "w_gate", "w_up", "w_down", "ln2_g", "ln2_b"]
    outs = [loss, grad_x[None]]
    for k in range(4):
        for nm in names:
            outs.append(big[nm][k][None] if nm in big else small[k][nm])
    return tuple(outs)
```

```python
import functools

import jax
import jax.numpy as jnp
from jax import lax
from jax.experimental import pallas as pl
from jax.experimental.pallas import tpu as pltpu

F32 = jnp.float32
BF16 = jnp.bfloat16

NDEV = 8
HEAD = 128
CTX = 256
GRID_W = 64
WINDOW = 128
ROPE_THETA = 10000.0
EPS = 1e-6
SCALE = HEAD ** -0.5
LOG2E = 1.4426950408889634
QK_LOG2 = SCALE * LOG2E
ALPHA = 2.0 ** 0.25
FFN_SHARD = 704
FFN_TILE = 512
FFN_PAIR = 2 * FFN_SHARD
IN_SHARD = 384
NEG = -1e30

ADAM_LR = 0.001
ADAM_B1 = 0.9
ADAM_B2 = 0.999
ADAM_EPS = 1e-08
ADAM_WD = 0.01
ADAM_STEP = 10

VMEM_CAP = 56 * 1024 * 1024

_KINDS = ["rope"] * 10 + ["none"] * 2 + ["qnorm"] * 8 + ["knorm"] * 2 + ["none"] * 2

_NT = (((1,), (1,)), ((), ()))
_TN = (((0,), (0,)), ((), ()))


def _pallas(body, **kw):
    return pl.pallas_call(body, **kw)


def _params(vmem_bytes):
    return pltpu.CompilerParams(vmem_limit_bytes=int(min(VMEM_CAP, vmem_bytes)))


def _mb(n):
    return int(n * 1024 * 1024)


def _sigmoid(x):
    return 1.0 / (1.0 + jnp.exp(-x))


def _colsum(a):
    return jnp.sum(a, axis=0, keepdims=True)


def _rowmean(a):
    return jnp.mean(a, axis=-1, keepdims=True)


def _exchange(src, scatter, name, after=None):
    blk = src.shape[1:] if scatter else src.shape
    after = src if after is None else after

    def body(src_ref, after_ref, out_ref, send_sems, recv_sems, local_sem):
        x, y, c = lax.axis_index("x"), lax.axis_index("y"), lax.axis_index("c")
        me = 4 * x + 2 * y + c
        copies = []
        for t in range(1, NDEV):
            px = 1 - x if (t >> 2) & 1 else x
            py = 1 - y if (t >> 1) & 1 else y
            pc = 1 - c if t & 1 else c
            peer = 4 * px + 2 * py + pc
            cp = pltpu.make_async_remote_copy(
                src_ref=src_ref.at[peer] if scatter else src_ref,
                dst_ref=out_ref.at[me],
                send_sem=send_sems.at[t - 1],
                recv_sem=recv_sems.at[t - 1],
                device_id=(px, py, pc),
                device_id_type=pl.DeviceIdType.MESH,
            )
            cp.start()
            copies.append(cp)
        own = pltpu.make_async_copy(src_ref.at[me] if scatter else src_ref, out_ref.at[me], local_sem)
        own.start()
        for cp in copies:
            cp.wait()
        own.wait()

    return _pallas(
        body, name=name,
        out_shape=jax.ShapeDtypeStruct((NDEV,) + tuple(blk), src.dtype),
        in_specs=[pl.BlockSpec(memory_space=pl.ANY), pl.BlockSpec(memory_space=pl.ANY)],
        out_specs=pl.BlockSpec(memory_space=pl.ANY),
        scratch_shapes=[pltpu.SemaphoreType.DMA((NDEV - 1,)), pltpu.SemaphoreType.DMA((NDEV - 1,)),
                        pltpu.SemaphoreType.DMA(())],
    )(src, after)


_HBM = pl.BlockSpec(memory_space=pltpu.HBM)
_SEM = pl.BlockSpec(memory_space=pltpu.SEMAPHORE)
_ANY = pl.BlockSpec(memory_space=pl.ANY)
_EFFECT = pltpu.SideEffectType.DATAFLOW_SIDE_EFFECTING


def _exchange_copies(src_ref, land_ref, send_sems, recv_sems, mode):
    x, y, c = lax.axis_index("x"), lax.axis_index("y"), lax.axis_index("c")
    me = 4 * x + 2 * y + c
    scatter = mode == "scatter"
    copies = []
    for t in ((1, 2, 4, 6) if mode == "chip" else range(1, NDEV)):
        px = 1 - x if (t >> 2) & 1 else x
        py = 1 - y if (t >> 1) & 1 else y
        pc = 1 - c if t & 1 else c
        peer = 4 * px + 2 * py + pc
        copies.append(pltpu.make_async_remote_copy(
            src_ref=src_ref.at[peer] if scatter else src_ref,
            dst_ref=land_ref.at[me],
            send_sem=send_sems.at[t - 1],
            recv_sem=recv_sems.at[t - 1],
            device_id=(px, py, pc),
            device_id_type=pl.DeviceIdType.MESH,
        ))
    own = pltpu.make_async_copy(src_ref.at[me] if scatter else src_ref, land_ref.at[me], send_sems.at[NDEV - 1])
    return copies, own


def _forward_copies(land_ref, send_sems, recv_sems):
    x, y, c = lax.axis_index("x"), lax.axis_index("y"), lax.axis_index("c")
    copies = []
    for k, t in enumerate((2, 4, 6)):
        px = 1 - x if (t >> 2) & 1 else x
        py = 1 - y if (t >> 1) & 1 else y
        mine, theirs = 4 * px + 2 * py + c, 4 * px + 2 * py + (1 - c)
        send = pltpu.make_async_remote_copy(
            src_ref=land_ref.at[mine], dst_ref=land_ref.at[mine], send_sem=send_sems.at[k], recv_sem=recv_sems.at[k],
            device_id=(x, y, 1 - c), device_id_type=pl.DeviceIdType.MESH)
        recv = pltpu.make_async_remote_copy(
            src_ref=land_ref.at[theirs], dst_ref=land_ref.at[theirs], send_sem=send_sems.at[k], recv_sem=recv_sems.at[k],
            device_id=(x, y, 1 - c), device_id_type=pl.DeviceIdType.MESH)
        copies.append((send, recv))
    return copies


def _forward_start(land, after, name):
    def body(land_ref, after_ref, send_sems, recv_sems, land_thru, token):
        for send, _ in _forward_copies(land_ref, send_sems, recv_sems):
            send.start()
        token[...] = jnp.zeros_like(token)

    res = _pallas(
        body, name=name,
        out_shape=(pltpu.SemaphoreType.DMA((3,)), pltpu.SemaphoreType.DMA((3,)), pltpu.HBM(land.shape, land.dtype),
                   jax.ShapeDtypeStruct((8, HEAD), F32)),
        in_specs=(_HBM, _ANY), out_specs=(_SEM, _SEM, _HBM, pl.BlockSpec(memory_space=pltpu.VMEM)),
        input_output_aliases={0: 2},
        compiler_params=pltpu.CompilerParams(has_side_effects=_EFFECT),
    )(land, after)
    return res[:3], res[3]


def _forward_wait(handle, after, name):
    send_sems, recv_sems, land_thru = handle

    def body(land_ref, send_sems, recv_sems, after_ref, got_ref):
        for send, recv in _forward_copies(land_ref, send_sems, recv_sems):
            send.wait_send()
            recv.wait_recv()

    return _pallas(
        body, name=name,
        out_shape=pltpu.HBM(land_thru.shape, land_thru.dtype),
        in_specs=(_HBM, _SEM, _SEM, _ANY), out_specs=_HBM,
        input_output_aliases={0: 0},
        compiler_params=pltpu.CompilerParams(has_side_effects=_EFFECT),
    )(land_thru, send_sems, recv_sems, after)


def _exchange_start(src, mode, after, name):
    blk = src.shape[1:] if mode == "scatter" else src.shape
    land = lax.empty((NDEV,) + tuple(blk), src.dtype)

    def body(src_ref, land_ref, after_ref, send_sems, recv_sems, src_thru, land_thru, token):
        copies, own = _exchange_copies(src_ref, land_ref, send_sems, recv_sems, mode)
        for cp in copies:
            cp.start()
        own.start()
        token[...] = jnp.zeros_like(token)

    res = _pallas(
        body, name=name,
        out_shape=(pltpu.SemaphoreType.DMA((NDEV,)), pltpu.SemaphoreType.DMA((NDEV,)),
                   pltpu.HBM(src.shape, src.dtype), pltpu.HBM(land.shape, land.dtype),
                   jax.ShapeDtypeStruct((8, HEAD), F32)),
        in_specs=(_HBM, _HBM, _ANY), out_specs=(_SEM, _SEM, _HBM, _HBM, pl.BlockSpec(memory_space=pltpu.VMEM)),
        input_output_aliases={0: 2, 1: 3},
        compiler_params=pltpu.CompilerParams(has_side_effects=_EFFECT),
    )(pltpu.with_memory_space_constraint(src, pltpu.HBM), pltpu.with_memory_space_constraint(land, pltpu.HBM), after)
    return res[:4], res[4]


def _exchange_wait(handle, mode, after, name):
    send_sems, recv_sems, src_thru, land_thru = handle

    def body(src_ref, land_ref, send_sems, recv_sems, after_ref, src_dead, got_ref):
        copies, own = _exchange_copies(src_ref, land_ref, send_sems, recv_sems, mode)
        for cp in copies:
            cp.wait_send()
            cp.wait_recv()
        own.wait()

    return _pallas(
        body, name=name,
        out_shape=(pltpu.HBM(src_thru.shape, src_thru.dtype), pltpu.HBM(land_thru.shape, land_thru.dtype)),
        in_specs=(_HBM, _HBM, _SEM, _SEM, _ANY), out_specs=(_HBM, _HBM),
        input_output_aliases={0: 0, 1: 1},
        compiler_params=pltpu.CompilerParams(has_side_effects=_EFFECT),
    )(src_thru, land_thru, send_sems, recv_sems, after)[1]


def _ada_fwd(c_all, w, bias):
    r, d = c_all.shape
    e = w.shape[1]
    tn = 512

    def body(c_ref, w_ref, b_ref, o_ref):
        cv = c_ref[...]
        s = (cv * _sigmoid(cv)).astype(BF16)
        o_ref[...] = jnp.dot(s, w_ref[...].astype(BF16), preferred_element_type=F32) + b_ref[...]

    return _pallas(
        body, name="ada_fwd", grid=(e // tn,),
        out_shape=jax.ShapeDtypeStruct((r, e), F32),
        in_specs=[pl.BlockSpec((r, d), lambda j: (0, 0)), pl.BlockSpec((d, tn), lambda j: (0, j)),
                  pl.BlockSpec((1, tn), lambda j: (0, j))],
        out_specs=pl.BlockSpec((r, tn), lambda j: (0, j)),
        compiler_params=_params(_mb(24)),
    )(c_all, w, bias)


def _ada_bwd(dm16, c_all, w):
    d, e = w.shape
    tn = 512

    def body(dm_ref, c_ref, w_ref, dw_ref, dr_ref):
        j = pl.program_id(0)
        dm = dm_ref[...]
        rid = lax.broadcasted_iota(jnp.int32, dm.shape, 0)
        ctx_sum = jnp.sum(jnp.where(rid >= 8, dm, 0.0), axis=0, keepdims=True)
        rows = jnp.where(rid < 8, dm, jnp.where(rid == 8, jnp.broadcast_to(ctx_sum, dm.shape), 0.0)).astype(BF16)
        cv = c_ref[...]
        s = (cv * _sigmoid(cv)).astype(BF16)
        dw_ref[...] = lax.dot_general(s, rows, _TN, preferred_element_type=F32)
        part = lax.dot_general(rows, w_ref[...].astype(BF16), _NT, preferred_element_type=F32)

        @pl.when(j == 0)
        def _():
            dr_ref[...] = part

        @pl.when(j > 0)
        def _():
            dr_ref[...] += part

    return _pallas(
        body, name="ada_bwd", grid=(e // tn,),
        out_shape=(jax.ShapeDtypeStruct((d, e), F32), jax.ShapeDtypeStruct((16, d), F32)),
        in_specs=[pl.BlockSpec((16, tn), lambda j: (0, j)), pl.BlockSpec((16, d), lambda j: (0, 0)),
                  pl.BlockSpec((d, tn), lambda j: (0, j))],
        out_specs=(pl.BlockSpec((d, tn), lambda j: (0, j)), pl.BlockSpec((16, d), lambda j: (0, 0))),
        compiler_params=_params(_mb(32)),
    )(dm16, c_all, w)


def _rope(v, cos, sa, sb):
    return v * cos + (pltpu.roll(v, 96, 1) * sa + pltpu.roll(v, 32, 1) * sb)


def _rope_t(dt, cos, sa, sb):
    return dt * cos + (pltpu.roll(dt * sa, 32, 1) + pltpu.roll(dt * sb, 96, 1))


def _qkv_fwd(x, ct, sc, sh, wint, qg, kg, cos, sa, sb):
    n, d = x.shape
    tm = CTX
    nlat = n // tm
    na = n + CTX
    wcols = wint.shape[0]

    def body(x_ref, ct_ref, sc_ref, sh_ref, w_ref, qg_ref, kg_ref, cos_ref, sa_ref, sb_ref, u_ref, h_ref, t_ref, kt_ref):
        i = pl.program_id(0)
        xin = jnp.where(i == nlat, ct_ref[...], x_ref[...])
        u = (xin * (1.0 + sc_ref[0]) + sh_ref[0]).astype(BF16)
        u_ref[...] = u
        cos, sa, sb = cos_ref[...], sa_ref[...], sb_ref[...]
        h = lax.dot_general(u, w_ref[...], _NT, preferred_element_type=F32)
        h_ref[...] = h
        for hd in range(24):
            v = h[:, hd * HEAD:(hd + 1) * HEAD]
            kind = _KINDS[hd]
            if kind == "qnorm":
                v = v * lax.rsqrt(_rowmean(v * v) + EPS) * qg_ref[...]
            elif kind == "knorm":
                v = v * lax.rsqrt(_rowmean(v * v) + EPS) * kg_ref[...]
            if kind != "none":
                v = _rope(v, cos, sa, sb)
            t_ref[:, hd * HEAD:(hd + 1) * HEAD] = v.astype(BF16)
            if kind == "knorm":
                kt_ref[(hd - 20) * HEAD:(hd - 19) * HEAD, :] = v.T.astype(BF16)

    lat = lambda i: (jnp.minimum(i, nlat - 1), 0)
    row = lambda i: (i, 0)
    const2 = lambda i: (0, 0)
    return _pallas(
        body, name="qkv_fwd", grid=(nlat + 1,),
        out_shape=(jax.ShapeDtypeStruct((na, d), BF16), jax.ShapeDtypeStruct((na, wcols), F32),
                   jax.ShapeDtypeStruct((na, wcols), BF16), jax.ShapeDtypeStruct((2 * HEAD, na), BF16)),
        in_specs=[pl.BlockSpec((tm, d), lat), pl.BlockSpec((tm, d), const2),
                  pl.BlockSpec((1, 1, d), lambda i: (i // nlat, 0, 0)),
                  pl.BlockSpec((1, 1, d), lambda i: (i // nlat, 0, 0)),
                  pl.BlockSpec((wcols, d), const2),
                  pl.BlockSpec((1, HEAD), const2), pl.BlockSpec((1, HEAD), const2),
                  pl.BlockSpec((tm, HEAD), row), pl.BlockSpec((tm, HEAD), row), pl.BlockSpec((tm, HEAD), row)],
        out_specs=(pl.BlockSpec((tm, d), row), pl.BlockSpec((tm, wcols), row), pl.BlockSpec((tm, wcols), row),
                   pl.BlockSpec((2 * HEAD, tm), lambda i: (0, i))),
        compiler_params=_params(_mb(56)),
    )(x, ct, sc, sh, wint, qg, kg, cos, sa, sb)


def _qkv_bwd_prep(dqa, dka, dva, dqb, dkb, dvb, h_all, qg, kg, cos, sa, sb):
    na, wcols = h_all.shape
    n = na - CTX
    tm = CTX
    nlat = n // tm

    def body(dqa_ref, dka_ref, dva_ref, dqb_ref, dkb_ref, dvb_ref, h_ref, qg_ref, kg_ref, cos_ref, sa_ref, sb_ref,
             dh_ref, dg_ref):
        i = pl.program_id(0)

        @pl.when(i == 0)
        def _():
            dg_ref[...] = jnp.zeros_like(dg_ref)

        cos, sa, sb = cos_ref[...], sa_ref[...], sb_ref[...]
        is_lat = i < nlat
        for hd in range(24):
            kind = _KINDS[hd]
            if hd < 8:
                dt = jnp.where(is_lat, dqa_ref[:, hd * HEAD:(hd + 1) * HEAD], 0.0)
            elif hd < 10:
                dt = dka_ref[:, (hd - 8) * HEAD:(hd - 7) * HEAD]
            elif hd < 12:
                dt = dva_ref[:, (hd - 10) * HEAD:(hd - 9) * HEAD]
            elif hd < 20:
                dt = jnp.where(is_lat, dqb_ref[:, (hd - 12) * HEAD:(hd - 11) * HEAD], 0.0)
            elif hd < 22:
                dt = dkb_ref[:, (hd - 20) * HEAD:(hd - 19) * HEAD]
            else:
                dt = dvb_ref[:, (hd - 22) * HEAD:(hd - 21) * HEAD]
            if kind != "none":
                dt = _rope_t(dt, cos, sa, sb)
            if kind in ("qnorm", "knorm"):
                g_ref = qg_ref if kind == "qnorm" else kg_ref
                r0 = 0 if kind == "qnorm" else 1
                xv = h_ref[:, hd * HEAD:(hd + 1) * HEAD]
                xn = xv * lax.rsqrt(_rowmean(xv * xv) + EPS)
                dg_ref[r0:r0 + 1, :] += _colsum(dt * xn)
                dxn = dt * g_ref[...]
                dt = lax.rsqrt(_rowmean(xv * xv) + EPS) * (dxn - xn * _rowmean(dxn * xn))
            dh_ref[:, hd * HEAD:(hd + 1) * HEAD] = dt.astype(BF16)

    lat = lambda i: (jnp.minimum(i, nlat - 1), 0)
    row = lambda i: (i, 0)
    const2 = lambda i: (0, 0)
    return _pallas(
        body, name="qkv_bwd_prep", grid=(nlat + 1,),
        out_shape=(jax.ShapeDtypeStruct((na, wcols), BF16), jax.ShapeDtypeStruct((8, HEAD), F32)),
        in_specs=[pl.BlockSpec((tm, 8 * HEAD), lat), pl.BlockSpec((tm, 2 * HEAD), row), pl.BlockSpec((tm, 2 * HEAD), row),
                  pl.BlockSpec((tm, 8 * HEAD), lat), pl.BlockSpec((tm, 2 * HEAD), row), pl.BlockSpec((tm, 2 * HEAD), row),
                  pl.BlockSpec((tm, wcols), row),
                  pl.BlockSpec((1, HEAD), const2), pl.BlockSpec((1, HEAD), const2),
                  pl.BlockSpec((tm, HEAD), row), pl.BlockSpec((tm, HEAD), row), pl.BlockSpec((tm, HEAD), row)],
        out_specs=(pl.BlockSpec((tm, wcols), row), pl.BlockSpec((8, HEAD), const2)),
        compiler_params=_params(_mb(40)),
    )(dqa, dka, dva, dqb, dkb, dvb, h_all, qg, kg, cos, sa, sb)


def _window_keys(k_ref, v_ref, n, na):
    i = pl.program_id(1)
    tq = WINDOW
    start = pl.multiple_of(jnp.clip((i - 1) * tq, 0, n - 3 * tq), tq)
    kk = jnp.concatenate([k_ref[pl.ds(start, 3 * tq), :], k_ref[n:na, :]], axis=0)
    vv = jnp.concatenate([v_ref[pl.ds(start, 3 * tq), :], v_ref[n:na, :]], axis=0)
    nk = 3 * tq + CTX
    col = lax.broadcasted_iota(jnp.int32, (4 * tq, nk), 1)
    rowi = lax.broadcasted_iota(jnp.int32, (4 * tq, nk), 0)
    qpos = i * tq + (rowi & (tq - 1))
    valid = (jnp.abs(qpos - (start + col)) <= WINDOW) | (col >= 3 * tq)
    return kk, vv, valid, start


def _stack_heads(ref, width=HEAD):
    return jnp.concatenate([ref[:, g * HEAD:g * HEAD + width] for g in range(4)], axis=0)


def _sink_column(sink_ref, kv, tq):
    grp = lax.broadcasted_iota(jnp.int32, (4 * tq, 1), 0) // tq
    col = jnp.zeros((4 * tq, 1), F32)
    for g in range(4):
        col = jnp.where(grp == g, sink_ref[0, 4 * kv + g] * LOG2E, col)
    return col


def _attn_window_fwd(t_all, sink, after):
    na = t_all.shape[0]
    n = na - CTX
    tq = WINDOW

    def body(sink_ref, q_ref, k_ref, v_ref, after_ref, o_ref, lse_ref):
        kv = pl.program_id(0)
        kk, vv, valid, _ = _window_keys(k_ref, v_ref, n, na)
        t = lax.dot_general(_stack_heads(q_ref), kk, _NT, preferred_element_type=F32) * QK_LOG2
        t = jnp.where(valid, t, NEG)
        sk = _sink_column(sink_ref, kv, tq)
        m = jnp.maximum(jnp.max(t, axis=-1, keepdims=True), sk)
        p = jnp.exp2(t - m)
        l = jnp.sum(p, axis=-1, keepdims=True) + jnp.exp2(sk - m)
        o = jnp.dot(p.astype(BF16), vv, preferred_element_type=F32) * (1.0 / l)
        lse = m + jnp.log2(l)
        for g in range(4):
            o_ref[:, g * HEAD:(g + 1) * HEAD] = o[g * tq:(g + 1) * tq]
            lse_ref[:, g * HEAD:(g + 1) * HEAD] = jnp.broadcast_to(lse[g * tq:(g + 1) * tq], (tq, HEAD))

    blk = pl.BlockSpec((tq, 4 * HEAD), lambda kv, i: (i, kv))
    return _pallas(
        body, name="attn_window_fwd", grid=(2, n // tq),
        out_shape=(jax.ShapeDtypeStruct((n, 16 * HEAD), F32), jax.ShapeDtypeStruct((n, 8 * HEAD), F32)),
        in_specs=[pl.BlockSpec(memory_space=pltpu.SMEM), blk,
                  pl.BlockSpec((na, HEAD), lambda kv, i: (0, 8 + kv)),
                  pl.BlockSpec((na, HEAD), lambda kv, i: (0, 10 + kv)), _ANY],
        out_specs=(blk, blk),
        compiler_params=_params(_mb(32)),
    )(sink, t_all, t_all, t_all, after)


def _attn_global_fwd(t_all, o_part):
    na = t_all.shape[0]
    n = na - CTX
    tq = 256

    def body(q_ref, k_ref, v_ref, o_in_ref, o_ref, lse_ref):
        kk, vv = k_ref[...], v_ref[...]
        for g in range(4):
            q = q_ref[:, g * HEAD:(g + 1) * HEAD]
            t = lax.dot_general(q, kk, _NT, preferred_element_type=F32) * QK_LOG2
            m = jnp.max(t, axis=-1, keepdims=True)
            p = jnp.exp2(t - m)
            l = jnp.sum(p, axis=-1, keepdims=True)
            o_ref[:, g * HEAD:(g + 1) * HEAD] = jnp.dot(p.astype(BF16), vv, preferred_element_type=F32) * (1.0 / l)
            lse_ref[:, g * HEAD:(g + 1) * HEAD] = jnp.broadcast_to(m + jnp.log2(l), (tq, HEAD))

    return _pallas(
        body, name="attn_global_fwd", grid=(2, n // tq),
        out_shape=(jax.ShapeDtypeStruct((n, 16 * HEAD), F32), jax.ShapeDtypeStruct((n, 8 * HEAD), F32)),
        in_specs=[pl.BlockSpec((tq, 4 * HEAD), lambda kv, i: (i, 3 + kv)),
                  pl.BlockSpec((na, HEAD), lambda kv, i: (0, 20 + kv)),
                  pl.BlockSpec((na, HEAD), lambda kv, i: (0, 22 + kv)), _ANY],
        out_specs=(pl.BlockSpec((tq, 4 * HEAD), lambda kv, i: (i, 2 + kv)),
                   pl.BlockSpec((tq, 4 * HEAD), lambda kv, i: (i, kv))),
        input_output_aliases={3: 0},
        compiler_params=_params(_mb(48)),
    )(t_all, t_all, t_all, o_part)


def _attn_window_bwd(t_all, o, do, lse, sink):
    na = t_all.shape[0]
    n = na - CTX
    tq = WINDOW

    def body(sink_ref, q_ref, k_ref, v_ref, o_ref, do_ref, lse_ref, dq_ref, dk_ref, dv_ref, dsink_ref):
        kv = pl.program_id(0)

        @pl.when(pl.program_id(1) == 0)
        def _():
            dk_ref[...] = jnp.zeros_like(dk_ref)
            dv_ref[...] = jnp.zeros_like(dv_ref)
            dsink_ref[...] = jnp.zeros_like(dsink_ref)

        kk, vv, valid, start = _window_keys(k_ref, v_ref, n, na)
        q = _stack_heads(q_ref)
        t = lax.dot_general(q, kk, _NT, preferred_element_type=F32) * QK_LOG2
        t = jnp.where(valid, t, NEG)
        lse = _stack_heads(lse_ref, 1)
        p = jnp.exp2(t - lse)
        dof = _stack_heads(do_ref)
        delta = jnp.sum(dof * _stack_heads(o_ref), axis=-1, keepdims=True)
        dob = dof.astype(BF16)
        dv_acc = lax.dot_general(p.astype(BF16), dob, _TN, preferred_element_type=F32)
        dp = lax.dot_general(dob, vv, _NT, preferred_element_type=F32)
        ds = (p * (dp - delta) * SCALE).astype(BF16)
        dq = jnp.dot(ds, kk, preferred_element_type=F32)
        dk_acc = lax.dot_general(ds, q, _TN, preferred_element_type=F32)
        dsk = -(jnp.exp2(_sink_column(sink_ref, kv, tq) - lse) * delta)
        for g in range(4):
            dq_ref[:, g * HEAD:(g + 1) * HEAD] = dq[g * tq:(g + 1) * tq]
            dsink_ref[0, g:g + 1, :] += jnp.broadcast_to(_colsum(dsk[g * tq:(g + 1) * tq]), (1, HEAD))
        dk_ref[pl.ds(start, 3 * tq), :] += dk_acc[:3 * tq]
        dv_ref[pl.ds(start, 3 * tq), :] += dv_acc[:3 * tq]
        dk_ref[n:na, :] += dk_acc[3 * tq:]
        dv_ref[n:na, :] += dv_acc[3 * tq:]

    blk = pl.BlockSpec((tq, 4 * HEAD), lambda kv, i: (i, kv))
    kvout = pl.BlockSpec((na, HEAD), lambda kv, i: (0, kv))
    return _pallas(
        body, name="attn_window_bwd", grid=(2, n // tq),
        out_shape=(jax.ShapeDtypeStruct((n, 8 * HEAD), F32), jax.ShapeDtypeStruct((na, 2 * HEAD), F32),
                   jax.ShapeDtypeStruct((na, 2 * HEAD), F32), jax.ShapeDtypeStruct((2, 8, HEAD), F32)),
        in_specs=[pl.BlockSpec(memory_space=pltpu.SMEM), blk,
                  pl.BlockSpec((na, HEAD), lambda kv, i: (0, 8 + kv)),
                  pl.BlockSpec((na, HEAD), lambda kv, i: (0, 10 + kv)),
                  blk, blk, blk],
        out_specs=(blk, kvout, kvout, pl.BlockSpec((1, 8, HEAD), lambda kv, i: (kv, 0, 0))),
        compiler_params=_params(_mb(40)),
    )(sink, t_all, t_all, t_all, o, do, lse)


def _attn_global_bwd(t_all, kt, o, do, lse):
    na = t_all.shape[0]
    n = na - CTX
    tq = 256

    def body(q_ref, k_ref, v_ref, kt_ref, o_ref, do_ref, lse_ref, dq_ref, dk_ref, dv_ref, dkt_acc, dvt_acc):
        i = pl.program_id(1)

        @pl.when(i == 0)
        def _():
            dkt_acc[...] = jnp.zeros_like(dkt_acc)
            dvt_acc[...] = jnp.zeros_like(dvt_acc)

        kk, vv, kt_v = k_ref[...], v_ref[...], kt_ref[...]
        dkt = jnp.zeros((HEAD, na), F32)
        dvt = jnp.zeros((HEAD, na), F32)
        for g in range(4):
            q = q_ref[:, g * HEAD:(g + 1) * HEAD]
            t = lax.dot_general(q, kk, _NT, preferred_element_type=F32) * QK_LOG2
            p = jnp.exp2(t - lse_ref[:, g * HEAD:g * HEAD + 1])
            dof = do_ref[:, g * HEAD:(g + 1) * HEAD]
            delta = jnp.sum(dof * o_ref[:, g * HEAD:(g + 1) * HEAD], axis=-1, keepdims=True)
            dob = dof.astype(BF16)
            dvt = dvt + lax.dot_general(dob, p.astype(BF16), _TN, preferred_element_type=F32)
            dp = lax.dot_general(dob, vv, _NT, preferred_element_type=F32)
            ds = (p * (dp - delta) * SCALE).astype(BF16)
            dq_ref[:, g * HEAD:(g + 1) * HEAD] = lax.dot_general(kt_v, ds, _NT, preferred_element_type=F32).T
            dkt = dkt + lax.dot_general(q, ds, _TN, preferred_element_type=F32)
        dkt_acc[...] += dkt
        dvt_acc[...] += dvt

        @pl.when(i == pl.num_programs(1) - 1)
        def _():
            dk_ref[...] = dkt_acc[...].T
            dv_ref[...] = dvt_acc[...].T

    ospec = pl.BlockSpec((tq, 4 * HEAD), lambda kv, i: (i, 2 + kv))
    lspec = pl.BlockSpec((tq, 4 * HEAD), lambda kv, i: (i, kv))
    kvout = pl.BlockSpec((na, HEAD), lambda kv, i: (0, kv))
    return _pallas(
        body, name="attn_global_bwd", grid=(2, n // tq),
        out_shape=(jax.ShapeDtypeStruct((n, 8 * HEAD), F32), jax.ShapeDtypeStruct((na, 2 * HEAD), F32),
                   jax.ShapeDtypeStruct((na, 2 * HEAD), F32)),
        in_specs=[pl.BlockSpec((tq, 4 * HEAD), lambda kv, i: (i, 3 + kv)),
                  pl.BlockSpec((na, HEAD), lambda kv, i: (0, 20 + kv)),
                  pl.BlockSpec((na, HEAD), lambda kv, i: (0, 22 + kv)),
                  pl.BlockSpec((HEAD, na), lambda kv, i: (kv, 0)),
                  ospec, ospec, lspec],
        out_specs=(lspec, kvout, kvout),
        scratch_shapes=[pltpu.VMEM((HEAD, na), F32), pltpu.VMEM((HEAD, na), F32)],
        compiler_params=_params(_mb(56)),
    )(t_all, t_all, t_all, kt, o, do, lse)


def _outproj_ln1(o, wout, x, g1, lg, lb, sc2, sh2, after):
    n, d = x.shape
    tm = 256

    def body(o_ref, w_ref, x_ref, g1_ref, lg_ref, lb_ref, sc_ref, sh_ref, after_ref, a_ref, xh_ref, rs_ref, u_ref):
        a1 = jnp.dot(o_ref[...].astype(BF16), w_ref[...], preferred_element_type=F32)
        a_ref[...] = a1
        r = ALPHA * x_ref[...] + g1_ref[...] * a1
        dlt = r - _rowmean(r)
        rstd = lax.rsqrt(_rowmean(dlt * dlt) + EPS)
        xh = dlt * rstd
        xh_ref[...] = xh
        rs_ref[...] = rstd
        x1 = xh * lg_ref[...] + lb_ref[...]
        u_ref[...] = (x1 * (1.0 + sc_ref[...]) + sh_ref[...]).astype(BF16)

    row = lambda i: (i, 0)
    const2 = lambda i: (0, 0)
    vec = pl.BlockSpec((1, d), const2)
    big = pl.BlockSpec((tm, d), row)
    return _pallas(
        body, name="outproj_ln1", grid=(n // tm,),
        out_shape=(jax.ShapeDtypeStruct((n, d), F32), jax.ShapeDtypeStruct((n, d), F32),
                   jax.ShapeDtypeStruct((n, 1), F32), jax.ShapeDtypeStruct((n, d), BF16)),
        in_specs=[big, pl.BlockSpec((d, d), const2), big, vec, vec, vec, vec, vec, _ANY],
        out_specs=(big, big, pl.BlockSpec((tm, 1), row), big),
        compiler_params=_params(_mb(56)),
    )(o, wout, x, g1, lg, lb, sc2, sh2, after)


def _ffn_up(u2, wgt, wut, after):
    n, d = u2.shape
    f = wgt.shape[0]
    tm = min(1024, n)

    def body(u_ref, wg_ref, wu_ref, after_ref, sa_ref, sb_ref, hf_ref):
        u = u_ref[...]
        gv = lax.dot_general(u, wg_ref[...], _NT, preferred_element_type=F32)
        pv = lax.dot_general(u, wu_ref[...], _NT, preferred_element_type=F32)
        sg = _sigmoid(gv)
        silu = gv * sg
        sa_ref[...] = silu.astype(BF16)
        sb_ref[...] = (pv * (sg * (1.0 + gv * (1.0 - sg)))).astype(BF16)
        hf_ref[...] = (silu * pv).astype(BF16)

    tile = pl.BlockSpec((tm, FFN_TILE), lambda i, j: (i, j))
    wspec = pl.BlockSpec((FFN_TILE, d), lambda i, j: (j, 0))
    sds = jax.ShapeDtypeStruct((n, f), BF16)
    return _pallas(
        body, name="ffn_up", grid=(n // tm, f // FFN_TILE),
        out_shape=(sds, sds, sds),
        in_specs=[pl.BlockSpec((tm, d), lambda i, j: (i, 0)), wspec, wspec, _ANY],
        out_specs=(tile, tile, tile),
        compiler_params=_params(_mb(48)),
    )(u2, wgt, wut, after)


def _ffn_down(hf, wd):
    n, f = hf.shape
    d = wd.shape[1]
    tm, tn = min(1024, n), 512

    def body(h_ref, w_ref, o_ref):
        o_ref[...] = jnp.dot(h_ref[...], w_ref[...], preferred_element_type=F32)

    return _pallas(
        body, name="ffn_down", grid=(n // tm, d // tn),
        out_shape=jax.ShapeDtypeStruct((n, d), F32),
        in_specs=[pl.BlockSpec((tm, f), lambda i, j: (i, 0)), pl.BlockSpec((f, tn), lambda i, j: (0, j))],
        out_specs=pl.BlockSpec((tm, tn), lambda i, j: (i, j)),
        compiler_params=_params(_mb(56)),
    )(hf, wd)


def _ln2_loss(xh1, ffn, tgt, lg1, lb1, g2, lg2, lb2):
    n, d = xh1.shape
    tm = 256

    def body(xh_ref, f_ref, t_ref, lg1_ref, lb1_ref, g2_ref, lg2_ref, lb2_ref, dr_ref, df_ref, loss_ref, acc_ref):
        @pl.when(pl.program_id(0) == 0)
        def _():
            loss_ref[...] = jnp.zeros_like(loss_ref)
            acc_ref[...] = jnp.zeros_like(acc_ref)

        x1 = xh_ref[...] * lg1_ref[...] + lb1_ref[...]
        fv = f_ref[...]
        r = ALPHA * x1 + g2_ref[...] * fv
        dlt = r - _rowmean(r)
        rstd = lax.rsqrt(_rowmean(dlt * dlt) + EPS)
        xh2 = dlt * rstd
        err = xh2 * lg2_ref[...] + lb2_ref[...] - t_ref[...]
        loss_ref[...] += 0.5 * jnp.sum(_rowmean(err * err))
        dy = err * (1.0 / d)
        dyg = dy * lg2_ref[...]
        dr = rstd * (dyg - _rowmean(dyg) - xh2 * _rowmean(dyg * xh2))
        dr_ref[...] = dr
        df_ref[...] = (g2_ref[...] * dr).astype(BF16)
        acc_ref[0:1, :] += _colsum(dy * xh2)
        acc_ref[1:2, :] += _colsum(dy)
        acc_ref[2:3, :] += _colsum(dr * fv)

    row = lambda i: (i, 0)
    const2 = lambda i: (0, 0)
    vec = pl.BlockSpec((1, d), const2)
    big = pl.BlockSpec((tm, d), row)
    return _pallas(
        body, name="ln2_loss", grid=(n // tm,),
        out_shape=(jax.ShapeDtypeStruct((n, d), F32), jax.ShapeDtypeStruct((n, d), BF16),
                   jax.ShapeDtypeStruct((8, HEAD), F32), jax.ShapeDtypeStruct((8, d), F32)),
        in_specs=[big, big, big, vec, vec, vec, vec, vec],
        out_specs=(big, big, pl.BlockSpec((8, HEAD), const2), pl.BlockSpec((8, d), const2)),
        compiler_params=_params(_mb(48)),
    )(xh1, ffn, tgt, lg1, lb1, g2, lg2, lb2)


def _ffn_dhf(df, wd, sa, sb):
    n, d = df.shape
    f = sa.shape[1]
    tm = min(1024, n)

    def body(df_ref, w_ref, sa_ref, sb_ref, dg_ref, dp_ref):
        dhf = lax.dot_general(df_ref[...], w_ref[...], _NT, preferred_element_type=F32)
        dp_ref[...] = (dhf * sa_ref[...].astype(F32)).astype(BF16)
        dg_ref[...] = (dhf * sb_ref[...].astype(F32)).astype(BF16)

    tile = pl.BlockSpec((tm, FFN_TILE), lambda i, j: (i, j))
    sds = jax.ShapeDtypeStruct((n, f), BF16)
    return _pallas(
        body, name="ffn_dhf", grid=(n // tm, f // FFN_TILE),
        out_shape=(sds, sds),
        in_specs=[pl.BlockSpec((tm, d), lambda i, j: (i, 0)), pl.BlockSpec((FFN_TILE, d), lambda i, j: (j, 0)),
                  tile, tile],
        out_specs=(tile, tile),
        compiler_params=_params(_mb(48)),
    )(df, wd, sa, sb)


def _ffn_du2(dg, dp, wgt, wut, after):
    n, f = dg.shape
    d = wgt.shape[1]
    tm = min(1024, n)

    def body(dg_ref, dp_ref, wg_ref, wu_ref, after_ref, o_ref):
        part = (jnp.dot(dg_ref[...], wg_ref[...], preferred_element_type=F32)
                + jnp.dot(dp_ref[...], wu_ref[...], preferred_element_type=F32))

        @pl.when(pl.program_id(1) == 0)
        def _():
            o_ref[...] = part

        @pl.when(pl.program_id(1) > 0)
        def _():
            o_ref[...] += part

    tile = pl.BlockSpec((tm, FFN_TILE), lambda i, j: (i, j))
    wspec = pl.BlockSpec((FFN_TILE, d), lambda i, j: (j, 0))
    return _pallas(
        body, name="ffn_du2", grid=(n // tm, f // FFN_TILE),
        out_shape=jax.ShapeDtypeStruct((n, d), F32),
        in_specs=[tile, tile, wspec, wspec, _ANY],
        out_specs=pl.BlockSpec((tm, d), lambda i, j: (i, 0)),
        compiler_params=_params(_mb(48)),
    )(dg, dp, wgt, wut, after)


def _ln1_bwd(du2, dr2, xh1, rs1, a1, lg1, lb1, sc2, g1):
    n, d = du2.shape
    tm = 256

    def body(du_ref, dr2_ref, xh_ref, rs_ref, a_ref, lg_ref, lb_ref, sc_ref, g1_ref, dr1_ref, da_ref, acc_ref):
        @pl.when(pl.program_id(0) == 0)
        def _():
            acc_ref[...] = jnp.zeros_like(acc_ref)

        du = du_ref[...]
        xh = xh_ref[...]
        x1 = xh * lg_ref[...] + lb_ref[...]
        dx1 = ALPHA * dr2_ref[...] + du * (1.0 + sc_ref[...])
        dxg = dx1 * lg_ref[...]
        dr1 = rs_ref[...] * (dxg - _rowmean(dxg) - xh * _rowmean(dxg * xh))
        dr1_ref[...] = dr1
        da_ref[...] = (g1_ref[...] * dr1).astype(BF16)
        acc_ref[0:1, :] += _colsum(du * x1)
        acc_ref[1:2, :] += _colsum(du)
        acc_ref[2:3, :] += _colsum(dx1 * xh)
        acc_ref[3:4, :] += _colsum(dx1)
        acc_ref[4:5, :] += _colsum(dr1 * a_ref[...])

    row = lambda i: (i, 0)
    const2 = lambda i: (0, 0)
    vec = pl.BlockSpec((1, d), const2)
    big = pl.BlockSpec((tm, d), row)
    return _pallas(
        body, name="ln1_bwd", grid=(n // tm,),
        out_shape=(jax.ShapeDtypeStruct((n, d), F32), jax.ShapeDtypeStruct((n, d), BF16),
                   jax.ShapeDtypeStruct((8, d), F32)),
        in_specs=[big, big, big, pl.BlockSpec((tm, 1), row), big, vec, vec, vec, vec],
        out_specs=(big, big, pl.BlockSpec((8, d), const2)),
        compiler_params=_params(_mb(48)),
    )(du2, dr2, xh1, rs1, a1, lg1, lb1, sc2, g1)


def _dw_rows(a, b, nblk, bw, tm, after, name):
    m = a.shape[0]
    nn = b.shape[1]

    def body(a_ref, b_ref, after_ref, o_ref, acc_ref):
        part = lax.dot_general(a_ref[...].astype(BF16), b_ref[...], _TN, preferred_element_type=F32)
        i = pl.program_id(1)

        @pl.when(i == 0)
        def _():
            acc_ref[...] = part

        @pl.when(i > 0)
        def _():
            acc_ref[...] += part

        @pl.when(i == pl.num_programs(1) - 1)
        def _():
            o_ref[0] = acc_ref[...].astype(BF16)

    return _pallas(
        body, name=name, grid=(nblk, m // tm),
        out_shape=jax.ShapeDtypeStruct((nblk, bw, nn), BF16),
        in_specs=[pl.BlockSpec((tm, bw), lambda j, i: (i, j)), pl.BlockSpec((tm, nn), lambda j, i: (i, 0)), _ANY],
        out_specs=pl.BlockSpec((1, bw, nn), lambda j, i: (j, 0, 0)),
        scratch_shapes=[pltpu.VMEM((bw, nn), F32)],
        compiler_params=_params(_mb(56)),
    )(a, b, after)


def _outproj_bwd(da1, wout, after):
    n, d = da1.shape
    tm = 512

    def body(a_ref, w_ref, after_ref, o_ref):
        o_ref[...] = lax.dot_general(a_ref[...], w_ref[...], _NT, preferred_element_type=F32)

    return _pallas(
        body, name="outproj_bwd", grid=(n // tm,),
        out_shape=jax.ShapeDtypeStruct((n, d), F32),
        in_specs=[pl.BlockSpec((tm, d), lambda i: (i, 0)), pl.BlockSpec((d, d), lambda i: (0, 0)), _ANY],
        out_specs=pl.BlockSpec((tm, d), lambda i: (i, 0)),
        compiler_params=_params(_mb(48)),
    )(da1, wout, after)


def _qkv_bwd(dh, wint, x, ct, dr1, sc):
    na, wcols = dh.shape
    n, d = x.shape
    tm = CTX
    nlat = n // tm

    def body(dh_ref, w_ref, x_ref, ct_ref, dr_ref, sc_ref, gx_ref, acc_ref):
        i = pl.program_id(0)

        @pl.when(i == 0)
        def _():
            acc_ref[...] = jnp.zeros_like(acc_ref)

        du = jnp.dot(dh_ref[...], w_ref[...], preferred_element_type=F32)

        @pl.when(i < nlat)
        def _():
            gx_ref[...] = ALPHA * dr_ref[...] + du * (1.0 + sc_ref[0])
            acc_ref[0:1, :] += _colsum(du)
            acc_ref[1:2, :] += _colsum(du * x_ref[...])

        @pl.when(i == nlat)
        def _():
            acc_ref[2:3, :] += _colsum(du)
            acc_ref[3:4, :] += _colsum(du * ct_ref[...])

    lat = lambda i: (jnp.minimum(i, nlat - 1), 0)
    const2 = lambda i: (0, 0)
    return _pallas(
        body, name="qkv_bwd", grid=(nlat + 1,),
        out_shape=(jax.ShapeDtypeStruct((n, d), F32), jax.ShapeDtypeStruct((8, d), F32)),
        in_specs=[pl.BlockSpec((tm, wcols), lambda i: (i, 0)), pl.BlockSpec((wcols, d), const2),
                  pl.BlockSpec((tm, d), lat), pl.BlockSpec((tm, d), const2), pl.BlockSpec((tm, d), lat),
                  pl.BlockSpec((1, 1, d), lambda i: (0, 0, 0))],
        out_specs=(pl.BlockSpec((tm, d), lat), pl.BlockSpec((8, d), const2)),
        compiler_params=_params(_mb(56)),
    )(dh, wint, x, ct, dr1, sc)


def _adam_math(w, g, m, v):
    m2 = ADAM_B1 * m + (1.0 - ADAM_B1) * g
    v2 = ADAM_B2 * v + (1.0 - ADAM_B2) * (g * g)
    m_hat = m2 * (1.0 / (1.0 - ADAM_B1 ** ADAM_STEP))
    v_hat = v2 * (1.0 / (1.0 - ADAM_B2 ** ADAM_STEP))
    delta = -ADAM_LR * (m_hat / (jnp.sqrt(v_hat) + ADAM_EPS) + ADAM_WD * w)
    return delta, m2, v2


def _adamw(w, gsrc, m, v, name):
    r, c = w.shape
    parts = gsrc.ndim == 3
    cg = gsrc.shape[-1]
    tr = r
    while tr * c * 4 > _mb(1) and tr % 32 == 0:
        tr //= 2

    def body(w_ref, g_ref, m_ref, v_ref, go_ref, d_ref, mo_ref, vo_ref):
        if parts:
            g = g_ref[0].astype(F32)
            for s in range(1, NDEV):
                g = g + g_ref[s].astype(F32)
            g = g[:, :c]
        else:
            g = g_ref[...]
        delta, m2, v2 = _adam_math(w_ref[...], g, m_ref[...], v_ref[...])
        go_ref[...] = g
        d_ref[...] = delta
        mo_ref[...] = m2
        vo_ref[...] = v2

    tile = pl.BlockSpec((tr, c), lambda i: (i, 0))
    gspec = pl.BlockSpec((NDEV, tr, cg), lambda i: (0, i, 0)) if parts else tile
    sds = jax.ShapeDtypeStruct((r, c), F32)
    return _pallas(
        body, name=name, grid=(r // tr,),
        out_shape=(sds, sds, sds, sds),
        in_specs=[tile, gspec, tile, tile],
        out_specs=(tile, tile, tile, tile),
        compiler_params=_params(_mb(48)),
    )(w, gsrc, m, v)


def _adamw_t(w, gsrc_t, m, v, name):
    r, c = w.shape
    tr = 256

    def body(w_ref, g_ref, m_ref, v_ref, go_ref, d_ref, mo_ref, vo_ref):
        gt = g_ref[0].astype(F32)
        for s in range(1, NDEV):
            gt = gt + g_ref[s].astype(F32)
        g = gt.T
        delta, m2, v2 = _adam_math(w_ref[...], g, m_ref[...], v_ref[...])
        go_ref[...] = g
        d_ref[...] = delta
        mo_ref[...] = m2
        vo_ref[...] = v2

    tile = pl.BlockSpec((tr, c), lambda i: (i, 0))
    sds = jax.ShapeDtypeStruct((r, c), F32)
    return _pallas(
        body, name=name, grid=(r // tr,),
        out_shape=(sds, sds, sds, sds),
        in_specs=[tile, pl.BlockSpec((NDEV, c, tr), lambda i: (0, 0, i)), tile, tile],
        out_specs=(tile, tile, tile, tile),
        compiler_params=_params(_mb(48)),
    )(w, gsrc_t, m, v)


def _small_update(gath, dcc, cc, w_s, m_s, v_s):
    d = w_s.shape[1]

    def body(g_ref, dcc_ref, cc_ref, w_ref, m_ref, v_ref, go_ref, d_ref, mo_ref, vo_ref):
        s = g_ref[0]
        for b in range(1, NDEV):
            s = s + g_ref[b]
        dsl = dcc_ref[0, 8:9, :]
        for b in range(1, NDEV):
            dsl = dsl + dcc_ref[b, 8:9, :]
        cv = cc_ref[...]
        sg = _sigmoid(cv)
        go_ref[...] = jnp.zeros_like(go_ref)
        go_ref[0:1, :] = dsl * (sg * (1.0 + cv * (1.0 - sg)))
        go_ref[1:3, :] = s[0:2] + s[6:8]
        go_ref[3:7, :] = s[2:6]
        go_ref[7:12, :] = s[8:13]
        delta, m2, v2 = _adam_math(w_ref[...], go_ref[...], m_ref[...], v_ref[...])
        d_ref[...] = delta
        mo_ref[...] = m2
        vo_ref[...] = v2

    full = pl.BlockSpec((16, d), lambda: (0, 0))
    g3 = pl.BlockSpec((NDEV, 16, d), lambda: (0, 0, 0))
    sds = jax.ShapeDtypeStruct((16, d), F32)
    return _pallas(
        body, name="small_update",
        out_shape=(sds, sds, sds, sds),
        in_specs=[g3, g3, pl.BlockSpec((1, d), lambda: (0, 0)), full, full, full],
        out_specs=(full, full, full, full),
        compiler_params=_params(_mb(24)),
    )(gath, dcc, cc, w_s, m_s, v_s)


def _rope_tables(n):
    rows = n // GRID_W
    row_ids = jnp.repeat(jnp.arange(rows, dtype=F32), GRID_W)
    col_ids = jnp.tile(jnp.arange(GRID_W, dtype=F32), rows)
    axis_dim = HEAD // 2
    inv_freq = jnp.power(ROPE_THETA, -jnp.arange(0, axis_dim, 2, dtype=F32) / axis_dim)
    ang_r = row_ids[:, None] * inv_freq
    ang_c = col_ids[:, None] * inv_freq
    ang = jnp.concatenate([ang_r, ang_r, ang_c, ang_c], axis=-1)
    cos, sin = jnp.cos(ang), jnp.sin(ang)
    first = (jnp.arange(HEAD) % (HEAD // 2)) < HEAD // 4
    sa = jnp.where(first, -sin, 0.0)
    sb = jnp.where(first, 0.0, sin)
    ones = jnp.ones((CTX, HEAD), F32)
    zeros = jnp.zeros((CTX, HEAD), F32)
    return (jnp.concatenate([cos, ones], 0), jnp.concatenate([sa, zeros], 0), jnp.concatenate([sb, zeros], 0))


def _pad_cols(a, width):
    return jnp.pad(a, ((0, 0), (0, width - a.shape[1])))


def _pad_rows(a, rows):
    return jnp.pad(a, ((0, rows - a.shape[0]), (0, 0)))


def _pack_small(c_ctx, b_ada, ln1_g, ln1_b, ln2_g, ln2_b, qg, kg, sink, d):
    misc = _pad_cols(jnp.concatenate([qg, kg, sink], axis=1), d)
    rows = jnp.concatenate([c_ctx.reshape(1, d), b_ada.reshape(6, d), ln1_g, ln1_b, ln2_g, ln2_b, misc], axis=0)
    return _pad_rows(rows, 16)


def _unpack_small(p, d):
    return dict(c_ctx=p[0], b_ada=p[1:7].reshape(1, 6 * d), ln1_g=p[7:8], ln1_b=p[8:9], ln2_g=p[9:10], ln2_b=p[10:11],
                q_norm_g=p[11:12, 0:HEAD], k_norm_g=p[11:12, HEAD:2 * HEAD], sink_logit=p[11:12, 2 * HEAD:2 * HEAD + 8])


def kernel(x, c, ctx, c_ctx, w_ada, b_ada, w_in, q_norm_g, k_norm_g, sink_logit, w_out, ln1_g, ln1_b, w_gate, w_up, w_down, ln2_g, ln2_b, loss_target, m_c_ctx, m_w_ada, m_b_ada, m_w_in, m_q_norm_g, m_k_norm_g, m_sink_logit, m_w_out, m_ln1_g, m_ln1_b, m_w_gate, m_w_up, m_w_down, m_ln2_g, m_ln2_b, v_c_ctx, v_w_ada, v_b_ada, v_w_in, v_q_norm_g, v_k_norm_g, v_sink_logit, v_w_out, v_ln1_g, v_ln1_b, v_w_gate, v_w_up, v_w_down, v_ln2_g, v_ln2_b):
    xs, cts, tgt = x[0], ctx[0], loss_target[0]
    n, d = xs.shape
    assert cts.shape == (CTX, d) and w_in.shape[2] == IN_SHARD and w_gate.shape[2] == FFN_SHARD
    me = 4 * lax.axis_index("x") + 2 * lax.axis_index("y") + lax.axis_index("c")
    e_sh = w_ada.shape[2]

    c_g = _exchange(_pad_rows(c, 8), False, "gather_c")
    c_all = jnp.concatenate([c_g[:, 0, :], _pad_rows(c_ctx.reshape(1, d), 8)], axis=0)
    bias_sh = lax.dynamic_slice(b_ada, (0, me * e_sh), (1, e_sh))
    mods_g = _exchange(_ada_fwd(c_all, w_ada[0], bias_sh), False, "gather_mods")
    mods = jnp.transpose(mods_g, (1, 0, 2)).reshape(16, NDEV * e_sh)
    mine = lax.dynamic_slice(mods, (me, 0), (1, 6 * d))
    sh1, sc1, g1, sh2, sc2, g2 = [mine[:, k * d:(k + 1) * d] for k in range(6)]
    csh1, csc1 = mods[8:9, 0:d], mods[8:9, d:2 * d]
    sc_pair = jnp.stack([sc1, csc1])
    sh_pair = jnp.stack([sh1, csh1])

    h_win, tok = _exchange_start(w_in[0].T.astype(BF16), "chip", mods, "gather_w_in_start")
    tok, (wo_l, wg_l, wu_l, wd_l) = lax.optimization_barrier((tok, (w_out, w_gate, w_up, w_down)))
    h_wout, tok = _exchange_start(wo_l[0].astype(BF16), "chip", tok, "gather_w_out_start")
    h_wg, tok = _exchange_start(wg_l[0].T.astype(BF16), "chip", tok, "gather_w_gate_start")
    h_wu, tok = _exchange_start(wu_l[0].T.astype(BF16), "chip", tok, "gather_w_up_start")
    h_wd, tok = _exchange_start(wd_l[0].astype(BF16), "chip", tok, "gather_w_down_start")

    cos, sa, sb = _rope_tables(n)
    f_win, tok = _forward_start(_exchange_wait(h_win, "chip", tok, "gather_w_in_wait"), tok, "forward_w_in_start")
    win_g = _forward_wait(f_win, tok, "forward_w_in_wait").reshape(NDEV * IN_SHARD, d)
    u_all, h_all, t_all, kt_b = _qkv_fwd(xs, cts, sc_pair, sh_pair, win_g, q_norm_g, k_norm_g, cos, sa, sb)
    f_wout, tok = _forward_start(_exchange_wait(h_wout, "chip", t_all, "gather_w_out_wait"), t_all, "forward_w_out_start")
    o_a, lse_a = _attn_window_fwd(t_all, sink_logit, tok)
    o, lse_b = _attn_global_fwd(t_all, o_a)
    f_wg, tok = _forward_start(_exchange_wait(h_wg, "chip", o, "gather_w_gate_wait"), o, "forward_w_gate_start")
    f_wu, tok = _forward_start(_exchange_wait(h_wu, "chip", tok, "gather_w_up_wait"), tok, "forward_w_up_start")
    wout_g = _forward_wait(f_wout, tok, "forward_w_out_wait").reshape(d, d)
    a1, xh1, rs1, u2 = _outproj_ln1(o, wout_g, xs, g1, ln1_g, ln1_b, sc2, sh2, tok)
    f_wd, tok = _forward_start(_exchange_wait(h_wd, "chip", rs1, "gather_w_down_wait"), rs1, "forward_w_down_start")
    ffn_w = (NDEV * FFN_SHARD, d)
    wg_g = _forward_wait(f_wg, tok, "forward_w_gate_wait").reshape(ffn_w)
    wu_g = _forward_wait(f_wu, tok, "forward_w_up_wait").reshape(ffn_w)
    sa_f, sb_f, hf = _ffn_up(u2, wg_g, wu_g, tok)
    wd_g = _forward_wait(f_wd, hf, "forward_w_down_wait").reshape(ffn_w)
    ffn = _ffn_down(hf, wd_g)
    dr2, df, loss_p, acc2 = _ln2_loss(xh1, ffn, tgt, ln1_g, ln1_b, g2, ln2_g, ln2_b)
    loss = lax.psum(loss_p[0, 0], ("x", "y", "c"))

    tk = min(n, 2048)
    parts = (NDEV, FFN_SHARD, d)
    dgm, dpm = _ffn_dhf(df, wd_g, sa_f, sb_f)
    dwd_p = _dw_rows(hf, df, NDEV // 2, FFN_PAIR, min(n, 1024), loss_p, "dw_down").reshape(parts)
    h_dwd, tok = _exchange_start(dwd_p, "scatter", loss.reshape(1, 1), "scatter_dw_down_start")
    dwg_p = _dw_rows(dgm, u2, NDEV // 2, FFN_PAIR, min(n, 1024), tok, "dw_gate").reshape(parts)
    h_dwg, tok = _exchange_start(dwg_p, "scatter", tok, "scatter_dw_gate_start")
    dwu_p = _dw_rows(dpm, u2, NDEV // 2, FFN_PAIR, min(n, 1024), tok, "dw_up").reshape(parts)
    h_dwu, tok = _exchange_start(dwu_p, "scatter", tok, "scatter_dw_up_start")
    du2 = _ffn_du2(dgm, dpm, wg_g, wu_g, tok)
    dr1, da1, acc1 = _ln1_bwd(du2, dr2, xh1, rs1, a1, ln1_g, ln1_b, sc2, g1)
    dwo_p = _dw_rows(o, da1, NDEV, 2 * HEAD, tk, loss_p, "dw_out")
    h_dwo, tok = _exchange_start(dwo_p, "scatter", loss_p, "scatter_dw_out_start")
    do = _outproj_bwd(da1, wout_g, tok)
    dqa, dka, dva, dsink = _attn_window_bwd(t_all, o, do, lse_a, sink_logit)
    dqb, dkb, dvb = _attn_global_bwd(t_all, kt_b, o, do, lse_b)
    dh_all, dnorm = _qkv_bwd_prep(dqa, dka, dva, dqb, dkb, dvb, h_all, q_norm_g, k_norm_g, cos, sa, sb)
    grad_x, acc0 = _qkv_bwd(dh_all, win_g, xs, cts, dr1, sc_pair)

    misc = _pad_cols(jnp.concatenate([dnorm[0:1], dnorm[1:2], dsink[:, 0:4, 0].reshape(1, 8)], axis=1), d)
    part = jnp.concatenate([
        acc0[0:2], acc1[4:5], acc1[1:2], acc1[0:1], acc2[2:3],
        acc0[2:4],
        acc1[2:4], acc2[0:2],
        misc, jnp.zeros((3, d), F32)], axis=0)
    gath = _exchange(part, False, "gather_small")
    dm_batch = gath[:, 0:6, :].reshape(NDEV, 6 * d)
    dm_ctx = _pad_cols(gath[:, 6:8, :].reshape(NDEV, 2 * d), 6 * d)
    dm16 = lax.dynamic_slice(jnp.concatenate([dm_batch, dm_ctx], axis=0), (0, me * e_sh), (16, e_sh))
    dw_ada, drow = _ada_bwd(dm16, c_all, w_ada[0])
    dcc = _exchange(drow, False, "gather_dcc")
    dwi_p = _dw_rows(dh_all, u_all, NDEV, IN_SHARD, (n + CTX) // 2, dcc, "dw_in")
    h_dwi, tok = _exchange_start(dwi_p, "scatter", dcc, "scatter_dw_in_start")

    w_s = _pack_small(c_ctx, b_ada, ln1_g, ln1_b, ln2_g, ln2_b, q_norm_g, k_norm_g, sink_logit, d)
    m_s = _pack_small(m_c_ctx, m_b_ada, m_ln1_g, m_ln1_b, m_ln2_g, m_ln2_b, m_q_norm_g, m_k_norm_g, m_sink_logit, d)
    v_s = _pack_small(v_c_ctx, v_b_ada, v_ln1_g, v_ln1_b, v_ln2_g, v_ln2_b, v_q_norm_g, v_k_norm_g, v_sink_logit, d)
    small = [_unpack_small(p, d) for p in _small_update(gath, dcc, c_ctx.reshape(1, d), w_s, m_s, v_s)]

    big = {}
    big["w_ada"] = _adamw(w_ada[0], dw_ada, m_w_ada[0], v_w_ada[0], "adamw_w_ada")
    late = tok
    big["w_down"] = _adamw(w_down[0], _exchange_wait(h_dwd, "scatter", late, "scatter_dw_down_wait"), m_w_down[0],
                           v_w_down[0], "adamw_w_down")
    for nm, wt, mt, vt, hd in (("w_gate", w_gate, m_w_gate, v_w_gate, h_dwg), ("w_up", w_up, m_w_up, v_w_up, h_dwu)):
        big[nm] = _adamw_t(wt[0], _exchange_wait(hd, "scatter", late, "scatter_d" + nm + "_wait"), mt[0], vt[0],
                           "adamw_" + nm)
    big["w_out"] = _adamw(w_out[0], _exchange_wait(h_dwo, "scatter", late, "scatter_dw_out_wait"), m_w_out[0], v_w_out[0],
                          "adamw_w_out")
    big["w_in"] = _adamw_t(w_in[0], _exchange_wait(h_dwi, "scatter", big["w_out"][1], "scatter_dw_in_wait"), m_w_in[0],
                           v_w_in[0], "adamw_w_in")

    names = ["c_ctx", "w_ada", "b_ada", "w_in", "q_norm_g", "k_norm_g", "sink_logit", "w_out", "ln1_g", "ln1_b",
             "w_gate", "w_up", "w_down", "ln2_g", "ln2_b"]
    outs = [loss, grad_x[None]]
    for k in range(4):
        for nm in names:
            outs.append(big[nm][k][None] if nm in big else small[k][nm])
    return tuple(outs)
```

```python
import functools

import jax
import jax.numpy as jnp
from jax import lax
from jax.experimental import pallas as pl
from jax.experimental.pallas import tpu as pltpu

F32 = jnp.float32
BF16 = jnp.bfloat16

NDEV = 8
HEAD = 128
CTX = 256
GRID_W = 64
WINDOW = 128
ROPE_THETA = 10000.0
EPS = 1e-6
SCALE = HEAD ** -0.5
LOG2E = 1.4426950408889634
QK_LOG2 = SCALE * LOG2E
ALPHA = 2.0 ** 0.25
FFN_SHARD = 704
FFN_TILE = 512
FFN_PAIR = 2 * FFN_SHARD
IN_SHARD = 384
NEG = -1e30

ADAM_LR = 0.001
ADAM_B1 = 0.9
ADAM_B2 = 0.999
ADAM_EPS = 1e-08
ADAM_WD = 0.01
ADAM_STEP = 10

VMEM_CAP = 56 * 1024 * 1024

_KINDS = ["rope"] * 10 + ["none"] * 2 + ["qnorm"] * 8 + ["knorm"] * 2 + ["none"] * 2

_NT = (((1,), (1,)), ((), ()))
_TN = (((0,), (0,)), ((), ()))


def _pallas(body, **kw):
    return pl.pallas_call(body, **kw)


def _params(vmem_bytes):
    return pltpu.CompilerParams(vmem_limit_bytes=int(min(VMEM_CAP, vmem_bytes)))


def _mb(n):
    return int(n * 1024 * 1024)


def _sigmoid(x):
    return 1.0 / (1.0 + jnp.exp(-x))


def _colsum(a):
    return jnp.sum(a, axis=0, keepdims=True)


def _rowmean(a):
    return jnp.mean(a, axis=-1, keepdims=True)


def _exchange(src, scatter, name, after=None):
    blk = src.shape[1:] if scatter else src.shape
    after = src if after is None else after

    def body(src_ref, after_ref, out_ref, send_sems, recv_sems, local_sem):
        x, y, c = lax.axis_index("x"), lax.axis_index("y"), lax.axis_index("c")
        me = 4 * x + 2 * y + c
        copies = []
        for t in range(1, NDEV):
            px = 1 - x if (t >> 2) & 1 else x
            py = 1 - y if (t >> 1) & 1 else y
            pc = 1 - c if t & 1 else c
            peer = 4 * px + 2 * py + pc
            cp = pltpu.make_async_remote_copy(
                src_ref=src_ref.at[peer] if scatter else src_ref,
                dst_ref=out_ref.at[me],
                send_sem=send_sems.at[t - 1],
                recv_sem=recv_sems.at[t - 1],
                device_id=(px, py, pc),
                device_id_type=pl.DeviceIdType.MESH,
            )
            cp.start()
            copies.append(cp)
        own = pltpu.make_async_copy(src_ref.at[me] if scatter else src_ref, out_ref.at[me], local_sem)
        own.start()
        for cp in copies:
            cp.wait()
        own.wait()

    return _pallas(
        body, name=name,
        out_shape=jax.ShapeDtypeStruct((NDEV,) + tuple(blk), src.dtype),
        in_specs=[pl.BlockSpec(memory_space=pl.ANY), pl.BlockSpec(memory_space=pl.ANY)],
        out_specs=pl.BlockSpec(memory_space=pl.ANY),
        scratch_shapes=[pltpu.SemaphoreType.DMA((NDEV - 1,)), pltpu.SemaphoreType.DMA((NDEV - 1,)),
                        pltpu.SemaphoreType.DMA(())],
    )(src, after)


_HBM = pl.BlockSpec(memory_space=pltpu.HBM)
_SEM = pl.BlockSpec(memory_space=pltpu.SEMAPHORE)
_ANY = pl.BlockSpec(memory_space=pl.ANY)
_EFFECT = pltpu.SideEffectType.DATAFLOW_SIDE_EFFECTING


def _exchange_copies(src_ref, land_ref, send_sems, recv_sems, mode):
    x, y, c = lax.axis_index("x"), lax.axis_index("y"), lax.axis_index("c")
    me = 4 * x + 2 * y + c
    scatter = mode == "scatter"
    copies = []
    for t in ((1, 2, 4, 6) if mode == "chip" else range(1, NDEV)):
        px = 1 - x if (t >> 2) & 1 else x
        py = 1 - y if (t >> 1) & 1 else y
        pc = 1 - c if t & 1 else c
        peer = 4 * px + 2 * py + pc
        copies.append(pltpu.make_async_remote_copy(
            src_ref=src_ref.at[peer] if scatter else src_ref,
            dst_ref=land_ref.at[me],
            send_sem=send_sems.at[t - 1],
            recv_sem=recv_sems.at[t - 1],
            device_id=(px, py, pc),
            device_id_type=pl.DeviceIdType.MESH,
        ))
    own = pltpu.make_async_copy(src_ref.at[me] if scatter else src_ref, land_ref.at[me], send_sems.at[NDEV - 1])
    return copies, own


def _forward_copies(land_ref, send_sems, recv_sems):
    x, y, c = lax.axis_index("x"), lax.axis_index("y"), lax.axis_index("c")
    copies = []
    for k, t in enumerate((2, 4, 6)):
        px = 1 - x if (t >> 2) & 1 else x
        py = 1 - y if (t >> 1) & 1 else y
        mine, theirs = 4 * px + 2 * py + c, 4 * px + 2 * py + (1 - c)
        send = pltpu.make_async_remote_copy(
            src_ref=land_ref.at[mine], dst_ref=land_ref.at[mine], send_sem=send_sems.at[k], recv_sem=recv_sems.at[k],
            device_id=(x, y, 1 - c), device_id_type=pl.DeviceIdType.MESH)
        recv = pltpu.make_async_remote_copy(
            src_ref=land_ref.at[theirs], dst_ref=land_ref.at[theirs], send_sem=send_sems.at[k], recv_sem=recv_sems.at[k],
            device_id=(x, y, 1 - c), device_id_type=pl.DeviceIdType.MESH)
        copies.append((send, recv))
    return copies


def _forward_start(land, after, name):
    def body(land_ref, after_ref, send_sems, recv_sems, land_thru, token):
        for send, _ in _forward_copies(land_ref, send_sems, recv_sems):
            send.start()
        token[...] = jnp.zeros_like(token)

    res = _pallas(
        body, name=name,
        out_shape=(pltpu.SemaphoreType.DMA((3,)), pltpu.SemaphoreType.DMA((3,)), pltpu.HBM(land.shape, land.dtype),
                   jax.ShapeDtypeStruct((8, HEAD), F32)),
        in_specs=(_HBM, _ANY), out_specs=(_SEM, _SEM, _HBM, pl.BlockSpec(memory_space=pltpu.VMEM)),
        input_output_aliases={0: 2},
        compiler_params=pltpu.CompilerParams(has_side_effects=_EFFECT),
    )(land, after)
    return res[:3], res[3]


def _forward_wait(handle, after, name):
    send_sems, recv_sems, land_thru = handle

    def body(land_ref, send_sems, recv_sems, after_ref, got_ref):
        for send, recv in _forward_copies(land_ref, send_sems, recv_sems):
            send.wait_send()
            recv.wait_recv()

    return _pallas(
        body, name=name,
        out_shape=pltpu.HBM(land_thru.shape, land_thru.dtype),
        in_specs=(_HBM, _SEM, _SEM, _ANY), out_specs=_HBM,
        input_output_aliases={0: 0},
        compiler_params=pltpu.CompilerParams(has_side_effects=_EFFECT),
    )(land_thru, send_sems, recv_sems, after)


def _exchange_start(src, mode, after, name):
    blk = src.shape[1:] if mode == "scatter" else src.shape
    land = lax.empty((NDEV,) + tuple(blk), src.dtype)

    def body(src_ref, land_ref, after_ref, send_sems, recv_sems, src_thru, land_thru, token):
        copies, own = _exchange_copies(src_ref, land_ref, send_sems, recv_sems, mode)
        for cp in copies:
            cp.start()
        own.start()
        token[...] = jnp.zeros_like(token)

    res = _pallas(
        body, name=name,
        out_shape=(pltpu.SemaphoreType.DMA((NDEV,)), pltpu.SemaphoreType.DMA((NDEV,)),
                   pltpu.HBM(src.shape, src.dtype), pltpu.HBM(land.shape, land.dtype),
                   jax.ShapeDtypeStruct((8, HEAD), F32)),
        in_specs=(_HBM, _HBM, _ANY), out_specs=(_SEM, _SEM, _HBM, _HBM, pl.BlockSpec(memory_space=pltpu.VMEM)),
        input_output_aliases={0: 2, 1: 3},
        compiler_params=pltpu.CompilerParams(has_side_effects=_EFFECT),
    )(pltpu.with_memory_space_constraint(src, pltpu.HBM), pltpu.with_memory_space_constraint(land, pltpu.HBM), after)
    return res[:4], res[4]


def _exchange_wait(handle, mode, after, name):
    send_sems, recv_sems, src_thru, land_thru = handle

    def body(src_ref, land_ref, send_sems, recv_sems, after_ref, src_dead, got_ref):
        copies, own = _exchange_copies(src_ref, land_ref, send_sems, recv_sems, mode)
        for cp in copies:
            cp.wait_send()
            cp.wait_recv()
        own.wait()

    return _pallas(
        body, name=name,
        out_shape=(pltpu.HBM(src_thru.shape, src_thru.dtype), pltpu.HBM(land_thru.shape, land_thru.dtype)),
        in_specs=(_HBM, _HBM, _SEM, _SEM, _ANY), out_specs=(_HBM, _HBM),
        input_output_aliases={0: 0, 1: 1},
        compiler_params=pltpu.CompilerParams(has_side_effects=_EFFECT),
    )(src_thru, land_thru, send_sems, recv_sems, after)[1]


def _ada_fwd(c_all, w, bias):
    r, d = c_all.shape
    e = w.shape[1]
    tn = 512

    def body(c_ref, w_ref, b_ref, o_ref):
        cv = c_ref[...]
        s = (cv * _sigmoid(cv)).astype(BF16)
        o_ref[...] = jnp.dot(s, w_ref[...].astype(BF16), preferred_element_type=F32) + b_ref[...]

    return _pallas(
        body, name="ada_fwd", grid=(e // tn,),
        out_shape=jax.ShapeDtypeStruct((r, e), F32),
        in_specs=[pl.BlockSpec((r, d), lambda j: (0, 0)), pl.BlockSpec((d, tn), lambda j: (0, j)),
                  pl.BlockSpec((1, tn), lambda j: (0, j))],
        out_specs=pl.BlockSpec((r, tn), lambda j: (0, j)),
        compiler_params=_params(_mb(24)),
    )(c_all, w, bias)


def _ada_bwd(dm16, c_all, w):
    d, e = w.shape
    tn = 512

    def body(dm_ref, c_ref, w_ref, dw_ref, dr_ref):
        j = pl.program_id(0)
        dm = dm_ref[...]
        rid = lax.broadcasted_iota(jnp.int32, dm.shape, 0)
        ctx_sum = jnp.sum(jnp.where(rid >= 8, dm, 0.0), axis=0, keepdims=True)
        rows = jnp.where(rid < 8, dm, jnp.where(rid == 8, jnp.broadcast_to(ctx_sum, dm.shape), 0.0)).astype(BF16)
        cv = c_ref[...]
        s = (cv * _sigmoid(cv)).astype(BF16)
        dw_ref[...] = lax.dot_general(s, rows, _TN, preferred_element_type=F32)
        part = lax.dot_general(rows, w_ref[...].astype(BF16), _NT, preferred_element_type=F32)

        @pl.when(j == 0)
        def _():
            dr_ref[...] = part

        @pl.when(j > 0)
        def _():
            dr_ref[...] += part

    return _pallas(
        body, name="ada_bwd", grid=(e // tn,),
        out_shape=(jax.ShapeDtypeStruct((d, e), F32), jax.ShapeDtypeStruct((16, d), F32)),
        in_specs=[pl.BlockSpec((16, tn), lambda j: (0, j)), pl.BlockSpec((16, d), lambda j: (0, 0)),
                  pl.BlockSpec((d, tn), lambda j: (0, j))],
        out_specs=(pl.BlockSpec((d, tn), lambda j: (0, j)), pl.BlockSpec((16, d), lambda j: (0, 0))),
        compiler_params=_params(_mb(32)),
    )(dm16, c_all, w)


def _rope(v, cos, sa, sb):
    return v * cos + (pltpu.roll(v, 96, 1) * sa + pltpu.roll(v, 32, 1) * sb)


def _rope_t(dt, cos, sa, sb):
    return dt * cos + (pltpu.roll(dt * sa, 32, 1) + pltpu.roll(dt * sb, 96, 1))


def _qkv_fwd(x, ct, sc, sh, wint, qg, kg, cos, sa, sb):
    n, d = x.shape
    tm = CTX
    nlat = n // tm
    na = n + CTX
    wcols = wint.shape[0]

    def body(x_ref, ct_ref, sc_ref, sh_ref, w_ref, qg_ref, kg_ref, cos_ref, sa_ref, sb_ref, u_ref, h_ref, t_ref, kt_ref):
        i = pl.program_id(0)
        xin = jnp.where(i == nlat, ct_ref[...], x_ref[...])
        u = (xin * (1.0 + sc_ref[0]) + sh_ref[0]).astype(BF16)
        u_ref[...] = u
        cos, sa, sb = cos_ref[...], sa_ref[...], sb_ref[...]
        h = lax.dot_general(u, w_ref[...], _NT, preferred_element_type=F32)
        h_ref[...] = h
        for hd in range(24):
            v = h[:, hd * HEAD:(hd + 1) * HEAD]
            kind = _KINDS[hd]
            if kind == "qnorm":
                v = v * lax.rsqrt(_rowmean(v * v) + EPS) * qg_ref[...]
            elif kind == "knorm":
                v = v * lax.rsqrt(_rowmean(v * v) + EPS) * kg_ref[...]
            if kind != "none":
                v = _rope(v, cos, sa, sb)
            t_ref[:, hd * HEAD:(hd + 1) * HEAD] = v.astype(BF16)
            if kind == "knorm":
                kt_ref[(hd - 20) * HEAD:(hd - 19) * HEAD, :] = v.T.astype(BF16)

    lat = lambda i: (jnp.minimum(i, nlat - 1), 0)
    row = lambda i: (i, 0)
    const2 = lambda i: (0, 0)
    return _pallas(
        body, name="qkv_fwd", grid=(nlat + 1,),
        out_shape=(jax.ShapeDtypeStruct((na, d), BF16), jax.ShapeDtypeStruct((na, wcols), F32),
                   jax.ShapeDtypeStruct((na, wcols), BF16), jax.ShapeDtypeStruct((2 * HEAD, na), BF16)),
        in_specs=[pl.BlockSpec((tm, d), lat), pl.BlockSpec((tm, d), const2),
                  pl.BlockSpec((1, 1, d), lambda i: (i // nlat, 0, 0)),
                  pl.BlockSpec((1, 1, d), lambda i: (i // nlat, 0, 0)),
                  pl.BlockSpec((wcols, d), const2),
                  pl.BlockSpec((1, HEAD), const2), pl.BlockSpec((1, HEAD), const2),
                  pl.BlockSpec((tm, HEAD), row), pl.BlockSpec((tm, HEAD), row), pl.BlockSpec((tm, HEAD), row)],
        out_specs=(pl.BlockSpec((tm, d), row), pl.BlockSpec((tm, wcols), row), pl.BlockSpec((tm, wcols), row),
                   pl.BlockSpec((2 * HEAD, tm), lambda i: (0, i))),
        compiler_params=_params(_mb(56)),
    )(x, ct, sc, sh, wint, qg, kg, cos, sa, sb)


def _qkv_bwd_prep(dqa, dka, dva, dqb, dkb, dvb, h_all, qg, kg, cos, sa, sb):
    na, wcols = h_all.shape
    n = na - CTX
    tm = CTX
    nlat = n // tm

    def body(dqa_ref, dka_ref, dva_ref, dqb_ref, dkb_ref, dvb_ref, h_ref, qg_ref, kg_ref, cos_ref, sa_ref, sb_ref,
             dh_ref, dg_ref):
        i = pl.program_id(0)

        @pl.when(i == 0)
        def _():
            dg_ref[...] = jnp.zeros_like(dg_ref)

        cos, sa, sb = cos_ref[...], sa_ref[...], sb_ref[...]
        is_lat = i < nlat
        for hd in range(24):
            kind = _KINDS[hd]
            if hd < 8:
                dt = jnp.where(is_lat, dqa_ref[:, hd * HEAD:(hd + 1) * HEAD], 0.0)
            elif hd < 10:
                dt = dka_ref[:, (hd - 8) * HEAD:(hd - 7) * HEAD]
            elif hd < 12:
                dt = dva_ref[:, (hd - 10) * HEAD:(hd - 9) * HEAD]
            elif hd < 20:
                dt = jnp.where(is_lat, dqb_ref[:, (hd - 12) * HEAD:(hd - 11) * HEAD], 0.0)
            elif hd < 22:
                dt = dkb_ref[:, (hd - 20) * HEAD:(hd - 19) * HEAD]
            else:
                dt = dvb_ref[:, (hd - 22) * HEAD:(hd - 21) * HEAD]
            if kind != "none":
                dt = _rope_t(dt, cos, sa, sb)
            if kind in ("qnorm", "knorm"):
                g_ref = qg_ref if kind == "qnorm" else kg_ref
                r0 = 0 if kind == "qnorm" else 1
                xv = h_ref[:, hd * HEAD:(hd + 1) * HEAD]
                xn = xv * lax.rsqrt(_rowmean(xv * xv) + EPS)
                dg_ref[r0:r0 + 1, :] += _colsum(dt * xn)
                dxn = dt * g_ref[...]
                dt = lax.rsqrt(_rowmean(xv * xv) + EPS) * (dxn - xn * _rowmean(dxn * xn))
            dh_ref[:, hd * HEAD:(hd + 1) * HEAD] = dt.astype(BF16)

    lat = lambda i: (jnp.minimum(i, nlat - 1), 0)
    row = lambda i: (i, 0)
    const2 = lambda i: (0, 0)
    return _pallas(
        body, name="qkv_bwd_prep", grid=(nlat + 1,),
        out_shape=(jax.ShapeDtypeStruct((na, wcols), BF16), jax.ShapeDtypeStruct((8, HEAD), F32)),
        in_specs=[pl.BlockSpec((tm, 8 * HEAD), lat), pl.BlockSpec((tm, 2 * HEAD), row), pl.BlockSpec((tm, 2 * HEAD), row),
                  pl.BlockSpec((tm, 8 * HEAD), lat), pl.BlockSpec((tm, 2 * HEAD), row), pl.BlockSpec((tm, 2 * HEAD), row),
                  pl.BlockSpec((tm, wcols), row),
                  pl.BlockSpec((1, HEAD), const2), pl.BlockSpec((1, HEAD), const2),
                  pl.BlockSpec((tm, HEAD), row), pl.BlockSpec((tm, HEAD), row), pl.BlockSpec((tm, HEAD), row)],
        out_specs=(pl.BlockSpec((tm, wcols), row), pl.BlockSpec((8, HEAD), const2)),
        compiler_params=_params(_mb(40)),
    )(dqa, dka, dva, dqb, dkb, dvb, h_all, qg, kg, cos, sa, sb)


def _window_keys(k_ref, v_ref, n, na):
    i = pl.program_id(1)
    tq = WINDOW
    start = pl.multiple_of(jnp.clip((i - 1) * tq, 0, n - 3 * tq), tq)
    kk = jnp.concatenate([k_ref[pl.ds(start, 3 * tq), :], k_ref[n:na, :]], axis=0)
    vv = jnp.concatenate([v_ref[pl.ds(start, 3 * tq), :], v_ref[n:na, :]], axis=0)
    return kk, vv, start


def _window_bias():
    tq = WINDOW
    r = (jnp.arange(4 * tq) % tq)[:, None]
    c = jnp.arange(3 * tq + CTX)[None, :]
    variants = []
    for back in (0, tq, 2 * tq):
        seen = (jnp.abs(back + r - c) <= WINDOW) | (c >= 3 * tq)
        variants.append(jnp.where(seen, 0.0, NEG).astype(F32))
    return jnp.stack(variants)


def _window_bias_spec(nq):
    return pl.BlockSpec((1, 4 * WINDOW, 3 * WINDOW + CTX),
                        lambda kv, i: (jnp.where(i == 0, 0, jnp.where(i == nq - 1, 2, 1)), 0, 0))


def _stack_heads(ref, width=HEAD):
    return jnp.concatenate([ref[:, g * HEAD:g * HEAD + width] for g in range(4)], axis=0)


def _sink_column(sink_ref, kv, tq):
    grp = lax.broadcasted_iota(jnp.int32, (4 * tq, 1), 0) // tq
    col = jnp.zeros((4 * tq, 1), F32)
    for g in range(4):
        col = jnp.where(grp == g, sink_ref[0, 4 * kv + g] * LOG2E, col)
    return col


def _attn_window_fwd(t_all, sink, bias, after):
    na = t_all.shape[0]
    n = na - CTX
    tq = WINDOW

    def body(sink_ref, q_ref, k_ref, v_ref, bias_ref, after_ref, o_ref, lse_ref):
        kv = pl.program_id(0)
        kk, vv, _ = _window_keys(k_ref, v_ref, n, na)
        t = lax.dot_general(_stack_heads(q_ref), kk, _NT, preferred_element_type=F32) * QK_LOG2 + bias_ref[0]
        sk = _sink_column(sink_ref, kv, tq)
        m = jnp.maximum(jnp.max(t, axis=-1, keepdims=True), sk)
        p = jnp.exp2(t - m)
        l = jnp.sum(p, axis=-1, keepdims=True) + jnp.exp2(sk - m)
        o = jnp.dot(p.astype(BF16), vv, preferred_element_type=F32) * (1.0 / l)
        lse = m + jnp.log2(l)
        for g in range(4):
            o_ref[:, g * HEAD:(g + 1) * HEAD] = o[g * tq:(g + 1) * tq]
            lse_ref[:, g * HEAD:(g + 1) * HEAD] = jnp.broadcast_to(lse[g * tq:(g + 1) * tq], (tq, HEAD))

    blk = pl.BlockSpec((tq, 4 * HEAD), lambda kv, i: (i, kv))
    return _pallas(
        body, name="attn_window_fwd", grid=(2, n // tq),
        out_shape=(jax.ShapeDtypeStruct((n, 16 * HEAD), F32), jax.ShapeDtypeStruct((n, 8 * HEAD), F32)),
        in_specs=[pl.BlockSpec(memory_space=pltpu.SMEM), blk,
                  pl.BlockSpec((na, HEAD), lambda kv, i: (0, 8 + kv)),
                  pl.BlockSpec((na, HEAD), lambda kv, i: (0, 10 + kv)), _window_bias_spec(n // tq), _ANY],
        out_specs=(blk, blk),
        compiler_params=_params(_mb(32)),
    )(sink, t_all, t_all, t_all, bias, after)


def _attn_global_fwd(t_all, o_part):
    na = t_all.shape[0]
    n = na - CTX
    tq = 256

    def body(q_ref, k_ref, v_ref, o_in_ref, o_ref, lse_ref):
        kk, vv = k_ref[...], v_ref[...]
        for g in range(4):
            q = q_ref[:, g * HEAD:(g + 1) * HEAD]
            t = lax.dot_general(q, kk, _NT, preferred_element_type=F32) * QK_LOG2
            m = jnp.max(t, axis=-1, keepdims=True)
            p = jnp.exp2(t - m)
            l = jnp.sum(p, axis=-1, keepdims=True)
            o_ref[:, g * HEAD:(g + 1) * HEAD] = jnp.dot(p.astype(BF16), vv, preferred_element_type=F32) * (1.0 / l)
            lse_ref[:, g * HEAD:(g + 1) * HEAD] = jnp.broadcast_to(m + jnp.log2(l), (tq, HEAD))

    return _pallas(
        body, name="attn_global_fwd", grid=(2, n // tq),
        out_shape=(jax.ShapeDtypeStruct((n, 16 * HEAD), F32), jax.ShapeDtypeStruct((n, 8 * HEAD), F32)),
        in_specs=[pl.BlockSpec((tq, 4 * HEAD), lambda kv, i: (i, 3 + kv)),
                  pl.BlockSpec((na, HEAD), lambda kv, i: (0, 20 + kv)),
                  pl.BlockSpec((na, HEAD), lambda kv, i: (0, 22 + kv)), _ANY],
        out_specs=(pl.BlockSpec((tq, 4 * HEAD), lambda kv, i: (i, 2 + kv)),
                   pl.BlockSpec((tq, 4 * HEAD), lambda kv, i: (i, kv))),
        input_output_aliases={3: 0},
        compiler_params=_params(_mb(48)),
    )(t_all, t_all, t_all, o_part)


def _attn_window_bwd(t_all, o, do, lse, sink, bias):
    na = t_all.shape[0]
    n = na - CTX
    tq = WINDOW

    def body(sink_ref, q_ref, k_ref, v_ref, o_ref, do_ref, lse_ref, bias_ref, dq_ref, dk_ref, dv_ref, dsink_ref):
        kv = pl.program_id(0)

        @pl.when(pl.program_id(1) == 0)
        def _():
            dk_ref[...] = jnp.zeros_like(dk_ref)
            dv_ref[...] = jnp.zeros_like(dv_ref)
            dsink_ref[...] = jnp.zeros_like(dsink_ref)

        kk, vv, start = _window_keys(k_ref, v_ref, n, na)
        q = _stack_heads(q_ref)
        t = lax.dot_general(q, kk, _NT, preferred_element_type=F32) * QK_LOG2 + bias_ref[0]
        lse = _stack_heads(lse_ref, 1)
        p = jnp.exp2(t - lse)
        dof = _stack_heads(do_ref)
        delta = jnp.sum(dof * _stack_heads(o_ref), axis=-1, keepdims=True)
        dob = dof.astype(BF16)
        dv_acc = lax.dot_general(p.astype(BF16), dob, _TN, preferred_element_type=F32)
        dp = lax.dot_general(dob, vv, _NT, preferred_element_type=F32)
        ds = (p * (dp - delta) * SCALE).astype(BF16)
        dq = jnp.dot(ds, kk, preferred_element_type=F32)
        dk_acc = lax.dot_general(ds, q, _TN, preferred_element_type=F32)
        dsk = -(jnp.exp2(_sink_column(sink_ref, kv, tq) - lse) * delta)
        for g in range(4):
            dq_ref[:, g * HEAD:(g + 1) * HEAD] = dq[g * tq:(g + 1) * tq]
            dsink_ref[0, g:g + 1, :] += jnp.broadcast_to(_colsum(dsk[g * tq:(g + 1) * tq]), (1, HEAD))
        dk_ref[pl.ds(start, 3 * tq), :] += dk_acc[:3 * tq]
        dv_ref[pl.ds(start, 3 * tq), :] += dv_acc[:3 * tq]
        dk_ref[n:na, :] += dk_acc[3 * tq:]
        dv_ref[n:na, :] += dv_acc[3 * tq:]

    blk = pl.BlockSpec((tq, 4 * HEAD), lambda kv, i: (i, kv))
    kvout = pl.BlockSpec((na, HEAD), lambda kv, i: (0, kv))
    return _pallas(
        body, name="attn_window_bwd", grid=(2, n // tq),
        out_shape=(jax.ShapeDtypeStruct((n, 8 * HEAD), F32), jax.ShapeDtypeStruct((na, 2 * HEAD), F32),
                   jax.ShapeDtypeStruct((na, 2 * HEAD), F32), jax.ShapeDtypeStruct((2, 8, HEAD), F32)),
        in_specs=[pl.BlockSpec(memory_space=pltpu.SMEM), blk,
                  pl.BlockSpec((na, HEAD), lambda kv, i: (0, 8 + kv)),
                  pl.BlockSpec((na, HEAD), lambda kv, i: (0, 10 + kv)),
                  blk, blk, blk, _window_bias_spec(n // tq)],
        out_specs=(blk, kvout, kvout, pl.BlockSpec((1, 8, HEAD), lambda kv, i: (kv, 0, 0))),
        compiler_params=_params(_mb(40)),
    )(sink, t_all, t_all, t_all, o, do, lse, bias)


def _attn_global_bwd(t_all, kt, o, do, lse):
    na = t_all.shape[0]
    n = na - CTX
    tq = 256

    def body(q_ref, k_ref, v_ref, kt_ref, o_ref, do_ref, lse_ref, dq_ref, dk_ref, dv_ref, dkt_acc, dvt_acc):
        i = pl.program_id(1)

        @pl.when(i == 0)
        def _():
            dkt_acc[...] = jnp.zeros_like(dkt_acc)
            dvt_acc[...] = jnp.zeros_like(dvt_acc)

        kk, vv, kt_v = k_ref[...], v_ref[...], kt_ref[...]
        dkt = jnp.zeros((HEAD, na), F32)
        dvt = jnp.zeros((HEAD, na), F32)
        for g in range(4):
            q = q_ref[:, g * HEAD:(g + 1) * HEAD]
            t = lax.dot_general(q, kk, _NT, preferred_element_type=F32) * QK_LOG2
            p = jnp.exp2(t - lse_ref[:, g * HEAD:g * HEAD + 1])
            dof = do_ref[:, g * HEAD:(g + 1) * HEAD]
            delta = jnp.sum(dof * o_ref[:, g * HEAD:(g + 1) * HEAD], axis=-1, keepdims=True)
            dob = dof.astype(BF16)
            dvt = dvt + lax.dot_general(dob, p.astype(BF16), _TN, preferred_element_type=F32)
            dp = lax.dot_general(dob, vv, _NT, preferred_element_type=F32)
            ds = (p * (dp - delta) * SCALE).astype(BF16)
            dq_ref[:, g * HEAD:(g + 1) * HEAD] = lax.dot_general(kt_v, ds, _NT, preferred_element_type=F32).T
            dkt = dkt + lax.dot_general(q, ds, _TN, preferred_element_type=F32)
        dkt_acc[...] += dkt
        dvt_acc[...] += dvt

        @pl.when(i == pl.num_programs(1) - 1)
        def _():
            dk_ref[...] = dkt_acc[...].T
            dv_ref[...] = dvt_acc[...].T

    ospec = pl.BlockSpec((tq, 4 * HEAD), lambda kv, i: (i, 2 + kv))
    lspec = pl.BlockSpec((tq, 4 * HEAD), lambda kv, i: (i, kv))
    kvout = pl.BlockSpec((na, HEAD), lambda kv, i: (0, kv))
    return _pallas(
        body, name="attn_global_bwd", grid=(2, n // tq),
        out_shape=(jax.ShapeDtypeStruct((n, 8 * HEAD), F32), jax.ShapeDtypeStruct((na, 2 * HEAD), F32),
                   jax.ShapeDtypeStruct((na, 2 * HEAD), F32)),
        in_specs=[pl.BlockSpec((tq, 4 * HEAD), lambda kv, i: (i, 3 + kv)),
                  pl.BlockSpec((na, HEAD), lambda kv, i: (0, 20 + kv)),
                  pl.BlockSpec((na, HEAD), lambda kv, i: (0, 22 + kv)),
                  pl.BlockSpec((HEAD, na), lambda kv, i: (kv, 0)),
                  ospec, ospec, lspec],
        out_specs=(lspec, kvout, kvout),
        scratch_shapes=[pltpu.VMEM((HEAD, na), F32), pltpu.VMEM((HEAD, na), F32)],
        compiler_params=_params(_mb(56)),
    )(t_all, t_all, t_all, kt, o, do, lse)


def _outproj_ln1(o, wout, x, g1, lg, lb, sc2, sh2, after):
    n, d = x.shape
    tm = 256

    def body(o_ref, w_ref, x_ref, g1_ref, lg_ref, lb_ref, sc_ref, sh_ref, after_ref, a_ref, xh_ref, rs_ref, u_ref):
        a1 = jnp.dot(o_ref[...].astype(BF16), w_ref[...], preferred_element_type=F32)
        a_ref[...] = a1
        r = ALPHA * x_ref[...] + g1_ref[...] * a1
        dlt = r - _rowmean(r)
        rstd = lax.rsqrt(_rowmean(dlt * dlt) + EPS)
        xh = dlt * rstd
        xh_ref[...] = xh
        rs_ref[...] = rstd
        x1 = xh * lg_ref[...] + lb_ref[...]
        u_ref[...] = (x1 * (1.0 + sc_ref[...]) + sh_ref[...]).astype(BF16)

    row = lambda i: (i, 0)
    const2 = lambda i: (0, 0)
    vec = pl.BlockSpec((1, d), const2)
    big = pl.BlockSpec((tm, d), row)
    return _pallas(
        body, name="outproj_ln1", grid=(n // tm,),
        out_shape=(jax.ShapeDtypeStruct((n, d), F32), jax.ShapeDtypeStruct((n, d), F32),
                   jax.ShapeDtypeStruct((n, 1), F32), jax.ShapeDtypeStruct((n, d), BF16)),
        in_specs=[big, pl.BlockSpec((d, d), const2), big, vec, vec, vec, vec, vec, _ANY],
        out_specs=(big, big, pl.BlockSpec((tm, 1), row), big),
        compiler_params=_params(_mb(56)),
    )(o, wout, x, g1, lg, lb, sc2, sh2, after)


def _ffn_up(u2, wgt, wut, after):
    n, d = u2.shape
    f = wgt.shape[0]
    tm = min(1024, n)

    def body(u_ref, wg_ref, wu_ref, after_ref, sa_ref, sb_ref, hf_ref):
        u = u_ref[...]
        gv = lax.dot_general(u, wg_ref[...], _NT, preferred_element_type=F32)
        pv = lax.dot_general(u, wu_ref[...], _NT, preferred_element_type=F32)
        sg = _sigmoid(gv)
        silu = gv * sg
        sa_ref[...] = silu.astype(BF16)
        sb_ref[...] = (pv * (sg * (1.0 + gv * (1.0 - sg)))).astype(BF16)
        hf_ref[...] = (silu * pv).astype(BF16)

    tile = pl.BlockSpec((tm, FFN_TILE), lambda i, j: (i, j))
    wspec = pl.BlockSpec((FFN_TILE, d), lambda i, j: (j, 0))
    sds = jax.ShapeDtypeStruct((n, f), BF16)
    return _pallas(
        body, name="ffn_up", grid=(n // tm, f // FFN_TILE),
        out_shape=(sds, sds, sds),
        in_specs=[pl.BlockSpec((tm, d), lambda i, j: (i, 0)), wspec, wspec, _ANY],
        out_specs=(tile, tile, tile),
        compiler_params=_params(_mb(48)),
    )(u2, wgt, wut, after)


def _ffn_down(hf, wd):
    n, f = hf.shape
    d = wd.shape[1]
    tm, tn = min(1024, n), 512

    def body(h_ref, w_ref, o_ref):
        o_ref[...] = jnp.dot(h_ref[...], w_ref[...], preferred_element_type=F32)

    return _pallas(
        body, name="ffn_down", grid=(n // tm, d // tn),
        out_shape=jax.ShapeDtypeStruct((n, d), F32),
        in_specs=[pl.BlockSpec((tm, f), lambda i, j: (i, 0)), pl.BlockSpec((f, tn), lambda i, j: (0, j))],
        out_specs=pl.BlockSpec((tm, tn), lambda i, j: (i, j)),
        compiler_params=_params(_mb(56)),
    )(hf, wd)


def _ln2_loss(xh1, ffn, tgt, lg1, lb1, g2, lg2, lb2):
    n, d = xh1.shape
    tm = 256

    def body(xh_ref, f_ref, t_ref, lg1_ref, lb1_ref, g2_ref, lg2_ref, lb2_ref, dr_ref, df_ref, loss_ref, acc_ref):
        @pl.when(pl.program_id(0) == 0)
        def _():
            loss_ref[...] = jnp.zeros_like(loss_ref)
            acc_ref[...] = jnp.zeros_like(acc_ref)

        x1 = xh_ref[...] * lg1_ref[...] + lb1_ref[...]
        fv = f_ref[...]
        r = ALPHA * x1 + g2_ref[...] * fv
        dlt = r - _rowmean(r)
        rstd = lax.rsqrt(_rowmean(dlt * dlt) + EPS)
        xh2 = dlt * rstd
        err = xh2 * lg2_ref[...] + lb2_ref[...] - t_ref[...]
        loss_ref[...] += 0.5 * jnp.sum(_rowmean(err * err))
        dy = err * (1.0 / d)
        dyg = dy * lg2_ref[...]
        dr = rstd * (dyg - _rowmean(dyg) - xh2 * _rowmean(dyg * xh2))
        dr_ref[...] = dr
        df_ref[...] = (g2_ref[...] * dr).astype(BF16)
        acc_ref[0:1, :] += _colsum(dy * xh2)
        acc_ref[1:2, :] += _colsum(dy)
        acc_ref[2:3, :] += _colsum(dr * fv)

    row = lambda i: (i, 0)
    const2 = lambda i: (0, 0)
    vec = pl.BlockSpec((1, d), const2)
    big = pl.BlockSpec((tm, d), row)
    return _pallas(
        body, name="ln2_loss", grid=(n // tm,),
        out_shape=(jax.ShapeDtypeStruct((n, d), F32), jax.ShapeDtypeStruct((n, d), BF16),
                   jax.ShapeDtypeStruct((8, HEAD), F32), jax.ShapeDtypeStruct((8, d), F32)),
        in_specs=[big, big, big, vec, vec, vec, vec, vec],
        out_specs=(big, big, pl.BlockSpec((8, HEAD), const2), pl.BlockSpec((8, d), const2)),
        compiler_params=_params(_mb(48)),
    )(xh1, ffn, tgt, lg1, lb1, g2, lg2, lb2)


def _ffn_dhf(df, wd, sa, sb):
    n, d = df.shape
    f = sa.shape[1]
    tm = min(1024, n)

    def body(df_ref, w_ref, sa_ref, sb_ref, dgp_ref):
        dhf = lax.dot_general(df_ref[...], w_ref[...], _NT, preferred_element_type=F32)
        dgp_ref[:, :FFN_TILE] = (dhf * sb_ref[...].astype(F32)).astype(BF16)
        dgp_ref[:, FFN_TILE:] = (dhf * sa_ref[...].astype(F32)).astype(BF16)

    tile = pl.BlockSpec((tm, FFN_TILE), lambda i, j: (i, j))
    return _pallas(
        body, name="ffn_dhf", grid=(n // tm, f // FFN_TILE),
        out_shape=jax.ShapeDtypeStruct((n, 2 * f), BF16),
        in_specs=[pl.BlockSpec((tm, d), lambda i, j: (i, 0)), pl.BlockSpec((FFN_TILE, d), lambda i, j: (j, 0)),
                  tile, tile],
        out_specs=pl.BlockSpec((tm, 2 * FFN_TILE), lambda i, j: (i, j)),
        compiler_params=_params(_mb(48)),
    )(df, wd, sa, sb)


def _ffn_du2(dgp, wgt, wut, after):
    n = dgp.shape[0]
    f, d = wgt.shape
    tm = min(1024, n)

    def body(dgp_ref, wg_ref, wu_ref, after_ref, o_ref):
        w = jnp.concatenate([wg_ref[...], wu_ref[...]], axis=0)
        part = jnp.dot(dgp_ref[...], w, preferred_element_type=F32)

        @pl.when(pl.program_id(1) == 0)
        def _():
            o_ref[...] = part

        @pl.when(pl.program_id(1) > 0)
        def _():
            o_ref[...] += part

    wspec = pl.BlockSpec((FFN_TILE, d), lambda i, j: (j, 0))
    return _pallas(
        body, name="ffn_du2", grid=(n // tm, f // FFN_TILE),
        out_shape=jax.ShapeDtypeStruct((n, d), F32),
        in_specs=[pl.BlockSpec((tm, 2 * FFN_TILE), lambda i, j: (i, j)), wspec, wspec, _ANY],
        out_specs=pl.BlockSpec((tm, d), lambda i, j: (i, 0)),
        compiler_params=_params(_mb(48)),
    )(dgp, wgt, wut, after)


def _dw_gate_up(dgp, u2, after):
    n, d = u2.shape
    f = dgp.shape[1] // 2
    tm = min(1024, n)

    def body(a_ref, b_ref, after_ref, og_ref, ou_ref, acc_ref):
        part = lax.dot_general(a_ref[...], b_ref[...], _TN, preferred_element_type=F32)
        i = pl.program_id(1)

        @pl.when(i == 0)
        def _():
            acc_ref[...] = part

        @pl.when(i > 0)
        def _():
            acc_ref[...] += part

        @pl.when(i == pl.num_programs(1) - 1)
        def _():
            og_ref[...] = acc_ref[:FFN_TILE].astype(BF16)
            ou_ref[...] = acc_ref[FFN_TILE:].astype(BF16)

    out = pl.BlockSpec((FFN_TILE, d), lambda j, i: (j, 0))
    sds = jax.ShapeDtypeStruct((f, d), BF16)
    return _pallas(
        body, name="dw_gate_up", grid=(f // FFN_TILE, n // tm),
        out_shape=(sds, sds),
        in_specs=[pl.BlockSpec((tm, 2 * FFN_TILE), lambda j, i: (i, j)), pl.BlockSpec((tm, d), lambda j, i: (i, 0)), _ANY],
        out_specs=(out, out),
        scratch_shapes=[pltpu.VMEM((2 * FFN_TILE, d), F32)],
        compiler_params=_params(_mb(56)),
    )(dgp, u2, after)


def _ln1_bwd(du2, dr2, xh1, rs1, a1, lg1, lb1, sc2, g1):
    n, d = du2.shape
    tm = 256

    def body(du_ref, dr2_ref, xh_ref, rs_ref, a_ref, lg_ref, lb_ref, sc_ref, g1_ref, dr1_ref, da_ref, acc_ref):
        @pl.when(pl.program_id(0) == 0)
        def _():
            acc_ref[...] = jnp.zeros_like(acc_ref)

        du = du_ref[...]
        xh = xh_ref[...]
        x1 = xh * lg_ref[...] + lb_ref[...]
        dx1 = ALPHA * dr2_ref[...] + du * (1.0 + sc_ref[...])
        dxg = dx1 * lg_ref[...]
        dr1 = rs_ref[...] * (dxg - _rowmean(dxg) - xh * _rowmean(dxg * xh))
        dr1_ref[...] = dr1
        da_ref[...] = (g1_ref[...] * dr1).astype(BF16)
        acc_ref[0:1, :] += _colsum(du * x1)
        acc_ref[1:2, :] += _colsum(du)
        acc_ref[2:3, :] += _colsum(dx1 * xh)
        acc_ref[3:4, :] += _colsum(dx1)
        acc_ref[4:5, :] += _colsum(dr1 * a_ref[...])

    row = lambda i: (i, 0)
    const2 = lambda i: (0, 0)
    vec = pl.BlockSpec((1, d), const2)
    big = pl.BlockSpec((tm, d), row)
    return _pallas(
        body, name="ln1_bwd", grid=(n // tm,),
        out_shape=(jax.ShapeDtypeStruct((n, d), F32), jax.ShapeDtypeStruct((n, d), BF16),
                   jax.ShapeDtypeStruct((8, d), F32)),
        in_specs=[big, big, big, pl.BlockSpec((tm, 1), row), big, vec, vec, vec, vec],
        out_specs=(big, big, pl.BlockSpec((8, d), const2)),
        compiler_params=_params(_mb(48)),
    )(du2, dr2, xh1, rs1, a1, lg1, lb1, sc2, g1)


def _dw_rows(a, b, nblk, bw, tm, after, name):
    m = a.shape[0]
    nn = b.shape[1]

    def body(a_ref, b_ref, after_ref, o_ref, acc_ref):
        part = lax.dot_general(a_ref[...].astype(BF16), b_ref[...], _TN, preferred_element_type=F32)
        i = pl.program_id(1)

        @pl.when(i == 0)
        def _():
            acc_ref[...] = part

        @pl.when(i > 0)
        def _():
            acc_ref[...] += part

        @pl.when(i == pl.num_programs(1) - 1)
        def _():
            o_ref[0] = acc_ref[...].astype(BF16)

    return _pallas(
        body, name=name, grid=(nblk, m // tm),
        out_shape=jax.ShapeDtypeStruct((nblk, bw, nn), BF16),
        in_specs=[pl.BlockSpec((tm, bw), lambda j, i: (i, j)), pl.BlockSpec((tm, nn), lambda j, i: (i, 0)), _ANY],
        out_specs=pl.BlockSpec((1, bw, nn), lambda j, i: (j, 0, 0)),
        scratch_shapes=[pltpu.VMEM((bw, nn), F32)],
        compiler_params=_params(_mb(56)),
    )(a, b, after)


def _outproj_bwd(da1, wout, after):
    n, d = da1.shape
    tm = 512

    def body(a_ref, w_ref, after_ref, o_ref):
        o_ref[...] = lax.dot_general(a_ref[...], w_ref[...], _NT, preferred_element_type=F32)

    return _pallas(
        body, name="outproj_bwd", grid=(n // tm,),
        out_shape=jax.ShapeDtypeStruct((n, d), F32),
        in_specs=[pl.BlockSpec((tm, d), lambda i: (i, 0)), pl.BlockSpec((d, d), lambda i: (0, 0)), _ANY],
        out_specs=pl.BlockSpec((tm, d), lambda i: (i, 0)),
        compiler_params=_params(_mb(48)),
    )(da1, wout, after)


def _qkv_bwd(dh, wint, x, ct, dr1, sc):
    na, wcols = dh.shape
    n, d = x.shape
    tm = CTX
    nlat = n // tm

    def body(dh_ref, w_ref, x_ref, ct_ref, dr_ref, sc_ref, gx_ref, acc_ref):
        i = pl.program_id(0)

        @pl.when(i == 0)
        def _():
            acc_ref[...] = jnp.zeros_like(acc_ref)

        du = jnp.dot(dh_ref[...], w_ref[...], preferred_element_type=F32)

        @pl.when(i < nlat)
        def _():
            gx_ref[...] = ALPHA * dr_ref[...] + du * (1.0 + sc_ref[0])
            acc_ref[0:1, :] += _colsum(du)
            acc_ref[1:2, :] += _colsum(du * x_ref[...])

        @pl.when(i == nlat)
        def _():
            acc_ref[2:3, :] += _colsum(du)
            acc_ref[3:4, :] += _colsum(du * ct_ref[...])

    lat = lambda i: (jnp.minimum(i, nlat - 1), 0)
    const2 = lambda i: (0, 0)
    return _pallas(
        body, name="qkv_bwd", grid=(nlat + 1,),
        out_shape=(jax.ShapeDtypeStruct((n, d), F32), jax.ShapeDtypeStruct((8, d), F32)),
        in_specs=[pl.BlockSpec((tm, wcols), lambda i: (i, 0)), pl.BlockSpec((wcols, d), const2),
                  pl.BlockSpec((tm, d), lat), pl.BlockSpec((tm, d), const2), pl.BlockSpec((tm, d), lat),
                  pl.BlockSpec((1, 1, d), lambda i: (0, 0, 0))],
        out_specs=(pl.BlockSpec((tm, d), lat), pl.BlockSpec((8, d), const2)),
        compiler_params=_params(_mb(56)),
    )(dh, wint, x, ct, dr1, sc)


def _adam_math(w, g, m, v):
    m2 = ADAM_B1 * m + (1.0 - ADAM_B1) * g
    v2 = ADAM_B2 * v + (1.0 - ADAM_B2) * (g * g)
    m_hat = m2 * (1.0 / (1.0 - ADAM_B1 ** ADAM_STEP))
    v_hat = v2 * (1.0 / (1.0 - ADAM_B2 ** ADAM_STEP))
    delta = -ADAM_LR * (m_hat / (jnp.sqrt(v_hat) + ADAM_EPS) + ADAM_WD * w)
    return delta, m2, v2


def _adamw(w, gsrc, m, v, name):
    r, c = w.shape
    parts = gsrc.ndim == 3
    cg = gsrc.shape[-1]
    tr = r
    while tr * c * 4 > _mb(1) and tr % 32 == 0:
        tr //= 2

    def body(w_ref, g_ref, m_ref, v_ref, go_ref, d_ref, mo_ref, vo_ref):
        if parts:
            g = g_ref[0].astype(F32)
            for s in range(1, NDEV):
                g = g + g_ref[s].astype(F32)
            g = g[:, :c]
        else:
            g = g_ref[...]
        delta, m2, v2 = _adam_math(w_ref[...], g, m_ref[...], v_ref[...])
        go_ref[...] = g
        d_ref[...] = delta
        mo_ref[...] = m2
        vo_ref[...] = v2

    tile = pl.BlockSpec((tr, c), lambda i: (i, 0))
    gspec = pl.BlockSpec((NDEV, tr, cg), lambda i: (0, i, 0)) if parts else tile
    sds = jax.ShapeDtypeStruct((r, c), F32)
    return _pallas(
        body, name=name, grid=(r // tr,),
        out_shape=(sds, sds, sds, sds),
        in_specs=[tile, gspec, tile, tile],
        out_specs=(tile, tile, tile, tile),
        compiler_params=_params(_mb(48)),
    )(w, gsrc, m, v)


def _adamw_t(w, gsrc_t, m, v, name):
    r, c = w.shape
    tr = 256

    def body(w_ref, g_ref, m_ref, v_ref, go_ref, d_ref, mo_ref, vo_ref):
        gt = g_ref[0].astype(F32)
        for s in range(1, NDEV):
            gt = gt + g_ref[s].astype(F32)
        g = gt.T
        delta, m2, v2 = _adam_math(w_ref[...], g, m_ref[...], v_ref[...])
        go_ref[...] = g
        d_ref[...] = delta
        mo_ref[...] = m2
        vo_ref[...] = v2

    tile = pl.BlockSpec((tr, c), lambda i: (i, 0))
    sds = jax.ShapeDtypeStruct((r, c), F32)
    return _pallas(
        body, name=name, grid=(r // tr,),
        out_shape=(sds, sds, sds, sds),
        in_specs=[tile, pl.BlockSpec((NDEV, c, tr), lambda i: (0, 0, i)), tile, tile],
        out_specs=(tile, tile, tile, tile),
        compiler_params=_params(_mb(48)),
    )(w, gsrc_t, m, v)


def _small_update(gath, dcc, cc, w_s, m_s, v_s):
    d = w_s.shape[1]

    def body(g_ref, dcc_ref, cc_ref, w_ref, m_ref, v_ref, go_ref, d_ref, mo_ref, vo_ref):
        s = g_ref[0]
        for b in range(1, NDEV):
            s = s + g_ref[b]
        dsl = dcc_ref[0, 8:9, :]
        for b in range(1, NDEV):
            dsl = dsl + dcc_ref[b, 8:9, :]
        cv = cc_ref[...]
        sg = _sigmoid(cv)
        go_ref[...] = jnp.zeros_like(go_ref)
        go_ref[0:1, :] = dsl * (sg * (1.0 + cv * (1.0 - sg)))
        go_ref[1:3, :] = s[0:2] + s[6:8]
        go_ref[3:7, :] = s[2:6]
        go_ref[7:12, :] = s[8:13]
        delta, m2, v2 = _adam_math(w_ref[...], go_ref[...], m_ref[...], v_ref[...])
        d_ref[...] = delta
        mo_ref[...] = m2
        vo_ref[...] = v2

    full = pl.BlockSpec((16, d), lambda: (0, 0))
    g3 = pl.BlockSpec((NDEV, 16, d), lambda: (0, 0, 0))
    sds = jax.ShapeDtypeStruct((16, d), F32)
    return _pallas(
        body, name="small_update",
        out_shape=(sds, sds, sds, sds),
        in_specs=[g3, g3, pl.BlockSpec((1, d), lambda: (0, 0)), full, full, full],
        out_specs=(full, full, full, full),
        compiler_params=_params(_mb(24)),
    )(gath, dcc, cc, w_s, m_s, v_s)


def _rope_tables(n):
    rows = n // GRID_W
    row_ids = jnp.repeat(jnp.arange(rows, dtype=F32), GRID_W)
    col_ids = jnp.tile(jnp.arange(GRID_W, dtype=F32), rows)
    axis_dim = HEAD // 2
    inv_freq = jnp.power(ROPE_THETA, -jnp.arange(0, axis_dim, 2, dtype=F32) / axis_dim)
    ang_r = row_ids[:, None] * inv_freq
    ang_c = col_ids[:, None] * inv_freq
    ang = jnp.concatenate([ang_r, ang_r, ang_c, ang_c], axis=-1)
    cos, sin = jnp.cos(ang), jnp.sin(ang)
    first = (jnp.arange(HEAD) % (HEAD // 2)) < HEAD // 4
    sa = jnp.where(first, -sin, 0.0)
    sb = jnp.where(first, 0.0, sin)
    ones = jnp.ones((CTX, HEAD), F32)
    zeros = jnp.zeros((CTX, HEAD), F32)
    return (jnp.concatenate([cos, ones], 0), jnp.concatenate([sa, zeros], 0), jnp.concatenate([sb, zeros], 0))


def _pad_cols(a, width):
    return jnp.pad(a, ((0, 0), (0, width - a.shape[1])))


def _pad_rows(a, rows):
    return jnp.pad(a, ((0, rows - a.shape[0]), (0, 0)))


def _pack_small(c_ctx, b_ada, ln1_g, ln1_b, ln2_g, ln2_b, qg, kg, sink, d):
    misc = _pad_cols(jnp.concatenate([qg, kg, sink], axis=1), d)
    rows = jnp.concatenate([c_ctx.reshape(1, d), b_ada.reshape(6, d), ln1_g, ln1_b, ln2_g, ln2_b, misc], axis=0)
    return _pad_rows(rows, 16)


def _unpack_small(p, d):
    return dict(c_ctx=p[0], b_ada=p[1:7].reshape(1, 6 * d), ln1_g=p[7:8], ln1_b=p[8:9], ln2_g=p[9:10], ln2_b=p[10:11],
                q_norm_g=p[11:12, 0:HEAD], k_norm_g=p[11:12, HEAD:2 * HEAD], sink_logit=p[11:12, 2 * HEAD:2 * HEAD + 8])


def kernel(x, c, ctx, c_ctx, w_ada, b_ada, w_in, q_norm_g, k_norm_g, sink_logit, w_out, ln1_g, ln1_b, w_gate, w_up, w_down, ln2_g, ln2_b, loss_target, m_c_ctx, m_w_ada, m_b_ada, m_w_in, m_q_norm_g, m_k_norm_g, m_sink_logit, m_w_out, m_ln1_g, m_ln1_b, m_w_gate, m_w_up, m_w_down, m_ln2_g, m_ln2_b, v_c_ctx, v_w_ada, v_b_ada, v_w_in, v_q_norm_g, v_k_norm_g, v_sink_logit, v_w_out, v_ln1_g, v_ln1_b, v_w_gate, v_w_up, v_w_down, v_ln2_g, v_ln2_b):
    xs, cts, tgt = x[0], ctx[0], loss_target[0]
    n, d = xs.shape
    assert cts.shape == (CTX, d) and w_in.shape[2] == IN_SHARD and w_gate.shape[2] == FFN_SHARD
    me = 4 * lax.axis_index("x") + 2 * lax.axis_index("y") + lax.axis_index("c")
    e_sh = w_ada.shape[2]

    c_g = _exchange(_pad_rows(c, 8), False, "gather_c")
    c_all = jnp.concatenate([c_g[:, 0, :], _pad_rows(c_ctx.reshape(1, d), 8)], axis=0)
    bias_sh = lax.dynamic_slice(b_ada, (0, me * e_sh), (1, e_sh))
    mods_g = _exchange(_ada_fwd(c_all, w_ada[0], bias_sh), False, "gather_mods")
    mods = jnp.transpose(mods_g, (1, 0, 2)).reshape(16, NDEV * e_sh)
    mine = lax.dynamic_slice(mods, (me, 0), (1, 6 * d))
    sh1, sc1, g1, sh2, sc2, g2 = [mine[:, k * d:(k + 1) * d] for k in range(6)]
    csh1, csc1 = mods[8:9, 0:d], mods[8:9, d:2 * d]
    sc_pair = jnp.stack([sc1, csc1])
    sh_pair = jnp.stack([sh1, csh1])

    h_win, tok = _exchange_start(w_in[0].T.astype(BF16), "chip", mods, "gather_w_in_start")
    tok, (wo_l, wg_l, wu_l, wd_l) = lax.optimization_barrier((tok, (w_out, w_gate, w_up, w_down)))
    h_wout, tok = _exchange_start(wo_l[0].astype(BF16), "chip", tok, "gather_w_out_start")
    h_wg, tok = _exchange_start(wg_l[0].T.astype(BF16), "chip", tok, "gather_w_gate_start")
    h_wu, tok = _exchange_start(wu_l[0].T.astype(BF16), "chip", tok, "gather_w_up_start")
    h_wd, tok = _exchange_start(wd_l[0].astype(BF16), "chip", tok, "gather_w_down_start")

    cos, sa, sb = _rope_tables(n)
    f_win, tok = _forward_start(_exchange_wait(h_win, "chip", tok, "gather_w_in_wait"), tok, "forward_w_in_start")
    win_g = _forward_wait(f_win, tok, "forward_w_in_wait").reshape(NDEV * IN_SHARD, d)
    u_all, h_all, t_all, kt_b = _qkv_fwd(xs, cts, sc_pair, sh_pair, win_g, q_norm_g, k_norm_g, cos, sa, sb)
    f_wout, tok = _forward_start(_exchange_wait(h_wout, "chip", t_all, "gather_w_out_wait"), t_all, "forward_w_out_start")
    win_bias = _window_bias()
    o_a, lse_a = _attn_window_fwd(t_all, sink_logit, win_bias, tok)
    o, lse_b = _attn_global_fwd(t_all, o_a)
    f_wg, tok = _forward_start(_exchange_wait(h_wg, "chip", o, "gather_w_gate_wait"), o, "forward_w_gate_start")
    f_wu, tok = _forward_start(_exchange_wait(h_wu, "chip", tok, "gather_w_up_wait"), tok, "forward_w_up_start")
    wout_g = _forward_wait(f_wout, tok, "forward_w_out_wait").reshape(d, d)
    a1, xh1, rs1, u2 = _outproj_ln1(o, wout_g, xs, g1, ln1_g, ln1_b, sc2, sh2, tok)
    f_wd, tok = _forward_start(_exchange_wait(h_wd, "chip", rs1, "gather_w_down_wait"), rs1, "forward_w_down_start")
    ffn_w = (NDEV * FFN_SHARD, d)
    wg_g = _forward_wait(f_wg, tok, "forward_w_gate_wait").reshape(ffn_w)
    wu_g = _forward_wait(f_wu, tok, "forward_w_up_wait").reshape(ffn_w)
    sa_f, sb_f, hf = _ffn_up(u2, wg_g, wu_g, tok)
    wd_g = _forward_wait(f_wd, hf, "forward_w_down_wait").reshape(ffn_w)
    ffn = _ffn_down(hf, wd_g)
    dr2, df, loss_p, acc2 = _ln2_loss(xh1, ffn, tgt, ln1_g, ln1_b, g2, ln2_g, ln2_b)
    loss = lax.psum(loss_p[0, 0], ("x", "y", "c"))

    tk = min(n, 2048)
    parts = (NDEV, FFN_SHARD, d)
    dgp = _ffn_dhf(df, wd_g, sa_f, sb_f)
    dwd_p = _dw_rows(hf, df, NDEV // 2, FFN_PAIR, min(n, 1024), loss_p, "dw_down").reshape(parts)
    h_dwd, tok = _exchange_start(dwd_p, "scatter", loss.reshape(1, 1), "scatter_dw_down_start")
    dwg_t, dwu_t = _dw_gate_up(dgp, u2, tok)
    h_dwg, tok = _exchange_start(dwg_t.reshape(parts), "scatter", tok, "scatter_dw_gate_start")
    h_dwu, tok = _exchange_start(dwu_t.reshape(parts), "scatter", tok, "scatter_dw_up_start")
    du2 = _ffn_du2(dgp, wg_g, wu_g, tok)
    dr1, da1, acc1 = _ln1_bwd(du2, dr2, xh1, rs1, a1, ln1_g, ln1_b, sc2, g1)
    dwo_p = _dw_rows(o, da1, NDEV, 2 * HEAD, tk, loss_p, "dw_out")
    h_dwo, tok = _exchange_start(dwo_p, "scatter", loss_p, "scatter_dw_out_start")
    do = _outproj_bwd(da1, wout_g, tok)
    dqa, dka, dva, dsink = _attn_window_bwd(t_all, o, do, lse_a, sink_logit, win_bias)
    dqb, dkb, dvb = _attn_global_bwd(t_all, kt_b, o, do, lse_b)
    dh_all, dnorm = _qkv_bwd_prep(dqa, dka, dva, dqb, dkb, dvb, h_all, q_norm_g, k_norm_g, cos, sa, sb)
    grad_x, acc0 = _qkv_bwd(dh_all, win_g, xs, cts, dr1, sc_pair)

    misc = _pad_cols(jnp.concatenate([dnorm[0:1], dnorm[1:2], dsink[:, 0:4, 0].reshape(1, 8)], axis=1), d)
    part = jnp.concatenate([
        acc0[0:2], acc1[4:5], acc1[1:2], acc1[0:1], acc2[2:3],
        acc0[2:4],
        acc1[2:4], acc2[0:2],
        misc, jnp.zeros((3, d), F32)], axis=0)
    gath = _exchange(part, False, "gather_small")
    dm_batch = gath[:, 0:6, :].reshape(NDEV, 6 * d)
    dm_ctx = _pad_cols(gath[:, 6:8, :].reshape(NDEV, 2 * d), 6 * d)
    dm16 = lax.dynamic_slice(jnp.concatenate([dm_batch, dm_ctx], axis=0), (0, me * e_sh), (16, e_sh))
    dw_ada, drow = _ada_bwd(dm16, c_all, w_ada[0])
    dcc = _exchange(drow, False, "gather_dcc")
    dwi_p = _dw_rows(dh_all, u_all, NDEV, IN_SHARD, (n + CTX) // 2, dcc, "dw_in")
    h_dwi, tok = _exchange_start(dwi_p, "scatter", dcc, "scatter_dw_in_start")

    w_s = _pack_small(c_ctx, b_ada, ln1_g, ln1_b, ln2_g, ln2_b, q_norm_g, k_norm_g, sink_logit, d)
    m_s = _pack_small(m_c_ctx, m_b_ada, m_ln1_g, m_ln1_b, m_ln2_g, m_ln2_b, m_q_norm_g, m_k_norm_g, m_sink_logit, d)
    v_s = _pack_small(v_c_ctx, v_b_ada, v_ln1_g, v_ln1_b, v_ln2_g, v_ln2_b, v_q_norm_g, v_k_norm_g, v_sink_logit, d)
    small = [_unpack_small(p, d) for p in _small_update(gath, dcc, c_ctx.reshape(1, d), w_s, m_s, v_s)]

    big = {}
    big["w_ada"] = _adamw(w_ada[0], dw_ada, m_w_ada[0], v_w_ada[0], "adamw_w_ada")
    late = tok
    big["w_down"] = _adamw(w_down[0], _exchange_wait(h_dwd, "scatter", late, "scatter_dw_down_wait"), m_w_down[0],
                           v_w_down[0], "adamw_w_down")
    for nm, wt, mt, vt, hd in (("w_gate", w_gate, m_w_gate, v_w_gate, h_dwg), ("w_up", w_up, m_w_up, v_w_up, h_dwu)):
        big[nm] = _adamw_t(wt[0], _exchange_wait(hd, "scatter", late, "scatter_d" + nm + "_wait"), mt[0], vt[0],
                           "adamw_" + nm)
    big["w_out"] = _adamw(w_out[0], _exchange_wait(h_dwo, "scatter", late, "scatter_dw_out_wait"), m_w_out[0], v_w_out[0],
                          "adamw_w_out")
    big["w_in"] = _adamw_t(w_in[0], _exchange_wait(h_dwi, "scatter", big["w_out"][1], "scatter_dw_in_wait"), m_w_in[0],
                           v_w_in[0], "adamw_w_in")

    names = ["c_ctx", "w_ada", "b_ada", "w_in", "q_norm_g", "k_norm_g", "sink_logit", "w_out", "ln1_g", "ln1_b",
             "w_gate", "w_up", "w_down", "ln2_g", "ln2_b"]
    outs = [loss, grad_x[None]]
    for k in range(4):
        for nm in names:
            outs.append(big[nm][k][None] if nm in big else small[k][nm])
    return tuple(outs)
```

```python
import functools

import jax
import jax.numpy as jnp
from jax import lax
from jax.experimental import pallas as pl
from jax.experimental.pallas import tpu as pltpu

F32 = jnp.float32
BF16 = jnp.bfloat16

NDEV = 8
HEAD = 128
CTX = 256
GRID_W = 64
WINDOW = 128
ROPE_THETA = 10000.0
EPS = 1e-6
SCALE = HEAD ** -0.5
LOG2E = 1.4426950408889634
QK_LOG2 = SCALE * LOG2E
ALPHA = 2.0 ** 0.25
FFN_SHARD = 704
FFN_TILE = 512
FFN_PAIR = 2 * FFN_SHARD
IN_SHARD = 384
NEG = -1e30

ADAM_LR = 0.001
ADAM_B1 = 0.9
ADAM_B2 = 0.999
ADAM_EPS = 1e-08
ADAM_WD = 0.01
ADAM_STEP = 10

VMEM_CAP = 56 * 1024 * 1024

_KINDS = ["rope"] * 10 + ["none"] * 2 + ["qnorm"] * 8 + ["knorm"] * 2 + ["none"] * 2

_NT = (((1,), (1,)), ((), ()))
_TN = (((0,), (0,)), ((), ()))


def _pallas(body, **kw):
    return pl.pallas_call(body, **kw)


def _params(vmem_bytes):
    return pltpu.CompilerParams(vmem_limit_bytes=int(min(VMEM_CAP, vmem_bytes)))


def _mb(n):
    return int(n * 1024 * 1024)


def _sigmoid(x):
    return 1.0 / (1.0 + jnp.exp(-x))


def _colsum(a):
    return jnp.sum(a, axis=0, keepdims=True)


def _rowmean(a):
    return jnp.mean(a, axis=-1, keepdims=True)


def _exchange(src, scatter, name, after=None):
    blk = src.shape[1:] if scatter else src.shape
    after = src if after is None else after

    def body(src_ref, after_ref, out_ref, send_sems, recv_sems, local_sem):
        x, y, c = lax.axis_index("x"), lax.axis_index("y"), lax.axis_index("c")
        me = 4 * x + 2 * y + c
        copies = []
        for t in range(1, NDEV):
            px = 1 - x if (t >> 2) & 1 else x
            py = 1 - y if (t >> 1) & 1 else y
            pc = 1 - c if t & 1 else c
            peer = 4 * px + 2 * py + pc
            cp = pltpu.make_async_remote_copy(
                src_ref=src_ref.at[peer] if scatter else src_ref,
                dst_ref=out_ref.at[me],
                send_sem=send_sems.at[t - 1],
                recv_sem=recv_sems.at[t - 1],
                device_id=(px, py, pc),
                device_id_type=pl.DeviceIdType.MESH,
            )
            cp.start()
            copies.append(cp)
        own = pltpu.make_async_copy(src_ref.at[me] if scatter else src_ref, out_ref.at[me], local_sem)
        own.start()
        for cp in copies:
            cp.wait()
        own.wait()

    return _pallas(
        body, name=name,
        out_shape=jax.ShapeDtypeStruct((NDEV,) + tuple(blk), src.dtype),
        in_specs=[pl.BlockSpec(memory_space=pl.ANY), pl.BlockSpec(memory_space=pl.ANY)],
        out_specs=pl.BlockSpec(memory_space=pl.ANY),
        scratch_shapes=[pltpu.SemaphoreType.DMA((NDEV - 1,)), pltpu.SemaphoreType.DMA((NDEV - 1,)),
                        pltpu.SemaphoreType.DMA(())],
    )(src, after)


_HBM = pl.BlockSpec(memory_space=pltpu.HBM)
_SEM = pl.BlockSpec(memory_space=pltpu.SEMAPHORE)
_ANY = pl.BlockSpec(memory_space=pl.ANY)
_EFFECT = pltpu.SideEffectType.DATAFLOW_SIDE_EFFECTING


def _exchange_copies(src_ref, land_ref, send_sems, recv_sems, mode):
    x, y, c = lax.axis_index("x"), lax.axis_index("y"), lax.axis_index("c")
    me = 4 * x + 2 * y + c
    scatter = mode == "scatter"
    copies = []
    for t in ((1, 2, 4, 6) if mode == "chip" else range(1, NDEV)):
        px = 1 - x if (t >> 2) & 1 else x
        py = 1 - y if (t >> 1) & 1 else y
        pc = 1 - c if t & 1 else c
        peer = 4 * px + 2 * py + pc
        copies.append(pltpu.make_async_remote_copy(
            src_ref=src_ref.at[peer] if scatter else src_ref,
            dst_ref=land_ref.at[me],
            send_sem=send_sems.at[t - 1],
            recv_sem=recv_sems.at[t - 1],
            device_id=(px, py, pc),
            device_id_type=pl.DeviceIdType.MESH,
        ))
    own = pltpu.make_async_copy(src_ref.at[me] if scatter else src_ref, land_ref.at[me], send_sems.at[NDEV - 1])
    return copies, own


def _forward_copies(land_ref, send_sems, recv_sems):
    x, y, c = lax.axis_index("x"), lax.axis_index("y"), lax.axis_index("c")
    copies = []
    for k, t in enumerate((2, 4, 6)):
        px = 1 - x if (t >> 2) & 1 else x
        py = 1 - y if (t >> 1) & 1 else y
        mine, theirs = 4 * px + 2 * py + c, 4 * px + 2 * py + (1 - c)
        send = pltpu.make_async_remote_copy(
            src_ref=land_ref.at[mine], dst_ref=land_ref.at[mine], send_sem=send_sems.at[k], recv_sem=recv_sems.at[k],
            device_id=(x, y, 1 - c), device_id_type=pl.DeviceIdType.MESH)
        recv = pltpu.make_async_remote_copy(
            src_ref=land_ref.at[theirs], dst_ref=land_ref.at[theirs], send_sem=send_sems.at[k], recv_sem=recv_sems.at[k],
            device_id=(x, y, 1 - c), device_id_type=pl.DeviceIdType.MESH)
        copies.append((send, recv))
    return copies


def _forward_start(land, after, name):
    def body(land_ref, after_ref, send_sems, recv_sems, land_thru, token):
        for send, _ in _forward_copies(land_ref, send_sems, recv_sems):
            send.start()
        token[...] = jnp.zeros_like(token)

    res = _pallas(
        body, name=name,
        out_shape=(pltpu.SemaphoreType.DMA((3,)), pltpu.SemaphoreType.DMA((3,)), pltpu.HBM(land.shape, land.dtype),
                   jax.ShapeDtypeStruct((8, HEAD), F32)),
        in_specs=(_HBM, _ANY), out_specs=(_SEM, _SEM, _HBM, pl.BlockSpec(memory_space=pltpu.VMEM)),
        input_output_aliases={0: 2},
        compiler_params=pltpu.CompilerParams(has_side_effects=_EFFECT),
    )(land, after)
    return res[:3], res[3]


def _forward_wait(handle, after, name):
    send_sems, recv_sems, land_thru = handle

    def body(land_ref, send_sems, recv_sems, after_ref, got_ref):
        for send, recv in _forward_copies(land_ref, send_sems, recv_sems):
            send.wait_send()
            recv.wait_recv()

    return _pallas(
        body, name=name,
        out_shape=pltpu.HBM(land_thru.shape, land_thru.dtype),
        in_specs=(_HBM, _SEM, _SEM, _ANY), out_specs=_HBM,
        input_output_aliases={0: 0},
        compiler_params=pltpu.CompilerParams(has_side_effects=_EFFECT),
    )(land_thru, send_sems, recv_sems, after)


def _exchange_start(src, mode, after, name):
    blk = src.shape[1:] if mode == "scatter" else src.shape
    land = lax.empty((NDEV,) + tuple(blk), src.dtype)

    def body(src_ref, land_ref, after_ref, send_sems, recv_sems, src_thru, land_thru, token):
        copies, own = _exchange_copies(src_ref, land_ref, send_sems, recv_sems, mode)
        for cp in copies:
            cp.start()
        own.start()
        token[...] = jnp.zeros_like(token)

    res = _pallas(
        body, name=name,
        out_shape=(pltpu.SemaphoreType.DMA((NDEV,)), pltpu.SemaphoreType.DMA((NDEV,)),
                   pltpu.HBM(src.shape, src.dtype), pltpu.HBM(land.shape, land.dtype),
                   jax.ShapeDtypeStruct((8, HEAD), F32)),
        in_specs=(_HBM, _HBM, _ANY), out_specs=(_SEM, _SEM, _HBM, _HBM, pl.BlockSpec(memory_space=pltpu.VMEM)),
        input_output_aliases={0: 2, 1: 3},
        compiler_params=pltpu.CompilerParams(has_side_effects=_EFFECT),
    )(pltpu.with_memory_space_constraint(src, pltpu.HBM), pltpu.with_memory_space_constraint(land, pltpu.HBM), after)
    return res[:4], res[4]


def _exchange_wait(handle, mode, after, name):
    send_sems, recv_sems, src_thru, land_thru = handle

    def body(src_ref, land_ref, send_sems, recv_sems, after_ref, src_dead, got_ref):
        copies, own = _exchange_copies(src_ref, land_ref, send_sems, recv_sems, mode)
        for cp in copies:
            cp.wait_send()
            cp.wait_recv()
        own.wait()

    return _pallas(
        body, name=name,
        out_shape=(pltpu.HBM(src_thru.shape, src_thru.dtype), pltpu.HBM(land_thru.shape, land_thru.dtype)),
        in_specs=(_HBM, _HBM, _SEM, _SEM, _ANY), out_specs=(_HBM, _HBM),
        input_output_aliases={0: 0, 1: 1},
        compiler_params=pltpu.CompilerParams(has_side_effects=_EFFECT),
    )(src_thru, land_thru, send_sems, recv_sems, after)[1]


def _ada_fwd(c_all, w, bias):
    r, d = c_all.shape
    e = w.shape[1]
    tn = 512

    def body(c_ref, w_ref, b_ref, o_ref):
        cv = c_ref[...]
        s = (cv * _sigmoid(cv)).astype(BF16)
        o_ref[...] = jnp.dot(s, w_ref[...].astype(BF16), preferred_element_type=F32) + b_ref[...]

    return _pallas(
        body, name="ada_fwd", grid=(e // tn,),
        out_shape=jax.ShapeDtypeStruct((r, e), F32),
        in_specs=[pl.BlockSpec((r, d), lambda j: (0, 0)), pl.BlockSpec((d, tn), lambda j: (0, j)),
                  pl.BlockSpec((1, tn), lambda j: (0, j))],
        out_specs=pl.BlockSpec((r, tn), lambda j: (0, j)),
        compiler_params=_params(_mb(24)),
    )(c_all, w, bias)


def _ada_bwd(dm16, c_all, w):
    d, e = w.shape
    tn = 512

    def body(dm_ref, c_ref, w_ref, dw_ref, dr_ref):
        j = pl.program_id(0)
        dm = dm_ref[...]
        rid = lax.broadcasted_iota(jnp.int32, dm.shape, 0)
        ctx_sum = jnp.sum(jnp.where(rid >= 8, dm, 0.0), axis=0, keepdims=True)
        rows = jnp.where(rid < 8, dm, jnp.where(rid == 8, jnp.broadcast_to(ctx_sum, dm.shape), 0.0)).astype(BF16)
        cv = c_ref[...]
        s = (cv * _sigmoid(cv)).astype(BF16)
        dw_ref[...] = lax.dot_general(s, rows, _TN, preferred_element_type=F32)
        part = lax.dot_general(rows, w_ref[...].astype(BF16), _NT, preferred_element_type=F32)

        @pl.when(j == 0)
        def _():
            dr_ref[...] = part

        @pl.when(j > 0)
        def _():
            dr_ref[...] += part

    return _pallas(
        body, name="ada_bwd", grid=(e // tn,),
        out_shape=(jax.ShapeDtypeStruct((d, e), F32), jax.ShapeDtypeStruct((16, d), F32)),
        in_specs=[pl.BlockSpec((16, tn), lambda j: (0, j)), pl.BlockSpec((16, d), lambda j: (0, 0)),
                  pl.BlockSpec((d, tn), lambda j: (0, j))],
        out_specs=(pl.BlockSpec((d, tn), lambda j: (0, j)), pl.BlockSpec((16, d), lambda j: (0, 0))),
        compiler_params=_params(_mb(32)),
    )(dm16, c_all, w)


def _rope(v, cos, sa, sb):
    return v * cos + (pltpu.roll(v, 96, 1) * sa + pltpu.roll(v, 32, 1) * sb)


def _rope_t(dt, cos, sa, sb):
    return dt * cos + (pltpu.roll(dt * sa, 32, 1) + pltpu.roll(dt * sb, 96, 1))


def _qkv_fwd(x, ct, sc, sh, wint, qg, kg, cos, sa, sb):
    n, d = x.shape
    tm = CTX
    nlat = n // tm
    na = n + CTX
    wcols = wint.shape[0]

    def body(x_ref, ct_ref, sc_ref, sh_ref, w_ref, qg_ref, kg_ref, cos_ref, sa_ref, sb_ref, u_ref, h_ref, t_ref, kt_ref):
        i = pl.program_id(0)
        xin = jnp.where(i == nlat, ct_ref[...], x_ref[...])
        u = (xin * (1.0 + sc_ref[0]) + sh_ref[0]).astype(BF16)
        u_ref[...] = u
        cos, sa, sb = cos_ref[...], sa_ref[...], sb_ref[...]
        h = lax.dot_general(u, w_ref[...], _NT, preferred_element_type=F32)
        h_ref[...] = h
        for hd in range(24):
            v = h[:, hd * HEAD:(hd + 1) * HEAD]
            kind = _KINDS[hd]
            if kind == "qnorm":
                v = v * lax.rsqrt(_rowmean(v * v) + EPS) * qg_ref[...]
            elif kind == "knorm":
                v = v * lax.rsqrt(_rowmean(v * v) + EPS) * kg_ref[...]
            if kind != "none":
                v = _rope(v, cos, sa, sb)
            t_ref[:, hd * HEAD:(hd + 1) * HEAD] = v.astype(BF16)
            if kind == "knorm":
                kt_ref[(hd - 20) * HEAD:(hd - 19) * HEAD, :] = v.T.astype(BF16)

    lat = lambda i: (jnp.minimum(i, nlat - 1), 0)
    row = lambda i: (i, 0)
    const2 = lambda i: (0, 0)
    return _pallas(
        body, name="qkv_fwd", grid=(nlat + 1,),
        out_shape=(jax.ShapeDtypeStruct((na, d), BF16), jax.ShapeDtypeStruct((na, wcols), F32),
                   jax.ShapeDtypeStruct((na, wcols), BF16), jax.ShapeDtypeStruct((2 * HEAD, na), BF16)),
        in_specs=[pl.BlockSpec((tm, d), lat), pl.BlockSpec((tm, d), const2),
                  pl.BlockSpec((1, 1, d), lambda i: (i // nlat, 0, 0)),
                  pl.BlockSpec((1, 1, d), lambda i: (i // nlat, 0, 0)),
                  pl.BlockSpec((wcols, d), const2),
                  pl.BlockSpec((1, HEAD), const2), pl.BlockSpec((1, HEAD), const2),
                  pl.BlockSpec((tm, HEAD), row), pl.BlockSpec((tm, HEAD), row), pl.BlockSpec((tm, HEAD), row)],
        out_specs=(pl.BlockSpec((tm, d), row), pl.BlockSpec((tm, wcols), row), pl.BlockSpec((tm, wcols), row),
                   pl.BlockSpec((2 * HEAD, tm), lambda i: (0, i))),
        compiler_params=_params(_mb(56)),
    )(x, ct, sc, sh, wint, qg, kg, cos, sa, sb)


def _qkv_bwd_prep(dqa, dka, dva, dqb, dkb, dvb, h_all, qg, kg, cos, sa, sb):
    na, wcols = h_all.shape
    n = na - CTX
    tm = CTX
    nlat = n // tm

    def body(dqa_ref, dka_ref, dva_ref, dqb_ref, dkb_ref, dvb_ref, h_ref, qg_ref, kg_ref, cos_ref, sa_ref, sb_ref,
             dh_ref, dg_ref):
        i = pl.program_id(0)

        @pl.when(i == 0)
        def _():
            dg_ref[...] = jnp.zeros_like(dg_ref)

        cos, sa, sb = cos_ref[...], sa_ref[...], sb_ref[...]
        is_lat = i < nlat
        for hd in range(24):
            kind = _KINDS[hd]
            if hd < 8:
                dt = jnp.where(is_lat, dqa_ref[:, hd * HEAD:(hd + 1) * HEAD], 0.0)
            elif hd < 10:
                dt = dka_ref[:, (hd - 8) * HEAD:(hd - 7) * HEAD]
            elif hd < 12:
                dt = dva_ref[:, (hd - 10) * HEAD:(hd - 9) * HEAD]
            elif hd < 20:
                dt = jnp.where(is_lat, dqb_ref[:, (hd - 12) * HEAD:(hd - 11) * HEAD], 0.0)
            elif hd < 22:
                dt = dkb_ref[:, (hd - 20) * HEAD:(hd - 19) * HEAD]
            else:
                dt = dvb_ref[:, (hd - 22) * HEAD:(hd - 21) * HEAD]
            if kind != "none":
                dt = _rope_t(dt, cos, sa, sb)
            if kind in ("qnorm", "knorm"):
                g_ref = qg_ref if kind == "qnorm" else kg_ref
                r0 = 0 if kind == "qnorm" else 1
                xv = h_ref[:, hd * HEAD:(hd + 1) * HEAD]
                xn = xv * lax.rsqrt(_rowmean(xv * xv) + EPS)
                dg_ref[r0:r0 + 1, :] += _colsum(dt * xn)
                dxn = dt * g_ref[...]
                dt = lax.rsqrt(_rowmean(xv * xv) + EPS) * (dxn - xn * _rowmean(dxn * xn))
            dh_ref[:, hd * HEAD:(hd + 1) * HEAD] = dt.astype(BF16)

    lat = lambda i: (jnp.minimum(i, nlat - 1), 0)
    row = lambda i: (i, 0)
    const2 = lambda i: (0, 0)
    return _pallas(
        body, name="qkv_bwd_prep", grid=(nlat + 1,),
        out_shape=(jax.ShapeDtypeStruct((na, wcols), BF16), jax.ShapeDtypeStruct((8, HEAD), F32)),
        in_specs=[pl.BlockSpec((tm, 8 * HEAD), lat), pl.BlockSpec((tm, 2 * HEAD), row), pl.BlockSpec((tm, 2 * HEAD), row),
                  pl.BlockSpec((tm, 8 * HEAD), lat), pl.BlockSpec((tm, 2 * HEAD), row), pl.BlockSpec((tm, 2 * HEAD), row),
                  pl.BlockSpec((tm, wcols), row),
                  pl.BlockSpec((1, HEAD), const2), pl.BlockSpec((1, HEAD), const2),
                  pl.BlockSpec((tm, HEAD), row), pl.BlockSpec((tm, HEAD), row), pl.BlockSpec((tm, HEAD), row)],
        out_specs=(pl.BlockSpec((tm, wcols), row), pl.BlockSpec((8, HEAD), const2)),
        compiler_params=_params(_mb(40)),
    )(dqa, dka, dva, dqb, dkb, dvb, h_all, qg, kg, cos, sa, sb)


def _window_keys(k_ref, v_ref, n, na):
    i = pl.program_id(1)
    tq = WINDOW
    start = pl.multiple_of(jnp.clip((i - 1) * tq, 0, n - 3 * tq), tq)
    kk = jnp.concatenate([k_ref[pl.ds(start, 3 * tq), :], k_ref[n:na, :]], axis=0)
    vv = jnp.concatenate([v_ref[pl.ds(start, 3 * tq), :], v_ref[n:na, :]], axis=0)
    return kk, vv, start


def _window_bias():
    tq = WINDOW
    r = (jnp.arange(4 * tq) % tq)[:, None]
    c = jnp.arange(3 * tq + CTX)[None, :]
    variants = []
    for back in (0, tq, 2 * tq):
        seen = (jnp.abs(back + r - c) <= WINDOW) | (c >= 3 * tq)
        variants.append(jnp.where(seen, 0.0, NEG).astype(F32))
    return jnp.stack(variants)


def _window_bias_spec(nq):
    return pl.BlockSpec((1, 4 * WINDOW, 3 * WINDOW + CTX),
                        lambda kv, i: (jnp.where(i == 0, 0, jnp.where(i == nq - 1, 2, 1)), 0, 0))


def _stack_heads(ref, width=HEAD):
    return jnp.concatenate([ref[:, g * HEAD:g * HEAD + width] for g in range(4)], axis=0)


def _sink_column(sink_ref, kv, tq):
    grp = lax.broadcasted_iota(jnp.int32, (4 * tq, 1), 0) // tq
    col = jnp.zeros((4 * tq, 1), F32)
    for g in range(4):
        col = jnp.where(grp == g, sink_ref[0, 4 * kv + g] * LOG2E, col)
    return col


def _attn_window_fwd(t_all, sink, bias, after):
    na = t_all.shape[0]
    n = na - CTX
    tq = WINDOW

    def body(sink_ref, q_ref, k_ref, v_ref, bias_ref, after_ref, o_ref, lse_ref):
        kv = pl.program_id(0)
        kk, vv, _ = _window_keys(k_ref, v_ref, n, na)
        t = lax.dot_general(_stack_heads(q_ref), kk, _NT, preferred_element_type=F32) * QK_LOG2 + bias_ref[0]
        sk = _sink_column(sink_ref, kv, tq)
        m = jnp.maximum(jnp.max(t, axis=-1, keepdims=True), sk)
        p = jnp.exp2(t - m)
        l = jnp.sum(p, axis=-1, keepdims=True) + jnp.exp2(sk - m)
        o = jnp.dot(p.astype(BF16), vv, preferred_element_type=F32) * (1.0 / l)
        lse = m + jnp.log2(l)
        for g in range(4):
            o_ref[:, g * HEAD:(g + 1) * HEAD] = o[g * tq:(g + 1) * tq]
            lse_ref[:, g * HEAD:(g + 1) * HEAD] = jnp.broadcast_to(lse[g * tq:(g + 1) * tq], (tq, HEAD))

    blk = pl.BlockSpec((tq, 4 * HEAD), lambda kv, i: (i, kv))
    return _pallas(
        body, name="attn_window_fwd", grid=(2, n // tq),
        out_shape=(jax.ShapeDtypeStruct((n, 16 * HEAD), F32), jax.ShapeDtypeStruct((n, 8 * HEAD), F32)),
        in_specs=[pl.BlockSpec(memory_space=pltpu.SMEM), blk,
                  pl.BlockSpec((na, HEAD), lambda kv, i: (0, 8 + kv)),
                  pl.BlockSpec((na, HEAD), lambda kv, i: (0, 10 + kv)), _window_bias_spec(n // tq), _ANY],
        out_specs=(blk, blk),
        compiler_params=_params(_mb(32)),
    )(sink, t_all, t_all, t_all, bias, after)


def _attn_global_fwd(t_all, o_part):
    na = t_all.shape[0]
    n = na - CTX
    tq = 256

    def body(q_ref, k_ref, v_ref, o_in_ref, o_ref, p_ref, linv_ref):
        kk, vv = k_ref[...], v_ref[...]
        for g in range(4):
            q = q_ref[:, g * HEAD:(g + 1) * HEAD]
            t = lax.dot_general(q, kk, _NT, preferred_element_type=F32) * QK_LOG2
            m = jnp.max(t, axis=-1, keepdims=True)
            p = jnp.exp2(t - m)
            linv = 1.0 / jnp.sum(p, axis=-1, keepdims=True)
            pb = p.astype(BF16)
            p_ref[g] = pb
            o_ref[:, g * HEAD:(g + 1) * HEAD] = jnp.dot(pb, vv, preferred_element_type=F32) * linv
            linv_ref[:, g * HEAD:(g + 1) * HEAD] = jnp.broadcast_to(linv, (tq, HEAD))

    return _pallas(
        body, name="attn_global_fwd", grid=(2, n // tq),
        out_shape=(jax.ShapeDtypeStruct((n, 16 * HEAD), F32), jax.ShapeDtypeStruct((8, n, na), BF16),
                   jax.ShapeDtypeStruct((n, 8 * HEAD), F32)),
        in_specs=[pl.BlockSpec((tq, 4 * HEAD), lambda kv, i: (i, 3 + kv)),
                  pl.BlockSpec((na, HEAD), lambda kv, i: (0, 20 + kv)),
                  pl.BlockSpec((na, HEAD), lambda kv, i: (0, 22 + kv)), _ANY],
        out_specs=(pl.BlockSpec((tq, 4 * HEAD), lambda kv, i: (i, 2 + kv)),
                   pl.BlockSpec((4, tq, na), lambda kv, i: (kv, i, 0)),
                   pl.BlockSpec((tq, 4 * HEAD), lambda kv, i: (i, kv))),
        input_output_aliases={3: 0},
        compiler_params=_params(_mb(56)),
    )(t_all, t_all, t_all, o_part)


def _attn_window_bwd(t_all, o, do, lse, sink, bias):
    na = t_all.shape[0]
    n = na - CTX
    tq = WINDOW

    def body(sink_ref, q_ref, k_ref, v_ref, o_ref, do_ref, lse_ref, bias_ref, dq_ref, dk_ref, dv_ref, dsink_ref):
        kv = pl.program_id(0)

        @pl.when(pl.program_id(1) == 0)
        def _():
            dk_ref[...] = jnp.zeros_like(dk_ref)
            dv_ref[...] = jnp.zeros_like(dv_ref)
            dsink_ref[...] = jnp.zeros_like(dsink_ref)

        kk, vv, start = _window_keys(k_ref, v_ref, n, na)
        q = _stack_heads(q_ref)
        t = lax.dot_general(q, kk, _NT, preferred_element_type=F32) * QK_LOG2 + bias_ref[0]
        lse = _stack_heads(lse_ref, 1)
        p = jnp.exp2(t - lse)
        dof = _stack_heads(do_ref)
        delta = jnp.sum(dof * _stack_heads(o_ref), axis=-1, keepdims=True)
        dob = dof.astype(BF16)
        dv_acc = lax.dot_general(p.astype(BF16), dob, _TN, preferred_element_type=F32)
        dp = lax.dot_general(dob, vv, _NT, preferred_element_type=F32)
        ds = (p * (dp - delta) * SCALE).astype(BF16)
        dq = jnp.dot(ds, kk, preferred_element_type=F32)
        dk_acc = lax.dot_general(ds, q, _TN, preferred_element_type=F32)
        dsk = -(jnp.exp2(_sink_column(sink_ref, kv, tq) - lse) * delta)
        for g in range(4):
            dq_ref[:, g * HEAD:(g + 1) * HEAD] = dq[g * tq:(g + 1) * tq]
            dsink_ref[0, g:g + 1, :] += jnp.broadcast_to(_colsum(dsk[g * tq:(g + 1) * tq]), (1, HEAD))
        dk_ref[pl.ds(start, 3 * tq), :] += dk_acc[:3 * tq]
        dv_ref[pl.ds(start, 3 * tq), :] += dv_acc[:3 * tq]
        dk_ref[n:na, :] += dk_acc[3 * tq:]
        dv_ref[n:na, :] += dv_acc[3 * tq:]

    blk = pl.BlockSpec((tq, 4 * HEAD), lambda kv, i: (i, kv))
    kvout = pl.BlockSpec((na, HEAD), lambda kv, i: (0, kv))
    return _pallas(
        body, name="attn_window_bwd", grid=(2, n // tq),
        out_shape=(jax.ShapeDtypeStruct((n, 8 * HEAD), F32), jax.ShapeDtypeStruct((na, 2 * HEAD), F32),
                   jax.ShapeDtypeStruct((na, 2 * HEAD), F32), jax.ShapeDtypeStruct((2, 8, HEAD), F32)),
        in_specs=[pl.BlockSpec(memory_space=pltpu.SMEM), blk,
                  pl.BlockSpec((na, HEAD), lambda kv, i: (0, 8 + kv)),
                  pl.BlockSpec((na, HEAD), lambda kv, i: (0, 10 + kv)),
                  blk, blk, blk, _window_bias_spec(n // tq)],
        out_specs=(blk, kvout, kvout, pl.BlockSpec((1, 8, HEAD), lambda kv, i: (kv, 0, 0))),
        compiler_params=_params(_mb(40)),
    )(sink, t_all, t_all, t_all, o, do, lse, bias)


def _attn_global_bwd(t_all, kt, o, do, p_all, linv):
    na = t_all.shape[0]
    n = na - CTX
    tq = 256

    def body(q_ref, v_ref, kt_ref, o_ref, do_ref, p_ref, linv_ref, dq_ref, dk_ref, dv_ref, dkt_acc, dvt_acc):
        i = pl.program_id(1)

        @pl.when(i == 0)
        def _():
            dkt_acc[...] = jnp.zeros_like(dkt_acc)
            dvt_acc[...] = jnp.zeros_like(dvt_acc)

        vv, kt_v = v_ref[...], kt_ref[...]
        dkt = jnp.zeros((HEAD, na), F32)
        dvt = jnp.zeros((HEAD, na), F32)
        for g in range(4):
            q = q_ref[:, g * HEAD:(g + 1) * HEAD]
            p = p_ref[g].astype(F32) * linv_ref[:, g * HEAD:g * HEAD + 1]
            dof = do_ref[:, g * HEAD:(g + 1) * HEAD]
            delta = jnp.sum(dof * o_ref[:, g * HEAD:(g + 1) * HEAD], axis=-1, keepdims=True)
            dob = dof.astype(BF16)
            dvt = dvt + lax.dot_general(dob, p.astype(BF16), _TN, preferred_element_type=F32)
            dp = lax.dot_general(dob, vv, _NT, preferred_element_type=F32)
            ds = (p * (dp - delta) * SCALE).astype(BF16)
            dq_ref[:, g * HEAD:(g + 1) * HEAD] = lax.dot_general(kt_v, ds, _NT, preferred_element_type=F32).T
            dkt = dkt + lax.dot_general(q, ds, _TN, preferred_element_type=F32)
        dkt_acc[...] += dkt
        dvt_acc[...] += dvt

        @pl.when(i == pl.num_programs(1) - 1)
        def _():
            dk_ref[...] = dkt_acc[...].T
            dv_ref[...] = dvt_acc[...].T

    ospec = pl.BlockSpec((tq, 4 * HEAD), lambda kv, i: (i, 2 + kv))
    lspec = pl.BlockSpec((tq, 4 * HEAD), lambda kv, i: (i, kv))
    kvout = pl.BlockSpec((na, HEAD), lambda kv, i: (0, kv))
    return _pallas(
        body, name="attn_global_bwd", grid=(2, n // tq),
        out_shape=(jax.ShapeDtypeStruct((n, 8 * HEAD), F32), jax.ShapeDtypeStruct((na, 2 * HEAD), F32),
                   jax.ShapeDtypeStruct((na, 2 * HEAD), F32)),
        in_specs=[pl.BlockSpec((tq, 4 * HEAD), lambda kv, i: (i, 3 + kv)),
                  pl.BlockSpec((na, HEAD), lambda kv, i: (0, 22 + kv)),
                  pl.BlockSpec((HEAD, na), lambda kv, i: (kv, 0)),
                  ospec, ospec, pl.BlockSpec((4, tq, na), lambda kv, i: (kv, i, 0)), lspec],
        out_specs=(lspec, kvout, kvout),
        scratch_shapes=[pltpu.VMEM((HEAD, na), F32), pltpu.VMEM((HEAD, na), F32)],
        compiler_params=_params(_mb(56)),
    )(t_all, t_all, kt, o, do, p_all, linv)


def _outproj_ln1(o, wout, x, g1, lg, lb, sc2, sh2, after):
    n, d = x.shape
    tm = 256

    def body(o_ref, w_ref, x_ref, g1_ref, lg_ref, lb_ref, sc_ref, sh_ref, after_ref, a_ref, xh_ref, rs_ref, u_ref):
        a1 = jnp.dot(o_ref[...].astype(BF16), w_ref[...], preferred_element_type=F32)
        a_ref[...] = a1
        r = ALPHA * x_ref[...] + g1_ref[...] * a1
        dlt = r - _rowmean(r)
        rstd = lax.rsqrt(_rowmean(dlt * dlt) + EPS)
        xh = dlt * rstd
        xh_ref[...] = xh
        rs_ref[...] = rstd
        x1 = xh * lg_ref[...] + lb_ref[...]
        u_ref[...] = (x1 * (1.0 + sc_ref[...]) + sh_ref[...]).astype(BF16)

    row = lambda i: (i, 0)
    const2 = lambda i: (0, 0)
    vec = pl.BlockSpec((1, d), const2)
    big = pl.BlockSpec((tm, d), row)
    return _pallas(
        body, name="outproj_ln1", grid=(n // tm,),
        out_shape=(jax.ShapeDtypeStruct((n, d), F32), jax.ShapeDtypeStruct((n, d), F32),
                   jax.ShapeDtypeStruct((n, 1), F32), jax.ShapeDtypeStruct((n, d), BF16)),
        in_specs=[big, pl.BlockSpec((d, d), const2), big, vec, vec, vec, vec, vec, _ANY],
        out_specs=(big, big, pl.BlockSpec((tm, 1), row), big),
        compiler_params=_params(_mb(56)),
    )(o, wout, x, g1, lg, lb, sc2, sh2, after)


def _ffn_up(u2, wgt, wut, after):
    n, d = u2.shape
    f = wgt.shape[0]
    tm = min(1024, n)

    def body(u_ref, wg_ref, wu_ref, after_ref, sa_ref, sb_ref, hf_ref):
        u = u_ref[...]
        gv = lax.dot_general(u, wg_ref[...], _NT, preferred_element_type=F32)
        pv = lax.dot_general(u, wu_ref[...], _NT, preferred_element_type=F32)
        sg = _sigmoid(gv)
        silu = gv * sg
        sa_ref[...] = silu.astype(BF16)
        sb_ref[...] = (pv * (sg * (1.0 + gv * (1.0 - sg)))).astype(BF16)
        hf_ref[...] = (silu * pv).astype(BF16)

    tile = pl.BlockSpec((tm, FFN_TILE), lambda i, j: (i, j))
    wspec = pl.BlockSpec((FFN_TILE, d), lambda i, j: (j, 0))
    sds = jax.ShapeDtypeStruct((n, f), BF16)
    return _pallas(
        body, name="ffn_up", grid=(n // tm, f // FFN_TILE),
        out_shape=(sds, sds, sds),
        in_specs=[pl.BlockSpec((tm, d), lambda i, j: (i, 0)), wspec, wspec, _ANY],
        out_specs=(tile, tile, tile),
        compiler_params=_params(_mb(48)),
    )(u2, wgt, wut, after)


def _ffn_down(hf, wd):
    n, f = hf.shape
    d = wd.shape[1]
    tm, tn = min(1024, n), 512

    def body(h_ref, w_ref, o_ref):
        o_ref[...] = jnp.dot(h_ref[...], w_ref[...], preferred_element_type=F32)

    return _pallas(
        body, name="ffn_down", grid=(n // tm, d // tn),
        out_shape=jax.ShapeDtypeStruct((n, d), F32),
        in_specs=[pl.BlockSpec((tm, f), lambda i, j: (i, 0)), pl.BlockSpec((f, tn), lambda i, j: (0, j))],
        out_specs=pl.BlockSpec((tm, tn), lambda i, j: (i, j)),
        compiler_params=_params(_mb(56)),
    )(hf, wd)


def _ln2_loss(xh1, ffn, tgt, lg1, lb1, g2, lg2, lb2):
    n, d = xh1.shape
    tm = 256

    def body(xh_ref, f_ref, t_ref, lg1_ref, lb1_ref, g2_ref, lg2_ref, lb2_ref, dr_ref, df_ref, loss_ref, acc_ref):
        @pl.when(pl.program_id(0) == 0)
        def _():
            loss_ref[...] = jnp.zeros_like(loss_ref)
            acc_ref[...] = jnp.zeros_like(acc_ref)

        x1 = xh_ref[...] * lg1_ref[...] + lb1_ref[...]
        fv = f_ref[...]
        r = ALPHA * x1 + g2_ref[...] * fv
        dlt = r - _rowmean(r)
        rstd = lax.rsqrt(_rowmean(dlt * dlt) + EPS)
        xh2 = dlt * rstd
        err = xh2 * lg2_ref[...] + lb2_ref[...] - t_ref[...]
        loss_ref[...] += 0.5 * jnp.sum(_rowmean(err * err))
        dy = err * (1.0 / d)
        dyg = dy * lg2_ref[...]
        dr = rstd * (dyg - _rowmean(dyg) - xh2 * _rowmean(dyg * xh2))
        dr_ref[...] = dr
        df_ref[...] = (g2_ref[...] * dr).astype(BF16)
        acc_ref[0:1, :] += _colsum(dy * xh2)
        acc_ref[1:2, :] += _colsum(dy)
        acc_ref[2:3, :] += _colsum(dr * fv)

    row = lambda i: (i, 0)
    const2 = lambda i: (0, 0)
    vec = pl.BlockSpec((1, d), const2)
    big = pl.BlockSpec((tm, d), row)
    return _pallas(
        body, name="ln2_loss", grid=(n // tm,),
        out_shape=(jax.ShapeDtypeStruct((n, d), F32), jax.ShapeDtypeStruct((n, d), BF16),
                   jax.ShapeDtypeStruct((8, HEAD), F32), jax.ShapeDtypeStruct((8, d), F32)),
        in_specs=[big, big, big, vec, vec, vec, vec, vec],
        out_specs=(big, big, pl.BlockSpec((8, HEAD), const2), pl.BlockSpec((8, d), const2)),
        compiler_params=_params(_mb(48)),
    )(xh1, ffn, tgt, lg1, lb1, g2, lg2, lb2)


def _ffn_dhf(df, wd, sa, sb):
    n, d = df.shape
    f = sa.shape[1]
    tm = min(1024, n)

    def body(df_ref, w_ref, sa_ref, sb_ref, dgp_ref):
        dhf = lax.dot_general(df_ref[...], w_ref[...], _NT, preferred_element_type=F32)
        dgp_ref[:, :FFN_TILE] = (dhf * sb_ref[...].astype(F32)).astype(BF16)
        dgp_ref[:, FFN_TILE:] = (dhf * sa_ref[...].astype(F32)).astype(BF16)

    tile = pl.BlockSpec((tm, FFN_TILE), lambda i, j: (i, j))
    return _pallas(
        body, name="ffn_dhf", grid=(n // tm, f // FFN_TILE),
        out_shape=jax.ShapeDtypeStruct((n, 2 * f), BF16),
        in_specs=[pl.BlockSpec((tm, d), lambda i, j: (i, 0)), pl.BlockSpec((FFN_TILE, d), lambda i, j: (j, 0)),
                  tile, tile],
        out_specs=pl.BlockSpec((tm, 2 * FFN_TILE), lambda i, j: (i, j)),
        compiler_params=_params(_mb(48)),
    )(df, wd, sa, sb)


def _ffn_du2(dgp, wgt, wut, after):
    n = dgp.shape[0]
    f, d = wgt.shape
    tm = min(1024, n)

    def body(dgp_ref, wg_ref, wu_ref, after_ref, o_ref):
        w = jnp.concatenate([wg_ref[...], wu_ref[...]], axis=0)
        part = jnp.dot(dgp_ref[...], w, preferred_element_type=F32)

        @pl.when(pl.program_id(1) == 0)
        def _():
            o_ref[...] = part

        @pl.when(pl.program_id(1) > 0)
        def _():
            o_ref[...] += part

    wspec = pl.BlockSpec((FFN_TILE, d), lambda i, j: (j, 0))
    return _pallas(
        body, name="ffn_du2", grid=(n // tm, f // FFN_TILE),
        out_shape=jax.ShapeDtypeStruct((n, d), F32),
        in_specs=[pl.BlockSpec((tm, 2 * FFN_TILE), lambda i, j: (i, j)), wspec, wspec, _ANY],
        out_specs=pl.BlockSpec((tm, d), lambda i, j: (i, 0)),
        compiler_params=_params(_mb(48)),
    )(dgp, wgt, wut, after)


def _dw_gate_up(dgp, u2, after):
    n, d = u2.shape
    f = dgp.shape[1] // 2
    tm = min(1024, n)

    def body(a_ref, b_ref, after_ref, og_ref, ou_ref, acc_ref):
        part = lax.dot_general(a_ref[...], b_ref[...], _TN, preferred_element_type=F32)
        i = pl.program_id(1)

        @pl.when(i == 0)
        def _():
            acc_ref[...] = part

        @pl.when(i > 0)
        def _():
            acc_ref[...] += part

        @pl.when(i == pl.num_programs(1) - 1)
        def _():
            og_ref[...] = acc_ref[:FFN_TILE].astype(BF16)
            ou_ref[...] = acc_ref[FFN_TILE:].astype(BF16)

    out = pl.BlockSpec((FFN_TILE, d), lambda j, i: (j, 0))
    sds = jax.ShapeDtypeStruct((f, d), BF16)
    return _pallas(
        body, name="dw_gate_up", grid=(f // FFN_TILE, n // tm),
        out_shape=(sds, sds),
        in_specs=[pl.BlockSpec((tm, 2 * FFN_TILE), lambda j, i: (i, j)), pl.BlockSpec((tm, d), lambda j, i: (i, 0)), _ANY],
        out_specs=(out, out),
        scratch_shapes=[pltpu.VMEM((2 * FFN_TILE, d), F32)],
        compiler_params=_params(_mb(56)),
    )(dgp, u2, after)


def _ln1_bwd(du2, dr2, xh1, rs1, a1, lg1, lb1, sc2, g1):
    n, d = du2.shape
    tm = 256

    def body(du_ref, dr2_ref, xh_ref, rs_ref, a_ref, lg_ref, lb_ref, sc_ref, g1_ref, dr1_ref, da_ref, acc_ref):
        @pl.when(pl.program_id(0) == 0)
        def _():
            acc_ref[...] = jnp.zeros_like(acc_ref)

        du = du_ref[...]
        xh = xh_ref[...]
        x1 = xh * lg_ref[...] + lb_ref[...]
        dx1 = ALPHA * dr2_ref[...] + du * (1.0 + sc_ref[...])
        dxg = dx1 * lg_ref[...]
        dr1 = rs_ref[...] * (dxg - _rowmean(dxg) - xh * _rowmean(dxg * xh))
        dr1_ref[...] = dr1
        da_ref[...] = (g1_ref[...] * dr1).astype(BF16)
        acc_ref[0:1, :] += _colsum(du * x1)
        acc_ref[1:2, :] += _colsum(du)
        acc_ref[2:3, :] += _colsum(dx1 * xh)
        acc_ref[3:4, :] += _colsum(dx1)
        acc_ref[4:5, :] += _colsum(dr1 * a_ref[...])

    row = lambda i: (i, 0)
    const2 = lambda i: (0, 0)
    vec = pl.BlockSpec((1, d), const2)
    big = pl.BlockSpec((tm, d), row)
    return _pallas(
        body, name="ln1_bwd", grid=(n // tm,),
        out_shape=(jax.ShapeDtypeStruct((n, d), F32), jax.ShapeDtypeStruct((n, d), BF16),
                   jax.ShapeDtypeStruct((8, d), F32)),
        in_specs=[big, big, big, pl.BlockSpec((tm, 1), row), big, vec, vec, vec, vec],
        out_specs=(big, big, pl.BlockSpec((8, d), const2)),
        compiler_params=_params(_mb(48)),
    )(du2, dr2, xh1, rs1, a1, lg1, lb1, sc2, g1)


def _dw_rows(a, b, nblk, bw, tm, after, name):
    m = a.shape[0]
    nn = b.shape[1]

    def body(a_ref, b_ref, after_ref, o_ref, acc_ref):
        part = lax.dot_general(a_ref[...].astype(BF16), b_ref[...], _TN, preferred_element_type=F32)
        i = pl.program_id(1)

        @pl.when(i == 0)
        def _():
            acc_ref[...] = part

        @pl.when(i > 0)
        def _():
            acc_ref[...] += part

        @pl.when(i == pl.num_programs(1) - 1)
        def _():
            o_ref[0] = acc_ref[...].astype(BF16)

    return _pallas(
        body, name=name, grid=(nblk, m // tm),
        out_shape=jax.ShapeDtypeStruct((nblk, bw, nn), BF16),
        in_specs=[pl.BlockSpec((tm, bw), lambda j, i: (i, j)), pl.BlockSpec((tm, nn), lambda j, i: (i, 0)), _ANY],
        out_specs=pl.BlockSpec((1, bw, nn), lambda j, i: (j, 0, 0)),
        scratch_shapes=[pltpu.VMEM((bw, nn), F32)],
        compiler_params=_params(_mb(56)),
    )(a, b, after)


def _outproj_bwd(da1, wout, after):
    n, d = da1.shape
    tm = 512

    def body(a_ref, w_ref, after_ref, o_ref):
        o_ref[...] = lax.dot_general(a_ref[...], w_ref[...], _NT, preferred_element_type=F32)

    return _pallas(
        body, name="outproj_bwd", grid=(n // tm,),
        out_shape=jax.ShapeDtypeStruct((n, d), F32),
        in_specs=[pl.BlockSpec((tm, d), lambda i: (i, 0)), pl.BlockSpec((d, d), lambda i: (0, 0)), _ANY],
        out_specs=pl.BlockSpec((tm, d), lambda i: (i, 0)),
        compiler_params=_params(_mb(48)),
    )(da1, wout, after)


def _qkv_bwd(dh, wint, x, ct, dr1, sc):
    na, wcols = dh.shape
    n, d = x.shape
    tm = CTX
    nlat = n // tm

    def body(dh_ref, w_ref, x_ref, ct_ref, dr_ref, sc_ref, gx_ref, acc_ref):
        i = pl.program_id(0)

        @pl.when(i == 0)
        def _():
            acc_ref[...] = jnp.zeros_like(acc_ref)

        du = jnp.dot(dh_ref[...], w_ref[...], preferred_element_type=F32)

        @pl.when(i < nlat)
        def _():
            gx_ref[...] = ALPHA * dr_ref[...] + du * (1.0 + sc_ref[0])
            acc_ref[0:1, :] += _colsum(du)
            acc_ref[1:2, :] += _colsum(du * x_ref[...])

        @pl.when(i == nlat)
        def _():
            acc_ref[2:3, :] += _colsum(du)
            acc_ref[3:4, :] += _colsum(du * ct_ref[...])

    lat = lambda i: (jnp.minimum(i, nlat - 1), 0)
    const2 = lambda i: (0, 0)
    return _pallas(
        body, name="qkv_bwd", grid=(nlat + 1,),
        out_shape=(jax.ShapeDtypeStruct((n, d), F32), jax.ShapeDtypeStruct((8, d), F32)),
        in_specs=[pl.BlockSpec((tm, wcols), lambda i: (i, 0)), pl.BlockSpec((wcols, d), const2),
                  pl.BlockSpec((tm, d), lat), pl.BlockSpec((tm, d), const2), pl.BlockSpec((tm, d), lat),
                  pl.BlockSpec((1, 1, d), lambda i: (0, 0, 0))],
        out_specs=(pl.BlockSpec((tm, d), lat), pl.BlockSpec((8, d), const2)),
        compiler_params=_params(_mb(56)),
    )(dh, wint, x, ct, dr1, sc)


def _adam_math(w, g, m, v):
    m2 = ADAM_B1 * m + (1.0 - ADAM_B1) * g
    v2 = ADAM_B2 * v + (1.0 - ADAM_B2) * (g * g)
    m_hat = m2 * (1.0 / (1.0 - ADAM_B1 ** ADAM_STEP))
    v_hat = v2 * (1.0 / (1.0 - ADAM_B2 ** ADAM_STEP))
    delta = -ADAM_LR * (m_hat / (jnp.sqrt(v_hat) + ADAM_EPS) + ADAM_WD * w)
    return delta, m2, v2


def _adamw(w, gsrc, m, v, name):
    r, c = w.shape
    parts = gsrc.ndim == 3
    cg = gsrc.shape[-1]
    tr = r
    while tr * c * 4 > _mb(1) and tr % 32 == 0:
        tr //= 2

    def body(w_ref, g_ref, m_ref, v_ref, go_ref, d_ref, mo_ref, vo_ref):
        if parts:
            g = g_ref[0].astype(F32)
            for s in range(1, NDEV):
                g = g + g_ref[s].astype(F32)
            g = g[:, :c]
        else:
            g = g_ref[...]
        delta, m2, v2 = _adam_math(w_ref[...], g, m_ref[...], v_ref[...])
        go_ref[...] = g
        d_ref[...] = delta
        mo_ref[...] = m2
        vo_ref[...] = v2

    tile = pl.BlockSpec((tr, c), lambda i: (i, 0))
    gspec = pl.BlockSpec((NDEV, tr, cg), lambda i: (0, i, 0)) if parts else tile
    sds = jax.ShapeDtypeStruct((r, c), F32)
    return _pallas(
        body, name=name, grid=(r // tr,),
        out_shape=(sds, sds, sds, sds),
        in_specs=[tile, gspec, tile, tile],
        out_specs=(tile, tile, tile, tile),
        compiler_params=_params(_mb(48)),
    )(w, gsrc, m, v)


def _adamw_t(w, gsrc_t, m, v, name):
    r, c = w.shape
    tr = 256

    def body(w_ref, g_ref, m_ref, v_ref, go_ref, d_ref, mo_ref, vo_ref):
        gt = g_ref[0].astype(F32)
        for s in range(1, NDEV):
            gt = gt + g_ref[s].astype(F32)
        g = gt.T
        delta, m2, v2 = _adam_math(w_ref[...], g, m_ref[...], v_ref[...])
        go_ref[...] = g
        d_ref[...] = delta
        mo_ref[...] = m2
        vo_ref[...] = v2

    tile = pl.BlockSpec((tr, c), lambda i: (i, 0))
    sds = jax.ShapeDtypeStruct((r, c), F32)
    return _pallas(
        body, name=name, grid=(r // tr,),
        out_shape=(sds, sds, sds, sds),
        in_specs=[tile, pl.BlockSpec((NDEV, c, tr), lambda i: (0, 0, i)), tile, tile],
        out_specs=(tile, tile, tile, tile),
        compiler_params=_params(_mb(48)),
    )(w, gsrc_t, m, v)


def _small_update(gath, dcc, cc, w_s, m_s, v_s):
    d = w_s.shape[1]

    def body(g_ref, dcc_ref, cc_ref, w_ref, m_ref, v_ref, go_ref, d_ref, mo_ref, vo_ref):
        s = g_ref[0]
        for b in range(1, NDEV):
            s = s + g_ref[b]
        dsl = dcc_ref[0, 8:9, :]
        for b in range(1, NDEV):
            dsl = dsl + dcc_ref[b, 8:9, :]
        cv = cc_ref[...]
        sg = _sigmoid(cv)
        go_ref[...] = jnp.zeros_like(go_ref)
        go_ref[0:1, :] = dsl * (sg * (1.0 + cv * (1.0 - sg)))
        go_ref[1:3, :] = s[0:2] + s[6:8]
        go_ref[3:7, :] = s[2:6]
        go_ref[7:12, :] = s[8:13]
        delta, m2, v2 = _adam_math(w_ref[...], go_ref[...], m_ref[...], v_ref[...])
        d_ref[...] = delta
        mo_ref[...] = m2
        vo_ref[...] = v2

    full = pl.BlockSpec((16, d), lambda: (0, 0))
    g3 = pl.BlockSpec((NDEV, 16, d), lambda: (0, 0, 0))
    sds = jax.ShapeDtypeStruct((16, d), F32)
    return _pallas(
        body, name="small_update",
        out_shape=(sds, sds, sds, sds),
        in_specs=[g3, g3, pl.BlockSpec((1, d), lambda: (0, 0)), full, full, full],
        out_specs=(full, full, full, full),
        compiler_params=_params(_mb(24)),
    )(gath, dcc, cc, w_s, m_s, v_s)


def _rope_tables(n):
    rows = n // GRID_W
    row_ids = jnp.repeat(jnp.arange(rows, dtype=F32), GRID_W)
    col_ids = jnp.tile(jnp.arange(GRID_W, dtype=F32), rows)
    axis_dim = HEAD // 2
    inv_freq = jnp.power(ROPE_THETA, -jnp.arange(0, axis_dim, 2, dtype=F32) / axis_dim)
    ang_r = row_ids[:, None] * inv_freq
    ang_c = col_ids[:, None] * inv_freq
    ang = jnp.concatenate([ang_r, ang_r, ang_c, ang_c], axis=-1)
    cos, sin = jnp.cos(ang), jnp.sin(ang)
    first = (jnp.arange(HEAD) % (HEAD // 2)) < HEAD // 4
    sa = jnp.where(first, -sin, 0.0)
    sb = jnp.where(first, 0.0, sin)
    ones = jnp.ones((CTX, HEAD), F32)
    zeros = jnp.zeros((CTX, HEAD), F32)
    return (jnp.concatenate([cos, ones], 0), jnp.concatenate([sa, zeros], 0), jnp.concatenate([sb, zeros], 0))


def _pad_cols(a, width):
    return jnp.pad(a, ((0, 0), (0, width - a.shape[1])))


def _pad_rows(a, rows):
    return jnp.pad(a, ((0, rows - a.shape[0]), (0, 0)))


def _pack_small(c_ctx, b_ada, ln1_g, ln1_b, ln2_g, ln2_b, qg, kg, sink, d):
    misc = _pad_cols(jnp.concatenate([qg, kg, sink], axis=1), d)
    rows = jnp.concatenate([c_ctx.reshape(1, d), b_ada.reshape(6, d), ln1_g, ln1_b, ln2_g, ln2_b, misc], axis=0)
    return _pad_rows(rows, 16)


def _unpack_small(p, d):
    return dict(c_ctx=p[0], b_ada=p[1:7].reshape(1, 6 * d), ln1_g=p[7:8], ln1_b=p[8:9], ln2_g=p[9:10], ln2_b=p[10:11],
                q_norm_g=p[11:12, 0:HEAD], k_norm_g=p[11:12, HEAD:2 * HEAD], sink_logit=p[11:12, 2 * HEAD:2 * HEAD + 8])


def kernel(x, c, ctx, c_ctx, w_ada, b_ada, w_in, q_norm_g, k_norm_g, sink_logit, w_out, ln1_g, ln1_b, w_gate, w_up, w_down, ln2_g, ln2_b, loss_target, m_c_ctx, m_w_ada, m_b_ada, m_w_in, m_q_norm_g, m_k_norm_g, m_sink_logit, m_w_out, m_ln1_g, m_ln1_b, m_w_gate, m_w_up, m_w_down, m_ln2_g, m_ln2_b, v_c_ctx, v_w_ada, v_b_ada, v_w_in, v_q_norm_g, v_k_norm_g, v_sink_logit, v_w_out, v_ln1_g, v_ln1_b, v_w_gate, v_w_up, v_w_down, v_ln2_g, v_ln2_b):
    xs, cts, tgt = x[0], ctx[0], loss_target[0]
    n, d = xs.shape
    assert cts.shape == (CTX, d) and w_in.shape[2] == IN_SHARD and w_gate.shape[2] == FFN_SHARD
    me = 4 * lax.axis_index("x") + 2 * lax.axis_index("y") + lax.axis_index("c")
    e_sh = w_ada.shape[2]

    c_g = _exchange(_pad_rows(c, 8), False, "gather_c")
    c_all = jnp.concatenate([c_g[:, 0, :], _pad_rows(c_ctx.reshape(1, d), 8)], axis=0)
    bias_sh = lax.dynamic_slice(b_ada, (0, me * e_sh), (1, e_sh))
    mods_g = _exchange(_ada_fwd(c_all, w_ada[0], bias_sh), False, "gather_mods")
    mods = jnp.transpose(mods_g, (1, 0, 2)).reshape(16, NDEV * e_sh)
    mine = lax.dynamic_slice(mods, (me, 0), (1, 6 * d))
    sh1, sc1, g1, sh2, sc2, g2 = [mine[:, k * d:(k + 1) * d] for k in range(6)]
    csh1, csc1 = mods[8:9, 0:d], mods[8:9, d:2 * d]
    sc_pair = jnp.stack([sc1, csc1])
    sh_pair = jnp.stack([sh1, csh1])

    h_win, tok = _exchange_start(w_in[0].T.astype(BF16), "chip", mods, "gather_w_in_start")
    tok, (wo_l, wg_l, wu_l, wd_l) = lax.optimization_barrier((tok, (w_out, w_gate, w_up, w_down)))
    h_wout, tok = _exchange_start(wo_l[0].astype(BF16), "chip", tok, "gather_w_out_start")
    h_wg, tok = _exchange_start(wg_l[0].T.astype(BF16), "chip", tok, "gather_w_gate_start")
    h_wu, tok = _exchange_start(wu_l[0].T.astype(BF16), "chip", tok, "gather_w_up_start")
    h_wd, tok = _exchange_start(wd_l[0].astype(BF16), "chip", tok, "gather_w_down_start")

    cos, sa, sb = _rope_tables(n)
    f_win, tok = _forward_start(_exchange_wait(h_win, "chip", tok, "gather_w_in_wait"), tok, "forward_w_in_start")
    win_g = _forward_wait(f_win, tok, "forward_w_in_wait").reshape(NDEV * IN_SHARD, d)
    u_all, h_all, t_all, kt_b = _qkv_fwd(xs, cts, sc_pair, sh_pair, win_g, q_norm_g, k_norm_g, cos, sa, sb)
    f_wout, tok = _forward_start(_exchange_wait(h_wout, "chip", t_all, "gather_w_out_wait"), t_all, "forward_w_out_start")
    win_bias = _window_bias()
    o_a, lse_a = _attn_window_fwd(t_all, sink_logit, win_bias, tok)
    o, p_b, linv_b = _attn_global_fwd(t_all, o_a)
    f_wg, tok = _forward_start(_exchange_wait(h_wg, "chip", o, "gather_w_gate_wait"), o, "forward_w_gate_start")
    f_wu, tok = _forward_start(_exchange_wait(h_wu, "chip", tok, "gather_w_up_wait"), tok, "forward_w_up_start")
    wout_g = _forward_wait(f_wout, tok, "forward_w_out_wait").reshape(d, d)
    a1, xh1, rs1, u2 = _outproj_ln1(o, wout_g, xs, g1, ln1_g, ln1_b, sc2, sh2, tok)
    f_wd, tok = _forward_start(_exchange_wait(h_wd, "chip", rs1, "gather_w_down_wait"), rs1, "forward_w_down_start")
    ffn_w = (NDEV * FFN_SHARD, d)
    wg_g = _forward_wait(f_wg, tok, "forward_w_gate_wait").reshape(ffn_w)
    wu_g = _forward_wait(f_wu, tok, "forward_w_up_wait").reshape(ffn_w)
    sa_f, sb_f, hf = _ffn_up(u2, wg_g, wu_g, tok)
    wd_g = _forward_wait(f_wd, hf, "forward_w_down_wait").reshape(ffn_w)
    ffn = _ffn_down(hf, wd_g)
    dr2, df, loss_p, acc2 = _ln2_loss(xh1, ffn, tgt, ln1_g, ln1_b, g2, ln2_g, ln2_b)
    loss = lax.psum(loss_p[0, 0], ("x", "y", "c"))

    tk = min(n, 2048)
    parts = (NDEV, FFN_SHARD, d)
    dgp = _ffn_dhf(df, wd_g, sa_f, sb_f)
    dwd_p = _dw_rows(hf, df, NDEV // 2, FFN_PAIR, min(n, 1024), loss_p, "dw_down").reshape(parts)
    h_dwd, tok = _exchange_start(dwd_p, "scatter", loss.reshape(1, 1), "scatter_dw_down_start")
    dwg_t, dwu_t = _dw_gate_up(dgp, u2, tok)
    h_dwg, tok = _exchange_start(dwg_t.reshape(parts), "scatter", tok, "scatter_dw_gate_start")
    h_dwu, tok = _exchange_start(dwu_t.reshape(parts), "scatter", tok, "scatter_dw_up_start")
    du2 = _ffn_du2(dgp, wg_g, wu_g, tok)
    dr1, da1, acc1 = _ln1_bwd(du2, dr2, xh1, rs1, a1, ln1_g, ln1_b, sc2, g1)
    dwo_p = _dw_rows(o, da1, NDEV, 2 * HEAD, tk, loss_p, "dw_out")
    h_dwo, tok = _exchange_start(dwo_p, "scatter", loss_p, "scatter_dw_out_start")
    do = _outproj_bwd(da1, wout_g, tok)
    dqa, dka, dva, dsink = _attn_window_bwd(t_all, o, do, lse_a, sink_logit, win_bias)
    dqb, dkb, dvb = _attn_global_bwd(t_all, kt_b, o, do, p_b, linv_b)
    dh_all, dnorm = _qkv_bwd_prep(dqa, dka, dva, dqb, dkb, dvb, h_all, q_norm_g, k_norm_g, cos, sa, sb)
    grad_x, acc0 = _qkv_bwd(dh_all, win_g, xs, cts, dr1, sc_pair)

    misc = _pad_cols(jnp.concatenate([dnorm[0:1], dnorm[1:2], dsink[:, 0:4, 0].reshape(1, 8)], axis=1), d)
    part = jnp.concatenate([
        acc0[0:2], acc1[4:5], acc1[1:2], acc1[0:1], acc2[2:3],
        acc0[2:4],
        acc1[2:4], acc2[0:2],
        misc, jnp.zeros((3, d), F32)], axis=0)
    gath = _exchange(part, False, "gather_small")
    dm_batch = gath[:, 0:6, :].reshape(NDEV, 6 * d)
    dm_ctx = _pad_cols(gath[:, 6:8, :].reshape(NDEV, 2 * d), 6 * d)
    dm16 = lax.dynamic_slice(jnp.concatenate([dm_batch, dm_ctx], axis=0), (0, me * e_sh), (16, e_sh))
    dw_ada, drow = _ada_bwd(dm16, c_all, w_ada[0])
    dcc = _exchange(drow, False, "gather_dcc")
    dwi_p = _dw_rows(dh_all, u_all, NDEV, IN_SHARD, (n + CTX) // 2, dcc, "dw_in")
    h_dwi, tok = _exchange_start(dwi_p, "scatter", dcc, "scatter_dw_in_start")

    w_s = _pack_small(c_ctx, b_ada, ln1_g, ln1_b, ln2_g, ln2_b, q_norm_g, k_norm_g, sink_logit, d)
    m_s = _pack_small(m_c_ctx, m_b_ada, m_ln1_g, m_ln1_b, m_ln2_g, m_ln2_b, m_q_norm_g, m_k_norm_g, m_sink_logit, d)
    v_s = _pack_small(v_c_ctx, v_b_ada, v_ln1_g, v_ln1_b, v_ln2_g, v_ln2_b, v_q_norm_g, v_k_norm_g, v_sink_logit, d)
    small = [_unpack_small(p, d) for p in _small_update(gath, dcc, c_ctx.reshape(1, d), w_s, m_s, v_s)]

    big = {}
    big["w_ada"] = _adamw(w_ada[0], dw_ada, m_w_ada[0], v_w_ada[0], "adamw_w_ada")
    late = tok
    big["w_down"] = _adamw(w_down[0], _exchange_wait(h_dwd, "scatter", late, "scatter_dw_down_wait"), m_w_down[0],
                           v_w_down[0], "adamw_w_down")
    for nm, wt, mt, vt, hd in (("w_gate", w_gate, m_w_gate, v_w_gate, h_dwg), ("w_up", w_up, m_w_up, v_w_up, h_dwu)):
        big[nm] = _adamw_t(wt[0], _exchange_wait(hd, "scatter", late, "scatter_d" + nm + "_wait"), mt[0], vt[0],
                           "adamw_" + nm)
    big["w_out"] = _adamw(w_out[0], _exchange_wait(h_dwo, "scatter", late, "scatter_dw_out_wait"), m_w_out[0], v_w_out[0],
                          "adamw_w_out")
    big["w_in"] = _adamw_t(w_in[0], _exchange_wait(h_dwi, "scatter", big["w_out"][1], "scatter_dw_in_wait"), m_w_in[0],
                           v_w_in[0], "adamw_w_in")

    names = ["c_ctx", "w_ada", "b_ada", "w_in", "q_norm_g", "k_norm_g", "sink_logit", "w_out", "ln1_g", "ln1_b",
             "w_gate", "w_up", "w_down", "ln2_g", "ln2_b"]
    outs = [loss, grad_x[None]]
    for k in range(4):
        for nm in names:
            outs.append(big[nm][k][None] if nm in big else small[k][nm])
    return tuple(outs)
```

```python
import functools

import jax
import jax.numpy as jnp
from jax import lax
from jax.experimental import pallas as pl
from jax.experimental.pallas import tpu as pltpu

F32 = jnp.float32
BF16 = jnp.bfloat16

NDEV = 8
HEAD = 128
CTX = 256
GRID_W = 64
WINDOW = 128
ROPE_THETA = 10000.0
EPS = 1e-6
SCALE = HEAD ** -0.5
LOG2E = 1.4426950408889634
QK_LOG2 = SCALE * LOG2E
ALPHA = 2.0 ** 0.25
FFN_SHARD = 704
FFN_TILE = 512
FFN_PAIR = 2 * FFN_SHARD
IN_SHARD = 384
NEG = -1e30

ADAM_LR = 0.001
ADAM_B1 = 0.9
ADAM_B2 = 0.999
ADAM_EPS = 1e-08
ADAM_WD = 0.01
ADAM_STEP = 10

VMEM_CAP = 56 * 1024 * 1024

_KINDS = ["rope"] * 10 + ["none"] * 2 + ["qnorm"] * 8 + ["knorm"] * 2 + ["none"] * 2

_NT = (((1,), (1,)), ((), ()))
_TN = (((0,), (0,)), ((), ()))


def _pallas(body, **kw):
    return pl.pallas_call(body, **kw)


def _params(vmem_bytes):
    return pltpu.CompilerParams(vmem_limit_bytes=int(min(VMEM_CAP, vmem_bytes)))


def _mb(n):
    return int(n * 1024 * 1024)


def _sigmoid(x):
    return 1.0 / (1.0 + jnp.exp(-x))


def _colsum(a):
    return jnp.sum(a, axis=0, keepdims=True)


def _rowmean(a):
    return jnp.mean(a, axis=-1, keepdims=True)


def _exchange(src, scatter, name, after=None):
    blk = src.shape[1:] if scatter else src.shape
    after = src if after is None else after

    def body(src_ref, after_ref, out_ref, send_sems, recv_sems, local_sem):
        x, y, c = lax.axis_index("x"), lax.axis_index("y"), lax.axis_index("c")
        me = 4 * x + 2 * y + c
        copies = []
        for t in range(1, NDEV):
            px = 1 - x if (t >> 2) & 1 else x
            py = 1 - y if (t >> 1) & 1 else y
            pc = 1 - c if t & 1 else c
            peer = 4 * px + 2 * py + pc
            cp = pltpu.make_async_remote_copy(
                src_ref=src_ref.at[peer] if scatter else src_ref,
                dst_ref=out_ref.at[me],
                send_sem=send_sems.at[t - 1],
                recv_sem=recv_sems.at[t - 1],
                device_id=(px, py, pc),
                device_id_type=pl.DeviceIdType.MESH,
            )
            cp.start()
            copies.append(cp)
        own = pltpu.make_async_copy(src_ref.at[me] if scatter else src_ref, out_ref.at[me], local_sem)
        own.start()
        for cp in copies:
            cp.wait()
        own.wait()

    return _pallas(
        body, name=name,
        out_shape=jax.ShapeDtypeStruct((NDEV,) + tuple(blk), src.dtype),
        in_specs=[pl.BlockSpec(memory_space=pl.ANY), pl.BlockSpec(memory_space=pl.ANY)],
        out_specs=pl.BlockSpec(memory_space=pl.ANY),
        scratch_shapes=[pltpu.SemaphoreType.DMA((NDEV - 1,)), pltpu.SemaphoreType.DMA((NDEV - 1,)),
                        pltpu.SemaphoreType.DMA(())],
    )(src, after)


_HBM = pl.BlockSpec(memory_space=pltpu.HBM)
_SEM = pl.BlockSpec(memory_space=pltpu.SEMAPHORE)
_ANY = pl.BlockSpec(memory_space=pl.ANY)
_EFFECT = pltpu.SideEffectType.DATAFLOW_SIDE_EFFECTING


def _exchange_copies(src_ref, land_ref, send_sems, recv_sems, mode):
    x, y, c = lax.axis_index("x"), lax.axis_index("y"), lax.axis_index("c")
    me = 4 * x + 2 * y + c
    scatter = mode == "scatter"
    copies = []
    for t in ((1, 2, 4, 6) if mode == "chip" else range(1, NDEV)):
        px = 1 - x if (t >> 2) & 1 else x
        py = 1 - y if (t >> 1) & 1 else y
        pc = 1 - c if t & 1 else c
        peer = 4 * px + 2 * py + pc
        copies.append(pltpu.make_async_remote_copy(
            src_ref=src_ref.at[peer] if scatter else src_ref,
            dst_ref=land_ref.at[me],
            send_sem=send_sems.at[t - 1],
            recv_sem=recv_sems.at[t - 1],
            device_id=(px, py, pc),
            device_id_type=pl.DeviceIdType.MESH,
        ))
    own = pltpu.make_async_copy(src_ref.at[me] if scatter else src_ref, land_ref.at[me], send_sems.at[NDEV - 1])
    return copies, own


def _forward_copies(land_ref, send_sems, recv_sems):
    x, y, c = lax.axis_index("x"), lax.axis_index("y"), lax.axis_index("c")
    copies = []
    for k, t in enumerate((2, 4, 6)):
        px = 1 - x if (t >> 2) & 1 else x
        py = 1 - y if (t >> 1) & 1 else y
        mine, theirs = 4 * px + 2 * py + c, 4 * px + 2 * py + (1 - c)
        send = pltpu.make_async_remote_copy(
            src_ref=land_ref.at[mine], dst_ref=land_ref.at[mine], send_sem=send_sems.at[k], recv_sem=recv_sems.at[k],
            device_id=(x, y, 1 - c), device_id_type=pl.DeviceIdType.MESH)
        recv = pltpu.make_async_remote_copy(
            src_ref=land_ref.at[theirs], dst_ref=land_ref.at[theirs], send_sem=send_sems.at[k], recv_sem=recv_sems.at[k],
            device_id=(x, y, 1 - c), device_id_type=pl.DeviceIdType.MESH)
        copies.append((send, recv))
    return copies


def _forward_start(land, after, name):
    def body(land_ref, after_ref, send_sems, recv_sems, land_thru, token):
        for send, _ in _forward_copies(land_ref, send_sems, recv_sems):
            send.start()
        token[...] = jnp.zeros_like(token)

    res = _pallas(
        body, name=name,
        out_shape=(pltpu.SemaphoreType.DMA((3,)), pltpu.SemaphoreType.DMA((3,)), pltpu.HBM(land.shape, land.dtype),
                   jax.ShapeDtypeStruct((8, HEAD), F32)),
        in_specs=(_HBM, _ANY), out_specs=(_SEM, _SEM, _HBM, pl.BlockSpec(memory_space=pltpu.VMEM)),
        input_output_aliases={0: 2},
        compiler_params=pltpu.CompilerParams(has_side_effects=_EFFECT),
    )(land, after)
    return res[:3], res[3]


def _forward_wait(handle, after, name):
    send_sems, recv_sems, land_thru = handle

    def body(land_ref, send_sems, recv_sems, after_ref, got_ref):
        for send, recv in _forward_copies(land_ref, send_sems, recv_sems):
            send.wait_send()
            recv.wait_recv()

    return _pallas(
        body, name=name,
        out_shape=pltpu.HBM(land_thru.shape, land_thru.dtype),
        in_specs=(_HBM, _SEM, _SEM, _ANY), out_specs=_HBM,
        input_output_aliases={0: 0},
        compiler_params=pltpu.CompilerParams(has_side_effects=_EFFECT),
    )(land_thru, send_sems, recv_sems, after)


def _exchange_start(src, mode, after, name):
    blk = src.shape[1:] if mode == "scatter" else src.shape
    land = lax.empty((NDEV,) + tuple(blk), src.dtype)

    def body(src_ref, land_ref, after_ref, send_sems, recv_sems, src_thru, land_thru, token):
        copies, own = _exchange_copies(src_ref, land_ref, send_sems, recv_sems, mode)
        for cp in copies:
            cp.start()
        own.start()
        token[...] = jnp.zeros_like(token)

    res = _pallas(
        body, name=name,
        out_shape=(pltpu.SemaphoreType.DMA((NDEV,)), pltpu.SemaphoreType.DMA((NDEV,)),
                   pltpu.HBM(src.shape, src.dtype), pltpu.HBM(land.shape, land.dtype),
                   jax.ShapeDtypeStruct((8, HEAD), F32)),
        in_specs=(_HBM, _HBM, _ANY), out_specs=(_SEM, _SEM, _HBM, _HBM, pl.BlockSpec(memory_space=pltpu.VMEM)),
        input_output_aliases={0: 2, 1: 3},
        compiler_params=pltpu.CompilerParams(has_side_effects=_EFFECT),
    )(pltpu.with_memory_space_constraint(src, pltpu.HBM), pltpu.with_memory_space_constraint(land, pltpu.HBM), after)
    return res[:4], res[4]


def _exchange_wait(handle, mode, after, name):
    send_sems, recv_sems, src_thru, land_thru = handle

    def body(src_ref, land_ref, send_sems, recv_sems, after_ref, src_dead, got_ref):
        copies, own = _exchange_copies(src_ref, land_ref, send_sems, recv_sems, mode)
        for cp in copies:
            cp.wait_send()
            cp.wait_recv()
        own.wait()

    return _pallas(
        body, name=name,
        out_shape=(pltpu.HBM(src_thru.shape, src_thru.dtype), pltpu.HBM(land_thru.shape, land_thru.dtype)),
        in_specs=(_HBM, _HBM, _SEM, _SEM, _ANY), out_specs=(_HBM, _HBM),
        input_output_aliases={0: 0, 1: 1},
        compiler_params=pltpu.CompilerParams(has_side_effects=_EFFECT),
    )(src_thru, land_thru, send_sems, recv_sems, after)[1]


def _ada_fwd(c_all, w, bias):
    r, d = c_all.shape
    e = w.shape[1]
    tn = 512

    def body(c_ref, w_ref, b_ref, o_ref):
        cv = c_ref[...]
        s = (cv * _sigmoid(cv)).astype(BF16)
        o_ref[...] = jnp.dot(s, w_ref[...].astype(BF16), preferred_element_type=F32) + b_ref[...]

    return _pallas(
        body, name="ada_fwd", grid=(e // tn,),
        out_shape=jax.ShapeDtypeStruct((r, e), F32),
        in_specs=[pl.BlockSpec((r, d), lambda j: (0, 0)), pl.BlockSpec((d, tn), lambda j: (0, j)),
                  pl.BlockSpec((1, tn), lambda j: (0, j))],
        out_specs=pl.BlockSpec((r, tn), lambda j: (0, j)),
        compiler_params=_params(_mb(24)),
    )(c_all, w, bias)


def _ada_bwd(dm16, c_all, w):
    d, e = w.shape
    tn = 512

    def body(dm_ref, c_ref, w_ref, dw_ref, dr_ref):
        j = pl.program_id(0)
        dm = dm_ref[...]
        rid = lax.broadcasted_iota(jnp.int32, dm.shape, 0)
        ctx_sum = jnp.sum(jnp.where(rid >= 8, dm, 0.0), axis=0, keepdims=True)
        rows = jnp.where(rid < 8, dm, jnp.where(rid == 8, jnp.broadcast_to(ctx_sum, dm.shape), 0.0)).astype(BF16)
        cv = c_ref[...]
        s = (cv * _sigmoid(cv)).astype(BF16)
        dw_ref[...] = lax.dot_general(s, rows, _TN, preferred_element_type=F32)
        part = lax.dot_general(rows, w_ref[...].astype(BF16), _NT, preferred_element_type=F32)

        @pl.when(j == 0)
        def _():
            dr_ref[...] = part

        @pl.when(j > 0)
        def _():
            dr_ref[...] += part

    return _pallas(
        body, name="ada_bwd", grid=(e // tn,),
        out_shape=(jax.ShapeDtypeStruct((d, e), F32), jax.ShapeDtypeStruct((16, d), F32)),
        in_specs=[pl.BlockSpec((16, tn), lambda j: (0, j)), pl.BlockSpec((16, d), lambda j: (0, 0)),
                  pl.BlockSpec((d, tn), lambda j: (0, j))],
        out_specs=(pl.BlockSpec((d, tn), lambda j: (0, j)), pl.BlockSpec((16, d), lambda j: (0, 0))),
        compiler_params=_params(_mb(32)),
    )(dm16, c_all, w)


def _rope(v, cos, sa, sb):
    return v * cos + (pltpu.roll(v, 96, 1) * sa + pltpu.roll(v, 32, 1) * sb)


def _rope_t(dt, cos, sa, sb):
    return dt * cos + (pltpu.roll(dt * sa, 32, 1) + pltpu.roll(dt * sb, 96, 1))


def _qkv_fwd(x, ct, sc, sh, wint, qg, kg, cos, sa, sb):
    n, d = x.shape
    tm = CTX
    nlat = n // tm
    na = n + CTX
    wcols = wint.shape[0]

    def body(x_ref, ct_ref, sc_ref, sh_ref, w_ref, qg_ref, kg_ref, cos_ref, sa_ref, sb_ref, u_ref, h_ref, t_ref, kt_ref):
        i = pl.program_id(0)
        xin = jnp.where(i == nlat, ct_ref[...], x_ref[...])
        u = (xin * (1.0 + sc_ref[0]) + sh_ref[0]).astype(BF16)
        u_ref[...] = u
        cos, sa, sb = cos_ref[...], sa_ref[...], sb_ref[...]
        h = lax.dot_general(u, w_ref[...], _NT, preferred_element_type=F32)
        h_ref[...] = h
        for hd in range(24):
            v = h[:, hd * HEAD:(hd + 1) * HEAD]
            kind = _KINDS[hd]
            if kind == "qnorm":
                v = v * lax.rsqrt(_rowmean(v * v) + EPS) * qg_ref[...]
            elif kind == "knorm":
                v = v * lax.rsqrt(_rowmean(v * v) + EPS) * kg_ref[...]
            if kind != "none":
                v = _rope(v, cos, sa, sb)
            t_ref[:, hd * HEAD:(hd + 1) * HEAD] = v.astype(BF16)
            if kind == "knorm":
                kt_ref[(hd - 20) * HEAD:(hd - 19) * HEAD, :] = v.T.astype(BF16)

    lat = lambda i: (jnp.minimum(i, nlat - 1), 0)
    row = lambda i: (i, 0)
    const2 = lambda i: (0, 0)
    return _pallas(
        body, name="qkv_fwd", grid=(nlat + 1,),
        out_shape=(jax.ShapeDtypeStruct((na, d), BF16), jax.ShapeDtypeStruct((na, wcols), F32),
                   jax.ShapeDtypeStruct((na, wcols), BF16), jax.ShapeDtypeStruct((2 * HEAD, na), BF16)),
        in_specs=[pl.BlockSpec((tm, d), lat), pl.BlockSpec((tm, d), const2),
                  pl.BlockSpec((1, 1, d), lambda i: (i // nlat, 0, 0)),
                  pl.BlockSpec((1, 1, d), lambda i: (i // nlat, 0, 0)),
                  pl.BlockSpec((wcols, d), const2),
                  pl.BlockSpec((1, HEAD), const2), pl.BlockSpec((1, HEAD), const2),
                  pl.BlockSpec((tm, HEAD), row), pl.BlockSpec((tm, HEAD), row), pl.BlockSpec((tm, HEAD), row)],
        out_specs=(pl.BlockSpec((tm, d), row), pl.BlockSpec((tm, wcols), row), pl.BlockSpec((tm, wcols), row),
                   pl.BlockSpec((2 * HEAD, tm), lambda i: (0, i))),
        compiler_params=_params(_mb(56)),
    )(x, ct, sc, sh, wint, qg, kg, cos, sa, sb)


def _qkv_bwd_prep(dqa, dka, dva, dqb, dkb, dvb, h_all, qg, kg, cos, sa, sb):
    na, wcols = h_all.shape
    n = na - CTX
    tm = CTX
    nlat = n // tm

    def body(dqa_ref, dka_ref, dva_ref, dqb_ref, dkb_ref, dvb_ref, h_ref, qg_ref, kg_ref, cos_ref, sa_ref, sb_ref,
             dh_ref, dg_ref):
        i = pl.program_id(0)

        @pl.when(i == 0)
        def _():
            dg_ref[...] = jnp.zeros_like(dg_ref)

        cos, sa, sb = cos_ref[...], sa_ref[...], sb_ref[...]
        is_lat = i < nlat
        for hd in range(24):
            kind = _KINDS[hd]
            if hd < 8:
                dt = jnp.where(is_lat, dqa_ref[:, hd * HEAD:(hd + 1) * HEAD], 0.0)
            elif hd < 10:
                dt = dka_ref[:, (hd - 8) * HEAD:(hd - 7) * HEAD]
            elif hd < 12:
                dt = dva_ref[:, (hd - 10) * HEAD:(hd - 9) * HEAD]
            elif hd < 20:
                dt = jnp.where(is_lat, dqb_ref[:, (hd - 12) * HEAD:(hd - 11) * HEAD], 0.0)
            elif hd < 22:
                dt = dkb_ref[:, (hd - 20) * HEAD:(hd - 19) * HEAD]
            else:
                dt = dvb_ref[:, (hd - 22) * HEAD:(hd - 21) * HEAD]
            if kind != "none":
                dt = _rope_t(dt, cos, sa, sb)
            if kind in ("qnorm", "knorm"):
                g_ref = qg_ref if kind == "qnorm" else kg_ref
                r0 = 0 if kind == "qnorm" else 1
                xv = h_ref[:, hd * HEAD:(hd + 1) * HEAD]
                xn = xv * lax.rsqrt(_rowmean(xv * xv) + EPS)
                dg_ref[r0:r0 + 1, :] += _colsum(dt * xn)
                dxn = dt * g_ref[...]
                dt = lax.rsqrt(_rowmean(xv * xv) + EPS) * (dxn - xn * _rowmean(dxn * xn))
            dh_ref[:, hd * HEAD:(hd + 1) * HEAD] = dt.astype(BF16)

    lat = lambda i: (jnp.minimum(i, nlat - 1), 0)
    row = lambda i: (i, 0)
    const2 = lambda i: (0, 0)
    return _pallas(
        body, name="qkv_bwd_prep", grid=(nlat + 1,),
        out_shape=(jax.ShapeDtypeStruct((na, wcols), BF16), jax.ShapeDtypeStruct((8, HEAD), F32)),
        in_specs=[pl.BlockSpec((tm, 8 * HEAD), lat), pl.BlockSpec((tm, 2 * HEAD), row), pl.BlockSpec((tm, 2 * HEAD), row),
                  pl.BlockSpec((tm, 8 * HEAD), lat), pl.BlockSpec((tm, 2 * HEAD), row), pl.BlockSpec((tm, 2 * HEAD), row),
                  pl.BlockSpec((tm, wcols), row),
                  pl.BlockSpec((1, HEAD), const2), pl.BlockSpec((1, HEAD), const2),
                  pl.BlockSpec((tm, HEAD), row), pl.BlockSpec((tm, HEAD), row), pl.BlockSpec((tm, HEAD), row)],
        out_specs=(pl.BlockSpec((tm, wcols), row), pl.BlockSpec((8, HEAD), const2)),
        compiler_params=_params(_mb(40)),
    )(dqa, dka, dva, dqb, dkb, dvb, h_all, qg, kg, cos, sa, sb)


def _window_keys(k_ref, v_ref, n, na):
    i = pl.program_id(1)
    tq = WINDOW
    start = pl.multiple_of(jnp.clip((i - 1) * tq, 0, n - 3 * tq), tq)
    kk = jnp.concatenate([k_ref[pl.ds(start, 3 * tq), :], k_ref[n:na, :]], axis=0)
    vv = jnp.concatenate([v_ref[pl.ds(start, 3 * tq), :], v_ref[n:na, :]], axis=0)
    return kk, vv, start


def _window_bias():
    tq = WINDOW
    r = (jnp.arange(4 * tq) % tq)[:, None]
    c = jnp.arange(3 * tq + CTX)[None, :]
    variants = []
    for back in (0, tq, 2 * tq):
        seen = (jnp.abs(back + r - c) <= WINDOW) | (c >= 3 * tq)
        variants.append(jnp.where(seen, 0.0, NEG).astype(F32))
    return jnp.stack(variants)


def _window_bias_spec(nq):
    return pl.BlockSpec((1, 4 * WINDOW, 3 * WINDOW + CTX),
                        lambda kv, i: (jnp.where(i == 0, 0, jnp.where(i == nq - 1, 2, 1)), 0, 0))


def _stack_heads(ref, width=HEAD):
    return jnp.concatenate([ref[:, g * HEAD:g * HEAD + width] for g in range(4)], axis=0)


def _sink_column(sink_ref, kv, tq):
    grp = lax.broadcasted_iota(jnp.int32, (4 * tq, 1), 0) // tq
    col = jnp.zeros((4 * tq, 1), F32)
    for g in range(4):
        col = jnp.where(grp == g, sink_ref[0, 4 * kv + g] * LOG2E, col)
    return col


def _attn_window_fwd(t_all, sink, bias, after):
    na = t_all.shape[0]
    n = na - CTX
    tq = WINDOW

    def body(sink_ref, q_ref, k_ref, v_ref, bias_ref, after_ref, o_ref, lse_ref):
        kv = pl.program_id(0)
        kk, vv, _ = _window_keys(k_ref, v_ref, n, na)
        t = lax.dot_general(_stack_heads(q_ref), kk, _NT, preferred_element_type=F32) * QK_LOG2 + bias_ref[0]
        sk = _sink_column(sink_ref, kv, tq)
        m = jnp.maximum(jnp.max(t, axis=-1, keepdims=True), sk)
        p = jnp.exp2(t - m)
        l = jnp.sum(p, axis=-1, keepdims=True) + jnp.exp2(sk - m)
        o = jnp.dot(p.astype(BF16), vv, preferred_element_type=F32) * (1.0 / l)
        lse = m + jnp.log2(l)
        for g in range(4):
            o_ref[:, g * HEAD:(g + 1) * HEAD] = o[g * tq:(g + 1) * tq]
            lse_ref[:, g * HEAD:(g + 1) * HEAD] = jnp.broadcast_to(lse[g * tq:(g + 1) * tq], (tq, HEAD))

    blk = pl.BlockSpec((tq, 4 * HEAD), lambda kv, i: (i, kv))
    return _pallas(
        body, name="attn_window_fwd", grid=(2, n // tq),
        out_shape=(jax.ShapeDtypeStruct((n, 16 * HEAD), F32), jax.ShapeDtypeStruct((n, 8 * HEAD), F32)),
        in_specs=[pl.BlockSpec(memory_space=pltpu.SMEM), blk,
                  pl.BlockSpec((na, HEAD), lambda kv, i: (0, 8 + kv)),
                  pl.BlockSpec((na, HEAD), lambda kv, i: (0, 10 + kv)), _window_bias_spec(n // tq), _ANY],
        out_specs=(blk, blk),
        compiler_params=_params(_mb(32)),
    )(sink, t_all, t_all, t_all, bias, after)


def _attn_global_fwd(t_all, o_part):
    na = t_all.shape[0]
    n = na - CTX
    tq = 256

    def body(q_ref, k_ref, v_ref, o_in_ref, o_ref, p_ref, linv_ref):
        kk, vv = k_ref[...], v_ref[...]
        for g in range(4):
            q = q_ref[:, g * HEAD:(g + 1) * HEAD]
            t = lax.dot_general(q, kk, _NT, preferred_element_type=F32) * QK_LOG2
            m = jnp.max(t, axis=-1, keepdims=True)
            p = jnp.exp2(t - m)
            linv = 1.0 / jnp.sum(p, axis=-1, keepdims=True)
            pb = p.astype(BF16)
            p_ref[g] = pb
            o_ref[:, g * HEAD:(g + 1) * HEAD] = jnp.dot(pb, vv, preferred_element_type=F32) * linv
            linv_ref[:, g * HEAD:(g + 1) * HEAD] = jnp.broadcast_to(linv, (tq, HEAD))

    return _pallas(
        body, name="attn_global_fwd", grid=(2, n // tq),
        out_shape=(jax.ShapeDtypeStruct((n, 16 * HEAD), F32), jax.ShapeDtypeStruct((8, n, na), BF16),
                   jax.ShapeDtypeStruct((n, 8 * HEAD), F32)),
        in_specs=[pl.BlockSpec((tq, 4 * HEAD), lambda kv, i: (i, 3 + kv)),
                  pl.BlockSpec((na, HEAD), lambda kv, i: (0, 20 + kv)),
                  pl.BlockSpec((na, HEAD), lambda kv, i: (0, 22 + kv)), _ANY],
        out_specs=(pl.BlockSpec((tq, 4 * HEAD), lambda kv, i: (i, 2 + kv)),
                   pl.BlockSpec((4, tq, na), lambda kv, i: (kv, i, 0)),
                   pl.BlockSpec((tq, 4 * HEAD), lambda kv, i: (i, kv))),
        input_output_aliases={3: 0},
        compiler_params=_params(_mb(56)),
    )(t_all, t_all, t_all, o_part)


def _attn_window_bwd(t_all, o, do, lse, sink, bias):
    na = t_all.shape[0]
    n = na - CTX
    tq = WINDOW

    def body(sink_ref, q_ref, k_ref, v_ref, o_ref, do_ref, lse_ref, bias_ref, dq_ref, dk_ref, dv_ref, dsink_ref):
        kv = pl.program_id(0)

        @pl.when(pl.program_id(1) == 0)
        def _():
            dk_ref[...] = jnp.zeros_like(dk_ref)
            dv_ref[...] = jnp.zeros_like(dv_ref)
            dsink_ref[...] = jnp.zeros_like(dsink_ref)

        kk, vv, start = _window_keys(k_ref, v_ref, n, na)
        q = _stack_heads(q_ref)
        t = lax.dot_general(q, kk, _NT, preferred_element_type=F32) * QK_LOG2 + bias_ref[0]
        lse = _stack_heads(lse_ref, 1)
        p = jnp.exp2(t - lse)
        dof = _stack_heads(do_ref)
        delta = jnp.sum(dof * _stack_heads(o_ref), axis=-1, keepdims=True)
        dob = dof.astype(BF16)
        dv_acc = lax.dot_general(p.astype(BF16), dob, _TN, preferred_element_type=F32)
        dp = lax.dot_general(dob, vv, _NT, preferred_element_type=F32)
        ds = (p * (dp - delta) * SCALE).astype(BF16)
        dq = jnp.dot(ds, kk, preferred_element_type=F32)
        dk_acc = lax.dot_general(ds, q, _TN, preferred_element_type=F32)
        dsk = -(jnp.exp2(_sink_column(sink_ref, kv, tq) - lse) * delta)
        for g in range(4):
            dq_ref[:, g * HEAD:(g + 1) * HEAD] = dq[g * tq:(g + 1) * tq]
            dsink_ref[0, g:g + 1, :] += jnp.broadcast_to(_colsum(dsk[g * tq:(g + 1) * tq]), (1, HEAD))
        dk_ref[pl.ds(start, 3 * tq), :] += dk_acc[:3 * tq]
        dv_ref[pl.ds(start, 3 * tq), :] += dv_acc[:3 * tq]
        dk_ref[n:na, :] += dk_acc[3 * tq:]
        dv_ref[n:na, :] += dv_acc[3 * tq:]

    blk = pl.BlockSpec((tq, 4 * HEAD), lambda kv, i: (i, kv))
    kvout = pl.BlockSpec((na, HEAD), lambda kv, i: (0, kv))
    return _pallas(
        body, name="attn_window_bwd", grid=(2, n // tq),
        out_shape=(jax.ShapeDtypeStruct((n, 8 * HEAD), F32), jax.ShapeDtypeStruct((na, 2 * HEAD), F32),
                   jax.ShapeDtypeStruct((na, 2 * HEAD), F32), jax.ShapeDtypeStruct((2, 8, HEAD), F32)),
        in_specs=[pl.BlockSpec(memory_space=pltpu.SMEM), blk,
                  pl.BlockSpec((na, HEAD), lambda kv, i: (0, 8 + kv)),
                  pl.BlockSpec((na, HEAD), lambda kv, i: (0, 10 + kv)),
                  blk, blk, blk, _window_bias_spec(n // tq)],
        out_specs=(blk, kvout, kvout, pl.BlockSpec((1, 8, HEAD), lambda kv, i: (kv, 0, 0))),
        compiler_params=_params(_mb(40)),
    )(sink, t_all, t_all, t_all, o, do, lse, bias)


def _attn_global_bwd(t_all, kt, o, do, p_all, linv):
    na = t_all.shape[0]
    n = na - CTX
    tq = 256

    def body(q_ref, v_ref, kt_ref, o_ref, do_ref, p_ref, linv_ref, dq_ref, dk_ref, dv_ref, dkt_acc, dvt_acc):
        i = pl.program_id(1)

        @pl.when(i == 0)
        def _():
            dkt_acc[...] = jnp.zeros_like(dkt_acc)
            dvt_acc[...] = jnp.zeros_like(dvt_acc)

        vv, kt_v = v_ref[...], kt_ref[...]
        dkt = jnp.zeros((HEAD, na), F32)
        dvt = jnp.zeros((HEAD, na), F32)
        for g in range(4):
            q = q_ref[:, g * HEAD:(g + 1) * HEAD]
            p = p_ref[g].astype(F32) * linv_ref[:, g * HEAD:g * HEAD + 1]
            dof = do_ref[:, g * HEAD:(g + 1) * HEAD]
            delta = jnp.sum(dof * o_ref[:, g * HEAD:(g + 1) * HEAD], axis=-1, keepdims=True)
            dob = dof.astype(BF16)
            dvt = dvt + lax.dot_general(dob, p.astype(BF16), _TN, preferred_element_type=F32)
            dp = lax.dot_general(dob, vv, _NT, preferred_element_type=F32)
            ds = (p * (dp - delta) * SCALE).astype(BF16)
            dq_ref[:, g * HEAD:(g + 1) * HEAD] = lax.dot_general(kt_v, ds, _NT, preferred_element_type=F32).T
            dkt = dkt + lax.dot_general(q, ds, _TN, preferred_element_type=F32)
        dkt_acc[...] += dkt
        dvt_acc[...] += dvt

        @pl.when(i == pl.num_programs(1) - 1)
        def _():
            dk_ref[...] = dkt_acc[...].T
            dv_ref[...] = dvt_acc[...].T

    ospec = pl.BlockSpec((tq, 4 * HEAD), lambda kv, i: (i, 2 + kv))
    lspec = pl.BlockSpec((tq, 4 * HEAD), lambda kv, i: (i, kv))
    kvout = pl.BlockSpec((na, HEAD), lambda kv, i: (0, kv))
    return _pallas(
        body, name="attn_global_bwd", grid=(2, n // tq),
        out_shape=(jax.ShapeDtypeStruct((n, 8 * HEAD), F32), jax.ShapeDtypeStruct((na, 2 * HEAD), F32),
                   jax.ShapeDtypeStruct((na, 2 * HEAD), F32)),
        in_specs=[pl.BlockSpec((tq, 4 * HEAD), lambda kv, i: (i, 3 + kv)),
                  pl.BlockSpec((na, HEAD), lambda kv, i: (0, 22 + kv)),
                  pl.BlockSpec((HEAD, na), lambda kv, i: (kv, 0)),
                  ospec, ospec, pl.BlockSpec((4, tq, na), lambda kv, i: (kv, i, 0)), lspec],
        out_specs=(lspec, kvout, kvout),
        scratch_shapes=[pltpu.VMEM((HEAD, na), F32), pltpu.VMEM((HEAD, na), F32)],
        compiler_params=_params(_mb(56)),
    )(t_all, t_all, kt, o, do, p_all, linv)


def _outproj_ln1(o, wout, x, g1, lg, lb, sc2, sh2, after):
    n, d = x.shape
    tm = 256

    def body(o_ref, w_ref, x_ref, g1_ref, lg_ref, lb_ref, sc_ref, sh_ref, after_ref, a_ref, xh_ref, rs_ref, u_ref):
        a1 = jnp.dot(o_ref[...].astype(BF16), w_ref[...], preferred_element_type=F32)
        a_ref[...] = a1
        r = ALPHA * x_ref[...] + g1_ref[...] * a1
        dlt = r - _rowmean(r)
        rstd = lax.rsqrt(_rowmean(dlt * dlt) + EPS)
        xh = dlt * rstd
        xh_ref[...] = xh
        rs_ref[...] = rstd
        x1 = xh * lg_ref[...] + lb_ref[...]
        u_ref[...] = (x1 * (1.0 + sc_ref[...]) + sh_ref[...]).astype(BF16)

    row = lambda i: (i, 0)
    const2 = lambda i: (0, 0)
    vec = pl.BlockSpec((1, d), const2)
    big = pl.BlockSpec((tm, d), row)
    return _pallas(
        body, name="outproj_ln1", grid=(n // tm,),
        out_shape=(jax.ShapeDtypeStruct((n, d), F32), jax.ShapeDtypeStruct((n, d), F32),
                   jax.ShapeDtypeStruct((n, 1), F32), jax.ShapeDtypeStruct((n, d), BF16)),
        in_specs=[big, pl.BlockSpec((d, d), const2), big, vec, vec, vec, vec, vec, _ANY],
        out_specs=(big, big, pl.BlockSpec((tm, 1), row), big),
        compiler_params=_params(_mb(56)),
    )(o, wout, x, g1, lg, lb, sc2, sh2, after)


def _ffn_up(u2, wgt, wut, after):
    n, d = u2.shape
    f = wgt.shape[0]
    tm = min(1024, n)

    def body(u_ref, wg_ref, wu_ref, after_ref, sa_ref, sb_ref, hf_ref):
        u = u_ref[...]
        gv = lax.dot_general(u, wg_ref[...], _NT, preferred_element_type=F32)
        pv = lax.dot_general(u, wu_ref[...], _NT, preferred_element_type=F32)
        sg = _sigmoid(gv)
        silu = gv * sg
        sa_ref[...] = silu.astype(BF16)
        sb_ref[...] = (pv * (sg * (1.0 + gv * (1.0 - sg)))).astype(BF16)
        hf_ref[...] = (silu * pv).astype(BF16)

    tile = pl.BlockSpec((tm, FFN_TILE), lambda i, j: (i, j))
    wspec = pl.BlockSpec((FFN_TILE, d), lambda i, j: (j, 0))
    sds = jax.ShapeDtypeStruct((n, f), BF16)
    return _pallas(
        body, name="ffn_up", grid=(n // tm, f // FFN_TILE),
        out_shape=(sds, sds, sds),
        in_specs=[pl.BlockSpec((tm, d), lambda i, j: (i, 0)), wspec, wspec, _ANY],
        out_specs=(tile, tile, tile),
        compiler_params=_params(_mb(48)),
    )(u2, wgt, wut, after)


def _ffn_down(hf, wd):
    n, f = hf.shape
    d = wd.shape[1]
    tm, tn = min(1024, n), 512

    def body(h_ref, w_ref, o_ref):
        o_ref[...] = jnp.dot(h_ref[...], w_ref[...], preferred_element_type=F32)

    return _pallas(
        body, name="ffn_down", grid=(n // tm, d // tn),
        out_shape=jax.ShapeDtypeStruct((n, d), F32),
        in_specs=[pl.BlockSpec((tm, f), lambda i, j: (i, 0)), pl.BlockSpec((f, tn), lambda i, j: (0, j))],
        out_specs=pl.BlockSpec((tm, tn), lambda i, j: (i, j)),
        compiler_params=_params(_mb(56)),
    )(hf, wd)


def _ln2_loss(xh1, ffn, tgt, lg1, lb1, g2, lg2, lb2):
    n, d = xh1.shape
    tm = 256

    def body(xh_ref, f_ref, t_ref, lg1_ref, lb1_ref, g2_ref, lg2_ref, lb2_ref, dr_ref, df_ref, loss_ref, acc_ref):
        @pl.when(pl.program_id(0) == 0)
        def _():
            loss_ref[...] = jnp.zeros_like(loss_ref)
            acc_ref[...] = jnp.zeros_like(acc_ref)

        x1 = xh_ref[...] * lg1_ref[...] + lb1_ref[...]
        fv = f_ref[...]
        r = ALPHA * x1 + g2_ref[...] * fv
        dlt = r - _rowmean(r)
        rstd = lax.rsqrt(_rowmean(dlt * dlt) + EPS)
        xh2 = dlt * rstd
        err = xh2 * lg2_ref[...] + lb2_ref[...] - t_ref[...]
        loss_ref[...] += 0.5 * jnp.sum(_rowmean(err * err))
        dy = err * (1.0 / d)
        dyg = dy * lg2_ref[...]
        dr = rstd * (dyg - _rowmean(dyg) - xh2 * _rowmean(dyg * xh2))
        dr_ref[...] = dr
        df_ref[...] = (g2_ref[...] * dr).astype(BF16)
        acc_ref[0:1, :] += _colsum(dy * xh2)
        acc_ref[1:2, :] += _colsum(dy)
        acc_ref[2:3, :] += _colsum(dr * fv)

    row = lambda i: (i, 0)
    const2 = lambda i: (0, 0)
    vec = pl.BlockSpec((1, d), const2)
    big = pl.BlockSpec((tm, d), row)
    return _pallas(
        body, name="ln2_loss", grid=(n // tm,),
        out_shape=(jax.ShapeDtypeStruct((n, d), F32), jax.ShapeDtypeStruct((n, d), BF16),
                   jax.ShapeDtypeStruct((8, HEAD), F32), jax.ShapeDtypeStruct((8, d), F32)),
        in_specs=[big, big, big, vec, vec, vec, vec, vec],
        out_specs=(big, big, pl.BlockSpec((8, HEAD), const2), pl.BlockSpec((8, d), const2)),
        compiler_params=_params(_mb(48)),
    )(xh1, ffn, tgt, lg1, lb1, g2, lg2, lb2)


def _ffn_dhf(df, wd, sa, sb):
    n, d = df.shape
    f = sa.shape[1]
    tm = min(1024, n)

    def body(df_ref, w_ref, sa_ref, sb_ref, dgp_ref):
        dhf = lax.dot_general(df_ref[...], w_ref[...], _NT, preferred_element_type=F32)
        dgp_ref[:, :FFN_TILE] = (dhf * sb_ref[...].astype(F32)).astype(BF16)
        dgp_ref[:, FFN_TILE:] = (dhf * sa_ref[...].astype(F32)).astype(BF16)

    tile = pl.BlockSpec((tm, FFN_TILE), lambda i, j: (i, j))
    return _pallas(
        body, name="ffn_dhf", grid=(n // tm, f // FFN_TILE),
        out_shape=jax.ShapeDtypeStruct((n, 2 * f), BF16),
        in_specs=[pl.BlockSpec((tm, d), lambda i, j: (i, 0)), pl.BlockSpec((FFN_TILE, d), lambda i, j: (j, 0)),
                  tile, tile],
        out_specs=pl.BlockSpec((tm, 2 * FFN_TILE), lambda i, j: (i, j)),
        compiler_params=_params(_mb(48)),
    )(df, wd, sa, sb)


def _ffn_du2(dgp, wgt, wut, after):
    n = dgp.shape[0]
    f, d = wgt.shape
    tm = min(1024, n)

    def body(dgp_ref, wg_ref, wu_ref, after_ref, o_ref):
        w = jnp.concatenate([wg_ref[...], wu_ref[...]], axis=0)
        part = jnp.dot(dgp_ref[...], w, preferred_element_type=F32)

        @pl.when(pl.program_id(1) == 0)
        def _():
            o_ref[...] = part

        @pl.when(pl.program_id(1) > 0)
        def _():
            o_ref[...] += part

    wspec = pl.BlockSpec((FFN_TILE, d), lambda i, j: (j, 0))
    return _pallas(
        body, name="ffn_du2", grid=(n // tm, f // FFN_TILE),
        out_shape=jax.ShapeDtypeStruct((n, d), F32),
        in_specs=[pl.BlockSpec((tm, 2 * FFN_TILE), lambda i, j: (i, j)), wspec, wspec, _ANY],
        out_specs=pl.BlockSpec((tm, d), lambda i, j: (i, 0)),
        compiler_params=_params(_mb(48)),
    )(dgp, wgt, wut, after)


def _dw_gate_up(dgp, u2, after):
    n, d = u2.shape
    f = dgp.shape[1] // 2
    tm = min(1024, n)

    def body(a_ref, b_ref, after_ref, og_ref, ou_ref, acc_ref):
        part = lax.dot_general(a_ref[...], b_ref[...], _TN, preferred_element_type=F32)
        i = pl.program_id(1)

        @pl.when(i == 0)
        def _():
            acc_ref[...] = part

        @pl.when(i > 0)
        def _():
            acc_ref[...] += part

        @pl.when(i == pl.num_programs(1) - 1)
        def _():
            og_ref[...] = acc_ref[:FFN_TILE].astype(BF16)
            ou_ref[...] = acc_ref[FFN_TILE:].astype(BF16)

    out = pl.BlockSpec((FFN_TILE, d), lambda j, i: (j, 0))
    sds = jax.ShapeDtypeStruct((f, d), BF16)
    return _pallas(
        body, name="dw_gate_up", grid=(f // FFN_TILE, n // tm),
        out_shape=(sds, sds),
        in_specs=[pl.BlockSpec((tm, 2 * FFN_TILE), lambda j, i: (i, j)), pl.BlockSpec((tm, d), lambda j, i: (i, 0)), _ANY],
        out_specs=(out, out),
        scratch_shapes=[pltpu.VMEM((2 * FFN_TILE, d), F32)],
        compiler_params=_params(_mb(56)),
    )(dgp, u2, after)


def _ln1_bwd(du2, dr2, xh1, rs1, a1, lg1, lb1, sc2, g1):
    n, d = du2.shape
    tm = 256

    def body(du_ref, dr2_ref, xh_ref, rs_ref, a_ref, lg_ref, lb_ref, sc_ref, g1_ref, dr1_ref, da_ref, acc_ref):
        @pl.when(pl.program_id(0) == 0)
        def _():
            acc_ref[...] = jnp.zeros_like(acc_ref)

        du = du_ref[...]
        xh = xh_ref[...]
        x1 = xh * lg_ref[...] + lb_ref[...]
        dx1 = ALPHA * dr2_ref[...] + du * (1.0 + sc_ref[...])
        dxg = dx1 * lg_ref[...]
        dr1 = rs_ref[...] * (dxg - _rowmean(dxg) - xh * _rowmean(dxg * xh))
        dr1_ref[...] = dr1
        da_ref[...] = (g1_ref[...] * dr1).astype(BF16)
        acc_ref[0:1, :] += _colsum(du * x1)
        acc_ref[1:2, :] += _colsum(du)
        acc_ref[2:3, :] += _colsum(dx1 * xh)
        acc_ref[3:4, :] += _colsum(dx1)
        acc_ref[4:5, :] += _colsum(dr1 * a_ref[...])

    row = lambda i: (i, 0)
    const2 = lambda i: (0, 0)
    vec = pl.BlockSpec((1, d), const2)
    big = pl.BlockSpec((tm, d), row)
    return _pallas(
        body, name="ln1_bwd", grid=(n // tm,),
        out_shape=(jax.ShapeDtypeStruct((n, d), F32), jax.ShapeDtypeStruct((n, d), BF16),
                   jax.ShapeDtypeStruct((8, d), F32)),
        in_specs=[big, big, big, pl.BlockSpec((tm, 1), row), big, vec, vec, vec, vec],
        out_specs=(big, big, pl.BlockSpec((8, d), const2)),
        compiler_params=_params(_mb(48)),
    )(du2, dr2, xh1, rs1, a1, lg1, lb1, sc2, g1)


def _dw_rows(a, b, nblk, bw, tm, after, name):
    m = a.shape[0]
    nn = b.shape[1]

    def body(a_ref, b_ref, after_ref, o_ref, acc_ref):
        part = lax.dot_general(a_ref[...].astype(BF16), b_ref[...], _TN, preferred_element_type=F32)
        i = pl.program_id(1)

        @pl.when(i == 0)
        def _():
            acc_ref[...] = part

        @pl.when(i > 0)
        def _():
            acc_ref[...] += part

        @pl.when(i == pl.num_programs(1) - 1)
        def _():
            o_ref[0] = acc_ref[...].astype(BF16)

    return _pallas(
        body, name=name, grid=(nblk, m // tm),
        out_shape=jax.ShapeDtypeStruct((nblk, bw, nn), BF16),
        in_specs=[pl.BlockSpec((tm, bw), lambda j, i: (i, j)), pl.BlockSpec((tm, nn), lambda j, i: (i, 0)), _ANY],
        out_specs=pl.BlockSpec((1, bw, nn), lambda j, i: (j, 0, 0)),
        scratch_shapes=[pltpu.VMEM((bw, nn), F32)],
        compiler_params=_params(_mb(56)),
    )(a, b, after)


def _outproj_bwd(da1, wout, after):
    n, d = da1.shape
    tm = 512

    def body(a_ref, w_ref, after_ref, o_ref):
        o_ref[...] = lax.dot_general(a_ref[...], w_ref[...], _NT, preferred_element_type=F32)

    return _pallas(
        body, name="outproj_bwd", grid=(n // tm,),
        out_shape=jax.ShapeDtypeStruct((n, d), F32),
        in_specs=[pl.BlockSpec((tm, d), lambda i: (i, 0)), pl.BlockSpec((d, d), lambda i: (0, 0)), _ANY],
        out_specs=pl.BlockSpec((tm, d), lambda i: (i, 0)),
        compiler_params=_params(_mb(48)),
    )(da1, wout, after)


def _qkv_bwd(dh, wint, x, ct, dr1, sc):
    na, wcols = dh.shape
    n, d = x.shape
    tm = CTX
    nlat = n // tm

    def body(dh_ref, w_ref, x_ref, ct_ref, dr_ref, sc_ref, gx_ref, acc_ref):
        i = pl.program_id(0)

        @pl.when(i == 0)
        def _():
            acc_ref[...] = jnp.zeros_like(acc_ref)

        du = jnp.dot(dh_ref[...], w_ref[...], preferred_element_type=F32)

        @pl.when(i < nlat)
        def _():
            gx_ref[...] = ALPHA * dr_ref[...] + du * (1.0 + sc_ref[0])
            acc_ref[0:1, :] += _colsum(du)
            acc_ref[1:2, :] += _colsum(du * x_ref[...])

        @pl.when(i == nlat)
        def _():
            acc_ref[2:3, :] += _colsum(du)
            acc_ref[3:4, :] += _colsum(du * ct_ref[...])

    lat = lambda i: (jnp.minimum(i, nlat - 1), 0)
    const2 = lambda i: (0, 0)
    return _pallas(
        body, name="qkv_bwd", grid=(nlat + 1,),
        out_shape=(jax.ShapeDtypeStruct((n, d), F32), jax.ShapeDtypeStruct((8, d), F32)),
        in_specs=[pl.BlockSpec((tm, wcols), lambda i: (i, 0)), pl.BlockSpec((wcols, d), const2),
                  pl.BlockSpec((tm, d), lat), pl.BlockSpec((tm, d), const2), pl.BlockSpec((tm, d), lat),
                  pl.BlockSpec((1, 1, d), lambda i: (0, 0, 0))],
        out_specs=(pl.BlockSpec((tm, d), lat), pl.BlockSpec((8, d), const2)),
        compiler_params=_params(_mb(56)),
    )(dh, wint, x, ct, dr1, sc)


def _adam_math(w, g, m, v):
    m2 = ADAM_B1 * m + (1.0 - ADAM_B1) * g
    v2 = ADAM_B2 * v + (1.0 - ADAM_B2) * (g * g)
    m_hat = m2 * (1.0 / (1.0 - ADAM_B1 ** ADAM_STEP))
    v_hat = v2 * (1.0 / (1.0 - ADAM_B2 ** ADAM_STEP))
    delta = -ADAM_LR * (m_hat / (jnp.sqrt(v_hat) + ADAM_EPS) + ADAM_WD * w)
    return delta, m2, v2


def _adamw(w, gsrc, m, v, name):
    r, c = w.shape
    parts = gsrc.ndim == 3
    cg = gsrc.shape[-1]
    tr = r
    while tr * c * 4 > _mb(1) and tr % 32 == 0:
        tr //= 2

    def body(w_ref, g_ref, m_ref, v_ref, go_ref, d_ref, mo_ref, vo_ref):
        if parts:
            g = g_ref[0].astype(F32)
            for s in range(1, NDEV):
                g = g + g_ref[s].astype(F32)
            g = g[:, :c]
        else:
            g = g_ref[...]
        delta, m2, v2 = _adam_math(w_ref[...], g, m_ref[...], v_ref[...])
        go_ref[...] = g
        d_ref[...] = delta
        mo_ref[...] = m2
        vo_ref[...] = v2

    tile = pl.BlockSpec((tr, c), lambda i: (i, 0))
    gspec = pl.BlockSpec((NDEV, tr, cg), lambda i: (0, i, 0)) if parts else tile
    sds = jax.ShapeDtypeStruct((r, c), F32)
    return _pallas(
        body, name=name, grid=(r // tr,),
        out_shape=(sds, sds, sds, sds),
        in_specs=[tile, gspec, tile, tile],
        out_specs=(tile, tile, tile, tile),
        compiler_params=_params(_mb(48)),
    )(w, gsrc, m, v)


def _adamw_t(w, gsrc_t, m, v, name):
    r, c = w.shape
    tr = 256

    def body(w_ref, g_ref, m_ref, v_ref, go_ref, d_ref, mo_ref, vo_ref):
        gt = g_ref[0].astype(F32)
        for s in range(1, NDEV):
            gt = gt + g_ref[s].astype(F32)
        g = gt.T
        delta, m2, v2 = _adam_math(w_ref[...], g, m_ref[...], v_ref[...])
        go_ref[...] = g
        d_ref[...] = delta
        mo_ref[...] = m2
        vo_ref[...] = v2

    tile = pl.BlockSpec((tr, c), lambda i: (i, 0))
    sds = jax.ShapeDtypeStruct((r, c), F32)
    return _pallas(
        body, name=name, grid=(r // tr,),
        out_shape=(sds, sds, sds, sds),
        in_specs=[tile, pl.BlockSpec((NDEV, c, tr), lambda i: (0, 0, i)), tile, tile],
        out_specs=(tile, tile, tile, tile),
        compiler_params=_params(_mb(48)),
    )(w, gsrc_t, m, v)


def _small_update(gath, dcc, cc, w_s, m_s, v_s):
    d = w_s.shape[1]

    def body(g_ref, dcc_ref, cc_ref, w_ref, m_ref, v_ref, go_ref, d_ref, mo_ref, vo_ref):
        s = g_ref[0]
        for b in range(1, NDEV):
            s = s + g_ref[b]
        dsl = dcc_ref[0, 8:9, :]
        for b in range(1, NDEV):
            dsl = dsl + dcc_ref[b, 8:9, :]
        cv = cc_ref[...]
        sg = _sigmoid(cv)
        go_ref[...] = jnp.zeros_like(go_ref)
        go_ref[0:1, :] = dsl * (sg * (1.0 + cv * (1.0 - sg)))
        go_ref[1:3, :] = s[0:2] + s[6:8]
        go_ref[3:7, :] = s[2:6]
        go_ref[7:12, :] = s[8:13]
        delta, m2, v2 = _adam_math(w_ref[...], go_ref[...], m_ref[...], v_ref[...])
        d_ref[...] = delta
        mo_ref[...] = m2
        vo_ref[...] = v2

    full = pl.BlockSpec((16, d), lambda: (0, 0))
    g3 = pl.BlockSpec((NDEV, 16, d), lambda: (0, 0, 0))
    sds = jax.ShapeDtypeStruct((16, d), F32)
    return _pallas(
        body, name="small_update",
        out_shape=(sds, sds, sds, sds),
        in_specs=[g3, g3, pl.BlockSpec((1, d), lambda: (0, 0)), full, full, full],
        out_specs=(full, full, full, full),
        compiler_params=_params(_mb(24)),
    )(gath, dcc, cc, w_s, m_s, v_s)


def _rope_tables(n):
    rows = n // GRID_W
    row_ids = jnp.repeat(jnp.arange(rows, dtype=F32), GRID_W)
    col_ids = jnp.tile(jnp.arange(GRID_W, dtype=F32), rows)
    axis_dim = HEAD // 2
    inv_freq = jnp.power(ROPE_THETA, -jnp.arange(0, axis_dim, 2, dtype=F32) / axis_dim)
    ang_r = row_ids[:, None] * inv_freq
    ang_c = col_ids[:, None] * inv_freq
    ang = jnp.concatenate([ang_r, ang_r, ang_c, ang_c], axis=-1)
    cos, sin = jnp.cos(ang), jnp.sin(ang)
    first = (jnp.arange(HEAD) % (HEAD // 2)) < HEAD // 4
    sa = jnp.where(first, -sin, 0.0)
    sb = jnp.where(first, 0.0, sin)
    ones = jnp.ones((CTX, HEAD), F32)
    zeros = jnp.zeros((CTX, HEAD), F32)
    return (jnp.concatenate([cos, ones], 0), jnp.concatenate([sa, zeros], 0), jnp.concatenate([sb, zeros], 0))


def _pad_cols(a, width):
    return jnp.pad(a, ((0, 0), (0, width - a.shape[1])))


def _pad_rows(a, rows):
    return jnp.pad(a, ((0, rows - a.shape[0]), (0, 0)))


def _pack_small(c_ctx, b_ada, ln1_g, ln1_b, ln2_g, ln2_b, qg, kg, sink, d):
    misc = _pad_cols(jnp.concatenate([qg, kg, sink], axis=1), d)
    rows = jnp.concatenate([c_ctx.reshape(1, d), b_ada.reshape(6, d), ln1_g, ln1_b, ln2_g, ln2_b, misc], axis=0)
    return _pad_rows(rows, 16)


def _unpack_small(p, d):
    return dict(c_ctx=p[0], b_ada=p[1:7].reshape(1, 6 * d), ln1_g=p[7:8], ln1_b=p[8:9], ln2_g=p[9:10], ln2_b=p[10:11],
                q_norm_g=p[11:12, 0:HEAD], k_norm_g=p[11:12, HEAD:2 * HEAD], sink_logit=p[11:12, 2 * HEAD:2 * HEAD + 8])


def kernel(x, c, ctx, c_ctx, w_ada, b_ada, w_in, q_norm_g, k_norm_g, sink_logit, w_out, ln1_g, ln1_b, w_gate, w_up, w_down, ln2_g, ln2_b, loss_target, m_c_ctx, m_w_ada, m_b_ada, m_w_in, m_q_norm_g, m_k_norm_g, m_sink_logit, m_w_out, m_ln1_g, m_ln1_b, m_w_gate, m_w_up, m_w_down, m_ln2_g, m_ln2_b, v_c_ctx, v_w_ada, v_b_ada, v_w_in, v_q_norm_g, v_k_norm_g, v_sink_logit, v_w_out, v_ln1_g, v_ln1_b, v_w_gate, v_w_up, v_w_down, v_ln2_g, v_ln2_b):
    xs, cts, tgt = x[0], ctx[0], loss_target[0]
    n, d = xs.shape
    assert cts.shape == (CTX, d) and w_in.shape[2] == IN_SHARD and w_gate.shape[2] == FFN_SHARD
    me = 4 * lax.axis_index("x") + 2 * lax.axis_index("y") + lax.axis_index("c")
    e_sh = w_ada.shape[2]

    c_g = _exchange(_pad_rows(c, 8), False, "gather_c")
    c_all = jnp.concatenate([c_g[:, 0, :], _pad_rows(c_ctx.reshape(1, d), 8)], axis=0)
    bias_sh = lax.dynamic_slice(b_ada, (0, me * e_sh), (1, e_sh))
    mods_g = _exchange(_ada_fwd(c_all, w_ada[0], bias_sh), False, "gather_mods")
    mods = jnp.transpose(mods_g, (1, 0, 2)).reshape(16, NDEV * e_sh)
    mine = lax.dynamic_slice(mods, (me, 0), (1, 6 * d))
    sh1, sc1, g1, sh2, sc2, g2 = [mine[:, k * d:(k + 1) * d] for k in range(6)]
    csh1, csc1 = mods[8:9, 0:d], mods[8:9, d:2 * d]
    sc_pair = jnp.stack([sc1, csc1])
    sh_pair = jnp.stack([sh1, csh1])

    h_win, tok = _exchange_start(w_in[0].T.astype(BF16), "chip", mods, "gather_w_in_start")
    tok, (wo_l, wg_l, wu_l, wd_l) = lax.optimization_barrier((tok, (w_out, w_gate, w_up, w_down)))
    h_wout, tok = _exchange_start(wo_l[0].astype(BF16), "chip", tok, "gather_w_out_start")
    h_wg, tok = _exchange_start(wg_l[0].T.astype(BF16), "chip", tok, "gather_w_gate_start")
    h_wu, tok = _exchange_start(wu_l[0].T.astype(BF16), "chip", tok, "gather_w_up_start")
    h_wd, tok = _exchange_start(wd_l[0].astype(BF16), "chip", tok, "gather_w_down_start")

    cos, sa, sb = _rope_tables(n)
    f_win, tok = _forward_start(_exchange_wait(h_win, "chip", tok, "gather_w_in_wait"), tok, "forward_w_in_start")
    win_g = _forward_wait(f_win, tok, "forward_w_in_wait").reshape(NDEV * IN_SHARD, d)
    u_all, h_all, t_all, kt_b = _qkv_fwd(xs, cts, sc_pair, sh_pair, win_g, q_norm_g, k_norm_g, cos, sa, sb)
    f_wout, tok = _forward_start(_exchange_wait(h_wout, "chip", t_all, "gather_w_out_wait"), t_all, "forward_w_out_start")
    win_bias = _window_bias()
    o_a, lse_a = _attn_window_fwd(t_all, sink_logit, win_bias, tok)
    o, p_b, linv_b = _attn_global_fwd(t_all, o_a)
    f_wg, tok = _forward_start(_exchange_wait(h_wg, "chip", o, "gather_w_gate_wait"), o, "forward_w_gate_start")
    f_wu, tok = _forward_start(_exchange_wait(h_wu, "chip", tok, "gather_w_up_wait"), tok, "forward_w_up_start")
    wout_g = _forward_wait(f_wout, tok, "forward_w_out_wait").reshape(d, d)
    a1, xh1, rs1, u2 = _outproj_ln1(o, wout_g, xs, g1, ln1_g, ln1_b, sc2, sh2, tok)
    f_wd, tok = _forward_start(_exchange_wait(h_wd, "chip", rs1, "gather_w_down_wait"), rs1, "forward_w_down_start")
    ffn_w = (NDEV * FFN_SHARD, d)
    wg_g = _forward_wait(f_wg, tok, "forward_w_gate_wait").reshape(ffn_w)
    wu_g = _forward_wait(f_wu, tok, "forward_w_up_wait").reshape(ffn_w)
    sa_f, sb_f, hf = _ffn_up(u2, wg_g, wu_g, tok)
    wd_g = _forward_wait(f_wd, hf, "forward_w_down_wait").reshape(ffn_w)
    ffn = _ffn_down(hf, wd_g)
    dr2, df, loss_p, acc2 = _ln2_loss(xh1, ffn, tgt, ln1_g, ln1_b, g2, ln2_g, ln2_b)
    loss = lax.psum(loss_p[0, 0], ("x", "y", "c"))

    tk = min(n, 2048)
    parts = (NDEV, FFN_SHARD, d)
    dgp = _ffn_dhf(df, wd_g, sa_f, sb_f)
    dwd_p = _dw_rows(hf, df, NDEV // 2, FFN_PAIR, min(n, 1024), loss_p, "dw_down").reshape(parts)
    h_dwd, tok = _exchange_start(dwd_p, "scatter", loss.reshape(1, 1), "scatter_dw_down_start")
    dwg_t, dwu_t = _dw_gate_up(dgp, u2, tok)
    h_dwg, tok = _exchange_start(dwg_t.reshape(parts), "scatter", tok, "scatter_dw_gate_start")
    h_dwu, tok = _exchange_start(dwu_t.reshape(parts), "scatter", tok, "scatter_dw_up_start")
    du2 = _ffn_du2(dgp, wg_g, wu_g, tok)
    dr1, da1, acc1 = _ln1_bwd(du2, dr2, xh1, rs1, a1, ln1_g, ln1_b, sc2, g1)
    dwo_p = _dw_rows(o, da1, NDEV, 2 * HEAD, tk, loss_p, "dw_out")
    h_dwo, tok = _exchange_start(dwo_p, "scatter", loss_p, "scatter_dw_out_start")
    do = _outproj_bwd(da1, wout_g, tok)
    dqa, dka, dva, dsink = _attn_window_bwd(t_all, o, do, lse_a, sink_logit, win_bias)
    dqb, dkb, dvb = _attn_global_bwd(t_all, kt_b, o, do, p_b, linv_b)
    dh_all, dnorm = _qkv_bwd_prep(dqa, dka, dva, dqb, dkb, dvb, h_all, q_norm_g, k_norm_g, cos, sa, sb)
    grad_x, acc0 = _qkv_bwd(dh_all, win_g, xs, cts, dr1, sc_pair)

    misc = _pad_cols(jnp.concatenate([dnorm[0:1], dnorm[1:2], dsink[:, 0:4, 0].reshape(1, 8)], axis=1), d)
    part = jnp.concatenate([
        acc0[0:2], acc1[4:5], acc1[1:2], acc1[0:1], acc2[2:3],
        acc0[2:4],
        acc1[2:4], acc2[0:2],
        misc, jnp.zeros((3, d), F32)], axis=0)
    gath = _exchange(part, False, "gather_small")
    dm_batch = gath[:, 0:6, :].reshape(NDEV, 6 * d)
    dm_ctx = _pad_cols(gath[:, 6:8, :].reshape(NDEV, 2 * d), 6 * d)
    dm16 = lax.dynamic_slice(jnp.concatenate([dm_batch, dm_ctx], axis=0), (0, me * e_sh), (16, e_sh))
    dw_ada, drow = _ada_bwd(dm16, c_all, w_ada[0])
    dcc = _exchange(drow, False, "gather_dcc")
    dwi_p = _dw_rows(dh_all, u_all, NDEV, IN_SHARD, (n + CTX) // 2, dcc, "dw_in")
    h_dwi, tok = _exchange_start(dwi_p, "scatter", dcc, "scatter_dw_in_start")

    w_s = _pack_small(c_ctx, b_ada, ln1_g, ln1_b, ln2_g, ln2_b, q_norm_g, k_norm_g, sink_logit, d)
    m_s = _pack_small(m_c_ctx, m_b_ada, m_ln1_g, m_ln1_b, m_ln2_g, m_ln2_b, m_q_norm_g, m_k_norm_g, m_sink_logit, d)
    v_s = _pack_small(v_c_ctx, v_b_ada, v_ln1_g, v_ln1_b, v_ln2_g, v_ln2_b, v_q_norm_g, v_k_norm_g, v_sink_logit, d)
    small = [_unpack_small(p, d) for p in _small_update(gath, dcc, c_ctx.reshape(1, d), w_s, m_s, v_s)]

    big = {}
    big["w_ada"] = _adamw(w_ada[0], dw_ada, m_w_ada[0], v_w_ada[0], "adamw_w_ada")
    late = tok
    big["w_down"] = _adamw(w_down[0], _exchange_wait(h_dwd, "scatter", late, "scatter_dw_down_wait"), m_w_down[0],
                           v_w_down[0], "adamw_w_down")
    for nm, wt, mt, vt, hd in (("w_gate", w_gate, m_w_gate, v_w_gate, h_dwg), ("w_up", w_up, m_w_up, v_w_up, h_dwu)):
        res = _adamw(wt[0].T, _exchange_wait(hd, "scatter", late, "scatter_d" + nm + "_wait"), mt[0].T, vt[0].T,
                     "adamw_" + nm)
        big[nm] = [r.T for r in res]
    big["w_out"] = _adamw(w_out[0], _exchange_wait(h_dwo, "scatter", late, "scatter_dw_out_wait"), m_w_out[0], v_w_out[0],
                          "adamw_w_out")
    big["w_in"] = _adamw_t(w_in[0], _exchange_wait(h_dwi, "scatter", big["w_out"][1], "scatter_dw_in_wait"), m_w_in[0],
                           v_w_in[0], "adamw_w_in")

    names = ["c_ctx", "w_ada", "b_ada", "w_in", "q_norm_g", "k_norm_g", "sink_logit", "w_out", "ln1_g", "ln1_b",
             "w_gate", "w_up", "w_down", "ln2_g", "ln2_b"]
    outs = [loss, grad_x[None]]
    for k in range(4):
        for nm in names:
            outs.append(big[nm][k][None] if nm in big else small[k][nm])
    return tuple(outs)
```

```python
import functools

import jax
import jax.numpy as jnp
from jax import lax
from jax.experimental import pallas as pl
from jax.experimental.pallas import tpu as pltpu

F32 = jnp.float32
BF16 = jnp.bfloat16

NDEV = 8
HEAD = 128
CTX = 256
GRID_W = 64
WINDOW = 128
ROPE_THETA = 10000.0
EPS = 1e-6
SCALE = HEAD ** -0.5
LOG2E = 1.4426950408889634
QK_LOG2 = SCALE * LOG2E
ALPHA = 2.0 ** 0.25
FFN_SHARD = 704
FFN_TILE = 512
FFN_PAIR = 2 * FFN_SHARD
IN_SHARD = 384
NEG = -1e30

ADAM_LR = 0.001
ADAM_B1 = 0.9
ADAM_B2 = 0.999
ADAM_EPS = 1e-08
ADAM_WD = 0.01
ADAM_STEP = 10

VMEM_CAP = 56 * 1024 * 1024

_KINDS = ["rope"] * 10 + ["none"] * 2 + ["qnorm"] * 8 + ["knorm"] * 2 + ["none"] * 2

_NT = (((1,), (1,)), ((), ()))
_TN = (((0,), (0,)), ((), ()))


def _pallas(body, **kw):
    return pl.pallas_call(body, **kw)


def _params(vmem_bytes):
    return pltpu.CompilerParams(vmem_limit_bytes=int(min(VMEM_CAP, vmem_bytes)))


def _mb(n):
    return int(n * 1024 * 1024)


def _sigmoid(x):
    return 1.0 / (1.0 + jnp.exp(-x))


def _colsum(a):
    return jnp.sum(a, axis=0, keepdims=True)


def _rowmean(a):
    return jnp.mean(a, axis=-1, keepdims=True)


def _exchange(src, scatter, name, after=None):
    blk = src.shape[1:] if scatter else src.shape
    after = src if after is None else after

    def body(src_ref, after_ref, out_ref, send_sems, recv_sems, local_sem):
        x, y, c = lax.axis_index("x"), lax.axis_index("y"), lax.axis_index("c")
        me = 4 * x + 2 * y + c
        copies = []
        for t in range(1, NDEV):
            px = 1 - x if (t >> 2) & 1 else x
            py = 1 - y if (t >> 1) & 1 else y
            pc = 1 - c if t & 1 else c
            peer = 4 * px + 2 * py + pc
            cp = pltpu.make_async_remote_copy(
                src_ref=src_ref.at[peer] if scatter else src_ref,
                dst_ref=out_ref.at[me],
                send_sem=send_sems.at[t - 1],
                recv_sem=recv_sems.at[t - 1],
                device_id=(px, py, pc),
                device_id_type=pl.DeviceIdType.MESH,
            )
            cp.start()
            copies.append(cp)
        own = pltpu.make_async_copy(src_ref.at[me] if scatter else src_ref, out_ref.at[me], local_sem)
        own.start()
        for cp in copies:
            cp.wait()
        own.wait()

    return _pallas(
        body, name=name,
        out_shape=jax.ShapeDtypeStruct((NDEV,) + tuple(blk), src.dtype),
        in_specs=[pl.BlockSpec(memory_space=pl.ANY), pl.BlockSpec(memory_space=pl.ANY)],
        out_specs=pl.BlockSpec(memory_space=pl.ANY),
        scratch_shapes=[pltpu.SemaphoreType.DMA((NDEV - 1,)), pltpu.SemaphoreType.DMA((NDEV - 1,)),
                        pltpu.SemaphoreType.DMA(())],
    )(src, after)


_HBM = pl.BlockSpec(memory_space=pltpu.HBM)
_SEM = pl.BlockSpec(memory_space=pltpu.SEMAPHORE)
_ANY = pl.BlockSpec(memory_space=pl.ANY)
_EFFECT = pltpu.SideEffectType.DATAFLOW_SIDE_EFFECTING


def _exchange_copies(src_ref, land_ref, send_sems, recv_sems, mode):
    x, y, c = lax.axis_index("x"), lax.axis_index("y"), lax.axis_index("c")
    me = 4 * x + 2 * y + c
    scatter = mode == "scatter"
    copies = []
    for t in ((1, 2, 4, 6) if mode == "chip" else range(1, NDEV)):
        px = 1 - x if (t >> 2) & 1 else x
        py = 1 - y if (t >> 1) & 1 else y
        pc = 1 - c if t & 1 else c
        peer = 4 * px + 2 * py + pc
        copies.append(pltpu.make_async_remote_copy(
            src_ref=src_ref.at[peer] if scatter else src_ref,
            dst_ref=land_ref.at[me],
            send_sem=send_sems.at[t - 1],
            recv_sem=recv_sems.at[t - 1],
            device_id=(px, py, pc),
            device_id_type=pl.DeviceIdType.MESH,
        ))
    own = pltpu.make_async_copy(src_ref.at[me] if scatter else src_ref, land_ref.at[me], send_sems.at[NDEV - 1])
    return copies, own


def _forward_copies(land_ref, send_sems, recv_sems):
    x, y, c = lax.axis_index("x"), lax.axis_index("y"), lax.axis_index("c")
    copies = []
    for k, t in enumerate((2, 4, 6)):
        px = 1 - x if (t >> 2) & 1 else x
        py = 1 - y if (t >> 1) & 1 else y
        mine, theirs = 4 * px + 2 * py + c, 4 * px + 2 * py + (1 - c)
        send = pltpu.make_async_remote_copy(
            src_ref=land_ref.at[mine], dst_ref=land_ref.at[mine], send_sem=send_sems.at[k], recv_sem=recv_sems.at[k],
            device_id=(x, y, 1 - c), device_id_type=pl.DeviceIdType.MESH)
        recv = pltpu.make_async_remote_copy(
            src_ref=land_ref.at[theirs], dst_ref=land_ref.at[theirs], send_sem=send_sems.at[k], recv_sem=recv_sems.at[k],
            device_id=(x, y, 1 - c), device_id_type=pl.DeviceIdType.MESH)
        copies.append((send, recv))
    return copies


def _forward_start(land, after, name):
    def body(land_ref, after_ref, send_sems, recv_sems, land_thru, token):
        for send, _ in _forward_copies(land_ref, send_sems, recv_sems):
            send.start()
        token[...] = jnp.zeros_like(token)

    res = _pallas(
        body, name=name,
        out_shape=(pltpu.SemaphoreType.DMA((3,)), pltpu.SemaphoreType.DMA((3,)), pltpu.HBM(land.shape, land.dtype),
                   jax.ShapeDtypeStruct((8, HEAD), F32)),
        in_specs=(_HBM, _ANY), out_specs=(_SEM, _SEM, _HBM, pl.BlockSpec(memory_space=pltpu.VMEM)),
        input_output_aliases={0: 2},
        compiler_params=pltpu.CompilerParams(has_side_effects=_EFFECT),
    )(land, after)
    return res[:3], res[3]


def _forward_wait(handle, after, name):
    send_sems, recv_sems, land_thru = handle

    def body(land_ref, send_sems, recv_sems, after_ref, got_ref):
        for send, recv in _forward_copies(land_ref, send_sems, recv_sems):
            send.wait_send()
            recv.wait_recv()

    return _pallas(
        body, name=name,
        out_shape=pltpu.HBM(land_thru.shape, land_thru.dtype),
        in_specs=(_HBM, _SEM, _SEM, _ANY), out_specs=_HBM,
        input_output_aliases={0: 0},
        compiler_params=pltpu.CompilerParams(has_side_effects=_EFFECT),
    )(land_thru, send_sems, recv_sems, after)


def _exchange_start(src, mode, after, name):
    blk = src.shape[1:] if mode == "scatter" else src.shape
    land = lax.empty((NDEV,) + tuple(blk), src.dtype)

    def body(src_ref, land_ref, after_ref, send_sems, recv_sems, src_thru, land_thru, token):
        copies, own = _exchange_copies(src_ref, land_ref, send_sems, recv_sems, mode)
        for cp in copies:
            cp.start()
        own.start()
        token[...] = jnp.zeros_like(token)

    res = _pallas(
        body, name=name,
        out_shape=(pltpu.SemaphoreType.DMA((NDEV,)), pltpu.SemaphoreType.DMA((NDEV,)),
                   pltpu.HBM(src.shape, src.dtype), pltpu.HBM(land.shape, land.dtype),
                   jax.ShapeDtypeStruct((8, HEAD), F32)),
        in_specs=(_HBM, _HBM, _ANY), out_specs=(_SEM, _SEM, _HBM, _HBM, pl.BlockSpec(memory_space=pltpu.VMEM)),
        input_output_aliases={0: 2, 1: 3},
        compiler_params=pltpu.CompilerParams(has_side_effects=_EFFECT),
    )(pltpu.with_memory_space_constraint(src, pltpu.HBM), pltpu.with_memory_space_constraint(land, pltpu.HBM), after)
    return res[:4], res[4]


def _exchange_wait(handle, mode, after, name):
    send_sems, recv_sems, src_thru, land_thru = handle

    def body(src_ref, land_ref, send_sems, recv_sems, after_ref, src_dead, got_ref):
        copies, own = _exchange_copies(src_ref, land_ref, send_sems, recv_sems, mode)
        for cp in copies:
            cp.wait_send()
            cp.wait_recv()
        own.wait()

    return _pallas(
        body, name=name,
        out_shape=(pltpu.HBM(src_thru.shape, src_thru.dtype), pltpu.HBM(land_thru.shape, land_thru.dtype)),
        in_specs=(_HBM, _HBM, _SEM, _SEM, _ANY), out_specs=(_HBM, _HBM),
        input_output_aliases={0: 0, 1: 1},
        compiler_params=pltpu.CompilerParams(has_side_effects=_EFFECT),
    )(src_thru, land_thru, send_sems, recv_sems, after)[1]


def _ada_fwd(c_all, w, bias):
    r, d = c_all.shape
    e = w.shape[1]
    tn = 512

    def body(c_ref, w_ref, b_ref, o_ref):
        cv = c_ref[...]
        s = (cv * _sigmoid(cv)).astype(BF16)
        o_ref[...] = jnp.dot(s, w_ref[...].astype(BF16), preferred_element_type=F32) + b_ref[...]

    return _pallas(
        body, name="ada_fwd", grid=(e // tn,),
        out_shape=jax.ShapeDtypeStruct((r, e), F32),
        in_specs=[pl.BlockSpec((r, d), lambda j: (0, 0)), pl.BlockSpec((d, tn), lambda j: (0, j)),
                  pl.BlockSpec((1, tn), lambda j: (0, j))],
        out_specs=pl.BlockSpec((r, tn), lambda j: (0, j)),
        compiler_params=_params(_mb(24)),
    )(c_all, w, bias)


def _ada_bwd(dm16, c_all, w):
    d, e = w.shape
    tn = 512

    def body(dm_ref, c_ref, w_ref, dw_ref, dr_ref):
        j = pl.program_id(0)
        dm = dm_ref[...]
        rid = lax.broadcasted_iota(jnp.int32, dm.shape, 0)
        ctx_sum = jnp.sum(jnp.where(rid >= 8, dm, 0.0), axis=0, keepdims=True)
        rows = jnp.where(rid < 8, dm, jnp.where(rid == 8, jnp.broadcast_to(ctx_sum, dm.shape), 0.0)).astype(BF16)
        cv = c_ref[...]
        s = (cv * _sigmoid(cv)).astype(BF16)
        dw_ref[...] = lax.dot_general(s, rows, _TN, preferred_element_type=F32)
        part = lax.dot_general(rows, w_ref[...].astype(BF16), _NT, preferred_element_type=F32)

        @pl.when(j == 0)
        def _():
            dr_ref[...] = part

        @pl.when(j > 0)
        def _():
            dr_ref[...] += part

    return _pallas(
        body, name="ada_bwd", grid=(e // tn,),
        out_shape=(jax.ShapeDtypeStruct((d, e), F32), jax.ShapeDtypeStruct((16, d), F32)),
        in_specs=[pl.BlockSpec((16, tn), lambda j: (0, j)), pl.BlockSpec((16, d), lambda j: (0, 0)),
                  pl.BlockSpec((d, tn), lambda j: (0, j))],
        out_specs=(pl.BlockSpec((d, tn), lambda j: (0, j)), pl.BlockSpec((16, d), lambda j: (0, 0))),
        compiler_params=_params(_mb(32)),
    )(dm16, c_all, w)


def _rope(v, cos, sa, sb):
    return v * cos + (pltpu.roll(v, 96, 1) * sa + pltpu.roll(v, 32, 1) * sb)


def _rope_t(dt, cos, sa, sb):
    return dt * cos + (pltpu.roll(dt * sa, 32, 1) + pltpu.roll(dt * sb, 96, 1))


def _qkv_fwd(x, ct, sc, sh, wint, qg, kg, cos, sa, sb):
    n, d = x.shape
    tm = CTX
    nlat = n // tm
    na = n + CTX
    wcols = wint.shape[0]

    def body(x_ref, ct_ref, sc_ref, sh_ref, w_ref, qg_ref, kg_ref, cos_ref, sa_ref, sb_ref, u_ref, h_ref, t_ref, kt_ref):
        i = pl.program_id(0)
        xin = jnp.where(i == nlat, ct_ref[...], x_ref[...])
        u = (xin * (1.0 + sc_ref[0]) + sh_ref[0]).astype(BF16)
        u_ref[...] = u
        cos, sa, sb = cos_ref[...], sa_ref[...], sb_ref[...]
        h = lax.dot_general(u, w_ref[...], _NT, preferred_element_type=F32)
        h_ref[...] = h
        for hd in range(24):
            v = h[:, hd * HEAD:(hd + 1) * HEAD]
            kind = _KINDS[hd]
            if kind == "qnorm":
                v = v * lax.rsqrt(_rowmean(v * v) + EPS) * qg_ref[...]
            elif kind == "knorm":
                v = v * lax.rsqrt(_rowmean(v * v) + EPS) * kg_ref[...]
            if kind != "none":
                v = _rope(v, cos, sa, sb)
            t_ref[:, hd * HEAD:(hd + 1) * HEAD] = v.astype(BF16)
            if kind == "knorm":
                kt_ref[(hd - 20) * HEAD:(hd - 19) * HEAD, :] = v.T.astype(BF16)

    lat = lambda i: (jnp.minimum(i, nlat - 1), 0)
    row = lambda i: (i, 0)
    const2 = lambda i: (0, 0)
    return _pallas(
        body, name="qkv_fwd", grid=(nlat + 1,),
        out_shape=(jax.ShapeDtypeStruct((na, d), BF16), jax.ShapeDtypeStruct((na, wcols), F32),
                   jax.ShapeDtypeStruct((na, wcols), BF16), jax.ShapeDtypeStruct((2 * HEAD, na), BF16)),
        in_specs=[pl.BlockSpec((tm, d), lat), pl.BlockSpec((tm, d), const2),
                  pl.BlockSpec((1, 1, d), lambda i: (i // nlat, 0, 0)),
                  pl.BlockSpec((1, 1, d), lambda i: (i // nlat, 0, 0)),
                  pl.BlockSpec((wcols, d), const2),
                  pl.BlockSpec((1, HEAD), const2), pl.BlockSpec((1, HEAD), const2),
                  pl.BlockSpec((tm, HEAD), row), pl.BlockSpec((tm, HEAD), row), pl.BlockSpec((tm, HEAD), row)],
        out_specs=(pl.BlockSpec((tm, d), row), pl.BlockSpec((tm, wcols), row), pl.BlockSpec((tm, wcols), row),
                   pl.BlockSpec((2 * HEAD, tm), lambda i: (0, i))),
        compiler_params=_params(_mb(56)),
    )(x, ct, sc, sh, wint, qg, kg, cos, sa, sb)


def _qkv_bwd_prep(dqa, dka, dva, dqb, dkb, dvb, h_all, qg, kg, cos, sa, sb):
    na, wcols = h_all.shape
    n = na - CTX
    tm = CTX
    nlat = n // tm

    def body(dqa_ref, dka_ref, dva_ref, dqb_ref, dkb_ref, dvb_ref, h_ref, qg_ref, kg_ref, cos_ref, sa_ref, sb_ref,
             dh_ref, dg_ref):
        i = pl.program_id(0)

        @pl.when(i == 0)
        def _():
            dg_ref[...] = jnp.zeros_like(dg_ref)

        cos, sa, sb = cos_ref[...], sa_ref[...], sb_ref[...]
        is_lat = i < nlat
        for hd in range(24):
            kind = _KINDS[hd]
            if hd < 8:
                dt = jnp.where(is_lat, dqa_ref[:, hd * HEAD:(hd + 1) * HEAD], 0.0)
            elif hd < 10:
                dt = dka_ref[:, (hd - 8) * HEAD:(hd - 7) * HEAD]
            elif hd < 12:
                dt = dva_ref[:, (hd - 10) * HEAD:(hd - 9) * HEAD]
            elif hd < 20:
                dt = jnp.where(is_lat, dqb_ref[:, (hd - 12) * HEAD:(hd - 11) * HEAD], 0.0)
            elif hd < 22:
                dt = dkb_ref[:, (hd - 20) * HEAD:(hd - 19) * HEAD]
            else:
                dt = dvb_ref[:, (hd - 22) * HEAD:(hd - 21) * HEAD]
            if kind != "none":
                dt = _rope_t(dt, cos, sa, sb)
            if kind in ("qnorm", "knorm"):
                g_ref = qg_ref if kind == "qnorm" else kg_ref
                r0 = 0 if kind == "qnorm" else 1
                xv = h_ref[:, hd * HEAD:(hd + 1) * HEAD]
                xn = xv * lax.rsqrt(_rowmean(xv * xv) + EPS)
                dg_ref[r0:r0 + 1, :] += _colsum(dt * xn)
                dxn = dt * g_ref[...]
                dt = lax.rsqrt(_rowmean(xv * xv) + EPS) * (dxn - xn * _rowmean(dxn * xn))
            dh_ref[:, hd * HEAD:(hd + 1) * HEAD] = dt.astype(BF16)

    lat = lambda i: (jnp.minimum(i, nlat - 1), 0)
    row = lambda i: (i, 0)
    const2 = lambda i: (0, 0)
    return _pallas(
        body, name="qkv_bwd_prep", grid=(nlat + 1,),
        out_shape=(jax.ShapeDtypeStruct((na, wcols), BF16), jax.ShapeDtypeStruct((8, HEAD), F32)),
        in_specs=[pl.BlockSpec((tm, 8 * HEAD), lat), pl.BlockSpec((tm, 2 * HEAD), row), pl.BlockSpec((tm, 2 * HEAD), row),
                  pl.BlockSpec((tm, 8 * HEAD), lat), pl.BlockSpec((tm, 2 * HEAD), row), pl.BlockSpec((tm, 2 * HEAD), row),
                  pl.BlockSpec((tm, wcols), row),
                  pl.BlockSpec((1, HEAD), const2), pl.BlockSpec((1, HEAD), const2),
                  pl.BlockSpec((tm, HEAD), row), pl.BlockSpec((tm, HEAD), row), pl.BlockSpec((tm, HEAD), row)],
        out_specs=(pl.BlockSpec((tm, wcols), row), pl.BlockSpec((8, HEAD), const2)),
        compiler_params=_params(_mb(40)),
    )(dqa, dka, dva, dqb, dkb, dvb, h_all, qg, kg, cos, sa, sb)


def _window_keys(k_ref, v_ref, n, na):
    i = pl.program_id(1)
    tq = WINDOW
    start = pl.multiple_of(jnp.clip((i - 1) * tq, 0, n - 3 * tq), tq)
    kk = jnp.concatenate([k_ref[pl.ds(start, 3 * tq), :], k_ref[n:na, :]], axis=0)
    vv = jnp.concatenate([v_ref[pl.ds(start, 3 * tq), :], v_ref[n:na, :]], axis=0)
    return kk, vv, start


def _window_bias():
    tq = WINDOW
    r = (jnp.arange(4 * tq) % tq)[:, None]
    c = jnp.arange(3 * tq + CTX)[None, :]
    variants = []
    for back in (0, tq, 2 * tq):
        seen = (jnp.abs(back + r - c) <= WINDOW) | (c >= 3 * tq)
        variants.append(jnp.where(seen, 0.0, NEG).astype(F32))
    return jnp.stack(variants)


def _window_bias_spec(nq):
    return pl.BlockSpec((1, 4 * WINDOW, 3 * WINDOW + CTX),
                        lambda kv, i: (jnp.where(i == 0, 0, jnp.where(i == nq - 1, 2, 1)), 0, 0))


def _stack_heads(ref, width=HEAD):
    return jnp.concatenate([ref[:, g * HEAD:g * HEAD + width] for g in range(4)], axis=0)


def _sink_column(sink_ref, kv, tq):
    grp = lax.broadcasted_iota(jnp.int32, (4 * tq, 1), 0) // tq
    col = jnp.zeros((4 * tq, 1), F32)
    for g in range(4):
        col = jnp.where(grp == g, sink_ref[0, 4 * kv + g] * LOG2E, col)
    return col


def _attn_window_fwd(t_all, sink, bias, after):
    na = t_all.shape[0]
    n = na - CTX
    tq = WINDOW

    def body(sink_ref, q_ref, k_ref, v_ref, bias_ref, after_ref, o_ref, lse_ref):
        kv = pl.program_id(0)
        kk, vv, _ = _window_keys(k_ref, v_ref, n, na)
        t = lax.dot_general(_stack_heads(q_ref), kk, _NT, preferred_element_type=F32) * QK_LOG2 + bias_ref[0]
        sk = _sink_column(sink_ref, kv, tq)
        m = jnp.maximum(jnp.max(t, axis=-1, keepdims=True), sk)
        p = jnp.exp2(t - m)
        l = jnp.sum(p, axis=-1, keepdims=True) + jnp.exp2(sk - m)
        o = jnp.dot(p.astype(BF16), vv, preferred_element_type=F32) * (1.0 / l)
        lse = m + jnp.log2(l)
        for g in range(4):
            o_ref[:, g * HEAD:(g + 1) * HEAD] = o[g * tq:(g + 1) * tq]
            lse_ref[:, g * HEAD:(g + 1) * HEAD] = jnp.broadcast_to(lse[g * tq:(g + 1) * tq], (tq, HEAD))

    blk = pl.BlockSpec((tq, 4 * HEAD), lambda kv, i: (i, kv))
    return _pallas(
        body, name="attn_window_fwd", grid=(2, n // tq),
        out_shape=(jax.ShapeDtypeStruct((n, 16 * HEAD), F32), jax.ShapeDtypeStruct((n, 8 * HEAD), F32)),
        in_specs=[pl.BlockSpec(memory_space=pltpu.SMEM), blk,
                  pl.BlockSpec((na, HEAD), lambda kv, i: (0, 8 + kv)),
                  pl.BlockSpec((na, HEAD), lambda kv, i: (0, 10 + kv)), _window_bias_spec(n // tq), _ANY],
        out_specs=(blk, blk),
        compiler_params=_params(_mb(32)),
    )(sink, t_all, t_all, t_all, bias, after)


def _attn_global_fwd(t_all, o_part):
    na = t_all.shape[0]
    n = na - CTX
    tq = 256

    def body(q_ref, k_ref, v_ref, o_in_ref, o_ref, p_ref, linv_ref):
        kk, vv = k_ref[...], v_ref[...]
        for g in range(4):
            q = q_ref[:, g * HEAD:(g + 1) * HEAD]
            t = lax.dot_general(q, kk, _NT, preferred_element_type=F32) * QK_LOG2
            m = jnp.max(t, axis=-1, keepdims=True)
            p = jnp.exp2(t - m)
            linv = 1.0 / jnp.sum(p, axis=-1, keepdims=True)
            pb = p.astype(BF16)
            p_ref[g] = pb
            o_ref[:, g * HEAD:(g + 1) * HEAD] = jnp.dot(pb, vv, preferred_element_type=F32) * linv
            linv_ref[:, g * HEAD:(g + 1) * HEAD] = jnp.broadcast_to(linv, (tq, HEAD))

    return _pallas(
        body, name="attn_global_fwd", grid=(2, n // tq),
        out_shape=(jax.ShapeDtypeStruct((n, 16 * HEAD), F32), jax.ShapeDtypeStruct((8, n, na), BF16),
                   jax.ShapeDtypeStruct((n, 8 * HEAD), F32)),
        in_specs=[pl.BlockSpec((tq, 4 * HEAD), lambda kv, i: (i, 3 + kv)),
                  pl.BlockSpec((na, HEAD), lambda kv, i: (0, 20 + kv)),
                  pl.BlockSpec((na, HEAD), lambda kv, i: (0, 22 + kv)), _ANY],
        out_specs=(pl.BlockSpec((tq, 4 * HEAD), lambda kv, i: (i, 2 + kv)),
                   pl.BlockSpec((4, tq, na), lambda kv, i: (kv, i, 0)),
                   pl.BlockSpec((tq, 4 * HEAD), lambda kv, i: (i, kv))),
        input_output_aliases={3: 0},
        compiler_params=_params(_mb(56)),
    )(t_all, t_all, t_all, o_part)


def _attn_window_bwd(t_all, o, do, lse, sink, bias):
    na = t_all.shape[0]
    n = na - CTX
    tq = WINDOW

    def body(sink_ref, q_ref, k_ref, v_ref, o_ref, do_ref, lse_ref, bias_ref, dq_ref, dk_ref, dv_ref, dsink_ref):
        kv = pl.program_id(0)

        @pl.when(pl.program_id(1) == 0)
        def _():
            dk_ref[...] = jnp.zeros_like(dk_ref)
            dv_ref[...] = jnp.zeros_like(dv_ref)
            dsink_ref[...] = jnp.zeros_like(dsink_ref)

        kk, vv, start = _window_keys(k_ref, v_ref, n, na)
        q = _stack_heads(q_ref)
        t = lax.dot_general(q, kk, _NT, preferred_element_type=F32) * QK_LOG2 + bias_ref[0]
        lse = _stack_heads(lse_ref, 1)
        p = jnp.exp2(t - lse)
        dof = _stack_heads(do_ref)
        delta = jnp.sum(dof * _stack_heads(o_ref), axis=-1, keepdims=True)
        dob = dof.astype(BF16)
        dv_acc = lax.dot_general(p.astype(BF16), dob, _TN, preferred_element_type=F32)
        dp = lax.dot_general(dob, vv, _NT, preferred_element_type=F32)
        ds = (p * (dp - delta) * SCALE).astype(BF16)
        dq = jnp.dot(ds, kk, preferred_element_type=F32)
        dk_acc = lax.dot_general(ds, q, _TN, preferred_element_type=F32)
        dsk = -(jnp.exp2(_sink_column(sink_ref, kv, tq) - lse) * delta)
        for g in range(4):
            dq_ref[:, g * HEAD:(g + 1) * HEAD] = dq[g * tq:(g + 1) * tq]
            dsink_ref[0, g:g + 1, :] += jnp.broadcast_to(_colsum(dsk[g * tq:(g + 1) * tq]), (1, HEAD))
        dk_ref[pl.ds(start, 3 * tq), :] += dk_acc[:3 * tq]
        dv_ref[pl.ds(start, 3 * tq), :] += dv_acc[:3 * tq]
        dk_ref[n:na, :] += dk_acc[3 * tq:]
        dv_ref[n:na, :] += dv_acc[3 * tq:]

    blk = pl.BlockSpec((tq, 4 * HEAD), lambda kv, i: (i, kv))
    kvout = pl.BlockSpec((na, HEAD), lambda kv, i: (0, kv))
    return _pallas(
        body, name="attn_window_bwd", grid=(2, n // tq),
        out_shape=(jax.ShapeDtypeStruct((n, 8 * HEAD), F32), jax.ShapeDtypeStruct((na, 2 * HEAD), F32),
                   jax.ShapeDtypeStruct((na, 2 * HEAD), F32), jax.ShapeDtypeStruct((2, 8, HEAD), F32)),
        in_specs=[pl.BlockSpec(memory_space=pltpu.SMEM), blk,
                  pl.BlockSpec((na, HEAD), lambda kv, i: (0, 8 + kv)),
                  pl.BlockSpec((na, HEAD), lambda kv, i: (0, 10 + kv)),
                  blk, blk, blk, _window_bias_spec(n // tq)],
        out_specs=(blk, kvout, kvout, pl.BlockSpec((1, 8, HEAD), lambda kv, i: (kv, 0, 0))),
        compiler_params=_params(_mb(40)),
    )(sink, t_all, t_all, t_all, o, do, lse, bias)


def _attn_global_bwd(t_all, kt, o, do, p_all, linv):
    na = t_all.shape[0]
    n = na - CTX
    tq = 256

    def body(q_ref, v_ref, kt_ref, o_ref, do_ref, p_ref, linv_ref, dq_ref, dk_ref, dv_ref, dkt_acc, dvt_acc):
        i = pl.program_id(1)

        @pl.when(i == 0)
        def _():
            dkt_acc[...] = jnp.zeros_like(dkt_acc)
            dvt_acc[...] = jnp.zeros_like(dvt_acc)

        vv, kt_v = v_ref[...], kt_ref[...]
        dkt = jnp.zeros((HEAD, na), F32)
        dvt = jnp.zeros((HEAD, na), F32)
        for g in range(4):
            q = q_ref[:, g * HEAD:(g + 1) * HEAD]
            p = p_ref[g].astype(F32) * linv_ref[:, g * HEAD:g * HEAD + 1]
            dof = do_ref[:, g * HEAD:(g + 1) * HEAD]
            delta = jnp.sum(dof * o_ref[:, g * HEAD:(g + 1) * HEAD], axis=-1, keepdims=True)
            dob = dof.astype(BF16)
            dvt = dvt + lax.dot_general(dob, p.astype(BF16), _TN, preferred_element_type=F32)
            dp = lax.dot_general(dob, vv, _NT, preferred_element_type=F32)
            ds = (p * (dp - delta) * SCALE).astype(BF16)
            dq_ref[:, g * HEAD:(g + 1) * HEAD] = lax.dot_general(kt_v, ds, _NT, preferred_element_type=F32).T
            dkt = dkt + lax.dot_general(q, ds, _TN, preferred_element_type=F32)
        dkt_acc[...] += dkt
        dvt_acc[...] += dvt

        @pl.when(i == pl.num_programs(1) - 1)
        def _():
            dk_ref[...] = dkt_acc[...].T
            dv_ref[...] = dvt_acc[...].T

    ospec = pl.BlockSpec((tq, 4 * HEAD), lambda kv, i: (i, 2 + kv))
    lspec = pl.BlockSpec((tq, 4 * HEAD), lambda kv, i: (i, kv))
    kvout = pl.BlockSpec((na, HEAD), lambda kv, i: (0, kv))
    return _pallas(
        body, name="attn_global_bwd", grid=(2, n // tq),
        out_shape=(jax.ShapeDtypeStruct((n, 8 * HEAD), F32), jax.ShapeDtypeStruct((na, 2 * HEAD), F32),
                   jax.ShapeDtypeStruct((na, 2 * HEAD), F32)),
        in_specs=[pl.BlockSpec((tq, 4 * HEAD), lambda kv, i: (i, 3 + kv)),
                  pl.BlockSpec((na, HEAD), lambda kv, i: (0, 22 + kv)),
                  pl.BlockSpec((HEAD, na), lambda kv, i: (kv, 0)),
                  ospec, ospec, pl.BlockSpec((4, tq, na), lambda kv, i: (kv, i, 0)), lspec],
        out_specs=(lspec, kvout, kvout),
        scratch_shapes=[pltpu.VMEM((HEAD, na), F32), pltpu.VMEM((HEAD, na), F32)],
        compiler_params=_params(_mb(56)),
    )(t_all, t_all, kt, o, do, p_all, linv)


def _outproj_ln1(o, wout, x, g1, lg, lb, sc2, sh2, after):
    n, d = x.shape
    tm = 256

    def body(o_ref, w_ref, x_ref, g1_ref, lg_ref, lb_ref, sc_ref, sh_ref, after_ref, a_ref, xh_ref, rs_ref, u_ref):
        a1 = jnp.dot(o_ref[...].astype(BF16), w_ref[...], preferred_element_type=F32)
        a_ref[...] = a1
        r = ALPHA * x_ref[...] + g1_ref[...] * a1
        dlt = r - _rowmean(r)
        rstd = lax.rsqrt(_rowmean(dlt * dlt) + EPS)
        xh = dlt * rstd
        xh_ref[...] = xh
        rs_ref[...] = rstd
        x1 = xh * lg_ref[...] + lb_ref[...]
        u_ref[...] = (x1 * (1.0 + sc_ref[...]) + sh_ref[...]).astype(BF16)

    row = lambda i: (i, 0)
    const2 = lambda i: (0, 0)
    vec = pl.BlockSpec((1, d), const2)
    big = pl.BlockSpec((tm, d), row)
    return _pallas(
        body, name="outproj_ln1", grid=(n // tm,),
        out_shape=(jax.ShapeDtypeStruct((n, d), F32), jax.ShapeDtypeStruct((n, d), F32),
                   jax.ShapeDtypeStruct((n, 1), F32), jax.ShapeDtypeStruct((n, d), BF16)),
        in_specs=[big, pl.BlockSpec((d, d), const2), big, vec, vec, vec, vec, vec, _ANY],
        out_specs=(big, big, pl.BlockSpec((tm, 1), row), big),
        compiler_params=_params(_mb(56)),
    )(o, wout, x, g1, lg, lb, sc2, sh2, after)


def _ffn_up(u2, wgt, wut, after):
    n, d = u2.shape
    f = wgt.shape[0]
    tm = min(1024, n)

    def body(u_ref, wg_ref, wu_ref, after_ref, sa_ref, sb_ref, hf_ref):
        u = u_ref[...]
        gv = lax.dot_general(u, wg_ref[...], _NT, preferred_element_type=F32)
        pv = lax.dot_general(u, wu_ref[...], _NT, preferred_element_type=F32)
        sg = _sigmoid(gv)
        silu = gv * sg
        sa_ref[...] = silu.astype(BF16)
        sb_ref[...] = (pv * (sg * (1.0 + gv * (1.0 - sg)))).astype(BF16)
        hf_ref[...] = (silu * pv).astype(BF16)

    tile = pl.BlockSpec((tm, FFN_TILE), lambda i, j: (i, j))
    wspec = pl.BlockSpec((FFN_TILE, d), lambda i, j: (j, 0))
    sds = jax.ShapeDtypeStruct((n, f), BF16)
    return _pallas(
        body, name="ffn_up", grid=(n // tm, f // FFN_TILE),
        out_shape=(sds, sds, sds),
        in_specs=[pl.BlockSpec((tm, d), lambda i, j: (i, 0)), wspec, wspec, _ANY],
        out_specs=(tile, tile, tile),
        compiler_params=_params(_mb(48)),
    )(u2, wgt, wut, after)


def _ffn_down(hf, wd):
    n, f = hf.shape
    d = wd.shape[1]
    tm, tn = min(1024, n), 512

    def body(h_ref, w_ref, o_ref):
        o_ref[...] = jnp.dot(h_ref[...], w_ref[...], preferred_element_type=F32)

    return _pallas(
        body, name="ffn_down", grid=(n // tm, d // tn),
        out_shape=jax.ShapeDtypeStruct((n, d), F32),
        in_specs=[pl.BlockSpec((tm, f), lambda i, j: (i, 0)), pl.BlockSpec((f, tn), lambda i, j: (0, j))],
        out_specs=pl.BlockSpec((tm, tn), lambda i, j: (i, j)),
        compiler_params=_params(_mb(56)),
    )(hf, wd)


def _ln2_loss(xh1, ffn, tgt, lg1, lb1, g2, lg2, lb2):
    n, d = xh1.shape
    tm = 256

    def body(xh_ref, f_ref, t_ref, lg1_ref, lb1_ref, g2_ref, lg2_ref, lb2_ref, dr_ref, df_ref, loss_ref, acc_ref):
        @pl.when(pl.program_id(0) == 0)
        def _():
            loss_ref[...] = jnp.zeros_like(loss_ref)
            acc_ref[...] = jnp.zeros_like(acc_ref)

        x1 = xh_ref[...] * lg1_ref[...] + lb1_ref[...]
        fv = f_ref[...]
        r = ALPHA * x1 + g2_ref[...] * fv
        dlt = r - _rowmean(r)
        rstd = lax.rsqrt(_rowmean(dlt * dlt) + EPS)
        xh2 = dlt * rstd
        err = xh2 * lg2_ref[...] + lb2_ref[...] - t_ref[...]
        loss_ref[...] += 0.5 * jnp.sum(_rowmean(err * err))
        dy = err * (1.0 / d)
        dyg = dy * lg2_ref[...]
        dr = rstd * (dyg - _rowmean(dyg) - xh2 * _rowmean(dyg * xh2))
        dr_ref[...] = dr
        df_ref[...] = (g2_ref[...] * dr).astype(BF16)
        acc_ref[0:1, :] += _colsum(dy * xh2)
        acc_ref[1:2, :] += _colsum(dy)
        acc_ref[2:3, :] += _colsum(dr * fv)

    row = lambda i: (i, 0)
    const2 = lambda i: (0, 0)
    vec = pl.BlockSpec((1, d), const2)
    big = pl.BlockSpec((tm, d), row)
    return _pallas(
        body, name="ln2_loss", grid=(n // tm,),
        out_shape=(jax.ShapeDtypeStruct((n, d), F32), jax.ShapeDtypeStruct((n, d), BF16),
                   jax.ShapeDtypeStruct((8, HEAD), F32), jax.ShapeDtypeStruct((8, d), F32)),
        in_specs=[big, big, big, vec, vec, vec, vec, vec],
        out_specs=(big, big, pl.BlockSpec((8, HEAD), const2), pl.BlockSpec((8, d), const2)),
        compiler_params=_params(_mb(48)),
    )(xh1, ffn, tgt, lg1, lb1, g2, lg2, lb2)


def _ffn_dhf(df, wd, sa, sb):
    n, d = df.shape
    f = sa.shape[1]
    tm = min(1024, n)

    def body(df_ref, w_ref, sa_ref, sb_ref, dgp_ref):
        dhf = lax.dot_general(df_ref[...], w_ref[...], _NT, preferred_element_type=F32)
        dgp_ref[:, :FFN_TILE] = (dhf * sb_ref[...].astype(F32)).astype(BF16)
        dgp_ref[:, FFN_TILE:] = (dhf * sa_ref[...].astype(F32)).astype(BF16)

    tile = pl.BlockSpec((tm, FFN_TILE), lambda i, j: (i, j))
    return _pallas(
        body, name="ffn_dhf", grid=(n // tm, f // FFN_TILE),
        out_shape=jax.ShapeDtypeStruct((n, 2 * f), BF16),
        in_specs=[pl.BlockSpec((tm, d), lambda i, j: (i, 0)), pl.BlockSpec((FFN_TILE, d), lambda i, j: (j, 0)),
                  tile, tile],
        out_specs=pl.BlockSpec((tm, 2 * FFN_TILE), lambda i, j: (i, j)),
        compiler_params=_params(_mb(48)),
    )(df, wd, sa, sb)


def _ffn_du2(dgp, wgt, wut, after):
    n = dgp.shape[0]
    f, d = wgt.shape
    tm = min(1024, n)

    def body(dgp_ref, wg_ref, wu_ref, after_ref, o_ref):
        w = jnp.concatenate([wg_ref[...], wu_ref[...]], axis=0)
        part = jnp.dot(dgp_ref[...], w, preferred_element_type=F32)

        @pl.when(pl.program_id(1) == 0)
        def _():
            o_ref[...] = part

        @pl.when(pl.program_id(1) > 0)
        def _():
            o_ref[...] += part

    wspec = pl.BlockSpec((FFN_TILE, d), lambda i, j: (j, 0))
    return _pallas(
        body, name="ffn_du2", grid=(n // tm, f // FFN_TILE),
        out_shape=jax.ShapeDtypeStruct((n, d), F32),
        in_specs=[pl.BlockSpec((tm, 2 * FFN_TILE), lambda i, j: (i, j)), wspec, wspec, _ANY],
        out_specs=pl.BlockSpec((tm, d), lambda i, j: (i, 0)),
        compiler_params=_params(_mb(48)),
    )(dgp, wgt, wut, after)


def _dw_gate_up(dgp, u2, after):
    n, d = u2.shape
    f = dgp.shape[1] // 2
    tm = min(1024, n)

    def body(a_ref, b_ref, after_ref, og_ref, ou_ref, acc_ref):
        part = lax.dot_general(a_ref[...], b_ref[...], _TN, preferred_element_type=F32)
        i = pl.program_id(1)

        @pl.when(i == 0)
        def _():
            acc_ref[...] = part

        @pl.when(i > 0)
        def _():
            acc_ref[...] += part

        @pl.when(i == pl.num_programs(1) - 1)
        def _():
            og_ref[...] = acc_ref[:FFN_TILE].astype(BF16)
            ou_ref[...] = acc_ref[FFN_TILE:].astype(BF16)

    out = pl.BlockSpec((FFN_TILE, d), lambda j, i: (j, 0))
    sds = jax.ShapeDtypeStruct((f, d), BF16)
    return _pallas(
        body, name="dw_gate_up", grid=(f // FFN_TILE, n // tm),
        out_shape=(sds, sds),
        in_specs=[pl.BlockSpec((tm, 2 * FFN_TILE), lambda j, i: (i, j)), pl.BlockSpec((tm, d), lambda j, i: (i, 0)), _ANY],
        out_specs=(out, out),
        scratch_shapes=[pltpu.VMEM((2 * FFN_TILE, d), F32)],
        compiler_params=_params(_mb(56)),
    )(dgp, u2, after)


def _ln1_bwd(du2, dr2, xh1, rs1, a1, lg1, lb1, sc2, g1):
    n, d = du2.shape
    tm = 256

    def body(du_ref, dr2_ref, xh_ref, rs_ref, a_ref, lg_ref, lb_ref, sc_ref, g1_ref, dr1_ref, da_ref, acc_ref):
        @pl.when(pl.program_id(0) == 0)
        def _():
            acc_ref[...] = jnp.zeros_like(acc_ref)

        du = du_ref[...]
        xh = xh_ref[...]
        x1 = xh * lg_ref[...] + lb_ref[...]
        dx1 = ALPHA * dr2_ref[...] + du * (1.0 + sc_ref[...])
        dxg = dx1 * lg_ref[...]
        dr1 = rs_ref[...] * (dxg - _rowmean(dxg) - xh * _rowmean(dxg * xh))
        dr1_ref[...] = dr1
        da_ref[...] = (g1_ref[...] * dr1).astype(BF16)
        acc_ref[0:1, :] += _colsum(du * x1)
        acc_ref[1:2, :] += _colsum(du)
        acc_ref[2:3, :] += _colsum(dx1 * xh)
        acc_ref[3:4, :] += _colsum(dx1)
        acc_ref[4:5, :] += _colsum(dr1 * a_ref[...])

    row = lambda i: (i, 0)
    const2 = lambda i: (0, 0)
    vec = pl.BlockSpec((1, d), const2)
    big = pl.BlockSpec((tm, d), row)
    return _pallas(
        body, name="ln1_bwd", grid=(n // tm,),
        out_shape=(jax.ShapeDtypeStruct((n, d), F32), jax.ShapeDtypeStruct((n, d), BF16),
                   jax.ShapeDtypeStruct((8, d), F32)),
        in_specs=[big, big, big, pl.BlockSpec((tm, 1), row), big, vec, vec, vec, vec],
        out_specs=(big, big, pl.BlockSpec((8, d), const2)),
        compiler_params=_params(_mb(48)),
    )(du2, dr2, xh1, rs1, a1, lg1, lb1, sc2, g1)


def _dw_rows(a, b, nblk, bw, tm, after, name):
    m = a.shape[0]
    nn = b.shape[1]

    def body(a_ref, b_ref, after_ref, o_ref, acc_ref):
        part = lax.dot_general(a_ref[...].astype(BF16), b_ref[...], _TN, preferred_element_type=F32)
        i = pl.program_id(1)

        @pl.when(i == 0)
        def _():
            acc_ref[...] = part

        @pl.when(i > 0)
        def _():
            acc_ref[...] += part

        @pl.when(i == pl.num_programs(1) - 1)
        def _():
            o_ref[0] = acc_ref[...].astype(BF16)

    return _pallas(
        body, name=name, grid=(nblk, m // tm),
        out_shape=jax.ShapeDtypeStruct((nblk, bw, nn), BF16),
        in_specs=[pl.BlockSpec((tm, bw), lambda j, i: (i, j)), pl.BlockSpec((tm, nn), lambda j, i: (i, 0)), _ANY],
        out_specs=pl.BlockSpec((1, bw, nn), lambda j, i: (j, 0, 0)),
        scratch_shapes=[pltpu.VMEM((bw, nn), F32)],
        compiler_params=_params(_mb(56)),
    )(a, b, after)


def _outproj_bwd(da1, wout, after):
    n, d = da1.shape
    tm = 512

    def body(a_ref, w_ref, after_ref, o_ref):
        o_ref[...] = lax.dot_general(a_ref[...], w_ref[...], _NT, preferred_element_type=F32)

    return _pallas(
        body, name="outproj_bwd", grid=(n // tm,),
        out_shape=jax.ShapeDtypeStruct((n, d), F32),
        in_specs=[pl.BlockSpec((tm, d), lambda i: (i, 0)), pl.BlockSpec((d, d), lambda i: (0, 0)), _ANY],
        out_specs=pl.BlockSpec((tm, d), lambda i: (i, 0)),
        compiler_params=_params(_mb(48)),
    )(da1, wout, after)


def _qkv_bwd(dh, wint, x, ct, dr1, sc):
    na, wcols = dh.shape
    n, d = x.shape
    tm = CTX
    nlat = n // tm

    def body(dh_ref, w_ref, x_ref, ct_ref, dr_ref, sc_ref, gx_ref, acc_ref):
        i = pl.program_id(0)

        @pl.when(i == 0)
        def _():
            acc_ref[...] = jnp.zeros_like(acc_ref)

        du = jnp.dot(dh_ref[...], w_ref[...], preferred_element_type=F32)

        @pl.when(i < nlat)
        def _():
            gx_ref[...] = ALPHA * dr_ref[...] + du * (1.0 + sc_ref[0])
            acc_ref[0:1, :] += _colsum(du)
            acc_ref[1:2, :] += _colsum(du * x_ref[...])

        @pl.when(i == nlat)
        def _():
            acc_ref[2:3, :] += _colsum(du)
            acc_ref[3:4, :] += _colsum(du * ct_ref[...])

    lat = lambda i: (jnp.minimum(i, nlat - 1), 0)
    const2 = lambda i: (0, 0)
    return _pallas(
        body, name="qkv_bwd", grid=(nlat + 1,),
        out_shape=(jax.ShapeDtypeStruct((n, d), F32), jax.ShapeDtypeStruct((8, d), F32)),
        in_specs=[pl.BlockSpec((tm, wcols), lambda i: (i, 0)), pl.BlockSpec((wcols, d), const2),
                  pl.BlockSpec((tm, d), lat), pl.BlockSpec((tm, d), const2), pl.BlockSpec((tm, d), lat),
                  pl.BlockSpec((1, 1, d), lambda i: (0, 0, 0))],
        out_specs=(pl.BlockSpec((tm, d), lat), pl.BlockSpec((8, d), const2)),
        compiler_params=_params(_mb(56)),
    )(dh, wint, x, ct, dr1, sc)


def _adam_math(w, g, m, v):
    m2 = ADAM_B1 * m + (1.0 - ADAM_B1) * g
    v2 = ADAM_B2 * v + (1.0 - ADAM_B2) * (g * g)
    m_hat = m2 * (1.0 / (1.0 - ADAM_B1 ** ADAM_STEP))
    v_hat = v2 * (1.0 / (1.0 - ADAM_B2 ** ADAM_STEP))
    delta = -ADAM_LR * (m_hat / (jnp.sqrt(v_hat) + ADAM_EPS) + ADAM_WD * w)
    return delta, m2, v2


def _adamw(w, gsrc, m, v, name):
    r, c = w.shape
    parts = gsrc.ndim == 3
    cg = gsrc.shape[-1]
    tr = r
    while tr * c * 4 > _mb(1) and tr % 32 == 0:
        tr //= 2

    def body(w_ref, g_ref, m_ref, v_ref, go_ref, d_ref, mo_ref, vo_ref):
        if parts:
            g = g_ref[0].astype(F32)
            for s in range(1, NDEV):
                g = g + g_ref[s].astype(F32)
            g = g[:, :c]
        else:
            g = g_ref[...]
        delta, m2, v2 = _adam_math(w_ref[...], g, m_ref[...], v_ref[...])
        go_ref[...] = g
        d_ref[...] = delta
        mo_ref[...] = m2
        vo_ref[...] = v2

    tile = pl.BlockSpec((tr, c), lambda i: (i, 0))
    gspec = pl.BlockSpec((NDEV, tr, cg), lambda i: (0, i, 0)) if parts else tile
    sds = jax.ShapeDtypeStruct((r, c), F32)
    return _pallas(
        body, name=name, grid=(r // tr,),
        out_shape=(sds, sds, sds, sds),
        in_specs=[tile, gspec, tile, tile],
        out_specs=(tile, tile, tile, tile),
        compiler_params=_params(_mb(48)),
    )(w, gsrc, m, v)


def _adamw_t(w, gsrc_t, m, v, name):
    r, c = w.shape
    tr = 256

    def body(w_ref, g_ref, m_ref, v_ref, go_ref, d_ref, mo_ref, vo_ref):
        gt = g_ref[0].astype(F32)
        for s in range(1, NDEV):
            gt = gt + g_ref[s].astype(F32)
        g = gt.T
        delta, m2, v2 = _adam_math(w_ref[...], g, m_ref[...], v_ref[...])
        go_ref[...] = g
        d_ref[...] = delta
        mo_ref[...] = m2
        vo_ref[...] = v2

    tile = pl.BlockSpec((tr, c), lambda i: (i, 0))
    sds = jax.ShapeDtypeStruct((r, c), F32)
    return _pallas(
        body, name=name, grid=(r // tr,),
        out_shape=(sds, sds, sds, sds),
        in_specs=[tile, pl.BlockSpec((NDEV, c, tr), lambda i: (0, 0, i)), tile, tile],
        out_specs=(tile, tile, tile, tile),
        compiler_params=_params(_mb(48)),
    )(w, gsrc_t, m, v)


def _small_update(gath, dcc, cc, w_s, m_s, v_s):
    d = w_s.shape[1]

    def body(g_ref, dcc_ref, cc_ref, w_ref, m_ref, v_ref, go_ref, d_ref, mo_ref, vo_ref):
        s = g_ref[0]
        for b in range(1, NDEV):
            s = s + g_ref[b]
        dsl = dcc_ref[0, 8:9, :]
        for b in range(1, NDEV):
            dsl = dsl + dcc_ref[b, 8:9, :]
        cv = cc_ref[...]
        sg = _sigmoid(cv)
        go_ref[...] = jnp.zeros_like(go_ref)
        go_ref[0:1, :] = dsl * (sg * (1.0 + cv * (1.0 - sg)))
        go_ref[1:3, :] = s[0:2] + s[6:8]
        go_ref[3:7, :] = s[2:6]
        go_ref[7:12, :] = s[8:13]
        delta, m2, v2 = _adam_math(w_ref[...], go_ref[...], m_ref[...], v_ref[...])
        d_ref[...] = delta
        mo_ref[...] = m2
        vo_ref[...] = v2

    full = pl.BlockSpec((16, d), lambda: (0, 0))
    g3 = pl.BlockSpec((NDEV, 16, d), lambda: (0, 0, 0))
    sds = jax.ShapeDtypeStruct((16, d), F32)
    return _pallas(
        body, name="small_update",
        out_shape=(sds, sds, sds, sds),
        in_specs=[g3, g3, pl.BlockSpec((1, d), lambda: (0, 0)), full, full, full],
        out_specs=(full, full, full, full),
        compiler_params=_params(_mb(24)),
    )(gath, dcc, cc, w_s, m_s, v_s)


def _rope_tables(n):
    rows = n // GRID_W
    row_ids = jnp.repeat(jnp.arange(rows, dtype=F32), GRID_W)
    col_ids = jnp.tile(jnp.arange(GRID_W, dtype=F32), rows)
    axis_dim = HEAD // 2
    inv_freq = jnp.power(ROPE_THETA, -jnp.arange(0, axis_dim, 2, dtype=F32) / axis_dim)
    ang_r = row_ids[:, None] * inv_freq
    ang_c = col_ids[:, None] * inv_freq
    ang = jnp.concatenate([ang_r, ang_r, ang_c, ang_c], axis=-1)
    cos, sin = jnp.cos(ang), jnp.sin(ang)
    first = (jnp.arange(HEAD) % (HEAD // 2)) < HEAD // 4
    sa = jnp.where(first, -sin, 0.0)
    sb = jnp.where(first, 0.0, sin)
    ones = jnp.ones((CTX, HEAD), F32)
    zeros = jnp.zeros((CTX, HEAD), F32)
    return (jnp.concatenate([cos, ones], 0), jnp.concatenate([sa, zeros], 0), jnp.concatenate([sb, zeros], 0))


def _pad_cols(a, width):
    return jnp.pad(a, ((0, 0), (0, width - a.shape[1])))


def _pad_rows(a, rows):
    return jnp.pad(a, ((0, rows - a.shape[0]), (0, 0)))


def _pack_small(c_ctx, b_ada, ln1_g, ln1_b, ln2_g, ln2_b, qg, kg, sink, d):
    misc = _pad_cols(jnp.concatenate([qg, kg, sink], axis=1), d)
    rows = jnp.concatenate([c_ctx.reshape(1, d), b_ada.reshape(6, d), ln1_g, ln1_b, ln2_g, ln2_b, misc], axis=0)
    return _pad_rows(rows, 16)


def _unpack_small(p, d):
    return dict(c_ctx=p[0], b_ada=p[1:7].reshape(1, 6 * d), ln1_g=p[7:8], ln1_b=p[8:9], ln2_g=p[9:10], ln2_b=p[10:11],
                q_norm_g=p[11:12, 0:HEAD], k_norm_g=p[11:12, HEAD:2 * HEAD], sink_logit=p[11:12, 2 * HEAD:2 * HEAD + 8])


def kernel(x, c, ctx, c_ctx, w_ada, b_ada, w_in, q_norm_g, k_norm_g, sink_logit, w_out, ln1_g, ln1_b, w_gate, w_up, w_down, ln2_g, ln2_b, loss_target, m_c_ctx, m_w_ada, m_b_ada, m_w_in, m_q_norm_g, m_k_norm_g, m_sink_logit, m_w_out, m_ln1_g, m_ln1_b, m_w_gate, m_w_up, m_w_down, m_ln2_g, m_ln2_b, v_c_ctx, v_w_ada, v_b_ada, v_w_in, v_q_norm_g, v_k_norm_g, v_sink_logit, v_w_out, v_ln1_g, v_ln1_b, v_w_gate, v_w_up, v_w_down, v_ln2_g, v_ln2_b):
    xs, cts, tgt = x[0], ctx[0], loss_target[0]
    n, d = xs.shape
    assert cts.shape == (CTX, d) and w_in.shape[2] == IN_SHARD and w_gate.shape[2] == FFN_SHARD
    me = 4 * lax.axis_index("x") + 2 * lax.axis_index("y") + lax.axis_index("c")
    e_sh = w_ada.shape[2]

    c_g = _exchange(_pad_rows(c, 8), False, "gather_c")
    c_all = jnp.concatenate([c_g[:, 0, :], _pad_rows(c_ctx.reshape(1, d), 8)], axis=0)
    bias_sh = lax.dynamic_slice(b_ada, (0, me * e_sh), (1, e_sh))
    mods_g = _exchange(_ada_fwd(c_all, w_ada[0], bias_sh), False, "gather_mods")
    mods = jnp.transpose(mods_g, (1, 0, 2)).reshape(16, NDEV * e_sh)
    mine = lax.dynamic_slice(mods, (me, 0), (1, 6 * d))
    sh1, sc1, g1, sh2, sc2, g2 = [mine[:, k * d:(k + 1) * d] for k in range(6)]
    csh1, csc1 = mods[8:9, 0:d], mods[8:9, d:2 * d]
    sc_pair = jnp.stack([sc1, csc1])
    sh_pair = jnp.stack([sh1, csh1])

    h_win, tok = _exchange_start(w_in[0].T.astype(BF16), "chip", mods, "gather_w_in_start")
    tok, (wo_l, wg_l, wu_l, wd_l) = lax.optimization_barrier((tok, (w_out, w_gate, w_up, w_down)))
    h_wout, tok = _exchange_start(wo_l[0].astype(BF16), "chip", tok, "gather_w_out_start")
    h_wg, tok = _exchange_start(wg_l[0].T.astype(BF16), "chip", tok, "gather_w_gate_start")
    h_wu, tok = _exchange_start(wu_l[0].T.astype(BF16), "chip", tok, "gather_w_up_start")
    h_wd, tok = _exchange_start(wd_l[0].astype(BF16), "chip", tok, "gather_w_down_start")

    cos, sa, sb = _rope_tables(n)
    f_win, tok = _forward_start(_exchange_wait(h_win, "chip", tok, "gather_w_in_wait"), tok, "forward_w_in_start")
    win_g = _forward_wait(f_win, tok, "forward_w_in_wait").reshape(NDEV * IN_SHARD, d)
    u_all, h_all, t_all, kt_b = _qkv_fwd(xs, cts, sc_pair, sh_pair, win_g, q_norm_g, k_norm_g, cos, sa, sb)
    f_wout, tok = _forward_start(_exchange_wait(h_wout, "chip", t_all, "gather_w_out_wait"), t_all, "forward_w_out_start")
    win_bias = _window_bias()
    o_a, lse_a = _attn_window_fwd(t_all, sink_logit, win_bias, tok)
    o, p_b, linv_b = _attn_global_fwd(t_all, o_a)
    f_wg, tok = _forward_start(_exchange_wait(h_wg, "chip", o, "gather_w_gate_wait"), o, "forward_w_gate_start")
    f_wu, tok = _forward_start(_exchange_wait(h_wu, "chip", tok, "gather_w_up_wait"), tok, "forward_w_up_start")
    wout_g = _forward_wait(f_wout, tok, "forward_w_out_wait").reshape(d, d)
    a1, xh1, rs1, u2 = _outproj_ln1(o, wout_g, xs, g1, ln1_g, ln1_b, sc2, sh2, tok)
    f_wd, tok = _forward_start(_exchange_wait(h_wd, "chip", rs1, "gather_w_down_wait"), rs1, "forward_w_down_start")
    ffn_w = (NDEV * FFN_SHARD, d)
    wg_g = _forward_wait(f_wg, tok, "forward_w_gate_wait").reshape(ffn_w)
    wu_g = _forward_wait(f_wu, tok, "forward_w_up_wait").reshape(ffn_w)
    sa_f, sb_f, hf = _ffn_up(u2, wg_g, wu_g, tok)
    wd_g = _forward_wait(f_wd, hf, "forward_w_down_wait").reshape(ffn_w)
    ffn = _ffn_down(hf, wd_g)
    dr2, df, loss_p, acc2 = _ln2_loss(xh1, ffn, tgt, ln1_g, ln1_b, g2, ln2_g, ln2_b)
    loss = lax.psum(loss_p[0, 0], ("x", "y", "c"))

    tk = min(n, 2048)
    parts = (NDEV, FFN_SHARD, d)
    dgp = _ffn_dhf(df, wd_g, sa_f, sb_f)
    dwd_p = _dw_rows(hf, df, NDEV // 2, FFN_PAIR, min(n, 1024), loss_p, "dw_down").reshape(parts)
    h_dwd, tok = _exchange_start(dwd_p, "scatter", loss.reshape(1, 1), "scatter_dw_down_start")
    dwg_t, dwu_t = _dw_gate_up(dgp, u2, tok)
    h_dwg, tok = _exchange_start(dwg_t.reshape(parts), "scatter", tok, "scatter_dw_gate_start")
    h_dwu, tok = _exchange_start(dwu_t.reshape(parts), "scatter", tok, "scatter_dw_up_start")
    du2 = _ffn_du2(dgp, wg_g, wu_g, tok)
    dr1, da1, acc1 = _ln1_bwd(du2, dr2, xh1, rs1, a1, ln1_g, ln1_b, sc2, g1)
    dwo_p = _dw_rows(o, da1, NDEV, 2 * HEAD, tk, loss_p, "dw_out")
    h_dwo, tok = _exchange_start(dwo_p, "scatter", loss_p, "scatter_dw_out_start")
    do = _outproj_bwd(da1, wout_g, tok)
    dqa, dka, dva, dsink = _attn_window_bwd(t_all, o, do, lse_a, sink_logit, win_bias)
    dqb, dkb, dvb = _attn_global_bwd(t_all, kt_b, o, do, p_b, linv_b)
    dh_all, dnorm = _qkv_bwd_prep(dqa, dka, dva, dqb, dkb, dvb, h_all, q_norm_g, k_norm_g, cos, sa, sb)
    grad_x, acc0 = _qkv_bwd(dh_all, win_g, xs, cts, dr1, sc_pair)

    misc = _pad_cols(jnp.concatenate([dnorm[0:1], dnorm[1:2], dsink[:, 0:4, 0].reshape(1, 8)], axis=1), d)
    part = jnp.concatenate([
        acc0[0:2], acc1[4:5], acc1[1:2], acc1[0:1], acc2[2:3],
        acc0[2:4],
        acc1[2:4], acc2[0:2],
        misc, jnp.zeros((3, d), F32)], axis=0)
    gath = _exchange(part, False, "gather_small")
    dm_batch = gath[:, 0:6, :].reshape(NDEV, 6 * d)
    dm_ctx = _pad_cols(gath[:, 6:8, :].reshape(NDEV, 2 * d), 6 * d)
    dm16 = lax.dynamic_slice(jnp.concatenate([dm_batch, dm_ctx], axis=0), (0, me * e_sh), (16, e_sh))
    dw_ada, drow = _ada_bwd(dm16, c_all, w_ada[0])
    dcc = _exchange(drow, False, "gather_dcc")
    dwi_p = _dw_rows(dh_all, u_all, NDEV, IN_SHARD, (n + CTX) // 2, dcc, "dw_in")
    h_dwi, tok = _exchange_start(dwi_p, "scatter", dcc, "scatter_dw_in_start")

    w_s = _pack_small(c_ctx, b_ada, ln1_g, ln1_b, ln2_g, ln2_b, q_norm_g, k_norm_g, sink_logit, d)
    m_s = _pack_small(m_c_ctx, m_b_ada, m_ln1_g, m_ln1_b, m_ln2_g, m_ln2_b, m_q_norm_g, m_k_norm_g, m_sink_logit, d)
    v_s = _pack_small(v_c_ctx, v_b_ada, v_ln1_g, v_ln1_b, v_ln2_g, v_ln2_b, v_q_norm_g, v_k_norm_g, v_sink_logit, d)
    small = [_unpack_small(p, d) for p in _small_update(gath, dcc, c_ctx.reshape(1, d), w_s, m_s, v_s)]

    big = {}
    big["w_ada"] = _adamw(w_ada[0], dw_ada, m_w_ada[0], v_w_ada[0], "adamw_w_ada")
    big["w_down"] = _adamw(w_down[0], _exchange_wait(h_dwd, "scatter", tok, "scatter_dw_down_wait"), m_w_down[0],
                           v_w_down[0], "adamw_w_down")
    late = big["w_down"][1]
    for nm, wt, mt, vt, hd in (("w_gate", w_gate, m_w_gate, v_w_gate, h_dwg), ("w_up", w_up, m_w_up, v_w_up, h_dwu)):
        res = _adamw(wt[0].T, _exchange_wait(hd, "scatter", late, "scatter_d" + nm + "_wait"), mt[0].T, vt[0].T,
                     "adamw_" + nm)
        big[nm] = [r.T for r in res]
        late = res[1]
    big["w_out"] = _adamw(w_out[0], _exchange_wait(h_dwo, "scatter", late, "scatter_dw_out_wait"), m_w_out[0], v_w_out[0],
                          "adamw_w_out")
    big["w_in"] = _adamw_t(w_in[0], _exchange_wait(h_dwi, "scatter", big["w_out"][1], "scatter_dw_in_wait"), m_w_in[0],
                           v_w_in[0], "adamw_w_in")

    names = ["c_ctx", "w_ada", "b_ada", "w_in", "q_norm_g", "k_norm_g", "sink_logit", "w_out", "ln1_g", "ln1_b",
             "w_gate", "w_up", "w_down", "ln2_g", "ln2_b"]
    outs = [loss, grad_x[None]]
    for k in range(4):
        for nm in names:
            outs.append(big[nm][k][None] if nm in big else small[k][nm])
    return tuple(outs)
```

```python
import functools

import jax
import jax.numpy as jnp
from jax import lax
from jax.experimental import pallas as pl
from jax.experimental.pallas import tpu as pltpu

F32 = jnp.float32
BF16 = jnp.bfloat16

NDEV = 8
HEAD = 128
CTX = 256
GRID_W = 64
WINDOW = 128
ROPE_THETA = 10000.0
EPS = 1e-6
SCALE = HEAD ** -0.5
LOG2E = 1.4426950408889634
QK_LOG2 = SCALE * LOG2E
ALPHA = 2.0 ** 0.25
FFN_SHARD = 704
FFN_TILE = 512
FFN_PAIR = 2 * FFN_SHARD
IN_SHARD = 384
NEG = -1e30

ADAM_LR = 0.001
ADAM_B1 = 0.9
ADAM_B2 = 0.999
ADAM_EPS = 1e-08
ADAM_WD = 0.01
ADAM_STEP = 10

VMEM_CAP = 56 * 1024 * 1024

_KINDS = ["rope"] * 10 + ["none"] * 2 + ["qnorm"] * 8 + ["knorm"] * 2 + ["none"] * 2

_NT = (((1,), (1,)), ((), ()))
_TN = (((0,), (0,)), ((), ()))


def _pallas(body, **kw):
    return pl.pallas_call(body, **kw)


def _params(vmem_bytes):
    return pltpu.CompilerParams(vmem_limit_bytes=int(min(VMEM_CAP, vmem_bytes)))


def _mb(n):
    return int(n * 1024 * 1024)


def _sigmoid(x):
    return 1.0 / (1.0 + jnp.exp(-x))


def _colsum(a):
    return jnp.sum(a, axis=0, keepdims=True)


def _rowmean(a):
    return jnp.mean(a, axis=-1, keepdims=True)


def _exchange(src, scatter, name, after=None):
    blk = src.shape[1:] if scatter else src.shape
    after = src if after is None else after

    def body(src_ref, after_ref, out_ref, send_sems, recv_sems, local_sem):
        x, y, c = lax.axis_index("x"), lax.axis_index("y"), lax.axis_index("c")
        me = 4 * x + 2 * y + c
        copies = []
        for t in range(1, NDEV):
            px = 1 - x if (t >> 2) & 1 else x
            py = 1 - y if (t >> 1) & 1 else y
            pc = 1 - c if t & 1 else c
            peer = 4 * px + 2 * py + pc
            cp = pltpu.make_async_remote_copy(
                src_ref=src_ref.at[peer] if scatter else src_ref,
                dst_ref=out_ref.at[me],
                send_sem=send_sems.at[t - 1],
                recv_sem=recv_sems.at[t - 1],
                device_id=(px, py, pc),
                device_id_type=pl.DeviceIdType.MESH,
            )
            cp.start()
            copies.append(cp)
        own = pltpu.make_async_copy(src_ref.at[me] if scatter else src_ref, out_ref.at[me], local_sem)
        own.start()
        for cp in copies:
            cp.wait()
        own.wait()

    return _pallas(
        body, name=name,
        out_shape=jax.ShapeDtypeStruct((NDEV,) + tuple(blk), src.dtype),
        in_specs=[pl.BlockSpec(memory_space=pl.ANY), pl.BlockSpec(memory_space=pl.ANY)],
        out_specs=pl.BlockSpec(memory_space=pl.ANY),
        scratch_shapes=[pltpu.SemaphoreType.DMA((NDEV - 1,)), pltpu.SemaphoreType.DMA((NDEV - 1,)),
                        pltpu.SemaphoreType.DMA(())],
    )(src, after)


_HBM = pl.BlockSpec(memory_space=pltpu.HBM)
_SEM = pl.BlockSpec(memory_space=pltpu.SEMAPHORE)
_ANY = pl.BlockSpec(memory_space=pl.ANY)
_EFFECT = pltpu.SideEffectType.DATAFLOW_SIDE_EFFECTING


def _exchange_copies(src_ref, land_ref, send_sems, recv_sems, mode):
    x, y, c = lax.axis_index("x"), lax.axis_index("y"), lax.axis_index("c")
    me = 4 * x + 2 * y + c
    scatter = mode == "scatter"
    copies = []
    for t in ((1, 2, 4, 6) if mode == "chip" else range(1, NDEV)):
        px = 1 - x if (t >> 2) & 1 else x
        py = 1 - y if (t >> 1) & 1 else y
        pc = 1 - c if t & 1 else c
        peer = 4 * px + 2 * py + pc
        copies.append(pltpu.make_async_remote_copy(
            src_ref=src_ref.at[peer] if scatter else src_ref,
            dst_ref=land_ref.at[me],
            send_sem=send_sems.at[t - 1],
            recv_sem=recv_sems.at[t - 1],
            device_id=(px, py, pc),
            device_id_type=pl.DeviceIdType.MESH,
        ))
    own = pltpu.make_async_copy(src_ref.at[me] if scatter else src_ref, land_ref.at[me], send_sems.at[NDEV - 1])
    return copies, own


def _forward_copies(land_ref, send_sems, recv_sems):
    x, y, c = lax.axis_index("x"), lax.axis_index("y"), lax.axis_index("c")
    copies = []
    for k, t in enumerate((2, 4, 6)):
        px = 1 - x if (t >> 2) & 1 else x
        py = 1 - y if (t >> 1) & 1 else y
        mine, theirs = 4 * px + 2 * py + c, 4 * px + 2 * py + (1 - c)
        send = pltpu.make_async_remote_copy(
            src_ref=land_ref.at[mine], dst_ref=land_ref.at[mine], send_sem=send_sems.at[k], recv_sem=recv_sems.at[k],
            device_id=(x, y, 1 - c), device_id_type=pl.DeviceIdType.MESH)
        recv = pltpu.make_async_remote_copy(
            src_ref=land_ref.at[theirs], dst_ref=land_ref.at[theirs], send_sem=send_sems.at[k], recv_sem=recv_sems.at[k],
            device_id=(x, y, 1 - c), device_id_type=pl.DeviceIdType.MESH)
        copies.append((send, recv))
    return copies


def _forward_start(land, after, name):
    def body(land_ref, after_ref, send_sems, recv_sems, land_thru, token):
        for send, _ in _forward_copies(land_ref, send_sems, recv_sems):
            send.start()
        token[...] = jnp.zeros_like(token)

    res = _pallas(
        body, name=name,
        out_shape=(pltpu.SemaphoreType.DMA((3,)), pltpu.SemaphoreType.DMA((3,)), pltpu.HBM(land.shape, land.dtype),
                   jax.ShapeDtypeStruct((8, HEAD), F32)),
        in_specs=(_HBM, _ANY), out_specs=(_SEM, _SEM, _HBM, pl.BlockSpec(memory_space=pltpu.VMEM)),
        input_output_aliases={0: 2},
        compiler_params=pltpu.CompilerParams(has_side_effects=_EFFECT),
    )(land, after)
    return res[:3], res[3]


def _forward_wait(handle, after, name):
    send_sems, recv_sems, land_thru = handle

    def body(land_ref, send_sems, recv_sems, after_ref, got_ref):
        for send, recv in _forward_copies(land_ref, send_sems, recv_sems):
            send.wait_send()
            recv.wait_recv()

    return _pallas(
        body, name=name,
        out_shape=pltpu.HBM(land_thru.shape, land_thru.dtype),
        in_specs=(_HBM, _SEM, _SEM, _ANY), out_specs=_HBM,
        input_output_aliases={0: 0},
        compiler_params=pltpu.CompilerParams(has_side_effects=_EFFECT),
    )(land_thru, send_sems, recv_sems, after)


def _exchange_start(src, mode, after, name):
    blk = src.shape[1:] if mode == "scatter" else src.shape
    land = lax.empty((NDEV,) + tuple(blk), src.dtype)

    def body(src_ref, land_ref, after_ref, send_sems, recv_sems, src_thru, land_thru, token):
        copies, own = _exchange_copies(src_ref, land_ref, send_sems, recv_sems, mode)
        for cp in copies:
            cp.start()
        own.start()
        token[...] = jnp.zeros_like(token)

    res = _pallas(
        body, name=name,
        out_shape=(pltpu.SemaphoreType.DMA((NDEV,)), pltpu.SemaphoreType.DMA((NDEV,)),
                   pltpu.HBM(src.shape, src.dtype), pltpu.HBM(land.shape, land.dtype),
                   jax.ShapeDtypeStruct((8, HEAD), F32)),
        in_specs=(_HBM, _HBM, _ANY), out_specs=(_SEM, _SEM, _HBM, _HBM, pl.BlockSpec(memory_space=pltpu.VMEM)),
        input_output_aliases={0: 2, 1: 3},
        compiler_params=pltpu.CompilerParams(has_side_effects=_EFFECT),
    )(pltpu.with_memory_space_constraint(src, pltpu.HBM), pltpu.with_memory_space_constraint(land, pltpu.HBM), after)
    return res[:4], res[4]


def _exchange_wait(handle, mode, after, name):
    send_sems, recv_sems, src_thru, land_thru = handle

    def body(src_ref, land_ref, send_sems, recv_sems, after_ref, src_dead, got_ref):
        copies, own = _exchange_copies(src_ref, land_ref, send_sems, recv_sems, mode)
        for cp in copies:
            cp.wait_send()
            cp.wait_recv()
        own.wait()

    return _pallas(
        body, name=name,
        out_shape=(pltpu.HBM(src_thru.shape, src_thru.dtype), pltpu.HBM(land_thru.shape, land_thru.dtype)),
        in_specs=(_HBM, _HBM, _SEM, _SEM, _ANY), out_specs=(_HBM, _HBM),
        input_output_aliases={0: 0, 1: 1},
        compiler_params=pltpu.CompilerParams(has_side_effects=_EFFECT),
    )(src_thru, land_thru, send_sems, recv_sems, after)[1]


def _ada_fwd(c_all, w, bias):
    r, d = c_all.shape
    e = w.shape[1]
    tn = 512

    def body(c_ref, w_ref, b_ref, o_ref):
        cv = c_ref[...]
        s = (cv * _sigmoid(cv)).astype(BF16)
        o_ref[...] = jnp.dot(s, w_ref[...].astype(BF16), preferred_element_type=F32) + b_ref[...]

    return _pallas(
        body, name="ada_fwd", grid=(e // tn,),
        out_shape=jax.ShapeDtypeStruct((r, e), F32),
        in_specs=[pl.BlockSpec((r, d), lambda j: (0, 0)), pl.BlockSpec((d, tn), lambda j: (0, j)),
                  pl.BlockSpec((1, tn), lambda j: (0, j))],
        out_specs=pl.BlockSpec((r, tn), lambda j: (0, j)),
        compiler_params=_params(_mb(24)),
    )(c_all, w, bias)


def _ada_bwd(dm16, c_all, w):
    d, e = w.shape
    tn = 512

    def body(dm_ref, c_ref, w_ref, dw_ref, dr_ref):
        j = pl.program_id(0)
        dm = dm_ref[...]
        rid = lax.broadcasted_iota(jnp.int32, dm.shape, 0)
        ctx_sum = jnp.sum(jnp.where(rid >= 8, dm, 0.0), axis=0, keepdims=True)
        rows = jnp.where(rid < 8, dm, jnp.where(rid == 8, jnp.broadcast_to(ctx_sum, dm.shape), 0.0)).astype(BF16)
        cv = c_ref[...]
        s = (cv * _sigmoid(cv)).astype(BF16)
        dw_ref[...] = lax.dot_general(s, rows, _TN, preferred_element_type=F32)
        part = lax.dot_general(rows, w_ref[...].astype(BF16), _NT, preferred_element_type=F32)

        @pl.when(j == 0)
        def _():
            dr_ref[...] = part

        @pl.when(j > 0)
        def _():
            dr_ref[...] += part

    return _pallas(
        body, name="ada_bwd", grid=(e // tn,),
        out_shape=(jax.ShapeDtypeStruct((d, e), F32), jax.ShapeDtypeStruct((16, d), F32)),
        in_specs=[pl.BlockSpec((16, tn), lambda j: (0, j)), pl.BlockSpec((16, d), lambda j: (0, 0)),
                  pl.BlockSpec((d, tn), lambda j: (0, j))],
        out_specs=(pl.BlockSpec((d, tn), lambda j: (0, j)), pl.BlockSpec((16, d), lambda j: (0, 0))),
        compiler_params=_params(_mb(32)),
    )(dm16, c_all, w)


def _rope(v, cos, sa, sb):
    return v * cos + (pltpu.roll(v, 96, 1) * sa + pltpu.roll(v, 32, 1) * sb)


def _rope_t(dt, cos, sa, sb):
    return dt * cos + (pltpu.roll(dt * sa, 32, 1) + pltpu.roll(dt * sb, 96, 1))


def _qkv_fwd(x, ct, sc, sh, wint, qg, kg, cos, sa, sb):
    n, d = x.shape
    tm = CTX
    nlat = n // tm
    na = n + CTX
    wcols = wint.shape[0]

    def body(x_ref, ct_ref, sc_ref, sh_ref, w_ref, qg_ref, kg_ref, cos_ref, sa_ref, sb_ref, u_ref, h_ref, t_ref, kt_ref):
        i = pl.program_id(0)
        xin = jnp.where(i == nlat, ct_ref[...], x_ref[...])
        u = (xin * (1.0 + sc_ref[0]) + sh_ref[0]).astype(BF16)
        u_ref[...] = u
        cos, sa, sb = cos_ref[...], sa_ref[...], sb_ref[...]
        h = lax.dot_general(u, w_ref[...], _NT, preferred_element_type=F32)
        h_ref[...] = h
        for hd in range(24):
            v = h[:, hd * HEAD:(hd + 1) * HEAD]
            kind = _KINDS[hd]
            if kind == "qnorm":
                v = v * lax.rsqrt(_rowmean(v * v) + EPS) * qg_ref[...]
            elif kind == "knorm":
                v = v * lax.rsqrt(_rowmean(v * v) + EPS) * kg_ref[...]
            if kind != "none":
                v = _rope(v, cos, sa, sb)
            t_ref[:, hd * HEAD:(hd + 1) * HEAD] = v.astype(BF16)
            if kind == "knorm":
                kt_ref[(hd - 20) * HEAD:(hd - 19) * HEAD, :] = v.T.astype(BF16)

    lat = lambda i: (jnp.minimum(i, nlat - 1), 0)
    row = lambda i: (i, 0)
    const2 = lambda i: (0, 0)
    return _pallas(
        body, name="qkv_fwd", grid=(nlat + 1,),
        out_shape=(jax.ShapeDtypeStruct((na, d), BF16), jax.ShapeDtypeStruct((na, wcols), F32),
                   jax.ShapeDtypeStruct((na, wcols), BF16), jax.ShapeDtypeStruct((2 * HEAD, na), BF16)),
        in_specs=[pl.BlockSpec((tm, d), lat), pl.BlockSpec((tm, d), const2),
                  pl.BlockSpec((1, 1, d), lambda i: (i // nlat, 0, 0)),
                  pl.BlockSpec((1, 1, d), lambda i: (i // nlat, 0, 0)),
                  pl.BlockSpec((wcols, d), const2),
                  pl.BlockSpec((1, HEAD), const2), pl.BlockSpec((1, HEAD), const2),
                  pl.BlockSpec((tm, HEAD), row), pl.BlockSpec((tm, HEAD), row), pl.BlockSpec((tm, HEAD), row)],
        out_specs=(pl.BlockSpec((tm, d), row), pl.BlockSpec((tm, wcols), row), pl.BlockSpec((tm, wcols), row),
                   pl.BlockSpec((2 * HEAD, tm), lambda i: (0, i))),
        compiler_params=_params(_mb(56)),
    )(x, ct, sc, sh, wint, qg, kg, cos, sa, sb)


def _qkv_bwd_prep(dqa, dka, dva, dqb, dkb, dvb, h_all, qg, kg, cos, sa, sb):
    na, wcols = h_all.shape
    n = na - CTX
    tm = CTX
    nlat = n // tm

    def body(dqa_ref, dka_ref, dva_ref, dqb_ref, dkb_ref, dvb_ref, h_ref, qg_ref, kg_ref, cos_ref, sa_ref, sb_ref,
             dh_ref, dg_ref):
        i = pl.program_id(0)

        @pl.when(i == 0)
        def _():
            dg_ref[...] = jnp.zeros_like(dg_ref)

        cos, sa, sb = cos_ref[...], sa_ref[...], sb_ref[...]
        is_lat = i < nlat
        for hd in range(24):
            kind = _KINDS[hd]
            if hd < 8:
                dt = jnp.where(is_lat, dqa_ref[:, hd * HEAD:(hd + 1) * HEAD], 0.0)
            elif hd < 10:
                dt = dka_ref[:, (hd - 8) * HEAD:(hd - 7) * HEAD]
            elif hd < 12:
                dt = dva_ref[:, (hd - 10) * HEAD:(hd - 9) * HEAD]
            elif hd < 20:
                dt = jnp.where(is_lat, dqb_ref[:, (hd - 12) * HEAD:(hd - 11) * HEAD], 0.0)
            elif hd < 22:
                dt = dkb_ref[:, (hd - 20) * HEAD:(hd - 19) * HEAD]
            else:
                dt = dvb_ref[:, (hd - 22) * HEAD:(hd - 21) * HEAD]
            if kind != "none":
                dt = _rope_t(dt, cos, sa, sb)
            if kind in ("qnorm", "knorm"):
                g_ref = qg_ref if kind == "qnorm" else kg_ref
                r0 = 0 if kind == "qnorm" else 1
                xv = h_ref[:, hd * HEAD:(hd + 1) * HEAD]
                xn = xv * lax.rsqrt(_rowmean(xv * xv) + EPS)
                dg_ref[r0:r0 + 1, :] += _colsum(dt * xn)
                dxn = dt * g_ref[...]
                dt = lax.rsqrt(_rowmean(xv * xv) + EPS) * (dxn - xn * _rowmean(dxn * xn))
            dh_ref[:, hd * HEAD:(hd + 1) * HEAD] = dt.astype(BF16)

    lat = lambda i: (jnp.minimum(i, nlat - 1), 0)
    row = lambda i: (i, 0)
    const2 = lambda i: (0, 0)
    return _pallas(
        body, name="qkv_bwd_prep", grid=(nlat + 1,),
        out_shape=(jax.ShapeDtypeStruct((na, wcols), BF16), jax.ShapeDtypeStruct((8, HEAD), F32)),
        in_specs=[pl.BlockSpec((tm, 8 * HEAD), lat), pl.BlockSpec((tm, 2 * HEAD), row), pl.BlockSpec((tm, 2 * HEAD), row),
                  pl.BlockSpec((tm, 8 * HEAD), lat), pl.BlockSpec((tm, 2 * HEAD), row), pl.BlockSpec((tm, 2 * HEAD), row),
                  pl.BlockSpec((tm, wcols), row),
                  pl.BlockSpec((1, HEAD), const2), pl.BlockSpec((1, HEAD), const2),
                  pl.BlockSpec((tm, HEAD), row), pl.BlockSpec((tm, HEAD), row), pl.BlockSpec((tm, HEAD), row)],
        out_specs=(pl.BlockSpec((tm, wcols), row), pl.BlockSpec((8, HEAD), const2)),
        compiler_params=_params(_mb(40)),
    )(dqa, dka, dva, dqb, dkb, dvb, h_all, qg, kg, cos, sa, sb)


def _window_keys(k_ref, v_ref, n, na):
    i = pl.program_id(1)
    tq = WINDOW
    start = pl.multiple_of(jnp.clip((i - 1) * tq, 0, n - 3 * tq), tq)
    kk = jnp.concatenate([k_ref[pl.ds(start, 3 * tq), :], k_ref[n:na, :]], axis=0)
    vv = jnp.concatenate([v_ref[pl.ds(start, 3 * tq), :], v_ref[n:na, :]], axis=0)
    return kk, vv, start


def _window_bias():
    tq = WINDOW
    r = (jnp.arange(4 * tq) % tq)[:, None]
    c = jnp.arange(3 * tq + CTX)[None, :]
    variants = []
    for back in (0, tq, 2 * tq):
        seen = (jnp.abs(back + r - c) <= WINDOW) | (c >= 3 * tq)
        variants.append(jnp.where(seen, 0.0, NEG).astype(F32))
    return jnp.stack(variants)


def _window_bias_spec(nq):
    return pl.BlockSpec((1, 4 * WINDOW, 3 * WINDOW + CTX),
                        lambda kv, i: (jnp.where(i == 0, 0, jnp.where(i == nq - 1, 2, 1)), 0, 0))


def _stack_heads(ref, width=HEAD):
    return jnp.concatenate([ref[:, g * HEAD:g * HEAD + width] for g in range(4)], axis=0)


def _sink_column(sink_ref, kv, tq):
    grp = lax.broadcasted_iota(jnp.int32, (4 * tq, 1), 0) // tq
    col = jnp.zeros((4 * tq, 1), F32)
    for g in range(4):
        col = jnp.where(grp == g, sink_ref[0, 4 * kv + g] * LOG2E, col)
    return col


def _attn_window_fwd(t_all, sink, bias, after):
    na = t_all.shape[0]
    n = na - CTX
    tq = WINDOW

    def body(sink_ref, q_ref, k_ref, v_ref, bias_ref, after_ref, o_ref, lse_ref):
        kv = pl.program_id(0)
        kk, vv, _ = _window_keys(k_ref, v_ref, n, na)
        t = lax.dot_general(_stack_heads(q_ref), kk, _NT, preferred_element_type=F32) * QK_LOG2 + bias_ref[0]
        sk = _sink_column(sink_ref, kv, tq)
        m = jnp.maximum(jnp.max(t, axis=-1, keepdims=True), sk)
        p = jnp.exp2(t - m)
        l = jnp.sum(p, axis=-1, keepdims=True) + jnp.exp2(sk - m)
        o = jnp.dot(p.astype(BF16), vv, preferred_element_type=F32) * (1.0 / l)
        lse = m + jnp.log2(l)
        for g in range(4):
            o_ref[:, g * HEAD:(g + 1) * HEAD] = o[g * tq:(g + 1) * tq]
            lse_ref[:, g * HEAD:(g + 1) * HEAD] = jnp.broadcast_to(lse[g * tq:(g + 1) * tq], (tq, HEAD))

    blk = pl.BlockSpec((tq, 4 * HEAD), lambda kv, i: (i, kv))
    return _pallas(
        body, name="attn_window_fwd", grid=(2, n // tq),
        out_shape=(jax.ShapeDtypeStruct((n, 16 * HEAD), F32), jax.ShapeDtypeStruct((n, 8 * HEAD), F32)),
        in_specs=[pl.BlockSpec(memory_space=pltpu.SMEM), blk,
                  pl.BlockSpec((na, HEAD), lambda kv, i: (0, 8 + kv)),
                  pl.BlockSpec((na, HEAD), lambda kv, i: (0, 10 + kv)), _window_bias_spec(n // tq), _ANY],
        out_specs=(blk, blk),
        compiler_params=_params(_mb(32)),
    )(sink, t_all, t_all, t_all, bias, after)


def _attn_global_fwd(t_all, o_part):
    na = t_all.shape[0]
    n = na - CTX
    tq = 256

    def body(q_ref, k_ref, v_ref, o_in_ref, o_ref, p_ref, linv_ref):
        kk, vv = k_ref[...], v_ref[...]
        for g in range(4):
            q = q_ref[:, g * HEAD:(g + 1) * HEAD]
            t = lax.dot_general(q, kk, _NT, preferred_element_type=F32) * QK_LOG2
            m = jnp.max(t, axis=-1, keepdims=True)
            p = jnp.exp2(t - m)
            linv = 1.0 / jnp.sum(p, axis=-1, keepdims=True)
            pb = p.astype(BF16)
            p_ref[g] = pb
            o_ref[:, g * HEAD:(g + 1) * HEAD] = jnp.dot(pb, vv, preferred_element_type=F32) * linv
            linv_ref[:, g * HEAD:(g + 1) * HEAD] = jnp.broadcast_to(linv, (tq, HEAD))

    return _pallas(
        body, name="attn_global_fwd", grid=(2, n // tq),
        out_shape=(jax.ShapeDtypeStruct((n, 16 * HEAD), F32), jax.ShapeDtypeStruct((8, n, na), BF16),
                   jax.ShapeDtypeStruct((n, 8 * HEAD), F32)),
        in_specs=[pl.BlockSpec((tq, 4 * HEAD), lambda kv, i: (i, 3 + kv)),
                  pl.BlockSpec((na, HEAD), lambda kv, i: (0, 20 + kv)),
                  pl.BlockSpec((na, HEAD), lambda kv, i: (0, 22 + kv)), _ANY],
        out_specs=(pl.BlockSpec((tq, 4 * HEAD), lambda kv, i: (i, 2 + kv)),
                   pl.BlockSpec((4, tq, na), lambda kv, i: (kv, i, 0)),
                   pl.BlockSpec((tq, 4 * HEAD), lambda kv, i: (i, kv))),
        input_output_aliases={3: 0},
        compiler_params=_params(_mb(56)),
    )(t_all, t_all, t_all, o_part)


def _attn_window_bwd(t_all, o, do, lse, sink, bias):
    na = t_all.shape[0]
    n = na - CTX
    tq = WINDOW

    def body(sink_ref, q_ref, k_ref, v_ref, o_ref, do_ref, lse_ref, bias_ref, dq_ref, dk_ref, dv_ref, dsink_ref):
        kv = pl.program_id(0)

        @pl.when(pl.program_id(1) == 0)
        def _():
            dk_ref[...] = jnp.zeros_like(dk_ref)
            dv_ref[...] = jnp.zeros_like(dv_ref)
            dsink_ref[...] = jnp.zeros_like(dsink_ref)

        kk, vv, start = _window_keys(k_ref, v_ref, n, na)
        q = _stack_heads(q_ref)
        t = lax.dot_general(q, kk, _NT, preferred_element_type=F32) * QK_LOG2 + bias_ref[0]
        lse = _stack_heads(lse_ref, 1)
        p = jnp.exp2(t - lse)
        dof = _stack_heads(do_ref)
        delta = jnp.sum(dof * _stack_heads(o_ref), axis=-1, keepdims=True)
        dob = dof.astype(BF16)
        dv_acc = lax.dot_general(p.astype(BF16), dob, _TN, preferred_element_type=F32)
        dp = lax.dot_general(dob, vv, _NT, preferred_element_type=F32)
        ds = (p * (dp - delta) * SCALE).astype(BF16)
        dq = jnp.dot(ds, kk, preferred_element_type=F32)
        dk_acc = lax.dot_general(ds, q, _TN, preferred_element_type=F32)
        dsk = -(jnp.exp2(_sink_column(sink_ref, kv, tq) - lse) * delta)
        for g in range(4):
            dq_ref[:, g * HEAD:(g + 1) * HEAD] = dq[g * tq:(g + 1) * tq]
            dsink_ref[0, g:g + 1, :] += jnp.broadcast_to(_colsum(dsk[g * tq:(g + 1) * tq]), (1, HEAD))
        dk_ref[pl.ds(start, 3 * tq), :] += dk_acc[:3 * tq]
        dv_ref[pl.ds(start, 3 * tq), :] += dv_acc[:3 * tq]
        dk_ref[n:na, :] += dk_acc[3 * tq:]
        dv_ref[n:na, :] += dv_acc[3 * tq:]

    blk = pl.BlockSpec((tq, 4 * HEAD), lambda kv, i: (i, kv))
    kvout = pl.BlockSpec((na, HEAD), lambda kv, i: (0, kv))
    return _pallas(
        body, name="attn_window_bwd", grid=(2, n // tq),
        out_shape=(jax.ShapeDtypeStruct((n, 8 * HEAD), F32), jax.ShapeDtypeStruct((na, 2 * HEAD), F32),
                   jax.ShapeDtypeStruct((na, 2 * HEAD), F32), jax.ShapeDtypeStruct((2, 8, HEAD), F32)),
        in_specs=[pl.BlockSpec(memory_space=pltpu.SMEM), blk,
                  pl.BlockSpec((na, HEAD), lambda kv, i: (0, 8 + kv)),
                  pl.BlockSpec((na, HEAD), lambda kv, i: (0, 10 + kv)),
                  blk, blk, blk, _window_bias_spec(n // tq)],
        out_specs=(blk, kvout, kvout, pl.BlockSpec((1, 8, HEAD), lambda kv, i: (kv, 0, 0))),
        compiler_params=_params(_mb(40)),
    )(sink, t_all, t_all, t_all, o, do, lse, bias)


def _attn_global_bwd(t_all, kt, o, do, p_all, linv):
    na = t_all.shape[0]
    n = na - CTX
    tq = 256

    def body(q_ref, v_ref, kt_ref, o_ref, do_ref, p_ref, linv_ref, dq_ref, dk_ref, dv_ref, dkt_acc, dvt_acc):
        i = pl.program_id(1)

        @pl.when(i == 0)
        def _():
            dkt_acc[...] = jnp.zeros_like(dkt_acc)
            dvt_acc[...] = jnp.zeros_like(dvt_acc)

        vv, kt_v = v_ref[...], kt_ref[...]
        dkt = jnp.zeros((HEAD, na), F32)
        dvt = jnp.zeros((HEAD, na), F32)
        for g in range(4):
            q = q_ref[:, g * HEAD:(g + 1) * HEAD]
            p = p_ref[g].astype(F32) * linv_ref[:, g * HEAD:g * HEAD + 1]
            dof = do_ref[:, g * HEAD:(g + 1) * HEAD]
            delta = jnp.sum(dof * o_ref[:, g * HEAD:(g + 1) * HEAD], axis=-1, keepdims=True)
            dob = dof.astype(BF16)
            dvt = dvt + lax.dot_general(dob, p.astype(BF16), _TN, preferred_element_type=F32)
            dp = lax.dot_general(dob, vv, _NT, preferred_element_type=F32)
            ds = (p * (dp - delta) * SCALE).astype(BF16)
            dq_ref[:, g * HEAD:(g + 1) * HEAD] = lax.dot_general(kt_v, ds, _NT, preferred_element_type=F32).T
            dkt = dkt + lax.dot_general(q, ds, _TN, preferred_element_type=F32)
        dkt_acc[...] += dkt
        dvt_acc[...] += dvt

        @pl.when(i == pl.num_programs(1) - 1)
        def _():
            dk_ref[...] = dkt_acc[...].T
            dv_ref[...] = dvt_acc[...].T

    ospec = pl.BlockSpec((tq, 4 * HEAD), lambda kv, i: (i, 2 + kv))
    lspec = pl.BlockSpec((tq, 4 * HEAD), lambda kv, i: (i, kv))
    kvout = pl.BlockSpec((na, HEAD), lambda kv, i: (0, kv))
    return _pallas(
        body, name="attn_global_bwd", grid=(2, n // tq),
        out_shape=(jax.ShapeDtypeStruct((n, 8 * HEAD), F32), jax.ShapeDtypeStruct((na, 2 * HEAD), F32),
                   jax.ShapeDtypeStruct((na, 2 * HEAD), F32)),
        in_specs=[pl.BlockSpec((tq, 4 * HEAD), lambda kv, i: (i, 3 + kv)),
                  pl.BlockSpec((na, HEAD), lambda kv, i: (0, 22 + kv)),
                  pl.BlockSpec((HEAD, na), lambda kv, i: (kv, 0)),
                  ospec, ospec, pl.BlockSpec((4, tq, na), lambda kv, i: (kv, i, 0)), lspec],
        out_specs=(lspec, kvout, kvout),
        scratch_shapes=[pltpu.VMEM((HEAD, na), F32), pltpu.VMEM((HEAD, na), F32)],
        compiler_params=_params(_mb(56)),
    )(t_all, t_all, kt, o, do, p_all, linv)


def _outproj_ln1(o, wout, x, g1, lg, lb, sc2, sh2, after):
    n, d = x.shape
    tm = 256

    def body(o_ref, w_ref, x_ref, g1_ref, lg_ref, lb_ref, sc_ref, sh_ref, after_ref, a_ref, xh_ref, rs_ref, u_ref):
        a1 = jnp.dot(o_ref[...].astype(BF16), w_ref[...], preferred_element_type=F32)
        a_ref[...] = a1
        r = ALPHA * x_ref[...] + g1_ref[...] * a1
        dlt = r - _rowmean(r)
        rstd = lax.rsqrt(_rowmean(dlt * dlt) + EPS)
        xh = dlt * rstd
        xh_ref[...] = xh
        rs_ref[...] = rstd
        x1 = xh * lg_ref[...] + lb_ref[...]
        u_ref[...] = (x1 * (1.0 + sc_ref[...]) + sh_ref[...]).astype(BF16)

    row = lambda i: (i, 0)
    const2 = lambda i: (0, 0)
    vec = pl.BlockSpec((1, d), const2)
    big = pl.BlockSpec((tm, d), row)
    return _pallas(
        body, name="outproj_ln1", grid=(n // tm,),
        out_shape=(jax.ShapeDtypeStruct((n, d), F32), jax.ShapeDtypeStruct((n, d), F32),
                   jax.ShapeDtypeStruct((n, 1), F32), jax.ShapeDtypeStruct((n, d), BF16)),
        in_specs=[big, pl.BlockSpec((d, d), const2), big, vec, vec, vec, vec, vec, _ANY],
        out_specs=(big, big, pl.BlockSpec((tm, 1), row), big),
        compiler_params=_params(_mb(56)),
    )(o, wout, x, g1, lg, lb, sc2, sh2, after)


def _ffn_up(u2, wgt, wut, after):
    n, d = u2.shape
    f = wgt.shape[0]
    tm = min(1024, n)

    def body(u_ref, wg_ref, wu_ref, after_ref, sa_ref, sb_ref, hf_ref):
        u = u_ref[...]
        gv = lax.dot_general(u, wg_ref[...], _NT, preferred_element_type=F32)
        pv = lax.dot_general(u, wu_ref[...], _NT, preferred_element_type=F32)
        sg = _sigmoid(gv)
        silu = gv * sg
        sa_ref[...] = silu.astype(BF16)
        sb_ref[...] = (pv * (sg * (1.0 + gv * (1.0 - sg)))).astype(BF16)
        hf_ref[...] = (silu * pv).astype(BF16)

    tile = pl.BlockSpec((tm, FFN_TILE), lambda i, j: (i, j))
    wspec = pl.BlockSpec((FFN_TILE, d), lambda i, j: (j, 0))
    sds = jax.ShapeDtypeStruct((n, f), BF16)
    return _pallas(
        body, name="ffn_up", grid=(n // tm, f // FFN_TILE),
        out_shape=(sds, sds, sds),
        in_specs=[pl.BlockSpec((tm, d), lambda i, j: (i, 0)), wspec, wspec, _ANY],
        out_specs=(tile, tile, tile),
        compiler_params=_params(_mb(48)),
    )(u2, wgt, wut, after)


def _ffn_down(hf, wd):
    n, f = hf.shape
    d = wd.shape[1]
    tm, tn = min(1024, n), 512

    def body(h_ref, w_ref, o_ref):
        o_ref[...] = jnp.dot(h_ref[...], w_ref[...], preferred_element_type=F32)

    return _pallas(
        body, name="ffn_down", grid=(n // tm, d // tn),
        out_shape=jax.ShapeDtypeStruct((n, d), F32),
        in_specs=[pl.BlockSpec((tm, f), lambda i, j: (i, 0)), pl.BlockSpec((f, tn), lambda i, j: (0, j))],
        out_specs=pl.BlockSpec((tm, tn), lambda i, j: (i, j)),
        compiler_params=_params(_mb(56)),
    )(hf, wd)


def _ln2_loss(xh1, ffn, tgt, lg1, lb1, g2, lg2, lb2):
    n, d = xh1.shape
    tm = 256

    def body(xh_ref, f_ref, t_ref, lg1_ref, lb1_ref, g2_ref, lg2_ref, lb2_ref, dr_ref, df_ref, loss_ref, acc_ref):
        @pl.when(pl.program_id(0) == 0)
        def _():
            loss_ref[...] = jnp.zeros_like(loss_ref)
            acc_ref[...] = jnp.zeros_like(acc_ref)

        x1 = xh_ref[...] * lg1_ref[...] + lb1_ref[...]
        fv = f_ref[...]
        r = ALPHA * x1 + g2_ref[...] * fv
        dlt = r - _rowmean(r)
        rstd = lax.rsqrt(_rowmean(dlt * dlt) + EPS)
        xh2 = dlt * rstd
        err = xh2 * lg2_ref[...] + lb2_ref[...] - t_ref[...]
        loss_ref[...] += 0.5 * jnp.sum(_rowmean(err * err))
        dy = err * (1.0 / d)
        dyg = dy * lg2_ref[...]
        dr = rstd * (dyg - _rowmean(dyg) - xh2 * _rowmean(dyg * xh2))
        dr_ref[...] = dr
        df_ref[...] = (g2_ref[...] * dr).astype(BF16)
        acc_ref[0:1, :] += _colsum(dy * xh2)
        acc_ref[1:2, :] += _colsum(dy)
        acc_ref[2:3, :] += _colsum(dr * fv)

    row = lambda i: (i, 0)
    const2 = lambda i: (0, 0)
    vec = pl.BlockSpec((1, d), const2)
    big = pl.BlockSpec((tm, d), row)
    return _pallas(
        body, name="ln2_loss", grid=(n // tm,),
        out_shape=(jax.ShapeDtypeStruct((n, d), F32), jax.ShapeDtypeStruct((n, d), BF16),
                   jax.ShapeDtypeStruct((8, HEAD), F32), jax.ShapeDtypeStruct((8, d), F32)),
        in_specs=[big, big, big, vec, vec, vec, vec, vec],
        out_specs=(big, big, pl.BlockSpec((8, HEAD), const2), pl.BlockSpec((8, d), const2)),
        compiler_params=_params(_mb(48)),
    )(xh1, ffn, tgt, lg1, lb1, g2, lg2, lb2)


def _ffn_dhf(df, wd, sa, sb):
    n, d = df.shape
    f = sa.shape[1]
    tm = min(1024, n)

    def body(df_ref, w_ref, sa_ref, sb_ref, dgp_ref):
        dhf = lax.dot_general(df_ref[...], w_ref[...], _NT, preferred_element_type=F32)
        dgp_ref[:, :FFN_TILE] = (dhf * sb_ref[...].astype(F32)).astype(BF16)
        dgp_ref[:, FFN_TILE:] = (dhf * sa_ref[...].astype(F32)).astype(BF16)

    tile = pl.BlockSpec((tm, FFN_TILE), lambda i, j: (i, j))
    return _pallas(
        body, name="ffn_dhf", grid=(n // tm, f // FFN_TILE),
        out_shape=jax.ShapeDtypeStruct((n, 2 * f), BF16),
        in_specs=[pl.BlockSpec((tm, d), lambda i, j: (i, 0)), pl.BlockSpec((FFN_TILE, d), lambda i, j: (j, 0)),
                  tile, tile],
        out_specs=pl.BlockSpec((tm, 2 * FFN_TILE), lambda i, j: (i, j)),
        compiler_params=_params(_mb(48)),
    )(df, wd, sa, sb)


def _ffn_du2(dgp, wgt, wut, after):
    n = dgp.shape[0]
    f, d = wgt.shape
    tm = min(1024, n)

    def body(dgp_ref, wg_ref, wu_ref, after_ref, o_ref):
        w = jnp.concatenate([wg_ref[...], wu_ref[...]], axis=0)
        part = jnp.dot(dgp_ref[...], w, preferred_element_type=F32)

        @pl.when(pl.program_id(1) == 0)
        def _():
            o_ref[...] = part

        @pl.when(pl.program_id(1) > 0)
        def _():
            o_ref[...] += part

    wspec = pl.BlockSpec((FFN_TILE, d), lambda i, j: (j, 0))
    return _pallas(
        body, name="ffn_du2", grid=(n // tm, f // FFN_TILE),
        out_shape=jax.ShapeDtypeStruct((n, d), F32),
        in_specs=[pl.BlockSpec((tm, 2 * FFN_TILE), lambda i, j: (i, j)), wspec, wspec, _ANY],
        out_specs=pl.BlockSpec((tm, d), lambda i, j: (i, 0)),
        compiler_params=_params(_mb(48)),
    )(dgp, wgt, wut, after)


def _dw_gate_up(dgp, u2, after):
    n, d = u2.shape
    f = dgp.shape[1] // 2
    tm = min(1024, n)

    def body(a_ref, b_ref, after_ref, og_ref, ou_ref, acc_ref):
        part = lax.dot_general(a_ref[...], b_ref[...], _TN, preferred_element_type=F32)
        i = pl.program_id(1)

        @pl.when(i == 0)
        def _():
            acc_ref[...] = part

        @pl.when(i > 0)
        def _():
            acc_ref[...] += part

        @pl.when(i == pl.num_programs(1) - 1)
        def _():
            og_ref[...] = acc_ref[:FFN_TILE].astype(BF16)
            ou_ref[...] = acc_ref[FFN_TILE:].astype(BF16)

    out = pl.BlockSpec((FFN_TILE, d), lambda j, i: (j, 0))
    sds = jax.ShapeDtypeStruct((f, d), BF16)
    return _pallas(
        body, name="dw_gate_up", grid=(f // FFN_TILE, n // tm),
        out_shape=(sds, sds),
        in_specs=[pl.BlockSpec((tm, 2 * FFN_TILE), lambda j, i: (i, j)), pl.BlockSpec((tm, d), lambda j, i: (i, 0)), _ANY],
        out_specs=(out, out),
        scratch_shapes=[pltpu.VMEM((2 * FFN_TILE, d), F32)],
        compiler_params=_params(_mb(56)),
    )(dgp, u2, after)


def _ln1_bwd(du2, dr2, xh1, rs1, a1, lg1, lb1, sc2, g1):
    n, d = du2.shape
    tm = 256

    def body(du_ref, dr2_ref, xh_ref, rs_ref, a_ref, lg_ref, lb_ref, sc_ref, g1_ref, dr1_ref, da_ref, acc_ref):
        @pl.when(pl.program_id(0) == 0)
        def _():
            acc_ref[...] = jnp.zeros_like(acc_ref)

        du = du_ref[...]
        xh = xh_ref[...]
        x1 = xh * lg_ref[...] + lb_ref[...]
        dx1 = ALPHA * dr2_ref[...] + du * (1.0 + sc_ref[...])
        dxg = dx1 * lg_ref[...]
        dr1 = rs_ref[...] * (dxg - _rowmean(dxg) - xh * _rowmean(dxg * xh))
        dr1_ref[...] = dr1
        da_ref[...] = (g1_ref[...] * dr1).astype(BF16)
        acc_ref[0:1, :] += _colsum(du * x1)
        acc_ref[1:2, :] += _colsum(du)
        acc_ref[2:3, :] += _colsum(dx1 * xh)
        acc_ref[3:4, :] += _colsum(dx1)
        acc_ref[4:5, :] += _colsum(dr1 * a_ref[...])

    row = lambda i: (i, 0)
    const2 = lambda i: (0, 0)
    vec = pl.BlockSpec((1, d), const2)
    big = pl.BlockSpec((tm, d), row)
    return _pallas(
        body, name="ln1_bwd", grid=(n // tm,),
        out_shape=(jax.ShapeDtypeStruct((n, d), F32), jax.ShapeDtypeStruct((n, d), BF16),
                   jax.ShapeDtypeStruct((8, d), F32)),
        in_specs=[big, big, big, pl.BlockSpec((tm, 1), row), big, vec, vec, vec, vec],
        out_specs=(big, big, pl.BlockSpec((8, d), const2)),
        compiler_params=_params(_mb(48)),
    )(du2, dr2, xh1, rs1, a1, lg1, lb1, sc2, g1)


def _dw_rows(a, b, nblk, bw, tm, after, name):
    m = a.shape[0]
    nn = b.shape[1]

    def body(a_ref, b_ref, after_ref, o_ref, acc_ref):
        part = lax.dot_general(a_ref[...].astype(BF16), b_ref[...], _TN, preferred_element_type=F32)
        i = pl.program_id(1)

        @pl.when(i == 0)
        def _():
            acc_ref[...] = part

        @pl.when(i > 0)
        def _():
            acc_ref[...] += part

        @pl.when(i == pl.num_programs(1) - 1)
        def _():
            o_ref[0] = acc_ref[...].astype(BF16)

    return _pallas(
        body, name=name, grid=(nblk, m // tm),
        out_shape=jax.ShapeDtypeStruct((nblk, bw, nn), BF16),
        in_specs=[pl.BlockSpec((tm, bw), lambda j, i: (i, j)), pl.BlockSpec((tm, nn), lambda j, i: (i, 0)), _ANY],
        out_specs=pl.BlockSpec((1, bw, nn), lambda j, i: (j, 0, 0)),
        scratch_shapes=[pltpu.VMEM((bw, nn), F32)],
        compiler_params=_params(_mb(56)),
    )(a, b, after)


def _outproj_bwd(da1, wout, after):
    n, d = da1.shape
    tm = 512

    def body(a_ref, w_ref, after_ref, o_ref):
        o_ref[...] = lax.dot_general(a_ref[...], w_ref[...], _NT, preferred_element_type=F32)

    return _pallas(
        body, name="outproj_bwd", grid=(n // tm,),
        out_shape=jax.ShapeDtypeStruct((n, d), F32),
        in_specs=[pl.BlockSpec((tm, d), lambda i: (i, 0)), pl.BlockSpec((d, d), lambda i: (0, 0)), _ANY],
        out_specs=pl.BlockSpec((tm, d), lambda i: (i, 0)),
        compiler_params=_params(_mb(48)),
    )(da1, wout, after)


def _qkv_bwd(dh, wint, x, ct, dr1, sc):
    na, wcols = dh.shape
    n, d = x.shape
    tm = CTX
    nlat = n // tm

    def body(dh_ref, w_ref, x_ref, ct_ref, dr_ref, sc_ref, gx_ref, acc_ref):
        i = pl.program_id(0)

        @pl.when(i == 0)
        def _():
            acc_ref[...] = jnp.zeros_like(acc_ref)

        du = jnp.dot(dh_ref[...], w_ref[...], preferred_element_type=F32)

        @pl.when(i < nlat)
        def _():
            gx_ref[...] = ALPHA * dr_ref[...] + du * (1.0 + sc_ref[0])
            acc_ref[0:1, :] += _colsum(du)
            acc_ref[1:2, :] += _colsum(du * x_ref[...])

        @pl.when(i == nlat)
        def _():
            acc_ref[2:3, :] += _colsum(du)
            acc_ref[3:4, :] += _colsum(du * ct_ref[...])

    lat = lambda i: (jnp.minimum(i, nlat - 1), 0)
    const2 = lambda i: (0, 0)
    return _pallas(
        body, name="qkv_bwd", grid=(nlat + 1,),
        out_shape=(jax.ShapeDtypeStruct((n, d), F32), jax.ShapeDtypeStruct((8, d), F32)),
        in_specs=[pl.BlockSpec((tm, wcols), lambda i: (i, 0)), pl.BlockSpec((wcols, d), const2),
                  pl.BlockSpec((tm, d), lat), pl.BlockSpec((tm, d), const2), pl.BlockSpec((tm, d), lat),
                  pl.BlockSpec((1, 1, d), lambda i: (0, 0, 0))],
        out_specs=(pl.BlockSpec((tm, d), lat), pl.BlockSpec((8, d), const2)),
        compiler_params=_params(_mb(56)),
    )(dh, wint, x, ct, dr1, sc)


def _adam_math(w, g, m, v):
    m2 = ADAM_B1 * m + (1.0 - ADAM_B1) * g
    v2 = ADAM_B2 * v + (1.0 - ADAM_B2) * (g * g)
    m_hat = m2 * (1.0 / (1.0 - ADAM_B1 ** ADAM_STEP))
    v_hat = v2 * (1.0 / (1.0 - ADAM_B2 ** ADAM_STEP))
    delta = -ADAM_LR * (m_hat / (jnp.sqrt(v_hat) + ADAM_EPS) + ADAM_WD * w)
    return delta, m2, v2


def _adamw(w, gsrc, m, v, name, after=None):
    r, c = w.shape
    parts = gsrc.ndim == 3
    after = w if after is None else after
    tr = r
    while tr * c * 4 > _mb(1) and tr % 32 == 0:
        tr //= 2

    def body(w_ref, g_ref, m_ref, v_ref, after_ref, go_ref, d_ref, mo_ref, vo_ref):
        if parts:
            g = g_ref[0].astype(F32)
            for s in range(1, NDEV):
                g = g + g_ref[s].astype(F32)
        else:
            g = g_ref[...]
        delta, m2, v2 = _adam_math(w_ref[...], g, m_ref[...], v_ref[...])
        go_ref[...] = g
        d_ref[...] = delta
        mo_ref[...] = m2
        vo_ref[...] = v2

    tile = pl.BlockSpec((tr, c), lambda i: (i, 0))
    gspec = pl.BlockSpec((NDEV, tr, c), lambda i: (0, i, 0)) if parts else tile
    sds = jax.ShapeDtypeStruct((r, c), F32)
    return _pallas(
        body, name=name, grid=(r // tr,),
        out_shape=(sds, sds, sds, sds),
        in_specs=[tile, gspec, tile, tile, _ANY],
        out_specs=(tile, tile, tile, tile),
        compiler_params=_params(_mb(48)),
    )(w, gsrc, m, v, after)


def _adamw_t(w, gsrc_t, m, v, name):
    r, c = w.shape
    tr = 256

    def body(w_ref, g_ref, m_ref, v_ref, go_ref, d_ref, mo_ref, vo_ref):
        gt = g_ref[0].astype(F32)
        for s in range(1, NDEV):
            gt = gt + g_ref[s].astype(F32)
        g = gt.T
        delta, m2, v2 = _adam_math(w_ref[...], g, m_ref[...], v_ref[...])
        go_ref[...] = g
        d_ref[...] = delta
        mo_ref[...] = m2
        vo_ref[...] = v2

    tile = pl.BlockSpec((tr, c), lambda i: (i, 0))
    sds = jax.ShapeDtypeStruct((r, c), F32)
    return _pallas(
        body, name=name, grid=(r // tr,),
        out_shape=(sds, sds, sds, sds),
        in_specs=[tile, pl.BlockSpec((NDEV, c, tr), lambda i: (0, 0, i)), tile, tile],
        out_specs=(tile, tile, tile, tile),
        compiler_params=_params(_mb(48)),
    )(w, gsrc_t, m, v)


def _small_update(gath, dcc, cc, w_s, m_s, v_s):
    d = w_s.shape[1]

    def body(g_ref, dcc_ref, cc_ref, w_ref, m_ref, v_ref, go_ref, d_ref, mo_ref, vo_ref):
        s = g_ref[0]
        for b in range(1, NDEV):
            s = s + g_ref[b]
        dsl = dcc_ref[0, 8:9, :]
        for b in range(1, NDEV):
            dsl = dsl + dcc_ref[b, 8:9, :]
        cv = cc_ref[...]
        sg = _sigmoid(cv)
        go_ref[...] = jnp.zeros_like(go_ref)
        go_ref[0:1, :] = dsl * (sg * (1.0 + cv * (1.0 - sg)))
        go_ref[1:3, :] = s[0:2] + s[6:8]
        go_ref[3:7, :] = s[2:6]
        go_ref[7:12, :] = s[8:13]
        delta, m2, v2 = _adam_math(w_ref[...], go_ref[...], m_ref[...], v_ref[...])
        d_ref[...] = delta
        mo_ref[...] = m2
        vo_ref[...] = v2

    full = pl.BlockSpec((16, d), lambda: (0, 0))
    g3 = pl.BlockSpec((NDEV, 16, d), lambda: (0, 0, 0))
    sds = jax.ShapeDtypeStruct((16, d), F32)
    return _pallas(
        body, name="small_update",
        out_shape=(sds, sds, sds, sds),
        in_specs=[g3, g3, pl.BlockSpec((1, d), lambda: (0, 0)), full, full, full],
        out_specs=(full, full, full, full),
        compiler_params=_params(_mb(24)),
    )(gath, dcc, cc, w_s, m_s, v_s)


def _rope_tables(n):
    rows = n // GRID_W
    row_ids = jnp.repeat(jnp.arange(rows, dtype=F32), GRID_W)
    col_ids = jnp.tile(jnp.arange(GRID_W, dtype=F32), rows)
    axis_dim = HEAD // 2
    inv_freq = jnp.power(ROPE_THETA, -jnp.arange(0, axis_dim, 2, dtype=F32) / axis_dim)
    ang_r = row_ids[:, None] * inv_freq
    ang_c = col_ids[:, None] * inv_freq
    ang = jnp.concatenate([ang_r, ang_r, ang_c, ang_c], axis=-1)
    cos, sin = jnp.cos(ang), jnp.sin(ang)
    first = (jnp.arange(HEAD) % (HEAD // 2)) < HEAD // 4
    sa = jnp.where(first, -sin, 0.0)
    sb = jnp.where(first, 0.0, sin)
    ones = jnp.ones((CTX, HEAD), F32)
    zeros = jnp.zeros((CTX, HEAD), F32)
    return (jnp.concatenate([cos, ones], 0), jnp.concatenate([sa, zeros], 0), jnp.concatenate([sb, zeros], 0))


def _pad_cols(a, width):
    return jnp.pad(a, ((0, 0), (0, width - a.shape[1])))


def _pad_rows(a, rows):
    return jnp.pad(a, ((0, rows - a.shape[0]), (0, 0)))


def _pack_small(c_ctx, b_ada, ln1_g, ln1_b, ln2_g, ln2_b, qg, kg, sink, d):
    misc = _pad_cols(jnp.concatenate([qg, kg, sink], axis=1), d)
    rows = jnp.concatenate([c_ctx.reshape(1, d), b_ada.reshape(6, d), ln1_g, ln1_b, ln2_g, ln2_b, misc], axis=0)
    return _pad_rows(rows, 16)


def _unpack_small(p, d):
    return dict(c_ctx=p[0], b_ada=p[1:7].reshape(1, 6 * d), ln1_g=p[7:8], ln1_b=p[8:9], ln2_g=p[9:10], ln2_b=p[10:11],
                q_norm_g=p[11:12, 0:HEAD], k_norm_g=p[11:12, HEAD:2 * HEAD], sink_logit=p[11:12, 2 * HEAD:2 * HEAD + 8])


def kernel(x, c, ctx, c_ctx, w_ada, b_ada, w_in, q_norm_g, k_norm_g, sink_logit, w_out, ln1_g, ln1_b, w_gate, w_up, w_down, ln2_g, ln2_b, loss_target, m_c_ctx, m_w_ada, m_b_ada, m_w_in, m_q_norm_g, m_k_norm_g, m_sink_logit, m_w_out, m_ln1_g, m_ln1_b, m_w_gate, m_w_up, m_w_down, m_ln2_g, m_ln2_b, v_c_ctx, v_w_ada, v_b_ada, v_w_in, v_q_norm_g, v_k_norm_g, v_sink_logit, v_w_out, v_ln1_g, v_ln1_b, v_w_gate, v_w_up, v_w_down, v_ln2_g, v_ln2_b):
    xs, cts, tgt = x[0], ctx[0], loss_target[0]
    n, d = xs.shape
    assert cts.shape == (CTX, d) and w_in.shape[2] == IN_SHARD and w_gate.shape[2] == FFN_SHARD
    me = 4 * lax.axis_index("x") + 2 * lax.axis_index("y") + lax.axis_index("c")
    e_sh = w_ada.shape[2]

    c_g = _exchange(_pad_rows(c, 8), False, "gather_c")
    c_all = jnp.concatenate([c_g[:, 0, :], _pad_rows(c_ctx.reshape(1, d), 8)], axis=0)
    bias_sh = lax.dynamic_slice(b_ada, (0, me * e_sh), (1, e_sh))
    mods_g = _exchange(_ada_fwd(c_all, w_ada[0], bias_sh), False, "gather_mods")
    mods = jnp.transpose(mods_g, (1, 0, 2)).reshape(16, NDEV * e_sh)
    mine = lax.dynamic_slice(mods, (me, 0), (1, 6 * d))
    sh1, sc1, g1, sh2, sc2, g2 = [mine[:, k * d:(k + 1) * d] for k in range(6)]
    csh1, csc1 = mods[8:9, 0:d], mods[8:9, d:2 * d]
    sc_pair = jnp.stack([sc1, csc1])
    sh_pair = jnp.stack([sh1, csh1])

    h_win, tok = _exchange_start(w_in[0].T.astype(BF16), "chip", mods, "gather_w_in_start")
    tok, (wo_l, wg_l, wu_l, wd_l) = lax.optimization_barrier((tok, (w_out, w_gate, w_up, w_down)))
    h_wout, tok = _exchange_start(wo_l[0].astype(BF16), "chip", tok, "gather_w_out_start")
    h_wg, tok = _exchange_start(wg_l[0].T.astype(BF16), "chip", tok, "gather_w_gate_start")
    h_wu, tok = _exchange_start(wu_l[0].T.astype(BF16), "chip", tok, "gather_w_up_start")
    h_wd, tok = _exchange_start(wd_l[0].astype(BF16), "chip", tok, "gather_w_down_start")

    cos, sa, sb = _rope_tables(n)
    f_win, tok = _forward_start(_exchange_wait(h_win, "chip", tok, "gather_w_in_wait"), tok, "forward_w_in_start")
    win_g = _forward_wait(f_win, tok, "forward_w_in_wait").reshape(NDEV * IN_SHARD, d)
    u_all, h_all, t_all, kt_b = _qkv_fwd(xs, cts, sc_pair, sh_pair, win_g, q_norm_g, k_norm_g, cos, sa, sb)
    f_wout, tok = _forward_start(_exchange_wait(h_wout, "chip", t_all, "gather_w_out_wait"), t_all, "forward_w_out_start")
    win_bias = _window_bias()
    o_a, lse_a = _attn_window_fwd(t_all, sink_logit, win_bias, tok)
    o, p_b, linv_b = _attn_global_fwd(t_all, o_a)
    f_wg, tok = _forward_start(_exchange_wait(h_wg, "chip", o, "gather_w_gate_wait"), o, "forward_w_gate_start")
    f_wu, tok = _forward_start(_exchange_wait(h_wu, "chip", tok, "gather_w_up_wait"), tok, "forward_w_up_start")
    wout_g = _forward_wait(f_wout, tok, "forward_w_out_wait").reshape(d, d)
    a1, xh1, rs1, u2 = _outproj_ln1(o, wout_g, xs, g1, ln1_g, ln1_b, sc2, sh2, tok)
    f_wd, tok = _forward_start(_exchange_wait(h_wd, "chip", rs1, "gather_w_down_wait"), rs1, "forward_w_down_start")
    ffn_w = (NDEV * FFN_SHARD, d)
    wg_g = _forward_wait(f_wg, tok, "forward_w_gate_wait").reshape(ffn_w)
    wu_g = _forward_wait(f_wu, tok, "forward_w_up_wait").reshape(ffn_w)
    sa_f, sb_f, hf = _ffn_up(u2, wg_g, wu_g, tok)
    wd_g = _forward_wait(f_wd, hf, "forward_w_down_wait").reshape(ffn_w)
    ffn = _ffn_down(hf, wd_g)
    dr2, df, loss_p, acc2 = _ln2_loss(xh1, ffn, tgt, ln1_g, ln1_b, g2, ln2_g, ln2_b)
    loss = lax.psum(loss_p[0, 0], ("x", "y", "c"))

    tk = min(n, 2048)
    parts = (NDEV, FFN_SHARD, d)
    dgp = _ffn_dhf(df, wd_g, sa_f, sb_f)
    dwd_p = _dw_rows(hf, df, NDEV // 2, FFN_PAIR, min(n, 1024), loss_p, "dw_down").reshape(parts)
    h_dwd, tok = _exchange_start(dwd_p, "scatter", loss.reshape(1, 1), "scatter_dw_down_start")
    dwg_t, dwu_t = _dw_gate_up(dgp, u2, tok)
    h_dwg, tok = _exchange_start(dwg_t.reshape(parts), "scatter", tok, "scatter_dw_gate_start")
    h_dwu, tok = _exchange_start(dwu_t.reshape(parts), "scatter", tok, "scatter_dw_up_start")
    du2 = _ffn_du2(dgp, wg_g, wu_g, tok)
    dr1, da1, acc1 = _ln1_bwd(du2, dr2, xh1, rs1, a1, ln1_g, ln1_b, sc2, g1)
    dwo_p = _dw_rows(o, da1, NDEV, 2 * HEAD, tk, loss_p, "dw_out")
    h_dwo, tok = _exchange_start(dwo_p, "scatter", loss_p, "scatter_dw_out_start")
    do = _outproj_bwd(da1, wout_g, tok)
    dqa, dka, dva, dsink = _attn_window_bwd(t_all, o, do, lse_a, sink_logit, win_bias)
    dqb, dkb, dvb = _attn_global_bwd(t_all, kt_b, o, do, p_b, linv_b)
    dh_all, dnorm = _qkv_bwd_prep(dqa, dka, dva, dqb, dkb, dvb, h_all, q_norm_g, k_norm_g, cos, sa, sb)
    grad_x, acc0 = _qkv_bwd(dh_all, win_g, xs, cts, dr1, sc_pair)

    misc = _pad_cols(jnp.concatenate([dnorm[0:1], dnorm[1:2], dsink[:, 0:4, 0].reshape(1, 8)], axis=1), d)
    part = jnp.concatenate([
        acc0[0:2], acc1[4:5], acc1[1:2], acc1[0:1], acc2[2:3],
        acc0[2:4],
        acc1[2:4], acc2[0:2],
        misc, jnp.zeros((3, d), F32)], axis=0)
    gath = _exchange(part, False, "gather_small")
    dm_batch = gath[:, 0:6, :].reshape(NDEV, 6 * d)
    dm_ctx = _pad_cols(gath[:, 6:8, :].reshape(NDEV, 2 * d), 6 * d)
    dm16 = lax.dynamic_slice(jnp.concatenate([dm_batch, dm_ctx], axis=0), (0, me * e_sh), (16, e_sh))
    dw_ada, drow = _ada_bwd(dm16, c_all, w_ada[0])
    dcc = _exchange(drow, False, "gather_dcc")
    dwi_p = _dw_rows(dh_all, u_all, NDEV, IN_SHARD, (n + CTX) // 2, dcc, "dw_in")
    h_dwi, tok = _exchange_start(dwi_p, "scatter", dcc, "scatter_dw_in_start")

    w_s = _pack_small(c_ctx, b_ada, ln1_g, ln1_b, ln2_g, ln2_b, q_norm_g, k_norm_g, sink_logit, d)
    m_s = _pack_small(m_c_ctx, m_b_ada, m_ln1_g, m_ln1_b, m_ln2_g, m_ln2_b, m_q_norm_g, m_k_norm_g, m_sink_logit, d)
    v_s = _pack_small(v_c_ctx, v_b_ada, v_ln1_g, v_ln1_b, v_ln2_g, v_ln2_b, v_q_norm_g, v_k_norm_g, v_sink_logit, d)
    small = [_unpack_small(p, d) for p in _small_update(gath, dcc, c_ctx.reshape(1, d), w_s, m_s, v_s)]

    big = {}
    big["w_ada"] = _adamw(w_ada[0], dw_ada, m_w_ada[0], v_w_ada[0], "adamw_w_ada", after=tok)
    big["w_down"] = _adamw(w_down[0], _exchange_wait(h_dwd, "scatter", big["w_ada"][1], "scatter_dw_down_wait"),
                           m_w_down[0], v_w_down[0], "adamw_w_down")
    late = big["w_down"][1]
    for nm, wt, mt, vt, hd in (("w_gate", w_gate, m_w_gate, v_w_gate, h_dwg), ("w_up", w_up, m_w_up, v_w_up, h_dwu)):
        res = _adamw(wt[0].T, _exchange_wait(hd, "scatter", late, "scatter_d" + nm + "_wait"), mt[0].T, vt[0].T,
                     "adamw_" + nm)
        big[nm] = [r.T for r in res]
        late = res[1]
    big["w_out"] = _adamw(w_out[0], _exchange_wait(h_dwo, "scatter", late, "scatter_dw_out_wait"), m_w_out[0], v_w_out[0],
                          "adamw_w_out")
    big["w_in"] = _adamw_t(w_in[0], _exchange_wait(h_dwi, "scatter", big["w_out"][1], "scatter_dw_in_wait"), m_w_in[0],
                           v_w_in[0], "adamw_w_in")

    names = ["c_ctx", "w_ada", "b_ada", "w_in", "q_norm_g", "k_norm_g", "sink_logit", "w_out", "ln1_g", "ln1_b",
             "w_gate", "w_up", "w_down", "ln2_g", "ln2_b"]
    outs = [loss, grad_x[None]]
    for k in range(4):
        for nm in names:
            outs.append(big[nm][k][None] if nm in big else small[k][nm])
    return tuple(outs)
```

```python
import functools

import jax
import jax.numpy as jnp
from jax import lax
from jax.experimental import pallas as pl
from jax.experimental.pallas import tpu as pltpu

F32 = jnp.float32
BF16 = jnp.bfloat16

NDEV = 8
HEAD = 128
CTX = 256
GRID_W = 64
WINDOW = 128
ROPE_THETA = 10000.0
EPS = 1e-6
SCALE = HEAD ** -0.5
LOG2E = 1.4426950408889634
QK_LOG2 = SCALE * LOG2E
ALPHA = 2.0 ** 0.25
FFN_SHARD = 704
FFN_TILE = 512
FFN_PAIR = 2 * FFN_SHARD
IN_SHARD = 384
NEG = -1e30

ADAM_LR = 0.001
ADAM_B1 = 0.9
ADAM_B2 = 0.999
ADAM_EPS = 1e-08
ADAM_WD = 0.01
ADAM_STEP = 10

VMEM_CAP = 56 * 1024 * 1024

_KINDS = ["rope"] * 10 + ["none"] * 2 + ["qnorm"] * 8 + ["knorm"] * 2 + ["none"] * 2

_NT = (((1,), (1,)), ((), ()))
_TN = (((0,), (0,)), ((), ()))


def _pallas(body, **kw):
    return pl.pallas_call(body, **kw)


def _params(vmem_bytes):
    return pltpu.CompilerParams(vmem_limit_bytes=int(min(VMEM_CAP, vmem_bytes)))


def _mb(n):
    return int(n * 1024 * 1024)


def _sigmoid(x):
    return 1.0 / (1.0 + jnp.exp(-x))


def _colsum(a):
    return jnp.sum(a, axis=0, keepdims=True)


def _rowmean(a):
    return jnp.mean(a, axis=-1, keepdims=True)


def _exchange(src, scatter, name, after=None):
    blk = src.shape[1:] if scatter else src.shape
    after = src if after is None else after

    def body(src_ref, after_ref, out_ref, send_sems, recv_sems, local_sem):
        x, y, c = lax.axis_index("x"), lax.axis_index("y"), lax.axis_index("c")
        me = 4 * x + 2 * y + c
        copies = []
        for t in range(1, NDEV):
            px = 1 - x if (t >> 2) & 1 else x
            py = 1 - y if (t >> 1) & 1 else y
            pc = 1 - c if t & 1 else c
            peer = 4 * px + 2 * py + pc
            cp = pltpu.make_async_remote_copy(
                src_ref=src_ref.at[peer] if scatter else src_ref,
                dst_ref=out_ref.at[me],
                send_sem=send_sems.at[t - 1],
                recv_sem=recv_sems.at[t - 1],
                device_id=(px, py, pc),
                device_id_type=pl.DeviceIdType.MESH,
            )
            cp.start()
            copies.append(cp)
        own = pltpu.make_async_copy(src_ref.at[me] if scatter else src_ref, out_ref.at[me], local_sem)
        own.start()
        for cp in copies:
            cp.wait()
        own.wait()

    return _pallas(
        body, name=name,
        out_shape=jax.ShapeDtypeStruct((NDEV,) + tuple(blk), src.dtype),
        in_specs=[pl.BlockSpec(memory_space=pl.ANY), pl.BlockSpec(memory_space=pl.ANY)],
        out_specs=pl.BlockSpec(memory_space=pl.ANY),
        scratch_shapes=[pltpu.SemaphoreType.DMA((NDEV - 1,)), pltpu.SemaphoreType.DMA((NDEV - 1,)),
                        pltpu.SemaphoreType.DMA(())],
    )(src, after)


_HBM = pl.BlockSpec(memory_space=pltpu.HBM)
_SEM = pl.BlockSpec(memory_space=pltpu.SEMAPHORE)
_ANY = pl.BlockSpec(memory_space=pl.ANY)
_EFFECT = pltpu.SideEffectType.DATAFLOW_SIDE_EFFECTING


def _exchange_copies(src_ref, land_ref, send_sems, recv_sems, mode):
    x, y, c = lax.axis_index("x"), lax.axis_index("y"), lax.axis_index("c")
    me = 4 * x + 2 * y + c
    scatter = mode == "scatter"
    copies = []
    for t in ((1, 2, 4, 6) if mode == "chip" else range(1, NDEV)):
        px = 1 - x if (t >> 2) & 1 else x
        py = 1 - y if (t >> 1) & 1 else y
        pc = 1 - c if t & 1 else c
        peer = 4 * px + 2 * py + pc
        copies.append(pltpu.make_async_remote_copy(
            src_ref=src_ref.at[peer] if scatter else src_ref,
            dst_ref=land_ref.at[me],
            send_sem=send_sems.at[t - 1],
            recv_sem=recv_sems.at[t - 1],
            device_id=(px, py, pc),
            device_id_type=pl.DeviceIdType.MESH,
        ))
    own = pltpu.make_async_copy(src_ref.at[me] if scatter else src_ref, land_ref.at[me], send_sems.at[NDEV - 1])
    return copies, own


def _forward_copies(land_ref, send_sems, recv_sems):
    x, y, c = lax.axis_index("x"), lax.axis_index("y"), lax.axis_index("c")
    copies = []
    for k, t in enumerate((2, 4, 6)):
        px = 1 - x if (t >> 2) & 1 else x
        py = 1 - y if (t >> 1) & 1 else y
        mine, theirs = 4 * px + 2 * py + c, 4 * px + 2 * py + (1 - c)
        send = pltpu.make_async_remote_copy(
            src_ref=land_ref.at[mine], dst_ref=land_ref.at[mine], send_sem=send_sems.at[k], recv_sem=recv_sems.at[k],
            device_id=(x, y, 1 - c), device_id_type=pl.DeviceIdType.MESH)
        recv = pltpu.make_async_remote_copy(
            src_ref=land_ref.at[theirs], dst_ref=land_ref.at[theirs], send_sem=send_sems.at[k], recv_sem=recv_sems.at[k],
            device_id=(x, y, 1 - c), device_id_type=pl.DeviceIdType.MESH)
        copies.append((send, recv))
    return copies


def _forward_start(land, after, name):
    def body(land_ref, after_ref, send_sems, recv_sems, land_thru, token):
        for send, _ in _forward_copies(land_ref, send_sems, recv_sems):
            send.start()
        token[...] = jnp.zeros_like(token)

    res = _pallas(
        body, name=name,
        out_shape=(pltpu.SemaphoreType.DMA((3,)), pltpu.SemaphoreType.DMA((3,)), pltpu.HBM(land.shape, land.dtype),
                   jax.ShapeDtypeStruct((8, HEAD), F32)),
        in_specs=(_HBM, _ANY), out_specs=(_SEM, _SEM, _HBM, pl.BlockSpec(memory_space=pltpu.VMEM)),
        input_output_aliases={0: 2},
        compiler_params=pltpu.CompilerParams(has_side_effects=_EFFECT),
    )(land, after)
    return res[:3], res[3]


def _forward_wait(handle, after, name):
    send_sems, recv_sems, land_thru = handle

    def body(land_ref, send_sems, recv_sems, after_ref, got_ref):
        for send, recv in _forward_copies(land_ref, send_sems, recv_sems):
            send.wait_send()
            recv.wait_recv()

    return _pallas(
        body, name=name,
        out_shape=pltpu.HBM(land_thru.shape, land_thru.dtype),
        in_specs=(_HBM, _SEM, _SEM, _ANY), out_specs=_HBM,
        input_output_aliases={0: 0},
        compiler_params=pltpu.CompilerParams(has_side_effects=_EFFECT),
    )(land_thru, send_sems, recv_sems, after)


def _exchange_start(src, mode, after, name):
    blk = src.shape[1:] if mode == "scatter" else src.shape
    land = lax.empty((NDEV,) + tuple(blk), src.dtype)

    def body(src_ref, land_ref, after_ref, send_sems, recv_sems, src_thru, land_thru, token):
        copies, own = _exchange_copies(src_ref, land_ref, send_sems, recv_sems, mode)
        for cp in copies:
            cp.start()
        own.start()
        token[...] = jnp.zeros_like(token)

    res = _pallas(
        body, name=name,
        out_shape=(pltpu.SemaphoreType.DMA((NDEV,)), pltpu.SemaphoreType.DMA((NDEV,)),
                   pltpu.HBM(src.shape, src.dtype), pltpu.HBM(land.shape, land.dtype),
                   jax.ShapeDtypeStruct((8, HEAD), F32)),
        in_specs=(_HBM, _HBM, _ANY), out_specs=(_SEM, _SEM, _HBM, _HBM, pl.BlockSpec(memory_space=pltpu.VMEM)),
        input_output_aliases={0: 2, 1: 3},
        compiler_params=pltpu.CompilerParams(has_side_effects=_EFFECT),
    )(pltpu.with_memory_space_constraint(src, pltpu.HBM), pltpu.with_memory_space_constraint(land, pltpu.HBM), after)
    return res[:4], res[4]


def _exchange_wait(handle, mode, after, name):
    send_sems, recv_sems, src_thru, land_thru = handle

    def body(src_ref, land_ref, send_sems, recv_sems, after_ref, src_dead, got_ref):
        copies, own = _exchange_copies(src_ref, land_ref, send_sems, recv_sems, mode)
        for cp in copies:
            cp.wait_send()
            cp.wait_recv()
        own.wait()

    return _pallas(
        body, name=name,
        out_shape=(pltpu.HBM(src_thru.shape, src_thru.dtype), pltpu.HBM(land_thru.shape, land_thru.dtype)),
        in_specs=(_HBM, _HBM, _SEM, _SEM, _ANY), out_specs=(_HBM, _HBM),
        input_output_aliases={0: 0, 1: 1},
        compiler_params=pltpu.CompilerParams(has_side_effects=_EFFECT),
    )(src_thru, land_thru, send_sems, recv_sems, after)[1]


def _ada_fwd(c_all, w, bias):
    r, d = c_all.shape
    e = w.shape[1]
    tn = 512

    def body(c_ref, w_ref, b_ref, o_ref):
        cv = c_ref[...]
        s = (cv * _sigmoid(cv)).astype(BF16)
        o_ref[...] = jnp.dot(s, w_ref[...].astype(BF16), preferred_element_type=F32) + b_ref[...]

    return _pallas(
        body, name="ada_fwd", grid=(e // tn,),
        out_shape=jax.ShapeDtypeStruct((r, e), F32),
        in_specs=[pl.BlockSpec((r, d), lambda j: (0, 0)), pl.BlockSpec((d, tn), lambda j: (0, j)),
                  pl.BlockSpec((1, tn), lambda j: (0, j))],
        out_specs=pl.BlockSpec((r, tn), lambda j: (0, j)),
        compiler_params=_params(_mb(24)),
    )(c_all, w, bias)


def _ada_bwd(dm16, c_all, w):
    d, e = w.shape
    tn = 512

    def body(dm_ref, c_ref, w_ref, dw_ref, dr_ref):
        j = pl.program_id(0)
        dm = dm_ref[...]
        rid = lax.broadcasted_iota(jnp.int32, dm.shape, 0)
        ctx_sum = jnp.sum(jnp.where(rid >= 8, dm, 0.0), axis=0, keepdims=True)
        rows = jnp.where(rid < 8, dm, jnp.where(rid == 8, jnp.broadcast_to(ctx_sum, dm.shape), 0.0)).astype(BF16)
        cv = c_ref[...]
        s = (cv * _sigmoid(cv)).astype(BF16)
        dw_ref[...] = lax.dot_general(s, rows, _TN, preferred_element_type=F32)
        part = lax.dot_general(rows, w_ref[...].astype(BF16), _NT, preferred_element_type=F32)

        @pl.when(j == 0)
        def _():
            dr_ref[...] = part

        @pl.when(j > 0)
        def _():
            dr_ref[...] += part

    return _pallas(
        body, name="ada_bwd", grid=(e // tn,),
        out_shape=(jax.ShapeDtypeStruct((d, e), F32), jax.ShapeDtypeStruct((16, d), F32)),
        in_specs=[pl.BlockSpec((16, tn), lambda j: (0, j)), pl.BlockSpec((16, d), lambda j: (0, 0)),
                  pl.BlockSpec((d, tn), lambda j: (0, j))],
        out_specs=(pl.BlockSpec((d, tn), lambda j: (0, j)), pl.BlockSpec((16, d), lambda j: (0, 0))),
        compiler_params=_params(_mb(32)),
    )(dm16, c_all, w)


def _rope(v, cos, sa, sb):
    return v * cos + (pltpu.roll(v, 96, 1) * sa + pltpu.roll(v, 32, 1) * sb)


def _rope_t(dt, cos, sa, sb):
    return dt * cos + (pltpu.roll(dt * sa, 32, 1) + pltpu.roll(dt * sb, 96, 1))


def _qkv_fwd(x, ct, sc, sh, wint, qg, kg, cos, sa, sb):
    n, d = x.shape
    tm = CTX
    nlat = n // tm
    na = n + CTX
    wcols = wint.shape[0]

    def body(x_ref, ct_ref, sc_ref, sh_ref, w_ref, qg_ref, kg_ref, cos_ref, sa_ref, sb_ref, u_ref, h_ref, t_ref, kt_ref):
        i = pl.program_id(0)
        xin = jnp.where(i == nlat, ct_ref[...], x_ref[...])
        u = (xin * (1.0 + sc_ref[0]) + sh_ref[0]).astype(BF16)
        u_ref[...] = u
        cos, sa, sb = cos_ref[...], sa_ref[...], sb_ref[...]
        h = lax.dot_general(u, w_ref[...], _NT, preferred_element_type=F32)
        h_ref[...] = h
        for hd in range(24):
            v = h[:, hd * HEAD:(hd + 1) * HEAD]
            kind = _KINDS[hd]
            if kind == "qnorm":
                v = v * lax.rsqrt(_rowmean(v * v) + EPS) * qg_ref[...]
            elif kind == "knorm":
                v = v * lax.rsqrt(_rowmean(v * v) + EPS) * kg_ref[...]
            if kind != "none":
                v = _rope(v, cos, sa, sb)
            t_ref[:, hd * HEAD:(hd + 1) * HEAD] = v.astype(BF16)
            if kind == "knorm":
                kt_ref[(hd - 20) * HEAD:(hd - 19) * HEAD, :] = v.T.astype(BF16)

    lat = lambda i: (jnp.minimum(i, nlat - 1), 0)
    row = lambda i: (i, 0)
    const2 = lambda i: (0, 0)
    return _pallas(
        body, name="qkv_fwd", grid=(nlat + 1,),
        out_shape=(jax.ShapeDtypeStruct((na, d), BF16), jax.ShapeDtypeStruct((na, wcols), F32),
                   jax.ShapeDtypeStruct((na, wcols), BF16), jax.ShapeDtypeStruct((2 * HEAD, na), BF16)),
        in_specs=[pl.BlockSpec((tm, d), lat), pl.BlockSpec((tm, d), const2),
                  pl.BlockSpec((1, 1, d), lambda i: (i // nlat, 0, 0)),
                  pl.BlockSpec((1, 1, d), lambda i: (i // nlat, 0, 0)),
                  pl.BlockSpec((wcols, d), const2),
                  pl.BlockSpec((1, HEAD), const2), pl.BlockSpec((1, HEAD), const2),
                  pl.BlockSpec((tm, HEAD), row), pl.BlockSpec((tm, HEAD), row), pl.BlockSpec((tm, HEAD), row)],
        out_specs=(pl.BlockSpec((tm, d), row), pl.BlockSpec((tm, wcols), row), pl.BlockSpec((tm, wcols), row),
                   pl.BlockSpec((2 * HEAD, tm), lambda i: (0, i))),
        compiler_params=_params(_mb(56)),
    )(x, ct, sc, sh, wint, qg, kg, cos, sa, sb)


def _qkv_bwd_prep(dqa, dka, dva, dqb, dkb, dvb, h_all, qg, kg, cos, sa, sb):
    na, wcols = h_all.shape
    n = na - CTX
    tm = CTX
    nlat = n // tm

    def body(dqa_ref, dka_ref, dva_ref, dqb_ref, dkb_ref, dvb_ref, h_ref, qg_ref, kg_ref, cos_ref, sa_ref, sb_ref,
             dh_ref, dg_ref):
        i = pl.program_id(0)

        @pl.when(i == 0)
        def _():
            dg_ref[...] = jnp.zeros_like(dg_ref)

        cos, sa, sb = cos_ref[...], sa_ref[...], sb_ref[...]
        is_lat = i < nlat
        for hd in range(24):
            kind = _KINDS[hd]
            if hd < 8:
                dt = jnp.where(is_lat, dqa_ref[:, hd * HEAD:(hd + 1) * HEAD], 0.0)
            elif hd < 10:
                dt = dka_ref[:, (hd - 8) * HEAD:(hd - 7) * HEAD]
            elif hd < 12:
                dt = dva_ref[:, (hd - 10) * HEAD:(hd - 9) * HEAD]
            elif hd < 20:
                dt = jnp.where(is_lat, dqb_ref[:, (hd - 12) * HEAD:(hd - 11) * HEAD], 0.0)
            elif hd < 22:
                dt = dkb_ref[:, (hd - 20) * HEAD:(hd - 19) * HEAD]
            else:
                dt = dvb_ref[:, (hd - 22) * HEAD:(hd - 21) * HEAD]
            if kind != "none":
                dt = _rope_t(dt, cos, sa, sb)
            if kind in ("qnorm", "knorm"):
                g_ref = qg_ref if kind == "qnorm" else kg_ref
                r0 = 0 if kind == "qnorm" else 1
                xv = h_ref[:, hd * HEAD:(hd + 1) * HEAD]
                xn = xv * lax.rsqrt(_rowmean(xv * xv) + EPS)
                dg_ref[r0:r0 + 1, :] += _colsum(dt * xn)
                dxn = dt * g_ref[...]
                dt = lax.rsqrt(_rowmean(xv * xv) + EPS) * (dxn - xn * _rowmean(dxn * xn))
            dh_ref[:, hd * HEAD:(hd + 1) * HEAD] = dt.astype(BF16)

    lat = lambda i: (jnp.minimum(i, nlat - 1), 0)
    row = lambda i: (i, 0)
    const2 = lambda i: (0, 0)
    return _pallas(
        body, name="qkv_bwd_prep", grid=(nlat + 1,),
        out_shape=(jax.ShapeDtypeStruct((na, wcols), BF16), jax.ShapeDtypeStruct((8, HEAD), F32)),
        in_specs=[pl.BlockSpec((tm, 8 * HEAD), lat), pl.BlockSpec((tm, 2 * HEAD), row), pl.BlockSpec((tm, 2 * HEAD), row),
                  pl.BlockSpec((tm, 8 * HEAD), lat), pl.BlockSpec((tm, 2 * HEAD), row), pl.BlockSpec((tm, 2 * HEAD), row),
                  pl.BlockSpec((tm, wcols), row),
                  pl.BlockSpec((1, HEAD), const2), pl.BlockSpec((1, HEAD), const2),
                  pl.BlockSpec((tm, HEAD), row), pl.BlockSpec((tm, HEAD), row), pl.BlockSpec((tm, HEAD), row)],
        out_specs=(pl.BlockSpec((tm, wcols), row), pl.BlockSpec((8, HEAD), const2)),
        compiler_params=_params(_mb(40)),
    )(dqa, dka, dva, dqb, dkb, dvb, h_all, qg, kg, cos, sa, sb)


def _window_keys(k_ref, v_ref, n, na):
    i = pl.program_id(1)
    tq = WINDOW
    start = pl.multiple_of(jnp.clip((i - 1) * tq, 0, n - 3 * tq), tq)
    kk = jnp.concatenate([k_ref[pl.ds(start, 3 * tq), :], k_ref[n:na, :]], axis=0)
    vv = jnp.concatenate([v_ref[pl.ds(start, 3 * tq), :], v_ref[n:na, :]], axis=0)
    return kk, vv, start


def _window_bias():
    tq = WINDOW
    r = (jnp.arange(4 * tq) % tq)[:, None]
    c = jnp.arange(3 * tq + CTX)[None, :]
    variants = []
    for back in (0, tq, 2 * tq):
        seen = (jnp.abs(back + r - c) <= WINDOW) | (c >= 3 * tq)
        variants.append(jnp.where(seen, 0.0, NEG).astype(F32))
    return jnp.stack(variants)


def _window_bias_spec(nq):
    return pl.BlockSpec((1, 4 * WINDOW, 3 * WINDOW + CTX),
                        lambda kv, i: (jnp.where(i == 0, 0, jnp.where(i == nq - 1, 2, 1)), 0, 0))


def _stack_heads(ref, width=HEAD):
    return jnp.concatenate([ref[:, g * HEAD:g * HEAD + width] for g in range(4)], axis=0)


def _sink_column(sink_ref, kv, tq):
    grp = lax.broadcasted_iota(jnp.int32, (4 * tq, 1), 0) // tq
    col = jnp.zeros((4 * tq, 1), F32)
    for g in range(4):
        col = jnp.where(grp == g, sink_ref[0, 4 * kv + g] * LOG2E, col)
    return col


def _attn_window_fwd(t_all, sink, bias, after):
    na = t_all.shape[0]
    n = na - CTX
    tq = WINDOW

    def body(sink_ref, q_ref, k_ref, v_ref, bias_ref, after_ref, o_ref, lse_ref):
        kv = pl.program_id(0)
        kk, vv, _ = _window_keys(k_ref, v_ref, n, na)
        t = lax.dot_general(_stack_heads(q_ref), kk, _NT, preferred_element_type=F32) * QK_LOG2 + bias_ref[0]
        sk = _sink_column(sink_ref, kv, tq)
        m = jnp.maximum(jnp.max(t, axis=-1, keepdims=True), sk)
        p = jnp.exp2(t - m)
        l = jnp.sum(p, axis=-1, keepdims=True) + jnp.exp2(sk - m)
        o = jnp.dot(p.astype(BF16), vv, preferred_element_type=F32) * (1.0 / l)
        lse = m + jnp.log2(l)
        for g in range(4):
            o_ref[:, g * HEAD:(g + 1) * HEAD] = o[g * tq:(g + 1) * tq]
            lse_ref[:, g * HEAD:(g + 1) * HEAD] = jnp.broadcast_to(lse[g * tq:(g + 1) * tq], (tq, HEAD))

    blk = pl.BlockSpec((tq, 4 * HEAD), lambda kv, i: (i, kv))
    return _pallas(
        body, name="attn_window_fwd", grid=(2, n // tq),
        out_shape=(jax.ShapeDtypeStruct((n, 16 * HEAD), F32), jax.ShapeDtypeStruct((n, 8 * HEAD), F32)),
        in_specs=[pl.BlockSpec(memory_space=pltpu.SMEM), blk,
                  pl.BlockSpec((na, HEAD), lambda kv, i: (0, 8 + kv)),
                  pl.BlockSpec((na, HEAD), lambda kv, i: (0, 10 + kv)), _window_bias_spec(n // tq), _ANY],
        out_specs=(blk, blk),
        compiler_params=_params(_mb(32)),
    )(sink, t_all, t_all, t_all, bias, after)


def _attn_global_fwd(t_all, o_part):
    na = t_all.shape[0]
    n = na - CTX
    tq = 256

    def body(q_ref, k_ref, v_ref, o_in_ref, o_ref, p_ref, linv_ref):
        kk, vv = k_ref[...], v_ref[...]
        for g in range(4):
            q = q_ref[:, g * HEAD:(g + 1) * HEAD]
            t = lax.dot_general(q, kk, _NT, preferred_element_type=F32) * QK_LOG2
            m = jnp.max(t, axis=-1, keepdims=True)
            p = jnp.exp2(t - m)
            linv = 1.0 / jnp.sum(p, axis=-1, keepdims=True)
            pb = p.astype(BF16)
            p_ref[g] = pb
            o_ref[:, g * HEAD:(g + 1) * HEAD] = jnp.dot(pb, vv, preferred_element_type=F32) * linv
            linv_ref[:, g * HEAD:(g + 1) * HEAD] = jnp.broadcast_to(linv, (tq, HEAD))

    return _pallas(
        body, name="attn_global_fwd", grid=(2, n // tq),
        out_shape=(jax.ShapeDtypeStruct((n, 16 * HEAD), F32), jax.ShapeDtypeStruct((8, n, na), BF16),
                   jax.ShapeDtypeStruct((n, 8 * HEAD), F32)),
        in_specs=[pl.BlockSpec((tq, 4 * HEAD), lambda kv, i: (i, 3 + kv)),
                  pl.BlockSpec((na, HEAD), lambda kv, i: (0, 20 + kv)),
                  pl.BlockSpec((na, HEAD), lambda kv, i: (0, 22 + kv)), _ANY],
        out_specs=(pl.BlockSpec((tq, 4 * HEAD), lambda kv, i: (i, 2 + kv)),
                   pl.BlockSpec((4, tq, na), lambda kv, i: (kv, i, 0)),
                   pl.BlockSpec((tq, 4 * HEAD), lambda kv, i: (i, kv))),
        input_output_aliases={3: 0},
        compiler_params=_params(_mb(56)),
    )(t_all, t_all, t_all, o_part)


def _attn_window_bwd(t_all, o, do, lse, sink, bias):
    na = t_all.shape[0]
    n = na - CTX
    tq = WINDOW

    def body(sink_ref, q_ref, k_ref, v_ref, o_ref, do_ref, lse_ref, bias_ref, dq_ref, dk_ref, dv_ref, dsink_ref):
        kv = pl.program_id(0)

        @pl.when(pl.program_id(1) == 0)
        def _():
            dk_ref[...] = jnp.zeros_like(dk_ref)
            dv_ref[...] = jnp.zeros_like(dv_ref)
            dsink_ref[...] = jnp.zeros_like(dsink_ref)

        kk, vv, start = _window_keys(k_ref, v_ref, n, na)
        q = _stack_heads(q_ref)
        t = lax.dot_general(q, kk, _NT, preferred_element_type=F32) * QK_LOG2 + bias_ref[0]
        lse = _stack_heads(lse_ref, 1)
        p = jnp.exp2(t - lse)
        dof = _stack_heads(do_ref)
        delta = jnp.sum(dof * _stack_heads(o_ref), axis=-1, keepdims=True)
        dob = dof.astype(BF16)
        dv_acc = lax.dot_general(p.astype(BF16), dob, _TN, preferred_element_type=F32)
        dp = lax.dot_general(dob, vv, _NT, preferred_element_type=F32)
        ds = (p * (dp - delta) * SCALE).astype(BF16)
        dq = jnp.dot(ds, kk, preferred_element_type=F32)
        dk_acc = lax.dot_general(ds, q, _TN, preferred_element_type=F32)
        dsk = -(jnp.exp2(_sink_column(sink_ref, kv, tq) - lse) * delta)
        for g in range(4):
            dq_ref[:, g * HEAD:(g + 1) * HEAD] = dq[g * tq:(g + 1) * tq]
            dsink_ref[0, g:g + 1, :] += jnp.broadcast_to(_colsum(dsk[g * tq:(g + 1) * tq]), (1, HEAD))
        dk_ref[pl.ds(start, 3 * tq), :] += dk_acc[:3 * tq]
        dv_ref[pl.ds(start, 3 * tq), :] += dv_acc[:3 * tq]
        dk_ref[n:na, :] += dk_acc[3 * tq:]
        dv_ref[n:na, :] += dv_acc[3 * tq:]

    blk = pl.BlockSpec((tq, 4 * HEAD), lambda kv, i: (i, kv))
    kvout = pl.BlockSpec((na, HEAD), lambda kv, i: (0, kv))
    return _pallas(
        body, name="attn_window_bwd", grid=(2, n // tq),
        out_shape=(jax.ShapeDtypeStruct((n, 8 * HEAD), F32), jax.ShapeDtypeStruct((na, 2 * HEAD), F32),
                   jax.ShapeDtypeStruct((na, 2 * HEAD), F32), jax.ShapeDtypeStruct((2, 8, HEAD), F32)),
        in_specs=[pl.BlockSpec(memory_space=pltpu.SMEM), blk,
                  pl.BlockSpec((na, HEAD), lambda kv, i: (0, 8 + kv)),
                  pl.BlockSpec((na, HEAD), lambda kv, i: (0, 10 + kv)),
                  blk, blk, blk, _window_bias_spec(n // tq)],
        out_specs=(blk, kvout, kvout, pl.BlockSpec((1, 8, HEAD), lambda kv, i: (kv, 0, 0))),
        compiler_params=_params(_mb(40)),
    )(sink, t_all, t_all, t_all, o, do, lse, bias)


def _attn_global_bwd(t_all, kt, o, do, p_all, linv):
    na = t_all.shape[0]
    n = na - CTX
    tq = 256

    def body(q_ref, v_ref, kt_ref, o_ref, do_ref, p_ref, linv_ref, dq_ref, dk_ref, dv_ref, dkt_acc, dvt_acc):
        i = pl.program_id(1)

        @pl.when(i == 0)
        def _():
            dkt_acc[...] = jnp.zeros_like(dkt_acc)
            dvt_acc[...] = jnp.zeros_like(dvt_acc)

        vv, kt_v = v_ref[...], kt_ref[...]
        dkt = jnp.zeros((HEAD, na), F32)
        dvt = jnp.zeros((HEAD, na), F32)
        for g in range(4):
            q = q_ref[:, g * HEAD:(g + 1) * HEAD]
            p = p_ref[g].astype(F32) * linv_ref[:, g * HEAD:g * HEAD + 1]
            dof = do_ref[:, g * HEAD:(g + 1) * HEAD]
            delta = jnp.sum(dof * o_ref[:, g * HEAD:(g + 1) * HEAD], axis=-1, keepdims=True)
            dob = dof.astype(BF16)
            dvt = dvt + lax.dot_general(dob, p.astype(BF16), _TN, preferred_element_type=F32)
            dp = lax.dot_general(dob, vv, _NT, preferred_element_type=F32)
            ds = (p * (dp - delta) * SCALE).astype(BF16)
            dq_ref[:, g * HEAD:(g + 1) * HEAD] = lax.dot_general(kt_v, ds, _NT, preferred_element_type=F32).T
            dkt = dkt + lax.dot_general(q, ds, _TN, preferred_element_type=F32)
        dkt_acc[...] += dkt
        dvt_acc[...] += dvt

        @pl.when(i == pl.num_programs(1) - 1)
        def _():
            dk_ref[...] = dkt_acc[...].T
            dv_ref[...] = dvt_acc[...].T

    ospec = pl.BlockSpec((tq, 4 * HEAD), lambda kv, i: (i, 2 + kv))
    lspec = pl.BlockSpec((tq, 4 * HEAD), lambda kv, i: (i, kv))
    kvout = pl.BlockSpec((na, HEAD), lambda kv, i: (0, kv))
    return _pallas(
        body, name="attn_global_bwd", grid=(2, n // tq),
        out_shape=(jax.ShapeDtypeStruct((n, 8 * HEAD), F32), jax.ShapeDtypeStruct((na, 2 * HEAD), F32),
                   jax.ShapeDtypeStruct((na, 2 * HEAD), F32)),
        in_specs=[pl.BlockSpec((tq, 4 * HEAD), lambda kv, i: (i, 3 + kv)),
                  pl.BlockSpec((na, HEAD), lambda kv, i: (0, 22 + kv)),
                  pl.BlockSpec((HEAD, na), lambda kv, i: (kv, 0)),
                  ospec, ospec, pl.BlockSpec((4, tq, na), lambda kv, i: (kv, i, 0)), lspec],
        out_specs=(lspec, kvout, kvout),
        scratch_shapes=[pltpu.VMEM((HEAD, na), F32), pltpu.VMEM((HEAD, na), F32)],
        compiler_params=_params(_mb(56)),
    )(t_all, t_all, kt, o, do, p_all, linv)


def _outproj_ln1(o, wout, x, g1, lg, lb, sc2, sh2, after):
    n, d = x.shape
    tm = 256

    def body(o_ref, w_ref, x_ref, g1_ref, lg_ref, lb_ref, sc_ref, sh_ref, after_ref, a_ref, xh_ref, rs_ref, u_ref):
        a1 = jnp.dot(o_ref[...].astype(BF16), w_ref[...], preferred_element_type=F32)
        a_ref[...] = a1
        r = ALPHA * x_ref[...] + g1_ref[...] * a1
        dlt = r - _rowmean(r)
        rstd = lax.rsqrt(_rowmean(dlt * dlt) + EPS)
        xh = dlt * rstd
        xh_ref[...] = xh
        rs_ref[...] = rstd
        x1 = xh * lg_ref[...] + lb_ref[...]
        u_ref[...] = (x1 * (1.0 + sc_ref[...]) + sh_ref[...]).astype(BF16)

    row = lambda i: (i, 0)
    const2 = lambda i: (0, 0)
    vec = pl.BlockSpec((1, d), const2)
    big = pl.BlockSpec((tm, d), row)
    return _pallas(
        body, name="outproj_ln1", grid=(n // tm,),
        out_shape=(jax.ShapeDtypeStruct((n, d), F32), jax.ShapeDtypeStruct((n, d), F32),
                   jax.ShapeDtypeStruct((n, 1), F32), jax.ShapeDtypeStruct((n, d), BF16)),
        in_specs=[big, pl.BlockSpec((d, d), const2), big, vec, vec, vec, vec, vec, _ANY],
        out_specs=(big, big, pl.BlockSpec((tm, 1), row), big),
        compiler_params=_params(_mb(56)),
    )(o, wout, x, g1, lg, lb, sc2, sh2, after)


def _ffn_up(u2, wgt, wut, after):
    n, d = u2.shape
    f = wgt.shape[0]
    tm = min(1024, n)

    def body(u_ref, wg_ref, wu_ref, after_ref, sa_ref, sb_ref, hf_ref):
        u = u_ref[...]
        gv = lax.dot_general(u, wg_ref[...], _NT, preferred_element_type=F32)
        pv = lax.dot_general(u, wu_ref[...], _NT, preferred_element_type=F32)
        sg = _sigmoid(gv)
        silu = gv * sg
        sa_ref[...] = silu.astype(BF16)
        sb_ref[...] = (pv * (sg * (1.0 + gv * (1.0 - sg)))).astype(BF16)
        hf_ref[...] = (silu * pv).astype(BF16)

    tile = pl.BlockSpec((tm, FFN_TILE), lambda i, j: (i, j))
    wspec = pl.BlockSpec((FFN_TILE, d), lambda i, j: (j, 0))
    sds = jax.ShapeDtypeStruct((n, f), BF16)
    return _pallas(
        body, name="ffn_up", grid=(n // tm, f // FFN_TILE),
        out_shape=(sds, sds, sds),
        in_specs=[pl.BlockSpec((tm, d), lambda i, j: (i, 0)), wspec, wspec, _ANY],
        out_specs=(tile, tile, tile),
        compiler_params=_params(_mb(48)),
    )(u2, wgt, wut, after)


def _ffn_down(hf, wd):
    n, f = hf.shape
    d = wd.shape[1]
    tm, tn = min(1024, n), 512

    def body(h_ref, w_ref, o_ref):
        o_ref[...] = jnp.dot(h_ref[...], w_ref[...], preferred_element_type=F32)

    return _pallas(
        body, name="ffn_down", grid=(n // tm, d // tn),
        out_shape=jax.ShapeDtypeStruct((n, d), F32),
        in_specs=[pl.BlockSpec((tm, f), lambda i, j: (i, 0)), pl.BlockSpec((f, tn), lambda i, j: (0, j))],
        out_specs=pl.BlockSpec((tm, tn), lambda i, j: (i, j)),
        compiler_params=_params(_mb(56)),
    )(hf, wd)


def _ln2_loss(xh1, ffn, tgt, lg1, lb1, g2, lg2, lb2):
    n, d = xh1.shape
    tm = 256

    def body(xh_ref, f_ref, t_ref, lg1_ref, lb1_ref, g2_ref, lg2_ref, lb2_ref, dr_ref, df_ref, loss_ref, acc_ref):
        @pl.when(pl.program_id(0) == 0)
        def _():
            loss_ref[...] = jnp.zeros_like(loss_ref)
            acc_ref[...] = jnp.zeros_like(acc_ref)

        x1 = xh_ref[...] * lg1_ref[...] + lb1_ref[...]
        fv = f_ref[...]
        r = ALPHA * x1 + g2_ref[...] * fv
        dlt = r - _rowmean(r)
        rstd = lax.rsqrt(_rowmean(dlt * dlt) + EPS)
        xh2 = dlt * rstd
        err = xh2 * lg2_ref[...] + lb2_ref[...] - t_ref[...]
        loss_ref[...] += 0.5 * jnp.sum(_rowmean(err * err))
        dy = err * (1.0 / d)
        dyg = dy * lg2_ref[...]
        dr = rstd * (dyg - _rowmean(dyg) - xh2 * _rowmean(dyg * xh2))
        dr_ref[...] = dr
        df_ref[...] = (g2_ref[...] * dr).astype(BF16)
        acc_ref[0:1, :] += _colsum(dy * xh2)
        acc_ref[1:2, :] += _colsum(dy)
        acc_ref[2:3, :] += _colsum(dr * fv)

    row = lambda i: (i, 0)
    const2 = lambda i: (0, 0)
    vec = pl.BlockSpec((1, d), const2)
    big = pl.BlockSpec((tm, d), row)
    return _pallas(
        body, name="ln2_loss", grid=(n // tm,),
        out_shape=(jax.ShapeDtypeStruct((n, d), F32), jax.ShapeDtypeStruct((n, d), BF16),
                   jax.ShapeDtypeStruct((8, HEAD), F32), jax.ShapeDtypeStruct((8, d), F32)),
        in_specs=[big, big, big, vec, vec, vec, vec, vec],
        out_specs=(big, big, pl.BlockSpec((8, HEAD), const2), pl.BlockSpec((8, d), const2)),
        compiler_params=_params(_mb(48)),
    )(xh1, ffn, tgt, lg1, lb1, g2, lg2, lb2)


def _ffn_dhf(df, wd, sa, sb):
    n, d = df.shape
    f = sa.shape[1]
    tm = min(2048, n)

    def body(df_ref, w_ref, sa_ref, sb_ref, dgp_ref):
        dhf = lax.dot_general(df_ref[...], w_ref[...], _NT, preferred_element_type=F32)
        dgp_ref[:, :FFN_TILE] = (dhf * sb_ref[...].astype(F32)).astype(BF16)
        dgp_ref[:, FFN_TILE:] = (dhf * sa_ref[...].astype(F32)).astype(BF16)

    tile = pl.BlockSpec((tm, FFN_TILE), lambda i, j: (i, j))
    return _pallas(
        body, name="ffn_dhf", grid=(n // tm, f // FFN_TILE),
        out_shape=jax.ShapeDtypeStruct((n, 2 * f), BF16),
        in_specs=[pl.BlockSpec((tm, d), lambda i, j: (i, 0)), pl.BlockSpec((FFN_TILE, d), lambda i, j: (j, 0)),
                  tile, tile],
        out_specs=pl.BlockSpec((tm, 2 * FFN_TILE), lambda i, j: (i, j)),
        compiler_params=_params(_mb(48)),
    )(df, wd, sa, sb)


def _ffn_du2(dgp, wgt, wut, after):
    n = dgp.shape[0]
    f, d = wgt.shape
    tm = min(1024, n)

    def body(dgp_ref, wg_ref, wu_ref, after_ref, o_ref):
        w = jnp.concatenate([wg_ref[...], wu_ref[...]], axis=0)
        part = jnp.dot(dgp_ref[...], w, preferred_element_type=F32)

        @pl.when(pl.program_id(1) == 0)
        def _():
            o_ref[...] = part

        @pl.when(pl.program_id(1) > 0)
        def _():
            o_ref[...] += part

    wspec = pl.BlockSpec((FFN_TILE, d), lambda i, j: (j, 0))
    return _pallas(
        body, name="ffn_du2", grid=(n // tm, f // FFN_TILE),
        out_shape=jax.ShapeDtypeStruct((n, d), F32),
        in_specs=[pl.BlockSpec((tm, 2 * FFN_TILE), lambda i, j: (i, j)), wspec, wspec, _ANY],
        out_specs=pl.BlockSpec((tm, d), lambda i, j: (i, 0)),
        compiler_params=_params(_mb(48)),
    )(dgp, wgt, wut, after)


def _dw_gate_up(dgp, u2, after):
    n, d = u2.shape
    f = dgp.shape[1] // 2
    tm = min(1024, n)

    def body(a_ref, b_ref, after_ref, og_ref, ou_ref, acc_ref):
        part = lax.dot_general(a_ref[...], b_ref[...], _TN, preferred_element_type=F32)
        i = pl.program_id(1)

        @pl.when(i == 0)
        def _():
            acc_ref[...] = part

        @pl.when(i > 0)
        def _():
            acc_ref[...] += part

        @pl.when(i == pl.num_programs(1) - 1)
        def _():
            og_ref[...] = acc_ref[:FFN_TILE].astype(BF16)
            ou_ref[...] = acc_ref[FFN_TILE:].astype(BF16)

    out = pl.BlockSpec((FFN_TILE, d), lambda j, i: (j, 0))
    sds = jax.ShapeDtypeStruct((f, d), BF16)
    return _pallas(
        body, name="dw_gate_up", grid=(f // FFN_TILE, n // tm),
        out_shape=(sds, sds),
        in_specs=[pl.BlockSpec((tm, 2 * FFN_TILE), lambda j, i: (i, j)), pl.BlockSpec((tm, d), lambda j, i: (i, 0)), _ANY],
        out_specs=(out, out),
        scratch_shapes=[pltpu.VMEM((2 * FFN_TILE, d), F32)],
        compiler_params=_params(_mb(56)),
    )(dgp, u2, after)


def _ln1_bwd(du2, dr2, xh1, rs1, a1, lg1, lb1, sc2, g1):
    n, d = du2.shape
    tm = 256

    def body(du_ref, dr2_ref, xh_ref, rs_ref, a_ref, lg_ref, lb_ref, sc_ref, g1_ref, dr1_ref, da_ref, acc_ref):
        @pl.when(pl.program_id(0) == 0)
        def _():
            acc_ref[...] = jnp.zeros_like(acc_ref)

        du = du_ref[...]
        xh = xh_ref[...]
        x1 = xh * lg_ref[...] + lb_ref[...]
        dx1 = ALPHA * dr2_ref[...] + du * (1.0 + sc_ref[...])
        dxg = dx1 * lg_ref[...]
        dr1 = rs_ref[...] * (dxg - _rowmean(dxg) - xh * _rowmean(dxg * xh))
        dr1_ref[...] = dr1
        da_ref[...] = (g1_ref[...] * dr1).astype(BF16)
        acc_ref[0:1, :] += _colsum(du * x1)
        acc_ref[1:2, :] += _colsum(du)
        acc_ref[2:3, :] += _colsum(dx1 * xh)
        acc_ref[3:4, :] += _colsum(dx1)
        acc_ref[4:5, :] += _colsum(dr1 * a_ref[...])

    row = lambda i: (i, 0)
    const2 = lambda i: (0, 0)
    vec = pl.BlockSpec((1, d), const2)
    big = pl.BlockSpec((tm, d), row)
    return _pallas(
        body, name="ln1_bwd", grid=(n // tm,),
        out_shape=(jax.ShapeDtypeStruct((n, d), F32), jax.ShapeDtypeStruct((n, d), BF16),
                   jax.ShapeDtypeStruct((8, d), F32)),
        in_specs=[big, big, big, pl.BlockSpec((tm, 1), row), big, vec, vec, vec, vec],
        out_specs=(big, big, pl.BlockSpec((8, d), const2)),
        compiler_params=_params(_mb(48)),
    )(du2, dr2, xh1, rs1, a1, lg1, lb1, sc2, g1)


def _dw_rows(a, b, nblk, bw, tm, after, name):
    m = a.shape[0]
    nn = b.shape[1]

    def body(a_ref, b_ref, after_ref, o_ref, acc_ref):
        part = lax.dot_general(a_ref[...].astype(BF16), b_ref[...], _TN, preferred_element_type=F32)
        i = pl.program_id(1)

        @pl.when(i == 0)
        def _():
            acc_ref[...] = part

        @pl.when(i > 0)
        def _():
            acc_ref[...] += part

        @pl.when(i == pl.num_programs(1) - 1)
        def _():
            o_ref[0] = acc_ref[...].astype(BF16)

    return _pallas(
        body, name=name, grid=(nblk, m // tm),
        out_shape=jax.ShapeDtypeStruct((nblk, bw, nn), BF16),
        in_specs=[pl.BlockSpec((tm, bw), lambda j, i: (i, j)), pl.BlockSpec((tm, nn), lambda j, i: (i, 0)), _ANY],
        out_specs=pl.BlockSpec((1, bw, nn), lambda j, i: (j, 0, 0)),
        scratch_shapes=[pltpu.VMEM((bw, nn), F32)],
        compiler_params=_params(_mb(56)),
    )(a, b, after)


def _outproj_bwd(da1, wout, after):
    n, d = da1.shape
    tm = 512

    def body(a_ref, w_ref, after_ref, o_ref):
        o_ref[...] = lax.dot_general(a_ref[...], w_ref[...], _NT, preferred_element_type=F32)

    return _pallas(
        body, name="outproj_bwd", grid=(n // tm,),
        out_shape=jax.ShapeDtypeStruct((n, d), F32),
        in_specs=[pl.BlockSpec((tm, d), lambda i: (i, 0)), pl.BlockSpec((d, d), lambda i: (0, 0)), _ANY],
        out_specs=pl.BlockSpec((tm, d), lambda i: (i, 0)),
        compiler_params=_params(_mb(48)),
    )(da1, wout, after)


def _qkv_bwd(dh, wint, x, ct, dr1, sc):
    na, wcols = dh.shape
    n, d = x.shape
    tm = CTX
    nlat = n // tm

    def body(dh_ref, w_ref, x_ref, ct_ref, dr_ref, sc_ref, gx_ref, acc_ref):
        i = pl.program_id(0)

        @pl.when(i == 0)
        def _():
            acc_ref[...] = jnp.zeros_like(acc_ref)

        du = jnp.dot(dh_ref[...], w_ref[...], preferred_element_type=F32)

        @pl.when(i < nlat)
        def _():
            gx_ref[...] = ALPHA * dr_ref[...] + du * (1.0 + sc_ref[0])
            acc_ref[0:1, :] += _colsum(du)
            acc_ref[1:2, :] += _colsum(du * x_ref[...])

        @pl.when(i == nlat)
        def _():
            acc_ref[2:3, :] += _colsum(du)
            acc_ref[3:4, :] += _colsum(du * ct_ref[...])

    lat = lambda i: (jnp.minimum(i, nlat - 1), 0)
    const2 = lambda i: (0, 0)
    return _pallas(
        body, name="qkv_bwd", grid=(nlat + 1,),
        out_shape=(jax.ShapeDtypeStruct((n, d), F32), jax.ShapeDtypeStruct((8, d), F32)),
        in_specs=[pl.BlockSpec((tm, wcols), lambda i: (i, 0)), pl.BlockSpec((wcols, d), const2),
                  pl.BlockSpec((tm, d), lat), pl.BlockSpec((tm, d), const2), pl.BlockSpec((tm, d), lat),
                  pl.BlockSpec((1, 1, d), lambda i: (0, 0, 0))],
        out_specs=(pl.BlockSpec((tm, d), lat), pl.BlockSpec((8, d), const2)),
        compiler_params=_params(_mb(56)),
    )(dh, wint, x, ct, dr1, sc)


def _adam_math(w, g, m, v):
    m2 = ADAM_B1 * m + (1.0 - ADAM_B1) * g
    v2 = ADAM_B2 * v + (1.0 - ADAM_B2) * (g * g)
    m_hat = m2 * (1.0 / (1.0 - ADAM_B1 ** ADAM_STEP))
    v_hat = v2 * (1.0 / (1.0 - ADAM_B2 ** ADAM_STEP))
    delta = -ADAM_LR * (m_hat / (jnp.sqrt(v_hat) + ADAM_EPS) + ADAM_WD * w)
    return delta, m2, v2


def _adamw(w, gsrc, m, v, name, after=None):
    r, c = w.shape
    parts = gsrc.ndim == 3
    after = w if after is None else after
    tr = r
    while tr * c * 4 > _mb(1) and tr % 32 == 0:
        tr //= 2

    def body(w_ref, g_ref, m_ref, v_ref, after_ref, go_ref, d_ref, mo_ref, vo_ref):
        if parts:
            g = g_ref[0].astype(F32)
            for s in range(1, NDEV):
                g = g + g_ref[s].astype(F32)
        else:
            g = g_ref[...]
        delta, m2, v2 = _adam_math(w_ref[...], g, m_ref[...], v_ref[...])
        go_ref[...] = g
        d_ref[...] = delta
        mo_ref[...] = m2
        vo_ref[...] = v2

    tile = pl.BlockSpec((tr, c), lambda i: (i, 0))
    gspec = pl.BlockSpec((NDEV, tr, c), lambda i: (0, i, 0)) if parts else tile
    sds = jax.ShapeDtypeStruct((r, c), F32)
    return _pallas(
        body, name=name, grid=(r // tr,),
        out_shape=(sds, sds, sds, sds),
        in_specs=[tile, gspec, tile, tile, _ANY],
        out_specs=(tile, tile, tile, tile),
        compiler_params=_params(_mb(48)),
    )(w, gsrc, m, v, after)


def _adamw_t(w, gsrc_t, m, v, name):
    r, c = w.shape
    tr = 256

    def body(w_ref, g_ref, m_ref, v_ref, go_ref, d_ref, mo_ref, vo_ref):
        gt = g_ref[0].astype(F32)
        for s in range(1, NDEV):
            gt = gt + g_ref[s].astype(F32)
        g = gt.T
        delta, m2, v2 = _adam_math(w_ref[...], g, m_ref[...], v_ref[...])
        go_ref[...] = g
        d_ref[...] = delta
        mo_ref[...] = m2
        vo_ref[...] = v2

    tile = pl.BlockSpec((tr, c), lambda i: (i, 0))
    sds = jax.ShapeDtypeStruct((r, c), F32)
    return _pallas(
        body, name=name, grid=(r // tr,),
        out_shape=(sds, sds, sds, sds),
        in_specs=[tile, pl.BlockSpec((NDEV, c, tr), lambda i: (0, 0, i)), tile, tile],
        out_specs=(tile, tile, tile, tile),
        compiler_params=_params(_mb(48)),
    )(w, gsrc_t, m, v)


def _small_update(gath, dcc, cc, w_s, m_s, v_s):
    d = w_s.shape[1]

    def body(g_ref, dcc_ref, cc_ref, w_ref, m_ref, v_ref, go_ref, d_ref, mo_ref, vo_ref):
        s = g_ref[0]
        for b in range(1, NDEV):
            s = s + g_ref[b]
        dsl = dcc_ref[0, 8:9, :]
        for b in range(1, NDEV):
            dsl = dsl + dcc_ref[b, 8:9, :]
        cv = cc_ref[...]
        sg = _sigmoid(cv)
        go_ref[...] = jnp.zeros_like(go_ref)
        go_ref[0:1, :] = dsl * (sg * (1.0 + cv * (1.0 - sg)))
        go_ref[1:3, :] = s[0:2] + s[6:8]
        go_ref[3:7, :] = s[2:6]
        go_ref[7:12, :] = s[8:13]
        delta, m2, v2 = _adam_math(w_ref[...], go_ref[...], m_ref[...], v_ref[...])
        d_ref[...] = delta
        mo_ref[...] = m2
        vo_ref[...] = v2

    full = pl.BlockSpec((16, d), lambda: (0, 0))
    g3 = pl.BlockSpec((NDEV, 16, d), lambda: (0, 0, 0))
    sds = jax.ShapeDtypeStruct((16, d), F32)
    return _pallas(
        body, name="small_update",
        out_shape=(sds, sds, sds, sds),
        in_specs=[g3, g3, pl.BlockSpec((1, d), lambda: (0, 0)), full, full, full],
        out_specs=(full, full, full, full),
        compiler_params=_params(_mb(24)),
    )(gath, dcc, cc, w_s, m_s, v_s)


def _rope_tables(n):
    rows = n // GRID_W
    row_ids = jnp.repeat(jnp.arange(rows, dtype=F32), GRID_W)
    col_ids = jnp.tile(jnp.arange(GRID_W, dtype=F32), rows)
    axis_dim = HEAD // 2
    inv_freq = jnp.power(ROPE_THETA, -jnp.arange(0, axis_dim, 2, dtype=F32) / axis_dim)
    ang_r = row_ids[:, None] * inv_freq
    ang_c = col_ids[:, None] * inv_freq
    ang = jnp.concatenate([ang_r, ang_r, ang_c, ang_c], axis=-1)
    cos, sin = jnp.cos(ang), jnp.sin(ang)
    first = (jnp.arange(HEAD) % (HEAD // 2)) < HEAD // 4
    sa = jnp.where(first, -sin, 0.0)
    sb = jnp.where(first, 0.0, sin)
    ones = jnp.ones((CTX, HEAD), F32)
    zeros = jnp.zeros((CTX, HEAD), F32)
    return (jnp.concatenate([cos, ones], 0), jnp.concatenate([sa, zeros], 0), jnp.concatenate([sb, zeros], 0))


def _pad_cols(a, width):
    return jnp.pad(a, ((0, 0), (0, width - a.shape[1])))


def _pad_rows(a, rows):
    return jnp.pad(a, ((0, rows - a.shape[0]), (0, 0)))


def _pack_small(c_ctx, b_ada, ln1_g, ln1_b, ln2_g, ln2_b, qg, kg, sink, d):
    misc = _pad_cols(jnp.concatenate([qg, kg, sink], axis=1), d)
    rows = jnp.concatenate([c_ctx.reshape(1, d), b_ada.reshape(6, d), ln1_g, ln1_b, ln2_g, ln2_b, misc], axis=0)
    return _pad_rows(rows, 16)


def _unpack_small(p, d):
    return dict(c_ctx=p[0], b_ada=p[1:7].reshape(1, 6 * d), ln1_g=p[7:8], ln1_b=p[8:9], ln2_g=p[9:10], ln2_b=p[10:11],
                q_norm_g=p[11:12, 0:HEAD], k_norm_g=p[11:12, HEAD:2 * HEAD], sink_logit=p[11:12, 2 * HEAD:2 * HEAD + 8])


def kernel(x, c, ctx, c_ctx, w_ada, b_ada, w_in, q_norm_g, k_norm_g, sink_logit, w_out, ln1_g, ln1_b, w_gate, w_up, w_down, ln2_g, ln2_b, loss_target, m_c_ctx, m_w_ada, m_b_ada, m_w_in, m_q_norm_g, m_k_norm_g, m_sink_logit, m_w_out, m_ln1_g, m_ln1_b, m_w_gate, m_w_up, m_w_down, m_ln2_g, m_ln2_b, v_c_ctx, v_w_ada, v_b_ada, v_w_in, v_q_norm_g, v_k_norm_g, v_sink_logit, v_w_out, v_ln1_g, v_ln1_b, v_w_gate, v_w_up, v_w_down, v_ln2_g, v_ln2_b):
    xs, cts, tgt = x[0], ctx[0], loss_target[0]
    n, d = xs.shape
    assert cts.shape == (CTX, d) and w_in.shape[2] == IN_SHARD and w_gate.shape[2] == FFN_SHARD
    me = 4 * lax.axis_index("x") + 2 * lax.axis_index("y") + lax.axis_index("c")
    e_sh = w_ada.shape[2]

    c_g = _exchange(_pad_rows(c, 8), False, "gather_c")
    c_all = jnp.concatenate([c_g[:, 0, :], _pad_rows(c_ctx.reshape(1, d), 8)], axis=0)
    bias_sh = lax.dynamic_slice(b_ada, (0, me * e_sh), (1, e_sh))
    mods_g = _exchange(_ada_fwd(c_all, w_ada[0], bias_sh), False, "gather_mods")
    mods = jnp.transpose(mods_g, (1, 0, 2)).reshape(16, NDEV * e_sh)
    mine = lax.dynamic_slice(mods, (me, 0), (1, 6 * d))
    sh1, sc1, g1, sh2, sc2, g2 = [mine[:, k * d:(k + 1) * d] for k in range(6)]
    csh1, csc1 = mods[8:9, 0:d], mods[8:9, d:2 * d]
    sc_pair = jnp.stack([sc1, csc1])
    sh_pair = jnp.stack([sh1, csh1])

    h_win, tok = _exchange_start(w_in[0].T.astype(BF16), "chip", mods, "gather_w_in_start")
    tok, (wo_l, wg_l, wu_l, wd_l) = lax.optimization_barrier((tok, (w_out, w_gate, w_up, w_down)))
    h_wout, tok = _exchange_start(wo_l[0].astype(BF16), "chip", tok, "gather_w_out_start")
    h_wg, tok = _exchange_start(wg_l[0].T.astype(BF16), "chip", tok, "gather_w_gate_start")
    h_wu, tok = _exchange_start(wu_l[0].T.astype(BF16), "chip", tok, "gather_w_up_start")
    h_wd, tok = _exchange_start(wd_l[0].astype(BF16), "chip", tok, "gather_w_down_start")

    cos, sa, sb = _rope_tables(n)
    f_win, tok = _forward_start(_exchange_wait(h_win, "chip", tok, "gather_w_in_wait"), tok, "forward_w_in_start")
    win_g = _forward_wait(f_win, tok, "forward_w_in_wait").reshape(NDEV * IN_SHARD, d)
    u_all, h_all, t_all, kt_b = _qkv_fwd(xs, cts, sc_pair, sh_pair, win_g, q_norm_g, k_norm_g, cos, sa, sb)
    f_wout, tok = _forward_start(_exchange_wait(h_wout, "chip", t_all, "gather_w_out_wait"), t_all, "forward_w_out_start")
    win_bias = _window_bias()
    o_a, lse_a = _attn_window_fwd(t_all, sink_logit, win_bias, tok)
    o, p_b, linv_b = _attn_global_fwd(t_all, o_a)
    f_wg, tok = _forward_start(_exchange_wait(h_wg, "chip", o, "gather_w_gate_wait"), o, "forward_w_gate_start")
    f_wu, tok = _forward_start(_exchange_wait(h_wu, "chip", tok, "gather_w_up_wait"), tok, "forward_w_up_start")
    wout_g = _forward_wait(f_wout, tok, "forward_w_out_wait").reshape(d, d)
    a1, xh1, rs1, u2 = _outproj_ln1(o, wout_g, xs, g1, ln1_g, ln1_b, sc2, sh2, tok)
    f_wd, tok = _forward_start(_exchange_wait(h_wd, "chip", rs1, "gather_w_down_wait"), rs1, "forward_w_down_start")
    ffn_w = (NDEV * FFN_SHARD, d)
    wg_g = _forward_wait(f_wg, tok, "forward_w_gate_wait").reshape(ffn_w)
    wu_g = _forward_wait(f_wu, tok, "forward_w_up_wait").reshape(ffn_w)
    sa_f, sb_f, hf = _ffn_up(u2, wg_g, wu_g, tok)
    wd_g = _forward_wait(f_wd, hf, "forward_w_down_wait").reshape(ffn_w)
    ffn = _ffn_down(hf, wd_g)
    dr2, df, loss_p, acc2 = _ln2_loss(xh1, ffn, tgt, ln1_g, ln1_b, g2, ln2_g, ln2_b)
    loss = lax.psum(loss_p[0, 0], ("x", "y", "c"))

    parts = (NDEV, FFN_SHARD, d)
    dgp = _ffn_dhf(df, wd_g, sa_f, sb_f)
    dwd_p = _dw_rows(hf, df, NDEV // 2, FFN_PAIR, min(n, 1024), loss_p, "dw_down").reshape(parts)
    h_dwd, tok = _exchange_start(dwd_p, "scatter", loss.reshape(1, 1), "scatter_dw_down_start")
    dwg_t, dwu_t = _dw_gate_up(dgp, u2, tok)
    h_dwg, tok = _exchange_start(dwg_t.reshape(parts), "scatter", tok, "scatter_dw_gate_start")
    h_dwu, tok = _exchange_start(dwu_t.reshape(parts), "scatter", tok, "scatter_dw_up_start")
    du2 = _ffn_du2(dgp, wg_g, wu_g, tok)
    dr1, da1, acc1 = _ln1_bwd(du2, dr2, xh1, rs1, a1, ln1_g, ln1_b, sc2, g1)
    dwo_p = _dw_rows(o, da1, 2, 8 * HEAD, min(n, 1024), loss_p, "dw_out").reshape(NDEV, 2 * HEAD, d)
    h_dwo, tok = _exchange_start(dwo_p, "scatter", loss_p, "scatter_dw_out_start")
    do = _outproj_bwd(da1, wout_g, tok)
    dqa, dka, dva, dsink = _attn_window_bwd(t_all, o, do, lse_a, sink_logit, win_bias)
    dqb, dkb, dvb = _attn_global_bwd(t_all, kt_b, o, do, p_b, linv_b)
    dh_all, dnorm = _qkv_bwd_prep(dqa, dka, dva, dqb, dkb, dvb, h_all, q_norm_g, k_norm_g, cos, sa, sb)
    grad_x, acc0 = _qkv_bwd(dh_all, win_g, xs, cts, dr1, sc_pair)

    misc = _pad_cols(jnp.concatenate([dnorm[0:1], dnorm[1:2], dsink[:, 0:4, 0].reshape(1, 8)], axis=1), d)
    part = jnp.concatenate([
        acc0[0:2], acc1[4:5], acc1[1:2], acc1[0:1], acc2[2:3],
        acc0[2:4],
        acc1[2:4], acc2[0:2],
        misc, jnp.zeros((3, d), F32)], axis=0)
    gath = _exchange(part, False, "gather_small")
    dm_batch = gath[:, 0:6, :].reshape(NDEV, 6 * d)
    dm_ctx = _pad_cols(gath[:, 6:8, :].reshape(NDEV, 2 * d), 6 * d)
    dm16 = lax.dynamic_slice(jnp.concatenate([dm_batch, dm_ctx], axis=0), (0, me * e_sh), (16, e_sh))
    dw_ada, drow = _ada_bwd(dm16, c_all, w_ada[0])
    dcc = _exchange(drow, False, "gather_dcc")
    dwi_p = _dw_rows(dh_all, u_all, NDEV // 2, 2 * IN_SHARD, (n + CTX) // 2, dcc, "dw_in")
    dwi_p = dwi_p.reshape(NDEV, IN_SHARD, d)
    h_dwi, tok = _exchange_start(dwi_p, "scatter", dcc, "scatter_dw_in_start")

    w_s = _pack_small(c_ctx, b_ada, ln1_g, ln1_b, ln2_g, ln2_b, q_norm_g, k_norm_g, sink_logit, d)
    m_s = _pack_small(m_c_ctx, m_b_ada, m_ln1_g, m_ln1_b, m_ln2_g, m_ln2_b, m_q_norm_g, m_k_norm_g, m_sink_logit, d)
    v_s = _pack_small(v_c_ctx, v_b_ada, v_ln1_g, v_ln1_b, v_ln2_g, v_ln2_b, v_q_norm_g, v_k_norm_g, v_sink_logit, d)
    small = [_unpack_small(p, d) for p in _small_update(gath, dcc, c_ctx.reshape(1, d), w_s, m_s, v_s)]

    big = {}
    big["w_ada"] = _adamw(w_ada[0], dw_ada, m_w_ada[0], v_w_ada[0], "adamw_w_ada", after=tok)
    big["w_down"] = _adamw(w_down[0], _exchange_wait(h_dwd, "scatter", big["w_ada"][1], "scatter_dw_down_wait"),
                           m_w_down[0], v_w_down[0], "adamw_w_down")
    late = big["w_down"][1]
    for nm, wt, mt, vt, hd in (("w_gate", w_gate, m_w_gate, v_w_gate, h_dwg), ("w_up", w_up, m_w_up, v_w_up, h_dwu)):
        res = _adamw(wt[0].T, _exchange_wait(hd, "scatter", late, "scatter_d" + nm + "_wait"), mt[0].T, vt[0].T,
                     "adamw_" + nm)
        big[nm] = [r.T for r in res]
        late = res[1]
    big["w_out"] = _adamw(w_out[0], _exchange_wait(h_dwo, "scatter", late, "scatter_dw_out_wait"), m_w_out[0], v_w_out[0],
                          "adamw_w_out")
    big["w_in"] = _adamw_t(w_in[0], _exchange_wait(h_dwi, "scatter", big["w_out"][1], "scatter_dw_in_wait"), m_w_in[0],
                           v_w_in[0], "adamw_w_in")

    names = ["c_ctx", "w_ada", "b_ada", "w_in", "q_norm_g", "k_norm_g", "sink_logit", "w_out", "ln1_g", "ln1_b",
             "w_gate", "w_up", "w_down", "ln2_g", "ln2_b"]
    outs = [loss, grad_x[None]]
    for k in range(4):
        for nm in names:
            outs.append(big[nm][k][None] if nm in big else small[k][nm])
    return tuple(outs)
```

```python
import functools

import jax
import jax.numpy as jnp
from jax import lax
from jax.experimental import pallas as pl
from jax.experimental.pallas import tpu as pltpu

F32 = jnp.float32
BF16 = jnp.bfloat16

NDEV = 8
HEAD = 128
CTX = 256
GRID_W = 64
WINDOW = 128
ROPE_THETA = 10000.0
EPS = 1e-6
SCALE = HEAD ** -0.5
LOG2E = 1.4426950408889634
QK_LOG2 = SCALE * LOG2E
ALPHA = 2.0 ** 0.25
FFN_SHARD = 704
FFN_TILE = 512
FFN_PAIR = 2 * FFN_SHARD
IN_SHARD = 384
NEG = -1e30

ADAM_LR = 0.001
ADAM_B1 = 0.9
ADAM_B2 = 0.999
ADAM_EPS = 1e-08
ADAM_WD = 0.01
ADAM_STEP = 10

VMEM_CAP = 56 * 1024 * 1024

_KINDS = ["rope"] * 10 + ["none"] * 2 + ["qnorm"] * 8 + ["knorm"] * 2 + ["none"] * 2
NORM_HEAD0 = _KINDS.index("qnorm")
NORM_HEADS = _KINDS.count("qnorm") + _KINDS.count("knorm")

_NT = (((1,), (1,)), ((), ()))
_TN = (((0,), (0,)), ((), ()))


def _pallas(body, **kw):
    return pl.pallas_call(body, **kw)


def _params(vmem_bytes):
    return pltpu.CompilerParams(vmem_limit_bytes=int(min(VMEM_CAP, vmem_bytes)))


def _mb(n):
    return int(n * 1024 * 1024)


def _sigmoid(x):
    return 1.0 / (1.0 + jnp.exp(-x))


def _colsum(a):
    return jnp.sum(a, axis=0, keepdims=True)


def _rowmean(a):
    return jnp.mean(a, axis=-1, keepdims=True)


def _exchange(src, scatter, name, after=None):
    blk = src.shape[1:] if scatter else src.shape
    after = src if after is None else after

    def body(src_ref, after_ref, out_ref, send_sems, recv_sems, local_sem):
        x, y, c = lax.axis_index("x"), lax.axis_index("y"), lax.axis_index("c")
        me = 4 * x + 2 * y + c
        copies = []
        for t in range(1, NDEV):
            px = 1 - x if (t >> 2) & 1 else x
            py = 1 - y if (t >> 1) & 1 else y
            pc = 1 - c if t & 1 else c
            peer = 4 * px + 2 * py + pc
            cp = pltpu.make_async_remote_copy(
                src_ref=src_ref.at[peer] if scatter else src_ref,
                dst_ref=out_ref.at[me],
                send_sem=send_sems.at[t - 1],
                recv_sem=recv_sems.at[t - 1],
                device_id=(px, py, pc),
                device_id_type=pl.DeviceIdType.MESH,
            )
            cp.start()
            copies.append(cp)
        own = pltpu.make_async_copy(src_ref.at[me] if scatter else src_ref, out_ref.at[me], local_sem)
        own.start()
        for cp in copies:
            cp.wait()
        own.wait()

    return _pallas(
        body, name=name,
        out_shape=jax.ShapeDtypeStruct((NDEV,) + tuple(blk), src.dtype),
        in_specs=[pl.BlockSpec(memory_space=pl.ANY), pl.BlockSpec(memory_space=pl.ANY)],
        out_specs=pl.BlockSpec(memory_space=pl.ANY),
        scratch_shapes=[pltpu.SemaphoreType.DMA((NDEV - 1,)), pltpu.SemaphoreType.DMA((NDEV - 1,)),
                        pltpu.SemaphoreType.DMA(())],
    )(src, after)


_HBM = pl.BlockSpec(memory_space=pltpu.HBM)
_SEM = pl.BlockSpec(memory_space=pltpu.SEMAPHORE)
_ANY = pl.BlockSpec(memory_space=pl.ANY)
_EFFECT = pltpu.SideEffectType.DATAFLOW_SIDE_EFFECTING


def _exchange_copies(src_ref, land_ref, send_sems, recv_sems, mode):
    x, y, c = lax.axis_index("x"), lax.axis_index("y"), lax.axis_index("c")
    me = 4 * x + 2 * y + c
    scatter = mode == "scatter"
    copies = []
    for t in ((1, 2, 4, 6) if mode == "chip" else range(1, NDEV)):
        px = 1 - x if (t >> 2) & 1 else x
        py = 1 - y if (t >> 1) & 1 else y
        pc = 1 - c if t & 1 else c
        peer = 4 * px + 2 * py + pc
        copies.append(pltpu.make_async_remote_copy(
            src_ref=src_ref.at[peer] if scatter else src_ref,
            dst_ref=land_ref.at[me],
            send_sem=send_sems.at[t - 1],
            recv_sem=recv_sems.at[t - 1],
            device_id=(px, py, pc),
            device_id_type=pl.DeviceIdType.MESH,
        ))
    own = pltpu.make_async_copy(src_ref.at[me] if scatter else src_ref, land_ref.at[me], send_sems.at[NDEV - 1])
    return copies, own


def _forward_copies(land_ref, send_sems, recv_sems):
    x, y, c = lax.axis_index("x"), lax.axis_index("y"), lax.axis_index("c")
    copies = []
    for k, t in enumerate((2, 4, 6)):
        px = 1 - x if (t >> 2) & 1 else x
        py = 1 - y if (t >> 1) & 1 else y
        mine, theirs = 4 * px + 2 * py + c, 4 * px + 2 * py + (1 - c)
        send = pltpu.make_async_remote_copy(
            src_ref=land_ref.at[mine], dst_ref=land_ref.at[mine], send_sem=send_sems.at[k], recv_sem=recv_sems.at[k],
            device_id=(x, y, 1 - c), device_id_type=pl.DeviceIdType.MESH)
        recv = pltpu.make_async_remote_copy(
            src_ref=land_ref.at[theirs], dst_ref=land_ref.at[theirs], send_sem=send_sems.at[k], recv_sem=recv_sems.at[k],
            device_id=(x, y, 1 - c), device_id_type=pl.DeviceIdType.MESH)
        copies.append((send, recv))
    return copies


def _forward_start(land, after, name):
    def body(land_ref, after_ref, send_sems, recv_sems, land_thru, token):
        for send, _ in _forward_copies(land_ref, send_sems, recv_sems):
            send.start()
        token[...] = jnp.zeros_like(token)

    res = _pallas(
        body, name=name,
        out_shape=(pltpu.SemaphoreType.DMA((3,)), pltpu.SemaphoreType.DMA((3,)), pltpu.HBM(land.shape, land.dtype),
                   jax.ShapeDtypeStruct((8, HEAD), F32)),
        in_specs=(_HBM, _ANY), out_specs=(_SEM, _SEM, _HBM, pl.BlockSpec(memory_space=pltpu.VMEM)),
        input_output_aliases={0: 2},
        compiler_params=pltpu.CompilerParams(has_side_effects=_EFFECT),
    )(land, after)
    return res[:3], res[3]


def _forward_wait(handle, after, name):
    send_sems, recv_sems, land_thru = handle

    def body(land_ref, send_sems, recv_sems, after_ref, got_ref):
        for send, recv in _forward_copies(land_ref, send_sems, recv_sems):
            send.wait_send()
            recv.wait_recv()

    return _pallas(
        body, name=name,
        out_shape=pltpu.HBM(land_thru.shape, land_thru.dtype),
        in_specs=(_HBM, _SEM, _SEM, _ANY), out_specs=_HBM,
        input_output_aliases={0: 0},
        compiler_params=pltpu.CompilerParams(has_side_effects=_EFFECT),
    )(land_thru, send_sems, recv_sems, after)


def _exchange_start(src, mode, after, name):
    blk = src.shape[1:] if mode == "scatter" else src.shape
    land = lax.empty((NDEV,) + tuple(blk), src.dtype)

    def body(src_ref, land_ref, after_ref, send_sems, recv_sems, src_thru, land_thru, token):
        copies, own = _exchange_copies(src_ref, land_ref, send_sems, recv_sems, mode)
        for cp in copies:
            cp.start()
        own.start()
        token[...] = jnp.zeros_like(token)

    res = _pallas(
        body, name=name,
        out_shape=(pltpu.SemaphoreType.DMA((NDEV,)), pltpu.SemaphoreType.DMA((NDEV,)),
                   pltpu.HBM(src.shape, src.dtype), pltpu.HBM(land.shape, land.dtype),
                   jax.ShapeDtypeStruct((8, HEAD), F32)),
        in_specs=(_HBM, _HBM, _ANY), out_specs=(_SEM, _SEM, _HBM, _HBM, pl.BlockSpec(memory_space=pltpu.VMEM)),
        input_output_aliases={0: 2, 1: 3},
        compiler_params=pltpu.CompilerParams(has_side_effects=_EFFECT),
    )(pltpu.with_memory_space_constraint(src, pltpu.HBM), pltpu.with_memory_space_constraint(land, pltpu.HBM), after)
    return res[:4], res[4]


def _exchange_wait(handle, mode, after, name):
    send_sems, recv_sems, src_thru, land_thru = handle

    def body(src_ref, land_ref, send_sems, recv_sems, after_ref, src_dead, got_ref):
        copies, own = _exchange_copies(src_ref, land_ref, send_sems, recv_sems, mode)
        for cp in copies:
            cp.wait_send()
            cp.wait_recv()
        own.wait()

    return _pallas(
        body, name=name,
        out_shape=(pltpu.HBM(src_thru.shape, src_thru.dtype), pltpu.HBM(land_thru.shape, land_thru.dtype)),
        in_specs=(_HBM, _HBM, _SEM, _SEM, _ANY), out_specs=(_HBM, _HBM),
        input_output_aliases={0: 0, 1: 1},
        compiler_params=pltpu.CompilerParams(has_side_effects=_EFFECT),
    )(src_thru, land_thru, send_sems, recv_sems, after)[1]


def _ada_fwd(c_all, w, bias):
    r, d = c_all.shape
    e = w.shape[1]
    tn = 512

    def body(c_ref, w_ref, b_ref, o_ref):
        cv = c_ref[...]
        s = (cv * _sigmoid(cv)).astype(BF16)
        o_ref[...] = jnp.dot(s, w_ref[...].astype(BF16), preferred_element_type=F32) + b_ref[...]

    return _pallas(
        body, name="ada_fwd", grid=(e // tn,),
        out_shape=jax.ShapeDtypeStruct((r, e), F32),
        in_specs=[pl.BlockSpec((r, d), lambda j: (0, 0)), pl.BlockSpec((d, tn), lambda j: (0, j)),
                  pl.BlockSpec((1, tn), lambda j: (0, j))],
        out_specs=pl.BlockSpec((r, tn), lambda j: (0, j)),
        compiler_params=_params(_mb(24)),
    )(c_all, w, bias)


def _ada_bwd(dm16, c_all, w):
    d, e = w.shape
    tn = 512

    def body(dm_ref, c_ref, w_ref, dw_ref, dr_ref):
        j = pl.program_id(0)
        dm = dm_ref[...]
        rid = lax.broadcasted_iota(jnp.int32, dm.shape, 0)
        ctx_sum = jnp.sum(jnp.where(rid >= 8, dm, 0.0), axis=0, keepdims=True)
        rows = jnp.where(rid < 8, dm, jnp.where(rid == 8, jnp.broadcast_to(ctx_sum, dm.shape), 0.0)).astype(BF16)
        cv = c_ref[...]
        s = (cv * _sigmoid(cv)).astype(BF16)
        dw_ref[...] = lax.dot_general(s, rows, _TN, preferred_element_type=F32)
        part = lax.dot_general(rows, w_ref[...].astype(BF16), _NT, preferred_element_type=F32)

        @pl.when(j == 0)
        def _():
            dr_ref[...] = part

        @pl.when(j > 0)
        def _():
            dr_ref[...] += part

    return _pallas(
        body, name="ada_bwd", grid=(e // tn,),
        out_shape=(jax.ShapeDtypeStruct((d, e), F32), jax.ShapeDtypeStruct((16, d), F32)),
        in_specs=[pl.BlockSpec((16, tn), lambda j: (0, j)), pl.BlockSpec((16, d), lambda j: (0, 0)),
                  pl.BlockSpec((d, tn), lambda j: (0, j))],
        out_specs=(pl.BlockSpec((d, tn), lambda j: (0, j)), pl.BlockSpec((16, d), lambda j: (0, 0))),
        compiler_params=_params(_mb(32)),
    )(dm16, c_all, w)


def _rope(v, cos, sa, sb):
    return v * cos + (pltpu.roll(v, 96, 1) * sa + pltpu.roll(v, 32, 1) * sb)


def _rope_t(dt, cos, sa, sb):
    return dt * cos + (pltpu.roll(dt * sa, 32, 1) + pltpu.roll(dt * sb, 96, 1))


def _qkv_fwd(x, ct, sc, sh, wint, qg, kg, cos, sa, sb):
    n, d = x.shape
    tm = CTX
    nlat = n // tm
    na = n + CTX
    wcols = wint.shape[0]

    def body(x_ref, ct_ref, sc_ref, sh_ref, w_ref, qg_ref, kg_ref, cos_ref, sa_ref, sb_ref, u_ref, h_ref, t_ref, kt_ref):
        i = pl.program_id(0)
        xin = jnp.where(i == nlat, ct_ref[...], x_ref[...])
        u = (xin * (1.0 + sc_ref[0]) + sh_ref[0]).astype(BF16)
        u_ref[...] = u
        cos, sa, sb = cos_ref[...], sa_ref[...], sb_ref[...]
        h = lax.dot_general(u, w_ref[...], _NT, preferred_element_type=F32)
        h_ref[...] = h[:, NORM_HEAD0 * HEAD:(NORM_HEAD0 + NORM_HEADS) * HEAD]
        for hd in range(24):
            v = h[:, hd * HEAD:(hd + 1) * HEAD]
            kind = _KINDS[hd]
            if kind == "qnorm":
                v = v * lax.rsqrt(_rowmean(v * v) + EPS) * qg_ref[...]
            elif kind == "knorm":
                v = v * lax.rsqrt(_rowmean(v * v) + EPS) * kg_ref[...]
            if kind != "none":
                v = _rope(v, cos, sa, sb)
            t_ref[:, hd * HEAD:(hd + 1) * HEAD] = v.astype(BF16)
            if kind == "knorm":
                kt_ref[(hd - 20) * HEAD:(hd - 19) * HEAD, :] = v.T.astype(BF16)

    lat = lambda i: (jnp.minimum(i, nlat - 1), 0)
    row = lambda i: (i, 0)
    const2 = lambda i: (0, 0)
    return _pallas(
        body, name="qkv_fwd", grid=(nlat + 1,),
        out_shape=(jax.ShapeDtypeStruct((na, d), BF16), jax.ShapeDtypeStruct((na, NORM_HEADS * HEAD), F32),
                   jax.ShapeDtypeStruct((na, wcols), BF16), jax.ShapeDtypeStruct((2 * HEAD, na), BF16)),
        in_specs=[pl.BlockSpec((tm, d), lat), pl.BlockSpec((tm, d), const2),
                  pl.BlockSpec((1, 1, d), lambda i: (i // nlat, 0, 0)),
                  pl.BlockSpec((1, 1, d), lambda i: (i // nlat, 0, 0)),
                  pl.BlockSpec((wcols, d), const2),
                  pl.BlockSpec((1, HEAD), const2), pl.BlockSpec((1, HEAD), const2),
                  pl.BlockSpec((tm, HEAD), row), pl.BlockSpec((tm, HEAD), row), pl.BlockSpec((tm, HEAD), row)],
        out_specs=(pl.BlockSpec((tm, d), row), pl.BlockSpec((tm, NORM_HEADS * HEAD), row), pl.BlockSpec((tm, wcols), row),
                   pl.BlockSpec((2 * HEAD, tm), lambda i: (0, i))),
        compiler_params=_params(_mb(56)),
    )(x, ct, sc, sh, wint, qg, kg, cos, sa, sb)


def _qkv_bwd_prep(dqa, dka, dva, dqb, dkb, dvb, h_norm, qg, kg, cos, sa, sb):
    na = h_norm.shape[0]
    wcols = 24 * HEAD
    n = na - CTX
    tm = CTX
    nlat = n // tm

    def body(dqa_ref, dka_ref, dva_ref, dqb_ref, dkb_ref, dvb_ref, h_ref, qg_ref, kg_ref, cos_ref, sa_ref, sb_ref,
             dh_ref, dg_ref):
        i = pl.program_id(0)

        @pl.when(i == 0)
        def _():
            dg_ref[...] = jnp.zeros_like(dg_ref)

        cos, sa, sb = cos_ref[...], sa_ref[...], sb_ref[...]
        is_lat = i < nlat
        for hd in range(24):
            kind = _KINDS[hd]
            if hd < 8:
                dt = jnp.where(is_lat, dqa_ref[:, hd * HEAD:(hd + 1) * HEAD], 0.0)
            elif hd < 10:
                dt = dka_ref[:, (hd - 8) * HEAD:(hd - 7) * HEAD]
            elif hd < 12:
                dt = dva_ref[:, (hd - 10) * HEAD:(hd - 9) * HEAD]
            elif hd < 20:
                dt = jnp.where(is_lat, dqb_ref[:, (hd - 12) * HEAD:(hd - 11) * HEAD], 0.0)
            elif hd < 22:
                dt = dkb_ref[:, (hd - 20) * HEAD:(hd - 19) * HEAD]
            else:
                dt = dvb_ref[:, (hd - 22) * HEAD:(hd - 21) * HEAD]
            if kind != "none":
                dt = _rope_t(dt, cos, sa, sb)
            if kind in ("qnorm", "knorm"):
                g_ref = qg_ref if kind == "qnorm" else kg_ref
                r0 = 0 if kind == "qnorm" else 1
                xv = h_ref[:, (hd - NORM_HEAD0) * HEAD:(hd - NORM_HEAD0 + 1) * HEAD]
                xn = xv * lax.rsqrt(_rowmean(xv * xv) + EPS)
                dg_ref[r0:r0 + 1, :] += _colsum(dt * xn)
                dxn = dt * g_ref[...]
                dt = lax.rsqrt(_rowmean(xv * xv) + EPS) * (dxn - xn * _rowmean(dxn * xn))
            dh_ref[:, hd * HEAD:(hd + 1) * HEAD] = dt.astype(BF16)

    lat = lambda i: (jnp.minimum(i, nlat - 1), 0)
    row = lambda i: (i, 0)
    const2 = lambda i: (0, 0)
    return _pallas(
        body, name="qkv_bwd_prep", grid=(nlat + 1,),
        out_shape=(jax.ShapeDtypeStruct((na, wcols), BF16), jax.ShapeDtypeStruct((8, HEAD), F32)),
        in_specs=[pl.BlockSpec((tm, 8 * HEAD), lat), pl.BlockSpec((tm, 2 * HEAD), row), pl.BlockSpec((tm, 2 * HEAD), row),
                  pl.BlockSpec((tm, 8 * HEAD), lat), pl.BlockSpec((tm, 2 * HEAD), row), pl.BlockSpec((tm, 2 * HEAD), row),
                  pl.BlockSpec((tm, NORM_HEADS * HEAD), row),
                  pl.BlockSpec((1, HEAD), const2), pl.BlockSpec((1, HEAD), const2),
                  pl.BlockSpec((tm, HEAD), row), pl.BlockSpec((tm, HEAD), row), pl.BlockSpec((tm, HEAD), row)],
        out_specs=(pl.BlockSpec((tm, wcols), row), pl.BlockSpec((8, HEAD), const2)),
        compiler_params=_params(_mb(40)),
    )(dqa, dka, dva, dqb, dkb, dvb, h_norm, qg, kg, cos, sa, sb)


def _window_keys(k_ref, v_ref, n, na):
    i = pl.program_id(1)
    tq = WINDOW
    start = pl.multiple_of(jnp.clip((i - 1) * tq, 0, n - 3 * tq), tq)
    kk = jnp.concatenate([k_ref[pl.ds(start, 3 * tq), :], k_ref[n:na, :]], axis=0)
    vv = jnp.concatenate([v_ref[pl.ds(start, 3 * tq), :], v_ref[n:na, :]], axis=0)
    return kk, vv, start


def _window_bias():
    tq = WINDOW
    r = (jnp.arange(4 * tq) % tq)[:, None]
    c = jnp.arange(3 * tq + CTX)[None, :]
    variants = []
    for back in (0, tq, 2 * tq):
        seen = (jnp.abs(back + r - c) <= WINDOW) | (c >= 3 * tq)
        variants.append(jnp.where(seen, 0.0, NEG).astype(F32))
    return jnp.stack(variants)


def _window_bias_spec(nq):
    return pl.BlockSpec((1, 4 * WINDOW, 3 * WINDOW + CTX),
                        lambda kv, i: (jnp.where(i == 0, 0, jnp.where(i == nq - 1, 2, 1)), 0, 0))


def _stack_heads(ref, width=HEAD):
    return jnp.concatenate([ref[:, g * HEAD:g * HEAD + width] for g in range(4)], axis=0)


def _sink_column(sink_ref, kv, tq):
    grp = lax.broadcasted_iota(jnp.int32, (4 * tq, 1), 0) // tq
    col = jnp.zeros((4 * tq, 1), F32)
    for g in range(4):
        col = jnp.where(grp == g, sink_ref[0, 4 * kv + g] * LOG2E, col)
    return col


def _attn_window_fwd(t_all, sink, bias, after):
    na = t_all.shape[0]
    n = na - CTX
    tq = WINDOW

    def body(sink_ref, q_ref, k_ref, v_ref, bias_ref, after_ref, o_ref, lse_ref):
        kv = pl.program_id(0)
        kk, vv, _ = _window_keys(k_ref, v_ref, n, na)
        t = lax.dot_general(_stack_heads(q_ref), kk, _NT, preferred_element_type=F32) * QK_LOG2 + bias_ref[0]
        sk = _sink_column(sink_ref, kv, tq)
        m = jnp.maximum(jnp.max(t, axis=-1, keepdims=True), sk)
        p = jnp.exp2(t - m)
        l = jnp.sum(p, axis=-1, keepdims=True) + jnp.exp2(sk - m)
        o = jnp.dot(p.astype(BF16), vv, preferred_element_type=F32) * (1.0 / l)
        lse = m + jnp.log2(l)
        for g in range(4):
            o_ref[:, g * HEAD:(g + 1) * HEAD] = o[g * tq:(g + 1) * tq]
            lse_ref[:, g * HEAD:(g + 1) * HEAD] = jnp.broadcast_to(lse[g * tq:(g + 1) * tq], (tq, HEAD))

    blk = pl.BlockSpec((tq, 4 * HEAD), lambda kv, i: (i, kv))
    return _pallas(
        body, name="attn_window_fwd", grid=(2, n // tq),
        out_shape=(jax.ShapeDtypeStruct((n, 16 * HEAD), F32), jax.ShapeDtypeStruct((n, 8 * HEAD), F32)),
        in_specs=[pl.BlockSpec(memory_space=pltpu.SMEM), blk,
                  pl.BlockSpec((na, HEAD), lambda kv, i: (0, 8 + kv)),
                  pl.BlockSpec((na, HEAD), lambda kv, i: (0, 10 + kv)), _window_bias_spec(n // tq), _ANY],
        out_specs=(blk, blk),
        compiler_params=_params(_mb(32)),
    )(sink, t_all, t_all, t_all, bias, after)


def _attn_global_fwd(t_all, o_part):
    na = t_all.shape[0]
    n = na - CTX
    tq = 256

    def body(q_ref, k_ref, v_ref, o_in_ref, o_ref, p_ref, linv_ref):
        kk, vv = k_ref[...], v_ref[...]
        for g in range(4):
            q = q_ref[:, g * HEAD:(g + 1) * HEAD]
            t = lax.dot_general(q, kk, _NT, preferred_element_type=F32) * QK_LOG2
            m = jnp.max(t, axis=-1, keepdims=True)
            p = jnp.exp2(t - m)
            linv = 1.0 / jnp.sum(p, axis=-1, keepdims=True)
            pb = p.astype(BF16)
            p_ref[g] = pb
            o_ref[:, g * HEAD:(g + 1) * HEAD] = jnp.dot(pb, vv, preferred_element_type=F32) * linv
            linv_ref[:, g * HEAD:(g + 1) * HEAD] = jnp.broadcast_to(linv, (tq, HEAD))

    return _pallas(
        body, name="attn_global_fwd", grid=(2, n // tq),
        out_shape=(jax.ShapeDtypeStruct((n, 16 * HEAD), F32), jax.ShapeDtypeStruct((8, n, na), BF16),
                   jax.ShapeDtypeStruct((n, 8 * HEAD), F32)),
        in_specs=[pl.BlockSpec((tq, 4 * HEAD), lambda kv, i: (i, 3 + kv)),
                  pl.BlockSpec((na, HEAD), lambda kv, i: (0, 20 + kv)),
                  pl.BlockSpec((na, HEAD), lambda kv, i: (0, 22 + kv)), _ANY],
        out_specs=(pl.BlockSpec((tq, 4 * HEAD), lambda kv, i: (i, 2 + kv)),
                   pl.BlockSpec((4, tq, na), lambda kv, i: (kv, i, 0)),
                   pl.BlockSpec((tq, 4 * HEAD), lambda kv, i: (i, kv))),
        input_output_aliases={3: 0},
        compiler_params=_params(_mb(56)),
    )(t_all, t_all, t_all, o_part)


def _attn_window_bwd(t_all, o, do, lse, sink, bias):
    na = t_all.shape[0]
    n = na - CTX
    tq = WINDOW

    def body(sink_ref, q_ref, k_ref, v_ref, o_ref, do_ref, lse_ref, bias_ref, dq_ref, dk_ref, dv_ref, dsink_ref):
        kv = pl.program_id(0)

        @pl.when(pl.program_id(1) == 0)
        def _():
            dk_ref[...] = jnp.zeros_like(dk_ref)
            dv_ref[...] = jnp.zeros_like(dv_ref)
            dsink_ref[...] = jnp.zeros_like(dsink_ref)

        kk, vv, start = _window_keys(k_ref, v_ref, n, na)
        q = _stack_heads(q_ref)
        t = lax.dot_general(q, kk, _NT, preferred_element_type=F32) * QK_LOG2 + bias_ref[0]
        lse = _stack_heads(lse_ref, 1)
        p = jnp.exp2(t - lse)
        dof = _stack_heads(do_ref)
        delta = jnp.sum(dof * _stack_heads(o_ref), axis=-1, keepdims=True)
        dob = dof.astype(BF16)
        dv_acc = lax.dot_general(p.astype(BF16), dob, _TN, preferred_element_type=F32)
        dp = lax.dot_general(dob, vv, _NT, preferred_element_type=F32)
        ds = (p * (dp - delta) * SCALE).astype(BF16)
        dq = jnp.dot(ds, kk, preferred_element_type=F32)
        dk_acc = lax.dot_general(ds, q, _TN, preferred_element_type=F32)
        dsk = -(jnp.exp2(_sink_column(sink_ref, kv, tq) - lse) * delta)
        for g in range(4):
            dq_ref[:, g * HEAD:(g + 1) * HEAD] = dq[g * tq:(g + 1) * tq]
            dsink_ref[0, g:g + 1, :] += jnp.broadcast_to(_colsum(dsk[g * tq:(g + 1) * tq]), (1, HEAD))
        dk_ref[pl.ds(start, 3 * tq), :] += dk_acc[:3 * tq]
        dv_ref[pl.ds(start, 3 * tq), :] += dv_acc[:3 * tq]
        dk_ref[n:na, :] += dk_acc[3 * tq:]
        dv_ref[n:na, :] += dv_acc[3 * tq:]

    blk = pl.BlockSpec((tq, 4 * HEAD), lambda kv, i: (i, kv))
    kvout = pl.BlockSpec((na, HEAD), lambda kv, i: (0, kv))
    return _pallas(
        body, name="attn_window_bwd", grid=(2, n // tq),
        out_shape=(jax.ShapeDtypeStruct((n, 8 * HEAD), F32), jax.ShapeDtypeStruct((na, 2 * HEAD), F32),
                   jax.ShapeDtypeStruct((na, 2 * HEAD), F32), jax.ShapeDtypeStruct((2, 8, HEAD), F32)),
        in_specs=[pl.BlockSpec(memory_space=pltpu.SMEM), blk,
                  pl.BlockSpec((na, HEAD), lambda kv, i: (0, 8 + kv)),
                  pl.BlockSpec((na, HEAD), lambda kv, i: (0, 10 + kv)),
                  blk, blk, blk, _window_bias_spec(n // tq)],
        out_specs=(blk, kvout, kvout, pl.BlockSpec((1, 8, HEAD), lambda kv, i: (kv, 0, 0))),
        compiler_params=_params(_mb(40)),
    )(sink, t_all, t_all, t_all, o, do, lse, bias)


def _attn_global_bwd(t_all, kt, o, do, p_all, linv):
    na = t_all.shape[0]
    n = na - CTX
    tq = 256

    def body(q_ref, v_ref, kt_ref, o_ref, do_ref, p_ref, linv_ref, dq_ref, dk_ref, dv_ref, dkt_acc, dvt_acc):
        i = pl.program_id(1)

        @pl.when(i == 0)
        def _():
            dkt_acc[...] = jnp.zeros_like(dkt_acc)
            dvt_acc[...] = jnp.zeros_like(dvt_acc)

        vv, kt_v = v_ref[...], kt_ref[...]
        dkt = jnp.zeros((HEAD, na), F32)
        dvt = jnp.zeros((HEAD, na), F32)
        for g in range(4):
            q = q_ref[:, g * HEAD:(g + 1) * HEAD]
            p = p_ref[g].astype(F32) * linv_ref[:, g * HEAD:g * HEAD + 1]
            dof = do_ref[:, g * HEAD:(g + 1) * HEAD]
            delta = jnp.sum(dof * o_ref[:, g * HEAD:(g + 1) * HEAD], axis=-1, keepdims=True)
            dob = dof.astype(BF16)
            dvt = dvt + lax.dot_general(dob, p.astype(BF16), _TN, preferred_element_type=F32)
            dp = lax.dot_general(dob, vv, _NT, preferred_element_type=F32)
            ds = (p * (dp - delta) * SCALE).astype(BF16)
            dq_ref[:, g * HEAD:(g + 1) * HEAD] = lax.dot_general(kt_v, ds, _NT, preferred_element_type=F32).T
            dkt = dkt + lax.dot_general(q, ds, _TN, preferred_element_type=F32)
        dkt_acc[...] += dkt
        dvt_acc[...] += dvt

        @pl.when(i == pl.num_programs(1) - 1)
        def _():
            dk_ref[...] = dkt_acc[...].T
            dv_ref[...] = dvt_acc[...].T

    ospec = pl.BlockSpec((tq, 4 * HEAD), lambda kv, i: (i, 2 + kv))
    lspec = pl.BlockSpec((tq, 4 * HEAD), lambda kv, i: (i, kv))
    kvout = pl.BlockSpec((na, HEAD), lambda kv, i: (0, kv))
    return _pallas(
        body, name="attn_global_bwd", grid=(2, n // tq),
        out_shape=(jax.ShapeDtypeStruct((n, 8 * HEAD), F32), jax.ShapeDtypeStruct((na, 2 * HEAD), F32),
                   jax.ShapeDtypeStruct((na, 2 * HEAD), F32)),
        in_specs=[pl.BlockSpec((tq, 4 * HEAD), lambda kv, i: (i, 3 + kv)),
                  pl.BlockSpec((na, HEAD), lambda kv, i: (0, 22 + kv)),
                  pl.BlockSpec((HEAD, na), lambda kv, i: (kv, 0)),
                  ospec, ospec, pl.BlockSpec((4, tq, na), lambda kv, i: (kv, i, 0)), lspec],
        out_specs=(lspec, kvout, kvout),
        scratch_shapes=[pltpu.VMEM((HEAD, na), F32), pltpu.VMEM((HEAD, na), F32)],
        compiler_params=_params(_mb(56)),
    )(t_all, t_all, kt, o, do, p_all, linv)


def _outproj_ln1(o, wout, x, g1, lg, lb, sc2, sh2, after):
    n, d = x.shape
    tm = 256

    def body(o_ref, w_ref, x_ref, g1_ref, lg_ref, lb_ref, sc_ref, sh_ref, after_ref, a_ref, xh_ref, rs_ref, u_ref):
        a1 = jnp.dot(o_ref[...].astype(BF16), w_ref[...], preferred_element_type=F32)
        a_ref[...] = a1.astype(BF16)
        r = ALPHA * x_ref[...] + g1_ref[...] * a1
        dlt = r - _rowmean(r)
        rstd = lax.rsqrt(_rowmean(dlt * dlt) + EPS)
        xh = dlt * rstd
        xh_ref[...] = xh
        rs_ref[...] = rstd
        x1 = xh * lg_ref[...] + lb_ref[...]
        u_ref[...] = (x1 * (1.0 + sc_ref[...]) + sh_ref[...]).astype(BF16)

    row = lambda i: (i, 0)
    const2 = lambda i: (0, 0)
    vec = pl.BlockSpec((1, d), const2)
    big = pl.BlockSpec((tm, d), row)
    return _pallas(
        body, name="outproj_ln1", grid=(n // tm,),
        out_shape=(jax.ShapeDtypeStruct((n, d), BF16), jax.ShapeDtypeStruct((n, d), F32),
                   jax.ShapeDtypeStruct((n, 1), F32), jax.ShapeDtypeStruct((n, d), BF16)),
        in_specs=[big, pl.BlockSpec((d, d), const2), big, vec, vec, vec, vec, vec, _ANY],
        out_specs=(big, big, pl.BlockSpec((tm, 1), row), big),
        compiler_params=_params(_mb(56)),
    )(o, wout, x, g1, lg, lb, sc2, sh2, after)


def _ffn_up(u2, wgt, wut, after):
    n, d = u2.shape
    f = wgt.shape[0]
    tm = min(1024, n)

    def body(u_ref, wg_ref, wu_ref, after_ref, sa_ref, sb_ref, hf_ref):
        u = u_ref[...]
        gv = lax.dot_general(u, wg_ref[...], _NT, preferred_element_type=F32)
        pv = lax.dot_general(u, wu_ref[...], _NT, preferred_element_type=F32)
        sg = _sigmoid(gv)
        silu = gv * sg
        sa_ref[...] = silu.astype(BF16)
        sb_ref[...] = (pv * (sg * (1.0 + gv * (1.0 - sg)))).astype(BF16)
        hf_ref[...] = (silu * pv).astype(BF16)

    tile = pl.BlockSpec((tm, FFN_TILE), lambda i, j: (i, j))
    wspec = pl.BlockSpec((FFN_TILE, d), lambda i, j: (j, 0))
    sds = jax.ShapeDtypeStruct((n, f), BF16)
    return _pallas(
        body, name="ffn_up", grid=(n // tm, f // FFN_TILE),
        out_shape=(sds, sds, sds),
        in_specs=[pl.BlockSpec((tm, d), lambda i, j: (i, 0)), wspec, wspec, _ANY],
        out_specs=(tile, tile, tile),
        compiler_params=_params(_mb(48)),
    )(u2, wgt, wut, after)


def _ffn_down(hf, wd):
    n, f = hf.shape
    d = wd.shape[1]
    tm, tn = min(1024, n), 512

    def body(h_ref, w_ref, o_ref):
        o_ref[...] = jnp.dot(h_ref[...], w_ref[...], preferred_element_type=F32)

    return _pallas(
        body, name="ffn_down", grid=(n // tm, d // tn),
        out_shape=jax.ShapeDtypeStruct((n, d), F32),
        in_specs=[pl.BlockSpec((tm, f), lambda i, j: (i, 0)), pl.BlockSpec((f, tn), lambda i, j: (0, j))],
        out_specs=pl.BlockSpec((tm, tn), lambda i, j: (i, j)),
        compiler_params=_params(_mb(56)),
    )(hf, wd)


def _ln2_loss(xh1, ffn, tgt, lg1, lb1, g2, lg2, lb2):
    n, d = xh1.shape
    tm = 256

    def body(xh_ref, f_ref, t_ref, lg1_ref, lb1_ref, g2_ref, lg2_ref, lb2_ref, dr_ref, df_ref, loss_ref, acc_ref):
        @pl.when(pl.program_id(0) == 0)
        def _():
            loss_ref[...] = jnp.zeros_like(loss_ref)
            acc_ref[...] = jnp.zeros_like(acc_ref)

        x1 = xh_ref[...] * lg1_ref[...] + lb1_ref[...]
        fv = f_ref[...]
        r = ALPHA * x1 + g2_ref[...] * fv
        dlt = r - _rowmean(r)
        rstd = lax.rsqrt(_rowmean(dlt * dlt) + EPS)
        xh2 = dlt * rstd
        err = xh2 * lg2_ref[...] + lb2_ref[...] - t_ref[...]
        loss_ref[...] += 0.5 * jnp.sum(_rowmean(err * err))
        dy = err * (1.0 / d)
        dyg = dy * lg2_ref[...]
        dr = rstd * (dyg - _rowmean(dyg) - xh2 * _rowmean(dyg * xh2))
        dr_ref[...] = dr
        df_ref[...] = (g2_ref[...] * dr).astype(BF16)
        acc_ref[0:1, :] += _colsum(dy * xh2)
        acc_ref[1:2, :] += _colsum(dy)
        acc_ref[2:3, :] += _colsum(dr * fv)

    row = lambda i: (i, 0)
    const2 = lambda i: (0, 0)
    vec = pl.BlockSpec((1, d), const2)
    big = pl.BlockSpec((tm, d), row)
    return _pallas(
        body, name="ln2_loss", grid=(n // tm,),
        out_shape=(jax.ShapeDtypeStruct((n, d), F32), jax.ShapeDtypeStruct((n, d), BF16),
                   jax.ShapeDtypeStruct((8, HEAD), F32), jax.ShapeDtypeStruct((8, d), F32)),
        in_specs=[big, big, big, vec, vec, vec, vec, vec],
        out_specs=(big, big, pl.BlockSpec((8, HEAD), const2), pl.BlockSpec((8, d), const2)),
        compiler_params=_params(_mb(48)),
    )(xh1, ffn, tgt, lg1, lb1, g2, lg2, lb2)


def _ffn_dhf(df, wd, sa, sb):
    n, d = df.shape
    f = sa.shape[1]
    tm = min(2048, n)

    def body(df_ref, w_ref, sa_ref, sb_ref, dgp_ref):
        dhf = lax.dot_general(df_ref[...], w_ref[...], _NT, preferred_element_type=F32)
        dgp_ref[:, :FFN_TILE] = (dhf * sb_ref[...].astype(F32)).astype(BF16)
        dgp_ref[:, FFN_TILE:] = (dhf * sa_ref[...].astype(F32)).astype(BF16)

    tile = pl.BlockSpec((tm, FFN_TILE), lambda i, j: (i, j))
    return _pallas(
        body, name="ffn_dhf", grid=(n // tm, f // FFN_TILE),
        out_shape=jax.ShapeDtypeStruct((n, 2 * f), BF16),
        in_specs=[pl.BlockSpec((tm, d), lambda i, j: (i, 0)), pl.BlockSpec((FFN_TILE, d), lambda i, j: (j, 0)),
                  tile, tile],
        out_specs=pl.BlockSpec((tm, 2 * FFN_TILE), lambda i, j: (i, j)),
        compiler_params=_params(_mb(48)),
    )(df, wd, sa, sb)


def _ffn_du2(dgp, wgt, wut, after):
    n = dgp.shape[0]
    f, d = wgt.shape
    tm = min(1024, n)

    def body(dgp_ref, wg_ref, wu_ref, after_ref, o_ref):
        w = jnp.concatenate([wg_ref[...], wu_ref[...]], axis=0)
        part = jnp.dot(dgp_ref[...], w, preferred_element_type=F32)

        @pl.when(pl.program_id(1) == 0)
        def _():
            o_ref[...] = part

        @pl.when(pl.program_id(1) > 0)
        def _():
            o_ref[...] += part

    wspec = pl.BlockSpec((FFN_TILE, d), lambda i, j: (j, 0))
    return _pallas(
        body, name="ffn_du2", grid=(n // tm, f // FFN_TILE),
        out_shape=jax.ShapeDtypeStruct((n, d), F32),
        in_specs=[pl.BlockSpec((tm, 2 * FFN_TILE), lambda i, j: (i, j)), wspec, wspec, _ANY],
        out_specs=pl.BlockSpec((tm, d), lambda i, j: (i, 0)),
        compiler_params=_params(_mb(48)),
    )(dgp, wgt, wut, after)


def _dw_gate_up(dgp, u2, after):
    n, d = u2.shape
    f = dgp.shape[1] // 2
    tm = min(1024, n)

    def body(a_ref, b_ref, after_ref, og_ref, ou_ref, acc_ref):
        part = lax.dot_general(a_ref[...], b_ref[...], _TN, preferred_element_type=F32)
        i = pl.program_id(1)

        @pl.when(i == 0)
        def _():
            acc_ref[...] = part

        @pl.when(i > 0)
        def _():
            acc_ref[...] += part

        @pl.when(i == pl.num_programs(1) - 1)
        def _():
            og_ref[...] = acc_ref[:FFN_TILE].astype(BF16)
            ou_ref[...] = acc_ref[FFN_TILE:].astype(BF16)

    out = pl.BlockSpec((FFN_TILE, d), lambda j, i: (j, 0))
    sds = jax.ShapeDtypeStruct((f, d), BF16)
    return _pallas(
        body, name="dw_gate_up", grid=(f // FFN_TILE, n // tm),
        out_shape=(sds, sds),
        in_specs=[pl.BlockSpec((tm, 2 * FFN_TILE), lambda j, i: (i, j)), pl.BlockSpec((tm, d), lambda j, i: (i, 0)), _ANY],
        out_specs=(out, out),
        scratch_shapes=[pltpu.VMEM((2 * FFN_TILE, d), F32)],
        compiler_params=_params(_mb(56)),
    )(dgp, u2, after)


def _ln1_bwd(du2, dr2, xh1, rs1, a1, lg1, lb1, sc2, g1):
    n, d = du2.shape
    tm = 256

    def body(du_ref, dr2_ref, xh_ref, rs_ref, a_ref, lg_ref, lb_ref, sc_ref, g1_ref, dr1_ref, da_ref, acc_ref):
        @pl.when(pl.program_id(0) == 0)
        def _():
            acc_ref[...] = jnp.zeros_like(acc_ref)

        du = du_ref[...]
        xh = xh_ref[...]
        x1 = xh * lg_ref[...] + lb_ref[...]
        dx1 = ALPHA * dr2_ref[...] + du * (1.0 + sc_ref[...])
        dxg = dx1 * lg_ref[...]
        dr1 = rs_ref[...] * (dxg - _rowmean(dxg) - xh * _rowmean(dxg * xh))
        dr1_ref[...] = dr1
        da_ref[...] = (g1_ref[...] * dr1).astype(BF16)
        acc_ref[0:1, :] += _colsum(du * x1)
        acc_ref[1:2, :] += _colsum(du)
        acc_ref[2:3, :] += _colsum(dx1 * xh)
        acc_ref[3:4, :] += _colsum(dx1)
        acc_ref[4:5, :] += _colsum(dr1 * a_ref[...].astype(F32))

    row = lambda i: (i, 0)
    const2 = lambda i: (0, 0)
    vec = pl.BlockSpec((1, d), const2)
    big = pl.BlockSpec((tm, d), row)
    return _pallas(
        body, name="ln1_bwd", grid=(n // tm,),
        out_shape=(jax.ShapeDtypeStruct((n, d), F32), jax.ShapeDtypeStruct((n, d), BF16),
                   jax.ShapeDtypeStruct((8, d), F32)),
        in_specs=[big, big, big, pl.BlockSpec((tm, 1), row), big, vec, vec, vec, vec],
        out_specs=(big, big, pl.BlockSpec((8, d), const2)),
        compiler_params=_params(_mb(48)),
    )(du2, dr2, xh1, rs1, a1, lg1, lb1, sc2, g1)


def _dw_rows(a, b, nblk, bw, tm, after, name):
    m = a.shape[0]
    nn = b.shape[1]

    def body(a_ref, b_ref, after_ref, o_ref, acc_ref):
        part = lax.dot_general(a_ref[...].astype(BF16), b_ref[...], _TN, preferred_element_type=F32)
        i = pl.program_id(1)

        @pl.when(i == 0)
        def _():
            acc_ref[...] = part

        @pl.when(i > 0)
        def _():
            acc_ref[...] += part

        @pl.when(i == pl.num_programs(1) - 1)
        def _():
            o_ref[0] = acc_ref[...].astype(BF16)

    return _pallas(
        body, name=name, grid=(nblk, m // tm),
        out_shape=jax.ShapeDtypeStruct((nblk, bw, nn), BF16),
        in_specs=[pl.BlockSpec((tm, bw), lambda j, i: (i, j)), pl.BlockSpec((tm, nn), lambda j, i: (i, 0)), _ANY],
        out_specs=pl.BlockSpec((1, bw, nn), lambda j, i: (j, 0, 0)),
        scratch_shapes=[pltpu.VMEM((bw, nn), F32)],
        compiler_params=_params(_mb(56)),
    )(a, b, after)


def _outproj_bwd(da1, wout, after):
    n, d = da1.shape
    tm = 512

    def body(a_ref, w_ref, after_ref, o_ref):
        o_ref[...] = lax.dot_general(a_ref[...], w_ref[...], _NT, preferred_element_type=F32)

    return _pallas(
        body, name="outproj_bwd", grid=(n // tm,),
        out_shape=jax.ShapeDtypeStruct((n, d), F32),
        in_specs=[pl.BlockSpec((tm, d), lambda i: (i, 0)), pl.BlockSpec((d, d), lambda i: (0, 0)), _ANY],
        out_specs=pl.BlockSpec((tm, d), lambda i: (i, 0)),
        compiler_params=_params(_mb(48)),
    )(da1, wout, after)


def _qkv_bwd(dh, wint, x, ct, dr1, sc):
    na, wcols = dh.shape
    n, d = x.shape
    tm = CTX
    nlat = n // tm

    def body(dh_ref, w_ref, x_ref, ct_ref, dr_ref, sc_ref, gx_ref, acc_ref):
        i = pl.program_id(0)

        @pl.when(i == 0)
        def _():
            acc_ref[...] = jnp.zeros_like(acc_ref)

        du = jnp.dot(dh_ref[...], w_ref[...], preferred_element_type=F32)

        @pl.when(i < nlat)
        def _():
            gx_ref[...] = ALPHA * dr_ref[...] + du * (1.0 + sc_ref[0])
            acc_ref[0:1, :] += _colsum(du)
            acc_ref[1:2, :] += _colsum(du * x_ref[...])

        @pl.when(i == nlat)
        def _():
            acc_ref[2:3, :] += _colsum(du)
            acc_ref[3:4, :] += _colsum(du * ct_ref[...])

    lat = lambda i: (jnp.minimum(i, nlat - 1), 0)
    const2 = lambda i: (0, 0)
    return _pallas(
        body, name="qkv_bwd", grid=(nlat + 1,),
        out_shape=(jax.ShapeDtypeStruct((n, d), F32), jax.ShapeDtypeStruct((8, d), F32)),
        in_specs=[pl.BlockSpec((tm, wcols), lambda i: (i, 0)), pl.BlockSpec((wcols, d), const2),
                  pl.BlockSpec((tm, d), lat), pl.BlockSpec((tm, d), const2), pl.BlockSpec((tm, d), lat),
                  pl.BlockSpec((1, 1, d), lambda i: (0, 0, 0))],
        out_specs=(pl.BlockSpec((tm, d), lat), pl.BlockSpec((8, d), const2)),
        compiler_params=_params(_mb(56)),
    )(dh, wint, x, ct, dr1, sc)


def _adam_math(w, g, m, v):
    m2 = ADAM_B1 * m + (1.0 - ADAM_B1) * g
    v2 = ADAM_B2 * v + (1.0 - ADAM_B2) * (g * g)
    m_hat = m2 * (1.0 / (1.0 - ADAM_B1 ** ADAM_STEP))
    v_hat = v2 * (1.0 / (1.0 - ADAM_B2 ** ADAM_STEP))
    delta = -ADAM_LR * (m_hat / (jnp.sqrt(v_hat) + ADAM_EPS) + ADAM_WD * w)
    return delta, m2, v2


def _adamw(w, gsrc, m, v, name, after=None):
    r, c = w.shape
    parts = gsrc.ndim == 3
    after = w if after is None else after
    tr = r
    while tr * c * 4 > _mb(1) and tr % 32 == 0:
        tr //= 2

    def body(w_ref, g_ref, m_ref, v_ref, after_ref, go_ref, d_ref, mo_ref, vo_ref):
        if parts:
            g = g_ref[0].astype(F32)
            for s in range(1, NDEV):
                g = g + g_ref[s].astype(F32)
        else:
            g = g_ref[...]
        delta, m2, v2 = _adam_math(w_ref[...], g, m_ref[...], v_ref[...])
        go_ref[...] = g
        d_ref[...] = delta
        mo_ref[...] = m2
        vo_ref[...] = v2

    tile = pl.BlockSpec((tr, c), lambda i: (i, 0))
    gspec = pl.BlockSpec((NDEV, tr, c), lambda i: (0, i, 0)) if parts else tile
    sds = jax.ShapeDtypeStruct((r, c), F32)
    return _pallas(
        body, name=name, grid=(r // tr,),
        out_shape=(sds, sds, sds, sds),
        in_specs=[tile, gspec, tile, tile, _ANY],
        out_specs=(tile, tile, tile, tile),
        compiler_params=_params(_mb(48)),
    )(w, gsrc, m, v, after)


def _adamw_t(w, gsrc_t, m, v, name):
    r, c = w.shape
    tr = 256

    def body(w_ref, g_ref, m_ref, v_ref, go_ref, d_ref, mo_ref, vo_ref):
        gt = g_ref[0].astype(F32)
        for s in range(1, NDEV):
            gt = gt + g_ref[s].astype(F32)
        g = gt.T
        delta, m2, v2 = _adam_math(w_ref[...], g, m_ref[...], v_ref[...])
        go_ref[...] = g
        d_ref[...] = delta
        mo_ref[...] = m2
        vo_ref[...] = v2

    tile = pl.BlockSpec((tr, c), lambda i: (i, 0))
    sds = jax.ShapeDtypeStruct((r, c), F32)
    return _pallas(
        body, name=name, grid=(r // tr,),
        out_shape=(sds, sds, sds, sds),
        in_specs=[tile, pl.BlockSpec((NDEV, c, tr), lambda i: (0, 0, i)), tile, tile],
        out_specs=(tile, tile, tile, tile),
        compiler_params=_params(_mb(48)),
    )(w, gsrc_t, m, v)


def _small_update(gath, dcc, cc, w_s, m_s, v_s):
    d = w_s.shape[1]

    def body(g_ref, dcc_ref, cc_ref, w_ref, m_ref, v_ref, go_ref, d_ref, mo_ref, vo_ref):
        s = g_ref[0]
        for b in range(1, NDEV):
            s = s + g_ref[b]
        dsl = dcc_ref[0, 8:9, :]
        for b in range(1, NDEV):
            dsl = dsl + dcc_ref[b, 8:9, :]
        cv = cc_ref[...]
        sg = _sigmoid(cv)
        go_ref[...] = jnp.zeros_like(go_ref)
        go_ref[0:1, :] = dsl * (sg * (1.0 + cv * (1.0 - sg)))
        go_ref[1:3, :] = s[0:2] + s[6:8]
        go_ref[3:7, :] = s[2:6]
        go_ref[7:12, :] = s[8:13]
        delta, m2, v2 = _adam_math(w_ref[...], go_ref[...], m_ref[...], v_ref[...])
        d_ref[...] = delta
        mo_ref[...] = m2
        vo_ref[...] = v2

    full = pl.BlockSpec((16, d), lambda: (0, 0))
    g3 = pl.BlockSpec((NDEV, 16, d), lambda: (0, 0, 0))
    sds = jax.ShapeDtypeStruct((16, d), F32)
    return _pallas(
        body, name="small_update",
        out_shape=(sds, sds, sds, sds),
        in_specs=[g3, g3, pl.BlockSpec((1, d), lambda: (0, 0)), full, full, full],
        out_specs=(full, full, full, full),
        compiler_params=_params(_mb(24)),
    )(gath, dcc, cc, w_s, m_s, v_s)


def _rope_tables(n):
    rows = n // GRID_W
    row_ids = jnp.repeat(jnp.arange(rows, dtype=F32), GRID_W)
    col_ids = jnp.tile(jnp.arange(GRID_W, dtype=F32), rows)
    axis_dim = HEAD // 2
    inv_freq = jnp.power(ROPE_THETA, -jnp.arange(0, axis_dim, 2, dtype=F32) / axis_dim)
    ang_r = row_ids[:, None] * inv_freq
    ang_c = col_ids[:, None] * inv_freq
    ang = jnp.concatenate([ang_r, ang_r, ang_c, ang_c], axis=-1)
    cos, sin = jnp.cos(ang), jnp.sin(ang)
    first = (jnp.arange(HEAD) % (HEAD // 2)) < HEAD // 4
    sa = jnp.where(first, -sin, 0.0)
    sb = jnp.where(first, 0.0, sin)
    ones = jnp.ones((CTX, HEAD), F32)
    zeros = jnp.zeros((CTX, HEAD), F32)
    return (jnp.concatenate([cos, ones], 0), jnp.concatenate([sa, zeros], 0), jnp.concatenate([sb, zeros], 0))


def _pad_cols(a, width):
    return jnp.pad(a, ((0, 0), (0, width - a.shape[1])))


def _pad_rows(a, rows):
    return jnp.pad(a, ((0, rows - a.shape[0]), (0, 0)))


def _pack_small(c_ctx, b_ada, ln1_g, ln1_b, ln2_g, ln2_b, qg, kg, sink, d):
    misc = _pad_cols(jnp.concatenate([qg, kg, sink], axis=1), d)
    rows = jnp.concatenate([c_ctx.reshape(1, d), b_ada.reshape(6, d), ln1_g, ln1_b, ln2_g, ln2_b, misc], axis=0)
    return _pad_rows(rows, 16)


def _unpack_small(p, d):
    return dict(c_ctx=p[0], b_ada=p[1:7].reshape(1, 6 * d), ln1_g=p[7:8], ln1_b=p[8:9], ln2_g=p[9:10], ln2_b=p[10:11],
                q_norm_g=p[11:12, 0:HEAD], k_norm_g=p[11:12, HEAD:2 * HEAD], sink_logit=p[11:12, 2 * HEAD:2 * HEAD + 8])


def kernel(x, c, ctx, c_ctx, w_ada, b_ada, w_in, q_norm_g, k_norm_g, sink_logit, w_out, ln1_g, ln1_b, w_gate, w_up, w_down, ln2_g, ln2_b, loss_target, m_c_ctx, m_w_ada, m_b_ada, m_w_in, m_q_norm_g, m_k_norm_g, m_sink_logit, m_w_out, m_ln1_g, m_ln1_b, m_w_gate, m_w_up, m_w_down, m_ln2_g, m_ln2_b, v_c_ctx, v_w_ada, v_b_ada, v_w_in, v_q_norm_g, v_k_norm_g, v_sink_logit, v_w_out, v_ln1_g, v_ln1_b, v_w_gate, v_w_up, v_w_down, v_ln2_g, v_ln2_b):
    xs, cts, tgt = x[0], ctx[0], loss_target[0]
    n, d = xs.shape
    assert cts.shape == (CTX, d) and w_in.shape[2] == IN_SHARD and w_gate.shape[2] == FFN_SHARD
    me = 4 * lax.axis_index("x") + 2 * lax.axis_index("y") + lax.axis_index("c")
    e_sh = w_ada.shape[2]

    c_g = _exchange(_pad_rows(c, 8), False, "gather_c")
    c_all = jnp.concatenate([c_g[:, 0, :], _pad_rows(c_ctx.reshape(1, d), 8)], axis=0)
    bias_sh = lax.dynamic_slice(b_ada, (0, me * e_sh), (1, e_sh))
    mods_g = _exchange(_ada_fwd(c_all, w_ada[0], bias_sh), False, "gather_mods")
    mods = jnp.transpose(mods_g, (1, 0, 2)).reshape(16, NDEV * e_sh)
    mine = lax.dynamic_slice(mods, (me, 0), (1, 6 * d))
    sh1, sc1, g1, sh2, sc2, g2 = [mine[:, k * d:(k + 1) * d] for k in range(6)]
    csh1, csc1 = mods[8:9, 0:d], mods[8:9, d:2 * d]
    sc_pair = jnp.stack([sc1, csc1])
    sh_pair = jnp.stack([sh1, csh1])

    h_win, tok = _exchange_start(w_in[0].T.astype(BF16), "chip", mods, "gather_w_in_start")
    tok, (wo_l, wg_l, wu_l, wd_l) = lax.optimization_barrier((tok, (w_out, w_gate, w_up, w_down)))
    h_wout, tok = _exchange_start(wo_l[0].astype(BF16), "chip", tok, "gather_w_out_start")
    h_wg, tok = _exchange_start(wg_l[0].T.astype(BF16), "chip", tok, "gather_w_gate_start")
    h_wu, tok = _exchange_start(wu_l[0].T.astype(BF16), "chip", tok, "gather_w_up_start")
    h_wd, tok = _exchange_start(wd_l[0].astype(BF16), "chip", tok, "gather_w_down_start")

    cos, sa, sb = _rope_tables(n)
    f_win, tok = _forward_start(_exchange_wait(h_win, "chip", tok, "gather_w_in_wait"), tok, "forward_w_in_start")
    win_g = _forward_wait(f_win, tok, "forward_w_in_wait").reshape(NDEV * IN_SHARD, d)
    u_all, h_all, t_all, kt_b = _qkv_fwd(xs, cts, sc_pair, sh_pair, win_g, q_norm_g, k_norm_g, cos, sa, sb)
    f_wout, tok = _forward_start(_exchange_wait(h_wout, "chip", t_all, "gather_w_out_wait"), t_all, "forward_w_out_start")
    win_bias = _window_bias()
    o_a, lse_a = _attn_window_fwd(t_all, sink_logit, win_bias, tok)
    o, p_b, linv_b = _attn_global_fwd(t_all, o_a)
    f_wg, tok = _forward_start(_exchange_wait(h_wg, "chip", o, "gather_w_gate_wait"), o, "forward_w_gate_start")
    f_wu, tok = _forward_start(_exchange_wait(h_wu, "chip", tok, "gather_w_up_wait"), tok, "forward_w_up_start")
    wout_g = _forward_wait(f_wout, tok, "forward_w_out_wait").reshape(d, d)
    a1, xh1, rs1, u2 = _outproj_ln1(o, wout_g, xs, g1, ln1_g, ln1_b, sc2, sh2, tok)
    f_wd, tok = _forward_start(_exchange_wait(h_wd, "chip", rs1, "gather_w_down_wait"), rs1, "forward_w_down_start")
    ffn_w = (NDEV * FFN_SHARD, d)
    wg_g = _forward_wait(f_wg, tok, "forward_w_gate_wait").reshape(ffn_w)
    wu_g = _forward_wait(f_wu, tok, "forward_w_up_wait").reshape(ffn_w)
    sa_f, sb_f, hf = _ffn_up(u2, wg_g, wu_g, tok)
    wd_g = _forward_wait(f_wd, hf, "forward_w_down_wait").reshape(ffn_w)
    ffn = _ffn_down(hf, wd_g)
    dr2, df, loss_p, acc2 = _ln2_loss(xh1, ffn, tgt, ln1_g, ln1_b, g2, ln2_g, ln2_b)
    loss = lax.psum(loss_p[0, 0], ("x", "y", "c"))

    parts = (NDEV, FFN_SHARD, d)
    dgp = _ffn_dhf(df, wd_g, sa_f, sb_f)
    dwd_p = _dw_rows(hf, df, NDEV // 2, FFN_PAIR, min(n, 1024), loss_p, "dw_down").reshape(parts)
    h_dwd, tok = _exchange_start(dwd_p, "scatter", loss.reshape(1, 1), "scatter_dw_down_start")
    dwg_t, dwu_t = _dw_gate_up(dgp, u2, tok)
    h_dwg, tok = _exchange_start(dwg_t.reshape(parts), "scatter", tok, "scatter_dw_gate_start")
    h_dwu, tok = _exchange_start(dwu_t.reshape(parts), "scatter", tok, "scatter_dw_up_start")
    du2 = _ffn_du2(dgp, wg_g, wu_g, tok)
    dr1, da1, acc1 = _ln1_bwd(du2, dr2, xh1, rs1, a1, ln1_g, ln1_b, sc2, g1)
    dwo_p = _dw_rows(o, da1, 2, 8 * HEAD, min(n, 1024), loss_p, "dw_out").reshape(NDEV, 2 * HEAD, d)
    h_dwo, tok = _exchange_start(dwo_p, "scatter", loss_p, "scatter_dw_out_start")
    do = _outproj_bwd(da1, wout_g, tok)
    dqa, dka, dva, dsink = _attn_window_bwd(t_all, o, do, lse_a, sink_logit, win_bias)
    dqb, dkb, dvb = _attn_global_bwd(t_all, kt_b, o, do, p_b, linv_b)
    dh_all, dnorm = _qkv_bwd_prep(dqa, dka, dva, dqb, dkb, dvb, h_all, q_norm_g, k_norm_g, cos, sa, sb)
    grad_x, acc0 = _qkv_bwd(dh_all, win_g, xs, cts, dr1, sc_pair)

    misc = _pad_cols(jnp.concatenate([dnorm[0:1], dnorm[1:2], dsink[:, 0:4, 0].reshape(1, 8)], axis=1), d)
    part = jnp.concatenate([
        acc0[0:2], acc1[4:5], acc1[1:2], acc1[0:1], acc2[2:3],
        acc0[2:4],
        acc1[2:4], acc2[0:2],
        misc, jnp.zeros((3, d), F32)], axis=0)
    gath = _exchange(part, False, "gather_small")
    dm_batch = gath[:, 0:6, :].reshape(NDEV, 6 * d)
    dm_ctx = _pad_cols(gath[:, 6:8, :].reshape(NDEV, 2 * d), 6 * d)
    dm16 = lax.dynamic_slice(jnp.concatenate([dm_batch, dm_ctx], axis=0), (0, me * e_sh), (16, e_sh))
    dw_ada, drow = _ada_bwd(dm16, c_all, w_ada[0])
    dcc = _exchange(drow, False, "gather_dcc")
    dwi_p = _dw_rows(dh_all, u_all, NDEV // 2, 2 * IN_SHARD, (n + CTX) // 2, dcc, "dw_in")
    dwi_p = dwi_p.reshape(NDEV, IN_SHARD, d)
    h_dwi, tok = _exchange_start(dwi_p, "scatter", dcc, "scatter_dw_in_start")

    w_s = _pack_small(c_ctx, b_ada, ln1_g, ln1_b, ln2_g, ln2_b, q_norm_g, k_norm_g, sink_logit, d)
    m_s = _pack_small(m_c_ctx, m_b_ada, m_ln1_g, m_ln1_b, m_ln2_g, m_ln2_b, m_q_norm_g, m_k_norm_g, m_sink_logit, d)
    v_s = _pack_small(v_c_ctx, v_b_ada, v_ln1_g, v_ln1_b, v_ln2_g, v_ln2_b, v_q_norm_g, v_k_norm_g, v_sink_logit, d)
    small = [_unpack_small(p, d) for p in _small_update(gath, dcc, c_ctx.reshape(1, d), w_s, m_s, v_s)]

    big = {}
    big["w_ada"] = _adamw(w_ada[0], dw_ada, m_w_ada[0], v_w_ada[0], "adamw_w_ada", after=tok)
    big["w_down"] = _adamw(w_down[0], _exchange_wait(h_dwd, "scatter", big["w_ada"][1], "scatter_dw_down_wait"),
                           m_w_down[0], v_w_down[0], "adamw_w_down")
    late = big["w_down"][1]
    for nm, wt, mt, vt, hd in (("w_gate", w_gate, m_w_gate, v_w_gate, h_dwg), ("w_up", w_up, m_w_up, v_w_up, h_dwu)):
        res = _adamw(wt[0].T, _exchange_wait(hd, "scatter", late, "scatter_d" + nm + "_wait"), mt[0].T, vt[0].T,
                     "adamw_" + nm)
        big[nm] = [r.T for r in res]
        late = res[1]
    big["w_out"] = _adamw(w_out[0], _exchange_wait(h_dwo, "scatter", late, "scatter_dw_out_wait"), m_w_out[0], v_w_out[0],
                          "adamw_w_out")
    big["w_in"] = _adamw_t(w_in[0], _exchange_wait(h_dwi, "scatter", big["w_out"][1], "scatter_dw_in_wait"), m_w_in[0],
                           v_w_in[0], "adamw_w_in")

    names = ["c_ctx", "w_ada", "b_ada", "w_in", "q_norm_g", "k_norm_g", "sink_logit", "w_out", "ln1_g", "ln1_b",
             "w_gate", "w_up", "w_down", "ln2_g", "ln2_b"]
    outs = [loss, grad_x[None]]
    for k in range(4):
        for nm in names:
            outs.append(big[nm][k][None] if nm in big else small[k][nm])
    return tuple(outs)
```

```python
import functools

import jax
import jax.numpy as jnp
from jax import lax
from jax.experimental import pallas as pl
from jax.experimental.pallas import tpu as pltpu

F32 = jnp.float32
BF16 = jnp.bfloat16

NDEV = 8
HEAD = 128
CTX = 256
GRID_W = 64
WINDOW = 128
ROPE_THETA = 10000.0
EPS = 1e-6
SCALE = HEAD ** -0.5
LOG2E = 1.4426950408889634
QK_LOG2 = SCALE * LOG2E
ALPHA = 2.0 ** 0.25
FFN_SHARD = 704
FFN_TILE = 512
FFN_PAIR = 2 * FFN_SHARD
IN_SHARD = 384
NEG = -1e30

ADAM_LR = 0.001
ADAM_B1 = 0.9
ADAM_B2 = 0.999
ADAM_EPS = 1e-08
ADAM_WD = 0.01
ADAM_STEP = 10

VMEM_CAP = 56 * 1024 * 1024

_KINDS = ["rope"] * 10 + ["none"] * 2 + ["qnorm"] * 8 + ["knorm"] * 2 + ["none"] * 2
NORM_HEAD0 = _KINDS.index("qnorm")
NORM_HEADS = _KINDS.count("qnorm") + _KINDS.count("knorm")

_NT = (((1,), (1,)), ((), ()))
_TN = (((0,), (0,)), ((), ()))


def _pallas(body, **kw):
    return pl.pallas_call(body, **kw)


def _params(vmem_bytes):
    return pltpu.CompilerParams(vmem_limit_bytes=int(min(VMEM_CAP, vmem_bytes)))


def _mb(n):
    return int(n * 1024 * 1024)


def _sigmoid(x):
    return 1.0 / (1.0 + jnp.exp(-x))


def _colsum(a):
    return jnp.sum(a, axis=0, keepdims=True)


def _rowmean(a):
    return jnp.mean(a, axis=-1, keepdims=True)


def _exchange(src, scatter, name, after=None):
    blk = src.shape[1:] if scatter else src.shape
    after = src if after is None else after

    def body(src_ref, after_ref, out_ref, send_sems, recv_sems, local_sem):
        x, y, c = lax.axis_index("x"), lax.axis_index("y"), lax.axis_index("c")
        me = 4 * x + 2 * y + c
        copies = []
        for t in range(1, NDEV):
            px = 1 - x if (t >> 2) & 1 else x
            py = 1 - y if (t >> 1) & 1 else y
            pc = 1 - c if t & 1 else c
            peer = 4 * px + 2 * py + pc
            cp = pltpu.make_async_remote_copy(
                src_ref=src_ref.at[peer] if scatter else src_ref,
                dst_ref=out_ref.at[me],
                send_sem=send_sems.at[t - 1],
                recv_sem=recv_sems.at[t - 1],
                device_id=(px, py, pc),
                device_id_type=pl.DeviceIdType.MESH,
            )
            cp.start()
            copies.append(cp)
        own = pltpu.make_async_copy(src_ref.at[me] if scatter else src_ref, out_ref.at[me], local_sem)
        own.start()
        for cp in copies:
            cp.wait()
        own.wait()

    return _pallas(
        body, name=name,
        out_shape=jax.ShapeDtypeStruct((NDEV,) + tuple(blk), src.dtype),
        in_specs=[pl.BlockSpec(memory_space=pl.ANY), pl.BlockSpec(memory_space=pl.ANY)],
        out_specs=pl.BlockSpec(memory_space=pl.ANY),
        scratch_shapes=[pltpu.SemaphoreType.DMA((NDEV - 1,)), pltpu.SemaphoreType.DMA((NDEV - 1,)),
                        pltpu.SemaphoreType.DMA(())],
    )(src, after)


_HBM = pl.BlockSpec(memory_space=pltpu.HBM)
_SEM = pl.BlockSpec(memory_space=pltpu.SEMAPHORE)
_ANY = pl.BlockSpec(memory_space=pl.ANY)
_EFFECT = pltpu.SideEffectType.DATAFLOW_SIDE_EFFECTING


def _exchange_copies(src_ref, land_ref, send_sems, recv_sems, mode):
    x, y, c = lax.axis_index("x"), lax.axis_index("y"), lax.axis_index("c")
    me = 4 * x + 2 * y + c
    scatter = mode == "scatter"
    copies = []
    for t in ((1, 2, 4, 6) if mode == "chip" else range(1, NDEV)):
        px = 1 - x if (t >> 2) & 1 else x
        py = 1 - y if (t >> 1) & 1 else y
        pc = 1 - c if t & 1 else c
        peer = 4 * px + 2 * py + pc
        copies.append(pltpu.make_async_remote_copy(
            src_ref=src_ref.at[peer] if scatter else src_ref,
            dst_ref=land_ref.at[me],
            send_sem=send_sems.at[t - 1],
            recv_sem=recv_sems.at[t - 1],
            device_id=(px, py, pc),
            device_id_type=pl.DeviceIdType.MESH,
        ))
    own = pltpu.make_async_copy(src_ref.at[me] if scatter else src_ref, land_ref.at[me], send_sems.at[NDEV - 1])
    return copies, own


def _forward_copies(land_ref, send_sems, recv_sems):
    x, y, c = lax.axis_index("x"), lax.axis_index("y"), lax.axis_index("c")
    copies = []
    for k, t in enumerate((2, 4, 6)):
        px = 1 - x if (t >> 2) & 1 else x
        py = 1 - y if (t >> 1) & 1 else y
        mine, theirs = 4 * px + 2 * py + c, 4 * px + 2 * py + (1 - c)
        send = pltpu.make_async_remote_copy(
            src_ref=land_ref.at[mine], dst_ref=land_ref.at[mine], send_sem=send_sems.at[k], recv_sem=recv_sems.at[k],
            device_id=(x, y, 1 - c), device_id_type=pl.DeviceIdType.MESH)
        recv = pltpu.make_async_remote_copy(
            src_ref=land_ref.at[theirs], dst_ref=land_ref.at[theirs], send_sem=send_sems.at[k], recv_sem=recv_sems.at[k],
            device_id=(x, y, 1 - c), device_id_type=pl.DeviceIdType.MESH)
        copies.append((send, recv))
    return copies


def _forward_start(land, after, name):
    def body(land_ref, after_ref, send_sems, recv_sems, land_thru, token):
        for send, _ in _forward_copies(land_ref, send_sems, recv_sems):
            send.start()
        token[...] = jnp.zeros_like(token)

    res = _pallas(
        body, name=name,
        out_shape=(pltpu.SemaphoreType.DMA((3,)), pltpu.SemaphoreType.DMA((3,)), pltpu.HBM(land.shape, land.dtype),
                   jax.ShapeDtypeStruct((8, HEAD), F32)),
        in_specs=(_HBM, _ANY), out_specs=(_SEM, _SEM, _HBM, pl.BlockSpec(memory_space=pltpu.VMEM)),
        input_output_aliases={0: 2},
        compiler_params=pltpu.CompilerParams(has_side_effects=_EFFECT),
    )(land, after)
    return res[:3], res[3]


def _forward_wait(handle, after, name):
    send_sems, recv_sems, land_thru = handle

    def body(land_ref, send_sems, recv_sems, after_ref, got_ref):
        for send, recv in _forward_copies(land_ref, send_sems, recv_sems):
            send.wait_send()
            recv.wait_recv()

    return _pallas(
        body, name=name,
        out_shape=pltpu.HBM(land_thru.shape, land_thru.dtype),
        in_specs=(_HBM, _SEM, _SEM, _ANY), out_specs=_HBM,
        input_output_aliases={0: 0},
        compiler_params=pltpu.CompilerParams(has_side_effects=_EFFECT),
    )(land_thru, send_sems, recv_sems, after)


def _exchange_start(src, mode, after, name):
    blk = src.shape[1:] if mode == "scatter" else src.shape
    land = lax.empty((NDEV,) + tuple(blk), src.dtype)

    def body(src_ref, land_ref, after_ref, send_sems, recv_sems, src_thru, land_thru, token):
        copies, own = _exchange_copies(src_ref, land_ref, send_sems, recv_sems, mode)
        for cp in copies:
            cp.start()
        own.start()
        token[...] = jnp.zeros_like(token)

    res = _pallas(
        body, name=name,
        out_shape=(pltpu.SemaphoreType.DMA((NDEV,)), pltpu.SemaphoreType.DMA((NDEV,)),
                   pltpu.HBM(src.shape, src.dtype), pltpu.HBM(land.shape, land.dtype),
                   jax.ShapeDtypeStruct((8, HEAD), F32)),
        in_specs=(_HBM, _HBM, _ANY), out_specs=(_SEM, _SEM, _HBM, _HBM, pl.BlockSpec(memory_space=pltpu.VMEM)),
        input_output_aliases={0: 2, 1: 3},
        compiler_params=pltpu.CompilerParams(has_side_effects=_EFFECT),
    )(pltpu.with_memory_space_constraint(src, pltpu.HBM), pltpu.with_memory_space_constraint(land, pltpu.HBM), after)
    return res[:4], res[4]


def _exchange_wait(handle, mode, after, name):
    send_sems, recv_sems, src_thru, land_thru = handle

    def body(src_ref, land_ref, send_sems, recv_sems, after_ref, src_dead, got_ref):
        copies, own = _exchange_copies(src_ref, land_ref, send_sems, recv_sems, mode)
        for cp in copies:
            cp.wait_send()
            cp.wait_recv()
        own.wait()

    return _pallas(
        body, name=name,
        out_shape=(pltpu.HBM(src_thru.shape, src_thru.dtype), pltpu.HBM(land_thru.shape, land_thru.dtype)),
        in_specs=(_HBM, _HBM, _SEM, _SEM, _ANY), out_specs=(_HBM, _HBM),
        input_output_aliases={0: 0, 1: 1},
        compiler_params=pltpu.CompilerParams(has_side_effects=_EFFECT),
    )(src_thru, land_thru, send_sems, recv_sems, after)[1]


def _ada_fwd(c_all, w, bias):
    r, d = c_all.shape
    e = w.shape[1]
    tn = 512

    def body(c_ref, w_ref, b_ref, o_ref):
        cv = c_ref[...]
        s = (cv * _sigmoid(cv)).astype(BF16)
        o_ref[...] = jnp.dot(s, w_ref[...].astype(BF16), preferred_element_type=F32) + b_ref[...]

    return _pallas(
        body, name="ada_fwd", grid=(e // tn,),
        out_shape=jax.ShapeDtypeStruct((r, e), F32),
        in_specs=[pl.BlockSpec((r, d), lambda j: (0, 0)), pl.BlockSpec((d, tn), lambda j: (0, j)),
                  pl.BlockSpec((1, tn), lambda j: (0, j))],
        out_specs=pl.BlockSpec((r, tn), lambda j: (0, j)),
        compiler_params=_params(_mb(24)),
    )(c_all, w, bias)


def _ada_bwd(dm16, c_all, w):
    d, e = w.shape
    tn = 512

    def body(dm_ref, c_ref, w_ref, dw_ref, dr_ref):
        j = pl.program_id(0)
        dm = dm_ref[...]
        rid = lax.broadcasted_iota(jnp.int32, dm.shape, 0)
        ctx_sum = jnp.sum(jnp.where(rid >= 8, dm, 0.0), axis=0, keepdims=True)
        rows = jnp.where(rid < 8, dm, jnp.where(rid == 8, jnp.broadcast_to(ctx_sum, dm.shape), 0.0)).astype(BF16)
        cv = c_ref[...]
        s = (cv * _sigmoid(cv)).astype(BF16)
        dw_ref[...] = lax.dot_general(s, rows, _TN, preferred_element_type=F32)
        part = lax.dot_general(rows, w_ref[...].astype(BF16), _NT, preferred_element_type=F32)

        @pl.when(j == 0)
        def _():
            dr_ref[...] = part

        @pl.when(j > 0)
        def _():
            dr_ref[...] += part

    return _pallas(
        body, name="ada_bwd", grid=(e // tn,),
        out_shape=(jax.ShapeDtypeStruct((d, e), F32), jax.ShapeDtypeStruct((16, d), F32)),
        in_specs=[pl.BlockSpec((16, tn), lambda j: (0, j)), pl.BlockSpec((16, d), lambda j: (0, 0)),
                  pl.BlockSpec((d, tn), lambda j: (0, j))],
        out_specs=(pl.BlockSpec((d, tn), lambda j: (0, j)), pl.BlockSpec((16, d), lambda j: (0, 0))),
        compiler_params=_params(_mb(32)),
    )(dm16, c_all, w)


def _rope(v, cos, sa, sb):
    return v * cos + (pltpu.roll(v, 96, 1) * sa + pltpu.roll(v, 32, 1) * sb)


def _rope_t(dt, cos, sa, sb):
    return dt * cos + (pltpu.roll(dt * sa, 32, 1) + pltpu.roll(dt * sb, 96, 1))


def _qkv_fwd(x, ct, sc, sh, wint, qg, kg, cos, sa, sb):
    n, d = x.shape
    tm = CTX
    nlat = n // tm
    na = n + CTX
    wcols = wint.shape[0]

    def body(x_ref, ct_ref, sc_ref, sh_ref, w_ref, qg_ref, kg_ref, cos_ref, sa_ref, sb_ref, u_ref, h_ref, t_ref, kt_ref):
        i = pl.program_id(0)
        xin = jnp.where(i == nlat, ct_ref[...], x_ref[...])
        u = (xin * (1.0 + sc_ref[0]) + sh_ref[0]).astype(BF16)
        u_ref[...] = u
        cos, sa, sb = cos_ref[...], sa_ref[...], sb_ref[...]
        h = lax.dot_general(u, w_ref[...], _NT, preferred_element_type=F32)
        h_ref[...] = h[:, NORM_HEAD0 * HEAD:(NORM_HEAD0 + NORM_HEADS) * HEAD]
        for hd in range(24):
            v = h[:, hd * HEAD:(hd + 1) * HEAD]
            kind = _KINDS[hd]
            if kind == "qnorm":
                v = v * lax.rsqrt(_rowmean(v * v) + EPS) * qg_ref[...]
            elif kind == "knorm":
                v = v * lax.rsqrt(_rowmean(v * v) + EPS) * kg_ref[...]
            if kind != "none":
                v = _rope(v, cos, sa, sb)
            t_ref[:, hd * HEAD:(hd + 1) * HEAD] = v.astype(BF16)
            if kind == "knorm":
                kt_ref[(hd - 20) * HEAD:(hd - 19) * HEAD, :] = v.T.astype(BF16)

    lat = lambda i: (jnp.minimum(i, nlat - 1), 0)
    row = lambda i: (i, 0)
    const2 = lambda i: (0, 0)
    return _pallas(
        body, name="qkv_fwd", grid=(nlat + 1,),
        out_shape=(jax.ShapeDtypeStruct((na, d), BF16), jax.ShapeDtypeStruct((na, NORM_HEADS * HEAD), F32),
                   jax.ShapeDtypeStruct((na, wcols), BF16), jax.ShapeDtypeStruct((2 * HEAD, na), BF16)),
        in_specs=[pl.BlockSpec((tm, d), lat), pl.BlockSpec((tm, d), const2),
                  pl.BlockSpec((1, 1, d), lambda i: (i // nlat, 0, 0)),
                  pl.BlockSpec((1, 1, d), lambda i: (i // nlat, 0, 0)),
                  pl.BlockSpec((wcols, d), const2),
                  pl.BlockSpec((1, HEAD), const2), pl.BlockSpec((1, HEAD), const2),
                  pl.BlockSpec((tm, HEAD), row), pl.BlockSpec((tm, HEAD), row), pl.BlockSpec((tm, HEAD), row)],
        out_specs=(pl.BlockSpec((tm, d), row), pl.BlockSpec((tm, NORM_HEADS * HEAD), row), pl.BlockSpec((tm, wcols), row),
                   pl.BlockSpec((2 * HEAD, tm), lambda i: (0, i))),
        compiler_params=_params(_mb(56)),
    )(x, ct, sc, sh, wint, qg, kg, cos, sa, sb)


def _qkv_bwd_prep(dqa, dka, dva, dqb, dkb, dvb, h_norm, qg, kg, cos, sa, sb):
    na = h_norm.shape[0]
    wcols = 24 * HEAD
    n = na - CTX
    tm = CTX
    nlat = n // tm

    def body(dqa_ref, dka_ref, dva_ref, dqb_ref, dkb_ref, dvb_ref, h_ref, qg_ref, kg_ref, cos_ref, sa_ref, sb_ref,
             dh_ref, dg_ref):
        i = pl.program_id(0)

        @pl.when(i == 0)
        def _():
            dg_ref[...] = jnp.zeros_like(dg_ref)

        cos, sa, sb = cos_ref[...], sa_ref[...], sb_ref[...]
        is_lat = i < nlat
        for hd in range(24):
            kind = _KINDS[hd]
            if hd < 8:
                dt = jnp.where(is_lat, dqa_ref[:, hd * HEAD:(hd + 1) * HEAD], 0.0)
            elif hd < 10:
                dt = dka_ref[:, (hd - 8) * HEAD:(hd - 7) * HEAD]
            elif hd < 12:
                dt = dva_ref[:, (hd - 10) * HEAD:(hd - 9) * HEAD]
            elif hd < 20:
                dt = jnp.where(is_lat, dqb_ref[:, (hd - 12) * HEAD:(hd - 11) * HEAD], 0.0)
            elif hd < 22:
                dt = dkb_ref[:, (hd - 20) * HEAD:(hd - 19) * HEAD]
            else:
                dt = dvb_ref[:, (hd - 22) * HEAD:(hd - 21) * HEAD]
            if kind != "none":
                dt = _rope_t(dt, cos, sa, sb)
            if kind in ("qnorm", "knorm"):
                g_ref = qg_ref if kind == "qnorm" else kg_ref
                r0 = 0 if kind == "qnorm" else 1
                xv = h_ref[:, (hd - NORM_HEAD0) * HEAD:(hd - NORM_HEAD0 + 1) * HEAD]
                xn = xv * lax.rsqrt(_rowmean(xv * xv) + EPS)
                dg_ref[r0:r0 + 1, :] += _colsum(dt * xn)
                dxn = dt * g_ref[...]
                dt = lax.rsqrt(_rowmean(xv * xv) + EPS) * (dxn - xn * _rowmean(dxn * xn))
            dh_ref[:, hd * HEAD:(hd + 1) * HEAD] = dt.astype(BF16)

    lat = lambda i: (jnp.minimum(i, nlat - 1), 0)
    row = lambda i: (i, 0)
    const2 = lambda i: (0, 0)
    return _pallas(
        body, name="qkv_bwd_prep", grid=(nlat + 1,),
        out_shape=(jax.ShapeDtypeStruct((na, wcols), BF16), jax.ShapeDtypeStruct((8, HEAD), F32)),
        in_specs=[pl.BlockSpec((tm, 8 * HEAD), lat), pl.BlockSpec((tm, 2 * HEAD), row), pl.BlockSpec((tm, 2 * HEAD), row),
                  pl.BlockSpec((tm, 8 * HEAD), lat), pl.BlockSpec((tm, 2 * HEAD), row), pl.BlockSpec((tm, 2 * HEAD), row),
                  pl.BlockSpec((tm, NORM_HEADS * HEAD), row),
                  pl.BlockSpec((1, HEAD), const2), pl.BlockSpec((1, HEAD), const2),
                  pl.BlockSpec((tm, HEAD), row), pl.BlockSpec((tm, HEAD), row), pl.BlockSpec((tm, HEAD), row)],
        out_specs=(pl.BlockSpec((tm, wcols), row), pl.BlockSpec((8, HEAD), const2)),
        compiler_params=_params(_mb(40)),
    )(dqa, dka, dva, dqb, dkb, dvb, h_norm, qg, kg, cos, sa, sb)


def _window_keys(k_ref, v_ref, n, na):
    i = pl.program_id(1)
    tq = WINDOW
    start = pl.multiple_of(jnp.clip((i - 1) * tq, 0, n - 3 * tq), tq)
    kk = jnp.concatenate([k_ref[pl.ds(start, 3 * tq), :], k_ref[n:na, :]], axis=0)
    vv = jnp.concatenate([v_ref[pl.ds(start, 3 * tq), :], v_ref[n:na, :]], axis=0)
    return kk, vv, start


def _window_bias():
    tq = WINDOW
    r = (jnp.arange(4 * tq) % tq)[:, None]
    c = jnp.arange(3 * tq + CTX)[None, :]
    variants = []
    for back in (0, tq, 2 * tq):
        seen = (jnp.abs(back + r - c) <= WINDOW) | (c >= 3 * tq)
        variants.append(jnp.where(seen, 0.0, NEG).astype(F32))
    return jnp.stack(variants)


def _window_bias_spec(nq):
    return pl.BlockSpec((1, 4 * WINDOW, 3 * WINDOW + CTX),
                        lambda kv, i: (jnp.where(i == 0, 0, jnp.where(i == nq - 1, 2, 1)), 0, 0))


def _stack_heads(ref, width=HEAD):
    return jnp.concatenate([ref[:, g * HEAD:g * HEAD + width] for g in range(4)], axis=0)


def _sink_column(sink_ref, kv, tq):
    grp = lax.broadcasted_iota(jnp.int32, (4 * tq, 1), 0) // tq
    col = jnp.zeros((4 * tq, 1), F32)
    for g in range(4):
        col = jnp.where(grp == g, sink_ref[0, 4 * kv + g] * LOG2E, col)
    return col


def _attn_window_fwd(t_all, sink, bias, after):
    na = t_all.shape[0]
    n = na - CTX
    tq = WINDOW

    def body(sink_ref, q_ref, k_ref, v_ref, bias_ref, after_ref, o_ref, lse_ref):
        kv = pl.program_id(0)
        kk, vv, _ = _window_keys(k_ref, v_ref, n, na)
        t = lax.dot_general(_stack_heads(q_ref), kk, _NT, preferred_element_type=F32) * QK_LOG2 + bias_ref[0]
        sk = _sink_column(sink_ref, kv, tq)
        m = jnp.maximum(jnp.max(t, axis=-1, keepdims=True), sk)
        p = jnp.exp2(t - m)
        l = jnp.sum(p, axis=-1, keepdims=True) + jnp.exp2(sk - m)
        o = jnp.dot(p.astype(BF16), vv, preferred_element_type=F32) * (1.0 / l)
        lse = m + jnp.log2(l)
        for g in range(4):
            o_ref[:, g * HEAD:(g + 1) * HEAD] = o[g * tq:(g + 1) * tq]
            lse_ref[:, g * HEAD:(g + 1) * HEAD] = jnp.broadcast_to(lse[g * tq:(g + 1) * tq], (tq, HEAD))

    blk = pl.BlockSpec((tq, 4 * HEAD), lambda kv, i: (i, kv))
    return _pallas(
        body, name="attn_window_fwd", grid=(2, n // tq),
        out_shape=(jax.ShapeDtypeStruct((n, 16 * HEAD), F32), jax.ShapeDtypeStruct((n, 8 * HEAD), F32)),
        in_specs=[pl.BlockSpec(memory_space=pltpu.SMEM), blk,
                  pl.BlockSpec((na, HEAD), lambda kv, i: (0, 8 + kv)),
                  pl.BlockSpec((na, HEAD), lambda kv, i: (0, 10 + kv)), _window_bias_spec(n // tq), _ANY],
        out_specs=(blk, blk),
        compiler_params=_params(_mb(32)),
    )(sink, t_all, t_all, t_all, bias, after)


def _attn_global_fwd(t_all, o_part):
    na = t_all.shape[0]
    n = na - CTX
    tq = 256

    def body(q_ref, k_ref, v_ref, o_in_ref, o_ref, p_ref, linv_ref):
        kk, vv = k_ref[...], v_ref[...]
        for g in range(4):
            q = q_ref[:, g * HEAD:(g + 1) * HEAD]
            t = lax.dot_general(q, kk, _NT, preferred_element_type=F32) * QK_LOG2
            m = jnp.max(t, axis=-1, keepdims=True)
            p = jnp.exp2(t - m)
            linv = 1.0 / jnp.sum(p, axis=-1, keepdims=True)
            pb = p.astype(BF16)
            p_ref[g] = pb
            o_ref[:, g * HEAD:(g + 1) * HEAD] = jnp.dot(pb, vv, preferred_element_type=F32) * linv
            linv_ref[:, g * HEAD:(g + 1) * HEAD] = jnp.broadcast_to(linv, (tq, HEAD))

    return _pallas(
        body, name="attn_global_fwd", grid=(2, n // tq),
        out_shape=(jax.ShapeDtypeStruct((n, 16 * HEAD), F32), jax.ShapeDtypeStruct((8, n, na), BF16),
                   jax.ShapeDtypeStruct((n, 8 * HEAD), F32)),
        in_specs=[pl.BlockSpec((tq, 4 * HEAD), lambda kv, i: (i, 3 + kv)),
                  pl.BlockSpec((na, HEAD), lambda kv, i: (0, 20 + kv)),
                  pl.BlockSpec((na, HEAD), lambda kv, i: (0, 22 + kv)), _ANY],
        out_specs=(pl.BlockSpec((tq, 4 * HEAD), lambda kv, i: (i, 2 + kv)),
                   pl.BlockSpec((4, tq, na), lambda kv, i: (kv, i, 0)),
                   pl.BlockSpec((tq, 4 * HEAD), lambda kv, i: (i, kv))),
        input_output_aliases={3: 0},
        compiler_params=_params(_mb(56)),
    )(t_all, t_all, t_all, o_part)


def _attn_window_bwd(t_all, o, do, lse, sink, bias):
    na = t_all.shape[0]
    n = na - CTX
    tq = WINDOW

    def body(sink_ref, q_ref, k_ref, v_ref, o_ref, do_ref, lse_ref, bias_ref, dq_ref, dk_ref, dv_ref, dsink_ref):
        kv = pl.program_id(0)

        @pl.when(pl.program_id(1) == 0)
        def _():
            dk_ref[...] = jnp.zeros_like(dk_ref)
            dv_ref[...] = jnp.zeros_like(dv_ref)
            dsink_ref[...] = jnp.zeros_like(dsink_ref)

        kk, vv, start = _window_keys(k_ref, v_ref, n, na)
        q = _stack_heads(q_ref)
        t = lax.dot_general(q, kk, _NT, preferred_element_type=F32) * QK_LOG2 + bias_ref[0]
        lse = _stack_heads(lse_ref, 1)
        p = jnp.exp2(t - lse)
        dof = _stack_heads(do_ref)
        delta = jnp.sum(dof * _stack_heads(o_ref), axis=-1, keepdims=True)
        dob = dof.astype(BF16)
        dv_acc = lax.dot_general(p.astype(BF16), dob, _TN, preferred_element_type=F32)
        dp = lax.dot_general(dob, vv, _NT, preferred_element_type=F32)
        ds = (p * (dp - delta) * SCALE).astype(BF16)
        dq = jnp.dot(ds, kk, preferred_element_type=F32)
        dk_acc = lax.dot_general(ds, q, _TN, preferred_element_type=F32)
        dsk = -(jnp.exp2(_sink_column(sink_ref, kv, tq) - lse) * delta)
        for g in range(4):
            dq_ref[:, g * HEAD:(g + 1) * HEAD] = dq[g * tq:(g + 1) * tq]
            dsink_ref[0, g:g + 1, :] += jnp.broadcast_to(_colsum(dsk[g * tq:(g + 1) * tq]), (1, HEAD))
        dk_ref[pl.ds(start, 3 * tq), :] += dk_acc[:3 * tq]
        dv_ref[pl.ds(start, 3 * tq), :] += dv_acc[:3 * tq]
        dk_ref[n:na, :] += dk_acc[3 * tq:]
        dv_ref[n:na, :] += dv_acc[3 * tq:]

    blk = pl.BlockSpec((tq, 4 * HEAD), lambda kv, i: (i, kv))
    kvout = pl.BlockSpec((na, HEAD), lambda kv, i: (0, kv))
    return _pallas(
        body, name="attn_window_bwd", grid=(2, n // tq),
        out_shape=(jax.ShapeDtypeStruct((n, 8 * HEAD), F32), jax.ShapeDtypeStruct((na, 2 * HEAD), F32),
                   jax.ShapeDtypeStruct((na, 2 * HEAD), F32), jax.ShapeDtypeStruct((2, 8, HEAD), F32)),
        in_specs=[pl.BlockSpec(memory_space=pltpu.SMEM), blk,
                  pl.BlockSpec((na, HEAD), lambda kv, i: (0, 8 + kv)),
                  pl.BlockSpec((na, HEAD), lambda kv, i: (0, 10 + kv)),
                  blk, blk, blk, _window_bias_spec(n // tq)],
        out_specs=(blk, kvout, kvout, pl.BlockSpec((1, 8, HEAD), lambda kv, i: (kv, 0, 0))),
        compiler_params=_params(_mb(40)),
    )(sink, t_all, t_all, t_all, o, do, lse, bias)


def _attn_global_bwd(t_all, kt, o, do, p_all, linv):
    na = t_all.shape[0]
    n = na - CTX
    tq = 256

    def body(q_ref, v_ref, kt_ref, o_ref, do_ref, p_ref, linv_ref, dq_ref, dk_ref, dv_ref, dkt_acc, dvt_acc):
        i = pl.program_id(1)

        @pl.when(i == 0)
        def _():
            dkt_acc[...] = jnp.zeros_like(dkt_acc)
            dvt_acc[...] = jnp.zeros_like(dvt_acc)

        vv, kt_v = v_ref[...], kt_ref[...]
        dkt = jnp.zeros((HEAD, na), F32)
        dvt = jnp.zeros((HEAD, na), F32)
        for g in range(4):
            q = q_ref[:, g * HEAD:(g + 1) * HEAD]
            p = p_ref[g].astype(F32) * linv_ref[:, g * HEAD:g * HEAD + 1]
            dof = do_ref[:, g * HEAD:(g + 1) * HEAD]
            delta = jnp.sum(dof * o_ref[:, g * HEAD:(g + 1) * HEAD], axis=-1, keepdims=True)
            dob = dof.astype(BF16)
            dvt = dvt + lax.dot_general(dob, p.astype(BF16), _TN, preferred_element_type=F32)
            dp = lax.dot_general(dob, vv, _NT, preferred_element_type=F32)
            ds = (p * (dp - delta) * SCALE).astype(BF16)
            dq_ref[:, g * HEAD:(g + 1) * HEAD] = lax.dot_general(kt_v, ds, _NT, preferred_element_type=F32).T
            dkt = dkt + lax.dot_general(q, ds, _TN, preferred_element_type=F32)
        dkt_acc[...] += dkt
        dvt_acc[...] += dvt

        @pl.when(i == pl.num_programs(1) - 1)
        def _():
            dk_ref[...] = dkt_acc[...].T
            dv_ref[...] = dvt_acc[...].T

    ospec = pl.BlockSpec((tq, 4 * HEAD), lambda kv, i: (i, 2 + kv))
    lspec = pl.BlockSpec((tq, 4 * HEAD), lambda kv, i: (i, kv))
    kvout = pl.BlockSpec((na, HEAD), lambda kv, i: (0, kv))
    return _pallas(
        body, name="attn_global_bwd", grid=(2, n // tq),
        out_shape=(jax.ShapeDtypeStruct((n, 8 * HEAD), F32), jax.ShapeDtypeStruct((na, 2 * HEAD), F32),
                   jax.ShapeDtypeStruct((na, 2 * HEAD), F32)),
        in_specs=[pl.BlockSpec((tq, 4 * HEAD), lambda kv, i: (i, 3 + kv)),
                  pl.BlockSpec((na, HEAD), lambda kv, i: (0, 22 + kv)),
                  pl.BlockSpec((HEAD, na), lambda kv, i: (kv, 0)),
                  ospec, ospec, pl.BlockSpec((4, tq, na), lambda kv, i: (kv, i, 0)), lspec],
        out_specs=(lspec, kvout, kvout),
        scratch_shapes=[pltpu.VMEM((HEAD, na), F32), pltpu.VMEM((HEAD, na), F32)],
        compiler_params=_params(_mb(56)),
    )(t_all, t_all, kt, o, do, p_all, linv)


def _outproj_ln1(o, wout, x, g1, lg, lb, sc2, sh2, after):
    n, d = x.shape
    tm = 256

    def body(o_ref, w_ref, x_ref, g1_ref, lg_ref, lb_ref, sc_ref, sh_ref, after_ref, a_ref, xh_ref, rs_ref, u_ref):
        a1 = jnp.dot(o_ref[...].astype(BF16), w_ref[...], preferred_element_type=F32)
        a_ref[...] = a1.astype(BF16)
        r = ALPHA * x_ref[...] + g1_ref[...] * a1
        dlt = r - _rowmean(r)
        rstd = lax.rsqrt(_rowmean(dlt * dlt) + EPS)
        xh = dlt * rstd
        xh_ref[...] = xh
        rs_ref[...] = rstd
        x1 = xh * lg_ref[...] + lb_ref[...]
        u_ref[...] = (x1 * (1.0 + sc_ref[...]) + sh_ref[...]).astype(BF16)

    row = lambda i: (i, 0)
    const2 = lambda i: (0, 0)
    vec = pl.BlockSpec((1, d), const2)
    big = pl.BlockSpec((tm, d), row)
    return _pallas(
        body, name="outproj_ln1", grid=(n // tm,),
        out_shape=(jax.ShapeDtypeStruct((n, d), BF16), jax.ShapeDtypeStruct((n, d), F32),
                   jax.ShapeDtypeStruct((n, 1), F32), jax.ShapeDtypeStruct((n, d), BF16)),
        in_specs=[big, pl.BlockSpec((d, d), const2), big, vec, vec, vec, vec, vec, _ANY],
        out_specs=(big, big, pl.BlockSpec((tm, 1), row), big),
        compiler_params=_params(_mb(56)),
    )(o, wout, x, g1, lg, lb, sc2, sh2, after)


def _ffn_up(u2, wgt, wut, after):
    n, d = u2.shape
    f = wgt.shape[0]
    tm = min(1024, n)

    def body(u_ref, wg_ref, wu_ref, after_ref, sa_ref, sb_ref, hf_ref):
        u = u_ref[...]
        gv = lax.dot_general(u, wg_ref[...], _NT, preferred_element_type=F32)
        pv = lax.dot_general(u, wu_ref[...], _NT, preferred_element_type=F32)
        sg = _sigmoid(gv)
        silu = gv * sg
        sa_ref[...] = silu.astype(BF16)
        sb_ref[...] = (pv * (sg * (1.0 + gv * (1.0 - sg)))).astype(BF16)
        hf_ref[...] = (silu * pv).astype(BF16)

    tile = pl.BlockSpec((tm, FFN_TILE), lambda i, j: (i, j))
    wspec = pl.BlockSpec((FFN_TILE, d), lambda i, j: (j, 0))
    sds = jax.ShapeDtypeStruct((n, f), BF16)
    return _pallas(
        body, name="ffn_up", grid=(n // tm, f // FFN_TILE),
        out_shape=(sds, sds, sds),
        in_specs=[pl.BlockSpec((tm, d), lambda i, j: (i, 0)), wspec, wspec, _ANY],
        out_specs=(tile, tile, tile),
        compiler_params=_params(_mb(48)),
    )(u2, wgt, wut, after)


def _ffn_down(hf, wd):
    n, f = hf.shape
    d = wd.shape[1]
    tm, tn = min(1024, n), 512

    def body(h_ref, w_ref, o_ref):
        o_ref[...] = jnp.dot(h_ref[...], w_ref[...], preferred_element_type=F32)

    return _pallas(
        body, name="ffn_down", grid=(n // tm, d // tn),
        out_shape=jax.ShapeDtypeStruct((n, d), F32),
        in_specs=[pl.BlockSpec((tm, f), lambda i, j: (i, 0)), pl.BlockSpec((f, tn), lambda i, j: (0, j))],
        out_specs=pl.BlockSpec((tm, tn), lambda i, j: (i, j)),
        compiler_params=_params(_mb(56)),
    )(hf, wd)


def _ln2_loss(xh1, ffn, tgt, lg1, lb1, g2, lg2, lb2):
    n, d = xh1.shape
    tm = 256

    def body(xh_ref, f_ref, t_ref, lg1_ref, lb1_ref, g2_ref, lg2_ref, lb2_ref, dr_ref, df_ref, loss_ref, acc_ref):
        @pl.when(pl.program_id(0) == 0)
        def _():
            loss_ref[...] = jnp.zeros_like(loss_ref)
            acc_ref[...] = jnp.zeros_like(acc_ref)

        x1 = xh_ref[...] * lg1_ref[...] + lb1_ref[...]
        fv = f_ref[...]
        r = ALPHA * x1 + g2_ref[...] * fv
        dlt = r - _rowmean(r)
        rstd = lax.rsqrt(_rowmean(dlt * dlt) + EPS)
        xh2 = dlt * rstd
        err = xh2 * lg2_ref[...] + lb2_ref[...] - t_ref[...]
        loss_ref[...] += 0.5 * jnp.sum(_rowmean(err * err))
        dy = err * (1.0 / d)
        dyg = dy * lg2_ref[...]
        dr = rstd * (dyg - _rowmean(dyg) - xh2 * _rowmean(dyg * xh2))
        dr_ref[...] = dr
        df_ref[...] = (g2_ref[...] * dr).astype(BF16)
        acc_ref[0:1, :] += _colsum(dy * xh2)
        acc_ref[1:2, :] += _colsum(dy)
        acc_ref[2:3, :] += _colsum(dr * fv)

    row = lambda i: (i, 0)
    const2 = lambda i: (0, 0)
    vec = pl.BlockSpec((1, d), const2)
    big = pl.BlockSpec((tm, d), row)
    return _pallas(
        body, name="ln2_loss", grid=(n // tm,),
        out_shape=(jax.ShapeDtypeStruct((n, d), F32), jax.ShapeDtypeStruct((n, d), BF16),
                   jax.ShapeDtypeStruct((8, HEAD), F32), jax.ShapeDtypeStruct((8, d), F32)),
        in_specs=[big, big, big, vec, vec, vec, vec, vec],
        out_specs=(big, big, pl.BlockSpec((8, HEAD), const2), pl.BlockSpec((8, d), const2)),
        compiler_params=_params(_mb(48)),
    )(xh1, ffn, tgt, lg1, lb1, g2, lg2, lb2)


def _ffn_dhf(df, wd, sa, sb):
    n, d = df.shape
    f = sa.shape[1]
    tm = min(2048, n)

    def body(df_ref, w_ref, sa_ref, sb_ref, dgp_ref):
        dhf = lax.dot_general(df_ref[...], w_ref[...], _NT, preferred_element_type=F32)
        dgp_ref[:, :FFN_TILE] = (dhf * sb_ref[...].astype(F32)).astype(BF16)
        dgp_ref[:, FFN_TILE:] = (dhf * sa_ref[...].astype(F32)).astype(BF16)

    tile = pl.BlockSpec((tm, FFN_TILE), lambda i, j: (i, j))
    return _pallas(
        body, name="ffn_dhf", grid=(n // tm, f // FFN_TILE),
        out_shape=jax.ShapeDtypeStruct((n, 2 * f), BF16),
        in_specs=[pl.BlockSpec((tm, d), lambda i, j: (i, 0)), pl.BlockSpec((FFN_TILE, d), lambda i, j: (j, 0)),
                  tile, tile],
        out_specs=pl.BlockSpec((tm, 2 * FFN_TILE), lambda i, j: (i, j)),
        compiler_params=_params(_mb(48)),
    )(df, wd, sa, sb)


def _ffn_du2(dgp, wgt, wut, after):
    n = dgp.shape[0]
    f, d = wgt.shape
    tm = min(1024, n)

    def body(dgp_ref, wg_ref, wu_ref, after_ref, o_ref):
        w = jnp.concatenate([wg_ref[...], wu_ref[...]], axis=0)
        part = jnp.dot(dgp_ref[...], w, preferred_element_type=F32)

        @pl.when(pl.program_id(1) == 0)
        def _():
            o_ref[...] = part

        @pl.when(pl.program_id(1) > 0)
        def _():
            o_ref[...] += part

    wspec = pl.BlockSpec((FFN_TILE, d), lambda i, j: (j, 0))
    return _pallas(
        body, name="ffn_du2", grid=(n // tm, f // FFN_TILE),
        out_shape=jax.ShapeDtypeStruct((n, d), F32),
        in_specs=[pl.BlockSpec((tm, 2 * FFN_TILE), lambda i, j: (i, j)), wspec, wspec, _ANY],
        out_specs=pl.BlockSpec((tm, d), lambda i, j: (i, 0)),
        compiler_params=_params(_mb(48)),
    )(dgp, wgt, wut, after)


def _dw_gate_up(dgp, u2, after):
    n, d = u2.shape
    f = dgp.shape[1] // 2
    tm = min(2048, n)

    def body(a_ref, b_ref, after_ref, og_ref, ou_ref, acc_ref):
        part = lax.dot_general(a_ref[...], b_ref[...], _TN, preferred_element_type=F32)
        i = pl.program_id(1)

        @pl.when(i == 0)
        def _():
            acc_ref[...] = part

        @pl.when(i > 0)
        def _():
            acc_ref[...] += part

        @pl.when(i == pl.num_programs(1) - 1)
        def _():
            og_ref[...] = acc_ref[:FFN_TILE].astype(BF16)
            ou_ref[...] = acc_ref[FFN_TILE:].astype(BF16)

    out = pl.BlockSpec((FFN_TILE, d), lambda j, i: (j, 0))
    sds = jax.ShapeDtypeStruct((f, d), BF16)
    return _pallas(
        body, name="dw_gate_up", grid=(f // FFN_TILE, n // tm),
        out_shape=(sds, sds),
        in_specs=[pl.BlockSpec((tm, 2 * FFN_TILE), lambda j, i: (i, j)), pl.BlockSpec((tm, d), lambda j, i: (i, 0)), _ANY],
        out_specs=(out, out),
        scratch_shapes=[pltpu.VMEM((2 * FFN_TILE, d), F32)],
        compiler_params=_params(_mb(56)),
    )(dgp, u2, after)


def _ln1_bwd(du2, dr2, xh1, rs1, a1, lg1, lb1, sc2, g1):
    n, d = du2.shape
    tm = 256

    def body(du_ref, dr2_ref, xh_ref, rs_ref, a_ref, lg_ref, lb_ref, sc_ref, g1_ref, dr1_ref, da_ref, acc_ref):
        @pl.when(pl.program_id(0) == 0)
        def _():
            acc_ref[...] = jnp.zeros_like(acc_ref)

        du = du_ref[...]
        xh = xh_ref[...]
        x1 = xh * lg_ref[...] + lb_ref[...]
        dx1 = ALPHA * dr2_ref[...] + du * (1.0 + sc_ref[...])
        dxg = dx1 * lg_ref[...]
        dr1 = rs_ref[...] * (dxg - _rowmean(dxg) - xh * _rowmean(dxg * xh))
        dr1_ref[...] = dr1
        da_ref[...] = (g1_ref[...] * dr1).astype(BF16)
        acc_ref[0:1, :] += _colsum(du * x1)
        acc_ref[1:2, :] += _colsum(du)
        acc_ref[2:3, :] += _colsum(dx1 * xh)
        acc_ref[3:4, :] += _colsum(dx1)
        acc_ref[4:5, :] += _colsum(dr1 * a_ref[...].astype(F32))

    row = lambda i: (i, 0)
    const2 = lambda i: (0, 0)
    vec = pl.BlockSpec((1, d), const2)
    big = pl.BlockSpec((tm, d), row)
    return _pallas(
        body, name="ln1_bwd", grid=(n // tm,),
        out_shape=(jax.ShapeDtypeStruct((n, d), F32), jax.ShapeDtypeStruct((n, d), BF16),
                   jax.ShapeDtypeStruct((8, d), F32)),
        in_specs=[big, big, big, pl.BlockSpec((tm, 1), row), big, vec, vec, vec, vec],
        out_specs=(big, big, pl.BlockSpec((8, d), const2)),
        compiler_params=_params(_mb(48)),
    )(du2, dr2, xh1, rs1, a1, lg1, lb1, sc2, g1)


def _dw_rows(a, b, nblk, bw, tm, after, name):
    m = a.shape[0]
    nn = b.shape[1]

    def body(a_ref, b_ref, after_ref, o_ref, acc_ref):
        part = lax.dot_general(a_ref[...].astype(BF16), b_ref[...], _TN, preferred_element_type=F32)
        i = pl.program_id(1)

        @pl.when(i == 0)
        def _():
            acc_ref[...] = part

        @pl.when(i > 0)
        def _():
            acc_ref[...] += part

        @pl.when(i == pl.num_programs(1) - 1)
        def _():
            o_ref[0] = acc_ref[...].astype(BF16)

    return _pallas(
        body, name=name, grid=(nblk, m // tm),
        out_shape=jax.ShapeDtypeStruct((nblk, bw, nn), BF16),
        in_specs=[pl.BlockSpec((tm, bw), lambda j, i: (i, j)), pl.BlockSpec((tm, nn), lambda j, i: (i, 0)), _ANY],
        out_specs=pl.BlockSpec((1, bw, nn), lambda j, i: (j, 0, 0)),
        scratch_shapes=[pltpu.VMEM((bw, nn), F32)],
        compiler_params=_params(_mb(56)),
    )(a, b, after)


def _outproj_bwd(da1, wout, after):
    n, d = da1.shape
    tm = 512

    def body(a_ref, w_ref, after_ref, o_ref):
        o_ref[...] = lax.dot_general(a_ref[...], w_ref[...], _NT, preferred_element_type=F32)

    return _pallas(
        body, name="outproj_bwd", grid=(n // tm,),
        out_shape=jax.ShapeDtypeStruct((n, d), F32),
        in_specs=[pl.BlockSpec((tm, d), lambda i: (i, 0)), pl.BlockSpec((d, d), lambda i: (0, 0)), _ANY],
        out_specs=pl.BlockSpec((tm, d), lambda i: (i, 0)),
        compiler_params=_params(_mb(48)),
    )(da1, wout, after)


def _qkv_bwd(dh, wint, x, ct, dr1, sc):
    na, wcols = dh.shape
    n, d = x.shape
    tm = CTX
    nlat = n // tm

    def body(dh_ref, w_ref, x_ref, ct_ref, dr_ref, sc_ref, gx_ref, acc_ref):
        i = pl.program_id(0)

        @pl.when(i == 0)
        def _():
            acc_ref[...] = jnp.zeros_like(acc_ref)

        du = jnp.dot(dh_ref[...], w_ref[...], preferred_element_type=F32)

        @pl.when(i < nlat)
        def _():
            gx_ref[...] = ALPHA * dr_ref[...] + du * (1.0 + sc_ref[0])
            acc_ref[0:1, :] += _colsum(du)
            acc_ref[1:2, :] += _colsum(du * x_ref[...])

        @pl.when(i == nlat)
        def _():
            acc_ref[2:3, :] += _colsum(du)
            acc_ref[3:4, :] += _colsum(du * ct_ref[...])

    lat = lambda i: (jnp.minimum(i, nlat - 1), 0)
    const2 = lambda i: (0, 0)
    return _pallas(
        body, name="qkv_bwd", grid=(nlat + 1,),
        out_shape=(jax.ShapeDtypeStruct((n, d), F32), jax.ShapeDtypeStruct((8, d), F32)),
        in_specs=[pl.BlockSpec((tm, wcols), lambda i: (i, 0)), pl.BlockSpec((wcols, d), const2),
                  pl.BlockSpec((tm, d), lat), pl.BlockSpec((tm, d), const2), pl.BlockSpec((tm, d), lat),
                  pl.BlockSpec((1, 1, d), lambda i: (0, 0, 0))],
        out_specs=(pl.BlockSpec((tm, d), lat), pl.BlockSpec((8, d), const2)),
        compiler_params=_params(_mb(56)),
    )(dh, wint, x, ct, dr1, sc)


def _adam_math(w, g, m, v):
    m2 = ADAM_B1 * m + (1.0 - ADAM_B1) * g
    v2 = ADAM_B2 * v + (1.0 - ADAM_B2) * (g * g)
    m_hat = m2 * (1.0 / (1.0 - ADAM_B1 ** ADAM_STEP))
    v_hat = v2 * (1.0 / (1.0 - ADAM_B2 ** ADAM_STEP))
    delta = -ADAM_LR * (m_hat / (jnp.sqrt(v_hat) + ADAM_EPS) + ADAM_WD * w)
    return delta, m2, v2


def _adamw(w, gsrc, m, v, name, after=None):
    r, c = w.shape
    parts = gsrc.ndim == 3
    after = w if after is None else after
    tr = r
    while tr * c * 4 > _mb(1) and tr % 32 == 0:
        tr //= 2

    def body(w_ref, g_ref, m_ref, v_ref, after_ref, go_ref, d_ref, mo_ref, vo_ref):
        if parts:
            g = g_ref[0].astype(F32)
            for s in range(1, NDEV):
                g = g + g_ref[s].astype(F32)
        else:
            g = g_ref[...]
        delta, m2, v2 = _adam_math(w_ref[...], g, m_ref[...], v_ref[...])
        go_ref[...] = g
        d_ref[...] = delta
        mo_ref[...] = m2
        vo_ref[...] = v2

    tile = pl.BlockSpec((tr, c), lambda i: (i, 0))
    gspec = pl.BlockSpec((NDEV, tr, c), lambda i: (0, i, 0)) if parts else tile
    sds = jax.ShapeDtypeStruct((r, c), F32)
    return _pallas(
        body, name=name, grid=(r // tr,),
        out_shape=(sds, sds, sds, sds),
        in_specs=[tile, gspec, tile, tile, _ANY],
        out_specs=(tile, tile, tile, tile),
        compiler_params=_params(_mb(48)),
    )(w, gsrc, m, v, after)


def _adamw_t(w, gsrc_t, m, v, name):
    r, c = w.shape
    tr = 256

    def body(w_ref, g_ref, m_ref, v_ref, go_ref, d_ref, mo_ref, vo_ref):
        gt = g_ref[0].astype(F32)
        for s in range(1, NDEV):
            gt = gt + g_ref[s].astype(F32)
        g = gt.T
        delta, m2, v2 = _adam_math(w_ref[...], g, m_ref[...], v_ref[...])
        go_ref[...] = g
        d_ref[...] = delta
        mo_ref[...] = m2
        vo_ref[...] = v2

    tile = pl.BlockSpec((tr, c), lambda i: (i, 0))
    sds = jax.ShapeDtypeStruct((r, c), F32)
    return _pallas(
        body, name=name, grid=(r // tr,),
        out_shape=(sds, sds, sds, sds),
        in_specs=[tile, pl.BlockSpec((NDEV, c, tr), lambda i: (0, 0, i)), tile, tile],
        out_specs=(tile, tile, tile, tile),
        compiler_params=_params(_mb(48)),
    )(w, gsrc_t, m, v)


def _small_update(gath, dcc, cc, w_s, m_s, v_s):
    d = w_s.shape[1]

    def body(g_ref, dcc_ref, cc_ref, w_ref, m_ref, v_ref, go_ref, d_ref, mo_ref, vo_ref):
        s = g_ref[0]
        for b in range(1, NDEV):
            s = s + g_ref[b]
        dsl = dcc_ref[0, 8:9, :]
        for b in range(1, NDEV):
            dsl = dsl + dcc_ref[b, 8:9, :]
        cv = cc_ref[...]
        sg = _sigmoid(cv)
        go_ref[...] = jnp.zeros_like(go_ref)
        go_ref[0:1, :] = dsl * (sg * (1.0 + cv * (1.0 - sg)))
        go_ref[1:3, :] = s[0:2] + s[6:8]
        go_ref[3:7, :] = s[2:6]
        go_ref[7:12, :] = s[8:13]
        delta, m2, v2 = _adam_math(w_ref[...], go_ref[...], m_ref[...], v_ref[...])
        d_ref[...] = delta
        mo_ref[...] = m2
        vo_ref[...] = v2

    full = pl.BlockSpec((16, d), lambda: (0, 0))
    g3 = pl.BlockSpec((NDEV, 16, d), lambda: (0, 0, 0))
    sds = jax.ShapeDtypeStruct((16, d), F32)
    return _pallas(
        body, name="small_update",
        out_shape=(sds, sds, sds, sds),
        in_specs=[g3, g3, pl.BlockSpec((1, d), lambda: (0, 0)), full, full, full],
        out_specs=(full, full, full, full),
        compiler_params=_params(_mb(24)),
    )(gath, dcc, cc, w_s, m_s, v_s)


def _rope_tables(n):
    rows = n // GRID_W
    row_ids = jnp.repeat(jnp.arange(rows, dtype=F32), GRID_W)
    col_ids = jnp.tile(jnp.arange(GRID_W, dtype=F32), rows)
    axis_dim = HEAD // 2
    inv_freq = jnp.power(ROPE_THETA, -jnp.arange(0, axis_dim, 2, dtype=F32) / axis_dim)
    ang_r = row_ids[:, None] * inv_freq
    ang_c = col_ids[:, None] * inv_freq
    ang = jnp.concatenate([ang_r, ang_r, ang_c, ang_c], axis=-1)
    cos, sin = jnp.cos(ang), jnp.sin(ang)
    first = (jnp.arange(HEAD) % (HEAD // 2)) < HEAD // 4
    sa = jnp.where(first, -sin, 0.0)
    sb = jnp.where(first, 0.0, sin)
    ones = jnp.ones((CTX, HEAD), F32)
    zeros = jnp.zeros((CTX, HEAD), F32)
    return (jnp.concatenate([cos, ones], 0), jnp.concatenate([sa, zeros], 0), jnp.concatenate([sb, zeros], 0))


def _pad_cols(a, width):
    return jnp.pad(a, ((0, 0), (0, width - a.shape[1])))


def _pad_rows(a, rows):
    return jnp.pad(a, ((0, rows - a.shape[0]), (0, 0)))


def _pack_small(c_ctx, b_ada, ln1_g, ln1_b, ln2_g, ln2_b, qg, kg, sink, d):
    misc = _pad_cols(jnp.concatenate([qg, kg, sink], axis=1), d)
    rows = jnp.concatenate([c_ctx.reshape(1, d), b_ada.reshape(6, d), ln1_g, ln1_b, ln2_g, ln2_b, misc], axis=0)
    return _pad_rows(rows, 16)


def _unpack_small(p, d):
    return dict(c_ctx=p[0], b_ada=p[1:7].reshape(1, 6 * d), ln1_g=p[7:8], ln1_b=p[8:9], ln2_g=p[9:10], ln2_b=p[10:11],
                q_norm_g=p[11:12, 0:HEAD], k_norm_g=p[11:12, HEAD:2 * HEAD], sink_logit=p[11:12, 2 * HEAD:2 * HEAD + 8])


def kernel(x, c, ctx, c_ctx, w_ada, b_ada, w_in, q_norm_g, k_norm_g, sink_logit, w_out, ln1_g, ln1_b, w_gate, w_up, w_down, ln2_g, ln2_b, loss_target, m_c_ctx, m_w_ada, m_b_ada, m_w_in, m_q_norm_g, m_k_norm_g, m_sink_logit, m_w_out, m_ln1_g, m_ln1_b, m_w_gate, m_w_up, m_w_down, m_ln2_g, m_ln2_b, v_c_ctx, v_w_ada, v_b_ada, v_w_in, v_q_norm_g, v_k_norm_g, v_sink_logit, v_w_out, v_ln1_g, v_ln1_b, v_w_gate, v_w_up, v_w_down, v_ln2_g, v_ln2_b):
    xs, cts, tgt = x[0], ctx[0], loss_target[0]
    n, d = xs.shape
    assert cts.shape == (CTX, d) and w_in.shape[2] == IN_SHARD and w_gate.shape[2] == FFN_SHARD
    me = 4 * lax.axis_index("x") + 2 * lax.axis_index("y") + lax.axis_index("c")
    e_sh = w_ada.shape[2]

    c_g = _exchange(_pad_rows(c, 8), False, "gather_c")
    c_all = jnp.concatenate([c_g[:, 0, :], _pad_rows(c_ctx.reshape(1, d), 8)], axis=0)
    bias_sh = lax.dynamic_slice(b_ada, (0, me * e_sh), (1, e_sh))
    mods_g = _exchange(_ada_fwd(c_all, w_ada[0], bias_sh), False, "gather_mods")
    mods = jnp.transpose(mods_g, (1, 0, 2)).reshape(16, NDEV * e_sh)
    mine = lax.dynamic_slice(mods, (me, 0), (1, 6 * d))
    sh1, sc1, g1, sh2, sc2, g2 = [mine[:, k * d:(k + 1) * d] for k in range(6)]
    csh1, csc1 = mods[8:9, 0:d], mods[8:9, d:2 * d]
    sc_pair = jnp.stack([sc1, csc1])
    sh_pair = jnp.stack([sh1, csh1])

    h_win, tok = _exchange_start(w_in[0].T.astype(BF16), "chip", mods, "gather_w_in_start")
    tok, (wo_l, wg_l, wu_l, wd_l) = lax.optimization_barrier((tok, (w_out, w_gate, w_up, w_down)))
    h_wout, tok = _exchange_start(wo_l[0].astype(BF16), "chip", tok, "gather_w_out_start")
    h_wg, tok = _exchange_start(wg_l[0].T.astype(BF16), "chip", tok, "gather_w_gate_start")
    h_wu, tok = _exchange_start(wu_l[0].T.astype(BF16), "chip", tok, "gather_w_up_start")
    h_wd, tok = _exchange_start(wd_l[0].astype(BF16), "chip", tok, "gather_w_down_start")

    cos, sa, sb = _rope_tables(n)
    f_win, tok = _forward_start(_exchange_wait(h_win, "chip", tok, "gather_w_in_wait"), tok, "forward_w_in_start")
    win_g = _forward_wait(f_win, tok, "forward_w_in_wait").reshape(NDEV * IN_SHARD, d)
    u_all, h_all, t_all, kt_b = _qkv_fwd(xs, cts, sc_pair, sh_pair, win_g, q_norm_g, k_norm_g, cos, sa, sb)
    f_wout, tok = _forward_start(_exchange_wait(h_wout, "chip", t_all, "gather_w_out_wait"), t_all, "forward_w_out_start")
    win_bias = _window_bias()
    o_a, lse_a = _attn_window_fwd(t_all, sink_logit, win_bias, tok)
    o, p_b, linv_b = _attn_global_fwd(t_all, o_a)
    f_wg, tok = _forward_start(_exchange_wait(h_wg, "chip", o, "gather_w_gate_wait"), o, "forward_w_gate_start")
    f_wu, tok = _forward_start(_exchange_wait(h_wu, "chip", tok, "gather_w_up_wait"), tok, "forward_w_up_start")
    wout_g = _forward_wait(f_wout, tok, "forward_w_out_wait").reshape(d, d)
    a1, xh1, rs1, u2 = _outproj_ln1(o, wout_g, xs, g1, ln1_g, ln1_b, sc2, sh2, tok)
    f_wd, tok = _forward_start(_exchange_wait(h_wd, "chip", rs1, "gather_w_down_wait"), rs1, "forward_w_down_start")
    ffn_w = (NDEV * FFN_SHARD, d)
    wg_g = _forward_wait(f_wg, tok, "forward_w_gate_wait").reshape(ffn_w)
    wu_g = _forward_wait(f_wu, tok, "forward_w_up_wait").reshape(ffn_w)
    sa_f, sb_f, hf = _ffn_up(u2, wg_g, wu_g, tok)
    wd_g = _forward_wait(f_wd, hf, "forward_w_down_wait").reshape(ffn_w)
    ffn = _ffn_down(hf, wd_g)
    dr2, df, loss_p, acc2 = _ln2_loss(xh1, ffn, tgt, ln1_g, ln1_b, g2, ln2_g, ln2_b)
    loss = lax.psum(loss_p[0, 0], ("x", "y", "c"))

    parts = (NDEV, FFN_SHARD, d)
    dgp = _ffn_dhf(df, wd_g, sa_f, sb_f)
    dwd_p = _dw_rows(hf, df, NDEV // 2, FFN_PAIR, min(n, 1024), loss_p, "dw_down").reshape(parts)
    h_dwd, tok = _exchange_start(dwd_p, "scatter", loss.reshape(1, 1), "scatter_dw_down_start")
    dwg_t, dwu_t = _dw_gate_up(dgp, u2, tok)
    h_dwg, tok = _exchange_start(dwg_t.reshape(parts), "scatter", tok, "scatter_dw_gate_start")
    h_dwu, tok = _exchange_start(dwu_t.reshape(parts), "scatter", tok, "scatter_dw_up_start")
    du2 = _ffn_du2(dgp, wg_g, wu_g, tok)
    dr1, da1, acc1 = _ln1_bwd(du2, dr2, xh1, rs1, a1, ln1_g, ln1_b, sc2, g1)
    dwo_p = _dw_rows(o, da1, 2, 8 * HEAD, min(n, 1024), loss_p, "dw_out").reshape(NDEV, 2 * HEAD, d)
    h_dwo, tok = _exchange_start(dwo_p, "scatter", loss_p, "scatter_dw_out_start")
    do = _outproj_bwd(da1, wout_g, tok)
    dqa, dka, dva, dsink = _attn_window_bwd(t_all, o, do, lse_a, sink_logit, win_bias)
    dqb, dkb, dvb = _attn_global_bwd(t_all, kt_b, o, do, p_b, linv_b)
    dh_all, dnorm = _qkv_bwd_prep(dqa, dka, dva, dqb, dkb, dvb, h_all, q_norm_g, k_norm_g, cos, sa, sb)
    grad_x, acc0 = _qkv_bwd(dh_all, win_g, xs, cts, dr1, sc_pair)

    misc = _pad_cols(jnp.concatenate([dnorm[0:1], dnorm[1:2], dsink[:, 0:4, 0].reshape(1, 8)], axis=1), d)
    part = jnp.concatenate([
        acc0[0:2], acc1[4:5], acc1[1:2], acc1[0:1], acc2[2:3],
        acc0[2:4],
        acc1[2:4], acc2[0:2],
        misc, jnp.zeros((3, d), F32)], axis=0)
    gath = _exchange(part, False, "gather_small")
    dm_batch = gath[:, 0:6, :].reshape(NDEV, 6 * d)
    dm_ctx = _pad_cols(gath[:, 6:8, :].reshape(NDEV, 2 * d), 6 * d)
    dm16 = lax.dynamic_slice(jnp.concatenate([dm_batch, dm_ctx], axis=0), (0, me * e_sh), (16, e_sh))
    dw_ada, drow = _ada_bwd(dm16, c_all, w_ada[0])
    dcc = _exchange(drow, False, "gather_dcc")
    dwi_p = _dw_rows(dh_all, u_all, NDEV // 2, 2 * IN_SHARD, (n + CTX) // 2, dcc, "dw_in")
    dwi_p = dwi_p.reshape(NDEV, IN_SHARD, d)
    h_dwi, tok = _exchange_start(dwi_p, "scatter", dcc, "scatter_dw_in_start")

    w_s = _pack_small(c_ctx, b_ada, ln1_g, ln1_b, ln2_g, ln2_b, q_norm_g, k_norm_g, sink_logit, d)
    m_s = _pack_small(m_c_ctx, m_b_ada, m_ln1_g, m_ln1_b, m_ln2_g, m_ln2_b, m_q_norm_g, m_k_norm_g, m_sink_logit, d)
    v_s = _pack_small(v_c_ctx, v_b_ada, v_ln1_g, v_ln1_b, v_ln2_g, v_ln2_b, v_q_norm_g, v_k_norm_g, v_sink_logit, d)
    small = [_unpack_small(p, d) for p in _small_update(gath, dcc, c_ctx.reshape(1, d), w_s, m_s, v_s)]

    big = {}
    big["w_ada"] = _adamw(w_ada[0], dw_ada, m_w_ada[0], v_w_ada[0], "adamw_w_ada", after=tok)
    big["w_down"] = _adamw(w_down[0], _exchange_wait(h_dwd, "scatter", big["w_ada"][1], "scatter_dw_down_wait"),
                           m_w_down[0], v_w_down[0], "adamw_w_down")
    late = big["w_down"][1]
    for nm, wt, mt, vt, hd in (("w_gate", w_gate, m_w_gate, v_w_gate, h_dwg), ("w_up", w_up, m_w_up, v_w_up, h_dwu)):
        res = _adamw(wt[0].T, _exchange_wait(hd, "scatter", late, "scatter_d" + nm + "_wait"), mt[0].T, vt[0].T,
                     "adamw_" + nm)
        big[nm] = [r.T for r in res]
        late = res[1]
    big["w_out"] = _adamw(w_out[0], _exchange_wait(h_dwo, "scatter", late, "scatter_dw_out_wait"), m_w_out[0], v_w_out[0],
                          "adamw_w_out")
    big["w_in"] = _adamw_t(w_in[0], _exchange_wait(h_dwi, "scatter", big["w_out"][1], "scatter_dw_in_wait"), m_w_in[0],
                           v_w_in[0], "adamw_w_in")

    names = ["c_ctx", "w_ada", "b_ada", "w_in", "q_norm_g", "k_norm_g", "sink_logit", "w_out", "ln1_g", "ln1_b",
             "w_gate", "w_up", "w_down", "ln2_g", "ln2_b"]
    outs = [loss, grad_x[None]]
    for k in range(4):
        for nm in names:
            outs.append(big[nm][k][None] if nm in big else small[k][nm])
    return tuple(outs)
```

```python
import functools

import jax
import jax.numpy as jnp
from jax import lax
from jax.experimental import pallas as pl
from jax.experimental.pallas import tpu as pltpu

F32 = jnp.float32
BF16 = jnp.bfloat16

NDEV = 8
HEAD = 128
CTX = 256
GRID_W = 64
WINDOW = 128
ROPE_THETA = 10000.0
EPS = 1e-6
SCALE = HEAD ** -0.5
LOG2E = 1.4426950408889634
QK_LOG2 = SCALE * LOG2E
ALPHA = 2.0 ** 0.25
FFN_SHARD = 704
FFN_TILE = 512
IN_SHARD = 384
NEG = -1e30

ADAM_LR = 0.001
ADAM_B1 = 0.9
ADAM_B2 = 0.999
ADAM_EPS = 1e-08
ADAM_WD = 0.01
ADAM_STEP = 10

VMEM_CAP = 56 * 1024 * 1024

_KINDS = ["rope"] * 10 + ["none"] * 2 + ["qnorm"] * 8 + ["knorm"] * 2 + ["none"] * 2
NORM_HEAD0 = _KINDS.index("qnorm")
NORM_HEADS = _KINDS.count("qnorm") + _KINDS.count("knorm")

_NT = (((1,), (1,)), ((), ()))
_TN = (((0,), (0,)), ((), ()))


def _pallas(body, **kw):
    return pl.pallas_call(body, **kw)


def _params(vmem_bytes):
    return pltpu.CompilerParams(vmem_limit_bytes=int(min(VMEM_CAP, vmem_bytes)))


def _mb(n):
    return int(n * 1024 * 1024)


def _sigmoid(x):
    return 1.0 / (1.0 + jnp.exp(-x))


def _colsum(a):
    return jnp.sum(a, axis=0, keepdims=True)


def _rowmean(a):
    return jnp.mean(a, axis=-1, keepdims=True)


def _exchange(src, scatter, name, after=None):
    blk = src.shape[1:] if scatter else src.shape
    after = src if after is None else after

    def body(src_ref, after_ref, out_ref, send_sems, recv_sems, local_sem):
        x, y, c = lax.axis_index("x"), lax.axis_index("y"), lax.axis_index("c")
        me = 4 * x + 2 * y + c
        copies = []
        for t in range(1, NDEV):
            px = 1 - x if (t >> 2) & 1 else x
            py = 1 - y if (t >> 1) & 1 else y
            pc = 1 - c if t & 1 else c
            peer = 4 * px + 2 * py + pc
            cp = pltpu.make_async_remote_copy(
                src_ref=src_ref.at[peer] if scatter else src_ref,
                dst_ref=out_ref.at[me],
                send_sem=send_sems.at[t - 1],
                recv_sem=recv_sems.at[t - 1],
                device_id=(px, py, pc),
                device_id_type=pl.DeviceIdType.MESH,
            )
            cp.start()
            copies.append(cp)
        own = pltpu.make_async_copy(src_ref.at[me] if scatter else src_ref, out_ref.at[me], local_sem)
        own.start()
        for cp in copies:
            cp.wait()
        own.wait()

    return _pallas(
        body, name=name,
        out_shape=jax.ShapeDtypeStruct((NDEV,) + tuple(blk), src.dtype),
        in_specs=[pl.BlockSpec(memory_space=pl.ANY), pl.BlockSpec(memory_space=pl.ANY)],
        out_specs=pl.BlockSpec(memory_space=pl.ANY),
        scratch_shapes=[pltpu.SemaphoreType.DMA((NDEV - 1,)), pltpu.SemaphoreType.DMA((NDEV - 1,)),
                        pltpu.SemaphoreType.DMA(())],
    )(src, after)


_HBM = pl.BlockSpec(memory_space=pltpu.HBM)
_SEM = pl.BlockSpec(memory_space=pltpu.SEMAPHORE)
_ANY = pl.BlockSpec(memory_space=pl.ANY)
_EFFECT = pltpu.SideEffectType.DATAFLOW_SIDE_EFFECTING


def _exchange_copies(src_ref, land_ref, send_sems, recv_sems, mode):
    x, y, c = lax.axis_index("x"), lax.axis_index("y"), lax.axis_index("c")
    me = 4 * x + 2 * y + c
    scatter = mode == "scatter"
    copies = []
    for t in ((1, 2, 4, 6) if mode == "chip" else range(1, NDEV)):
        px = 1 - x if (t >> 2) & 1 else x
        py = 1 - y if (t >> 1) & 1 else y
        pc = 1 - c if t & 1 else c
        peer = 4 * px + 2 * py + pc
        copies.append(pltpu.make_async_remote_copy(
            src_ref=src_ref.at[peer] if scatter else src_ref,
            dst_ref=land_ref.at[me],
            send_sem=send_sems.at[t - 1],
            recv_sem=recv_sems.at[t - 1],
            device_id=(px, py, pc),
            device_id_type=pl.DeviceIdType.MESH,
        ))
    own = pltpu.make_async_copy(src_ref.at[me] if scatter else src_ref, land_ref.at[me], send_sems.at[NDEV - 1])
    return copies, own


def _forward_copies(land_ref, send_sems, recv_sems):
    x, y, c = lax.axis_index("x"), lax.axis_index("y"), lax.axis_index("c")
    copies = []
    for k, t in enumerate((2, 4, 6)):
        px = 1 - x if (t >> 2) & 1 else x
        py = 1 - y if (t >> 1) & 1 else y
        mine, theirs = 4 * px + 2 * py + c, 4 * px + 2 * py + (1 - c)
        send = pltpu.make_async_remote_copy(
            src_ref=land_ref.at[mine], dst_ref=land_ref.at[mine], send_sem=send_sems.at[k], recv_sem=recv_sems.at[k],
            device_id=(x, y, 1 - c), device_id_type=pl.DeviceIdType.MESH)
        recv = pltpu.make_async_remote_copy(
            src_ref=land_ref.at[theirs], dst_ref=land_ref.at[theirs], send_sem=send_sems.at[k], recv_sem=recv_sems.at[k],
            device_id=(x, y, 1 - c), device_id_type=pl.DeviceIdType.MESH)
        copies.append((send, recv))
    return copies


def _forward_start(land, after, name):
    def body(land_ref, after_ref, send_sems, recv_sems, land_thru, token):
        for send, _ in _forward_copies(land_ref, send_sems, recv_sems):
            send.start()
        token[...] = jnp.zeros_like(token)

    res = _pallas(
        body, name=name,
        out_shape=(pltpu.SemaphoreType.DMA((3,)), pltpu.SemaphoreType.DMA((3,)), pltpu.HBM(land.shape, land.dtype),
                   jax.ShapeDtypeStruct((8, HEAD), F32)),
        in_specs=(_HBM, _ANY), out_specs=(_SEM, _SEM, _HBM, pl.BlockSpec(memory_space=pltpu.VMEM)),
        input_output_aliases={0: 2},
        compiler_params=pltpu.CompilerParams(has_side_effects=_EFFECT),
    )(land, after)
    return res[:3], res[3]


def _forward_wait(handle, after, name):
    send_sems, recv_sems, land_thru = handle

    def body(land_ref, send_sems, recv_sems, after_ref, got_ref):
        for send, recv in _forward_copies(land_ref, send_sems, recv_sems):
            send.wait_send()
            recv.wait_recv()

    return _pallas(
        body, name=name,
        out_shape=pltpu.HBM(land_thru.shape, land_thru.dtype),
        in_specs=(_HBM, _SEM, _SEM, _ANY), out_specs=_HBM,
        input_output_aliases={0: 0},
        compiler_params=pltpu.CompilerParams(has_side_effects=_EFFECT),
    )(land_thru, send_sems, recv_sems, after)


def _exchange_start(src, mode, after, name):
    blk = src.shape[1:] if mode == "scatter" else src.shape
    land = lax.empty((NDEV,) + tuple(blk), src.dtype)

    def body(src_ref, land_ref, after_ref, send_sems, recv_sems, src_thru, land_thru, token):
        copies, own = _exchange_copies(src_ref, land_ref, send_sems, recv_sems, mode)
        for cp in copies:
            cp.start()
        own.start()
        token[...] = jnp.zeros_like(token)

    res = _pallas(
        body, name=name,
        out_shape=(pltpu.SemaphoreType.DMA((NDEV,)), pltpu.SemaphoreType.DMA((NDEV,)),
                   pltpu.HBM(src.shape, src.dtype), pltpu.HBM(land.shape, land.dtype),
                   jax.ShapeDtypeStruct((8, HEAD), F32)),
        in_specs=(_HBM, _HBM, _ANY), out_specs=(_SEM, _SEM, _HBM, _HBM, pl.BlockSpec(memory_space=pltpu.VMEM)),
        input_output_aliases={0: 2, 1: 3},
        compiler_params=pltpu.CompilerParams(has_side_effects=_EFFECT),
    )(pltpu.with_memory_space_constraint(src, pltpu.HBM), pltpu.with_memory_space_constraint(land, pltpu.HBM), after)
    return res[:4], res[4]


def _exchange_wait(handle, mode, after, name):
    send_sems, recv_sems, src_thru, land_thru = handle

    def body(src_ref, land_ref, send_sems, recv_sems, after_ref, src_dead, got_ref):
        copies, own = _exchange_copies(src_ref, land_ref, send_sems, recv_sems, mode)
        for cp in copies:
            cp.wait_send()
            cp.wait_recv()
        own.wait()

    return _pallas(
        body, name=name,
        out_shape=(pltpu.HBM(src_thru.shape, src_thru.dtype), pltpu.HBM(land_thru.shape, land_thru.dtype)),
        in_specs=(_HBM, _HBM, _SEM, _SEM, _ANY), out_specs=(_HBM, _HBM),
        input_output_aliases={0: 0, 1: 1},
        compiler_params=pltpu.CompilerParams(has_side_effects=_EFFECT),
    )(src_thru, land_thru, send_sems, recv_sems, after)[1]


def _ada_fwd(c_all, w, bias):
    r, d = c_all.shape
    e = w.shape[1]
    tn = 512

    def body(c_ref, w_ref, b_ref, o_ref):
        cv = c_ref[...]
        s = (cv * _sigmoid(cv)).astype(BF16)
        o_ref[...] = jnp.dot(s, w_ref[...].astype(BF16), preferred_element_type=F32) + b_ref[...]

    return _pallas(
        body, name="ada_fwd", grid=(e // tn,),
        out_shape=jax.ShapeDtypeStruct((r, e), F32),
        in_specs=[pl.BlockSpec((r, d), lambda j: (0, 0)), pl.BlockSpec((d, tn), lambda j: (0, j)),
                  pl.BlockSpec((1, tn), lambda j: (0, j))],
        out_specs=pl.BlockSpec((r, tn), lambda j: (0, j)),
        compiler_params=_params(_mb(24)),
    )(c_all, w, bias)


def _ada_bwd(dm16, c_all, w):
    d, e = w.shape
    tn = 512

    def body(dm_ref, c_ref, w_ref, dw_ref, dr_ref):
        j = pl.program_id(0)
        dm = dm_ref[...]
        rid = lax.broadcasted_iota(jnp.int32, dm.shape, 0)
        ctx_sum = jnp.sum(jnp.where(rid >= 8, dm, 0.0), axis=0, keepdims=True)
        rows = jnp.where(rid < 8, dm, jnp.where(rid == 8, jnp.broadcast_to(ctx_sum, dm.shape), 0.0)).astype(BF16)
        cv = c_ref[...]
        s = (cv * _sigmoid(cv)).astype(BF16)
        dw_ref[...] = lax.dot_general(s, rows, _TN, preferred_element_type=F32)
        part = lax.dot_general(rows, w_ref[...].astype(BF16), _NT, preferred_element_type=F32)

        @pl.when(j == 0)
        def _():
            dr_ref[...] = part

        @pl.when(j > 0)
        def _():
            dr_ref[...] += part

    return _pallas(
        body, name="ada_bwd", grid=(e // tn,),
        out_shape=(jax.ShapeDtypeStruct((d, e), F32), jax.ShapeDtypeStruct((16, d), F32)),
        in_specs=[pl.BlockSpec((16, tn), lambda j: (0, j)), pl.BlockSpec((16, d), lambda j: (0, 0)),
                  pl.BlockSpec((d, tn), lambda j: (0, j))],
        out_specs=(pl.BlockSpec((d, tn), lambda j: (0, j)), pl.BlockSpec((16, d), lambda j: (0, 0))),
        compiler_params=_params(_mb(32)),
    )(dm16, c_all, w)


def _rope(v, cos, sa, sb):
    return v * cos + (pltpu.roll(v, 96, 1) * sa + pltpu.roll(v, 32, 1) * sb)


def _rope_t(dt, cos, sa, sb):
    return dt * cos + (pltpu.roll(dt * sa, 32, 1) + pltpu.roll(dt * sb, 96, 1))


def _qkv_fwd(x, ct, sc, sh, wint, qg, kg, cos, sa, sb):
    n, d = x.shape
    tm = CTX
    nlat = n // tm
    na = n + CTX
    wcols = wint.shape[0]

    def body(x_ref, ct_ref, sc_ref, sh_ref, w_ref, qg_ref, kg_ref, cos_ref, sa_ref, sb_ref, u_ref, h_ref, t_ref, kt_ref):
        i = pl.program_id(0)
        xin = jnp.where(i == nlat, ct_ref[...], x_ref[...])
        u = (xin * (1.0 + sc_ref[0]) + sh_ref[0]).astype(BF16)
        u_ref[...] = u
        cos, sa, sb = cos_ref[...], sa_ref[...], sb_ref[...]
        h = lax.dot_general(u, w_ref[...], _NT, preferred_element_type=F32)
        h_ref[...] = h[:, NORM_HEAD0 * HEAD:(NORM_HEAD0 + NORM_HEADS) * HEAD]
        for hd in range(24):
            v = h[:, hd * HEAD:(hd + 1) * HEAD]
            kind = _KINDS[hd]
            if kind == "qnorm":
                v = v * lax.rsqrt(_rowmean(v * v) + EPS) * qg_ref[...]
            elif kind == "knorm":
                v = v * lax.rsqrt(_rowmean(v * v) + EPS) * kg_ref[...]
            if kind != "none":
                v = _rope(v, cos, sa, sb)
            t_ref[:, hd * HEAD:(hd + 1) * HEAD] = v.astype(BF16)
            if kind == "knorm":
                kt_ref[(hd - 20) * HEAD:(hd - 19) * HEAD, :] = v.T.astype(BF16)

    lat = lambda i: (jnp.minimum(i, nlat - 1), 0)
    row = lambda i: (i, 0)
    const2 = lambda i: (0, 0)
    return _pallas(
        body, name="qkv_fwd", grid=(nlat + 1,),
        out_shape=(jax.ShapeDtypeStruct((na, d), BF16), jax.ShapeDtypeStruct((na, NORM_HEADS * HEAD), F32),
                   jax.ShapeDtypeStruct((na, wcols), BF16), jax.ShapeDtypeStruct((2 * HEAD, na), BF16)),
        in_specs=[pl.BlockSpec((tm, d), lat), pl.BlockSpec((tm, d), const2),
                  pl.BlockSpec((1, 1, d), lambda i: (i // nlat, 0, 0)),
                  pl.BlockSpec((1, 1, d), lambda i: (i // nlat, 0, 0)),
                  pl.BlockSpec((wcols, d), const2),
                  pl.BlockSpec((1, HEAD), const2), pl.BlockSpec((1, HEAD), const2),
                  pl.BlockSpec((tm, HEAD), row), pl.BlockSpec((tm, HEAD), row), pl.BlockSpec((tm, HEAD), row)],
        out_specs=(pl.BlockSpec((tm, d), row), pl.BlockSpec((tm, NORM_HEADS * HEAD), row), pl.BlockSpec((tm, wcols), row),
                   pl.BlockSpec((2 * HEAD, tm), lambda i: (0, i))),
        compiler_params=_params(_mb(56)),
    )(x, ct, sc, sh, wint, qg, kg, cos, sa, sb)


def _qkv_bwd_prep(dqa, dka, dva, dqb, dkb, dvb, h_norm, qg, kg, cos, sa, sb):
    na = h_norm.shape[0]
    wcols = 24 * HEAD
    n = na - CTX
    tm = CTX
    nlat = n // tm

    def body(dqa_ref, dka_ref, dva_ref, dqb_ref, dkb_ref, dvb_ref, h_ref, qg_ref, kg_ref, cos_ref, sa_ref, sb_ref,
             dh_ref, dg_ref):
        i = pl.program_id(0)

        @pl.when(i == 0)
        def _():
            dg_ref[...] = jnp.zeros_like(dg_ref)

        cos, sa, sb = cos_ref[...], sa_ref[...], sb_ref[...]
        is_lat = i < nlat
        for hd in range(24):
            kind = _KINDS[hd]
            if hd < 8:
                dt = jnp.where(is_lat, dqa_ref[:, hd * HEAD:(hd + 1) * HEAD], 0.0)
            elif hd < 10:
                dt = dka_ref[:, (hd - 8) * HEAD:(hd - 7) * HEAD]
            elif hd < 12:
                dt = dva_ref[:, (hd - 10) * HEAD:(hd - 9) * HEAD]
            elif hd < 20:
                dt = jnp.where(is_lat, dqb_ref[:, (hd - 12) * HEAD:(hd - 11) * HEAD], 0.0)
            elif hd < 22:
                dt = dkb_ref[:, (hd - 20) * HEAD:(hd - 19) * HEAD]
            else:
                dt = dvb_ref[:, (hd - 22) * HEAD:(hd - 21) * HEAD]
            if kind != "none":
                dt = _rope_t(dt, cos, sa, sb)
            if kind in ("qnorm", "knorm"):
                g_ref = qg_ref if kind == "qnorm" else kg_ref
                r0 = 0 if kind == "qnorm" else 1
                xv = h_ref[:, (hd - NORM_HEAD0) * HEAD:(hd - NORM_HEAD0 + 1) * HEAD]
                xn = xv * lax.rsqrt(_rowmean(xv * xv) + EPS)
                dg_ref[r0:r0 + 1, :] += _colsum(dt * xn)
                dxn = dt * g_ref[...]
                dt = lax.rsqrt(_rowmean(xv * xv) + EPS) * (dxn - xn * _rowmean(dxn * xn))
            dh_ref[:, hd * HEAD:(hd + 1) * HEAD] = dt.astype(BF16)

    lat = lambda i: (jnp.minimum(i, nlat - 1), 0)
    row = lambda i: (i, 0)
    const2 = lambda i: (0, 0)
    return _pallas(
        body, name="qkv_bwd_prep", grid=(nlat + 1,),
        out_shape=(jax.ShapeDtypeStruct((na, wcols), BF16), jax.ShapeDtypeStruct((8, HEAD), F32)),
        in_specs=[pl.BlockSpec((tm, 8 * HEAD), lat), pl.BlockSpec((tm, 2 * HEAD), row), pl.BlockSpec((tm, 2 * HEAD), row),
                  pl.BlockSpec((tm, 8 * HEAD), lat), pl.BlockSpec((tm, 2 * HEAD), row), pl.BlockSpec((tm, 2 * HEAD), row),
                  pl.BlockSpec((tm, NORM_HEADS * HEAD), row),
                  pl.BlockSpec((1, HEAD), const2), pl.BlockSpec((1, HEAD), const2),
                  pl.BlockSpec((tm, HEAD), row), pl.BlockSpec((tm, HEAD), row), pl.BlockSpec((tm, HEAD), row)],
        out_specs=(pl.BlockSpec((tm, wcols), row), pl.BlockSpec((8, HEAD), const2)),
        compiler_params=_params(_mb(40)),
    )(dqa, dka, dva, dqb, dkb, dvb, h_norm, qg, kg, cos, sa, sb)


def _window_keys(k_ref, v_ref, n, na):
    i = pl.program_id(1)
    tq = WINDOW
    start = pl.multiple_of(jnp.clip((i - 1) * tq, 0, n - 3 * tq), tq)
    kk = jnp.concatenate([k_ref[pl.ds(start, 3 * tq), :], k_ref[n:na, :]], axis=0)
    vv = jnp.concatenate([v_ref[pl.ds(start, 3 * tq), :], v_ref[n:na, :]], axis=0)
    return kk, vv, start


def _window_bias():
    tq = WINDOW
    r = (jnp.arange(4 * tq) % tq)[:, None]
    c = jnp.arange(3 * tq + CTX)[None, :]
    variants = []
    for back in (0, tq, 2 * tq):
        seen = (jnp.abs(back + r - c) <= WINDOW) | (c >= 3 * tq)
        variants.append(jnp.where(seen, 0.0, NEG).astype(F32))
    return jnp.stack(variants)


def _window_bias_spec(nq):
    return pl.BlockSpec((1, 4 * WINDOW, 3 * WINDOW + CTX),
                        lambda kv, i: (jnp.where(i == 0, 0, jnp.where(i == nq - 1, 2, 1)), 0, 0))


def _stack_heads(ref, width=HEAD):
    return jnp.concatenate([ref[:, g * HEAD:g * HEAD + width] for g in range(4)], axis=0)


def _sink_column(sink_ref, kv, tq):
    grp = lax.broadcasted_iota(jnp.int32, (4 * tq, 1), 0) // tq
    col = jnp.zeros((4 * tq, 1), F32)
    for g in range(4):
        col = jnp.where(grp == g, sink_ref[0, 4 * kv + g] * LOG2E, col)
    return col


def _attn_window_fwd(t_all, sink, bias, after):
    na = t_all.shape[0]
    n = na - CTX
    tq = WINDOW

    def body(sink_ref, q_ref, k_ref, v_ref, bias_ref, after_ref, o_ref, lse_ref):
        kv = pl.program_id(0)
        kk, vv, _ = _window_keys(k_ref, v_ref, n, na)
        t = lax.dot_general(_stack_heads(q_ref), kk, _NT, preferred_element_type=F32) * QK_LOG2 + bias_ref[0]
        sk = _sink_column(sink_ref, kv, tq)
        m = jnp.maximum(jnp.max(t, axis=-1, keepdims=True), sk)
        p = jnp.exp2(t - m)
        l = jnp.sum(p, axis=-1, keepdims=True) + jnp.exp2(sk - m)
        o = jnp.dot(p.astype(BF16), vv, preferred_element_type=F32) * (1.0 / l)
        lse = m + jnp.log2(l)
        for g in range(4):
            o_ref[:, g * HEAD:(g + 1) * HEAD] = o[g * tq:(g + 1) * tq]
            lse_ref[:, g * HEAD:(g + 1) * HEAD] = jnp.broadcast_to(lse[g * tq:(g + 1) * tq], (tq, HEAD))

    blk = pl.BlockSpec((tq, 4 * HEAD), lambda kv, i: (i, kv))
    return _pallas(
        body, name="attn_window_fwd", grid=(2, n // tq),
        out_shape=(jax.ShapeDtypeStruct((n, 16 * HEAD), F32), jax.ShapeDtypeStruct((n, 8 * HEAD), F32)),
        in_specs=[pl.BlockSpec(memory_space=pltpu.SMEM), blk,
                  pl.BlockSpec((na, HEAD), lambda kv, i: (0, 8 + kv)),
                  pl.BlockSpec((na, HEAD), lambda kv, i: (0, 10 + kv)), _window_bias_spec(n // tq), _ANY],
        out_specs=(blk, blk),
        compiler_params=_params(_mb(32)),
    )(sink, t_all, t_all, t_all, bias, after)


def _attn_global_fwd(t_all, o_part):
    na = t_all.shape[0]
    n = na - CTX
    tq = 256

    def body(q_ref, k_ref, v_ref, o_in_ref, o_ref, p_ref, linv_ref):
        kk, vv = k_ref[...], v_ref[...]
        for g in range(4):
            q = q_ref[:, g * HEAD:(g + 1) * HEAD]
            t = lax.dot_general(q, kk, _NT, preferred_element_type=F32) * QK_LOG2
            m = jnp.max(t, axis=-1, keepdims=True)
            p = jnp.exp2(t - m)
            linv = 1.0 / jnp.sum(p, axis=-1, keepdims=True)
            pb = p.astype(BF16)
            p_ref[g] = pb
            o_ref[:, g * HEAD:(g + 1) * HEAD] = jnp.dot(pb, vv, preferred_element_type=F32) * linv
            linv_ref[:, g * HEAD:(g + 1) * HEAD] = jnp.broadcast_to(linv, (tq, HEAD))

    return _pallas(
        body, name="attn_global_fwd", grid=(2, n // tq),
        out_shape=(jax.ShapeDtypeStruct((n, 16 * HEAD), F32), jax.ShapeDtypeStruct((8, n, na), BF16),
                   jax.ShapeDtypeStruct((n, 8 * HEAD), F32)),
        in_specs=[pl.BlockSpec((tq, 4 * HEAD), lambda kv, i: (i, 3 + kv)),
                  pl.BlockSpec((na, HEAD), lambda kv, i: (0, 20 + kv)),
                  pl.BlockSpec((na, HEAD), lambda kv, i: (0, 22 + kv)), _ANY],
        out_specs=(pl.BlockSpec((tq, 4 * HEAD), lambda kv, i: (i, 2 + kv)),
                   pl.BlockSpec((4, tq, na), lambda kv, i: (kv, i, 0)),
                   pl.BlockSpec((tq, 4 * HEAD), lambda kv, i: (i, kv))),
        input_output_aliases={3: 0},
        compiler_params=_params(_mb(56)),
    )(t_all, t_all, t_all, o_part)


def _attn_window_bwd(t_all, o, do, lse, sink, bias):
    na = t_all.shape[0]
    n = na - CTX
    tq = WINDOW

    def body(sink_ref, q_ref, k_ref, v_ref, o_ref, do_ref, lse_ref, bias_ref, dq_ref, dk_ref, dv_ref, dsink_ref):
        kv = pl.program_id(0)

        @pl.when(pl.program_id(1) == 0)
        def _():
            dk_ref[...] = jnp.zeros_like(dk_ref)
            dv_ref[...] = jnp.zeros_like(dv_ref)
            dsink_ref[...] = jnp.zeros_like(dsink_ref)

        kk, vv, start = _window_keys(k_ref, v_ref, n, na)
        q = _stack_heads(q_ref)
        t = lax.dot_general(q, kk, _NT, preferred_element_type=F32) * QK_LOG2 + bias_ref[0]
        lse = _stack_heads(lse_ref, 1)
        p = jnp.exp2(t - lse)
        dof = _stack_heads(do_ref)
        delta = jnp.sum(dof * _stack_heads(o_ref), axis=-1, keepdims=True)
        dob = dof.astype(BF16)
        dv_acc = lax.dot_general(p.astype(BF16), dob, _TN, preferred_element_type=F32)
        dp = lax.dot_general(dob, vv, _NT, preferred_element_type=F32)
        ds = (p * (dp - delta) * SCALE).astype(BF16)
        dq = jnp.dot(ds, kk, preferred_element_type=F32)
        dk_acc = lax.dot_general(ds, q, _TN, preferred_element_type=F32)
        dsk = -(jnp.exp2(_sink_column(sink_ref, kv, tq) - lse) * delta)
        for g in range(4):
            dq_ref[:, g * HEAD:(g + 1) * HEAD] = dq[g * tq:(g + 1) * tq]
            dsink_ref[0, g:g + 1, :] += jnp.broadcast_to(_colsum(dsk[g * tq:(g + 1) * tq]), (1, HEAD))
        dk_ref[pl.ds(start, 3 * tq), :] += dk_acc[:3 * tq]
        dv_ref[pl.ds(start, 3 * tq), :] += dv_acc[:3 * tq]
        dk_ref[n:na, :] += dk_acc[3 * tq:]
        dv_ref[n:na, :] += dv_acc[3 * tq:]

    blk = pl.BlockSpec((tq, 4 * HEAD), lambda kv, i: (i, kv))
    kvout = pl.BlockSpec((na, HEAD), lambda kv, i: (0, kv))
    return _pallas(
        body, name="attn_window_bwd", grid=(2, n // tq),
        out_shape=(jax.ShapeDtypeStruct((n, 8 * HEAD), F32), jax.ShapeDtypeStruct((na, 2 * HEAD), F32),
                   jax.ShapeDtypeStruct((na, 2 * HEAD), F32), jax.ShapeDtypeStruct((2, 8, HEAD), F32)),
        in_specs=[pl.BlockSpec(memory_space=pltpu.SMEM), blk,
                  pl.BlockSpec((na, HEAD), lambda kv, i: (0, 8 + kv)),
                  pl.BlockSpec((na, HEAD), lambda kv, i: (0, 10 + kv)),
                  blk, blk, blk, _window_bias_spec(n // tq)],
        out_specs=(blk, kvout, kvout, pl.BlockSpec((1, 8, HEAD), lambda kv, i: (kv, 0, 0))),
        compiler_params=_params(_mb(40)),
    )(sink, t_all, t_all, t_all, o, do, lse, bias)


def _attn_global_bwd(t_all, kt, o, do, p_all, linv):
    na = t_all.shape[0]
    n = na - CTX
    tq = 256

    def body(q_ref, v_ref, kt_ref, o_ref, do_ref, p_ref, linv_ref, dq_ref, dk_ref, dv_ref, dkt_acc, dvt_acc):
        i = pl.program_id(1)

        @pl.when(i == 0)
        def _():
            dkt_acc[...] = jnp.zeros_like(dkt_acc)
            dvt_acc[...] = jnp.zeros_like(dvt_acc)

        vv, kt_v = v_ref[...], kt_ref[...]
        dkt = jnp.zeros((HEAD, na), F32)
        dvt = jnp.zeros((HEAD, na), F32)
        for g in range(4):
            q = q_ref[:, g * HEAD:(g + 1) * HEAD]
            p = p_ref[g].astype(F32) * linv_ref[:, g * HEAD:g * HEAD + 1]
            dof = do_ref[:, g * HEAD:(g + 1) * HEAD]
            delta = jnp.sum(dof * o_ref[:, g * HEAD:(g + 1) * HEAD], axis=-1, keepdims=True)
            dob = dof.astype(BF16)
            dvt = dvt + lax.dot_general(dob, p.astype(BF16), _TN, preferred_element_type=F32)
            dp = lax.dot_general(dob, vv, _NT, preferred_element_type=F32)
            ds = (p * (dp - delta) * SCALE).astype(BF16)
            dq_ref[:, g * HEAD:(g + 1) * HEAD] = lax.dot_general(kt_v, ds, _NT, preferred_element_type=F32).T
            dkt = dkt + lax.dot_general(q, ds, _TN, preferred_element_type=F32)
        dkt_acc[...] += dkt
        dvt_acc[...] += dvt

        @pl.when(i == pl.num_programs(1) - 1)
        def _():
            dk_ref[...] = dkt_acc[...].T
            dv_ref[...] = dvt_acc[...].T

    ospec = pl.BlockSpec((tq, 4 * HEAD), lambda kv, i: (i, 2 + kv))
    lspec = pl.BlockSpec((tq, 4 * HEAD), lambda kv, i: (i, kv))
    kvout = pl.BlockSpec((na, HEAD), lambda kv, i: (0, kv))
    return _pallas(
        body, name="attn_global_bwd", grid=(2, n // tq),
        out_shape=(jax.ShapeDtypeStruct((n, 8 * HEAD), F32), jax.ShapeDtypeStruct((na, 2 * HEAD), F32),
                   jax.ShapeDtypeStruct((na, 2 * HEAD), F32)),
        in_specs=[pl.BlockSpec((tq, 4 * HEAD), lambda kv, i: (i, 3 + kv)),
                  pl.BlockSpec((na, HEAD), lambda kv, i: (0, 22 + kv)),
                  pl.BlockSpec((HEAD, na), lambda kv, i: (kv, 0)),
                  ospec, ospec, pl.BlockSpec((4, tq, na), lambda kv, i: (kv, i, 0)), lspec],
        out_specs=(lspec, kvout, kvout),
        scratch_shapes=[pltpu.VMEM((HEAD, na), F32), pltpu.VMEM((HEAD, na), F32)],
        compiler_params=_params(_mb(56)),
    )(t_all, t_all, kt, o, do, p_all, linv)


def _outproj_ln1(o, wout, x, g1, lg, lb, sc2, sh2, after):
    n, d = x.shape
    tm = 256

    def body(o_ref, w_ref, x_ref, g1_ref, lg_ref, lb_ref, sc_ref, sh_ref, after_ref, a_ref, xh_ref, rs_ref, u_ref):
        a1 = jnp.dot(o_ref[...].astype(BF16), w_ref[...], preferred_element_type=F32)
        a_ref[...] = a1.astype(BF16)
        r = ALPHA * x_ref[...] + g1_ref[...] * a1
        dlt = r - _rowmean(r)
        rstd = lax.rsqrt(_rowmean(dlt * dlt) + EPS)
        xh = dlt * rstd
        xh_ref[...] = xh
        rs_ref[...] = rstd
        x1 = xh * lg_ref[...] + lb_ref[...]
        u_ref[...] = (x1 * (1.0 + sc_ref[...]) + sh_ref[...]).astype(BF16)

    row = lambda i: (i, 0)
    const2 = lambda i: (0, 0)
    vec = pl.BlockSpec((1, d), const2)
    big = pl.BlockSpec((tm, d), row)
    return _pallas(
        body, name="outproj_ln1", grid=(n // tm,),
        out_shape=(jax.ShapeDtypeStruct((n, d), BF16), jax.ShapeDtypeStruct((n, d), F32),
                   jax.ShapeDtypeStruct((n, 1), F32), jax.ShapeDtypeStruct((n, d), BF16)),
        in_specs=[big, pl.BlockSpec((d, d), const2), big, vec, vec, vec, vec, vec, _ANY],
        out_specs=(big, big, pl.BlockSpec((tm, 1), row), big),
        compiler_params=_params(_mb(56)),
    )(o, wout, x, g1, lg, lb, sc2, sh2, after)


def _ffn_up(u2, wgt, wut, after):
    n, d = u2.shape
    f = wgt.shape[0]
    tm = min(1024, n)

    def body(u_ref, wg_ref, wu_ref, after_ref, sa_ref, sb_ref, hf_ref):
        u = u_ref[...]
        gv = lax.dot_general(u, wg_ref[...], _NT, preferred_element_type=F32)
        pv = lax.dot_general(u, wu_ref[...], _NT, preferred_element_type=F32)
        sg = _sigmoid(gv)
        silu = gv * sg
        sa_ref[...] = silu.astype(BF16)
        sb_ref[...] = (pv * (sg * (1.0 + gv * (1.0 - sg)))).astype(BF16)
        hf_ref[...] = (silu * pv).astype(BF16)

    tile = pl.BlockSpec((tm, FFN_TILE), lambda i, j: (i, j))
    wspec = pl.BlockSpec((FFN_TILE, d), lambda i, j: (j, 0))
    sds = jax.ShapeDtypeStruct((n, f), BF16)
    return _pallas(
        body, name="ffn_up", grid=(n // tm, f // FFN_TILE),
        out_shape=(sds, sds, sds),
        in_specs=[pl.BlockSpec((tm, d), lambda i, j: (i, 0)), wspec, wspec, _ANY],
        out_specs=(tile, tile, tile),
        compiler_params=_params(_mb(48)),
    )(u2, wgt, wut, after)


def _ffn_down(hf, wd):
    n, f = hf.shape
    d = wd.shape[1]
    tm, tn = min(1024, n), 512

    def body(h_ref, w_ref, o_ref):
        o_ref[...] = jnp.dot(h_ref[...], w_ref[...], preferred_element_type=F32)

    return _pallas(
        body, name="ffn_down", grid=(n // tm, d // tn),
        out_shape=jax.ShapeDtypeStruct((n, d), F32),
        in_specs=[pl.BlockSpec((tm, f), lambda i, j: (i, 0)), pl.BlockSpec((f, tn), lambda i, j: (0, j))],
        out_specs=pl.BlockSpec((tm, tn), lambda i, j: (i, j)),
        compiler_params=_params(_mb(56)),
    )(hf, wd)


def _ln2_loss(xh1, ffn, tgt, lg1, lb1, g2, lg2, lb2):
    n, d = xh1.shape
    tm = 256

    def body(xh_ref, f_ref, t_ref, lg1_ref, lb1_ref, g2_ref, lg2_ref, lb2_ref, dr_ref, df_ref, loss_ref, acc_ref):
        @pl.when(pl.program_id(0) == 0)
        def _():
            loss_ref[...] = jnp.zeros_like(loss_ref)
            acc_ref[...] = jnp.zeros_like(acc_ref)

        x1 = xh_ref[...] * lg1_ref[...] + lb1_ref[...]
        fv = f_ref[...]
        r = ALPHA * x1 + g2_ref[...] * fv
        dlt = r - _rowmean(r)
        rstd = lax.rsqrt(_rowmean(dlt * dlt) + EPS)
        xh2 = dlt * rstd
        err = xh2 * lg2_ref[...] + lb2_ref[...] - t_ref[...]
        loss_ref[...] += 0.5 * jnp.sum(_rowmean(err * err))
        dy = err * (1.0 / d)
        dyg = dy * lg2_ref[...]
        dr = rstd * (dyg - _rowmean(dyg) - xh2 * _rowmean(dyg * xh2))
        dr_ref[...] = dr
        df_ref[...] = (g2_ref[...] * dr).astype(BF16)
        acc_ref[0:1, :] += _colsum(dy * xh2)
        acc_ref[1:2, :] += _colsum(dy)
        acc_ref[2:3, :] += _colsum(dr * fv)

    row = lambda i: (i, 0)
    const2 = lambda i: (0, 0)
    vec = pl.BlockSpec((1, d), const2)
    big = pl.BlockSpec((tm, d), row)
    return _pallas(
        body, name="ln2_loss", grid=(n // tm,),
        out_shape=(jax.ShapeDtypeStruct((n, d), F32), jax.ShapeDtypeStruct((n, d), BF16),
                   jax.ShapeDtypeStruct((8, HEAD), F32), jax.ShapeDtypeStruct((8, d), F32)),
        in_specs=[big, big, big, vec, vec, vec, vec, vec],
        out_specs=(big, big, pl.BlockSpec((8, HEAD), const2), pl.BlockSpec((8, d), const2)),
        compiler_params=_params(_mb(48)),
    )(xh1, ffn, tgt, lg1, lb1, g2, lg2, lb2)


def _ffn_dhf(df, wd, sa, sb):
    n, d = df.shape
    f = sa.shape[1]
    tm = min(2048, n)

    def body(df_ref, w_ref, sa_ref, sb_ref, dgp_ref):
        dhf = lax.dot_general(df_ref[...], w_ref[...], _NT, preferred_element_type=F32)
        dgp_ref[:, :FFN_TILE] = (dhf * sb_ref[...].astype(F32)).astype(BF16)
        dgp_ref[:, FFN_TILE:] = (dhf * sa_ref[...].astype(F32)).astype(BF16)

    tile = pl.BlockSpec((tm, FFN_TILE), lambda i, j: (i, j))
    return _pallas(
        body, name="ffn_dhf", grid=(n // tm, f // FFN_TILE),
        out_shape=jax.ShapeDtypeStruct((n, 2 * f), BF16),
        in_specs=[pl.BlockSpec((tm, d), lambda i, j: (i, 0)), pl.BlockSpec((FFN_TILE, d), lambda i, j: (j, 0)),
                  tile, tile],
        out_specs=pl.BlockSpec((tm, 2 * FFN_TILE), lambda i, j: (i, j)),
        compiler_params=_params(_mb(48)),
    )(df, wd, sa, sb)


def _ffn_du2(dgp, wgt, wut, after):
    n = dgp.shape[0]
    f, d = wgt.shape
    tm = min(1024, n)

    def body(dgp_ref, wg_ref, wu_ref, after_ref, o_ref):
        w = jnp.concatenate([wg_ref[...], wu_ref[...]], axis=0)
        part = jnp.dot(dgp_ref[...], w, preferred_element_type=F32)

        @pl.when(pl.program_id(1) == 0)
        def _():
            o_ref[...] = part

        @pl.when(pl.program_id(1) > 0)
        def _():
            o_ref[...] += part

    wspec = pl.BlockSpec((FFN_TILE, d), lambda i, j: (j, 0))
    return _pallas(
        body, name="ffn_du2", grid=(n // tm, f // FFN_TILE),
        out_shape=jax.ShapeDtypeStruct((n, d), F32),
        in_specs=[pl.BlockSpec((tm, 2 * FFN_TILE), lambda i, j: (i, j)), wspec, wspec, _ANY],
        out_specs=pl.BlockSpec((tm, d), lambda i, j: (i, 0)),
        compiler_params=_params(_mb(48)),
    )(dgp, wgt, wut, after)


def _dw_gate_up(dgp, u2, after):
    n, d = u2.shape
    f = dgp.shape[1] // 2
    tm = min(2048, n)

    def body(a_ref, b_ref, after_ref, og_ref, ou_ref, acc_ref):
        part = lax.dot_general(a_ref[...], b_ref[...], _TN, preferred_element_type=F32)
        i = pl.program_id(1)

        @pl.when(i == 0)
        def _():
            acc_ref[...] = part

        @pl.when(i > 0)
        def _():
            acc_ref[...] += part

        @pl.when(i == pl.num_programs(1) - 1)
        def _():
            og_ref[...] = acc_ref[:FFN_TILE].astype(BF16)
            ou_ref[...] = acc_ref[FFN_TILE:].astype(BF16)

    out = pl.BlockSpec((FFN_TILE, d), lambda j, i: (j, 0))
    sds = jax.ShapeDtypeStruct((f, d), BF16)
    return _pallas(
        body, name="dw_gate_up", grid=(f // FFN_TILE, n // tm),
        out_shape=(sds, sds),
        in_specs=[pl.BlockSpec((tm, 2 * FFN_TILE), lambda j, i: (i, j)), pl.BlockSpec((tm, d), lambda j, i: (i, 0)), _ANY],
        out_specs=(out, out),
        scratch_shapes=[pltpu.VMEM((2 * FFN_TILE, d), F32)],
        compiler_params=_params(_mb(56)),
    )(dgp, u2, after)


def _ln1_bwd(du2, dr2, xh1, rs1, a1, lg1, lb1, sc2, g1):
    n, d = du2.shape
    tm = 256

    def body(du_ref, dr2_ref, xh_ref, rs_ref, a_ref, lg_ref, lb_ref, sc_ref, g1_ref, dr1_ref, da_ref, acc_ref):
        @pl.when(pl.program_id(0) == 0)
        def _():
            acc_ref[...] = jnp.zeros_like(acc_ref)

        du = du_ref[...]
        xh = xh_ref[...]
        x1 = xh * lg_ref[...] + lb_ref[...]
        dx1 = ALPHA * dr2_ref[...] + du * (1.0 + sc_ref[...])
        dxg = dx1 * lg_ref[...]
        dr1 = rs_ref[...] * (dxg - _rowmean(dxg) - xh * _rowmean(dxg * xh))
        dr1_ref[...] = dr1
        da_ref[...] = (g1_ref[...] * dr1).astype(BF16)
        acc_ref[0:1, :] += _colsum(du * x1)
        acc_ref[1:2, :] += _colsum(du)
        acc_ref[2:3, :] += _colsum(dx1 * xh)
        acc_ref[3:4, :] += _colsum(dx1)
        acc_ref[4:5, :] += _colsum(dr1 * a_ref[...].astype(F32))

    row = lambda i: (i, 0)
    const2 = lambda i: (0, 0)
    vec = pl.BlockSpec((1, d), const2)
    big = pl.BlockSpec((tm, d), row)
    return _pallas(
        body, name="ln1_bwd", grid=(n // tm,),
        out_shape=(jax.ShapeDtypeStruct((n, d), F32), jax.ShapeDtypeStruct((n, d), BF16),
                   jax.ShapeDtypeStruct((8, d), F32)),
        in_specs=[big, big, big, pl.BlockSpec((tm, 1), row), big, vec, vec, vec, vec],
        out_specs=(big, big, pl.BlockSpec((8, d), const2)),
        compiler_params=_params(_mb(48)),
    )(du2, dr2, xh1, rs1, a1, lg1, lb1, sc2, g1)


def _dw_rows(a, b, nblk, bw, tm, after, name):
    m = a.shape[0]
    nn = b.shape[1]

    def body(a_ref, b_ref, after_ref, o_ref, acc_ref):
        part = lax.dot_general(a_ref[...].astype(BF16), b_ref[...], _TN, preferred_element_type=F32)
        i = pl.program_id(1)

        @pl.when(i == 0)
        def _():
            acc_ref[...] = part

        @pl.when(i > 0)
        def _():
            acc_ref[...] += part

        @pl.when(i == pl.num_programs(1) - 1)
        def _():
            o_ref[0] = acc_ref[...].astype(BF16)

    return _pallas(
        body, name=name, grid=(nblk, m // tm),
        out_shape=jax.ShapeDtypeStruct((nblk, bw, nn), BF16),
        in_specs=[pl.BlockSpec((tm, bw), lambda j, i: (i, j)), pl.BlockSpec((tm, nn), lambda j, i: (i, 0)), _ANY],
        out_specs=pl.BlockSpec((1, bw, nn), lambda j, i: (j, 0, 0)),
        scratch_shapes=[pltpu.VMEM((bw, nn), F32)],
        compiler_params=_params(_mb(56)),
    )(a, b, after)


def _outproj_bwd(da1, wout, after):
    n, d = da1.shape
    tm = 512

    def body(a_ref, w_ref, after_ref, o_ref):
        o_ref[...] = lax.dot_general(a_ref[...], w_ref[...], _NT, preferred_element_type=F32)

    return _pallas(
        body, name="outproj_bwd", grid=(n // tm,),
        out_shape=jax.ShapeDtypeStruct((n, d), F32),
        in_specs=[pl.BlockSpec((tm, d), lambda i: (i, 0)), pl.BlockSpec((d, d), lambda i: (0, 0)), _ANY],
        out_specs=pl.BlockSpec((tm, d), lambda i: (i, 0)),
        compiler_params=_params(_mb(48)),
    )(da1, wout, after)


def _qkv_bwd(dh, wint, x, ct, dr1, sc):
    na, wcols = dh.shape
    n, d = x.shape
    tm = CTX
    nlat = n // tm

    def body(dh_ref, w_ref, x_ref, ct_ref, dr_ref, sc_ref, gx_ref, acc_ref):
        i = pl.program_id(0)

        @pl.when(i == 0)
        def _():
            acc_ref[...] = jnp.zeros_like(acc_ref)

        du = jnp.dot(dh_ref[...], w_ref[...], preferred_element_type=F32)

        @pl.when(i < nlat)
        def _():
            gx_ref[...] = ALPHA * dr_ref[...] + du * (1.0 + sc_ref[0])
            acc_ref[0:1, :] += _colsum(du)
            acc_ref[1:2, :] += _colsum(du * x_ref[...])

        @pl.when(i == nlat)
        def _():
            acc_ref[2:3, :] += _colsum(du)
            acc_ref[3:4, :] += _colsum(du * ct_ref[...])

    lat = lambda i: (jnp.minimum(i, nlat - 1), 0)
    const2 = lambda i: (0, 0)
    return _pallas(
        body, name="qkv_bwd", grid=(nlat + 1,),
        out_shape=(jax.ShapeDtypeStruct((n, d), F32), jax.ShapeDtypeStruct((8, d), F32)),
        in_specs=[pl.BlockSpec((tm, wcols), lambda i: (i, 0)), pl.BlockSpec((wcols, d), const2),
                  pl.BlockSpec((tm, d), lat), pl.BlockSpec((tm, d), const2), pl.BlockSpec((tm, d), lat),
                  pl.BlockSpec((1, 1, d), lambda i: (0, 0, 0))],
        out_specs=(pl.BlockSpec((tm, d), lat), pl.BlockSpec((8, d), const2)),
        compiler_params=_params(_mb(56)),
    )(dh, wint, x, ct, dr1, sc)


def _adam_math(w, g, m, v):
    m2 = ADAM_B1 * m + (1.0 - ADAM_B1) * g
    v2 = ADAM_B2 * v + (1.0 - ADAM_B2) * (g * g)
    m_hat = m2 * (1.0 / (1.0 - ADAM_B1 ** ADAM_STEP))
    v_hat = v2 * (1.0 / (1.0 - ADAM_B2 ** ADAM_STEP))
    delta = -ADAM_LR * (m_hat / (jnp.sqrt(v_hat) + ADAM_EPS) + ADAM_WD * w)
    return delta, m2, v2


def _adamw(w, gsrc, m, v, name, after=None):
    r, c = w.shape
    parts = gsrc.ndim == 3
    after = w if after is None else after
    tr = r
    while tr * c * 4 > _mb(1) and tr % 32 == 0:
        tr //= 2

    def body(w_ref, g_ref, m_ref, v_ref, after_ref, go_ref, d_ref, mo_ref, vo_ref):
        if parts:
            g = g_ref[0].astype(F32)
            for s in range(1, NDEV):
                g = g + g_ref[s].astype(F32)
        else:
            g = g_ref[...]
        delta, m2, v2 = _adam_math(w_ref[...], g, m_ref[...], v_ref[...])
        go_ref[...] = g
        d_ref[...] = delta
        mo_ref[...] = m2
        vo_ref[...] = v2

    tile = pl.BlockSpec((tr, c), lambda i: (i, 0))
    gspec = pl.BlockSpec((NDEV, tr, c), lambda i: (0, i, 0)) if parts else tile
    sds = jax.ShapeDtypeStruct((r, c), F32)
    return _pallas(
        body, name=name, grid=(r // tr,),
        out_shape=(sds, sds, sds, sds),
        in_specs=[tile, gspec, tile, tile, _ANY],
        out_specs=(tile, tile, tile, tile),
        compiler_params=_params(_mb(48)),
    )(w, gsrc, m, v, after)


def _adamw_t(w, gsrc_t, m, v, name):
    r, c = w.shape
    tr = 256

    def body(w_ref, g_ref, m_ref, v_ref, go_ref, d_ref, mo_ref, vo_ref):
        gt = g_ref[0].astype(F32)
        for s in range(1, NDEV):
            gt = gt + g_ref[s].astype(F32)
        g = gt.T
        delta, m2, v2 = _adam_math(w_ref[...], g, m_ref[...], v_ref[...])
        go_ref[...] = g
        d_ref[...] = delta
        mo_ref[...] = m2
        vo_ref[...] = v2

    tile = pl.BlockSpec((tr, c), lambda i: (i, 0))
    sds = jax.ShapeDtypeStruct((r, c), F32)
    return _pallas(
        body, name=name, grid=(r // tr,),
        out_shape=(sds, sds, sds, sds),
        in_specs=[tile, pl.BlockSpec((NDEV, c, tr), lambda i: (0, 0, i)), tile, tile],
        out_specs=(tile, tile, tile, tile),
        compiler_params=_params(_mb(48)),
    )(w, gsrc_t, m, v)


def _small_update(gath, dcc, cc, w_s, m_s, v_s):
    d = w_s.shape[1]

    def body(g_ref, dcc_ref, cc_ref, w_ref, m_ref, v_ref, go_ref, d_ref, mo_ref, vo_ref):
        s = g_ref[0]
        for b in range(1, NDEV):
            s = s + g_ref[b]
        dsl = dcc_ref[0, 8:9, :]
        for b in range(1, NDEV):
            dsl = dsl + dcc_ref[b, 8:9, :]
        cv = cc_ref[...]
        sg = _sigmoid(cv)
        go_ref[...] = jnp.zeros_like(go_ref)
        go_ref[0:1, :] = dsl * (sg * (1.0 + cv * (1.0 - sg)))
        go_ref[1:3, :] = s[0:2] + s[6:8]
        go_ref[3:7, :] = s[2:6]
        go_ref[7:12, :] = s[8:13]
        delta, m2, v2 = _adam_math(w_ref[...], go_ref[...], m_ref[...], v_ref[...])
        d_ref[...] = delta
        mo_ref[...] = m2
        vo_ref[...] = v2

    full = pl.BlockSpec((16, d), lambda: (0, 0))
    g3 = pl.BlockSpec((NDEV, 16, d), lambda: (0, 0, 0))
    sds = jax.ShapeDtypeStruct((16, d), F32)
    return _pallas(
        body, name="small_update",
        out_shape=(sds, sds, sds, sds),
        in_specs=[g3, g3, pl.BlockSpec((1, d), lambda: (0, 0)), full, full, full],
        out_specs=(full, full, full, full),
        compiler_params=_params(_mb(24)),
    )(gath, dcc, cc, w_s, m_s, v_s)


def _rope_tables(n):
    rows = n // GRID_W
    row_ids = jnp.repeat(jnp.arange(rows, dtype=F32), GRID_W)
    col_ids = jnp.tile(jnp.arange(GRID_W, dtype=F32), rows)
    axis_dim = HEAD // 2
    inv_freq = jnp.power(ROPE_THETA, -jnp.arange(0, axis_dim, 2, dtype=F32) / axis_dim)
    ang_r = row_ids[:, None] * inv_freq
    ang_c = col_ids[:, None] * inv_freq
    ang = jnp.concatenate([ang_r, ang_r, ang_c, ang_c], axis=-1)
    cos, sin = jnp.cos(ang), jnp.sin(ang)
    first = (jnp.arange(HEAD) % (HEAD // 2)) < HEAD // 4
    sa = jnp.where(first, -sin, 0.0)
    sb = jnp.where(first, 0.0, sin)
    ones = jnp.ones((CTX, HEAD), F32)
    zeros = jnp.zeros((CTX, HEAD), F32)
    return (jnp.concatenate([cos, ones], 0), jnp.concatenate([sa, zeros], 0), jnp.concatenate([sb, zeros], 0))


def _pad_cols(a, width):
    return jnp.pad(a, ((0, 0), (0, width - a.shape[1])))


def _pad_rows(a, rows):
    return jnp.pad(a, ((0, rows - a.shape[0]), (0, 0)))


def _pack_small(c_ctx, b_ada, ln1_g, ln1_b, ln2_g, ln2_b, qg, kg, sink, d):
    misc = _pad_cols(jnp.concatenate([qg, kg, sink], axis=1), d)
    rows = jnp.concatenate([c_ctx.reshape(1, d), b_ada.reshape(6, d), ln1_g, ln1_b, ln2_g, ln2_b, misc], axis=0)
    return _pad_rows(rows, 16)


def _unpack_small(p, d):
    return dict(c_ctx=p[0], b_ada=p[1:7].reshape(1, 6 * d), ln1_g=p[7:8], ln1_b=p[8:9], ln2_g=p[9:10], ln2_b=p[10:11],
                q_norm_g=p[11:12, 0:HEAD], k_norm_g=p[11:12, HEAD:2 * HEAD], sink_logit=p[11:12, 2 * HEAD:2 * HEAD + 8])


def kernel(x, c, ctx, c_ctx, w_ada, b_ada, w_in, q_norm_g, k_norm_g, sink_logit, w_out, ln1_g, ln1_b, w_gate, w_up, w_down, ln2_g, ln2_b, loss_target, m_c_ctx, m_w_ada, m_b_ada, m_w_in, m_q_norm_g, m_k_norm_g, m_sink_logit, m_w_out, m_ln1_g, m_ln1_b, m_w_gate, m_w_up, m_w_down, m_ln2_g, m_ln2_b, v_c_ctx, v_w_ada, v_b_ada, v_w_in, v_q_norm_g, v_k_norm_g, v_sink_logit, v_w_out, v_ln1_g, v_ln1_b, v_w_gate, v_w_up, v_w_down, v_ln2_g, v_ln2_b):
    xs, cts, tgt = x[0], ctx[0], loss_target[0]
    n, d = xs.shape
    assert cts.shape == (CTX, d) and w_in.shape[2] == IN_SHARD and w_gate.shape[2] == FFN_SHARD
    me = 4 * lax.axis_index("x") + 2 * lax.axis_index("y") + lax.axis_index("c")
    e_sh = w_ada.shape[2]

    c_g = _exchange(_pad_rows(c, 8), False, "gather_c")
    c_all = jnp.concatenate([c_g[:, 0, :], _pad_rows(c_ctx.reshape(1, d), 8)], axis=0)
    bias_sh = lax.dynamic_slice(b_ada, (0, me * e_sh), (1, e_sh))
    mods_g = _exchange(_ada_fwd(c_all, w_ada[0], bias_sh), False, "gather_mods")
    mods = jnp.transpose(mods_g, (1, 0, 2)).reshape(16, NDEV * e_sh)
    mine = lax.dynamic_slice(mods, (me, 0), (1, 6 * d))
    sh1, sc1, g1, sh2, sc2, g2 = [mine[:, k * d:(k + 1) * d] for k in range(6)]
    csh1, csc1 = mods[8:9, 0:d], mods[8:9, d:2 * d]
    sc_pair = jnp.stack([sc1, csc1])
    sh_pair = jnp.stack([sh1, csh1])

    h_win, tok = _exchange_start(w_in[0].T.astype(BF16), "chip", mods, "gather_w_in_start")
    tok, (wo_l, wg_l, wu_l, wd_l) = lax.optimization_barrier((tok, (w_out, w_gate, w_up, w_down)))
    h_wout, tok = _exchange_start(wo_l[0].astype(BF16), "chip", tok, "gather_w_out_start")
    h_wg, tok = _exchange_start(wg_l[0].T.astype(BF16), "chip", tok, "gather_w_gate_start")
    h_wu, tok = _exchange_start(wu_l[0].T.astype(BF16), "chip", tok, "gather_w_up_start")
    h_wd, tok = _exchange_start(wd_l[0].astype(BF16), "chip", tok, "gather_w_down_start")

    cos, sa, sb = _rope_tables(n)
    f_win, tok = _forward_start(_exchange_wait(h_win, "chip", tok, "gather_w_in_wait"), tok, "forward_w_in_start")
    win_g = _forward_wait(f_win, tok, "forward_w_in_wait").reshape(NDEV * IN_SHARD, d)
    u_all, h_all, t_all, kt_b = _qkv_fwd(xs, cts, sc_pair, sh_pair, win_g, q_norm_g, k_norm_g, cos, sa, sb)
    f_wout, tok = _forward_start(_exchange_wait(h_wout, "chip", t_all, "gather_w_out_wait"), t_all, "forward_w_out_start")
    win_bias = _window_bias()
    o_a, lse_a = _attn_window_fwd(t_all, sink_logit, win_bias, tok)
    o, p_b, linv_b = _attn_global_fwd(t_all, o_a)
    f_wg, tok = _forward_start(_exchange_wait(h_wg, "chip", o, "gather_w_gate_wait"), o, "forward_w_gate_start")
    f_wu, tok = _forward_start(_exchange_wait(h_wu, "chip", tok, "gather_w_up_wait"), tok, "forward_w_up_start")
    wout_g = _forward_wait(f_wout, tok, "forward_w_out_wait").reshape(d, d)
    a1, xh1, rs1, u2 = _outproj_ln1(o, wout_g, xs, g1, ln1_g, ln1_b, sc2, sh2, tok)
    f_wd, tok = _forward_start(_exchange_wait(h_wd, "chip", rs1, "gather_w_down_wait"), rs1, "forward_w_down_start")
    ffn_w = (NDEV * FFN_SHARD, d)
    wg_g = _forward_wait(f_wg, tok, "forward_w_gate_wait").reshape(ffn_w)
    wu_g = _forward_wait(f_wu, tok, "forward_w_up_wait").reshape(ffn_w)
    sa_f, sb_f, hf = _ffn_up(u2, wg_g, wu_g, tok)
    wd_g = _forward_wait(f_wd, hf, "forward_w_down_wait").reshape(ffn_w)
    ffn = _ffn_down(hf, wd_g)
    dr2, df, loss_p, acc2 = _ln2_loss(xh1, ffn, tgt, ln1_g, ln1_b, g2, ln2_g, ln2_b)
    loss = lax.psum(loss_p[0, 0], ("x", "y", "c"))

    parts = (NDEV, FFN_SHARD, d)
    dgp = _ffn_dhf(df, wd_g, sa_f, sb_f)
    dwd_p = _dw_rows(hf, df, hf.shape[1] // FFN_TILE, FFN_TILE, min(n, 2048), loss_p, "dw_down").reshape(parts)
    h_dwd, tok = _exchange_start(dwd_p, "scatter", loss.reshape(1, 1), "scatter_dw_down_start")
    dwg_t, dwu_t = _dw_gate_up(dgp, u2, tok)
    h_dwg, tok = _exchange_start(dwg_t.reshape(parts), "scatter", tok, "scatter_dw_gate_start")
    h_dwu, tok = _exchange_start(dwu_t.reshape(parts), "scatter", tok, "scatter_dw_up_start")
    du2 = _ffn_du2(dgp, wg_g, wu_g, tok)
    dr1, da1, acc1 = _ln1_bwd(du2, dr2, xh1, rs1, a1, ln1_g, ln1_b, sc2, g1)
    dwo_p = _dw_rows(o, da1, 2, 8 * HEAD, min(n, 1024), loss_p, "dw_out").reshape(NDEV, 2 * HEAD, d)
    h_dwo, tok = _exchange_start(dwo_p, "scatter", loss_p, "scatter_dw_out_start")
    do = _outproj_bwd(da1, wout_g, tok)
    dqa, dka, dva, dsink = _attn_window_bwd(t_all, o, do, lse_a, sink_logit, win_bias)
    dqb, dkb, dvb = _attn_global_bwd(t_all, kt_b, o, do, p_b, linv_b)
    dh_all, dnorm = _qkv_bwd_prep(dqa, dka, dva, dqb, dkb, dvb, h_all, q_norm_g, k_norm_g, cos, sa, sb)
    grad_x, acc0 = _qkv_bwd(dh_all, win_g, xs, cts, dr1, sc_pair)

    misc = _pad_cols(jnp.concatenate([dnorm[0:1], dnorm[1:2], dsink[:, 0:4, 0].reshape(1, 8)], axis=1), d)
    part = jnp.concatenate([
        acc0[0:2], acc1[4:5], acc1[1:2], acc1[0:1], acc2[2:3],
        acc0[2:4],
        acc1[2:4], acc2[0:2],
        misc, jnp.zeros((3, d), F32)], axis=0)
    gath = _exchange(part, False, "gather_small")
    dm_batch = gath[:, 0:6, :].reshape(NDEV, 6 * d)
    dm_ctx = _pad_cols(gath[:, 6:8, :].reshape(NDEV, 2 * d), 6 * d)
    dm16 = lax.dynamic_slice(jnp.concatenate([dm_batch, dm_ctx], axis=0), (0, me * e_sh), (16, e_sh))
    dw_ada, drow = _ada_bwd(dm16, c_all, w_ada[0])
    dcc = _exchange(drow, False, "gather_dcc")
    dwi_p = _dw_rows(dh_all, u_all, NDEV // 2, 2 * IN_SHARD, (n + CTX) // 2, dcc, "dw_in")
    dwi_p = dwi_p.reshape(NDEV, IN_SHARD, d)
    h_dwi, tok = _exchange_start(dwi_p, "scatter", dcc, "scatter_dw_in_start")

    w_s = _pack_small(c_ctx, b_ada, ln1_g, ln1_b, ln2_g, ln2_b, q_norm_g, k_norm_g, sink_logit, d)
    m_s = _pack_small(m_c_ctx, m_b_ada, m_ln1_g, m_ln1_b, m_ln2_g, m_ln2_b, m_q_norm_g, m_k_norm_g, m_sink_logit, d)
    v_s = _pack_small(v_c_ctx, v_b_ada, v_ln1_g, v_ln1_b, v_ln2_g, v_ln2_b, v_q_norm_g, v_k_norm_g, v_sink_logit, d)
    small = [_unpack_small(p, d) for p in _small_update(gath, dcc, c_ctx.reshape(1, d), w_s, m_s, v_s)]

    big = {}
    big["w_ada"] = _adamw(w_ada[0], dw_ada, m_w_ada[0], v_w_ada[0], "adamw_w_ada", after=tok)
    big["w_down"] = _adamw(w_down[0], _exchange_wait(h_dwd, "scatter", big["w_ada"][1], "scatter_dw_down_wait"),
                           m_w_down[0], v_w_down[0], "adamw_w_down")
    late = big["w_down"][1]
    for nm, wt, mt, vt, hd in (("w_gate", w_gate, m_w_gate, v_w_gate, h_dwg), ("w_up", w_up, m_w_up, v_w_up, h_dwu)):
        res = _adamw(wt[0].T, _exchange_wait(hd, "scatter", late, "scatter_d" + nm + "_wait"), mt[0].T, vt[0].T,
                     "adamw_" + nm)
        big[nm] = [r.T for r in res]
        late = res[1]
    big["w_out"] = _adamw(w_out[0], _exchange_wait(h_dwo, "scatter", late, "scatter_dw_out_wait"), m_w_out[0], v_w_out[0],
                          "adamw_w_out")
    big["w_in"] = _adamw_t(w_in[0], _exchange_wait(h_dwi, "scatter", big["w_out"][1], "scatter_dw_in_wait"), m_w_in[0],
                           v_w_in[0], "adamw_w_in")

    names = ["c_ctx", "w_ada", "b_ada", "w_in", "q_norm_g", "k_norm_g", "sink_logit", "w_out", "ln1_g", "ln1_b",
             "w_gate", "w_up", "w_down", "ln2_g", "ln2_b"]
    outs = [loss, grad_x[None]]
    for k in range(4):
        for nm in names:
            outs.append(big[nm][k][None] if nm in big else small[k][nm])
    return tuple(outs)
```

```python
import functools

import jax
import jax.numpy as jnp
from jax import lax
from jax.experimental import pallas as pl
from jax.experimental.pallas import tpu as pltpu

F32 = jnp.float32
BF16 = jnp.bfloat16

NDEV = 8
HEAD = 128
CTX = 256
GRID_W = 64
WINDOW = 128
ROPE_THETA = 10000.0
EPS = 1e-6
SCALE = HEAD ** -0.5
LOG2E = 1.4426950408889634
QK_LOG2 = SCALE * LOG2E
ALPHA = 2.0 ** 0.25
FFN_SHARD = 704
FFN_TILE = 512
IN_SHARD = 384
NEG = -1e30

ADAM_LR = 0.001
ADAM_B1 = 0.9
ADAM_B2 = 0.999
ADAM_EPS = 1e-08
ADAM_WD = 0.01
ADAM_STEP = 10

VMEM_CAP = 56 * 1024 * 1024

_KINDS = ["rope"] * 10 + ["none"] * 2 + ["qnorm"] * 8 + ["knorm"] * 2 + ["none"] * 2
NORM_HEAD0 = _KINDS.index("qnorm")
NORM_HEADS = _KINDS.count("qnorm") + _KINDS.count("knorm")

_NT = (((1,), (1,)), ((), ()))
_TN = (((0,), (0,)), ((), ()))


def _pallas(body, **kw):
    return pl.pallas_call(body, **kw)


def _params(vmem_bytes):
    return pltpu.CompilerParams(vmem_limit_bytes=int(min(VMEM_CAP, vmem_bytes)))


def _mb(n):
    return int(n * 1024 * 1024)


def _sigmoid(x):
    return 1.0 / (1.0 + jnp.exp(-x))


def _colsum(a):
    return jnp.sum(a, axis=0, keepdims=True)


def _rowmean(a):
    return jnp.mean(a, axis=-1, keepdims=True)


def _exchange(src, scatter, name, after=None):
    blk = src.shape[1:] if scatter else src.shape
    after = src if after is None else after

    def body(src_ref, after_ref, out_ref, send_sems, recv_sems, local_sem):
        x, y, c = lax.axis_index("x"), lax.axis_index("y"), lax.axis_index("c")
        me = 4 * x + 2 * y + c
        copies = []
        for t in range(1, NDEV):
            px = 1 - x if (t >> 2) & 1 else x
            py = 1 - y if (t >> 1) & 1 else y
            pc = 1 - c if t & 1 else c
            peer = 4 * px + 2 * py + pc
            cp = pltpu.make_async_remote_copy(
                src_ref=src_ref.at[peer] if scatter else src_ref,
                dst_ref=out_ref.at[me],
                send_sem=send_sems.at[t - 1],
                recv_sem=recv_sems.at[t - 1],
                device_id=(px, py, pc),
                device_id_type=pl.DeviceIdType.MESH,
            )
            cp.start()
            copies.append(cp)
        own = pltpu.make_async_copy(src_ref.at[me] if scatter else src_ref, out_ref.at[me], local_sem)
        own.start()
        for cp in copies:
            cp.wait()
        own.wait()

    return _pallas(
        body, name=name,
        out_shape=jax.ShapeDtypeStruct((NDEV,) + tuple(blk), src.dtype),
        in_specs=[pl.BlockSpec(memory_space=pl.ANY), pl.BlockSpec(memory_space=pl.ANY)],
        out_specs=pl.BlockSpec(memory_space=pl.ANY),
        scratch_shapes=[pltpu.SemaphoreType.DMA((NDEV - 1,)), pltpu.SemaphoreType.DMA((NDEV - 1,)),
                        pltpu.SemaphoreType.DMA(())],
    )(src, after)


_HBM = pl.BlockSpec(memory_space=pltpu.HBM)
_SEM = pl.BlockSpec(memory_space=pltpu.SEMAPHORE)
_ANY = pl.BlockSpec(memory_space=pl.ANY)
_EFFECT = pltpu.SideEffectType.DATAFLOW_SIDE_EFFECTING


def _exchange_copies(src_ref, land_ref, send_sems, recv_sems, mode):
    x, y, c = lax.axis_index("x"), lax.axis_index("y"), lax.axis_index("c")
    me = 4 * x + 2 * y + c
    scatter = mode == "scatter"
    copies = []
    for t in ((1, 2, 4, 6) if mode == "chip" else range(1, NDEV)):
        px = 1 - x if (t >> 2) & 1 else x
        py = 1 - y if (t >> 1) & 1 else y
        pc = 1 - c if t & 1 else c
        peer = 4 * px + 2 * py + pc
        copies.append(pltpu.make_async_remote_copy(
            src_ref=src_ref.at[peer] if scatter else src_ref,
            dst_ref=land_ref.at[me],
            send_sem=send_sems.at[t - 1],
            recv_sem=recv_sems.at[t - 1],
            device_id=(px, py, pc),
            device_id_type=pl.DeviceIdType.MESH,
        ))
    own = pltpu.make_async_copy(src_ref.at[me] if scatter else src_ref, land_ref.at[me], send_sems.at[NDEV - 1])
    return copies, own


def _forward_copies(land_ref, send_sems, recv_sems):
    x, y, c = lax.axis_index("x"), lax.axis_index("y"), lax.axis_index("c")
    copies = []
    for k, t in enumerate((2, 4, 6)):
        px = 1 - x if (t >> 2) & 1 else x
        py = 1 - y if (t >> 1) & 1 else y
        mine, theirs = 4 * px + 2 * py + c, 4 * px + 2 * py + (1 - c)
        send = pltpu.make_async_remote_copy(
            src_ref=land_ref.at[mine], dst_ref=land_ref.at[mine], send_sem=send_sems.at[k], recv_sem=recv_sems.at[k],
            device_id=(x, y, 1 - c), device_id_type=pl.DeviceIdType.MESH)
        recv = pltpu.make_async_remote_copy(
            src_ref=land_ref.at[theirs], dst_ref=land_ref.at[theirs], send_sem=send_sems.at[k], recv_sem=recv_sems.at[k],
            device_id=(x, y, 1 - c), device_id_type=pl.DeviceIdType.MESH)
        copies.append((send, recv))
    return copies


def _forward_start(land, after, name):
    def body(land_ref, after_ref, send_sems, recv_sems, land_thru, token):
        for send, _ in _forward_copies(land_ref, send_sems, recv_sems):
            send.start()
        token[...] = jnp.zeros_like(token)

    res = _pallas(
        body, name=name,
        out_shape=(pltpu.SemaphoreType.DMA((3,)), pltpu.SemaphoreType.DMA((3,)), pltpu.HBM(land.shape, land.dtype),
                   jax.ShapeDtypeStruct((8, HEAD), F32)),
        in_specs=(_HBM, _ANY), out_specs=(_SEM, _SEM, _HBM, pl.BlockSpec(memory_space=pltpu.VMEM)),
        input_output_aliases={0: 2},
        compiler_params=pltpu.CompilerParams(has_side_effects=_EFFECT),
    )(land, after)
    return res[:3], res[3]


def _forward_wait(handle, after, name):
    send_sems, recv_sems, land_thru = handle

    def body(land_ref, send_sems, recv_sems, after_ref, got_ref):
        for send, recv in _forward_copies(land_ref, send_sems, recv_sems):
            send.wait_send()
            recv.wait_recv()

    return _pallas(
        body, name=name,
        out_shape=pltpu.HBM(land_thru.shape, land_thru.dtype),
        in_specs=(_HBM, _SEM, _SEM, _ANY), out_specs=_HBM,
        input_output_aliases={0: 0},
        compiler_params=pltpu.CompilerParams(has_side_effects=_EFFECT),
    )(land_thru, send_sems, recv_sems, after)


def _exchange_start(src, mode, after, name):
    blk = src.shape[1:] if mode == "scatter" else src.shape
    land = lax.empty((NDEV,) + tuple(blk), src.dtype)

    def body(src_ref, land_ref, after_ref, send_sems, recv_sems, src_thru, land_thru, token):
        copies, own = _exchange_copies(src_ref, land_ref, send_sems, recv_sems, mode)
        for cp in copies:
            cp.start()
        own.start()
        token[...] = jnp.zeros_like(token)

    res = _pallas(
        body, name=name,
        out_shape=(pltpu.SemaphoreType.DMA((NDEV,)), pltpu.SemaphoreType.DMA((NDEV,)),
                   pltpu.HBM(src.shape, src.dtype), pltpu.HBM(land.shape, land.dtype),
                   jax.ShapeDtypeStruct((8, HEAD), F32)),
        in_specs=(_HBM, _HBM, _ANY), out_specs=(_SEM, _SEM, _HBM, _HBM, pl.BlockSpec(memory_space=pltpu.VMEM)),
        input_output_aliases={0: 2, 1: 3},
        compiler_params=pltpu.CompilerParams(has_side_effects=_EFFECT),
    )(pltpu.with_memory_space_constraint(src, pltpu.HBM), pltpu.with_memory_space_constraint(land, pltpu.HBM), after)
    return res[:4], res[4]


def _exchange_wait(handle, mode, after, name):
    send_sems, recv_sems, src_thru, land_thru = handle

    def body(src_ref, land_ref, send_sems, recv_sems, after_ref, src_dead, got_ref):
        copies, own = _exchange_copies(src_ref, land_ref, send_sems, recv_sems, mode)
        for cp in copies:
            cp.wait_send()
            cp.wait_recv()
        own.wait()

    return _pallas(
        body, name=name,
        out_shape=(pltpu.HBM(src_thru.shape, src_thru.dtype), pltpu.HBM(land_thru.shape, land_thru.dtype)),
        in_specs=(_HBM, _HBM, _SEM, _SEM, _ANY), out_specs=(_HBM, _HBM),
        input_output_aliases={0: 0, 1: 1},
        compiler_params=pltpu.CompilerParams(has_side_effects=_EFFECT),
    )(src_thru, land_thru, send_sems, recv_sems, after)[1]


def _ada_fwd(c_all, w, bias):
    r, d = c_all.shape
    e = w.shape[1]
    tn = 512

    def body(c_ref, w_ref, b_ref, o_ref):
        cv = c_ref[...]
        s = (cv * _sigmoid(cv)).astype(BF16)
        o_ref[...] = jnp.dot(s, w_ref[...].astype(BF16), preferred_element_type=F32) + b_ref[...]

    return _pallas(
        body, name="ada_fwd", grid=(e // tn,),
        out_shape=jax.ShapeDtypeStruct((r, e), F32),
        in_specs=[pl.BlockSpec((r, d), lambda j: (0, 0)), pl.BlockSpec((d, tn), lambda j: (0, j)),
                  pl.BlockSpec((1, tn), lambda j: (0, j))],
        out_specs=pl.BlockSpec((r, tn), lambda j: (0, j)),
        compiler_params=_params(_mb(24)),
    )(c_all, w, bias)


def _ada_bwd(dm16, c_all, w):
    d, e = w.shape
    tn = 512

    def body(dm_ref, c_ref, w_ref, dw_ref, dr_ref):
        j = pl.program_id(0)
        dm = dm_ref[...]
        rid = lax.broadcasted_iota(jnp.int32, dm.shape, 0)
        ctx_sum = jnp.sum(jnp.where(rid >= 8, dm, 0.0), axis=0, keepdims=True)
        rows = jnp.where(rid < 8, dm, jnp.where(rid == 8, jnp.broadcast_to(ctx_sum, dm.shape), 0.0)).astype(BF16)
        cv = c_ref[...]
        s = (cv * _sigmoid(cv)).astype(BF16)
        dw_ref[...] = lax.dot_general(s, rows, _TN, preferred_element_type=F32)
        part = lax.dot_general(rows, w_ref[...].astype(BF16), _NT, preferred_element_type=F32)

        @pl.when(j == 0)
        def _():
            dr_ref[...] = part

        @pl.when(j > 0)
        def _():
            dr_ref[...] += part

    return _pallas(
        body, name="ada_bwd", grid=(e // tn,),
        out_shape=(jax.ShapeDtypeStruct((d, e), F32), jax.ShapeDtypeStruct((16, d), F32)),
        in_specs=[pl.BlockSpec((16, tn), lambda j: (0, j)), pl.BlockSpec((16, d), lambda j: (0, 0)),
                  pl.BlockSpec((d, tn), lambda j: (0, j))],
        out_specs=(pl.BlockSpec((d, tn), lambda j: (0, j)), pl.BlockSpec((16, d), lambda j: (0, 0))),
        compiler_params=_params(_mb(32)),
    )(dm16, c_all, w)


def _rope(v, cos, sa, sb):
    return v * cos + (pltpu.roll(v, 96, 1) * sa + pltpu.roll(v, 32, 1) * sb)


def _rope_t(dt, cos, sa, sb):
    return dt * cos + (pltpu.roll(dt * sa, 32, 1) + pltpu.roll(dt * sb, 96, 1))


def _qkv_fwd(x, ct, sc, sh, wint, qg, kg, cos, sa, sb):
    n, d = x.shape
    tm = CTX
    nlat = n // tm
    na = n + CTX
    wcols = wint.shape[0]

    def body(x_ref, ct_ref, sc_ref, sh_ref, w_ref, qg_ref, kg_ref, cos_ref, sa_ref, sb_ref, u_ref, h_ref, t_ref, kt_ref):
        i = pl.program_id(0)
        xin = jnp.where(i == nlat, ct_ref[...], x_ref[...])
        u = (xin * (1.0 + sc_ref[0]) + sh_ref[0]).astype(BF16)
        u_ref[...] = u
        cos, sa, sb = cos_ref[...], sa_ref[...], sb_ref[...]
        h = lax.dot_general(u, w_ref[...], _NT, preferred_element_type=F32)
        h_ref[...] = h[:, NORM_HEAD0 * HEAD:(NORM_HEAD0 + NORM_HEADS) * HEAD]
        for hd in range(24):
            v = h[:, hd * HEAD:(hd + 1) * HEAD]
            kind = _KINDS[hd]
            if kind == "qnorm":
                v = v * lax.rsqrt(_rowmean(v * v) + EPS) * qg_ref[...]
            elif kind == "knorm":
                v = v * lax.rsqrt(_rowmean(v * v) + EPS) * kg_ref[...]
            if kind != "none":
                v = _rope(v, cos, sa, sb)
            t_ref[:, hd * HEAD:(hd + 1) * HEAD] = v.astype(BF16)
            if kind == "knorm":
                kt_ref[(hd - 20) * HEAD:(hd - 19) * HEAD, :] = v.T.astype(BF16)

    lat = lambda i: (jnp.minimum(i, nlat - 1), 0)
    row = lambda i: (i, 0)
    const2 = lambda i: (0, 0)
    return _pallas(
        body, name="qkv_fwd", grid=(nlat + 1,),
        out_shape=(jax.ShapeDtypeStruct((na, d), BF16), jax.ShapeDtypeStruct((na, NORM_HEADS * HEAD), F32),
                   jax.ShapeDtypeStruct((na, wcols), BF16), jax.ShapeDtypeStruct((2 * HEAD, na), BF16)),
        in_specs=[pl.BlockSpec((tm, d), lat), pl.BlockSpec((tm, d), const2),
                  pl.BlockSpec((1, 1, d), lambda i: (i // nlat, 0, 0)),
                  pl.BlockSpec((1, 1, d), lambda i: (i // nlat, 0, 0)),
                  pl.BlockSpec((wcols, d), const2),
                  pl.BlockSpec((1, HEAD), const2), pl.BlockSpec((1, HEAD), const2),
                  pl.BlockSpec((tm, HEAD), row), pl.BlockSpec((tm, HEAD), row), pl.BlockSpec((tm, HEAD), row)],
        out_specs=(pl.BlockSpec((tm, d), row), pl.BlockSpec((tm, NORM_HEADS * HEAD), row), pl.BlockSpec((tm, wcols), row),
                   pl.BlockSpec((2 * HEAD, tm), lambda i: (0, i))),
        compiler_params=_params(_mb(56)),
    )(x, ct, sc, sh, wint, qg, kg, cos, sa, sb)


def _qkv_bwd_prep(dqa, dka, dva, dqb, dkb, dvb, h_norm, qg, kg, cos, sa, sb):
    na = h_norm.shape[0]
    wcols = 24 * HEAD
    n = na - CTX
    tm = CTX
    nlat = n // tm

    def body(dqa_ref, dka_ref, dva_ref, dqb_ref, dkb_ref, dvb_ref, h_ref, qg_ref, kg_ref, cos_ref, sa_ref, sb_ref,
             dh_ref, dg_ref):
        i = pl.program_id(0)

        @pl.when(i == 0)
        def _():
            dg_ref[...] = jnp.zeros_like(dg_ref)

        cos, sa, sb = cos_ref[...], sa_ref[...], sb_ref[...]
        is_lat = i < nlat
        for hd in range(24):
            kind = _KINDS[hd]
            if hd < 8:
                dt = jnp.where(is_lat, dqa_ref[:, hd * HEAD:(hd + 1) * HEAD], 0.0)
            elif hd < 10:
                dt = dka_ref[:, (hd - 8) * HEAD:(hd - 7) * HEAD]
            elif hd < 12:
                dt = dva_ref[:, (hd - 10) * HEAD:(hd - 9) * HEAD]
            elif hd < 20:
                dt = jnp.where(is_lat, dqb_ref[:, (hd - 12) * HEAD:(hd - 11) * HEAD], 0.0)
            elif hd < 22:
                dt = dkb_ref[:, (hd - 20) * HEAD:(hd - 19) * HEAD]
            else:
                dt = dvb_ref[:, (hd - 22) * HEAD:(hd - 21) * HEAD]
            if kind != "none":
                dt = _rope_t(dt, cos, sa, sb)
            if kind in ("qnorm", "knorm"):
                g_ref = qg_ref if kind == "qnorm" else kg_ref
                r0 = 0 if kind == "qnorm" else 1
                xv = h_ref[:, (hd - NORM_HEAD0) * HEAD:(hd - NORM_HEAD0 + 1) * HEAD]
                xn = xv * lax.rsqrt(_rowmean(xv * xv) + EPS)
                dg_ref[r0:r0 + 1, :] += _colsum(dt * xn)
                dxn = dt * g_ref[...]
                dt = lax.rsqrt(_rowmean(xv * xv) + EPS) * (dxn - xn * _rowmean(dxn * xn))
            dh_ref[:, hd * HEAD:(hd + 1) * HEAD] = dt.astype(BF16)

    lat = lambda i: (jnp.minimum(i, nlat - 1), 0)
    row = lambda i: (i, 0)
    const2 = lambda i: (0, 0)
    return _pallas(
        body, name="qkv_bwd_prep", grid=(nlat + 1,),
        out_shape=(jax.ShapeDtypeStruct((na, wcols), BF16), jax.ShapeDtypeStruct((8, HEAD), F32)),
        in_specs=[pl.BlockSpec((tm, 8 * HEAD), lat), pl.BlockSpec((tm, 2 * HEAD), row), pl.BlockSpec((tm, 2 * HEAD), row),
                  pl.BlockSpec((tm, 8 * HEAD), lat), pl.BlockSpec((tm, 2 * HEAD), row), pl.BlockSpec((tm, 2 * HEAD), row),
                  pl.BlockSpec((tm, NORM_HEADS * HEAD), row),
                  pl.BlockSpec((1, HEAD), const2), pl.BlockSpec((1, HEAD), const2),
                  pl.BlockSpec((tm, HEAD), row), pl.BlockSpec((tm, HEAD), row), pl.BlockSpec((tm, HEAD), row)],
        out_specs=(pl.BlockSpec((tm, wcols), row), pl.BlockSpec((8, HEAD), const2)),
        compiler_params=_params(_mb(40)),
    )(dqa, dka, dva, dqb, dkb, dvb, h_norm, qg, kg, cos, sa, sb)


def _window_keys(k_ref, v_ref, n, na):
    i = pl.program_id(1)
    tq = WINDOW
    start = pl.multiple_of(jnp.clip((i - 1) * tq, 0, n - 3 * tq), tq)
    kk = jnp.concatenate([k_ref[pl.ds(start, 3 * tq), :], k_ref[n:na, :]], axis=0)
    vv = jnp.concatenate([v_ref[pl.ds(start, 3 * tq), :], v_ref[n:na, :]], axis=0)
    return kk, vv, start


def _window_bias():
    tq = WINDOW
    r = (jnp.arange(4 * tq) % tq)[:, None]
    c = jnp.arange(3 * tq + CTX)[None, :]
    variants = []
    for back in (0, tq, 2 * tq):
        seen = (jnp.abs(back + r - c) <= WINDOW) | (c >= 3 * tq)
        variants.append(jnp.where(seen, 0.0, NEG).astype(F32))
    return jnp.stack(variants)


def _window_bias_spec(nq):
    return pl.BlockSpec((1, 4 * WINDOW, 3 * WINDOW + CTX),
                        lambda kv, i: (jnp.where(i == 0, 0, jnp.where(i == nq - 1, 2, 1)), 0, 0))


def _stack_heads(ref, width=HEAD):
    return jnp.concatenate([ref[:, g * HEAD:g * HEAD + width] for g in range(4)], axis=0)


def _sink_column(sink_ref, kv, tq):
    grp = lax.broadcasted_iota(jnp.int32, (4 * tq, 1), 0) // tq
    col = jnp.zeros((4 * tq, 1), F32)
    for g in range(4):
        col = jnp.where(grp == g, sink_ref[0, 4 * kv + g] * LOG2E, col)
    return col


def _attn_window_fwd(t_all, sink, bias, after):
    na = t_all.shape[0]
    n = na - CTX
    tq = WINDOW

    def body(sink_ref, q_ref, k_ref, v_ref, bias_ref, after_ref, o_ref, lse_ref):
        kv = pl.program_id(0)
        kk, vv, _ = _window_keys(k_ref, v_ref, n, na)
        t = lax.dot_general(_stack_heads(q_ref), kk, _NT, preferred_element_type=F32) * QK_LOG2 + bias_ref[0]
        sk = _sink_column(sink_ref, kv, tq)
        m = jnp.maximum(jnp.max(t, axis=-1, keepdims=True), sk)
        p = jnp.exp2(t - m)
        l = jnp.sum(p, axis=-1, keepdims=True) + jnp.exp2(sk - m)
        o = jnp.dot(p.astype(BF16), vv, preferred_element_type=F32) * (1.0 / l)
        lse = m + jnp.log2(l)
        for g in range(4):
            o_ref[:, g * HEAD:(g + 1) * HEAD] = o[g * tq:(g + 1) * tq]
            lse_ref[:, g * HEAD:(g + 1) * HEAD] = jnp.broadcast_to(lse[g * tq:(g + 1) * tq], (tq, HEAD))

    blk = pl.BlockSpec((tq, 4 * HEAD), lambda kv, i: (i, kv))
    return _pallas(
        body, name="attn_window_fwd", grid=(2, n // tq),
        out_shape=(jax.ShapeDtypeStruct((n, 16 * HEAD), F32), jax.ShapeDtypeStruct((n, 8 * HEAD), F32)),
        in_specs=[pl.BlockSpec(memory_space=pltpu.SMEM), blk,
                  pl.BlockSpec((na, HEAD), lambda kv, i: (0, 8 + kv)),
                  pl.BlockSpec((na, HEAD), lambda kv, i: (0, 10 + kv)), _window_bias_spec(n // tq), _ANY],
        out_specs=(blk, blk),
        compiler_params=_params(_mb(32)),
    )(sink, t_all, t_all, t_all, bias, after)


def _attn_global_fwd(t_all, o_part):
    na = t_all.shape[0]
    n = na - CTX
    tq = 256

    def body(q_ref, k_ref, v_ref, o_in_ref, o_ref, p_ref, linv_ref):
        kk, vv = k_ref[...], v_ref[...]
        for g in range(4):
            q = q_ref[:, g * HEAD:(g + 1) * HEAD]
            t = lax.dot_general(q, kk, _NT, preferred_element_type=F32) * QK_LOG2
            m = jnp.max(t, axis=-1, keepdims=True)
            p = jnp.exp2(t - m)
            linv = 1.0 / jnp.sum(p, axis=-1, keepdims=True)
            pb = p.astype(BF16)
            p_ref[g] = pb
            o_ref[:, g * HEAD:(g + 1) * HEAD] = jnp.dot(pb, vv, preferred_element_type=F32) * linv
            linv_ref[:, g * HEAD:(g + 1) * HEAD] = jnp.broadcast_to(linv, (tq, HEAD))

    return _pallas(
        body, name="attn_global_fwd", grid=(2, n // tq),
        out_shape=(jax.ShapeDtypeStruct((n, 16 * HEAD), F32), jax.ShapeDtypeStruct((8, n, na), BF16),
                   jax.ShapeDtypeStruct((n, 8 * HEAD), F32)),
        in_specs=[pl.BlockSpec((tq, 4 * HEAD), lambda kv, i: (i, 3 + kv)),
                  pl.BlockSpec((na, HEAD), lambda kv, i: (0, 20 + kv)),
                  pl.BlockSpec((na, HEAD), lambda kv, i: (0, 22 + kv)), _ANY],
        out_specs=(pl.BlockSpec((tq, 4 * HEAD), lambda kv, i: (i, 2 + kv)),
                   pl.BlockSpec((4, tq, na), lambda kv, i: (kv, i, 0)),
                   pl.BlockSpec((tq, 4 * HEAD), lambda kv, i: (i, kv))),
        input_output_aliases={3: 0},
        compiler_params=_params(_mb(56)),
    )(t_all, t_all, t_all, o_part)


def _attn_window_bwd(t_all, o, do, lse, sink, bias):
    na = t_all.shape[0]
    n = na - CTX
    tq = WINDOW

    def body(sink_ref, q_ref, k_ref, v_ref, o_ref, do_ref, lse_ref, bias_ref, dq_ref, dk_ref, dv_ref, dsink_ref):
        kv = pl.program_id(0)

        @pl.when(pl.program_id(1) == 0)
        def _():
            dk_ref[...] = jnp.zeros_like(dk_ref)
            dv_ref[...] = jnp.zeros_like(dv_ref)
            dsink_ref[...] = jnp.zeros_like(dsink_ref)

        kk, vv, start = _window_keys(k_ref, v_ref, n, na)
        q = _stack_heads(q_ref)
        t = lax.dot_general(q, kk, _NT, preferred_element_type=F32) * QK_LOG2 + bias_ref[0]
        lse = _stack_heads(lse_ref, 1)
        p = jnp.exp2(t - lse)
        dof = _stack_heads(do_ref)
        delta = jnp.sum(dof * _stack_heads(o_ref), axis=-1, keepdims=True)
        dob = dof.astype(BF16)
        dv_acc = lax.dot_general(p.astype(BF16), dob, _TN, preferred_element_type=F32)
        dp = lax.dot_general(dob, vv, _NT, preferred_element_type=F32)
        ds = (p * (dp - delta) * SCALE).astype(BF16)
        dq = jnp.dot(ds, kk, preferred_element_type=F32)
        dk_acc = lax.dot_general(ds, q, _TN, preferred_element_type=F32)
        dsk = -(jnp.exp2(_sink_column(sink_ref, kv, tq) - lse) * delta)
        for g in range(4):
            dq_ref[:, g * HEAD:(g + 1) * HEAD] = dq[g * tq:(g + 1) * tq]
            dsink_ref[0, g:g + 1, :] += jnp.broadcast_to(_colsum(dsk[g * tq:(g + 1) * tq]), (1, HEAD))
        dk_ref[pl.ds(start, 3 * tq), :] += dk_acc[:3 * tq]
        dv_ref[pl.ds(start, 3 * tq), :] += dv_acc[:3 * tq]
        dk_ref[n:na, :] += dk_acc[3 * tq:]
        dv_ref[n:na, :] += dv_acc[3 * tq:]

    blk = pl.BlockSpec((tq, 4 * HEAD), lambda kv, i: (i, kv))
    kvout = pl.BlockSpec((na, HEAD), lambda kv, i: (0, kv))
    return _pallas(
        body, name="attn_window_bwd", grid=(2, n // tq),
        out_shape=(jax.ShapeDtypeStruct((n, 8 * HEAD), F32), jax.ShapeDtypeStruct((na, 2 * HEAD), F32),
                   jax.ShapeDtypeStruct((na, 2 * HEAD), F32), jax.ShapeDtypeStruct((2, 8, HEAD), F32)),
        in_specs=[pl.BlockSpec(memory_space=pltpu.SMEM), blk,
                  pl.BlockSpec((na, HEAD), lambda kv, i: (0, 8 + kv)),
                  pl.BlockSpec((na, HEAD), lambda kv, i: (0, 10 + kv)),
                  blk, blk, blk, _window_bias_spec(n // tq)],
        out_specs=(blk, kvout, kvout, pl.BlockSpec((1, 8, HEAD), lambda kv, i: (kv, 0, 0))),
        compiler_params=_params(_mb(40)),
    )(sink, t_all, t_all, t_all, o, do, lse, bias)


def _attn_global_bwd(t_all, kt, o, do, p_all, linv):
    na = t_all.shape[0]
    n = na - CTX
    tq = 256

    def body(q_ref, v_ref, kt_ref, o_ref, do_ref, p_ref, linv_ref, dq_ref, dk_ref, dv_ref, dkt_acc, dvt_acc):
        i = pl.program_id(1)

        @pl.when(i == 0)
        def _():
            dkt_acc[...] = jnp.zeros_like(dkt_acc)
            dvt_acc[...] = jnp.zeros_like(dvt_acc)

        vv, kt_v = v_ref[...], kt_ref[...]
        dkt = jnp.zeros((HEAD, na), F32)
        dvt = jnp.zeros((HEAD, na), F32)
        def probs(g):
            return p_ref[g].astype(F32) * linv_ref[:, g * HEAD:g * HEAD + 1]

        def dprobs(g):
            dob = do_ref[:, g * HEAD:(g + 1) * HEAD].astype(BF16)
            return dob, lax.dot_general(dob, vv, _NT, preferred_element_type=F32)

        nxt = dprobs(0)
        for g in range(4):
            q = q_ref[:, g * HEAD:(g + 1) * HEAD]
            p = probs(g)
            dob, dp = nxt
            if g < 3:
                nxt = dprobs(g + 1)
            delta = jnp.sum(do_ref[:, g * HEAD:(g + 1) * HEAD] * o_ref[:, g * HEAD:(g + 1) * HEAD], axis=-1,
                            keepdims=True)
            dvt = dvt + lax.dot_general(dob, p.astype(BF16), _TN, preferred_element_type=F32)
            ds = (p * (dp - delta) * SCALE).astype(BF16)
            dq_ref[:, g * HEAD:(g + 1) * HEAD] = lax.dot_general(kt_v, ds, _NT, preferred_element_type=F32).T
            dkt = dkt + lax.dot_general(q, ds, _TN, preferred_element_type=F32)
        dkt_acc[...] += dkt
        dvt_acc[...] += dvt

        @pl.when(i == pl.num_programs(1) - 1)
        def _():
            dk_ref[...] = dkt_acc[...].T
            dv_ref[...] = dvt_acc[...].T

    ospec = pl.BlockSpec((tq, 4 * HEAD), lambda kv, i: (i, 2 + kv))
    lspec = pl.BlockSpec((tq, 4 * HEAD), lambda kv, i: (i, kv))
    kvout = pl.BlockSpec((na, HEAD), lambda kv, i: (0, kv))
    return _pallas(
        body, name="attn_global_bwd", grid=(2, n // tq),
        out_shape=(jax.ShapeDtypeStruct((n, 8 * HEAD), F32), jax.ShapeDtypeStruct((na, 2 * HEAD), F32),
                   jax.ShapeDtypeStruct((na, 2 * HEAD), F32)),
        in_specs=[pl.BlockSpec((tq, 4 * HEAD), lambda kv, i: (i, 3 + kv)),
                  pl.BlockSpec((na, HEAD), lambda kv, i: (0, 22 + kv)),
                  pl.BlockSpec((HEAD, na), lambda kv, i: (kv, 0)),
                  ospec, ospec, pl.BlockSpec((4, tq, na), lambda kv, i: (kv, i, 0)), lspec],
        out_specs=(lspec, kvout, kvout),
        scratch_shapes=[pltpu.VMEM((HEAD, na), F32), pltpu.VMEM((HEAD, na), F32)],
        compiler_params=_params(_mb(56)),
    )(t_all, t_all, kt, o, do, p_all, linv)


def _outproj_ln1(o, wout, x, g1, lg, lb, sc2, sh2, after):
    n, d = x.shape
    tm = 256

    def body(o_ref, w_ref, x_ref, g1_ref, lg_ref, lb_ref, sc_ref, sh_ref, after_ref, a_ref, xh_ref, rs_ref, u_ref):
        a1 = jnp.dot(o_ref[...].astype(BF16), w_ref[...], preferred_element_type=F32)
        a_ref[...] = a1.astype(BF16)
        r = ALPHA * x_ref[...] + g1_ref[...] * a1
        dlt = r - _rowmean(r)
        rstd = lax.rsqrt(_rowmean(dlt * dlt) + EPS)
        xh = dlt * rstd
        xh_ref[...] = xh
        rs_ref[...] = rstd
        x1 = xh * lg_ref[...] + lb_ref[...]
        u_ref[...] = (x1 * (1.0 + sc_ref[...]) + sh_ref[...]).astype(BF16)

    row = lambda i: (i, 0)
    const2 = lambda i: (0, 0)
    vec = pl.BlockSpec((1, d), const2)
    big = pl.BlockSpec((tm, d), row)
    return _pallas(
        body, name="outproj_ln1", grid=(n // tm,),
        out_shape=(jax.ShapeDtypeStruct((n, d), BF16), jax.ShapeDtypeStruct((n, d), F32),
                   jax.ShapeDtypeStruct((n, 1), F32), jax.ShapeDtypeStruct((n, d), BF16)),
        in_specs=[big, pl.BlockSpec((d, d), const2), big, vec, vec, vec, vec, vec, _ANY],
        out_specs=(big, big, pl.BlockSpec((tm, 1), row), big),
        compiler_params=_params(_mb(56)),
    )(o, wout, x, g1, lg, lb, sc2, sh2, after)


def _ffn_up(u2, wgt, wut, after):
    n, d = u2.shape
    f = wgt.shape[0]
    tm = min(1024, n)

    def body(u_ref, wg_ref, wu_ref, after_ref, sa_ref, sb_ref, hf_ref):
        u = u_ref[...]
        gv = lax.dot_general(u, wg_ref[...], _NT, preferred_element_type=F32)
        pv = lax.dot_general(u, wu_ref[...], _NT, preferred_element_type=F32)
        sg = _sigmoid(gv)
        silu = gv * sg
        sa_ref[...] = silu.astype(BF16)
        sb_ref[...] = (pv * (sg * (1.0 + gv * (1.0 - sg)))).astype(BF16)
        hf_ref[...] = (silu * pv).astype(BF16)

    tile = pl.BlockSpec((tm, FFN_TILE), lambda i, j: (i, j))
    wspec = pl.BlockSpec((FFN_TILE, d), lambda i, j: (j, 0))
    sds = jax.ShapeDtypeStruct((n, f), BF16)
    return _pallas(
        body, name="ffn_up", grid=(n // tm, f // FFN_TILE),
        out_shape=(sds, sds, sds),
        in_specs=[pl.BlockSpec((tm, d), lambda i, j: (i, 0)), wspec, wspec, _ANY],
        out_specs=(tile, tile, tile),
        compiler_params=_params(_mb(48)),
    )(u2, wgt, wut, after)


def _ffn_down(hf, wd):
    n, f = hf.shape
    d = wd.shape[1]
    tm, tn = min(1024, n), 512

    def body(h_ref, w_ref, o_ref):
        o_ref[...] = jnp.dot(h_ref[...], w_ref[...], preferred_element_type=F32)

    return _pallas(
        body, name="ffn_down", grid=(n // tm, d // tn),
        out_shape=jax.ShapeDtypeStruct((n, d), F32),
        in_specs=[pl.BlockSpec((tm, f), lambda i, j: (i, 0)), pl.BlockSpec((f, tn), lambda i, j: (0, j))],
        out_specs=pl.BlockSpec((tm, tn), lambda i, j: (i, j)),
        compiler_params=_params(_mb(56)),
    )(hf, wd)


def _ln2_loss(xh1, ffn, tgt, lg1, lb1, g2, lg2, lb2):
    n, d = xh1.shape
    tm = 256

    def body(xh_ref, f_ref, t_ref, lg1_ref, lb1_ref, g2_ref, lg2_ref, lb2_ref, dr_ref, df_ref, loss_ref, acc_ref):
        @pl.when(pl.program_id(0) == 0)
        def _():
            loss_ref[...] = jnp.zeros_like(loss_ref)
            acc_ref[...] = jnp.zeros_like(acc_ref)

        x1 = xh_ref[...] * lg1_ref[...] + lb1_ref[...]
        fv = f_ref[...]
        r = ALPHA * x1 + g2_ref[...] * fv
        dlt = r - _rowmean(r)
        rstd = lax.rsqrt(_rowmean(dlt * dlt) + EPS)
        xh2 = dlt * rstd
        err = xh2 * lg2_ref[...] + lb2_ref[...] - t_ref[...]
        loss_ref[...] += 0.5 * jnp.sum(_rowmean(err * err))
        dy = err * (1.0 / d)
        dyg = dy * lg2_ref[...]
        dr = rstd * (dyg - _rowmean(dyg) - xh2 * _rowmean(dyg * xh2))
        dr_ref[...] = dr
        df_ref[...] = (g2_ref[...] * dr).astype(BF16)
        acc_ref[0:1, :] += _colsum(dy * xh2)
        acc_ref[1:2, :] += _colsum(dy)
        acc_ref[2:3, :] += _colsum(dr * fv)

    row = lambda i: (i, 0)
    const2 = lambda i: (0, 0)
    vec = pl.BlockSpec((1, d), const2)
    big = pl.BlockSpec((tm, d), row)
    return _pallas(
        body, name="ln2_loss", grid=(n // tm,),
        out_shape=(jax.ShapeDtypeStruct((n, d), F32), jax.ShapeDtypeStruct((n, d), BF16),
                   jax.ShapeDtypeStruct((8, HEAD), F32), jax.ShapeDtypeStruct((8, d), F32)),
        in_specs=[big, big, big, vec, vec, vec, vec, vec],
        out_specs=(big, big, pl.BlockSpec((8, HEAD), const2), pl.BlockSpec((8, d), const2)),
        compiler_params=_params(_mb(48)),
    )(xh1, ffn, tgt, lg1, lb1, g2, lg2, lb2)


def _ffn_dhf(df, wd, sa, sb):
    n, d = df.shape
    f = sa.shape[1]
    tm = min(2048, n)

    def body(df_ref, w_ref, sa_ref, sb_ref, dgp_ref):
        dhf = lax.dot_general(df_ref[...], w_ref[...], _NT, preferred_element_type=F32)
        dgp_ref[:, :FFN_TILE] = (dhf * sb_ref[...].astype(F32)).astype(BF16)
        dgp_ref[:, FFN_TILE:] = (dhf * sa_ref[...].astype(F32)).astype(BF16)

    tile = pl.BlockSpec((tm, FFN_TILE), lambda i, j: (i, j))
    return _pallas(
        body, name="ffn_dhf", grid=(n // tm, f // FFN_TILE),
        out_shape=jax.ShapeDtypeStruct((n, 2 * f), BF16),
        in_specs=[pl.BlockSpec((tm, d), lambda i, j: (i, 0)), pl.BlockSpec((FFN_TILE, d), lambda i, j: (j, 0)),
                  tile, tile],
        out_specs=pl.BlockSpec((tm, 2 * FFN_TILE), lambda i, j: (i, j)),
        compiler_params=_params(_mb(48)),
    )(df, wd, sa, sb)


def _ffn_du2(dgp, wgt, wut, after):
    n = dgp.shape[0]
    f, d = wgt.shape
    tm = min(1024, n)

    def body(dgp_ref, wg_ref, wu_ref, after_ref, o_ref):
        w = jnp.concatenate([wg_ref[...], wu_ref[...]], axis=0)
        part = jnp.dot(dgp_ref[...], w, preferred_element_type=F32)

        @pl.when(pl.program_id(1) == 0)
        def _():
            o_ref[...] = part

        @pl.when(pl.program_id(1) > 0)
        def _():
            o_ref[...] += part

    wspec = pl.BlockSpec((FFN_TILE, d), lambda i, j: (j, 0))
    return _pallas(
        body, name="ffn_du2", grid=(n // tm, f // FFN_TILE),
        out_shape=jax.ShapeDtypeStruct((n, d), F32),
        in_specs=[pl.BlockSpec((tm, 2 * FFN_TILE), lambda i, j: (i, j)), wspec, wspec, _ANY],
        out_specs=pl.BlockSpec((tm, d), lambda i, j: (i, 0)),
        compiler_params=_params(_mb(48)),
    )(dgp, wgt, wut, after)


def _dw_gate_up(dgp, u2, after):
    n, d = u2.shape
    f = dgp.shape[1] // 2
    tm = min(2048, n)

    def body(a_ref, b_ref, after_ref, og_ref, ou_ref, acc_ref):
        part = lax.dot_general(a_ref[...], b_ref[...], _TN, preferred_element_type=F32)
        i = pl.program_id(1)

        @pl.when(i == 0)
        def _():
            acc_ref[...] = part

        @pl.when(i > 0)
        def _():
            acc_ref[...] += part

        @pl.when(i == pl.num_programs(1) - 1)
        def _():
            og_ref[...] = acc_ref[:FFN_TILE].astype(BF16)
            ou_ref[...] = acc_ref[FFN_TILE:].astype(BF16)

    out = pl.BlockSpec((FFN_TILE, d), lambda j, i: (j, 0))
    sds = jax.ShapeDtypeStruct((f, d), BF16)
    return _pallas(
        body, name="dw_gate_up", grid=(f // FFN_TILE, n // tm),
        out_shape=(sds, sds),
        in_specs=[pl.BlockSpec((tm, 2 * FFN_TILE), lambda j, i: (i, j)), pl.BlockSpec((tm, d), lambda j, i: (i, 0)), _ANY],
        out_specs=(out, out),
        scratch_shapes=[pltpu.VMEM((2 * FFN_TILE, d), F32)],
        compiler_params=_params(_mb(56)),
    )(dgp, u2, after)


def _ln1_bwd(du2, dr2, xh1, rs1, a1, lg1, lb1, sc2, g1):
    n, d = du2.shape
    tm = 256

    def body(du_ref, dr2_ref, xh_ref, rs_ref, a_ref, lg_ref, lb_ref, sc_ref, g1_ref, dr1_ref, da_ref, acc_ref):
        @pl.when(pl.program_id(0) == 0)
        def _():
            acc_ref[...] = jnp.zeros_like(acc_ref)

        du = du_ref[...]
        xh = xh_ref[...]
        x1 = xh * lg_ref[...] + lb_ref[...]
        dx1 = ALPHA * dr2_ref[...] + du * (1.0 + sc_ref[...])
        dxg = dx1 * lg_ref[...]
        dr1 = rs_ref[...] * (dxg - _rowmean(dxg) - xh * _rowmean(dxg * xh))
        dr1_ref[...] = dr1
        da_ref[...] = (g1_ref[...] * dr1).astype(BF16)
        acc_ref[0:1, :] += _colsum(du * x1)
        acc_ref[1:2, :] += _colsum(du)
        acc_ref[2:3, :] += _colsum(dx1 * xh)
        acc_ref[3:4, :] += _colsum(dx1)
        acc_ref[4:5, :] += _colsum(dr1 * a_ref[...].astype(F32))

    row = lambda i: (i, 0)
    const2 = lambda i: (0, 0)
    vec = pl.BlockSpec((1, d), const2)
    big = pl.BlockSpec((tm, d), row)
    return _pallas(
        body, name="ln1_bwd", grid=(n // tm,),
        out_shape=(jax.ShapeDtypeStruct((n, d), F32), jax.ShapeDtypeStruct((n, d), BF16),
                   jax.ShapeDtypeStruct((8, d), F32)),
        in_specs=[big, big, big, pl.BlockSpec((tm, 1), row), big, vec, vec, vec, vec],
        out_specs=(big, big, pl.BlockSpec((8, d), const2)),
        compiler_params=_params(_mb(48)),
    )(du2, dr2, xh1, rs1, a1, lg1, lb1, sc2, g1)


def _dw_rows(a, b, nblk, bw, tm, after, name):
    m = a.shape[0]
    nn = b.shape[1]

    def body(a_ref, b_ref, after_ref, o_ref, acc_ref):
        part = lax.dot_general(a_ref[...].astype(BF16), b_ref[...], _TN, preferred_element_type=F32)
        i = pl.program_id(1)

        @pl.when(i == 0)
        def _():
            acc_ref[...] = part

        @pl.when(i > 0)
        def _():
            acc_ref[...] += part

        @pl.when(i == pl.num_programs(1) - 1)
        def _():
            o_ref[0] = acc_ref[...].astype(BF16)

    return _pallas(
        body, name=name, grid=(nblk, m // tm),
        out_shape=jax.ShapeDtypeStruct((nblk, bw, nn), BF16),
        in_specs=[pl.BlockSpec((tm, bw), lambda j, i: (i, j)), pl.BlockSpec((tm, nn), lambda j, i: (i, 0)), _ANY],
        out_specs=pl.BlockSpec((1, bw, nn), lambda j, i: (j, 0, 0)),
        scratch_shapes=[pltpu.VMEM((bw, nn), F32)],
        compiler_params=_params(_mb(56)),
    )(a, b, after)


def _outproj_bwd(da1, wout, after):
    n, d = da1.shape
    tm = 512

    def body(a_ref, w_ref, after_ref, o_ref):
        o_ref[...] = lax.dot_general(a_ref[...], w_ref[...], _NT, preferred_element_type=F32)

    return _pallas(
        body, name="outproj_bwd", grid=(n // tm,),
        out_shape=jax.ShapeDtypeStruct((n, d), F32),
        in_specs=[pl.BlockSpec((tm, d), lambda i: (i, 0)), pl.BlockSpec((d, d), lambda i: (0, 0)), _ANY],
        out_specs=pl.BlockSpec((tm, d), lambda i: (i, 0)),
        compiler_params=_params(_mb(48)),
    )(da1, wout, after)


def _qkv_bwd(dh, wint, x, ct, dr1, sc):
    na, wcols = dh.shape
    n, d = x.shape
    tm = CTX
    nlat = n // tm

    def body(dh_ref, w_ref, x_ref, ct_ref, dr_ref, sc_ref, gx_ref, acc_ref):
        i = pl.program_id(0)

        @pl.when(i == 0)
        def _():
            acc_ref[...] = jnp.zeros_like(acc_ref)

        du = jnp.dot(dh_ref[...], w_ref[...], preferred_element_type=F32)

        @pl.when(i < nlat)
        def _():
            gx_ref[...] = ALPHA * dr_ref[...] + du * (1.0 + sc_ref[0])
            acc_ref[0:1, :] += _colsum(du)
            acc_ref[1:2, :] += _colsum(du * x_ref[...])

        @pl.when(i == nlat)
        def _():
            acc_ref[2:3, :] += _colsum(du)
            acc_ref[3:4, :] += _colsum(du * ct_ref[...])

    lat = lambda i: (jnp.minimum(i, nlat - 1), 0)
    const2 = lambda i: (0, 0)
    return _pallas(
        body, name="qkv_bwd", grid=(nlat + 1,),
        out_shape=(jax.ShapeDtypeStruct((n, d), F32), jax.ShapeDtypeStruct((8, d), F32)),
        in_specs=[pl.BlockSpec((tm, wcols), lambda i: (i, 0)), pl.BlockSpec((wcols, d), const2),
                  pl.BlockSpec((tm, d), lat), pl.BlockSpec((tm, d), const2), pl.BlockSpec((tm, d), lat),
                  pl.BlockSpec((1, 1, d), lambda i: (0, 0, 0))],
        out_specs=(pl.BlockSpec((tm, d), lat), pl.BlockSpec((8, d), const2)),
        compiler_params=_params(_mb(56)),
    )(dh, wint, x, ct, dr1, sc)


def _adam_math(w, g, m, v):
    m2 = ADAM_B1 * m + (1.0 - ADAM_B1) * g
    v2 = ADAM_B2 * v + (1.0 - ADAM_B2) * (g * g)
    m_hat = m2 * (1.0 / (1.0 - ADAM_B1 ** ADAM_STEP))
    v_hat = v2 * (1.0 / (1.0 - ADAM_B2 ** ADAM_STEP))
    delta = -ADAM_LR * (m_hat / (jnp.sqrt(v_hat) + ADAM_EPS) + ADAM_WD * w)
    return delta, m2, v2


def _adamw(w, gsrc, m, v, name, after=None):
    r, c = w.shape
    parts = gsrc.ndim == 3
    after = w if after is None else after
    tr = r
    while tr * c * 4 > _mb(1) and tr % 32 == 0:
        tr //= 2

    def body(w_ref, g_ref, m_ref, v_ref, after_ref, go_ref, d_ref, mo_ref, vo_ref):
        if parts:
            g = g_ref[0].astype(F32)
            for s in range(1, NDEV):
                g = g + g_ref[s].astype(F32)
        else:
            g = g_ref[...]
        delta, m2, v2 = _adam_math(w_ref[...], g, m_ref[...], v_ref[...])
        go_ref[...] = g
        d_ref[...] = delta
        mo_ref[...] = m2
        vo_ref[...] = v2

    tile = pl.BlockSpec((tr, c), lambda i: (i, 0))
    gspec = pl.BlockSpec((NDEV, tr, c), lambda i: (0, i, 0)) if parts else tile
    sds = jax.ShapeDtypeStruct((r, c), F32)
    return _pallas(
        body, name=name, grid=(r // tr,),
        out_shape=(sds, sds, sds, sds),
        in_specs=[tile, gspec, tile, tile, _ANY],
        out_specs=(tile, tile, tile, tile),
        compiler_params=_params(_mb(48)),
    )(w, gsrc, m, v, after)


def _adamw_t(w, gsrc_t, m, v, name):
    r, c = w.shape
    tr = 256

    def body(w_ref, g_ref, m_ref, v_ref, go_ref, d_ref, mo_ref, vo_ref):
        gt = g_ref[0].astype(F32)
        for s in range(1, NDEV):
            gt = gt + g_ref[s].astype(F32)
        g = gt.T
        delta, m2, v2 = _adam_math(w_ref[...], g, m_ref[...], v_ref[...])
        go_ref[...] = g
        d_ref[...] = delta
        mo_ref[...] = m2
        vo_ref[...] = v2

    tile = pl.BlockSpec((tr, c), lambda i: (i, 0))
    sds = jax.ShapeDtypeStruct((r, c), F32)
    return _pallas(
        body, name=name, grid=(r // tr,),
        out_shape=(sds, sds, sds, sds),
        in_specs=[tile, pl.BlockSpec((NDEV, c, tr), lambda i: (0, 0, i)), tile, tile],
        out_specs=(tile, tile, tile, tile),
        compiler_params=_params(_mb(48)),
    )(w, gsrc_t, m, v)


def _small_update(gath, dcc, cc, w_s, m_s, v_s):
    d = w_s.shape[1]

    def body(g_ref, dcc_ref, cc_ref, w_ref, m_ref, v_ref, go_ref, d_ref, mo_ref, vo_ref):
        s = g_ref[0]
        for b in range(1, NDEV):
            s = s + g_ref[b]
        dsl = dcc_ref[0, 8:9, :]
        for b in range(1, NDEV):
            dsl = dsl + dcc_ref[b, 8:9, :]
        cv = cc_ref[...]
        sg = _sigmoid(cv)
        go_ref[...] = jnp.zeros_like(go_ref)
        go_ref[0:1, :] = dsl * (sg * (1.0 + cv * (1.0 - sg)))
        go_ref[1:3, :] = s[0:2] + s[6:8]
        go_ref[3:7, :] = s[2:6]
        go_ref[7:12, :] = s[8:13]
        delta, m2, v2 = _adam_math(w_ref[...], go_ref[...], m_ref[...], v_ref[...])
        d_ref[...] = delta
        mo_ref[...] = m2
        vo_ref[...] = v2

    full = pl.BlockSpec((16, d), lambda: (0, 0))
    g3 = pl.BlockSpec((NDEV, 16, d), lambda: (0, 0, 0))
    sds = jax.ShapeDtypeStruct((16, d), F32)
    return _pallas(
        body, name="small_update",
        out_shape=(sds, sds, sds, sds),
        in_specs=[g3, g3, pl.BlockSpec((1, d), lambda: (0, 0)), full, full, full],
        out_specs=(full, full, full, full),
        compiler_params=_params(_mb(24)),
    )(gath, dcc, cc, w_s, m_s, v_s)


def _rope_tables(n):
    rows = n // GRID_W
    row_ids = jnp.repeat(jnp.arange(rows, dtype=F32), GRID_W)
    col_ids = jnp.tile(jnp.arange(GRID_W, dtype=F32), rows)
    axis_dim = HEAD // 2
    inv_freq = jnp.power(ROPE_THETA, -jnp.arange(0, axis_dim, 2, dtype=F32) / axis_dim)
    ang_r = row_ids[:, None] * inv_freq
    ang_c = col_ids[:, None] * inv_freq
    ang = jnp.concatenate([ang_r, ang_r, ang_c, ang_c], axis=-1)
    cos, sin = jnp.cos(ang), jnp.sin(ang)
    first = (jnp.arange(HEAD) % (HEAD // 2)) < HEAD // 4
    sa = jnp.where(first, -sin, 0.0)
    sb = jnp.where(first, 0.0, sin)
    ones = jnp.ones((CTX, HEAD), F32)
    zeros = jnp.zeros((CTX, HEAD), F32)
    return (jnp.concatenate([cos, ones], 0), jnp.concatenate([sa, zeros], 0), jnp.concatenate([sb, zeros], 0))


def _pad_cols(a, width):
    return jnp.pad(a, ((0, 0), (0, width - a.shape[1])))


def _pad_rows(a, rows):
    return jnp.pad(a, ((0, rows - a.shape[0]), (0, 0)))


def _pack_small(c_ctx, b_ada, ln1_g, ln1_b, ln2_g, ln2_b, qg, kg, sink, d):
    misc = _pad_cols(jnp.concatenate([qg, kg, sink], axis=1), d)
    rows = jnp.concatenate([c_ctx.reshape(1, d), b_ada.reshape(6, d), ln1_g, ln1_b, ln2_g, ln2_b, misc], axis=0)
    return _pad_rows(rows, 16)


def _unpack_small(p, d):
    return dict(c_ctx=p[0], b_ada=p[1:7].reshape(1, 6 * d), ln1_g=p[7:8], ln1_b=p[8:9], ln2_g=p[9:10], ln2_b=p[10:11],
                q_norm_g=p[11:12, 0:HEAD], k_norm_g=p[11:12, HEAD:2 * HEAD], sink_logit=p[11:12, 2 * HEAD:2 * HEAD + 8])


def kernel(x, c, ctx, c_ctx, w_ada, b_ada, w_in, q_norm_g, k_norm_g, sink_logit, w_out, ln1_g, ln1_b, w_gate, w_up, w_down, ln2_g, ln2_b, loss_target, m_c_ctx, m_w_ada, m_b_ada, m_w_in, m_q_norm_g, m_k_norm_g, m_sink_logit, m_w_out, m_ln1_g, m_ln1_b, m_w_gate, m_w_up, m_w_down, m_ln2_g, m_ln2_b, v_c_ctx, v_w_ada, v_b_ada, v_w_in, v_q_norm_g, v_k_norm_g, v_sink_logit, v_w_out, v_ln1_g, v_ln1_b, v_w_gate, v_w_up, v_w_down, v_ln2_g, v_ln2_b):
    xs, cts, tgt = x[0], ctx[0], loss_target[0]
    n, d = xs.shape
    assert cts.shape == (CTX, d) and w_in.shape[2] == IN_SHARD and w_gate.shape[2] == FFN_SHARD
    me = 4 * lax.axis_index("x") + 2 * lax.axis_index("y") + lax.axis_index("c")
    e_sh = w_ada.shape[2]

    c_g = _exchange(_pad_rows(c, 8), False, "gather_c")
    c_all = jnp.concatenate([c_g[:, 0, :], _pad_rows(c_ctx.reshape(1, d), 8)], axis=0)
    bias_sh = lax.dynamic_slice(b_ada, (0, me * e_sh), (1, e_sh))
    mods_g = _exchange(_ada_fwd(c_all, w_ada[0], bias_sh), False, "gather_mods")
    mods = jnp.transpose(mods_g, (1, 0, 2)).reshape(16, NDEV * e_sh)
    mine = lax.dynamic_slice(mods, (me, 0), (1, 6 * d))
    sh1, sc1, g1, sh2, sc2, g2 = [mine[:, k * d:(k + 1) * d] for k in range(6)]
    csh1, csc1 = mods[8:9, 0:d], mods[8:9, d:2 * d]
    sc_pair = jnp.stack([sc1, csc1])
    sh_pair = jnp.stack([sh1, csh1])

    h_win, tok = _exchange_start(w_in[0].T.astype(BF16), "chip", mods, "gather_w_in_start")
    tok, (wo_l, wg_l, wu_l, wd_l) = lax.optimization_barrier((tok, (w_out, w_gate, w_up, w_down)))
    h_wout, tok = _exchange_start(wo_l[0].astype(BF16), "chip", tok, "gather_w_out_start")
    h_wg, tok = _exchange_start(wg_l[0].T.astype(BF16), "chip", tok, "gather_w_gate_start")
    h_wu, tok = _exchange_start(wu_l[0].T.astype(BF16), "chip", tok, "gather_w_up_start")
    h_wd, tok = _exchange_start(wd_l[0].astype(BF16), "chip", tok, "gather_w_down_start")

    cos, sa, sb = _rope_tables(n)
    f_win, tok = _forward_start(_exchange_wait(h_win, "chip", tok, "gather_w_in_wait"), tok, "forward_w_in_start")
    win_g = _forward_wait(f_win, tok, "forward_w_in_wait").reshape(NDEV * IN_SHARD, d)
    u_all, h_all, t_all, kt_b = _qkv_fwd(xs, cts, sc_pair, sh_pair, win_g, q_norm_g, k_norm_g, cos, sa, sb)
    f_wout, tok = _forward_start(_exchange_wait(h_wout, "chip", t_all, "gather_w_out_wait"), t_all, "forward_w_out_start")
    win_bias = _window_bias()
    o_a, lse_a = _attn_window_fwd(t_all, sink_logit, win_bias, tok)
    o, p_b, linv_b = _attn_global_fwd(t_all, o_a)
    f_wg, tok = _forward_start(_exchange_wait(h_wg, "chip", o, "gather_w_gate_wait"), o, "forward_w_gate_start")
    f_wu, tok = _forward_start(_exchange_wait(h_wu, "chip", tok, "gather_w_up_wait"), tok, "forward_w_up_start")
    wout_g = _forward_wait(f_wout, tok, "forward_w_out_wait").reshape(d, d)
    a1, xh1, rs1, u2 = _outproj_ln1(o, wout_g, xs, g1, ln1_g, ln1_b, sc2, sh2, tok)
    f_wd, tok = _forward_start(_exchange_wait(h_wd, "chip", rs1, "gather_w_down_wait"), rs1, "forward_w_down_start")
    ffn_w = (NDEV * FFN_SHARD, d)
    wg_g = _forward_wait(f_wg, tok, "forward_w_gate_wait").reshape(ffn_w)
    wu_g = _forward_wait(f_wu, tok, "forward_w_up_wait").reshape(ffn_w)
    sa_f, sb_f, hf = _ffn_up(u2, wg_g, wu_g, tok)
    wd_g = _forward_wait(f_wd, hf, "forward_w_down_wait").reshape(ffn_w)
    ffn = _ffn_down(hf, wd_g)
    dr2, df, loss_p, acc2 = _ln2_loss(xh1, ffn, tgt, ln1_g, ln1_b, g2, ln2_g, ln2_b)
    loss = lax.psum(loss_p[0, 0], ("x", "y", "c"))

    parts = (NDEV, FFN_SHARD, d)
    dgp = _ffn_dhf(df, wd_g, sa_f, sb_f)
    dwd_p = _dw_rows(hf, df, hf.shape[1] // FFN_TILE, FFN_TILE, min(n, 2048), loss_p, "dw_down").reshape(parts)
    h_dwd, tok = _exchange_start(dwd_p, "scatter", loss.reshape(1, 1), "scatter_dw_down_start")
    dwg_t, dwu_t = _dw_gate_up(dgp, u2, tok)
    h_dwg, tok = _exchange_start(dwg_t.reshape(parts), "scatter", tok, "scatter_dw_gate_start")
    h_dwu, tok = _exchange_start(dwu_t.reshape(parts), "scatter", tok, "scatter_dw_up_start")
    du2 = _ffn_du2(dgp, wg_g, wu_g, tok)
    dr1, da1, acc1 = _ln1_bwd(du2, dr2, xh1, rs1, a1, ln1_g, ln1_b, sc2, g1)
    dwo_p = _dw_rows(o, da1, 2, 8 * HEAD, min(n, 1024), loss_p, "dw_out").reshape(NDEV, 2 * HEAD, d)
    h_dwo, tok = _exchange_start(dwo_p, "scatter", loss_p, "scatter_dw_out_start")
    do = _outproj_bwd(da1, wout_g, tok)
    dqa, dka, dva, dsink = _attn_window_bwd(t_all, o, do, lse_a, sink_logit, win_bias)
    dqb, dkb, dvb = _attn_global_bwd(t_all, kt_b, o, do, p_b, linv_b)
    dh_all, dnorm = _qkv_bwd_prep(dqa, dka, dva, dqb, dkb, dvb, h_all, q_norm_g, k_norm_g, cos, sa, sb)
    grad_x, acc0 = _qkv_bwd(dh_all, win_g, xs, cts, dr1, sc_pair)

    misc = _pad_cols(jnp.concatenate([dnorm[0:1], dnorm[1:2], dsink[:, 0:4, 0].reshape(1, 8)], axis=1), d)
    part = jnp.concatenate([
        acc0[0:2], acc1[4:5], acc1[1:2], acc1[0:1], acc2[2:3],
        acc0[2:4],
        acc1[2:4], acc2[0:2],
        misc, jnp.zeros((3, d), F32)], axis=0)
    gath = _exchange(part, False, "gather_small")
    dm_batch = gath[:, 0:6, :].reshape(NDEV, 6 * d)
    dm_ctx = _pad_cols(gath[:, 6:8, :].reshape(NDEV, 2 * d), 6 * d)
    dm16 = lax.dynamic_slice(jnp.concatenate([dm_batch, dm_ctx], axis=0), (0, me * e_sh), (16, e_sh))
    dw_ada, drow = _ada_bwd(dm16, c_all, w_ada[0])
    dcc = _exchange(drow, False, "gather_dcc")
    dwi_p = _dw_rows(dh_all, u_all, NDEV // 2, 2 * IN_SHARD, (n + CTX) // 2, dcc, "dw_in")
    dwi_p = dwi_p.reshape(NDEV, IN_SHARD, d)
    h_dwi, tok = _exchange_start(dwi_p, "scatter", dcc, "scatter_dw_in_start")

    w_s = _pack_small(c_ctx, b_ada, ln1_g, ln1_b, ln2_g, ln2_b, q_norm_g, k_norm_g, sink_logit, d)
    m_s = _pack_small(m_c_ctx, m_b_ada, m_ln1_g, m_ln1_b, m_ln2_g, m_ln2_b, m_q_norm_g, m_k_norm_g, m_sink_logit, d)
    v_s = _pack_small(v_c_ctx, v_b_ada, v_ln1_g, v_ln1_b, v_ln2_g, v_ln2_b, v_q_norm_g, v_k_norm_g, v_sink_logit, d)
    small = [_unpack_small(p, d) for p in _small_update(gath, dcc, c_ctx.reshape(1, d), w_s, m_s, v_s)]

    big = {}
    big["w_ada"] = _adamw(w_ada[0], dw_ada, m_w_ada[0], v_w_ada[0], "adamw_w_ada", after=tok)
    big["w_down"] = _adamw(w_down[0], _exchange_wait(h_dwd, "scatter", big["w_ada"][1], "scatter_dw_down_wait"),
                           m_w_down[0], v_w_down[0], "adamw_w_down")
    late = big["w_down"][1]
    for nm, wt, mt, vt, hd in (("w_gate", w_gate, m_w_gate, v_w_gate, h_dwg), ("w_up", w_up, m_w_up, v_w_up, h_dwu)):
        res = _adamw(wt[0].T, _exchange_wait(hd, "scatter", late, "scatter_d" + nm + "_wait"), mt[0].T, vt[0].T,
                     "adamw_" + nm)
        big[nm] = [r.T for r in res]
        late = res[1]
    big["w_out"] = _adamw(w_out[0], _exchange_wait(h_dwo, "scatter", late, "scatter_dw_out_wait"), m_w_out[0], v_w_out[0],
                          "adamw_w_out")
    big["w_in"] = _adamw_t(w_in[0], _exchange_wait(h_dwi, "scatter", big["w_out"][1], "scatter_dw_in_wait"), m_w_in[0],
                           v_w_in[0], "adamw_w_in")

    names = ["c_ctx", "w_ada", "b_ada", "w_in", "q_norm_g", "k_norm_g", "sink_logit", "w_out", "ln1_g", "ln1_b",
             "w_gate", "w_up", "w_down", "ln2_g", "ln2_b"]
    outs = [loss, grad_x[None]]
    for k in range(4):
        for nm in names:
            outs.append(big[nm][k][None] if nm in big else small[k][nm])
    return tuple(outs)
```

```python
import functools

import jax
import jax.numpy as jnp
from jax import lax
from jax.experimental import pallas as pl
from jax.experimental.pallas import tpu as pltpu

F32 = jnp.float32
BF16 = jnp.bfloat16

NDEV = 8
HEAD = 128
CTX = 256
GRID_W = 64
WINDOW = 128
WIN_KEYS = 3 * WINDOW + CTX
WIN_P = WIN_KEYS + HEAD
ROPE_THETA = 10000.0
EPS = 1e-6
SCALE = HEAD ** -0.5
LOG2E = 1.4426950408889634
QK_LOG2 = SCALE * LOG2E
ALPHA = 2.0 ** 0.25
FFN_SHARD = 704
FFN_TILE = 512
IN_SHARD = 384
NEG = -1e30

ADAM_LR = 0.001
ADAM_B1 = 0.9
ADAM_B2 = 0.999
ADAM_EPS = 1e-08
ADAM_WD = 0.01
ADAM_STEP = 10

VMEM_CAP = 56 * 1024 * 1024

_KINDS = ["rope"] * 10 + ["none"] * 2 + ["qnorm"] * 8 + ["knorm"] * 2 + ["none"] * 2
NORM_HEAD0 = _KINDS.index("qnorm")
NORM_HEADS = _KINDS.count("qnorm") + _KINDS.count("knorm")

_NT = (((1,), (1,)), ((), ()))
_TN = (((0,), (0,)), ((), ()))


def _pallas(body, **kw):
    return pl.pallas_call(body, **kw)


def _params(vmem_bytes):
    return pltpu.CompilerParams(vmem_limit_bytes=int(min(VMEM_CAP, vmem_bytes)))


def _mb(n):
    return int(n * 1024 * 1024)


def _sigmoid(x):
    return 1.0 / (1.0 + jnp.exp(-x))


def _colsum(a):
    return jnp.sum(a, axis=0, keepdims=True)


def _rowmean(a):
    return jnp.mean(a, axis=-1, keepdims=True)


def _exchange(src, scatter, name, after=None):
    blk = src.shape[1:] if scatter else src.shape
    after = src if after is None else after

    def body(src_ref, after_ref, out_ref, send_sems, recv_sems, local_sem):
        x, y, c = lax.axis_index("x"), lax.axis_index("y"), lax.axis_index("c")
        me = 4 * x + 2 * y + c
        copies = []
        for t in range(1, NDEV):
            px = 1 - x if (t >> 2) & 1 else x
            py = 1 - y if (t >> 1) & 1 else y
            pc = 1 - c if t & 1 else c
            peer = 4 * px + 2 * py + pc
            cp = pltpu.make_async_remote_copy(
                src_ref=src_ref.at[peer] if scatter else src_ref,
                dst_ref=out_ref.at[me],
                send_sem=send_sems.at[t - 1],
                recv_sem=recv_sems.at[t - 1],
                device_id=(px, py, pc),
                device_id_type=pl.DeviceIdType.MESH,
            )
            cp.start()
            copies.append(cp)
        own = pltpu.make_async_copy(src_ref.at[me] if scatter else src_ref, out_ref.at[me], local_sem)
        own.start()
        for cp in copies:
            cp.wait()
        own.wait()

    return _pallas(
        body, name=name,
        out_shape=jax.ShapeDtypeStruct((NDEV,) + tuple(blk), src.dtype),
        in_specs=[pl.BlockSpec(memory_space=pl.ANY), pl.BlockSpec(memory_space=pl.ANY)],
        out_specs=pl.BlockSpec(memory_space=pl.ANY),
        scratch_shapes=[pltpu.SemaphoreType.DMA((NDEV - 1,)), pltpu.SemaphoreType.DMA((NDEV - 1,)),
                        pltpu.SemaphoreType.DMA(())],
    )(src, after)


_HBM = pl.BlockSpec(memory_space=pltpu.HBM)
_SEM = pl.BlockSpec(memory_space=pltpu.SEMAPHORE)
_ANY = pl.BlockSpec(memory_space=pl.ANY)
_EFFECT = pltpu.SideEffectType.DATAFLOW_SIDE_EFFECTING


def _exchange_copies(src_ref, land_ref, send_sems, recv_sems, mode):
    x, y, c = lax.axis_index("x"), lax.axis_index("y"), lax.axis_index("c")
    me = 4 * x + 2 * y + c
    scatter = mode == "scatter"
    copies = []
    for t in ((1, 2, 4, 6) if mode == "chip" else range(1, NDEV)):
        px = 1 - x if (t >> 2) & 1 else x
        py = 1 - y if (t >> 1) & 1 else y
        pc = 1 - c if t & 1 else c
        peer = 4 * px + 2 * py + pc
        copies.append(pltpu.make_async_remote_copy(
            src_ref=src_ref.at[peer] if scatter else src_ref,
            dst_ref=land_ref.at[me],
            send_sem=send_sems.at[t - 1],
            recv_sem=recv_sems.at[t - 1],
            device_id=(px, py, pc),
            device_id_type=pl.DeviceIdType.MESH,
        ))
    own = pltpu.make_async_copy(src_ref.at[me] if scatter else src_ref, land_ref.at[me], send_sems.at[NDEV - 1])
    return copies, own


def _forward_copies(land_ref, send_sems, recv_sems):
    x, y, c = lax.axis_index("x"), lax.axis_index("y"), lax.axis_index("c")
    copies = []
    for k, t in enumerate((2, 4, 6)):
        px = 1 - x if (t >> 2) & 1 else x
        py = 1 - y if (t >> 1) & 1 else y
        mine, theirs = 4 * px + 2 * py + c, 4 * px + 2 * py + (1 - c)
        send = pltpu.make_async_remote_copy(
            src_ref=land_ref.at[mine], dst_ref=land_ref.at[mine], send_sem=send_sems.at[k], recv_sem=recv_sems.at[k],
            device_id=(x, y, 1 - c), device_id_type=pl.DeviceIdType.MESH)
        recv = pltpu.make_async_remote_copy(
            src_ref=land_ref.at[theirs], dst_ref=land_ref.at[theirs], send_sem=send_sems.at[k], recv_sem=recv_sems.at[k],
            device_id=(x, y, 1 - c), device_id_type=pl.DeviceIdType.MESH)
        copies.append((send, recv))
    return copies


def _forward_start(land, after, name):
    def body(land_ref, after_ref, send_sems, recv_sems, land_thru, token):
        for send, _ in _forward_copies(land_ref, send_sems, recv_sems):
            send.start()
        token[...] = jnp.zeros_like(token)

    res = _pallas(
        body, name=name,
        out_shape=(pltpu.SemaphoreType.DMA((3,)), pltpu.SemaphoreType.DMA((3,)), pltpu.HBM(land.shape, land.dtype),
                   jax.ShapeDtypeStruct((8, HEAD), F32)),
        in_specs=(_HBM, _ANY), out_specs=(_SEM, _SEM, _HBM, pl.BlockSpec(memory_space=pltpu.VMEM)),
        input_output_aliases={0: 2},
        compiler_params=pltpu.CompilerParams(has_side_effects=_EFFECT),
    )(land, after)
    return res[:3], res[3]


def _forward_wait(handle, after, name):
    send_sems, recv_sems, land_thru = handle

    def body(land_ref, send_sems, recv_sems, after_ref, got_ref):
        for send, recv in _forward_copies(land_ref, send_sems, recv_sems):
            send.wait_send()
            recv.wait_recv()

    return _pallas(
        body, name=name,
        out_shape=pltpu.HBM(land_thru.shape, land_thru.dtype),
        in_specs=(_HBM, _SEM, _SEM, _ANY), out_specs=_HBM,
        input_output_aliases={0: 0},
        compiler_params=pltpu.CompilerParams(has_side_effects=_EFFECT),
    )(land_thru, send_sems, recv_sems, after)


def _exchange_start(src, mode, after, name):
    blk = src.shape[1:] if mode == "scatter" else src.shape
    land = lax.empty((NDEV,) + tuple(blk), src.dtype)

    def body(src_ref, land_ref, after_ref, send_sems, recv_sems, src_thru, land_thru, token):
        copies, own = _exchange_copies(src_ref, land_ref, send_sems, recv_sems, mode)
        for cp in copies:
            cp.start()
        own.start()
        token[...] = jnp.zeros_like(token)

    res = _pallas(
        body, name=name,
        out_shape=(pltpu.SemaphoreType.DMA((NDEV,)), pltpu.SemaphoreType.DMA((NDEV,)),
                   pltpu.HBM(src.shape, src.dtype), pltpu.HBM(land.shape, land.dtype),
                   jax.ShapeDtypeStruct((8, HEAD), F32)),
        in_specs=(_HBM, _HBM, _ANY), out_specs=(_SEM, _SEM, _HBM, _HBM, pl.BlockSpec(memory_space=pltpu.VMEM)),
        input_output_aliases={0: 2, 1: 3},
        compiler_params=pltpu.CompilerParams(has_side_effects=_EFFECT),
    )(pltpu.with_memory_space_constraint(src, pltpu.HBM), pltpu.with_memory_space_constraint(land, pltpu.HBM), after)
    return res[:4], res[4]


def _exchange_wait(handle, mode, after, name):
    send_sems, recv_sems, src_thru, land_thru = handle

    def body(src_ref, land_ref, send_sems, recv_sems, after_ref, src_dead, got_ref):
        copies, own = _exchange_copies(src_ref, land_ref, send_sems, recv_sems, mode)
        for cp in copies:
            cp.wait_send()
            cp.wait_recv()
        own.wait()

    return _pallas(
        body, name=name,
        out_shape=(pltpu.HBM(src_thru.shape, src_thru.dtype), pltpu.HBM(land_thru.shape, land_thru.dtype)),
        in_specs=(_HBM, _HBM, _SEM, _SEM, _ANY), out_specs=(_HBM, _HBM),
        input_output_aliases={0: 0, 1: 1},
        compiler_params=pltpu.CompilerParams(has_side_effects=_EFFECT),
    )(src_thru, land_thru, send_sems, recv_sems, after)[1]


def _ada_fwd(c_all, w, bias):
    r, d = c_all.shape
    e = w.shape[1]
    tn = 512

    def body(c_ref, w_ref, b_ref, o_ref):
        cv = c_ref[...]
        s = (cv * _sigmoid(cv)).astype(BF16)
        o_ref[...] = jnp.dot(s, w_ref[...].astype(BF16), preferred_element_type=F32) + b_ref[...]

    return _pallas(
        body, name="ada_fwd", grid=(e // tn,),
        out_shape=jax.ShapeDtypeStruct((r, e), F32),
        in_specs=[pl.BlockSpec((r, d), lambda j: (0, 0)), pl.BlockSpec((d, tn), lambda j: (0, j)),
                  pl.BlockSpec((1, tn), lambda j: (0, j))],
        out_specs=pl.BlockSpec((r, tn), lambda j: (0, j)),
        compiler_params=_params(_mb(24)),
    )(c_all, w, bias)


def _ada_bwd(dm16, c_all, w):
    d, e = w.shape
    tn = 512

    def body(dm_ref, c_ref, w_ref, dw_ref, dr_ref):
        j = pl.program_id(0)
        dm = dm_ref[...]
        rid = lax.broadcasted_iota(jnp.int32, dm.shape, 0)
        ctx_sum = jnp.sum(jnp.where(rid >= 8, dm, 0.0), axis=0, keepdims=True)
        rows = jnp.where(rid < 8, dm, jnp.where(rid == 8, jnp.broadcast_to(ctx_sum, dm.shape), 0.0)).astype(BF16)
        cv = c_ref[...]
        s = (cv * _sigmoid(cv)).astype(BF16)
        dw_ref[...] = lax.dot_general(s, rows, _TN, preferred_element_type=F32)
        part = lax.dot_general(rows, w_ref[...].astype(BF16), _NT, preferred_element_type=F32)

        @pl.when(j == 0)
        def _():
            dr_ref[...] = part

        @pl.when(j > 0)
        def _():
            dr_ref[...] += part

    return _pallas(
        body, name="ada_bwd", grid=(e // tn,),
        out_shape=(jax.ShapeDtypeStruct((d, e), F32), jax.ShapeDtypeStruct((16, d), F32)),
        in_specs=[pl.BlockSpec((16, tn), lambda j: (0, j)), pl.BlockSpec((16, d), lambda j: (0, 0)),
                  pl.BlockSpec((d, tn), lambda j: (0, j))],
        out_specs=(pl.BlockSpec((d, tn), lambda j: (0, j)), pl.BlockSpec((16, d), lambda j: (0, 0))),
        compiler_params=_params(_mb(32)),
    )(dm16, c_all, w)


def _rope(v, cos, sa, sb):
    return v * cos + (pltpu.roll(v, 96, 1) * sa + pltpu.roll(v, 32, 1) * sb)


def _rope_t(dt, cos, sa, sb):
    return dt * cos + (pltpu.roll(dt * sa, 32, 1) + pltpu.roll(dt * sb, 96, 1))


def _qkv_fwd(x, ct, sc, sh, wint, qg, kg, cos, sa, sb):
    n, d = x.shape
    tm = CTX
    nlat = n // tm
    na = n + CTX
    wcols = wint.shape[0]

    def body(x_ref, ct_ref, sc_ref, sh_ref, w_ref, qg_ref, kg_ref, cos_ref, sa_ref, sb_ref, u_ref, h_ref, t_ref, kt_ref):
        i = pl.program_id(0)
        xin = jnp.where(i == nlat, ct_ref[...], x_ref[...])
        u = (xin * (1.0 + sc_ref[0]) + sh_ref[0]).astype(BF16)
        u_ref[...] = u
        cos, sa, sb = cos_ref[...], sa_ref[...], sb_ref[...]
        h = lax.dot_general(u, w_ref[...], _NT, preferred_element_type=F32)
        h_ref[...] = h[:, NORM_HEAD0 * HEAD:(NORM_HEAD0 + NORM_HEADS) * HEAD]
        for hd in range(24):
            v = h[:, hd * HEAD:(hd + 1) * HEAD]
            kind = _KINDS[hd]
            if kind == "qnorm":
                v = v * lax.rsqrt(_rowmean(v * v) + EPS) * qg_ref[...]
            elif kind == "knorm":
                v = v * lax.rsqrt(_rowmean(v * v) + EPS) * kg_ref[...]
            if kind != "none":
                v = _rope(v, cos, sa, sb)
            t_ref[:, hd * HEAD:(hd + 1) * HEAD] = v.astype(BF16)
            if kind == "knorm":
                kt_ref[(hd - 20) * HEAD:(hd - 19) * HEAD, :] = v.T.astype(BF16)

    lat = lambda i: (jnp.minimum(i, nlat - 1), 0)
    row = lambda i: (i, 0)
    const2 = lambda i: (0, 0)
    return _pallas(
        body, name="qkv_fwd", grid=(nlat + 1,),
        out_shape=(jax.ShapeDtypeStruct((na, d), BF16), jax.ShapeDtypeStruct((na, NORM_HEADS * HEAD), F32),
                   jax.ShapeDtypeStruct((na, wcols), BF16), jax.ShapeDtypeStruct((2 * HEAD, na), BF16)),
        in_specs=[pl.BlockSpec((tm, d), lat), pl.BlockSpec((tm, d), const2),
                  pl.BlockSpec((1, 1, d), lambda i: (i // nlat, 0, 0)),
                  pl.BlockSpec((1, 1, d), lambda i: (i // nlat, 0, 0)),
                  pl.BlockSpec((wcols, d), const2),
                  pl.BlockSpec((1, HEAD), const2), pl.BlockSpec((1, HEAD), const2),
                  pl.BlockSpec((tm, HEAD), row), pl.BlockSpec((tm, HEAD), row), pl.BlockSpec((tm, HEAD), row)],
        out_specs=(pl.BlockSpec((tm, d), row), pl.BlockSpec((tm, NORM_HEADS * HEAD), row), pl.BlockSpec((tm, wcols), row),
                   pl.BlockSpec((2 * HEAD, tm), lambda i: (0, i))),
        compiler_params=_params(_mb(56)),
    )(x, ct, sc, sh, wint, qg, kg, cos, sa, sb)


def _qkv_bwd_prep(dqa, dka, dva, dqb, dkb, dvb, h_norm, qg, kg, cos, sa, sb):
    na = h_norm.shape[0]
    wcols = 24 * HEAD
    n = na - CTX
    tm = CTX
    nlat = n // tm

    def body(dqa_ref, dka_ref, dva_ref, dqb_ref, dkb_ref, dvb_ref, h_ref, qg_ref, kg_ref, cos_ref, sa_ref, sb_ref,
             dh_ref, dg_ref):
        i = pl.program_id(0)

        @pl.when(i == 0)
        def _():
            dg_ref[...] = jnp.zeros_like(dg_ref)

        cos, sa, sb = cos_ref[...], sa_ref[...], sb_ref[...]
        is_lat = i < nlat
        for hd in range(24):
            kind = _KINDS[hd]
            if hd < 8:
                dt = jnp.where(is_lat, dqa_ref[:, hd * HEAD:(hd + 1) * HEAD], 0.0)
            elif hd < 10:
                dt = dka_ref[:, (hd - 8) * HEAD:(hd - 7) * HEAD]
            elif hd < 12:
                dt = dva_ref[:, (hd - 10) * HEAD:(hd - 9) * HEAD]
            elif hd < 20:
                dt = jnp.where(is_lat, dqb_ref[:, (hd - 12) * HEAD:(hd - 11) * HEAD], 0.0)
            elif hd < 22:
                dt = dkb_ref[:, (hd - 20) * HEAD:(hd - 19) * HEAD]
            else:
                dt = dvb_ref[:, (hd - 22) * HEAD:(hd - 21) * HEAD]
            if kind != "none":
                dt = _rope_t(dt, cos, sa, sb)
            if kind in ("qnorm", "knorm"):
                g_ref = qg_ref if kind == "qnorm" else kg_ref
                r0 = 0 if kind == "qnorm" else 1
                xv = h_ref[:, (hd - NORM_HEAD0) * HEAD:(hd - NORM_HEAD0 + 1) * HEAD]
                xn = xv * lax.rsqrt(_rowmean(xv * xv) + EPS)
                dg_ref[r0:r0 + 1, :] += _colsum(dt * xn)
                dxn = dt * g_ref[...]
                dt = lax.rsqrt(_rowmean(xv * xv) + EPS) * (dxn - xn * _rowmean(dxn * xn))
            dh_ref[:, hd * HEAD:(hd + 1) * HEAD] = dt.astype(BF16)

    lat = lambda i: (jnp.minimum(i, nlat - 1), 0)
    row = lambda i: (i, 0)
    const2 = lambda i: (0, 0)
    return _pallas(
        body, name="qkv_bwd_prep", grid=(nlat + 1,),
        out_shape=(jax.ShapeDtypeStruct((na, wcols), BF16), jax.ShapeDtypeStruct((8, HEAD), F32)),
        in_specs=[pl.BlockSpec((tm, 8 * HEAD), lat), pl.BlockSpec((tm, 2 * HEAD), row), pl.BlockSpec((tm, 2 * HEAD), row),
                  pl.BlockSpec((tm, 8 * HEAD), lat), pl.BlockSpec((tm, 2 * HEAD), row), pl.BlockSpec((tm, 2 * HEAD), row),
                  pl.BlockSpec((tm, NORM_HEADS * HEAD), row),
                  pl.BlockSpec((1, HEAD), const2), pl.BlockSpec((1, HEAD), const2),
                  pl.BlockSpec((tm, HEAD), row), pl.BlockSpec((tm, HEAD), row), pl.BlockSpec((tm, HEAD), row)],
        out_specs=(pl.BlockSpec((tm, wcols), row), pl.BlockSpec((8, HEAD), const2)),
        compiler_params=_params(_mb(40)),
    )(dqa, dka, dva, dqb, dkb, dvb, h_norm, qg, kg, cos, sa, sb)


def _window_keys(k_ref, v_ref, n, na):
    i = pl.program_id(1)
    tq = WINDOW
    start = pl.multiple_of(jnp.clip((i - 1) * tq, 0, n - 3 * tq), tq)
    kk = jnp.concatenate([k_ref[pl.ds(start, 3 * tq), :], k_ref[n:na, :]], axis=0)
    vv = jnp.concatenate([v_ref[pl.ds(start, 3 * tq), :], v_ref[n:na, :]], axis=0)
    return kk, vv, start


def _window_bias():
    tq = WINDOW
    r = (jnp.arange(4 * tq) % tq)[:, None]
    c = jnp.arange(3 * tq + CTX)[None, :]
    variants = []
    for back in (0, tq, 2 * tq):
        seen = (jnp.abs(back + r - c) <= WINDOW) | (c >= 3 * tq)
        variants.append(jnp.where(seen, 0.0, NEG).astype(F32))
    return jnp.stack(variants)


def _window_bias_spec(nq):
    return pl.BlockSpec((1, 4 * WINDOW, 3 * WINDOW + CTX),
                        lambda kv, i: (jnp.where(i == 0, 0, jnp.where(i == nq - 1, 2, 1)), 0, 0))


def _stack_heads(ref, width=HEAD):
    return jnp.concatenate([ref[:, g * HEAD:g * HEAD + width] for g in range(4)], axis=0)


def _sink_column(sink_ref, kv, tq):
    grp = lax.broadcasted_iota(jnp.int32, (4 * tq, 1), 0) // tq
    col = jnp.zeros((4 * tq, 1), F32)
    for g in range(4):
        col = jnp.where(grp == g, sink_ref[0, 4 * kv + g] * LOG2E, col)
    return col


def _attn_window_fwd(t_all, sink, bias, after):
    na = t_all.shape[0]
    n = na - CTX
    tq = WINDOW

    def body(sink_ref, q_ref, k_ref, v_ref, bias_ref, after_ref, o_ref, p_ref, linv_ref):
        kv = pl.program_id(0)
        kk, vv, _ = _window_keys(k_ref, v_ref, n, na)
        t = lax.dot_general(_stack_heads(q_ref), kk, _NT, preferred_element_type=F32) * QK_LOG2 + bias_ref[0]
        sk = _sink_column(sink_ref, kv, tq)
        m = jnp.maximum(jnp.max(t, axis=-1, keepdims=True), sk)
        p = jnp.exp2(t - m)
        p_sink = jnp.exp2(sk - m)
        linv = 1.0 / (jnp.sum(p, axis=-1, keepdims=True) + p_sink)
        pb = p.astype(BF16)
        o = jnp.dot(pb, vv, preferred_element_type=F32) * linv
        p_all = jnp.concatenate([pb, jnp.broadcast_to(p_sink, (4 * tq, WIN_P - WIN_KEYS)).astype(BF16)], axis=1)
        for g in range(4):
            o_ref[:, g * HEAD:(g + 1) * HEAD] = o[g * tq:(g + 1) * tq]
            p_ref[g] = p_all[g * tq:(g + 1) * tq]
            linv_ref[:, g * HEAD:(g + 1) * HEAD] = jnp.broadcast_to(linv[g * tq:(g + 1) * tq], (tq, HEAD))

    blk = pl.BlockSpec((tq, 4 * HEAD), lambda kv, i: (i, kv))
    return _pallas(
        body, name="attn_window_fwd", grid=(2, n // tq),
        out_shape=(jax.ShapeDtypeStruct((n, 16 * HEAD), F32), jax.ShapeDtypeStruct((8, n, WIN_P), BF16),
                   jax.ShapeDtypeStruct((n, 8 * HEAD), F32)),
        in_specs=[pl.BlockSpec(memory_space=pltpu.SMEM), blk,
                  pl.BlockSpec((na, HEAD), lambda kv, i: (0, 8 + kv)),
                  pl.BlockSpec((na, HEAD), lambda kv, i: (0, 10 + kv)), _window_bias_spec(n // tq), _ANY],
        out_specs=(blk, pl.BlockSpec((4, tq, WIN_P), lambda kv, i: (kv, i, 0)), blk),
        compiler_params=_params(_mb(32)),
    )(sink, t_all, t_all, t_all, bias, after)


def _attn_global_fwd(t_all, o_part):
    na = t_all.shape[0]
    n = na - CTX
    tq = 256

    def body(q_ref, k_ref, v_ref, o_in_ref, o_ref, p_ref, linv_ref):
        kk, vv = k_ref[...], v_ref[...]
        for g in range(4):
            q = q_ref[:, g * HEAD:(g + 1) * HEAD]
            t = lax.dot_general(q, kk, _NT, preferred_element_type=F32) * QK_LOG2
            m = jnp.max(t, axis=-1, keepdims=True)
            p = jnp.exp2(t - m)
            linv = 1.0 / jnp.sum(p, axis=-1, keepdims=True)
            pb = p.astype(BF16)
            p_ref[g] = pb
            o_ref[:, g * HEAD:(g + 1) * HEAD] = jnp.dot(pb, vv, preferred_element_type=F32) * linv
            linv_ref[:, g * HEAD:(g + 1) * HEAD] = jnp.broadcast_to(linv, (tq, HEAD))

    return _pallas(
        body, name="attn_global_fwd", grid=(2, n // tq),
        out_shape=(jax.ShapeDtypeStruct((n, 16 * HEAD), F32), jax.ShapeDtypeStruct((8, n, na), BF16),
                   jax.ShapeDtypeStruct((n, 8 * HEAD), F32)),
        in_specs=[pl.BlockSpec((tq, 4 * HEAD), lambda kv, i: (i, 3 + kv)),
                  pl.BlockSpec((na, HEAD), lambda kv, i: (0, 20 + kv)),
                  pl.BlockSpec((na, HEAD), lambda kv, i: (0, 22 + kv)), _ANY],
        out_specs=(pl.BlockSpec((tq, 4 * HEAD), lambda kv, i: (i, 2 + kv)),
                   pl.BlockSpec((4, tq, na), lambda kv, i: (kv, i, 0)),
                   pl.BlockSpec((tq, 4 * HEAD), lambda kv, i: (i, kv))),
        input_output_aliases={3: 0},
        compiler_params=_params(_mb(56)),
    )(t_all, t_all, t_all, o_part)


def _attn_window_bwd(t_all, o, do, p_all, linv):
    na = t_all.shape[0]
    n = na - CTX
    tq = WINDOW

    def body(q_ref, k_ref, v_ref, o_ref, do_ref, p_ref, linv_ref, dq_ref, dk_ref, dv_ref, dsink_ref):
        @pl.when(pl.program_id(1) == 0)
        def _():
            dk_ref[...] = jnp.zeros_like(dk_ref)
            dv_ref[...] = jnp.zeros_like(dv_ref)
            dsink_ref[...] = jnp.zeros_like(dsink_ref)

        kk, vv, start = _window_keys(k_ref, v_ref, n, na)
        q = _stack_heads(q_ref)
        p_full = jnp.concatenate([p_ref[g] for g in range(4)], axis=0).astype(F32) * _stack_heads(linv_ref, 1)
        p = p_full[:, :WIN_KEYS]
        dof = _stack_heads(do_ref)
        delta = jnp.sum(dof * _stack_heads(o_ref), axis=-1, keepdims=True)
        dob = dof.astype(BF16)
        dv_acc = lax.dot_general(p.astype(BF16), dob, _TN, preferred_element_type=F32)
        dp = lax.dot_general(dob, vv, _NT, preferred_element_type=F32)
        ds = (p * (dp - delta) * SCALE).astype(BF16)
        dq = jnp.dot(ds, kk, preferred_element_type=F32)
        dk_acc = lax.dot_general(ds, q, _TN, preferred_element_type=F32)
        dsk = -(p_full[:, WIN_KEYS:WIN_KEYS + 1] * delta)
        for g in range(4):
            dq_ref[:, g * HEAD:(g + 1) * HEAD] = dq[g * tq:(g + 1) * tq]
            dsink_ref[0, g:g + 1, :] += jnp.broadcast_to(_colsum(dsk[g * tq:(g + 1) * tq]), (1, HEAD))
        dk_ref[pl.ds(start, 3 * tq), :] += dk_acc[:3 * tq]
        dv_ref[pl.ds(start, 3 * tq), :] += dv_acc[:3 * tq]
        dk_ref[n:na, :] += dk_acc[3 * tq:]
        dv_ref[n:na, :] += dv_acc[3 * tq:]

    blk = pl.BlockSpec((tq, 4 * HEAD), lambda kv, i: (i, kv))
    kvout = pl.BlockSpec((na, HEAD), lambda kv, i: (0, kv))
    return _pallas(
        body, name="attn_window_bwd", grid=(2, n // tq),
        out_shape=(jax.ShapeDtypeStruct((n, 8 * HEAD), F32), jax.ShapeDtypeStruct((na, 2 * HEAD), F32),
                   jax.ShapeDtypeStruct((na, 2 * HEAD), F32), jax.ShapeDtypeStruct((2, 8, HEAD), F32)),
        in_specs=[blk,
                  pl.BlockSpec((na, HEAD), lambda kv, i: (0, 8 + kv)),
                  pl.BlockSpec((na, HEAD), lambda kv, i: (0, 10 + kv)),
                  blk, blk, pl.BlockSpec((4, tq, WIN_P), lambda kv, i: (kv, i, 0)), blk],
        out_specs=(blk, kvout, kvout, pl.BlockSpec((1, 8, HEAD), lambda kv, i: (kv, 0, 0))),
        compiler_params=_params(_mb(40)),
    )(t_all, t_all, t_all, o, do, p_all, linv)


def _attn_global_bwd(t_all, kt, o, do, p_all, linv):
    na = t_all.shape[0]
    n = na - CTX
    tq = 256

    def body(q_ref, v_ref, kt_ref, o_ref, do_ref, p_ref, linv_ref, dq_ref, dk_ref, dv_ref, dkt_acc, dvt_acc):
        i = pl.program_id(1)

        @pl.when(i == 0)
        def _():
            dkt_acc[...] = jnp.zeros_like(dkt_acc)
            dvt_acc[...] = jnp.zeros_like(dvt_acc)

        vv, kt_v = v_ref[...], kt_ref[...]
        dkt = jnp.zeros((HEAD, na), F32)
        dvt = jnp.zeros((HEAD, na), F32)
        def probs(g):
            return p_ref[g].astype(F32) * linv_ref[:, g * HEAD:g * HEAD + 1]

        def dprobs(g):
            dob = do_ref[:, g * HEAD:(g + 1) * HEAD].astype(BF16)
            return dob, lax.dot_general(dob, vv, _NT, preferred_element_type=F32)

        nxt = dprobs(0)
        for g in range(4):
            q = q_ref[:, g * HEAD:(g + 1) * HEAD]
            p = probs(g)
            dob, dp = nxt
            if g < 3:
                nxt = dprobs(g + 1)
            delta = jnp.sum(do_ref[:, g * HEAD:(g + 1) * HEAD] * o_ref[:, g * HEAD:(g + 1) * HEAD], axis=-1,
                            keepdims=True)
            dvt = dvt + lax.dot_general(dob, p.astype(BF16), _TN, preferred_element_type=F32)
            ds = (p * (dp - delta) * SCALE).astype(BF16)
            dq_ref[:, g * HEAD:(g + 1) * HEAD] = lax.dot_general(kt_v, ds, _NT, preferred_element_type=F32).T
            dkt = dkt + lax.dot_general(q, ds, _TN, preferred_element_type=F32)
        dkt_acc[...] += dkt
        dvt_acc[...] += dvt

        @pl.when(i == pl.num_programs(1) - 1)
        def _():
            dk_ref[...] = dkt_acc[...].T
            dv_ref[...] = dvt_acc[...].T

    ospec = pl.BlockSpec((tq, 4 * HEAD), lambda kv, i: (i, 2 + kv))
    lspec = pl.BlockSpec((tq, 4 * HEAD), lambda kv, i: (i, kv))
    kvout = pl.BlockSpec((na, HEAD), lambda kv, i: (0, kv))
    return _pallas(
        body, name="attn_global_bwd", grid=(2, n // tq),
        out_shape=(jax.ShapeDtypeStruct((n, 8 * HEAD), F32), jax.ShapeDtypeStruct((na, 2 * HEAD), F32),
                   jax.ShapeDtypeStruct((na, 2 * HEAD), F32)),
        in_specs=[pl.BlockSpec((tq, 4 * HEAD), lambda kv, i: (i, 3 + kv)),
                  pl.BlockSpec((na, HEAD), lambda kv, i: (0, 22 + kv)),
                  pl.BlockSpec((HEAD, na), lambda kv, i: (kv, 0)),
                  ospec, ospec, pl.BlockSpec((4, tq, na), lambda kv, i: (kv, i, 0)), lspec],
        out_specs=(lspec, kvout, kvout),
        scratch_shapes=[pltpu.VMEM((HEAD, na), F32), pltpu.VMEM((HEAD, na), F32)],
        compiler_params=_params(_mb(56)),
    )(t_all, t_all, kt, o, do, p_all, linv)


def _outproj_ln1(o, wout, x, g1, lg, lb, sc2, sh2, after):
    n, d = x.shape
    tm = 256

    def body(o_ref, w_ref, x_ref, g1_ref, lg_ref, lb_ref, sc_ref, sh_ref, after_ref, a_ref, xh_ref, rs_ref, u_ref):
        a1 = jnp.dot(o_ref[...].astype(BF16), w_ref[...], preferred_element_type=F32)
        a_ref[...] = a1.astype(BF16)
        r = ALPHA * x_ref[...] + g1_ref[...] * a1
        dlt = r - _rowmean(r)
        rstd = lax.rsqrt(_rowmean(dlt * dlt) + EPS)
        xh = dlt * rstd
        xh_ref[...] = xh
        rs_ref[...] = rstd
        x1 = xh * lg_ref[...] + lb_ref[...]
        u_ref[...] = (x1 * (1.0 + sc_ref[...]) + sh_ref[...]).astype(BF16)

    row = lambda i: (i, 0)
    const2 = lambda i: (0, 0)
    vec = pl.BlockSpec((1, d), const2)
    big = pl.BlockSpec((tm, d), row)
    return _pallas(
        body, name="outproj_ln1", grid=(n // tm,),
        out_shape=(jax.ShapeDtypeStruct((n, d), BF16), jax.ShapeDtypeStruct((n, d), F32),
                   jax.ShapeDtypeStruct((n, 1), F32), jax.ShapeDtypeStruct((n, d), BF16)),
        in_specs=[big, pl.BlockSpec((d, d), const2), big, vec, vec, vec, vec, vec, _ANY],
        out_specs=(big, big, pl.BlockSpec((tm, 1), row), big),
        compiler_params=_params(_mb(56)),
    )(o, wout, x, g1, lg, lb, sc2, sh2, after)


def _ffn_up(u2, wgt, wut, after):
    n, d = u2.shape
    f = wgt.shape[0]
    tm = min(1024, n)

    def body(u_ref, wg_ref, wu_ref, after_ref, sa_ref, sb_ref, hf_ref):
        u = u_ref[...]
        gv = lax.dot_general(u, wg_ref[...], _NT, preferred_element_type=F32)
        pv = lax.dot_general(u, wu_ref[...], _NT, preferred_element_type=F32)
        sg = _sigmoid(gv)
        silu = gv * sg
        sa_ref[...] = silu.astype(BF16)
        sb_ref[...] = (pv * (sg * (1.0 + gv * (1.0 - sg)))).astype(BF16)
        hf_ref[...] = (silu * pv).astype(BF16)

    tile = pl.BlockSpec((tm, FFN_TILE), lambda i, j: (i, j))
    wspec = pl.BlockSpec((FFN_TILE, d), lambda i, j: (j, 0))
    sds = jax.ShapeDtypeStruct((n, f), BF16)
    return _pallas(
        body, name="ffn_up", grid=(n // tm, f // FFN_TILE),
        out_shape=(sds, sds, sds),
        in_specs=[pl.BlockSpec((tm, d), lambda i, j: (i, 0)), wspec, wspec, _ANY],
        out_specs=(tile, tile, tile),
        compiler_params=_params(_mb(48)),
    )(u2, wgt, wut, after)


def _ffn_down(hf, wd):
    n, f = hf.shape
    d = wd.shape[1]
    tm, tn = min(1024, n), 512

    def body(h_ref, w_ref, o_ref):
        o_ref[...] = jnp.dot(h_ref[...], w_ref[...], preferred_element_type=F32)

    return _pallas(
        body, name="ffn_down", grid=(n // tm, d // tn),
        out_shape=jax.ShapeDtypeStruct((n, d), F32),
        in_specs=[pl.BlockSpec((tm, f), lambda i, j: (i, 0)), pl.BlockSpec((f, tn), lambda i, j: (0, j))],
        out_specs=pl.BlockSpec((tm, tn), lambda i, j: (i, j)),
        compiler_params=_params(_mb(56)),
    )(hf, wd)


def _ln2_loss(xh1, ffn, tgt, lg1, lb1, g2, lg2, lb2):
    n, d = xh1.shape
    tm = 256

    def body(xh_ref, f_ref, t_ref, lg1_ref, lb1_ref, g2_ref, lg2_ref, lb2_ref, dr_ref, df_ref, loss_ref, acc_ref):
        @pl.when(pl.program_id(0) == 0)
        def _():
            loss_ref[...] = jnp.zeros_like(loss_ref)
            acc_ref[...] = jnp.zeros_like(acc_ref)

        x1 = xh_ref[...] * lg1_ref[...] + lb1_ref[...]
        fv = f_ref[...]
        r = ALPHA * x1 + g2_ref[...] * fv
        dlt = r - _rowmean(r)
        rstd = lax.rsqrt(_rowmean(dlt * dlt) + EPS)
        xh2 = dlt * rstd
        err = xh2 * lg2_ref[...] + lb2_ref[...] - t_ref[...]
        loss_ref[...] += 0.5 * jnp.sum(_rowmean(err * err))
        dy = err * (1.0 / d)
        dyg = dy * lg2_ref[...]
        dr = rstd * (dyg - _rowmean(dyg) - xh2 * _rowmean(dyg * xh2))
        dr_ref[...] = dr
        df_ref[...] = (g2_ref[...] * dr).astype(BF16)
        acc_ref[0:1, :] += _colsum(dy * xh2)
        acc_ref[1:2, :] += _colsum(dy)
        acc_ref[2:3, :] += _colsum(dr * fv)

    row = lambda i: (i, 0)
    const2 = lambda i: (0, 0)
    vec = pl.BlockSpec((1, d), const2)
    big = pl.BlockSpec((tm, d), row)
    return _pallas(
        body, name="ln2_loss", grid=(n // tm,),
        out_shape=(jax.ShapeDtypeStruct((n, d), F32), jax.ShapeDtypeStruct((n, d), BF16),
                   jax.ShapeDtypeStruct((8, HEAD), F32), jax.ShapeDtypeStruct((8, d), F32)),
        in_specs=[big, big, big, vec, vec, vec, vec, vec],
        out_specs=(big, big, pl.BlockSpec((8, HEAD), const2), pl.BlockSpec((8, d), const2)),
        compiler_params=_params(_mb(48)),
    )(xh1, ffn, tgt, lg1, lb1, g2, lg2, lb2)


def _ffn_dhf(df, wd, sa, sb):
    n, d = df.shape
    f = sa.shape[1]
    tm = min(2048, n)

    def body(df_ref, w_ref, sa_ref, sb_ref, dgp_ref):
        dhf = lax.dot_general(df_ref[...], w_ref[...], _NT, preferred_element_type=F32)
        dgp_ref[:, :FFN_TILE] = (dhf * sb_ref[...].astype(F32)).astype(BF16)
        dgp_ref[:, FFN_TILE:] = (dhf * sa_ref[...].astype(F32)).astype(BF16)

    tile = pl.BlockSpec((tm, FFN_TILE), lambda i, j: (i, j))
    return _pallas(
        body, name="ffn_dhf", grid=(n // tm, f // FFN_TILE),
        out_shape=jax.ShapeDtypeStruct((n, 2 * f), BF16),
        in_specs=[pl.BlockSpec((tm, d), lambda i, j: (i, 0)), pl.BlockSpec((FFN_TILE, d), lambda i, j: (j, 0)),
                  tile, tile],
        out_specs=pl.BlockSpec((tm, 2 * FFN_TILE), lambda i, j: (i, j)),
        compiler_params=_params(_mb(48)),
    )(df, wd, sa, sb)


def _ffn_du2(dgp, wgt, wut, after):
    n = dgp.shape[0]
    f, d = wgt.shape
    tm = min(1024, n)

    def body(dgp_ref, wg_ref, wu_ref, after_ref, o_ref):
        w = jnp.concatenate([wg_ref[...], wu_ref[...]], axis=0)
        part = jnp.dot(dgp_ref[...], w, preferred_element_type=F32)

        @pl.when(pl.program_id(1) == 0)
        def _():
            o_ref[...] = part

        @pl.when(pl.program_id(1) > 0)
        def _():
            o_ref[...] += part

    wspec = pl.BlockSpec((FFN_TILE, d), lambda i, j: (j, 0))
    return _pallas(
        body, name="ffn_du2", grid=(n // tm, f // FFN_TILE),
        out_shape=jax.ShapeDtypeStruct((n, d), F32),
        in_specs=[pl.BlockSpec((tm, 2 * FFN_TILE), lambda i, j: (i, j)), wspec, wspec, _ANY],
        out_specs=pl.BlockSpec((tm, d), lambda i, j: (i, 0)),
        compiler_params=_params(_mb(48)),
    )(dgp, wgt, wut, after)


def _dw_gate_up(dgp, u2, after):
    n, d = u2.shape
    f = dgp.shape[1] // 2
    tm = min(2048, n)

    def body(a_ref, b_ref, after_ref, og_ref, ou_ref, acc_ref):
        part = lax.dot_general(a_ref[...], b_ref[...], _TN, preferred_element_type=F32)
        i = pl.program_id(1)

        @pl.when(i == 0)
        def _():
            acc_ref[...] = part

        @pl.when(i > 0)
        def _():
            acc_ref[...] += part

        @pl.when(i == pl.num_programs(1) - 1)
        def _():
            og_ref[...] = acc_ref[:FFN_TILE].astype(BF16)
            ou_ref[...] = acc_ref[FFN_TILE:].astype(BF16)

    out = pl.BlockSpec((FFN_TILE, d), lambda j, i: (j, 0))
    sds = jax.ShapeDtypeStruct((f, d), BF16)
    return _pallas(
        body, name="dw_gate_up", grid=(f // FFN_TILE, n // tm),
        out_shape=(sds, sds),
        in_specs=[pl.BlockSpec((tm, 2 * FFN_TILE), lambda j, i: (i, j)), pl.BlockSpec((tm, d), lambda j, i: (i, 0)), _ANY],
        out_specs=(out, out),
        scratch_shapes=[pltpu.VMEM((2 * FFN_TILE, d), F32)],
        compiler_params=_params(_mb(56)),
    )(dgp, u2, after)


def _ln1_bwd(du2, dr2, xh1, rs1, a1, lg1, lb1, sc2, g1):
    n, d = du2.shape
    tm = 256

    def body(du_ref, dr2_ref, xh_ref, rs_ref, a_ref, lg_ref, lb_ref, sc_ref, g1_ref, dr1_ref, da_ref, acc_ref):
        @pl.when(pl.program_id(0) == 0)
        def _():
            acc_ref[...] = jnp.zeros_like(acc_ref)

        du = du_ref[...]
        xh = xh_ref[...]
        x1 = xh * lg_ref[...] + lb_ref[...]
        dx1 = ALPHA * dr2_ref[...] + du * (1.0 + sc_ref[...])
        dxg = dx1 * lg_ref[...]
        dr1 = rs_ref[...] * (dxg - _rowmean(dxg) - xh * _rowmean(dxg * xh))
        dr1_ref[...] = dr1
        da_ref[...] = (g1_ref[...] * dr1).astype(BF16)
        acc_ref[0:1, :] += _colsum(du * x1)
        acc_ref[1:2, :] += _colsum(du)
        acc_ref[2:3, :] += _colsum(dx1 * xh)
        acc_ref[3:4, :] += _colsum(dx1)
        acc_ref[4:5, :] += _colsum(dr1 * a_ref[...].astype(F32))

    row = lambda i: (i, 0)
    const2 = lambda i: (0, 0)
    vec = pl.BlockSpec((1, d), const2)
    big = pl.BlockSpec((tm, d), row)
    return _pallas(
        body, name="ln1_bwd", grid=(n // tm,),
        out_shape=(jax.ShapeDtypeStruct((n, d), F32), jax.ShapeDtypeStruct((n, d), BF16),
                   jax.ShapeDtypeStruct((8, d), F32)),
        in_specs=[big, big, big, pl.BlockSpec((tm, 1), row), big, vec, vec, vec, vec],
        out_specs=(big, big, pl.BlockSpec((8, d), const2)),
        compiler_params=_params(_mb(48)),
    )(du2, dr2, xh1, rs1, a1, lg1, lb1, sc2, g1)


def _dw_rows(a, b, nblk, bw, tm, after, name):
    m = a.shape[0]
    nn = b.shape[1]

    def body(a_ref, b_ref, after_ref, o_ref, acc_ref):
        part = lax.dot_general(a_ref[...].astype(BF16), b_ref[...], _TN, preferred_element_type=F32)
        i = pl.program_id(1)

        @pl.when(i == 0)
        def _():
            acc_ref[...] = part

        @pl.when(i > 0)
        def _():
            acc_ref[...] += part

        @pl.when(i == pl.num_programs(1) - 1)
        def _():
            o_ref[0] = acc_ref[...].astype(BF16)

    return _pallas(
        body, name=name, grid=(nblk, m // tm),
        out_shape=jax.ShapeDtypeStruct((nblk, bw, nn), BF16),
        in_specs=[pl.BlockSpec((tm, bw), lambda j, i: (i, j)), pl.BlockSpec((tm, nn), lambda j, i: (i, 0)), _ANY],
        out_specs=pl.BlockSpec((1, bw, nn), lambda j, i: (j, 0, 0)),
        scratch_shapes=[pltpu.VMEM((bw, nn), F32)],
        compiler_params=_params(_mb(56)),
    )(a, b, after)


def _outproj_bwd(da1, wout, after):
    n, d = da1.shape
    tm = 512

    def body(a_ref, w_ref, after_ref, o_ref):
        o_ref[...] = lax.dot_general(a_ref[...], w_ref[...], _NT, preferred_element_type=F32)

    return _pallas(
        body, name="outproj_bwd", grid=(n // tm,),
        out_shape=jax.ShapeDtypeStruct((n, d), F32),
        in_specs=[pl.BlockSpec((tm, d), lambda i: (i, 0)), pl.BlockSpec((d, d), lambda i: (0, 0)), _ANY],
        out_specs=pl.BlockSpec((tm, d), lambda i: (i, 0)),
        compiler_params=_params(_mb(48)),
    )(da1, wout, after)


def _qkv_bwd(dh, wint, x, ct, dr1, sc):
    na, wcols = dh.shape
    n, d = x.shape
    tm = CTX
    nlat = n // tm

    def body(dh_ref, w_ref, x_ref, ct_ref, dr_ref, sc_ref, gx_ref, acc_ref):
        i = pl.program_id(0)

        @pl.when(i == 0)
        def _():
            acc_ref[...] = jnp.zeros_like(acc_ref)

        du = jnp.dot(dh_ref[...], w_ref[...], preferred_element_type=F32)

        @pl.when(i < nlat)
        def _():
            gx_ref[...] = ALPHA * dr_ref[...] + du * (1.0 + sc_ref[0])
            acc_ref[0:1, :] += _colsum(du)
            acc_ref[1:2, :] += _colsum(du * x_ref[...])

        @pl.when(i == nlat)
        def _():
            acc_ref[2:3, :] += _colsum(du)
            acc_ref[3:4, :] += _colsum(du * ct_ref[...])

    lat = lambda i: (jnp.minimum(i, nlat - 1), 0)
    const2 = lambda i: (0, 0)
    return _pallas(
        body, name="qkv_bwd", grid=(nlat + 1,),
        out_shape=(jax.ShapeDtypeStruct((n, d), F32), jax.ShapeDtypeStruct((8, d), F32)),
        in_specs=[pl.BlockSpec((tm, wcols), lambda i: (i, 0)), pl.BlockSpec((wcols, d), const2),
                  pl.BlockSpec((tm, d), lat), pl.BlockSpec((tm, d), const2), pl.BlockSpec((tm, d), lat),
                  pl.BlockSpec((1, 1, d), lambda i: (0, 0, 0))],
        out_specs=(pl.BlockSpec((tm, d), lat), pl.BlockSpec((8, d), const2)),
        compiler_params=_params(_mb(56)),
    )(dh, wint, x, ct, dr1, sc)


def _adam_math(w, g, m, v):
    m2 = ADAM_B1 * m + (1.0 - ADAM_B1) * g
    v2 = ADAM_B2 * v + (1.0 - ADAM_B2) * (g * g)
    m_hat = m2 * (1.0 / (1.0 - ADAM_B1 ** ADAM_STEP))
    v_hat = v2 * (1.0 / (1.0 - ADAM_B2 ** ADAM_STEP))
    delta = -ADAM_LR * (m_hat / (jnp.sqrt(v_hat) + ADAM_EPS) + ADAM_WD * w)
    return delta, m2, v2


def _adamw(w, gsrc, m, v, name, after=None):
    r, c = w.shape
    parts = gsrc.ndim == 3
    after = w if after is None else after
    tr = r
    while tr * c * 4 > _mb(1) and tr % 32 == 0:
        tr //= 2

    def body(w_ref, g_ref, m_ref, v_ref, after_ref, go_ref, d_ref, mo_ref, vo_ref):
        if parts:
            g = g_ref[0].astype(F32)
            for s in range(1, NDEV):
                g = g + g_ref[s].astype(F32)
        else:
            g = g_ref[...]
        delta, m2, v2 = _adam_math(w_ref[...], g, m_ref[...], v_ref[...])
        go_ref[...] = g
        d_ref[...] = delta
        mo_ref[...] = m2
        vo_ref[...] = v2

    tile = pl.BlockSpec((tr, c), lambda i: (i, 0))
    gspec = pl.BlockSpec((NDEV, tr, c), lambda i: (0, i, 0)) if parts else tile
    sds = jax.ShapeDtypeStruct((r, c), F32)
    return _pallas(
        body, name=name, grid=(r // tr,),
        out_shape=(sds, sds, sds, sds),
        in_specs=[tile, gspec, tile, tile, _ANY],
        out_specs=(tile, tile, tile, tile),
        compiler_params=_params(_mb(48)),
    )(w, gsrc, m, v, after)


def _adamw_t(w, gsrc_t, m, v, name):
    r, c = w.shape
    tr = 256

    def body(w_ref, g_ref, m_ref, v_ref, go_ref, d_ref, mo_ref, vo_ref):
        gt = g_ref[0].astype(F32)
        for s in range(1, NDEV):
            gt = gt + g_ref[s].astype(F32)
        g = gt.T
        delta, m2, v2 = _adam_math(w_ref[...], g, m_ref[...], v_ref[...])
        go_ref[...] = g
        d_ref[...] = delta
        mo_ref[...] = m2
        vo_ref[...] = v2

    tile = pl.BlockSpec((tr, c), lambda i: (i, 0))
    sds = jax.ShapeDtypeStruct((r, c), F32)
    return _pallas(
        body, name=name, grid=(r // tr,),
        out_shape=(sds, sds, sds, sds),
        in_specs=[tile, pl.BlockSpec((NDEV, c, tr), lambda i: (0, 0, i)), tile, tile],
        out_specs=(tile, tile, tile, tile),
        compiler_params=_params(_mb(48)),
    )(w, gsrc_t, m, v)


def _small_update(gath, dcc, cc, w_s, m_s, v_s):
    d = w_s.shape[1]

    def body(g_ref, dcc_ref, cc_ref, w_ref, m_ref, v_ref, go_ref, d_ref, mo_ref, vo_ref):
        s = g_ref[0]
        for b in range(1, NDEV):
            s = s + g_ref[b]
        dsl = dcc_ref[0, 8:9, :]
        for b in range(1, NDEV):
            dsl = dsl + dcc_ref[b, 8:9, :]
        cv = cc_ref[...]
        sg = _sigmoid(cv)
        go_ref[...] = jnp.zeros_like(go_ref)
        go_ref[0:1, :] = dsl * (sg * (1.0 + cv * (1.0 - sg)))
        go_ref[1:3, :] = s[0:2] + s[6:8]
        go_ref[3:7, :] = s[2:6]
        go_ref[7:12, :] = s[8:13]
        delta, m2, v2 = _adam_math(w_ref[...], go_ref[...], m_ref[...], v_ref[...])
        d_ref[...] = delta
        mo_ref[...] = m2
        vo_ref[...] = v2

    full = pl.BlockSpec((16, d), lambda: (0, 0))
    g3 = pl.BlockSpec((NDEV, 16, d), lambda: (0, 0, 0))
    sds = jax.ShapeDtypeStruct((16, d), F32)
    return _pallas(
        body, name="small_update",
        out_shape=(sds, sds, sds, sds),
        in_specs=[g3, g3, pl.BlockSpec((1, d), lambda: (0, 0)), full, full, full],
        out_specs=(full, full, full, full),
        compiler_params=_params(_mb(24)),
    )(gath, dcc, cc, w_s, m_s, v_s)


def _rope_tables(n):
    rows = n // GRID_W
    row_ids = jnp.repeat(jnp.arange(rows, dtype=F32), GRID_W)
    col_ids = jnp.tile(jnp.arange(GRID_W, dtype=F32), rows)
    axis_dim = HEAD // 2
    inv_freq = jnp.power(ROPE_THETA, -jnp.arange(0, axis_dim, 2, dtype=F32) / axis_dim)
    ang_r = row_ids[:, None] * inv_freq
    ang_c = col_ids[:, None] * inv_freq
    ang = jnp.concatenate([ang_r, ang_r, ang_c, ang_c], axis=-1)
    cos, sin = jnp.cos(ang), jnp.sin(ang)
    first = (jnp.arange(HEAD) % (HEAD // 2)) < HEAD // 4
    sa = jnp.where(first, -sin, 0.0)
    sb = jnp.where(first, 0.0, sin)
    ones = jnp.ones((CTX, HEAD), F32)
    zeros = jnp.zeros((CTX, HEAD), F32)
    return (jnp.concatenate([cos, ones], 0), jnp.concatenate([sa, zeros], 0), jnp.concatenate([sb, zeros], 0))


def _pad_cols(a, width):
    return jnp.pad(a, ((0, 0), (0, width - a.shape[1])))


def _pad_rows(a, rows):
    return jnp.pad(a, ((0, rows - a.shape[0]), (0, 0)))


def _pack_small(c_ctx, b_ada, ln1_g, ln1_b, ln2_g, ln2_b, qg, kg, sink, d):
    misc = _pad_cols(jnp.concatenate([qg, kg, sink], axis=1), d)
    rows = jnp.concatenate([c_ctx.reshape(1, d), b_ada.reshape(6, d), ln1_g, ln1_b, ln2_g, ln2_b, misc], axis=0)
    return _pad_rows(rows, 16)


def _unpack_small(p, d):
    return dict(c_ctx=p[0], b_ada=p[1:7].reshape(1, 6 * d), ln1_g=p[7:8], ln1_b=p[8:9], ln2_g=p[9:10], ln2_b=p[10:11],
                q_norm_g=p[11:12, 0:HEAD], k_norm_g=p[11:12, HEAD:2 * HEAD], sink_logit=p[11:12, 2 * HEAD:2 * HEAD + 8])


def kernel(x, c, ctx, c_ctx, w_ada, b_ada, w_in, q_norm_g, k_norm_g, sink_logit, w_out, ln1_g, ln1_b, w_gate, w_up, w_down, ln2_g, ln2_b, loss_target, m_c_ctx, m_w_ada, m_b_ada, m_w_in, m_q_norm_g, m_k_norm_g, m_sink_logit, m_w_out, m_ln1_g, m_ln1_b, m_w_gate, m_w_up, m_w_down, m_ln2_g, m_ln2_b, v_c_ctx, v_w_ada, v_b_ada, v_w_in, v_q_norm_g, v_k_norm_g, v_sink_logit, v_w_out, v_ln1_g, v_ln1_b, v_w_gate, v_w_up, v_w_down, v_ln2_g, v_ln2_b):
    xs, cts, tgt = x[0], ctx[0], loss_target[0]
    n, d = xs.shape
    assert cts.shape == (CTX, d) and w_in.shape[2] == IN_SHARD and w_gate.shape[2] == FFN_SHARD
    me = 4 * lax.axis_index("x") + 2 * lax.axis_index("y") + lax.axis_index("c")
    e_sh = w_ada.shape[2]

    c_g = _exchange(_pad_rows(c, 8), False, "gather_c")
    c_all = jnp.concatenate([c_g[:, 0, :], _pad_rows(c_ctx.reshape(1, d), 8)], axis=0)
    bias_sh = lax.dynamic_slice(b_ada, (0, me * e_sh), (1, e_sh))
    mods_g = _exchange(_ada_fwd(c_all, w_ada[0], bias_sh), False, "gather_mods")
    mods = jnp.transpose(mods_g, (1, 0, 2)).reshape(16, NDEV * e_sh)
    mine = lax.dynamic_slice(mods, (me, 0), (1, 6 * d))
    sh1, sc1, g1, sh2, sc2, g2 = [mine[:, k * d:(k + 1) * d] for k in range(6)]
    csh1, csc1 = mods[8:9, 0:d], mods[8:9, d:2 * d]
    sc_pair = jnp.stack([sc1, csc1])
    sh_pair = jnp.stack([sh1, csh1])

    h_win, tok = _exchange_start(w_in[0].T.astype(BF16), "chip", mods, "gather_w_in_start")
    tok, (wo_l, wg_l, wu_l, wd_l) = lax.optimization_barrier((tok, (w_out, w_gate, w_up, w_down)))
    h_wout, tok = _exchange_start(wo_l[0].astype(BF16), "chip", tok, "gather_w_out_start")
    h_wg, tok = _exchange_start(wg_l[0].T.astype(BF16), "chip", tok, "gather_w_gate_start")
    h_wu, tok = _exchange_start(wu_l[0].T.astype(BF16), "chip", tok, "gather_w_up_start")
    h_wd, tok = _exchange_start(wd_l[0].astype(BF16), "chip", tok, "gather_w_down_start")

    cos, sa, sb = _rope_tables(n)
    f_win, tok = _forward_start(_exchange_wait(h_win, "chip", tok, "gather_w_in_wait"), tok, "forward_w_in_start")
    win_g = _forward_wait(f_win, tok, "forward_w_in_wait").reshape(NDEV * IN_SHARD, d)
    u_all, h_all, t_all, kt_b = _qkv_fwd(xs, cts, sc_pair, sh_pair, win_g, q_norm_g, k_norm_g, cos, sa, sb)
    f_wout, tok = _forward_start(_exchange_wait(h_wout, "chip", t_all, "gather_w_out_wait"), t_all, "forward_w_out_start")
    o_a, p_a, linv_a = _attn_window_fwd(t_all, sink_logit, _window_bias(), tok)
    o, p_b, linv_b = _attn_global_fwd(t_all, o_a)
    f_wg, tok = _forward_start(_exchange_wait(h_wg, "chip", o, "gather_w_gate_wait"), o, "forward_w_gate_start")
    f_wu, tok = _forward_start(_exchange_wait(h_wu, "chip", tok, "gather_w_up_wait"), tok, "forward_w_up_start")
    wout_g = _forward_wait(f_wout, tok, "forward_w_out_wait").reshape(d, d)
    a1, xh1, rs1, u2 = _outproj_ln1(o, wout_g, xs, g1, ln1_g, ln1_b, sc2, sh2, tok)
    f_wd, tok = _forward_start(_exchange_wait(h_wd, "chip", rs1, "gather_w_down_wait"), rs1, "forward_w_down_start")
    ffn_w = (NDEV * FFN_SHARD, d)
    wg_g = _forward_wait(f_wg, tok, "forward_w_gate_wait").reshape(ffn_w)
    wu_g = _forward_wait(f_wu, tok, "forward_w_up_wait").reshape(ffn_w)
    sa_f, sb_f, hf = _ffn_up(u2, wg_g, wu_g, tok)
    wd_g = _forward_wait(f_wd, hf, "forward_w_down_wait").reshape(ffn_w)
    ffn = _ffn_down(hf, wd_g)
    dr2, df, loss_p, acc2 = _ln2_loss(xh1, ffn, tgt, ln1_g, ln1_b, g2, ln2_g, ln2_b)
    loss = lax.psum(loss_p[0, 0], ("x", "y", "c"))

    parts = (NDEV, FFN_SHARD, d)
    dgp = _ffn_dhf(df, wd_g, sa_f, sb_f)
    dwd_p = _dw_rows(hf, df, hf.shape[1] // FFN_TILE, FFN_TILE, min(n, 2048), loss_p, "dw_down").reshape(parts)
    h_dwd, tok = _exchange_start(dwd_p, "scatter", loss.reshape(1, 1), "scatter_dw_down_start")
    dwg_t, dwu_t = _dw_gate_up(dgp, u2, tok)
    h_dwg, tok = _exchange_start(dwg_t.reshape(parts), "scatter", tok, "scatter_dw_gate_start")
    h_dwu, tok = _exchange_start(dwu_t.reshape(parts), "scatter", tok, "scatter_dw_up_start")
    du2 = _ffn_du2(dgp, wg_g, wu_g, tok)
    dr1, da1, acc1 = _ln1_bwd(du2, dr2, xh1, rs1, a1, ln1_g, ln1_b, sc2, g1)
    dwo_p = _dw_rows(o, da1, 2, 8 * HEAD, min(n, 1024), loss_p, "dw_out").reshape(NDEV, 2 * HEAD, d)
    h_dwo, tok = _exchange_start(dwo_p, "scatter", loss_p, "scatter_dw_out_start")
    do = _outproj_bwd(da1, wout_g, tok)
    dqa, dka, dva, dsink = _attn_window_bwd(t_all, o, do, p_a, linv_a)
    dqb, dkb, dvb = _attn_global_bwd(t_all, kt_b, o, do, p_b, linv_b)
    dh_all, dnorm = _qkv_bwd_prep(dqa, dka, dva, dqb, dkb, dvb, h_all, q_norm_g, k_norm_g, cos, sa, sb)
    grad_x, acc0 = _qkv_bwd(dh_all, win_g, xs, cts, dr1, sc_pair)

    misc = _pad_cols(jnp.concatenate([dnorm[0:1], dnorm[1:2], dsink[:, 0:4, 0].reshape(1, 8)], axis=1), d)
    part = jnp.concatenate([
        acc0[0:2], acc1[4:5], acc1[1:2], acc1[0:1], acc2[2:3],
        acc0[2:4],
        acc1[2:4], acc2[0:2],
        misc, jnp.zeros((3, d), F32)], axis=0)
    gath = _exchange(part, False, "gather_small")
    dm_batch = gath[:, 0:6, :].reshape(NDEV, 6 * d)
    dm_ctx = _pad_cols(gath[:, 6:8, :].reshape(NDEV, 2 * d), 6 * d)
    dm16 = lax.dynamic_slice(jnp.concatenate([dm_batch, dm_ctx], axis=0), (0, me * e_sh), (16, e_sh))
    dw_ada, drow = _ada_bwd(dm16, c_all, w_ada[0])
    dcc = _exchange(drow, False, "gather_dcc")
    dwi_p = _dw_rows(dh_all, u_all, NDEV // 2, 2 * IN_SHARD, (n + CTX) // 2, dcc, "dw_in")
    dwi_p = dwi_p.reshape(NDEV, IN_SHARD, d)
    h_dwi, tok = _exchange_start(dwi_p, "scatter", dcc, "scatter_dw_in_start")

    w_s = _pack_small(c_ctx, b_ada, ln1_g, ln1_b, ln2_g, ln2_b, q_norm_g, k_norm_g, sink_logit, d)
    m_s = _pack_small(m_c_ctx, m_b_ada, m_ln1_g, m_ln1_b, m_ln2_g, m_ln2_b, m_q_norm_g, m_k_norm_g, m_sink_logit, d)
    v_s = _pack_small(v_c_ctx, v_b_ada, v_ln1_g, v_ln1_b, v_ln2_g, v_ln2_b, v_q_norm_g, v_k_norm_g, v_sink_logit, d)
    small = [_unpack_small(p, d) for p in _small_update(gath, dcc, c_ctx.reshape(1, d), w_s, m_s, v_s)]

    big = {}
    big["w_ada"] = _adamw(w_ada[0], dw_ada, m_w_ada[0], v_w_ada[0], "adamw_w_ada", after=tok)
    big["w_down"] = _adamw(w_down[0], _exchange_wait(h_dwd, "scatter", big["w_ada"][1], "scatter_dw_down_wait"),
                           m_w_down[0], v_w_down[0], "adamw_w_down")
    late = big["w_down"][1]
    for nm, wt, mt, vt, hd in (("w_gate", w_gate, m_w_gate, v_w_gate, h_dwg), ("w_up", w_up, m_w_up, v_w_up, h_dwu)):
        res = _adamw(wt[0].T, _exchange_wait(hd, "scatter", late, "scatter_d" + nm + "_wait"), mt[0].T, vt[0].T,
                     "adamw_" + nm)
        big[nm] = [r.T for r in res]
        late = res[1]
    big["w_out"] = _adamw(w_out[0], _exchange_wait(h_dwo, "scatter", late, "scatter_dw_out_wait"), m_w_out[0], v_w_out[0],
                          "adamw_w_out")
    big["w_in"] = _adamw_t(w_in[0], _exchange_wait(h_dwi, "scatter", big["w_out"][1], "scatter_dw_in_wait"), m_w_in[0],
                           v_w_in[0], "adamw_w_in")

    names = ["c_ctx", "w_ada", "b_ada", "w_in", "q_norm_g", "k_norm_g", "sink_logit", "w_out", "ln1_g", "ln1_b",
             "w_gate", "w_up", "w_down", "ln2_g", "ln2_b"]
    outs = [loss, grad_x[None]]
    for k in range(4):
        for nm in names:
            outs.append(big[nm][k][None] if nm in big else small[k][nm])
    return tuple(outs)
```

```python
import functools

import jax
import jax.numpy as jnp
from jax import lax
from jax.experimental import pallas as pl
from jax.experimental.pallas import tpu as pltpu

F32 = jnp.float32
BF16 = jnp.bfloat16

NDEV = 8
HEAD = 128
CTX = 256
GRID_W = 64
WINDOW = 128
WIN_KEYS = 3 * WINDOW + CTX
WIN_P = WIN_KEYS + HEAD
ROPE_THETA = 10000.0
EPS = 1e-6
SCALE = HEAD ** -0.5
LOG2E = 1.4426950408889634
QK_LOG2 = SCALE * LOG2E
ALPHA = 2.0 ** 0.25
FFN_SHARD = 704
FFN_TILE = 512
IN_SHARD = 384
NEG = -1e30

ADAM_LR = 0.001
ADAM_B1 = 0.9
ADAM_B2 = 0.999
ADAM_EPS = 1e-08
ADAM_WD = 0.01
ADAM_STEP = 10

VMEM_CAP = 56 * 1024 * 1024

_KINDS = ["rope"] * 10 + ["none"] * 2 + ["qnorm"] * 8 + ["knorm"] * 2 + ["none"] * 2
NORM_HEAD0 = _KINDS.index("qnorm")
NORM_HEADS = _KINDS.count("qnorm") + _KINDS.count("knorm")

_NT = (((1,), (1,)), ((), ()))
_TN = (((0,), (0,)), ((), ()))


def _pallas(body, **kw):
    return pl.pallas_call(body, **kw)


def _params(vmem_bytes):
    return pltpu.CompilerParams(vmem_limit_bytes=int(min(VMEM_CAP, vmem_bytes)))


def _mb(n):
    return int(n * 1024 * 1024)


def _sigmoid(x):
    return 1.0 / (1.0 + jnp.exp(-x))


def _colsum(a):
    return jnp.sum(a, axis=0, keepdims=True)


def _rowmean(a):
    return jnp.mean(a, axis=-1, keepdims=True)


def _exchange(src, scatter, name, after=None):
    blk = src.shape[1:] if scatter else src.shape
    after = src if after is None else after

    def body(src_ref, after_ref, out_ref, send_sems, recv_sems, local_sem):
        x, y, c = lax.axis_index("x"), lax.axis_index("y"), lax.axis_index("c")
        me = 4 * x + 2 * y + c
        copies = []
        for t in range(1, NDEV):
            px = 1 - x if (t >> 2) & 1 else x
            py = 1 - y if (t >> 1) & 1 else y
            pc = 1 - c if t & 1 else c
            peer = 4 * px + 2 * py + pc
            cp = pltpu.make_async_remote_copy(
                src_ref=src_ref.at[peer] if scatter else src_ref,
                dst_ref=out_ref.at[me],
                send_sem=send_sems.at[t - 1],
                recv_sem=recv_sems.at[t - 1],
                device_id=(px, py, pc),
                device_id_type=pl.DeviceIdType.MESH,
            )
            cp.start()
            copies.append(cp)
        own = pltpu.make_async_copy(src_ref.at[me] if scatter else src_ref, out_ref.at[me], local_sem)
        own.start()
        for cp in copies:
            cp.wait()
        own.wait()

    return _pallas(
        body, name=name,
        out_shape=jax.ShapeDtypeStruct((NDEV,) + tuple(blk), src.dtype),
        in_specs=[pl.BlockSpec(memory_space=pl.ANY), pl.BlockSpec(memory_space=pl.ANY)],
        out_specs=pl.BlockSpec(memory_space=pl.ANY),
        scratch_shapes=[pltpu.SemaphoreType.DMA((NDEV - 1,)), pltpu.SemaphoreType.DMA((NDEV - 1,)),
                        pltpu.SemaphoreType.DMA(())],
    )(src, after)


_HBM = pl.BlockSpec(memory_space=pltpu.HBM)
_SEM = pl.BlockSpec(memory_space=pltpu.SEMAPHORE)
_ANY = pl.BlockSpec(memory_space=pl.ANY)
_EFFECT = pltpu.SideEffectType.DATAFLOW_SIDE_EFFECTING


def _exchange_copies(src_ref, land_ref, send_sems, recv_sems, mode):
    x, y, c = lax.axis_index("x"), lax.axis_index("y"), lax.axis_index("c")
    me = 4 * x + 2 * y + c
    scatter = mode == "scatter"
    copies = []
    for t in ((1, 2, 4, 6) if mode == "chip" else range(1, NDEV)):
        px = 1 - x if (t >> 2) & 1 else x
        py = 1 - y if (t >> 1) & 1 else y
        pc = 1 - c if t & 1 else c
        peer = 4 * px + 2 * py + pc
        copies.append(pltpu.make_async_remote_copy(
            src_ref=src_ref.at[peer] if scatter else src_ref,
            dst_ref=land_ref.at[me],
            send_sem=send_sems.at[t - 1],
            recv_sem=recv_sems.at[t - 1],
            device_id=(px, py, pc),
            device_id_type=pl.DeviceIdType.MESH,
        ))
    own = pltpu.make_async_copy(src_ref.at[me] if scatter else src_ref, land_ref.at[me], send_sems.at[NDEV - 1])
    return copies, own


def _forward_copies(land_ref, send_sems, recv_sems):
    x, y, c = lax.axis_index("x"), lax.axis_index("y"), lax.axis_index("c")
    copies = []
    for k, t in enumerate((2, 4, 6)):
        px = 1 - x if (t >> 2) & 1 else x
        py = 1 - y if (t >> 1) & 1 else y
        mine, theirs = 4 * px + 2 * py + c, 4 * px + 2 * py + (1 - c)
        send = pltpu.make_async_remote_copy(
            src_ref=land_ref.at[mine], dst_ref=land_ref.at[mine], send_sem=send_sems.at[k], recv_sem=recv_sems.at[k],
            device_id=(x, y, 1 - c), device_id_type=pl.DeviceIdType.MESH)
        recv = pltpu.make_async_remote_copy(
            src_ref=land_ref.at[theirs], dst_ref=land_ref.at[theirs], send_sem=send_sems.at[k], recv_sem=recv_sems.at[k],
            device_id=(x, y, 1 - c), device_id_type=pl.DeviceIdType.MESH)
        copies.append((send, recv))
    return copies


def _forward_start(land, after, name):
    def body(land_ref, after_ref, send_sems, recv_sems, land_thru, token):
        for send, _ in _forward_copies(land_ref, send_sems, recv_sems):
            send.start()
        token[...] = jnp.zeros_like(token)

    res = _pallas(
        body, name=name,
        out_shape=(pltpu.SemaphoreType.DMA((3,)), pltpu.SemaphoreType.DMA((3,)), pltpu.HBM(land.shape, land.dtype),
                   jax.ShapeDtypeStruct((8, HEAD), F32)),
        in_specs=(_HBM, _ANY), out_specs=(_SEM, _SEM, _HBM, pl.BlockSpec(memory_space=pltpu.VMEM)),
        input_output_aliases={0: 2},
        compiler_params=pltpu.CompilerParams(has_side_effects=_EFFECT),
    )(land, after)
    return res[:3], res[3]


def _forward_wait(handle, after, name):
    send_sems, recv_sems, land_thru = handle

    def body(land_ref, send_sems, recv_sems, after_ref, got_ref):
        for send, recv in _forward_copies(land_ref, send_sems, recv_sems):
            send.wait_send()
            recv.wait_recv()

    return _pallas(
        body, name=name,
        out_shape=pltpu.HBM(land_thru.shape, land_thru.dtype),
        in_specs=(_HBM, _SEM, _SEM, _ANY), out_specs=_HBM,
        input_output_aliases={0: 0},
        compiler_params=pltpu.CompilerParams(has_side_effects=_EFFECT),
    )(land_thru, send_sems, recv_sems, after)


def _exchange_start(src, mode, after, name):
    blk = src.shape[1:] if mode == "scatter" else src.shape
    land = lax.empty((NDEV,) + tuple(blk), src.dtype)

    def body(src_ref, land_ref, after_ref, send_sems, recv_sems, src_thru, land_thru, token):
        copies, own = _exchange_copies(src_ref, land_ref, send_sems, recv_sems, mode)
        for cp in copies:
            cp.start()
        own.start()
        token[...] = jnp.zeros_like(token)

    res = _pallas(
        body, name=name,
        out_shape=(pltpu.SemaphoreType.DMA((NDEV,)), pltpu.SemaphoreType.DMA((NDEV,)),
                   pltpu.HBM(src.shape, src.dtype), pltpu.HBM(land.shape, land.dtype),
                   jax.ShapeDtypeStruct((8, HEAD), F32)),
        in_specs=(_HBM, _HBM, _ANY), out_specs=(_SEM, _SEM, _HBM, _HBM, pl.BlockSpec(memory_space=pltpu.VMEM)),
        input_output_aliases={0: 2, 1: 3},
        compiler_params=pltpu.CompilerParams(has_side_effects=_EFFECT),
    )(pltpu.with_memory_space_constraint(src, pltpu.HBM), pltpu.with_memory_space_constraint(land, pltpu.HBM), after)
    return res[:4], res[4]


def _exchange_wait(handle, mode, after, name):
    send_sems, recv_sems, src_thru, land_thru = handle

    def body(src_ref, land_ref, send_sems, recv_sems, after_ref, src_dead, got_ref):
        copies, own = _exchange_copies(src_ref, land_ref, send_sems, recv_sems, mode)
        for cp in copies:
            cp.wait_send()
            cp.wait_recv()
        own.wait()

    return _pallas(
        body, name=name,
        out_shape=(pltpu.HBM(src_thru.shape, src_thru.dtype), pltpu.HBM(land_thru.shape, land_thru.dtype)),
        in_specs=(_HBM, _HBM, _SEM, _SEM, _ANY), out_specs=(_HBM, _HBM),
        input_output_aliases={0: 0, 1: 1},
        compiler_params=pltpu.CompilerParams(has_side_effects=_EFFECT),
    )(src_thru, land_thru, send_sems, recv_sems, after)[1]


def _ada_fwd(c_all, w, bias):
    r, d = c_all.shape
    e = w.shape[1]
    tn = 512

    def body(c_ref, w_ref, b_ref, o_ref):
        cv = c_ref[...]
        s = (cv * _sigmoid(cv)).astype(BF16)
        o_ref[...] = jnp.dot(s, w_ref[...].astype(BF16), preferred_element_type=F32) + b_ref[...]

    return _pallas(
        body, name="ada_fwd", grid=(e // tn,),
        out_shape=jax.ShapeDtypeStruct((r, e), F32),
        in_specs=[pl.BlockSpec((r, d), lambda j: (0, 0)), pl.BlockSpec((d, tn), lambda j: (0, j)),
                  pl.BlockSpec((1, tn), lambda j: (0, j))],
        out_specs=pl.BlockSpec((r, tn), lambda j: (0, j)),
        compiler_params=_params(_mb(24)),
    )(c_all, w, bias)


def _ada_bwd(dm16, c_all, w):
    d, e = w.shape
    tn = 512

    def body(dm_ref, c_ref, w_ref, dw_ref, dr_ref):
        j = pl.program_id(0)
        dm = dm_ref[...]
        rid = lax.broadcasted_iota(jnp.int32, dm.shape, 0)
        ctx_sum = jnp.sum(jnp.where(rid >= 8, dm, 0.0), axis=0, keepdims=True)
        rows = jnp.where(rid < 8, dm, jnp.where(rid == 8, jnp.broadcast_to(ctx_sum, dm.shape), 0.0)).astype(BF16)
        cv = c_ref[...]
        s = (cv * _sigmoid(cv)).astype(BF16)
        dw_ref[...] = lax.dot_general(s, rows, _TN, preferred_element_type=F32)
        part = lax.dot_general(rows, w_ref[...].astype(BF16), _NT, preferred_element_type=F32)

        @pl.when(j == 0)
        def _():
            dr_ref[...] = part

        @pl.when(j > 0)
        def _():
            dr_ref[...] += part

    return _pallas(
        body, name="ada_bwd", grid=(e // tn,),
        out_shape=(jax.ShapeDtypeStruct((d, e), F32), jax.ShapeDtypeStruct((16, d), F32)),
        in_specs=[pl.BlockSpec((16, tn), lambda j: (0, j)), pl.BlockSpec((16, d), lambda j: (0, 0)),
                  pl.BlockSpec((d, tn), lambda j: (0, j))],
        out_specs=(pl.BlockSpec((d, tn), lambda j: (0, j)), pl.BlockSpec((16, d), lambda j: (0, 0))),
        compiler_params=_params(_mb(32)),
    )(dm16, c_all, w)


def _rope(v, cos, sa, sb):
    return v * cos + (pltpu.roll(v, 96, 1) * sa + pltpu.roll(v, 32, 1) * sb)


def _rope_t(dt, cos, sa, sb):
    return dt * cos + (pltpu.roll(dt * sa, 32, 1) + pltpu.roll(dt * sb, 96, 1))


def _qkv_fwd(x, ct, sc, sh, wint, qg, kg, cos, sa, sb):
    n, d = x.shape
    tm = CTX
    nlat = n // tm
    na = n + CTX
    wcols = wint.shape[0]

    def body(x_ref, ct_ref, sc_ref, sh_ref, w_ref, qg_ref, kg_ref, cos_ref, sa_ref, sb_ref, u_ref, h_ref, t_ref, kt_ref):
        i = pl.program_id(0)
        xin = jnp.where(i == nlat, ct_ref[...], x_ref[...])
        u = (xin * (1.0 + sc_ref[0]) + sh_ref[0]).astype(BF16)
        u_ref[...] = u
        cos, sa, sb = cos_ref[...], sa_ref[...], sb_ref[...]
        h = lax.dot_general(u, w_ref[...], _NT, preferred_element_type=F32)
        h_ref[...] = h[:, NORM_HEAD0 * HEAD:(NORM_HEAD0 + NORM_HEADS) * HEAD]
        for hd in range(24):
            v = h[:, hd * HEAD:(hd + 1) * HEAD]
            kind = _KINDS[hd]
            if kind == "qnorm":
                v = v * lax.rsqrt(_rowmean(v * v) + EPS) * qg_ref[...]
            elif kind == "knorm":
                v = v * lax.rsqrt(_rowmean(v * v) + EPS) * kg_ref[...]
            if kind != "none":
                v = _rope(v, cos, sa, sb)
            t_ref[:, hd * HEAD:(hd + 1) * HEAD] = v.astype(BF16)
            if kind == "knorm":
                kt_ref[(hd - 20) * HEAD:(hd - 19) * HEAD, :] = v.T.astype(BF16)

    lat = lambda i: (jnp.minimum(i, nlat - 1), 0)
    row = lambda i: (i, 0)
    const2 = lambda i: (0, 0)
    return _pallas(
        body, name="qkv_fwd", grid=(nlat + 1,),
        out_shape=(jax.ShapeDtypeStruct((na, d), BF16), jax.ShapeDtypeStruct((na, NORM_HEADS * HEAD), F32),
                   jax.ShapeDtypeStruct((na, wcols), BF16), jax.ShapeDtypeStruct((2 * HEAD, na), BF16)),
        in_specs=[pl.BlockSpec((tm, d), lat), pl.BlockSpec((tm, d), const2),
                  pl.BlockSpec((1, 1, d), lambda i: (i // nlat, 0, 0)),
                  pl.BlockSpec((1, 1, d), lambda i: (i // nlat, 0, 0)),
                  pl.BlockSpec((wcols, d), const2),
                  pl.BlockSpec((1, HEAD), const2), pl.BlockSpec((1, HEAD), const2),
                  pl.BlockSpec((tm, HEAD), row), pl.BlockSpec((tm, HEAD), row), pl.BlockSpec((tm, HEAD), row)],
        out_specs=(pl.BlockSpec((tm, d), row), pl.BlockSpec((tm, NORM_HEADS * HEAD), row), pl.BlockSpec((tm, wcols), row),
                   pl.BlockSpec((2 * HEAD, tm), lambda i: (0, i))),
        compiler_params=_params(_mb(56)),
    )(x, ct, sc, sh, wint, qg, kg, cos, sa, sb)


def _qkv_bwd_prep(dqa, dka, dva, dqb, dkb, dvb, h_norm, qg, kg, cos, sa, sb):
    na = h_norm.shape[0]
    wcols = 24 * HEAD
    n = na - CTX
    tm = CTX
    nlat = n // tm

    def body(dqa_ref, dka_ref, dva_ref, dqb_ref, dkb_ref, dvb_ref, h_ref, qg_ref, kg_ref, cos_ref, sa_ref, sb_ref,
             dh_ref, dg_ref):
        i = pl.program_id(0)

        @pl.when(i == 0)
        def _():
            dg_ref[...] = jnp.zeros_like(dg_ref)

        cos, sa, sb = cos_ref[...], sa_ref[...], sb_ref[...]
        is_lat = i < nlat
        for hd in range(24):
            kind = _KINDS[hd]
            if hd < 8:
                dt = jnp.where(is_lat, dqa_ref[:, hd * HEAD:(hd + 1) * HEAD], 0.0)
            elif hd < 10:
                dt = dka_ref[:, (hd - 8) * HEAD:(hd - 7) * HEAD]
            elif hd < 12:
                dt = dva_ref[:, (hd - 10) * HEAD:(hd - 9) * HEAD]
            elif hd < 20:
                dt = jnp.where(is_lat, dqb_ref[:, (hd - 12) * HEAD:(hd - 11) * HEAD], 0.0)
            elif hd < 22:
                dt = dkb_ref[:, (hd - 20) * HEAD:(hd - 19) * HEAD]
            else:
                dt = dvb_ref[:, (hd - 22) * HEAD:(hd - 21) * HEAD]
            if kind != "none":
                dt = _rope_t(dt, cos, sa, sb)
            if kind in ("qnorm", "knorm"):
                g_ref = qg_ref if kind == "qnorm" else kg_ref
                r0 = 0 if kind == "qnorm" else 1
                xv = h_ref[:, (hd - NORM_HEAD0) * HEAD:(hd - NORM_HEAD0 + 1) * HEAD]
                xn = xv * lax.rsqrt(_rowmean(xv * xv) + EPS)
                dg_ref[r0:r0 + 1, :] += _colsum(dt * xn)
                dxn = dt * g_ref[...]
                dt = lax.rsqrt(_rowmean(xv * xv) + EPS) * (dxn - xn * _rowmean(dxn * xn))
            dh_ref[:, hd * HEAD:(hd + 1) * HEAD] = dt.astype(BF16)

    lat = lambda i: (jnp.minimum(i, nlat - 1), 0)
    row = lambda i: (i, 0)
    const2 = lambda i: (0, 0)
    return _pallas(
        body, name="qkv_bwd_prep", grid=(nlat + 1,),
        out_shape=(jax.ShapeDtypeStruct((na, wcols), BF16), jax.ShapeDtypeStruct((8, HEAD), F32)),
        in_specs=[pl.BlockSpec((tm, 8 * HEAD), lat), pl.BlockSpec((tm, 2 * HEAD), row), pl.BlockSpec((tm, 2 * HEAD), row),
                  pl.BlockSpec((tm, 8 * HEAD), lat), pl.BlockSpec((tm, 2 * HEAD), row), pl.BlockSpec((tm, 2 * HEAD), row),
                  pl.BlockSpec((tm, NORM_HEADS * HEAD), row),
                  pl.BlockSpec((1, HEAD), const2), pl.BlockSpec((1, HEAD), const2),
                  pl.BlockSpec((tm, HEAD), row), pl.BlockSpec((tm, HEAD), row), pl.BlockSpec((tm, HEAD), row)],
        out_specs=(pl.BlockSpec((tm, wcols), row), pl.BlockSpec((8, HEAD), const2)),
        compiler_params=_params(_mb(40)),
    )(dqa, dka, dva, dqb, dkb, dvb, h_norm, qg, kg, cos, sa, sb)


def _window_keys(k_ref, v_ref, n, na):
    i = pl.program_id(1)
    tq = WINDOW
    start = pl.multiple_of(jnp.clip((i - 1) * tq, 0, n - 3 * tq), tq)
    kk = jnp.concatenate([k_ref[pl.ds(start, 3 * tq), :], k_ref[n:na, :]], axis=0)
    vv = jnp.concatenate([v_ref[pl.ds(start, 3 * tq), :], v_ref[n:na, :]], axis=0)
    return kk, vv, start


def _window_bias():
    tq = WINDOW
    r = (jnp.arange(4 * tq) % tq)[:, None]
    c = jnp.arange(3 * tq + CTX)[None, :]
    variants = []
    for back in (0, tq, 2 * tq):
        seen = (jnp.abs(back + r - c) <= WINDOW) | (c >= 3 * tq)
        variants.append(jnp.where(seen, 0.0, NEG).astype(F32))
    return jnp.stack(variants)


def _window_bias_spec(nq):
    return pl.BlockSpec((1, 4 * WINDOW, 3 * WINDOW + CTX),
                        lambda kv, i: (jnp.where(i == 0, 0, jnp.where(i == nq - 1, 2, 1)), 0, 0))


def _stack_heads(ref, width=HEAD):
    return jnp.concatenate([ref[:, g * HEAD:g * HEAD + width] for g in range(4)], axis=0)


def _sink_column(sink_ref, kv, tq):
    grp = lax.broadcasted_iota(jnp.int32, (4 * tq, 1), 0) // tq
    col = jnp.zeros((4 * tq, 1), F32)
    for g in range(4):
        col = jnp.where(grp == g, sink_ref[0, 4 * kv + g] * LOG2E, col)
    return col


def _attn_window_fwd(t_all, sink, bias, after):
    na = t_all.shape[0]
    n = na - CTX
    tq = WINDOW

    def body(sink_ref, q_ref, k_ref, v_ref, bias_ref, after_ref, o_ref, p_ref, linv_ref):
        kv = pl.program_id(0)
        kk, vv, _ = _window_keys(k_ref, v_ref, n, na)
        t = lax.dot_general(_stack_heads(q_ref), kk, _NT, preferred_element_type=F32) * QK_LOG2 + bias_ref[0]
        sk = _sink_column(sink_ref, kv, tq)
        m = jnp.maximum(jnp.max(t, axis=-1, keepdims=True), sk)
        p = jnp.exp2(t - m)
        p_sink = jnp.exp2(sk - m)
        linv = 1.0 / (jnp.sum(p, axis=-1, keepdims=True) + p_sink)
        pb = p.astype(BF16)
        o = jnp.dot(pb, vv, preferred_element_type=F32) * linv
        p_all = jnp.concatenate([pb, jnp.broadcast_to(p_sink, (4 * tq, WIN_P - WIN_KEYS)).astype(BF16)], axis=1)
        for g in range(4):
            o_ref[:, g * HEAD:(g + 1) * HEAD] = o[g * tq:(g + 1) * tq]
            p_ref[g] = p_all[g * tq:(g + 1) * tq]
            linv_ref[:, g * HEAD:(g + 1) * HEAD] = jnp.broadcast_to(linv[g * tq:(g + 1) * tq], (tq, HEAD))

    blk = pl.BlockSpec((tq, 4 * HEAD), lambda kv, i: (i, kv))
    return _pallas(
        body, name="attn_window_fwd", grid=(2, n // tq),
        out_shape=(jax.ShapeDtypeStruct((n, 16 * HEAD), F32), jax.ShapeDtypeStruct((8, n, WIN_P), BF16),
                   jax.ShapeDtypeStruct((n, 8 * HEAD), F32)),
        in_specs=[pl.BlockSpec(memory_space=pltpu.SMEM), blk,
                  pl.BlockSpec((na, HEAD), lambda kv, i: (0, 8 + kv)),
                  pl.BlockSpec((na, HEAD), lambda kv, i: (0, 10 + kv)), _window_bias_spec(n // tq), _ANY],
        out_specs=(blk, pl.BlockSpec((4, tq, WIN_P), lambda kv, i: (kv, i, 0)), blk),
        compiler_params=_params(_mb(32)),
    )(sink, t_all, t_all, t_all, bias, after)


def _attn_global_fwd(t_all, o_part):
    na = t_all.shape[0]
    n = na - CTX
    tq = 256

    def body(q_ref, k_ref, v_ref, o_in_ref, o_ref, p_ref, linv_ref):
        kk, vv = k_ref[...], v_ref[...]
        for g in range(4):
            q = q_ref[:, g * HEAD:(g + 1) * HEAD]
            t = lax.dot_general(q, kk, _NT, preferred_element_type=F32) * QK_LOG2
            m = jnp.max(t, axis=-1, keepdims=True)
            p = jnp.exp2(t - m)
            linv = 1.0 / jnp.sum(p, axis=-1, keepdims=True)
            pb = p.astype(BF16)
            p_ref[g] = pb
            o_ref[:, g * HEAD:(g + 1) * HEAD] = jnp.dot(pb, vv, preferred_element_type=F32) * linv
            linv_ref[:, g * HEAD:(g + 1) * HEAD] = jnp.broadcast_to(linv, (tq, HEAD))

    return _pallas(
        body, name="attn_global_fwd", grid=(2, n // tq),
        out_shape=(jax.ShapeDtypeStruct((n, 16 * HEAD), F32), jax.ShapeDtypeStruct((8, n, na), BF16),
                   jax.ShapeDtypeStruct((n, 8 * HEAD), F32)),
        in_specs=[pl.BlockSpec((tq, 4 * HEAD), lambda kv, i: (i, 3 + kv)),
                  pl.BlockSpec((na, HEAD), lambda kv, i: (0, 20 + kv)),
                  pl.BlockSpec((na, HEAD), lambda kv, i: (0, 22 + kv)), _ANY],
        out_specs=(pl.BlockSpec((tq, 4 * HEAD), lambda kv, i: (i, 2 + kv)),
                   pl.BlockSpec((4, tq, na), lambda kv, i: (kv, i, 0)),
                   pl.BlockSpec((tq, 4 * HEAD), lambda kv, i: (i, kv))),
        input_output_aliases={3: 0},
        compiler_params=_params(_mb(56)),
    )(t_all, t_all, t_all, o_part)


def _attn_window_bwd(t_all, o, do, p_all, linv):
    na = t_all.shape[0]
    n = na - CTX
    tq = WINDOW

    def body(q_ref, k_ref, v_ref, o_ref, do_ref, p_ref, linv_ref, dq_ref, dk_ref, dv_ref, dsink_ref):
        @pl.when(pl.program_id(1) == 0)
        def _():
            dk_ref[...] = jnp.zeros_like(dk_ref)
            dv_ref[...] = jnp.zeros_like(dv_ref)
            dsink_ref[...] = jnp.zeros_like(dsink_ref)

        kk, vv, start = _window_keys(k_ref, v_ref, n, na)
        q = _stack_heads(q_ref)
        p_full = jnp.concatenate([p_ref[g] for g in range(4)], axis=0).astype(F32) * _stack_heads(linv_ref, 1)
        p = p_full[:, :WIN_KEYS]
        dof = _stack_heads(do_ref)
        delta = jnp.sum(dof * _stack_heads(o_ref), axis=-1, keepdims=True)
        dob = dof.astype(BF16)
        dv_acc = lax.dot_general(p.astype(BF16), dob, _TN, preferred_element_type=F32)
        dp = lax.dot_general(dob, vv, _NT, preferred_element_type=F32)
        ds = (p * (dp - delta) * SCALE).astype(BF16)
        dq = jnp.dot(ds, kk, preferred_element_type=F32)
        dk_acc = lax.dot_general(ds, q, _TN, preferred_element_type=F32)
        dsk = -(p_full[:, WIN_KEYS:WIN_KEYS + 1] * delta)
        for g in range(4):
            dq_ref[:, g * HEAD:(g + 1) * HEAD] = dq[g * tq:(g + 1) * tq]
            dsink_ref[0, g:g + 1, :] += jnp.broadcast_to(_colsum(dsk[g * tq:(g + 1) * tq]), (1, HEAD))
        dk_ref[pl.ds(start, 3 * tq), :] += dk_acc[:3 * tq]
        dv_ref[pl.ds(start, 3 * tq), :] += dv_acc[:3 * tq]
        dk_ref[n:na, :] += dk_acc[3 * tq:]
        dv_ref[n:na, :] += dv_acc[3 * tq:]

    blk = pl.BlockSpec((tq, 4 * HEAD), lambda kv, i: (i, kv))
    kvout = pl.BlockSpec((na, HEAD), lambda kv, i: (0, kv))
    return _pallas(
        body, name="attn_window_bwd", grid=(2, n // tq),
        out_shape=(jax.ShapeDtypeStruct((n, 8 * HEAD), F32), jax.ShapeDtypeStruct((na, 2 * HEAD), F32),
                   jax.ShapeDtypeStruct((na, 2 * HEAD), F32), jax.ShapeDtypeStruct((2, 8, HEAD), F32)),
        in_specs=[blk,
                  pl.BlockSpec((na, HEAD), lambda kv, i: (0, 8 + kv)),
                  pl.BlockSpec((na, HEAD), lambda kv, i: (0, 10 + kv)),
                  blk, blk, pl.BlockSpec((4, tq, WIN_P), lambda kv, i: (kv, i, 0)), blk],
        out_specs=(blk, kvout, kvout, pl.BlockSpec((1, 8, HEAD), lambda kv, i: (kv, 0, 0))),
        compiler_params=_params(_mb(40)),
    )(t_all, t_all, t_all, o, do, p_all, linv)


def _attn_global_bwd(t_all, kt, o, do, p_all, linv):
    na = t_all.shape[0]
    n = na - CTX
    tq = 256

    def body(q_ref, v_ref, kt_ref, o_ref, do_ref, p_ref, linv_ref, dq_ref, dk_ref, dv_ref, dkt_acc, dvt_acc):
        i = pl.program_id(1)

        @pl.when(i == 0)
        def _():
            dkt_acc[...] = jnp.zeros_like(dkt_acc)
            dvt_acc[...] = jnp.zeros_like(dvt_acc)

        vv, kt_v = v_ref[...], kt_ref[...]
        dkt = jnp.zeros((HEAD, na), F32)
        dvt = jnp.zeros((HEAD, na), F32)
        def probs(g):
            return p_ref[g].astype(F32) * linv_ref[:, g * HEAD:g * HEAD + 1]

        def dprobs(g):
            dob = do_ref[:, g * HEAD:(g + 1) * HEAD].astype(BF16)
            return dob, lax.dot_general(dob, vv, _NT, preferred_element_type=F32)

        nxt = dprobs(0)
        for g in range(4):
            q = q_ref[:, g * HEAD:(g + 1) * HEAD]
            p = probs(g)
            dob, dp = nxt
            if g < 3:
                nxt = dprobs(g + 1)
            delta = jnp.sum(do_ref[:, g * HEAD:(g + 1) * HEAD] * o_ref[:, g * HEAD:(g + 1) * HEAD], axis=-1,
                            keepdims=True)
            dvt = dvt + lax.dot_general(dob, p.astype(BF16), _TN, preferred_element_type=F32)
            ds = (p * (dp - delta) * SCALE).astype(BF16)
            dq_ref[:, g * HEAD:(g + 1) * HEAD] = lax.dot_general(kt_v, ds, _NT, preferred_element_type=F32).T
            dkt = dkt + lax.dot_general(q, ds, _TN, preferred_element_type=F32)
        dkt_acc[...] += dkt
        dvt_acc[...] += dvt

        @pl.when(i == pl.num_programs(1) - 1)
        def _():
            dk_ref[...] = dkt_acc[...].T
            dv_ref[...] = dvt_acc[...].T

    ospec = pl.BlockSpec((tq, 4 * HEAD), lambda kv, i: (i, 2 + kv))
    lspec = pl.BlockSpec((tq, 4 * HEAD), lambda kv, i: (i, kv))
    kvout = pl.BlockSpec((na, HEAD), lambda kv, i: (0, kv))
    return _pallas(
        body, name="attn_global_bwd", grid=(2, n // tq),
        out_shape=(jax.ShapeDtypeStruct((n, 8 * HEAD), F32), jax.ShapeDtypeStruct((na, 2 * HEAD), F32),
                   jax.ShapeDtypeStruct((na, 2 * HEAD), F32)),
        in_specs=[pl.BlockSpec((tq, 4 * HEAD), lambda kv, i: (i, 3 + kv)),
                  pl.BlockSpec((na, HEAD), lambda kv, i: (0, 22 + kv)),
                  pl.BlockSpec((HEAD, na), lambda kv, i: (kv, 0)),
                  ospec, ospec, pl.BlockSpec((4, tq, na), lambda kv, i: (kv, i, 0)), lspec],
        out_specs=(lspec, kvout, kvout),
        scratch_shapes=[pltpu.VMEM((HEAD, na), F32), pltpu.VMEM((HEAD, na), F32)],
        compiler_params=_params(_mb(56)),
    )(t_all, t_all, kt, o, do, p_all, linv)


def _outproj_ln1(o, wout, x, g1, lg, lb, sc2, sh2, after):
    n, d = x.shape
    tm = 256

    def body(o_ref, w_ref, x_ref, g1_ref, lg_ref, lb_ref, sc_ref, sh_ref, after_ref, a_ref, xh_ref, rs_ref, u_ref):
        a1 = jnp.dot(o_ref[...].astype(BF16), w_ref[...], preferred_element_type=F32)
        a_ref[...] = a1.astype(BF16)
        r = ALPHA * x_ref[...] + g1_ref[...] * a1
        dlt = r - _rowmean(r)
        rstd = lax.rsqrt(_rowmean(dlt * dlt) + EPS)
        xh = dlt * rstd
        xh_ref[...] = xh
        rs_ref[...] = rstd
        x1 = xh * lg_ref[...] + lb_ref[...]
        u_ref[...] = (x1 * (1.0 + sc_ref[...]) + sh_ref[...]).astype(BF16)

    row = lambda i: (i, 0)
    const2 = lambda i: (0, 0)
    vec = pl.BlockSpec((1, d), const2)
    big = pl.BlockSpec((tm, d), row)
    return _pallas(
        body, name="outproj_ln1", grid=(n // tm,),
        out_shape=(jax.ShapeDtypeStruct((n, d), BF16), jax.ShapeDtypeStruct((n, d), F32),
                   jax.ShapeDtypeStruct((n, 1), F32), jax.ShapeDtypeStruct((n, d), BF16)),
        in_specs=[big, pl.BlockSpec((d, d), const2), big, vec, vec, vec, vec, vec, _ANY],
        out_specs=(big, big, pl.BlockSpec((tm, 1), row), big),
        compiler_params=_params(_mb(56)),
    )(o, wout, x, g1, lg, lb, sc2, sh2, after)


def _ffn_up(u2, wgt, wut, after):
    n, d = u2.shape
    f = wgt.shape[0]
    tm = min(1024, n)

    def body(u_ref, wg_ref, wu_ref, after_ref, sa_ref, sb_ref, hf_ref):
        u = u_ref[...]
        gv = lax.dot_general(u, wg_ref[...], _NT, preferred_element_type=F32)
        pv = lax.dot_general(u, wu_ref[...], _NT, preferred_element_type=F32)
        sg = _sigmoid(gv)
        silu = gv * sg
        sa_ref[...] = silu.astype(BF16)
        sb_ref[...] = (pv * (sg * (1.0 + gv * (1.0 - sg)))).astype(BF16)
        hf_ref[...] = (silu * pv).astype(BF16)

    tile = pl.BlockSpec((tm, FFN_TILE), lambda i, j: (i, j))
    wspec = pl.BlockSpec((FFN_TILE, d), lambda i, j: (j, 0))
    sds = jax.ShapeDtypeStruct((n, f), BF16)
    return _pallas(
        body, name="ffn_up", grid=(n // tm, f // FFN_TILE),
        out_shape=(sds, sds, sds),
        in_specs=[pl.BlockSpec((tm, d), lambda i, j: (i, 0)), wspec, wspec, _ANY],
        out_specs=(tile, tile, tile),
        compiler_params=_params(_mb(48)),
    )(u2, wgt, wut, after)


def _ffn_down(hf, wd):
    n, f = hf.shape
    d = wd.shape[1]
    tm, tn = min(1024, n), 512

    def body(h_ref, w_ref, o_ref):
        o_ref[...] = jnp.dot(h_ref[...], w_ref[...], preferred_element_type=F32)

    return _pallas(
        body, name="ffn_down", grid=(n // tm, d // tn),
        out_shape=jax.ShapeDtypeStruct((n, d), F32),
        in_specs=[pl.BlockSpec((tm, f), lambda i, j: (i, 0)), pl.BlockSpec((f, tn), lambda i, j: (0, j))],
        out_specs=pl.BlockSpec((tm, tn), lambda i, j: (i, j)),
        compiler_params=_params(_mb(56)),
    )(hf, wd)


def _ln2_loss(xh1, ffn, tgt, lg1, lb1, g2, lg2, lb2):
    n, d = xh1.shape
    tm = 256

    def body(xh_ref, f_ref, t_ref, lg1_ref, lb1_ref, g2_ref, lg2_ref, lb2_ref, dr_ref, df_ref, loss_ref, acc_ref):
        @pl.when(pl.program_id(0) == 0)
        def _():
            loss_ref[...] = jnp.zeros_like(loss_ref)
            acc_ref[...] = jnp.zeros_like(acc_ref)

        x1 = xh_ref[...] * lg1_ref[...] + lb1_ref[...]
        fv = f_ref[...]
        r = ALPHA * x1 + g2_ref[...] * fv
        dlt = r - _rowmean(r)
        rstd = lax.rsqrt(_rowmean(dlt * dlt) + EPS)
        xh2 = dlt * rstd
        err = xh2 * lg2_ref[...] + lb2_ref[...] - t_ref[...]
        loss_ref[...] += 0.5 * jnp.sum(_rowmean(err * err))
        dy = err * (1.0 / d)
        dyg = dy * lg2_ref[...]
        dr = rstd * (dyg - _rowmean(dyg) - xh2 * _rowmean(dyg * xh2))
        dr_ref[...] = dr
        df_ref[...] = (g2_ref[...] * dr).astype(BF16)
        acc_ref[0:1, :] += _colsum(dy * xh2)
        acc_ref[1:2, :] += _colsum(dy)
        acc_ref[2:3, :] += _colsum(dr * fv)

    row = lambda i: (i, 0)
    const2 = lambda i: (0, 0)
    vec = pl.BlockSpec((1, d), const2)
    big = pl.BlockSpec((tm, d), row)
    return _pallas(
        body, name="ln2_loss", grid=(n // tm,),
        out_shape=(jax.ShapeDtypeStruct((n, d), F32), jax.ShapeDtypeStruct((n, d), BF16),
                   jax.ShapeDtypeStruct((8, HEAD), F32), jax.ShapeDtypeStruct((8, d), F32)),
        in_specs=[big, big, big, vec, vec, vec, vec, vec],
        out_specs=(big, big, pl.BlockSpec((8, HEAD), const2), pl.BlockSpec((8, d), const2)),
        compiler_params=_params(_mb(48)),
    )(xh1, ffn, tgt, lg1, lb1, g2, lg2, lb2)


def _ffn_dhf(df, wd, sa, sb):
    n, d = df.shape
    f = sa.shape[1]
    tm = min(2048, n)

    def body(df_ref, w_ref, sa_ref, sb_ref, dgp_ref):
        dhf = lax.dot_general(df_ref[...], w_ref[...], _NT, preferred_element_type=F32)
        dgp_ref[:, :FFN_TILE] = (dhf * sb_ref[...].astype(F32)).astype(BF16)
        dgp_ref[:, FFN_TILE:] = (dhf * sa_ref[...].astype(F32)).astype(BF16)

    tile = pl.BlockSpec((tm, FFN_TILE), lambda i, j: (i, j))
    return _pallas(
        body, name="ffn_dhf", grid=(n // tm, f // FFN_TILE),
        out_shape=jax.ShapeDtypeStruct((n, 2 * f), BF16),
        in_specs=[pl.BlockSpec((tm, d), lambda i, j: (i, 0)), pl.BlockSpec((FFN_TILE, d), lambda i, j: (j, 0)),
                  tile, tile],
        out_specs=pl.BlockSpec((tm, 2 * FFN_TILE), lambda i, j: (i, j)),
        compiler_params=_params(_mb(48)),
    )(df, wd, sa, sb)


def _ffn_du2(dgp, wgt, wut, after):
    n = dgp.shape[0]
    f, d = wgt.shape
    tm = min(1024, n)

    def body(dgp_ref, wg_ref, wu_ref, after_ref, o_ref):
        w = jnp.concatenate([wg_ref[...], wu_ref[...]], axis=0)
        part = jnp.dot(dgp_ref[...], w, preferred_element_type=F32)

        @pl.when(pl.program_id(1) == 0)
        def _():
            o_ref[...] = part

        @pl.when(pl.program_id(1) > 0)
        def _():
            o_ref[...] += part

    wspec = pl.BlockSpec((FFN_TILE, d), lambda i, j: (j, 0))
    return _pallas(
        body, name="ffn_du2", grid=(n // tm, f // FFN_TILE),
        out_shape=jax.ShapeDtypeStruct((n, d), F32),
        in_specs=[pl.BlockSpec((tm, 2 * FFN_TILE), lambda i, j: (i, j)), wspec, wspec, _ANY],
        out_specs=pl.BlockSpec((tm, d), lambda i, j: (i, 0)),
        compiler_params=_params(_mb(48)),
    )(dgp, wgt, wut, after)


def _dw_gate_up(dgp, u2, after):
    n, d = u2.shape
    f = dgp.shape[1] // 2
    tm = min(2048, n)

    def body(a_ref, b_ref, after_ref, og_ref, ou_ref, acc_ref):
        part = lax.dot_general(a_ref[...], b_ref[...], _TN, preferred_element_type=F32)
        i = pl.program_id(1)

        @pl.when(i == 0)
        def _():
            acc_ref[...] = part

        @pl.when(i > 0)
        def _():
            acc_ref[...] += part

        @pl.when(i == pl.num_programs(1) - 1)
        def _():
            og_ref[...] = acc_ref[:FFN_TILE].astype(BF16)
            ou_ref[...] = acc_ref[FFN_TILE:].astype(BF16)

    out = pl.BlockSpec((FFN_TILE, d), lambda j, i: (j, 0))
    sds = jax.ShapeDtypeStruct((f, d), BF16)
    return _pallas(
        body, name="dw_gate_up", grid=(f // FFN_TILE, n // tm),
        out_shape=(sds, sds),
        in_specs=[pl.BlockSpec((tm, 2 * FFN_TILE), lambda j, i: (i, j)), pl.BlockSpec((tm, d), lambda j, i: (i, 0)), _ANY],
        out_specs=(out, out),
        scratch_shapes=[pltpu.VMEM((2 * FFN_TILE, d), F32)],
        compiler_params=_params(_mb(56)),
    )(dgp, u2, after)


def _ln1_bwd(du2, dr2, xh1, rs1, a1, lg1, lb1, sc2, g1):
    n, d = du2.shape
    tm = 256

    def body(du_ref, dr2_ref, xh_ref, rs_ref, a_ref, lg_ref, lb_ref, sc_ref, g1_ref, dr1_ref, da_ref, acc_ref):
        @pl.when(pl.program_id(0) == 0)
        def _():
            acc_ref[...] = jnp.zeros_like(acc_ref)

        du = du_ref[...]
        xh = xh_ref[...]
        x1 = xh * lg_ref[...] + lb_ref[...]
        dx1 = ALPHA * dr2_ref[...] + du * (1.0 + sc_ref[...])
        dxg = dx1 * lg_ref[...]
        dr1 = rs_ref[...] * (dxg - _rowmean(dxg) - xh * _rowmean(dxg * xh))
        dr1_ref[...] = dr1
        da_ref[...] = (g1_ref[...] * dr1).astype(BF16)
        acc_ref[0:1, :] += _colsum(du * x1)
        acc_ref[1:2, :] += _colsum(du)
        acc_ref[2:3, :] += _colsum(dx1 * xh)
        acc_ref[3:4, :] += _colsum(dx1)
        acc_ref[4:5, :] += _colsum(dr1 * a_ref[...].astype(F32))

    row = lambda i: (i, 0)
    const2 = lambda i: (0, 0)
    vec = pl.BlockSpec((1, d), const2)
    big = pl.BlockSpec((tm, d), row)
    return _pallas(
        body, name="ln1_bwd", grid=(n // tm,),
        out_shape=(jax.ShapeDtypeStruct((n, d), F32), jax.ShapeDtypeStruct((n, d), BF16),
                   jax.ShapeDtypeStruct((8, d), F32)),
        in_specs=[big, big, big, pl.BlockSpec((tm, 1), row), big, vec, vec, vec, vec],
        out_specs=(big, big, pl.BlockSpec((8, d), const2)),
        compiler_params=_params(_mb(48)),
    )(du2, dr2, xh1, rs1, a1, lg1, lb1, sc2, g1)


def _dw_rows(a, b, nblk, bw, tm, after, name):
    m = a.shape[0]
    nn = b.shape[1]

    def body(a_ref, b_ref, after_ref, o_ref, acc_ref):
        part = lax.dot_general(a_ref[...].astype(BF16), b_ref[...], _TN, preferred_element_type=F32)
        i = pl.program_id(1)

        @pl.when(i == 0)
        def _():
            acc_ref[...] = part

        @pl.when(i > 0)
        def _():
            acc_ref[...] += part

        @pl.when(i == pl.num_programs(1) - 1)
        def _():
            o_ref[0] = acc_ref[...].astype(BF16)

    return _pallas(
        body, name=name, grid=(nblk, m // tm),
        out_shape=jax.ShapeDtypeStruct((nblk, bw, nn), BF16),
        in_specs=[pl.BlockSpec((tm, bw), lambda j, i: (i, j)), pl.BlockSpec((tm, nn), lambda j, i: (i, 0)), _ANY],
        out_specs=pl.BlockSpec((1, bw, nn), lambda j, i: (j, 0, 0)),
        scratch_shapes=[pltpu.VMEM((bw, nn), F32)],
        compiler_params=_params(_mb(56)),
    )(a, b, after)


def _outproj_bwd(da1, wout, after):
    n, d = da1.shape
    tm = 512

    def body(a_ref, w_ref, after_ref, o_ref):
        o_ref[...] = lax.dot_general(a_ref[...], w_ref[...], _NT, preferred_element_type=F32)

    return _pallas(
        body, name="outproj_bwd", grid=(n // tm,),
        out_shape=jax.ShapeDtypeStruct((n, d), F32),
        in_specs=[pl.BlockSpec((tm, d), lambda i: (i, 0)), pl.BlockSpec((d, d), lambda i: (0, 0)), _ANY],
        out_specs=pl.BlockSpec((tm, d), lambda i: (i, 0)),
        compiler_params=_params(_mb(48)),
    )(da1, wout, after)


def _qkv_bwd(dh, wint, x, ct, dr1, sc):
    na, wcols = dh.shape
    n, d = x.shape
    tm = CTX
    nlat = n // tm

    def body(dh_ref, w_ref, x_ref, ct_ref, dr_ref, sc_ref, gx_ref, acc_ref):
        i = pl.program_id(0)

        @pl.when(i == 0)
        def _():
            acc_ref[...] = jnp.zeros_like(acc_ref)

        du = jnp.dot(dh_ref[...], w_ref[...], preferred_element_type=F32)

        @pl.when(i < nlat)
        def _():
            gx_ref[...] = ALPHA * dr_ref[...] + du * (1.0 + sc_ref[0])
            acc_ref[0:1, :] += _colsum(du)
            acc_ref[1:2, :] += _colsum(du * x_ref[...])

        @pl.when(i == nlat)
        def _():
            acc_ref[2:3, :] += _colsum(du)
            acc_ref[3:4, :] += _colsum(du * ct_ref[...])

    lat = lambda i: (jnp.minimum(i, nlat - 1), 0)
    const2 = lambda i: (0, 0)
    return _pallas(
        body, name="qkv_bwd", grid=(nlat + 1,),
        out_shape=(jax.ShapeDtypeStruct((n, d), F32), jax.ShapeDtypeStruct((8, d), F32)),
        in_specs=[pl.BlockSpec((tm, wcols), lambda i: (i, 0)), pl.BlockSpec((wcols, d), const2),
                  pl.BlockSpec((tm, d), lat), pl.BlockSpec((tm, d), const2), pl.BlockSpec((tm, d), lat),
                  pl.BlockSpec((1, 1, d), lambda i: (0, 0, 0))],
        out_specs=(pl.BlockSpec((tm, d), lat), pl.BlockSpec((8, d), const2)),
        compiler_params=_params(_mb(56)),
    )(dh, wint, x, ct, dr1, sc)


def _adam_math(w, g, m, v):
    m2 = ADAM_B1 * m + (1.0 - ADAM_B1) * g
    v2 = ADAM_B2 * v + (1.0 - ADAM_B2) * (g * g)
    m_hat = m2 * (1.0 / (1.0 - ADAM_B1 ** ADAM_STEP))
    v_hat = v2 * (1.0 / (1.0 - ADAM_B2 ** ADAM_STEP))
    delta = -ADAM_LR * (m_hat / (jnp.sqrt(v_hat) + ADAM_EPS) + ADAM_WD * w)
    return delta, m2, v2


def _adamw(w, gsrc, m, v, name, after=None):
    r, c = w.shape
    parts = gsrc.ndim == 3
    after = w if after is None else after
    tr = r
    while tr * c * 4 > _mb(2) and tr % 32 == 0:
        tr //= 2

    def body(w_ref, g_ref, m_ref, v_ref, after_ref, go_ref, d_ref, mo_ref, vo_ref):
        if parts:
            g = g_ref[0].astype(F32)
            for s in range(1, NDEV):
                g = g + g_ref[s].astype(F32)
        else:
            g = g_ref[...]
        delta, m2, v2 = _adam_math(w_ref[...], g, m_ref[...], v_ref[...])
        go_ref[...] = g
        d_ref[...] = delta
        mo_ref[...] = m2
        vo_ref[...] = v2

    tile = pl.BlockSpec((tr, c), lambda i: (i, 0))
    gspec = pl.BlockSpec((NDEV, tr, c), lambda i: (0, i, 0)) if parts else tile
    sds = jax.ShapeDtypeStruct((r, c), F32)
    return _pallas(
        body, name=name, grid=(r // tr,),
        out_shape=(sds, sds, sds, sds),
        in_specs=[tile, gspec, tile, tile, _ANY],
        out_specs=(tile, tile, tile, tile),
        compiler_params=_params(_mb(48)),
    )(w, gsrc, m, v, after)


def _adamw_t(w, gsrc_t, m, v, name):
    r, c = w.shape
    tr = 256

    def body(w_ref, g_ref, m_ref, v_ref, go_ref, d_ref, mo_ref, vo_ref):
        gt = g_ref[0].astype(F32)
        for s in range(1, NDEV):
            gt = gt + g_ref[s].astype(F32)
        g = gt.T
        delta, m2, v2 = _adam_math(w_ref[...], g, m_ref[...], v_ref[...])
        go_ref[...] = g
        d_ref[...] = delta
        mo_ref[...] = m2
        vo_ref[...] = v2

    tile = pl.BlockSpec((tr, c), lambda i: (i, 0))
    sds = jax.ShapeDtypeStruct((r, c), F32)
    return _pallas(
        body, name=name, grid=(r // tr,),
        out_shape=(sds, sds, sds, sds),
        in_specs=[tile, pl.BlockSpec((NDEV, c, tr), lambda i: (0, 0, i)), tile, tile],
        out_specs=(tile, tile, tile, tile),
        compiler_params=_params(_mb(48)),
    )(w, gsrc_t, m, v)


def _small_update(gath, dcc, cc, w_s, m_s, v_s):
    d = w_s.shape[1]

    def body(g_ref, dcc_ref, cc_ref, w_ref, m_ref, v_ref, go_ref, d_ref, mo_ref, vo_ref):
        s = g_ref[0]
        for b in range(1, NDEV):
            s = s + g_ref[b]
        dsl = dcc_ref[0, 8:9, :]
        for b in range(1, NDEV):
            dsl = dsl + dcc_ref[b, 8:9, :]
        cv = cc_ref[...]
        sg = _sigmoid(cv)
        go_ref[...] = jnp.zeros_like(go_ref)
        go_ref[0:1, :] = dsl * (sg * (1.0 + cv * (1.0 - sg)))
        go_ref[1:3, :] = s[0:2] + s[6:8]
        go_ref[3:7, :] = s[2:6]
        go_ref[7:12, :] = s[8:13]
        delta, m2, v2 = _adam_math(w_ref[...], go_ref[...], m_ref[...], v_ref[...])
        d_ref[...] = delta
        mo_ref[...] = m2
        vo_ref[...] = v2

    full = pl.BlockSpec((16, d), lambda: (0, 0))
    g3 = pl.BlockSpec((NDEV, 16, d), lambda: (0, 0, 0))
    sds = jax.ShapeDtypeStruct((16, d), F32)
    return _pallas(
        body, name="small_update",
        out_shape=(sds, sds, sds, sds),
        in_specs=[g3, g3, pl.BlockSpec((1, d), lambda: (0, 0)), full, full, full],
        out_specs=(full, full, full, full),
        compiler_params=_params(_mb(24)),
    )(gath, dcc, cc, w_s, m_s, v_s)


def _rope_tables(n):
    rows = n // GRID_W
    row_ids = jnp.repeat(jnp.arange(rows, dtype=F32), GRID_W)
    col_ids = jnp.tile(jnp.arange(GRID_W, dtype=F32), rows)
    axis_dim = HEAD // 2
    inv_freq = jnp.power(ROPE_THETA, -jnp.arange(0, axis_dim, 2, dtype=F32) / axis_dim)
    ang_r = row_ids[:, None] * inv_freq
    ang_c = col_ids[:, None] * inv_freq
    ang = jnp.concatenate([ang_r, ang_r, ang_c, ang_c], axis=-1)
    cos, sin = jnp.cos(ang), jnp.sin(ang)
    first = (jnp.arange(HEAD) % (HEAD // 2)) < HEAD // 4
    sa = jnp.where(first, -sin, 0.0)
    sb = jnp.where(first, 0.0, sin)
    ones = jnp.ones((CTX, HEAD), F32)
    zeros = jnp.zeros((CTX, HEAD), F32)
    return (jnp.concatenate([cos, ones], 0), jnp.concatenate([sa, zeros], 0), jnp.concatenate([sb, zeros], 0))


def _pad_cols(a, width):
    return jnp.pad(a, ((0, 0), (0, width - a.shape[1])))


def _pad_rows(a, rows):
    return jnp.pad(a, ((0, rows - a.shape[0]), (0, 0)))


def _pack_small(c_ctx, b_ada, ln1_g, ln1_b, ln2_g, ln2_b, qg, kg, sink, d):
    misc = _pad_cols(jnp.concatenate([qg, kg, sink], axis=1), d)
    rows = jnp.concatenate([c_ctx.reshape(1, d), b_ada.reshape(6, d), ln1_g, ln1_b, ln2_g, ln2_b, misc], axis=0)
    return _pad_rows(rows, 16)


def _unpack_small(p, d):
    return dict(c_ctx=p[0], b_ada=p[1:7].reshape(1, 6 * d), ln1_g=p[7:8], ln1_b=p[8:9], ln2_g=p[9:10], ln2_b=p[10:11],
                q_norm_g=p[11:12, 0:HEAD], k_norm_g=p[11:12, HEAD:2 * HEAD], sink_logit=p[11:12, 2 * HEAD:2 * HEAD + 8])


def kernel(x, c, ctx, c_ctx, w_ada, b_ada, w_in, q_norm_g, k_norm_g, sink_logit, w_out, ln1_g, ln1_b, w_gate, w_up, w_down, ln2_g, ln2_b, loss_target, m_c_ctx, m_w_ada, m_b_ada, m_w_in, m_q_norm_g, m_k_norm_g, m_sink_logit, m_w_out, m_ln1_g, m_ln1_b, m_w_gate, m_w_up, m_w_down, m_ln2_g, m_ln2_b, v_c_ctx, v_w_ada, v_b_ada, v_w_in, v_q_norm_g, v_k_norm_g, v_sink_logit, v_w_out, v_ln1_g, v_ln1_b, v_w_gate, v_w_up, v_w_down, v_ln2_g, v_ln2_b):
    xs, cts, tgt = x[0], ctx[0], loss_target[0]
    n, d = xs.shape
    assert cts.shape == (CTX, d) and w_in.shape[2] == IN_SHARD and w_gate.shape[2] == FFN_SHARD
    me = 4 * lax.axis_index("x") + 2 * lax.axis_index("y") + lax.axis_index("c")
    e_sh = w_ada.shape[2]

    c_g = _exchange(_pad_rows(c, 8), False, "gather_c")
    c_all = jnp.concatenate([c_g[:, 0, :], _pad_rows(c_ctx.reshape(1, d), 8)], axis=0)
    bias_sh = lax.dynamic_slice(b_ada, (0, me * e_sh), (1, e_sh))
    mods_g = _exchange(_ada_fwd(c_all, w_ada[0], bias_sh), False, "gather_mods")
    mods = jnp.transpose(mods_g, (1, 0, 2)).reshape(16, NDEV * e_sh)
    mine = lax.dynamic_slice(mods, (me, 0), (1, 6 * d))
    sh1, sc1, g1, sh2, sc2, g2 = [mine[:, k * d:(k + 1) * d] for k in range(6)]
    csh1, csc1 = mods[8:9, 0:d], mods[8:9, d:2 * d]
    sc_pair = jnp.stack([sc1, csc1])
    sh_pair = jnp.stack([sh1, csh1])

    h_win, tok = _exchange_start(w_in[0].T.astype(BF16), "chip", mods, "gather_w_in_start")
    tok, (wo_l, wg_l, wu_l, wd_l) = lax.optimization_barrier((tok, (w_out, w_gate, w_up, w_down)))
    h_wout, tok = _exchange_start(wo_l[0].astype(BF16), "chip", tok, "gather_w_out_start")
    h_wg, tok = _exchange_start(wg_l[0].T.astype(BF16), "chip", tok, "gather_w_gate_start")
    h_wu, tok = _exchange_start(wu_l[0].T.astype(BF16), "chip", tok, "gather_w_up_start")
    h_wd, tok = _exchange_start(wd_l[0].astype(BF16), "chip", tok, "gather_w_down_start")

    cos, sa, sb = _rope_tables(n)
    f_win, tok = _forward_start(_exchange_wait(h_win, "chip", tok, "gather_w_in_wait"), tok, "forward_w_in_start")
    win_g = _forward_wait(f_win, tok, "forward_w_in_wait").reshape(NDEV * IN_SHARD, d)
    u_all, h_all, t_all, kt_b = _qkv_fwd(xs, cts, sc_pair, sh_pair, win_g, q_norm_g, k_norm_g, cos, sa, sb)
    f_wout, tok = _forward_start(_exchange_wait(h_wout, "chip", t_all, "gather_w_out_wait"), t_all, "forward_w_out_start")
    o_a, p_a, linv_a = _attn_window_fwd(t_all, sink_logit, _window_bias(), tok)
    o, p_b, linv_b = _attn_global_fwd(t_all, o_a)
    f_wg, tok = _forward_start(_exchange_wait(h_wg, "chip", o, "gather_w_gate_wait"), o, "forward_w_gate_start")
    f_wu, tok = _forward_start(_exchange_wait(h_wu, "chip", tok, "gather_w_up_wait"), tok, "forward_w_up_start")
    wout_g = _forward_wait(f_wout, tok, "forward_w_out_wait").reshape(d, d)
    a1, xh1, rs1, u2 = _outproj_ln1(o, wout_g, xs, g1, ln1_g, ln1_b, sc2, sh2, tok)
    f_wd, tok = _forward_start(_exchange_wait(h_wd, "chip", rs1, "gather_w_down_wait"), rs1, "forward_w_down_start")
    ffn_w = (NDEV * FFN_SHARD, d)
    wg_g = _forward_wait(f_wg, tok, "forward_w_gate_wait").reshape(ffn_w)
    wu_g = _forward_wait(f_wu, tok, "forward_w_up_wait").reshape(ffn_w)
    sa_f, sb_f, hf = _ffn_up(u2, wg_g, wu_g, tok)
    wd_g = _forward_wait(f_wd, hf, "forward_w_down_wait").reshape(ffn_w)
    ffn = _ffn_down(hf, wd_g)
    dr2, df, loss_p, acc2 = _ln2_loss(xh1, ffn, tgt, ln1_g, ln1_b, g2, ln2_g, ln2_b)
    loss = lax.psum(loss_p[0, 0], ("x", "y", "c"))

    parts = (NDEV, FFN_SHARD, d)
    dgp = _ffn_dhf(df, wd_g, sa_f, sb_f)
    dwd_p = _dw_rows(hf, df, hf.shape[1] // FFN_TILE, FFN_TILE, min(n, 2048), loss_p, "dw_down").reshape(parts)
    h_dwd, tok = _exchange_start(dwd_p, "scatter", loss.reshape(1, 1), "scatter_dw_down_start")
    dwg_t, dwu_t = _dw_gate_up(dgp, u2, tok)
    h_dwg, tok = _exchange_start(dwg_t.reshape(parts), "scatter", tok, "scatter_dw_gate_start")
    h_dwu, tok = _exchange_start(dwu_t.reshape(parts), "scatter", tok, "scatter_dw_up_start")
    du2 = _ffn_du2(dgp, wg_g, wu_g, tok)
    dr1, da1, acc1 = _ln1_bwd(du2, dr2, xh1, rs1, a1, ln1_g, ln1_b, sc2, g1)
    dwo_p = _dw_rows(o, da1, 2, 8 * HEAD, min(n, 1024), loss_p, "dw_out").reshape(NDEV, 2 * HEAD, d)
    h_dwo, tok = _exchange_start(dwo_p, "scatter", loss_p, "scatter_dw_out_start")
    do = _outproj_bwd(da1, wout_g, tok)
    dqa, dka, dva, dsink = _attn_window_bwd(t_all, o, do, p_a, linv_a)
    dqb, dkb, dvb = _attn_global_bwd(t_all, kt_b, o, do, p_b, linv_b)
    dh_all, dnorm = _qkv_bwd_prep(dqa, dka, dva, dqb, dkb, dvb, h_all, q_norm_g, k_norm_g, cos, sa, sb)
    grad_x, acc0 = _qkv_bwd(dh_all, win_g, xs, cts, dr1, sc_pair)

    misc = _pad_cols(jnp.concatenate([dnorm[0:1], dnorm[1:2], dsink[:, 0:4, 0].reshape(1, 8)], axis=1), d)
    part = jnp.concatenate([
        acc0[0:2], acc1[4:5], acc1[1:2], acc1[0:1], acc2[2:3],
        acc0[2:4],
        acc1[2:4], acc2[0:2],
        misc, jnp.zeros((3, d), F32)], axis=0)
    gath = _exchange(part, False, "gather_small")
    dm_batch = gath[:, 0:6, :].reshape(NDEV, 6 * d)
    dm_ctx = _pad_cols(gath[:, 6:8, :].reshape(NDEV, 2 * d), 6 * d)
    dm16 = lax.dynamic_slice(jnp.concatenate([dm_batch, dm_ctx], axis=0), (0, me * e_sh), (16, e_sh))
    dw_ada, drow = _ada_bwd(dm16, c_all, w_ada[0])
    dcc = _exchange(drow, False, "gather_dcc")
    dwi_p = _dw_rows(dh_all, u_all, NDEV // 2, 2 * IN_SHARD, (n + CTX) // 2, dcc, "dw_in")
    dwi_p = dwi_p.reshape(NDEV, IN_SHARD, d)
    h_dwi, tok = _exchange_start(dwi_p, "scatter", dcc, "scatter_dw_in_start")

    w_s = _pack_small(c_ctx, b_ada, ln1_g, ln1_b, ln2_g, ln2_b, q_norm_g, k_norm_g, sink_logit, d)
    m_s = _pack_small(m_c_ctx, m_b_ada, m_ln1_g, m_ln1_b, m_ln2_g, m_ln2_b, m_q_norm_g, m_k_norm_g, m_sink_logit, d)
    v_s = _pack_small(v_c_ctx, v_b_ada, v_ln1_g, v_ln1_b, v_ln2_g, v_ln2_b, v_q_norm_g, v_k_norm_g, v_sink_logit, d)
    small = [_unpack_small(p, d) for p in _small_update(gath, dcc, c_ctx.reshape(1, d), w_s, m_s, v_s)]

    big = {}
    big["w_ada"] = _adamw(w_ada[0], dw_ada, m_w_ada[0], v_w_ada[0], "adamw_w_ada", after=tok)
    big["w_down"] = _adamw(w_down[0], _exchange_wait(h_dwd, "scatter", big["w_ada"][1], "scatter_dw_down_wait"),
                           m_w_down[0], v_w_down[0], "adamw_w_down")
    late = big["w_down"][1]
    for nm, wt, mt, vt, hd in (("w_gate", w_gate, m_w_gate, v_w_gate, h_dwg), ("w_up", w_up, m_w_up, v_w_up, h_dwu)):
        res = _adamw(wt[0].T, _exchange_wait(hd, "scatter", late, "scatter_d" + nm + "_wait"), mt[0].T, vt[0].T,
                     "adamw_" + nm)
        big[nm] = [r.T for r in res]
        late = res[1]
    big["w_out"] = _adamw(w_out[0], _exchange_wait(h_dwo, "scatter", late, "scatter_dw_out_wait"), m_w_out[0], v_w_out[0],
                          "adamw_w_out")
    big["w_in"] = _adamw_t(w_in[0], _exchange_wait(h_dwi, "scatter", big["w_out"][1], "scatter_dw_in_wait"), m_w_in[0],
                           v_w_in[0], "adamw_w_in")

    names = ["c_ctx", "w_ada", "b_ada", "w_in", "q_norm_g", "k_norm_g", "sink_logit", "w_out", "ln1_g", "ln1_b",
             "w_gate", "w_up", "w_down", "ln2_g", "ln2_b"]
    outs = [loss, grad_x[None]]
    for k in range(4):
        for nm in names:
            outs.append(big[nm][k][None] if nm in big else small[k][nm])
    return tuple(outs)
```

```python
import functools

import jax
import jax.numpy as jnp
from jax import lax
from jax.experimental import pallas as pl
from jax.experimental.pallas import tpu as pltpu

F32 = jnp.float32
BF16 = jnp.bfloat16

NDEV = 8
HEAD = 128
CTX = 256
GRID_W = 64
WINDOW = 128
WIN_KEYS = 3 * WINDOW + CTX
WIN_P = WIN_KEYS + HEAD
ROPE_THETA = 10000.0
EPS = 1e-6
SCALE = HEAD ** -0.5
LOG2E = 1.4426950408889634
QK_LOG2 = SCALE * LOG2E
ALPHA = 2.0 ** 0.25
FFN_SHARD = 704
FFN_TILE = 512
IN_SHARD = 384
NEG = -1e30

ADAM_LR = 0.001
ADAM_B1 = 0.9
ADAM_B2 = 0.999
ADAM_EPS = 1e-08
ADAM_WD = 0.01
ADAM_STEP = 10

VMEM_CAP = 56 * 1024 * 1024

_KINDS = ["rope"] * 10 + ["none"] * 2 + ["qnorm"] * 8 + ["knorm"] * 2 + ["none"] * 2
NORM_HEAD0 = _KINDS.index("qnorm")
NORM_HEADS = _KINDS.count("qnorm") + _KINDS.count("knorm")

_NT = (((1,), (1,)), ((), ()))
_TN = (((0,), (0,)), ((), ()))


def _pallas(body, **kw):
    return pl.pallas_call(body, **kw)


def _params(vmem_bytes):
    return pltpu.CompilerParams(vmem_limit_bytes=int(min(VMEM_CAP, vmem_bytes)))


def _mb(n):
    return int(n * 1024 * 1024)


def _sigmoid(x):
    return 1.0 / (1.0 + jnp.exp(-x))


def _colsum(a):
    return jnp.sum(a, axis=0, keepdims=True)


def _rowmean(a):
    return jnp.mean(a, axis=-1, keepdims=True)


def _exchange(src, scatter, name, after=None):
    blk = src.shape[1:] if scatter else src.shape
    after = src if after is None else after

    def body(src_ref, after_ref, out_ref, send_sems, recv_sems, local_sem):
        x, y, c = lax.axis_index("x"), lax.axis_index("y"), lax.axis_index("c")
        me = 4 * x + 2 * y + c
        copies = []
        for t in range(1, NDEV):
            px = 1 - x if (t >> 2) & 1 else x
            py = 1 - y if (t >> 1) & 1 else y
            pc = 1 - c if t & 1 else c
            peer = 4 * px + 2 * py + pc
            cp = pltpu.make_async_remote_copy(
                src_ref=src_ref.at[peer] if scatter else src_ref,
                dst_ref=out_ref.at[me],
                send_sem=send_sems.at[t - 1],
                recv_sem=recv_sems.at[t - 1],
                device_id=(px, py, pc),
                device_id_type=pl.DeviceIdType.MESH,
            )
            cp.start()
            copies.append(cp)
        own = pltpu.make_async_copy(src_ref.at[me] if scatter else src_ref, out_ref.at[me], local_sem)
        own.start()
        for cp in copies:
            cp.wait()
        own.wait()

    return _pallas(
        body, name=name,
        out_shape=jax.ShapeDtypeStruct((NDEV,) + tuple(blk), src.dtype),
        in_specs=[pl.BlockSpec(memory_space=pl.ANY), pl.BlockSpec(memory_space=pl.ANY)],
        out_specs=pl.BlockSpec(memory_space=pl.ANY),
        scratch_shapes=[pltpu.SemaphoreType.DMA((NDEV - 1,)), pltpu.SemaphoreType.DMA((NDEV - 1,)),
                        pltpu.SemaphoreType.DMA(())],
    )(src, after)


_HBM = pl.BlockSpec(memory_space=pltpu.HBM)
_SEM = pl.BlockSpec(memory_space=pltpu.SEMAPHORE)
_ANY = pl.BlockSpec(memory_space=pl.ANY)
_EFFECT = pltpu.SideEffectType.DATAFLOW_SIDE_EFFECTING


def _exchange_copies(src_ref, land_ref, send_sems, recv_sems, mode):
    x, y, c = lax.axis_index("x"), lax.axis_index("y"), lax.axis_index("c")
    me = 4 * x + 2 * y + c
    scatter = mode == "scatter"
    copies = []
    for t in ((1, 2, 4, 6) if mode == "chip" else range(1, NDEV)):
        px = 1 - x if (t >> 2) & 1 else x
        py = 1 - y if (t >> 1) & 1 else y
        pc = 1 - c if t & 1 else c
        peer = 4 * px + 2 * py + pc
        copies.append(pltpu.make_async_remote_copy(
            src_ref=src_ref.at[peer] if scatter else src_ref,
            dst_ref=land_ref.at[me],
            send_sem=send_sems.at[t - 1],
            recv_sem=recv_sems.at[t - 1],
            device_id=(px, py, pc),
            device_id_type=pl.DeviceIdType.MESH,
        ))
    own = pltpu.make_async_copy(src_ref.at[me] if scatter else src_ref, land_ref.at[me], send_sems.at[NDEV - 1])
    return copies, own


def _forward_copies(land_ref, send_sems, recv_sems):
    x, y, c = lax.axis_index("x"), lax.axis_index("y"), lax.axis_index("c")
    copies = []
    for k, t in enumerate((2, 4, 6)):
        px = 1 - x if (t >> 2) & 1 else x
        py = 1 - y if (t >> 1) & 1 else y
        mine, theirs = 4 * px + 2 * py + c, 4 * px + 2 * py + (1 - c)
        send = pltpu.make_async_remote_copy(
            src_ref=land_ref.at[mine], dst_ref=land_ref.at[mine], send_sem=send_sems.at[k], recv_sem=recv_sems.at[k],
            device_id=(x, y, 1 - c), device_id_type=pl.DeviceIdType.MESH)
        recv = pltpu.make_async_remote_copy(
            src_ref=land_ref.at[theirs], dst_ref=land_ref.at[theirs], send_sem=send_sems.at[k], recv_sem=recv_sems.at[k],
            device_id=(x, y, 1 - c), device_id_type=pl.DeviceIdType.MESH)
        copies.append((send, recv))
    return copies


def _forward_start(land, after, name):
    def body(land_ref, after_ref, send_sems, recv_sems, land_thru, token):
        for send, _ in _forward_copies(land_ref, send_sems, recv_sems):
            send.start()
        token[...] = jnp.zeros_like(token)

    res = _pallas(
        body, name=name,
        out_shape=(pltpu.SemaphoreType.DMA((3,)), pltpu.SemaphoreType.DMA((3,)), pltpu.HBM(land.shape, land.dtype),
                   jax.ShapeDtypeStruct((8, HEAD), F32)),
        in_specs=(_HBM, _ANY), out_specs=(_SEM, _SEM, _HBM, pl.BlockSpec(memory_space=pltpu.VMEM)),
        input_output_aliases={0: 2},
        compiler_params=pltpu.CompilerParams(has_side_effects=_EFFECT),
    )(land, after)
    return res[:3], res[3]


def _forward_wait(handle, after, name):
    send_sems, recv_sems, land_thru = handle

    def body(land_ref, send_sems, recv_sems, after_ref, got_ref):
        for send, recv in _forward_copies(land_ref, send_sems, recv_sems):
            send.wait_send()
            recv.wait_recv()

    return _pallas(
        body, name=name,
        out_shape=pltpu.HBM(land_thru.shape, land_thru.dtype),
        in_specs=(_HBM, _SEM, _SEM, _ANY), out_specs=_HBM,
        input_output_aliases={0: 0},
        compiler_params=pltpu.CompilerParams(has_side_effects=_EFFECT),
    )(land_thru, send_sems, recv_sems, after)


def _exchange_start(src, mode, after, name):
    blk = src.shape[1:] if mode == "scatter" else src.shape
    land = lax.empty((NDEV,) + tuple(blk), src.dtype)

    def body(src_ref, land_ref, after_ref, send_sems, recv_sems, src_thru, land_thru, token):
        copies, own = _exchange_copies(src_ref, land_ref, send_sems, recv_sems, mode)
        for cp in copies:
            cp.start()
        own.start()
        token[...] = jnp.zeros_like(token)

    res = _pallas(
        body, name=name,
        out_shape=(pltpu.SemaphoreType.DMA((NDEV,)), pltpu.SemaphoreType.DMA((NDEV,)),
                   pltpu.HBM(src.shape, src.dtype), pltpu.HBM(land.shape, land.dtype),
                   jax.ShapeDtypeStruct((8, HEAD), F32)),
        in_specs=(_HBM, _HBM, _ANY), out_specs=(_SEM, _SEM, _HBM, _HBM, pl.BlockSpec(memory_space=pltpu.VMEM)),
        input_output_aliases={0: 2, 1: 3},
        compiler_params=pltpu.CompilerParams(has_side_effects=_EFFECT),
    )(pltpu.with_memory_space_constraint(src, pltpu.HBM), pltpu.with_memory_space_constraint(land, pltpu.HBM), after)
    return res[:4], res[4]


def _exchange_wait(handle, mode, after, name):
    send_sems, recv_sems, src_thru, land_thru = handle

    def body(src_ref, land_ref, send_sems, recv_sems, after_ref, src_dead, got_ref):
        copies, own = _exchange_copies(src_ref, land_ref, send_sems, recv_sems, mode)
        for cp in copies:
            cp.wait_send()
            cp.wait_recv()
        own.wait()

    return _pallas(
        body, name=name,
        out_shape=(pltpu.HBM(src_thru.shape, src_thru.dtype), pltpu.HBM(land_thru.shape, land_thru.dtype)),
        in_specs=(_HBM, _HBM, _SEM, _SEM, _ANY), out_specs=(_HBM, _HBM),
        input_output_aliases={0: 0, 1: 1},
        compiler_params=pltpu.CompilerParams(has_side_effects=_EFFECT),
    )(src_thru, land_thru, send_sems, recv_sems, after)[1]


def _ada_fwd(c_all, w, bias):
    r, d = c_all.shape
    e = w.shape[1]
    tn = 512

    def body(c_ref, w_ref, b_ref, o_ref):
        cv = c_ref[...]
        s = (cv * _sigmoid(cv)).astype(BF16)
        o_ref[...] = jnp.dot(s, w_ref[...].astype(BF16), preferred_element_type=F32) + b_ref[...]

    return _pallas(
        body, name="ada_fwd", grid=(e // tn,),
        out_shape=jax.ShapeDtypeStruct((r, e), F32),
        in_specs=[pl.BlockSpec((r, d), lambda j: (0, 0)), pl.BlockSpec((d, tn), lambda j: (0, j)),
                  pl.BlockSpec((1, tn), lambda j: (0, j))],
        out_specs=pl.BlockSpec((r, tn), lambda j: (0, j)),
        compiler_params=_params(_mb(24)),
    )(c_all, w, bias)


def _ada_bwd(dm16, c_all, w):
    d, e = w.shape
    tn = 512

    def body(dm_ref, c_ref, w_ref, dw_ref, dr_ref):
        j = pl.program_id(0)
        dm = dm_ref[...]
        rid = lax.broadcasted_iota(jnp.int32, dm.shape, 0)
        ctx_sum = jnp.sum(jnp.where(rid >= 8, dm, 0.0), axis=0, keepdims=True)
        rows = jnp.where(rid < 8, dm, jnp.where(rid == 8, jnp.broadcast_to(ctx_sum, dm.shape), 0.0)).astype(BF16)
        cv = c_ref[...]
        s = (cv * _sigmoid(cv)).astype(BF16)
        dw_ref[...] = lax.dot_general(s, rows, _TN, preferred_element_type=F32)
        part = lax.dot_general(rows, w_ref[...].astype(BF16), _NT, preferred_element_type=F32)

        @pl.when(j == 0)
        def _():
            dr_ref[...] = part

        @pl.when(j > 0)
        def _():
            dr_ref[...] += part

    return _pallas(
        body, name="ada_bwd", grid=(e // tn,),
        out_shape=(jax.ShapeDtypeStruct((d, e), F32), jax.ShapeDtypeStruct((16, d), F32)),
        in_specs=[pl.BlockSpec((16, tn), lambda j: (0, j)), pl.BlockSpec((16, d), lambda j: (0, 0)),
                  pl.BlockSpec((d, tn), lambda j: (0, j))],
        out_specs=(pl.BlockSpec((d, tn), lambda j: (0, j)), pl.BlockSpec((16, d), lambda j: (0, 0))),
        compiler_params=_params(_mb(32)),
    )(dm16, c_all, w)


def _rope(v, cos, sa, sb):
    return v * cos + (pltpu.roll(v, 96, 1) * sa + pltpu.roll(v, 32, 1) * sb)


def _rope_t(dt, cos, sa, sb):
    return dt * cos + (pltpu.roll(dt * sa, 32, 1) + pltpu.roll(dt * sb, 96, 1))


def _qkv_fwd(x, ct, sc, sh, wint, qg, kg, cos, sa, sb):
    n, d = x.shape
    tm = CTX
    nlat = n // tm
    na = n + CTX
    wcols = wint.shape[0]

    def body(x_ref, ct_ref, sc_ref, sh_ref, w_ref, qg_ref, kg_ref, cos_ref, sa_ref, sb_ref, u_ref, h_ref, t_ref, kt_ref):
        i = pl.program_id(0)
        xin = jnp.where(i == nlat, ct_ref[...], x_ref[...])
        u = (xin * (1.0 + sc_ref[0]) + sh_ref[0]).astype(BF16)
        u_ref[...] = u
        cos, sa, sb = cos_ref[...], sa_ref[...], sb_ref[...]
        h = lax.dot_general(u, w_ref[...], _NT, preferred_element_type=F32)
        h_ref[...] = h[:, NORM_HEAD0 * HEAD:(NORM_HEAD0 + NORM_HEADS) * HEAD]
        for hd in range(24):
            v = h[:, hd * HEAD:(hd + 1) * HEAD]
            kind = _KINDS[hd]
            if kind == "qnorm":
                v = v * lax.rsqrt(_rowmean(v * v) + EPS) * qg_ref[...]
            elif kind == "knorm":
                v = v * lax.rsqrt(_rowmean(v * v) + EPS) * kg_ref[...]
            if kind != "none":
                v = _rope(v, cos, sa, sb)
            t_ref[:, hd * HEAD:(hd + 1) * HEAD] = v.astype(BF16)
            if kind == "knorm":
                kt_ref[(hd - 20) * HEAD:(hd - 19) * HEAD, :] = v.T.astype(BF16)

    lat = lambda i: (jnp.minimum(i, nlat - 1), 0)
    row = lambda i: (i, 0)
    const2 = lambda i: (0, 0)
    return _pallas(
        body, name="qkv_fwd", grid=(nlat + 1,),
        out_shape=(jax.ShapeDtypeStruct((na, d), BF16), jax.ShapeDtypeStruct((na, NORM_HEADS * HEAD), F32),
                   jax.ShapeDtypeStruct((na, wcols), BF16), jax.ShapeDtypeStruct((2 * HEAD, na), BF16)),
        in_specs=[pl.BlockSpec((tm, d), lat), pl.BlockSpec((tm, d), const2),
                  pl.BlockSpec((1, 1, d), lambda i: (i // nlat, 0, 0)),
                  pl.BlockSpec((1, 1, d), lambda i: (i // nlat, 0, 0)),
                  pl.BlockSpec((wcols, d), const2),
                  pl.BlockSpec((1, HEAD), const2), pl.BlockSpec((1, HEAD), const2),
                  pl.BlockSpec((tm, HEAD), row), pl.BlockSpec((tm, HEAD), row), pl.BlockSpec((tm, HEAD), row)],
        out_specs=(pl.BlockSpec((tm, d), row), pl.BlockSpec((tm, NORM_HEADS * HEAD), row), pl.BlockSpec((tm, wcols), row),
                   pl.BlockSpec((2 * HEAD, tm), lambda i: (0, i))),
        compiler_params=_params(_mb(56)),
    )(x, ct, sc, sh, wint, qg, kg, cos, sa, sb)


def _qkv_bwd_prep(dqa, dka, dva, dqb, dkb, dvb, h_norm, qg, kg, cos, sa, sb):
    na = h_norm.shape[0]
    wcols = 24 * HEAD
    n = na - CTX
    tm = CTX
    nlat = n // tm

    def body(dqa_ref, dka_ref, dva_ref, dqb_ref, dkb_ref, dvb_ref, h_ref, qg_ref, kg_ref, cos_ref, sa_ref, sb_ref,
             dh_ref, dg_ref):
        i = pl.program_id(0)

        @pl.when(i == 0)
        def _():
            dg_ref[...] = jnp.zeros_like(dg_ref)

        cos, sa, sb = cos_ref[...], sa_ref[...], sb_ref[...]
        is_lat = i < nlat
        for hd in range(24):
            kind = _KINDS[hd]
            if hd < 8:
                dt = jnp.where(is_lat, dqa_ref[:, hd * HEAD:(hd + 1) * HEAD], 0.0)
            elif hd < 10:
                dt = dka_ref[:, (hd - 8) * HEAD:(hd - 7) * HEAD]
            elif hd < 12:
                dt = dva_ref[:, (hd - 10) * HEAD:(hd - 9) * HEAD]
            elif hd < 20:
                dt = jnp.where(is_lat, dqb_ref[:, (hd - 12) * HEAD:(hd - 11) * HEAD], 0.0)
            elif hd < 22:
                dt = dkb_ref[:, (hd - 20) * HEAD:(hd - 19) * HEAD]
            else:
                dt = dvb_ref[:, (hd - 22) * HEAD:(hd - 21) * HEAD]
            if kind != "none":
                dt = _rope_t(dt, cos, sa, sb)
            if kind in ("qnorm", "knorm"):
                g_ref = qg_ref if kind == "qnorm" else kg_ref
                r0 = 0 if kind == "qnorm" else 1
                xv = h_ref[:, (hd - NORM_HEAD0) * HEAD:(hd - NORM_HEAD0 + 1) * HEAD]
                xn = xv * lax.rsqrt(_rowmean(xv * xv) + EPS)
                dg_ref[r0:r0 + 1, :] += _colsum(dt * xn)
                dxn = dt * g_ref[...]
                dt = lax.rsqrt(_rowmean(xv * xv) + EPS) * (dxn - xn * _rowmean(dxn * xn))
            dh_ref[:, hd * HEAD:(hd + 1) * HEAD] = dt.astype(BF16)

    lat = lambda i: (jnp.minimum(i, nlat - 1), 0)
    row = lambda i: (i, 0)
    const2 = lambda i: (0, 0)
    return _pallas(
        body, name="qkv_bwd_prep", grid=(nlat + 1,),
        out_shape=(jax.ShapeDtypeStruct((na, wcols), BF16), jax.ShapeDtypeStruct((8, HEAD), F32)),
        in_specs=[pl.BlockSpec((tm, 8 * HEAD), lat), pl.BlockSpec((tm, 2 * HEAD), row), pl.BlockSpec((tm, 2 * HEAD), row),
                  pl.BlockSpec((tm, 8 * HEAD), lat), pl.BlockSpec((tm, 2 * HEAD), row), pl.BlockSpec((tm, 2 * HEAD), row),
                  pl.BlockSpec((tm, NORM_HEADS * HEAD), row),
                  pl.BlockSpec((1, HEAD), const2), pl.BlockSpec((1, HEAD), const2),
                  pl.BlockSpec((tm, HEAD), row), pl.BlockSpec((tm, HEAD), row), pl.BlockSpec((tm, HEAD), row)],
        out_specs=(pl.BlockSpec((tm, wcols), row), pl.BlockSpec((8, HEAD), const2)),
        compiler_params=_params(_mb(40)),
    )(dqa, dka, dva, dqb, dkb, dvb, h_norm, qg, kg, cos, sa, sb)


def _window_keys(k_ref, v_ref, n, na):
    i = pl.program_id(1)
    tq = WINDOW
    start = pl.multiple_of(jnp.clip((i - 1) * tq, 0, n - 3 * tq), tq)
    kk = jnp.concatenate([k_ref[pl.ds(start, 3 * tq), :], k_ref[n:na, :]], axis=0)
    vv = jnp.concatenate([v_ref[pl.ds(start, 3 * tq), :], v_ref[n:na, :]], axis=0)
    return kk, vv, start


def _window_bias():
    tq = WINDOW
    r = (jnp.arange(4 * tq) % tq)[:, None]
    c = jnp.arange(3 * tq + CTX)[None, :]
    variants = []
    for back in (0, tq, 2 * tq):
        seen = (jnp.abs(back + r - c) <= WINDOW) | (c >= 3 * tq)
        variants.append(jnp.where(seen, 0.0, NEG).astype(F32))
    return jnp.stack(variants)


def _window_bias_spec(nq):
    return pl.BlockSpec((1, 4 * WINDOW, 3 * WINDOW + CTX),
                        lambda kv, i: (jnp.where(i == 0, 0, jnp.where(i == nq - 1, 2, 1)), 0, 0))


def _stack_heads(ref, width=HEAD):
    return jnp.concatenate([ref[:, g * HEAD:g * HEAD + width] for g in range(4)], axis=0)


def _sink_column(sink_ref, kv, tq):
    grp = lax.broadcasted_iota(jnp.int32, (4 * tq, 1), 0) // tq
    col = jnp.zeros((4 * tq, 1), F32)
    for g in range(4):
        col = jnp.where(grp == g, sink_ref[0, 4 * kv + g] * LOG2E, col)
    return col


def _attn_window_fwd(t_all, sink, bias, after):
    na = t_all.shape[0]
    n = na - CTX
    tq = WINDOW

    def body(sink_ref, q_ref, k_ref, v_ref, bias_ref, after_ref, o_ref, p_ref, linv_ref):
        kv = pl.program_id(0)
        kk, vv, _ = _window_keys(k_ref, v_ref, n, na)
        t = lax.dot_general(_stack_heads(q_ref), kk, _NT, preferred_element_type=F32) * QK_LOG2 + bias_ref[0]
        sk = _sink_column(sink_ref, kv, tq)
        m = jnp.maximum(jnp.max(t, axis=-1, keepdims=True), sk)
        p = jnp.exp2(t - m)
        p_sink = jnp.exp2(sk - m)
        linv = 1.0 / (jnp.sum(p, axis=-1, keepdims=True) + p_sink)
        pb = p.astype(BF16)
        o = jnp.dot(pb, vv, preferred_element_type=F32) * linv
        p_all = jnp.concatenate([pb, jnp.broadcast_to(p_sink, (4 * tq, WIN_P - WIN_KEYS)).astype(BF16)], axis=1)
        for g in range(4):
            o_ref[:, g * HEAD:(g + 1) * HEAD] = o[g * tq:(g + 1) * tq]
            p_ref[g] = p_all[g * tq:(g + 1) * tq]
            linv_ref[:, g * HEAD:(g + 1) * HEAD] = jnp.broadcast_to(linv[g * tq:(g + 1) * tq], (tq, HEAD))

    blk = pl.BlockSpec((tq, 4 * HEAD), lambda kv, i: (i, kv))
    return _pallas(
        body, name="attn_window_fwd", grid=(2, n // tq),
        out_shape=(jax.ShapeDtypeStruct((n, 16 * HEAD), F32), jax.ShapeDtypeStruct((8, n, WIN_P), BF16),
                   jax.ShapeDtypeStruct((n, 8 * HEAD), F32)),
        in_specs=[pl.BlockSpec(memory_space=pltpu.SMEM), blk,
                  pl.BlockSpec((na, HEAD), lambda kv, i: (0, 8 + kv)),
                  pl.BlockSpec((na, HEAD), lambda kv, i: (0, 10 + kv)), _window_bias_spec(n // tq), _ANY],
        out_specs=(blk, pl.BlockSpec((4, tq, WIN_P), lambda kv, i: (kv, i, 0)), blk),
        compiler_params=_params(_mb(32)),
    )(sink, t_all, t_all, t_all, bias, after)


def _attn_global_fwd(t_all, o_part):
    na = t_all.shape[0]
    n = na - CTX
    tq = 256

    def body(q_ref, k_ref, v_ref, o_in_ref, o_ref, p_ref, linv_ref):
        kk, vv = k_ref[...], v_ref[...]
        for g in range(4):
            q = q_ref[:, g * HEAD:(g + 1) * HEAD]
            t = lax.dot_general(q, kk, _NT, preferred_element_type=F32) * QK_LOG2
            m = jnp.max(t, axis=-1, keepdims=True)
            p = jnp.exp2(t - m)
            linv = 1.0 / jnp.sum(p, axis=-1, keepdims=True)
            pb = p.astype(BF16)
            p_ref[g] = pb
            o_ref[:, g * HEAD:(g + 1) * HEAD] = jnp.dot(pb, vv, preferred_element_type=F32) * linv
            linv_ref[:, g * HEAD:(g + 1) * HEAD] = jnp.broadcast_to(linv, (tq, HEAD))

    return _pallas(
        body, name="attn_global_fwd", grid=(2, n // tq),
        out_shape=(jax.ShapeDtypeStruct((n, 16 * HEAD), F32), jax.ShapeDtypeStruct((8, n, na), BF16),
                   jax.ShapeDtypeStruct((n, 8 * HEAD), F32)),
        in_specs=[pl.BlockSpec((tq, 4 * HEAD), lambda kv, i: (i, 3 + kv)),
                  pl.BlockSpec((na, HEAD), lambda kv, i: (0, 20 + kv)),
                  pl.BlockSpec((na, HEAD), lambda kv, i: (0, 22 + kv)), _ANY],
        out_specs=(pl.BlockSpec((tq, 4 * HEAD), lambda kv, i: (i, 2 + kv)),
                   pl.BlockSpec((4, tq, na), lambda kv, i: (kv, i, 0)),
                   pl.BlockSpec((tq, 4 * HEAD), lambda kv, i: (i, kv))),
        input_output_aliases={3: 0},
        compiler_params=_params(_mb(56)),
    )(t_all, t_all, t_all, o_part)


def _attn_window_bwd(t_all, o, do, p_all, linv):
    na = t_all.shape[0]
    n = na - CTX
    tq = WINDOW

    def body(q_ref, k_ref, v_ref, o_ref, do_ref, p_ref, linv_ref, dq_ref, dk_ref, dv_ref, dsink_ref):
        @pl.when(pl.program_id(1) == 0)
        def _():
            dk_ref[...] = jnp.zeros_like(dk_ref)
            dv_ref[...] = jnp.zeros_like(dv_ref)
            dsink_ref[...] = jnp.zeros_like(dsink_ref)

        kk, vv, start = _window_keys(k_ref, v_ref, n, na)
        q = _stack_heads(q_ref)
        p_full = jnp.concatenate([p_ref[g] for g in range(4)], axis=0).astype(F32) * _stack_heads(linv_ref, 1)
        p = p_full[:, :WIN_KEYS]
        dof = _stack_heads(do_ref)
        delta = jnp.sum(dof * _stack_heads(o_ref), axis=-1, keepdims=True)
        dob = dof.astype(BF16)
        dv_acc = lax.dot_general(p.astype(BF16), dob, _TN, preferred_element_type=F32)
        dp = lax.dot_general(dob, vv, _NT, preferred_element_type=F32)
        ds = (p * (dp - delta) * SCALE).astype(BF16)
        dq = jnp.dot(ds, kk, preferred_element_type=F32)
        dk_acc = lax.dot_general(ds, q, _TN, preferred_element_type=F32)
        dsk = -(p_full[:, WIN_KEYS:WIN_KEYS + 1] * delta)
        for g in range(4):
            dq_ref[:, g * HEAD:(g + 1) * HEAD] = dq[g * tq:(g + 1) * tq]
            dsink_ref[0, g:g + 1, :] += jnp.broadcast_to(_colsum(dsk[g * tq:(g + 1) * tq]), (1, HEAD))
        dk_ref[pl.ds(start, 3 * tq), :] += dk_acc[:3 * tq]
        dv_ref[pl.ds(start, 3 * tq), :] += dv_acc[:3 * tq]
        dk_ref[n:na, :] += dk_acc[3 * tq:]
        dv_ref[n:na, :] += dv_acc[3 * tq:]

    blk = pl.BlockSpec((tq, 4 * HEAD), lambda kv, i: (i, kv))
    kvout = pl.BlockSpec((na, HEAD), lambda kv, i: (0, kv))
    return _pallas(
        body, name="attn_window_bwd", grid=(2, n // tq),
        out_shape=(jax.ShapeDtypeStruct((n, 8 * HEAD), F32), jax.ShapeDtypeStruct((na, 2 * HEAD), F32),
                   jax.ShapeDtypeStruct((na, 2 * HEAD), F32), jax.ShapeDtypeStruct((2, 8, HEAD), F32)),
        in_specs=[blk,
                  pl.BlockSpec((na, HEAD), lambda kv, i: (0, 8 + kv)),
                  pl.BlockSpec((na, HEAD), lambda kv, i: (0, 10 + kv)),
                  blk, blk, pl.BlockSpec((4, tq, WIN_P), lambda kv, i: (kv, i, 0)), blk],
        out_specs=(blk, kvout, kvout, pl.BlockSpec((1, 8, HEAD), lambda kv, i: (kv, 0, 0))),
        compiler_params=_params(_mb(40)),
    )(t_all, t_all, t_all, o, do, p_all, linv)


def _attn_global_bwd(t_all, kt, o, do, p_all, linv):
    na = t_all.shape[0]
    n = na - CTX
    tq = 256

    def body(q_ref, v_ref, kt_ref, o_ref, do_ref, p_ref, linv_ref, dq_ref, dk_ref, dv_ref, dkt_acc, dvt_acc):
        i = pl.program_id(1)

        @pl.when(i == 0)
        def _():
            dkt_acc[...] = jnp.zeros_like(dkt_acc)
            dvt_acc[...] = jnp.zeros_like(dvt_acc)

        half = na // 2
        for c0 in (0, half):
            vv, kt_v = v_ref[c0:c0 + half, :], kt_ref[:, c0:c0 + half]
            dkt = jnp.zeros((HEAD, half), F32)
            dvt = jnp.zeros((HEAD, half), F32)
            for g in range(4):
                q = q_ref[:, g * HEAD:(g + 1) * HEAD]
                dof = do_ref[:, g * HEAD:(g + 1) * HEAD]
                dob = dof.astype(BF16)
                delta = jnp.sum(dof * o_ref[:, g * HEAD:(g + 1) * HEAD], axis=-1, keepdims=True)
                p = p_ref[g, :, c0:c0 + half].astype(F32) * linv_ref[:, g * HEAD:g * HEAD + 1]
                dp = lax.dot_general(dob, vv, _NT, preferred_element_type=F32)
                dvt = dvt + lax.dot_general(dob, p.astype(BF16), _TN, preferred_element_type=F32)
                ds = (p * (dp - delta) * SCALE).astype(BF16)
                dqt = lax.dot_general(kt_v, ds, _NT, preferred_element_type=F32).T
                if c0 == 0:
                    dq_ref[:, g * HEAD:(g + 1) * HEAD] = dqt
                else:
                    dq_ref[:, g * HEAD:(g + 1) * HEAD] += dqt
                dkt = dkt + lax.dot_general(q, ds, _TN, preferred_element_type=F32)
            dkt_acc[:, c0:c0 + half] += dkt
            dvt_acc[:, c0:c0 + half] += dvt

        @pl.when(i == pl.num_programs(1) - 1)
        def _():
            dk_ref[...] = dkt_acc[...].T
            dv_ref[...] = dvt_acc[...].T

    ospec = pl.BlockSpec((tq, 4 * HEAD), lambda kv, i: (i, 2 + kv))
    lspec = pl.BlockSpec((tq, 4 * HEAD), lambda kv, i: (i, kv))
    kvout = pl.BlockSpec((na, HEAD), lambda kv, i: (0, kv))
    return _pallas(
        body, name="attn_global_bwd", grid=(2, n // tq),
        out_shape=(jax.ShapeDtypeStruct((n, 8 * HEAD), F32), jax.ShapeDtypeStruct((na, 2 * HEAD), F32),
                   jax.ShapeDtypeStruct((na, 2 * HEAD), F32)),
        in_specs=[pl.BlockSpec((tq, 4 * HEAD), lambda kv, i: (i, 3 + kv)),
                  pl.BlockSpec((na, HEAD), lambda kv, i: (0, 22 + kv)),
                  pl.BlockSpec((HEAD, na), lambda kv, i: (kv, 0)),
                  ospec, ospec, pl.BlockSpec((4, tq, na), lambda kv, i: (kv, i, 0)), lspec],
        out_specs=(lspec, kvout, kvout),
        scratch_shapes=[pltpu.VMEM((HEAD, na), F32), pltpu.VMEM((HEAD, na), F32)],
        compiler_params=_params(_mb(56)),
    )(t_all, t_all, kt, o, do, p_all, linv)


def _outproj_ln1(o, wout, x, g1, lg, lb, sc2, sh2, after):
    n, d = x.shape
    tm = 256

    def body(o_ref, w_ref, x_ref, g1_ref, lg_ref, lb_ref, sc_ref, sh_ref, after_ref, a_ref, xh_ref, rs_ref, u_ref):
        a1 = jnp.dot(o_ref[...].astype(BF16), w_ref[...], preferred_element_type=F32)
        a_ref[...] = a1.astype(BF16)
        r = ALPHA * x_ref[...] + g1_ref[...] * a1
        dlt = r - _rowmean(r)
        rstd = lax.rsqrt(_rowmean(dlt * dlt) + EPS)
        xh = dlt * rstd
        xh_ref[...] = xh
        rs_ref[...] = rstd
        x1 = xh * lg_ref[...] + lb_ref[...]
        u_ref[...] = (x1 * (1.0 + sc_ref[...]) + sh_ref[...]).astype(BF16)

    row = lambda i: (i, 0)
    const2 = lambda i: (0, 0)
    vec = pl.BlockSpec((1, d), const2)
    big = pl.BlockSpec((tm, d), row)
    return _pallas(
        body, name="outproj_ln1", grid=(n // tm,),
        out_shape=(jax.ShapeDtypeStruct((n, d), BF16), jax.ShapeDtypeStruct((n, d), F32),
                   jax.ShapeDtypeStruct((n, 1), F32), jax.ShapeDtypeStruct((n, d), BF16)),
        in_specs=[big, pl.BlockSpec((d, d), const2), big, vec, vec, vec, vec, vec, _ANY],
        out_specs=(big, big, pl.BlockSpec((tm, 1), row), big),
        compiler_params=_params(_mb(56)),
    )(o, wout, x, g1, lg, lb, sc2, sh2, after)


def _ffn_up(u2, wgt, wut, after):
    n, d = u2.shape
    f = wgt.shape[0]
    tm = min(1024, n)

    def body(u_ref, wg_ref, wu_ref, after_ref, sa_ref, sb_ref, hf_ref):
        u = u_ref[...]
        gv = lax.dot_general(u, wg_ref[...], _NT, preferred_element_type=F32)
        pv = lax.dot_general(u, wu_ref[...], _NT, preferred_element_type=F32)
        sg = _sigmoid(gv)
        silu = gv * sg
        sa_ref[...] = silu.astype(BF16)
        sb_ref[...] = (pv * (sg * (1.0 + gv * (1.0 - sg)))).astype(BF16)
        hf_ref[...] = (silu * pv).astype(BF16)

    tile = pl.BlockSpec((tm, FFN_TILE), lambda i, j: (i, j))
    wspec = pl.BlockSpec((FFN_TILE, d), lambda i, j: (j, 0))
    sds = jax.ShapeDtypeStruct((n, f), BF16)
    return _pallas(
        body, name="ffn_up", grid=(n // tm, f // FFN_TILE),
        out_shape=(sds, sds, sds),
        in_specs=[pl.BlockSpec((tm, d), lambda i, j: (i, 0)), wspec, wspec, _ANY],
        out_specs=(tile, tile, tile),
        compiler_params=_params(_mb(48)),
    )(u2, wgt, wut, after)


def _ffn_down(hf, wd):
    n, f = hf.shape
    d = wd.shape[1]
    tm, tn = min(1024, n), 512

    def body(h_ref, w_ref, o_ref):
        o_ref[...] = jnp.dot(h_ref[...], w_ref[...], preferred_element_type=F32)

    return _pallas(
        body, name="ffn_down", grid=(n // tm, d // tn),
        out_shape=jax.ShapeDtypeStruct((n, d), F32),
        in_specs=[pl.BlockSpec((tm, f), lambda i, j: (i, 0)), pl.BlockSpec((f, tn), lambda i, j: (0, j))],
        out_specs=pl.BlockSpec((tm, tn), lambda i, j: (i, j)),
        compiler_params=_params(_mb(56)),
    )(hf, wd)


def _ln2_loss(xh1, ffn, tgt, lg1, lb1, g2, lg2, lb2):
    n, d = xh1.shape
    tm = 256

    def body(xh_ref, f_ref, t_ref, lg1_ref, lb1_ref, g2_ref, lg2_ref, lb2_ref, dr_ref, df_ref, loss_ref, acc_ref):
        @pl.when(pl.program_id(0) == 0)
        def _():
            loss_ref[...] = jnp.zeros_like(loss_ref)
            acc_ref[...] = jnp.zeros_like(acc_ref)

        x1 = xh_ref[...] * lg1_ref[...] + lb1_ref[...]
        fv = f_ref[...]
        r = ALPHA * x1 + g2_ref[...] * fv
        dlt = r - _rowmean(r)
        rstd = lax.rsqrt(_rowmean(dlt * dlt) + EPS)
        xh2 = dlt * rstd
        err = xh2 * lg2_ref[...] + lb2_ref[...] - t_ref[...]
        loss_ref[...] += 0.5 * jnp.sum(_rowmean(err * err))
        dy = err * (1.0 / d)
        dyg = dy * lg2_ref[...]
        dr = rstd * (dyg - _rowmean(dyg) - xh2 * _rowmean(dyg * xh2))
        dr_ref[...] = dr
        df_ref[...] = (g2_ref[...] * dr).astype(BF16)
        acc_ref[0:1, :] += _colsum(dy * xh2)
        acc_ref[1:2, :] += _colsum(dy)
        acc_ref[2:3, :] += _colsum(dr * fv)

    row = lambda i: (i, 0)
    const2 = lambda i: (0, 0)
    vec = pl.BlockSpec((1, d), const2)
    big = pl.BlockSpec((tm, d), row)
    return _pallas(
        body, name="ln2_loss", grid=(n // tm,),
        out_shape=(jax.ShapeDtypeStruct((n, d), F32), jax.ShapeDtypeStruct((n, d), BF16),
                   jax.ShapeDtypeStruct((8, HEAD), F32), jax.ShapeDtypeStruct((8, d), F32)),
        in_specs=[big, big, big, vec, vec, vec, vec, vec],
        out_specs=(big, big, pl.BlockSpec((8, HEAD), const2), pl.BlockSpec((8, d), const2)),
        compiler_params=_params(_mb(48)),
    )(xh1, ffn, tgt, lg1, lb1, g2, lg2, lb2)


def _ffn_dhf(df, wd, sa, sb):
    n, d = df.shape
    f = sa.shape[1]
    tm = min(2048, n)

    def body(df_ref, w_ref, sa_ref, sb_ref, dgp_ref):
        dhf = lax.dot_general(df_ref[...], w_ref[...], _NT, preferred_element_type=F32)
        dgp_ref[:, :FFN_TILE] = (dhf * sb_ref[...].astype(F32)).astype(BF16)
        dgp_ref[:, FFN_TILE:] = (dhf * sa_ref[...].astype(F32)).astype(BF16)

    tile = pl.BlockSpec((tm, FFN_TILE), lambda i, j: (i, j))
    return _pallas(
        body, name="ffn_dhf", grid=(n // tm, f // FFN_TILE),
        out_shape=jax.ShapeDtypeStruct((n, 2 * f), BF16),
        in_specs=[pl.BlockSpec((tm, d), lambda i, j: (i, 0)), pl.BlockSpec((FFN_TILE, d), lambda i, j: (j, 0)),
                  tile, tile],
        out_specs=pl.BlockSpec((tm, 2 * FFN_TILE), lambda i, j: (i, j)),
        compiler_params=_params(_mb(48)),
    )(df, wd, sa, sb)


def _ffn_du2(dgp, wgt, wut, after):
    n = dgp.shape[0]
    f, d = wgt.shape
    tm = min(1024, n)

    def body(dgp_ref, wg_ref, wu_ref, after_ref, o_ref):
        w = jnp.concatenate([wg_ref[...], wu_ref[...]], axis=0)
        part = jnp.dot(dgp_ref[...], w, preferred_element_type=F32)

        @pl.when(pl.program_id(1) == 0)
        def _():
            o_ref[...] = part

        @pl.when(pl.program_id(1) > 0)
        def _():
            o_ref[...] += part

    wspec = pl.BlockSpec((FFN_TILE, d), lambda i, j: (j, 0))
    return _pallas(
        body, name="ffn_du2", grid=(n // tm, f // FFN_TILE),
        out_shape=jax.ShapeDtypeStruct((n, d), F32),
        in_specs=[pl.BlockSpec((tm, 2 * FFN_TILE), lambda i, j: (i, j)), wspec, wspec, _ANY],
        out_specs=pl.BlockSpec((tm, d), lambda i, j: (i, 0)),
        compiler_params=_params(_mb(48)),
    )(dgp, wgt, wut, after)


def _dw_gate_up(dgp, u2, after):
    n, d = u2.shape
    f = dgp.shape[1] // 2
    tm = min(2048, n)

    def body(a_ref, b_ref, after_ref, og_ref, ou_ref, acc_ref):
        part = lax.dot_general(a_ref[...], b_ref[...], _TN, preferred_element_type=F32)
        i = pl.program_id(1)

        @pl.when(i == 0)
        def _():
            acc_ref[...] = part

        @pl.when(i > 0)
        def _():
            acc_ref[...] += part

        @pl.when(i == pl.num_programs(1) - 1)
        def _():
            og_ref[...] = acc_ref[:FFN_TILE].astype(BF16)
            ou_ref[...] = acc_ref[FFN_TILE:].astype(BF16)

    out = pl.BlockSpec((FFN_TILE, d), lambda j, i: (j, 0))
    sds = jax.ShapeDtypeStruct((f, d), BF16)
    return _pallas(
        body, name="dw_gate_up", grid=(f // FFN_TILE, n // tm),
        out_shape=(sds, sds),
        in_specs=[pl.BlockSpec((tm, 2 * FFN_TILE), lambda j, i: (i, j)), pl.BlockSpec((tm, d), lambda j, i: (i, 0)), _ANY],
        out_specs=(out, out),
        scratch_shapes=[pltpu.VMEM((2 * FFN_TILE, d), F32)],
        compiler_params=_params(_mb(56)),
    )(dgp, u2, after)


def _ln1_bwd(du2, dr2, xh1, rs1, a1, lg1, lb1, sc2, g1):
    n, d = du2.shape
    tm = 256

    def body(du_ref, dr2_ref, xh_ref, rs_ref, a_ref, lg_ref, lb_ref, sc_ref, g1_ref, dr1_ref, da_ref, acc_ref):
        @pl.when(pl.program_id(0) == 0)
        def _():
            acc_ref[...] = jnp.zeros_like(acc_ref)

        du = du_ref[...]
        xh = xh_ref[...]
        x1 = xh * lg_ref[...] + lb_ref[...]
        dx1 = ALPHA * dr2_ref[...] + du * (1.0 + sc_ref[...])
        dxg = dx1 * lg_ref[...]
        dr1 = rs_ref[...] * (dxg - _rowmean(dxg) - xh * _rowmean(dxg * xh))
        dr1_ref[...] = dr1
        da_ref[...] = (g1_ref[...] * dr1).astype(BF16)
        acc_ref[0:1, :] += _colsum(du * x1)
        acc_ref[1:2, :] += _colsum(du)
        acc_ref[2:3, :] += _colsum(dx1 * xh)
        acc_ref[3:4, :] += _colsum(dx1)
        acc_ref[4:5, :] += _colsum(dr1 * a_ref[...].astype(F32))

    row = lambda i: (i, 0)
    const2 = lambda i: (0, 0)
    vec = pl.BlockSpec((1, d), const2)
    big = pl.BlockSpec((tm, d), row)
    return _pallas(
        body, name="ln1_bwd", grid=(n // tm,),
        out_shape=(jax.ShapeDtypeStruct((n, d), F32), jax.ShapeDtypeStruct((n, d), BF16),
                   jax.ShapeDtypeStruct((8, d), F32)),
        in_specs=[big, big, big, pl.BlockSpec((tm, 1), row), big, vec, vec, vec, vec],
        out_specs=(big, big, pl.BlockSpec((8, d), const2)),
        compiler_params=_params(_mb(48)),
    )(du2, dr2, xh1, rs1, a1, lg1, lb1, sc2, g1)


def _dw_rows(a, b, nblk, bw, tm, after, name):
    m = a.shape[0]
    nn = b.shape[1]

    def body(a_ref, b_ref, after_ref, o_ref, acc_ref):
        part = lax.dot_general(a_ref[...].astype(BF16), b_ref[...], _TN, preferred_element_type=F32)
        i = pl.program_id(1)

        @pl.when(i == 0)
        def _():
            acc_ref[...] = part

        @pl.when(i > 0)
        def _():
            acc_ref[...] += part

        @pl.when(i == pl.num_programs(1) - 1)
        def _():
            o_ref[0] = acc_ref[...].astype(BF16)

    return _pallas(
        body, name=name, grid=(nblk, m // tm),
        out_shape=jax.ShapeDtypeStruct((nblk, bw, nn), BF16),
        in_specs=[pl.BlockSpec((tm, bw), lambda j, i: (i, j)), pl.BlockSpec((tm, nn), lambda j, i: (i, 0)), _ANY],
        out_specs=pl.BlockSpec((1, bw, nn), lambda j, i: (j, 0, 0)),
        scratch_shapes=[pltpu.VMEM((bw, nn), F32)],
        compiler_params=_params(_mb(56)),
    )(a, b, after)


def _outproj_bwd(da1, wout, after):
    n, d = da1.shape
    tm = 512

    def body(a_ref, w_ref, after_ref, o_ref):
        o_ref[...] = lax.dot_general(a_ref[...], w_ref[...], _NT, preferred_element_type=F32)

    return _pallas(
        body, name="outproj_bwd", grid=(n // tm,),
        out_shape=jax.ShapeDtypeStruct((n, d), F32),
        in_specs=[pl.BlockSpec((tm, d), lambda i: (i, 0)), pl.BlockSpec((d, d), lambda i: (0, 0)), _ANY],
        out_specs=pl.BlockSpec((tm, d), lambda i: (i, 0)),
        compiler_params=_params(_mb(48)),
    )(da1, wout, after)


def _qkv_bwd(dh, wint, x, ct, dr1, sc):
    na, wcols = dh.shape
    n, d = x.shape
    tm = CTX
    nlat = n // tm

    def body(dh_ref, w_ref, x_ref, ct_ref, dr_ref, sc_ref, gx_ref, acc_ref):
        i = pl.program_id(0)

        @pl.when(i == 0)
        def _():
            acc_ref[...] = jnp.zeros_like(acc_ref)

        du = jnp.dot(dh_ref[...], w_ref[...], preferred_element_type=F32)

        @pl.when(i < nlat)
        def _():
            gx_ref[...] = ALPHA * dr_ref[...] + du * (1.0 + sc_ref[0])
            acc_ref[0:1, :] += _colsum(du)
            acc_ref[1:2, :] += _colsum(du * x_ref[...])

        @pl.when(i == nlat)
        def _():
            acc_ref[2:3, :] += _colsum(du)
            acc_ref[3:4, :] += _colsum(du * ct_ref[...])

    lat = lambda i: (jnp.minimum(i, nlat - 1), 0)
    const2 = lambda i: (0, 0)
    return _pallas(
        body, name="qkv_bwd", grid=(nlat + 1,),
        out_shape=(jax.ShapeDtypeStruct((n, d), F32), jax.ShapeDtypeStruct((8, d), F32)),
        in_specs=[pl.BlockSpec((tm, wcols), lambda i: (i, 0)), pl.BlockSpec((wcols, d), const2),
                  pl.BlockSpec((tm, d), lat), pl.BlockSpec((tm, d), const2), pl.BlockSpec((tm, d), lat),
                  pl.BlockSpec((1, 1, d), lambda i: (0, 0, 0))],
        out_specs=(pl.BlockSpec((tm, d), lat), pl.BlockSpec((8, d), const2)),
        compiler_params=_params(_mb(56)),
    )(dh, wint, x, ct, dr1, sc)


def _adam_math(w, g, m, v):
    m2 = ADAM_B1 * m + (1.0 - ADAM_B1) * g
    v2 = ADAM_B2 * v + (1.0 - ADAM_B2) * (g * g)
    m_hat = m2 * (1.0 / (1.0 - ADAM_B1 ** ADAM_STEP))
    v_hat = v2 * (1.0 / (1.0 - ADAM_B2 ** ADAM_STEP))
    delta = -ADAM_LR * (m_hat / (jnp.sqrt(v_hat) + ADAM_EPS) + ADAM_WD * w)
    return delta, m2, v2


def _adamw(w, gsrc, m, v, name, after=None):
    r, c = w.shape
    parts = gsrc.ndim == 3
    after = w if after is None else after
    tr = r
    while tr * c * 4 > _mb(1) and tr % 32 == 0:
        tr //= 2

    def body(w_ref, g_ref, m_ref, v_ref, after_ref, go_ref, d_ref, mo_ref, vo_ref):
        if parts:
            g = g_ref[0].astype(F32)
            for s in range(1, NDEV):
                g = g + g_ref[s].astype(F32)
        else:
            g = g_ref[...]
        delta, m2, v2 = _adam_math(w_ref[...], g, m_ref[...], v_ref[...])
        go_ref[...] = g
        d_ref[...] = delta
        mo_ref[...] = m2
        vo_ref[...] = v2

    tile = pl.BlockSpec((tr, c), lambda i: (i, 0))
    gspec = pl.BlockSpec((NDEV, tr, c), lambda i: (0, i, 0)) if parts else tile
    sds = jax.ShapeDtypeStruct((r, c), F32)
    return _pallas(
        body, name=name, grid=(r // tr,),
        out_shape=(sds, sds, sds, sds),
        in_specs=[tile, gspec, tile, tile, _ANY],
        out_specs=(tile, tile, tile, tile),
        compiler_params=_params(_mb(48)),
    )(w, gsrc, m, v, after)


def _adamw_t(w, gsrc_t, m, v, name):
    r, c = w.shape
    tr = 256

    def body(w_ref, g_ref, m_ref, v_ref, go_ref, d_ref, mo_ref, vo_ref):
        gt = g_ref[0].astype(F32)
        for s in range(1, NDEV):
            gt = gt + g_ref[s].astype(F32)
        g = gt.T
        delta, m2, v2 = _adam_math(w_ref[...], g, m_ref[...], v_ref[...])
        go_ref[...] = g
        d_ref[...] = delta
        mo_ref[...] = m2
        vo_ref[...] = v2

    tile = pl.BlockSpec((tr, c), lambda i: (i, 0))
    sds = jax.ShapeDtypeStruct((r, c), F32)
    return _pallas(
        body, name=name, grid=(r // tr,),
        out_shape=(sds, sds, sds, sds),
        in_specs=[tile, pl.BlockSpec((NDEV, c, tr), lambda i: (0, 0, i)), tile, tile],
        out_specs=(tile, tile, tile, tile),
        compiler_params=_params(_mb(48)),
    )(w, gsrc_t, m, v)


def _small_update(gath, dcc, cc, w_s, m_s, v_s):
    d = w_s.shape[1]

    def body(g_ref, dcc_ref, cc_ref, w_ref, m_ref, v_ref, go_ref, d_ref, mo_ref, vo_ref):
        s = g_ref[0]
        for b in range(1, NDEV):
            s = s + g_ref[b]
        dsl = dcc_ref[0, 8:9, :]
        for b in range(1, NDEV):
            dsl = dsl + dcc_ref[b, 8:9, :]
        cv = cc_ref[...]
        sg = _sigmoid(cv)
        go_ref[...] = jnp.zeros_like(go_ref)
        go_ref[0:1, :] = dsl * (sg * (1.0 + cv * (1.0 - sg)))
        go_ref[1:3, :] = s[0:2] + s[6:8]
        go_ref[3:7, :] = s[2:6]
        go_ref[7:12, :] = s[8:13]
        delta, m2, v2 = _adam_math(w_ref[...], go_ref[...], m_ref[...], v_ref[...])
        d_ref[...] = delta
        mo_ref[...] = m2
        vo_ref[...] = v2

    full = pl.BlockSpec((16, d), lambda: (0, 0))
    g3 = pl.BlockSpec((NDEV, 16, d), lambda: (0, 0, 0))
    sds = jax.ShapeDtypeStruct((16, d), F32)
    return _pallas(
        body, name="small_update",
        out_shape=(sds, sds, sds, sds),
        in_specs=[g3, g3, pl.BlockSpec((1, d), lambda: (0, 0)), full, full, full],
        out_specs=(full, full, full, full),
        compiler_params=_params(_mb(24)),
    )(gath, dcc, cc, w_s, m_s, v_s)


def _rope_tables(n):
    rows = n // GRID_W
    row_ids = jnp.repeat(jnp.arange(rows, dtype=F32), GRID_W)
    col_ids = jnp.tile(jnp.arange(GRID_W, dtype=F32), rows)
    axis_dim = HEAD // 2
    inv_freq = jnp.power(ROPE_THETA, -jnp.arange(0, axis_dim, 2, dtype=F32) / axis_dim)
    ang_r = row_ids[:, None] * inv_freq
    ang_c = col_ids[:, None] * inv_freq
    ang = jnp.concatenate([ang_r, ang_r, ang_c, ang_c], axis=-1)
    cos, sin = jnp.cos(ang), jnp.sin(ang)
    first = (jnp.arange(HEAD) % (HEAD // 2)) < HEAD // 4
    sa = jnp.where(first, -sin, 0.0)
    sb = jnp.where(first, 0.0, sin)
    ones = jnp.ones((CTX, HEAD), F32)
    zeros = jnp.zeros((CTX, HEAD), F32)
    return (jnp.concatenate([cos, ones], 0), jnp.concatenate([sa, zeros], 0), jnp.concatenate([sb, zeros], 0))


def _pad_cols(a, width):
    return jnp.pad(a, ((0, 0), (0, width - a.shape[1])))


def _pad_rows(a, rows):
    return jnp.pad(a, ((0, rows - a.shape[0]), (0, 0)))


def _pack_small(c_ctx, b_ada, ln1_g, ln1_b, ln2_g, ln2_b, qg, kg, sink, d):
    misc = _pad_cols(jnp.concatenate([qg, kg, sink], axis=1), d)
    rows = jnp.concatenate([c_ctx.reshape(1, d), b_ada.reshape(6, d), ln1_g, ln1_b, ln2_g, ln2_b, misc], axis=0)
    return _pad_rows(rows, 16)


def _unpack_small(p, d):
    return dict(c_ctx=p[0], b_ada=p[1:7].reshape(1, 6 * d), ln1_g=p[7:8], ln1_b=p[8:9], ln2_g=p[9:10], ln2_b=p[10:11],
                q_norm_g=p[11:12, 0:HEAD], k_norm_g=p[11:12, HEAD:2 * HEAD], sink_logit=p[11:12, 2 * HEAD:2 * HEAD + 8])


def kernel(x, c, ctx, c_ctx, w_ada, b_ada, w_in, q_norm_g, k_norm_g, sink_logit, w_out, ln1_g, ln1_b, w_gate, w_up, w_down, ln2_g, ln2_b, loss_target, m_c_ctx, m_w_ada, m_b_ada, m_w_in, m_q_norm_g, m_k_norm_g, m_sink_logit, m_w_out, m_ln1_g, m_ln1_b, m_w_gate, m_w_up, m_w_down, m_ln2_g, m_ln2_b, v_c_ctx, v_w_ada, v_b_ada, v_w_in, v_q_norm_g, v_k_norm_g, v_sink_logit, v_w_out, v_ln1_g, v_ln1_b, v_w_gate, v_w_up, v_w_down, v_ln2_g, v_ln2_b):
    xs, cts, tgt = x[0], ctx[0], loss_target[0]
    n, d = xs.shape
    assert cts.shape == (CTX, d) and w_in.shape[2] == IN_SHARD and w_gate.shape[2] == FFN_SHARD
    me = 4 * lax.axis_index("x") + 2 * lax.axis_index("y") + lax.axis_index("c")
    e_sh = w_ada.shape[2]

    c_g = _exchange(_pad_rows(c, 8), False, "gather_c")
    c_all = jnp.concatenate([c_g[:, 0, :], _pad_rows(c_ctx.reshape(1, d), 8)], axis=0)
    bias_sh = lax.dynamic_slice(b_ada, (0, me * e_sh), (1, e_sh))
    mods_g = _exchange(_ada_fwd(c_all, w_ada[0], bias_sh), False, "gather_mods")
    mods = jnp.transpose(mods_g, (1, 0, 2)).reshape(16, NDEV * e_sh)
    mine = lax.dynamic_slice(mods, (me, 0), (1, 6 * d))
    sh1, sc1, g1, sh2, sc2, g2 = [mine[:, k * d:(k + 1) * d] for k in range(6)]
    csh1, csc1 = mods[8:9, 0:d], mods[8:9, d:2 * d]
    sc_pair = jnp.stack([sc1, csc1])
    sh_pair = jnp.stack([sh1, csh1])

    h_win, tok = _exchange_start(w_in[0].T.astype(BF16), "chip", mods, "gather_w_in_start")
    tok, (wo_l, wg_l, wu_l, wd_l) = lax.optimization_barrier((tok, (w_out, w_gate, w_up, w_down)))
    h_wout, tok = _exchange_start(wo_l[0].astype(BF16), "chip", tok, "gather_w_out_start")
    h_wg, tok = _exchange_start(wg_l[0].T.astype(BF16), "chip", tok, "gather_w_gate_start")
    h_wu, tok = _exchange_start(wu_l[0].T.astype(BF16), "chip", tok, "gather_w_up_start")
    h_wd, tok = _exchange_start(wd_l[0].astype(BF16), "chip", tok, "gather_w_down_start")

    cos, sa, sb = _rope_tables(n)
    f_win, tok = _forward_start(_exchange_wait(h_win, "chip", tok, "gather_w_in_wait"), tok, "forward_w_in_start")
    win_g = _forward_wait(f_win, tok, "forward_w_in_wait").reshape(NDEV * IN_SHARD, d)
    u_all, h_all, t_all, kt_b = _qkv_fwd(xs, cts, sc_pair, sh_pair, win_g, q_norm_g, k_norm_g, cos, sa, sb)
    f_wout, tok = _forward_start(_exchange_wait(h_wout, "chip", t_all, "gather_w_out_wait"), t_all, "forward_w_out_start")
    o_a, p_a, linv_a = _attn_window_fwd(t_all, sink_logit, _window_bias(), tok)
    o, p_b, linv_b = _attn_global_fwd(t_all, o_a)
    f_wg, tok = _forward_start(_exchange_wait(h_wg, "chip", o, "gather_w_gate_wait"), o, "forward_w_gate_start")
    f_wu, tok = _forward_start(_exchange_wait(h_wu, "chip", tok, "gather_w_up_wait"), tok, "forward_w_up_start")
    wout_g = _forward_wait(f_wout, tok, "forward_w_out_wait").reshape(d, d)
    a1, xh1, rs1, u2 = _outproj_ln1(o, wout_g, xs, g1, ln1_g, ln1_b, sc2, sh2, tok)
    f_wd, tok = _forward_start(_exchange_wait(h_wd, "chip", rs1, "gather_w_down_wait"), rs1, "forward_w_down_start")
    ffn_w = (NDEV * FFN_SHARD, d)
    wg_g = _forward_wait(f_wg, tok, "forward_w_gate_wait").reshape(ffn_w)
    wu_g = _forward_wait(f_wu, tok, "forward_w_up_wait").reshape(ffn_w)
    sa_f, sb_f, hf = _ffn_up(u2, wg_g, wu_g, tok)
    wd_g = _forward_wait(f_wd, hf, "forward_w_down_wait").reshape(ffn_w)
    ffn = _ffn_down(hf, wd_g)
    dr2, df, loss_p, acc2 = _ln2_loss(xh1, ffn, tgt, ln1_g, ln1_b, g2, ln2_g, ln2_b)
    loss = lax.psum(loss_p[0, 0], ("x", "y", "c"))

    parts = (NDEV, FFN_SHARD, d)
    dgp = _ffn_dhf(df, wd_g, sa_f, sb_f)
    dwd_p = _dw_rows(hf, df, hf.shape[1] // FFN_TILE, FFN_TILE, min(n, 2048), loss_p, "dw_down").reshape(parts)
    h_dwd, tok = _exchange_start(dwd_p, "scatter", loss.reshape(1, 1), "scatter_dw_down_start")
    dwg_t, dwu_t = _dw_gate_up(dgp, u2, tok)
    h_dwg, tok = _exchange_start(dwg_t.reshape(parts), "scatter", tok, "scatter_dw_gate_start")
    h_dwu, tok = _exchange_start(dwu_t.reshape(parts), "scatter", tok, "scatter_dw_up_start")
    du2 = _ffn_du2(dgp, wg_g, wu_g, tok)
    dr1, da1, acc1 = _ln1_bwd(du2, dr2, xh1, rs1, a1, ln1_g, ln1_b, sc2, g1)
    dwo_p = _dw_rows(o, da1, 2, 8 * HEAD, min(n, 1024), loss_p, "dw_out").reshape(NDEV, 2 * HEAD, d)
    h_dwo, tok = _exchange_start(dwo_p, "scatter", loss_p, "scatter_dw_out_start")
    do = _outproj_bwd(da1, wout_g, tok)
    dqa, dka, dva, dsink = _attn_window_bwd(t_all, o, do, p_a, linv_a)
    dqb, dkb, dvb = _attn_global_bwd(t_all, kt_b, o, do, p_b, linv_b)
    dh_all, dnorm = _qkv_bwd_prep(dqa, dka, dva, dqb, dkb, dvb, h_all, q_norm_g, k_norm_g, cos, sa, sb)
    grad_x, acc0 = _qkv_bwd(dh_all, win_g, xs, cts, dr1, sc_pair)

    misc = _pad_cols(jnp.concatenate([dnorm[0:1], dnorm[1:2], dsink[:, 0:4, 0].reshape(1, 8)], axis=1), d)
    part = jnp.concatenate([
        acc0[0:2], acc1[4:5], acc1[1:2], acc1[0:1], acc2[2:3],
        acc0[2:4],
        acc1[2:4], acc2[0:2],
        misc, jnp.zeros((3, d), F32)], axis=0)
    gath = _exchange(part, False, "gather_small")
    dm_batch = gath[:, 0:6, :].reshape(NDEV, 6 * d)
    dm_ctx = _pad_cols(gath[:, 6:8, :].reshape(NDEV, 2 * d), 6 * d)
    dm16 = lax.dynamic_slice(jnp.concatenate([dm_batch, dm_ctx], axis=0), (0, me * e_sh), (16, e_sh))
    dw_ada, drow = _ada_bwd(dm16, c_all, w_ada[0])
    dcc = _exchange(drow, False, "gather_dcc")
    dwi_p = _dw_rows(dh_all, u_all, NDEV // 2, 2 * IN_SHARD, (n + CTX) // 2, dcc, "dw_in")
    dwi_p = dwi_p.reshape(NDEV, IN_SHARD, d)
    h_dwi, tok = _exchange_start(dwi_p, "scatter", dcc, "scatter_dw_in_start")

    w_s = _pack_small(c_ctx, b_ada, ln1_g, ln1_b, ln2_g, ln2_b, q_norm_g, k_norm_g, sink_logit, d)
    m_s = _pack_small(m_c_ctx, m_b_ada, m_ln1_g, m_ln1_b, m_ln2_g, m_ln2_b, m_q_norm_g, m_k_norm_g, m_sink_logit, d)
    v_s = _pack_small(v_c_ctx, v_b_ada, v_ln1_g, v_ln1_b, v_ln2_g, v_ln2_b, v_q_norm_g, v_k_norm_g, v_sink_logit, d)
    small = [_unpack_small(p, d) for p in _small_update(gath, dcc, c_ctx.reshape(1, d), w_s, m_s, v_s)]

    big = {}
    big["w_ada"] = _adamw(w_ada[0], dw_ada, m_w_ada[0], v_w_ada[0], "adamw_w_ada", after=tok)
    big["w_down"] = _adamw(w_down[0], _exchange_wait(h_dwd, "scatter", big["w_ada"][1], "scatter_dw_down_wait"),
                           m_w_down[0], v_w_down[0], "adamw_w_down")
    late = big["w_down"][1]
    for nm, wt, mt, vt, hd in (("w_gate", w_gate, m_w_gate, v_w_gate, h_dwg), ("w_up", w_up, m_w_up, v_w_up, h_dwu)):
        res = _adamw(wt[0].T, _exchange_wait(hd, "scatter", late, "scatter_d" + nm + "_wait"), mt[0].T, vt[0].T,
                     "adamw_" + nm)
        big[nm] = [r.T for r in res]
        late = res[1]
    big["w_out"] = _adamw(w_out[0], _exchange_wait(h_dwo, "scatter", late, "scatter_dw_out_wait"), m_w_out[0], v_w_out[0],
                          "adamw_w_out")
    big["w_in"] = _adamw_t(w_in[0], _exchange_wait(h_dwi, "scatter", big["w_out"][1], "scatter_dw_in_wait"), m_w_in[0],
                           v_w_in[0], "adamw_w_in")

    names = ["c_ctx", "w_ada", "b_ada", "w_in", "q_norm_g", "k_norm_g", "sink_logit", "w_out", "ln1_g", "ln1_b",
             "w_gate", "w_up", "w_down", "ln2_g", "ln2_b"]
    outs = [loss, grad_x[None]]
    for k in range(4):
        for nm in names:
            outs.append(big[nm][k][None] if nm in big else small[k][nm])
    return tuple(outs)
```

```python
import functools

import jax
import jax.numpy as jnp
from jax import lax
from jax.experimental import pallas as pl
from jax.experimental.pallas import tpu as pltpu

F32 = jnp.float32
BF16 = jnp.bfloat16

NDEV = 8
HEAD = 128
CTX = 256
GRID_W = 64
WINDOW = 128
WIN_KEYS = 3 * WINDOW + CTX
WIN_P = WIN_KEYS + HEAD
ROPE_THETA = 10000.0
EPS = 1e-6
SCALE = HEAD ** -0.5
LOG2E = 1.4426950408889634
QK_LOG2 = SCALE * LOG2E
ALPHA = 2.0 ** 0.25
FFN_SHARD = 704
FFN_TILE = 512
IN_SHARD = 384
NEG = -1e30

ADAM_LR = 0.001
ADAM_B1 = 0.9
ADAM_B2 = 0.999
ADAM_EPS = 1e-08
ADAM_WD = 0.01
ADAM_STEP = 10

VMEM_CAP = 56 * 1024 * 1024

_KINDS = ["rope"] * 10 + ["none"] * 2 + ["qnorm"] * 8 + ["knorm"] * 2 + ["none"] * 2
NORM_HEAD0 = _KINDS.index("qnorm")
NORM_HEADS = _KINDS.count("qnorm") + _KINDS.count("knorm")

_NT = (((1,), (1,)), ((), ()))
_TN = (((0,), (0,)), ((), ()))


def _pallas(body, **kw):
    return pl.pallas_call(body, **kw)


def _params(vmem_bytes):
    return pltpu.CompilerParams(vmem_limit_bytes=int(min(VMEM_CAP, vmem_bytes)))


def _mb(n):
    return int(n * 1024 * 1024)


def _sigmoid(x):
    return 1.0 / (1.0 + jnp.exp(-x))


def _colsum(a):
    return jnp.sum(a, axis=0, keepdims=True)


def _rowmean(a):
    return jnp.mean(a, axis=-1, keepdims=True)


def _exchange(src, scatter, name, after=None):
    blk = src.shape[1:] if scatter else src.shape
    after = src if after is None else after

    def body(src_ref, after_ref, out_ref, send_sems, recv_sems, local_sem):
        x, y, c = lax.axis_index("x"), lax.axis_index("y"), lax.axis_index("c")
        me = 4 * x + 2 * y + c
        copies = []
        for t in range(1, NDEV):
            px = 1 - x if (t >> 2) & 1 else x
            py = 1 - y if (t >> 1) & 1 else y
            pc = 1 - c if t & 1 else c
            peer = 4 * px + 2 * py + pc
            cp = pltpu.make_async_remote_copy(
                src_ref=src_ref.at[peer] if scatter else src_ref,
                dst_ref=out_ref.at[me],
                send_sem=send_sems.at[t - 1],
                recv_sem=recv_sems.at[t - 1],
                device_id=(px, py, pc),
                device_id_type=pl.DeviceIdType.MESH,
            )
            cp.start()
            copies.append(cp)
        own = pltpu.make_async_copy(src_ref.at[me] if scatter else src_ref, out_ref.at[me], local_sem)
        own.start()
        for cp in copies:
            cp.wait()
        own.wait()

    return _pallas(
        body, name=name,
        out_shape=jax.ShapeDtypeStruct((NDEV,) + tuple(blk), src.dtype),
        in_specs=[pl.BlockSpec(memory_space=pl.ANY), pl.BlockSpec(memory_space=pl.ANY)],
        out_specs=pl.BlockSpec(memory_space=pl.ANY),
        scratch_shapes=[pltpu.SemaphoreType.DMA((NDEV - 1,)), pltpu.SemaphoreType.DMA((NDEV - 1,)),
                        pltpu.SemaphoreType.DMA(())],
    )(src, after)


_HBM = pl.BlockSpec(memory_space=pltpu.HBM)
_SEM = pl.BlockSpec(memory_space=pltpu.SEMAPHORE)
_ANY = pl.BlockSpec(memory_space=pl.ANY)
_EFFECT = pltpu.SideEffectType.DATAFLOW_SIDE_EFFECTING


def _exchange_copies(src_ref, land_ref, send_sems, recv_sems, mode):
    x, y, c = lax.axis_index("x"), lax.axis_index("y"), lax.axis_index("c")
    me = 4 * x + 2 * y + c
    scatter = mode == "scatter"
    copies = []
    for t in ((1, 2, 4, 6) if mode == "chip" else range(1, NDEV)):
        px = 1 - x if (t >> 2) & 1 else x
        py = 1 - y if (t >> 1) & 1 else y
        pc = 1 - c if t & 1 else c
        peer = 4 * px + 2 * py + pc
        copies.append(pltpu.make_async_remote_copy(
            src_ref=src_ref.at[peer] if scatter else src_ref,
            dst_ref=land_ref.at[me],
            send_sem=send_sems.at[t - 1],
            recv_sem=recv_sems.at[t - 1],
            device_id=(px, py, pc),
            device_id_type=pl.DeviceIdType.MESH,
        ))
    own = pltpu.make_async_copy(src_ref.at[me] if scatter else src_ref, land_ref.at[me], send_sems.at[NDEV - 1])
    return copies, own


def _forward_copies(land_ref, send_sems, recv_sems):
    x, y, c = lax.axis_index("x"), lax.axis_index("y"), lax.axis_index("c")
    copies = []
    for k, t in enumerate((2, 4, 6)):
        px = 1 - x if (t >> 2) & 1 else x
        py = 1 - y if (t >> 1) & 1 else y
        mine, theirs = 4 * px + 2 * py + c, 4 * px + 2 * py + (1 - c)
        send = pltpu.make_async_remote_copy(
            src_ref=land_ref.at[mine], dst_ref=land_ref.at[mine], send_sem=send_sems.at[k], recv_sem=recv_sems.at[k],
            device_id=(x, y, 1 - c), device_id_type=pl.DeviceIdType.MESH)
        recv = pltpu.make_async_remote_copy(
            src_ref=land_ref.at[theirs], dst_ref=land_ref.at[theirs], send_sem=send_sems.at[k], recv_sem=recv_sems.at[k],
            device_id=(x, y, 1 - c), device_id_type=pl.DeviceIdType.MESH)
        copies.append((send, recv))
    return copies


def _forward_start(land, after, name):
    def body(land_ref, after_ref, send_sems, recv_sems, land_thru, token):
        for send, _ in _forward_copies(land_ref, send_sems, recv_sems):
            send.start()
        token[...] = jnp.zeros_like(token)

    res = _pallas(
        body, name=name,
        out_shape=(pltpu.SemaphoreType.DMA((3,)), pltpu.SemaphoreType.DMA((3,)), pltpu.HBM(land.shape, land.dtype),
                   jax.ShapeDtypeStruct((8, HEAD), F32)),
        in_specs=(_HBM, _ANY), out_specs=(_SEM, _SEM, _HBM, pl.BlockSpec(memory_space=pltpu.VMEM)),
        input_output_aliases={0: 2},
        compiler_params=pltpu.CompilerParams(has_side_effects=_EFFECT),
    )(land, after)
    return res[:3], res[3]


def _forward_wait(handle, after, name):
    send_sems, recv_sems, land_thru = handle

    def body(land_ref, send_sems, recv_sems, after_ref, got_ref):
        for send, recv in _forward_copies(land_ref, send_sems, recv_sems):
            send.wait_send()
            recv.wait_recv()

    return _pallas(
        body, name=name,
        out_shape=pltpu.HBM(land_thru.shape, land_thru.dtype),
        in_specs=(_HBM, _SEM, _SEM, _ANY), out_specs=_HBM,
        input_output_aliases={0: 0},
        compiler_params=pltpu.CompilerParams(has_side_effects=_EFFECT),
    )(land_thru, send_sems, recv_sems, after)


def _exchange_start(src, mode, after, name):
    blk = src.shape[1:] if mode == "scatter" else src.shape
    land = lax.empty((NDEV,) + tuple(blk), src.dtype)

    def body(src_ref, land_ref, after_ref, send_sems, recv_sems, src_thru, land_thru, token):
        copies, own = _exchange_copies(src_ref, land_ref, send_sems, recv_sems, mode)
        for cp in copies:
            cp.start()
        own.start()
        token[...] = jnp.zeros_like(token)

    res = _pallas(
        body, name=name,
        out_shape=(pltpu.SemaphoreType.DMA((NDEV,)), pltpu.SemaphoreType.DMA((NDEV,)),
                   pltpu.HBM(src.shape, src.dtype), pltpu.HBM(land.shape, land.dtype),
                   jax.ShapeDtypeStruct((8, HEAD), F32)),
        in_specs=(_HBM, _HBM, _ANY), out_specs=(_SEM, _SEM, _HBM, _HBM, pl.BlockSpec(memory_space=pltpu.VMEM)),
        input_output_aliases={0: 2, 1: 3},
        compiler_params=pltpu.CompilerParams(has_side_effects=_EFFECT),
    )(pltpu.with_memory_space_constraint(src, pltpu.HBM), pltpu.with_memory_space_constraint(land, pltpu.HBM), after)
    return res[:4], res[4]


def _exchange_wait(handle, mode, after, name):
    send_sems, recv_sems, src_thru, land_thru = handle

    def body(src_ref, land_ref, send_sems, recv_sems, after_ref, src_dead, got_ref):
        copies, own = _exchange_copies(src_ref, land_ref, send_sems, recv_sems, mode)
        for cp in copies:
            cp.wait_send()
            cp.wait_recv()
        own.wait()

    return _pallas(
        body, name=name,
        out_shape=(pltpu.HBM(src_thru.shape, src_thru.dtype), pltpu.HBM(land_thru.shape, land_thru.dtype)),
        in_specs=(_HBM, _HBM, _SEM, _SEM, _ANY), out_specs=(_HBM, _HBM),
        input_output_aliases={0: 0, 1: 1},
        compiler_params=pltpu.CompilerParams(has_side_effects=_EFFECT),
    )(src_thru, land_thru, send_sems, recv_sems, after)[1]


def _ada_fwd(c_all, w, bias):
    r, d = c_all.shape
    e = w.shape[1]
    tn = 512

    def body(c_ref, w_ref, b_ref, o_ref):
        cv = c_ref[...]
        s = (cv * _sigmoid(cv)).astype(BF16)
        o_ref[...] = jnp.dot(s, w_ref[...].astype(BF16), preferred_element_type=F32) + b_ref[...]

    return _pallas(
        body, name="ada_fwd", grid=(e // tn,),
        out_shape=jax.ShapeDtypeStruct((r, e), F32),
        in_specs=[pl.BlockSpec((r, d), lambda j: (0, 0)), pl.BlockSpec((d, tn), lambda j: (0, j)),
                  pl.BlockSpec((1, tn), lambda j: (0, j))],
        out_specs=pl.BlockSpec((r, tn), lambda j: (0, j)),
        compiler_params=_params(_mb(24)),
    )(c_all, w, bias)


def _ada_bwd(dm16, c_all, w):
    d, e = w.shape
    tn = 512

    def body(dm_ref, c_ref, w_ref, dw_ref, dr_ref):
        j = pl.program_id(0)
        dm = dm_ref[...]
        rid = lax.broadcasted_iota(jnp.int32, dm.shape, 0)
        ctx_sum = jnp.sum(jnp.where(rid >= 8, dm, 0.0), axis=0, keepdims=True)
        rows = jnp.where(rid < 8, dm, jnp.where(rid == 8, jnp.broadcast_to(ctx_sum, dm.shape), 0.0)).astype(BF16)
        cv = c_ref[...]
        s = (cv * _sigmoid(cv)).astype(BF16)
        dw_ref[...] = lax.dot_general(s, rows, _TN, preferred_element_type=F32)
        part = lax.dot_general(rows, w_ref[...].astype(BF16), _NT, preferred_element_type=F32)

        @pl.when(j == 0)
        def _():
            dr_ref[...] = part

        @pl.when(j > 0)
        def _():
            dr_ref[...] += part

    return _pallas(
        body, name="ada_bwd", grid=(e // tn,),
        out_shape=(jax.ShapeDtypeStruct((d, e), F32), jax.ShapeDtypeStruct((16, d), F32)),
        in_specs=[pl.BlockSpec((16, tn), lambda j: (0, j)), pl.BlockSpec((16, d), lambda j: (0, 0)),
                  pl.BlockSpec((d, tn), lambda j: (0, j))],
        out_specs=(pl.BlockSpec((d, tn), lambda j: (0, j)), pl.BlockSpec((16, d), lambda j: (0, 0))),
        compiler_params=_params(_mb(32)),
    )(dm16, c_all, w)


def _rope(v, cos, sa, sb):
    return v * cos + (pltpu.roll(v, 96, 1) * sa + pltpu.roll(v, 32, 1) * sb)


def _rope_t(dt, cos, sa, sb):
    return dt * cos + (pltpu.roll(dt * sa, 32, 1) + pltpu.roll(dt * sb, 96, 1))


def _qkv_fwd(x, ct, sc, sh, wint, qg, kg, cos, sa, sb):
    n, d = x.shape
    tm = CTX
    nlat = n // tm
    na = n + CTX
    wcols = wint.shape[0]

    def body(x_ref, ct_ref, sc_ref, sh_ref, w_ref, qg_ref, kg_ref, cos_ref, sa_ref, sb_ref, u_ref, h_ref, t_ref, kt_ref):
        i = pl.program_id(0)
        xin = jnp.where(i == nlat, ct_ref[...], x_ref[...])
        u = (xin * (1.0 + sc_ref[0]) + sh_ref[0]).astype(BF16)
        u_ref[...] = u
        cos, sa, sb = cos_ref[...], sa_ref[...], sb_ref[...]
        h = lax.dot_general(u, w_ref[...], _NT, preferred_element_type=F32)
        h_ref[...] = h[:, NORM_HEAD0 * HEAD:(NORM_HEAD0 + NORM_HEADS) * HEAD]
        for hd in range(24):
            v = h[:, hd * HEAD:(hd + 1) * HEAD]
            kind = _KINDS[hd]
            if kind == "qnorm":
                v = v * lax.rsqrt(_rowmean(v * v) + EPS) * qg_ref[...]
            elif kind == "knorm":
                v = v * lax.rsqrt(_rowmean(v * v) + EPS) * kg_ref[...]
            if kind != "none":
                v = _rope(v, cos, sa, sb)
            t_ref[:, hd * HEAD:(hd + 1) * HEAD] = v.astype(BF16)
            if kind == "knorm":
                kt_ref[(hd - 20) * HEAD:(hd - 19) * HEAD, :] = v.T.astype(BF16)

    lat = lambda i: (jnp.minimum(i, nlat - 1), 0)
    row = lambda i: (i, 0)
    const2 = lambda i: (0, 0)
    return _pallas(
        body, name="qkv_fwd", grid=(nlat + 1,),
        out_shape=(jax.ShapeDtypeStruct((na, d), BF16), jax.ShapeDtypeStruct((na, NORM_HEADS * HEAD), F32),
                   jax.ShapeDtypeStruct((na, wcols), BF16), jax.ShapeDtypeStruct((2 * HEAD, na), BF16)),
        in_specs=[pl.BlockSpec((tm, d), lat), pl.BlockSpec((tm, d), const2),
                  pl.BlockSpec((1, 1, d), lambda i: (i // nlat, 0, 0)),
                  pl.BlockSpec((1, 1, d), lambda i: (i // nlat, 0, 0)),
                  pl.BlockSpec((wcols, d), const2),
                  pl.BlockSpec((1, HEAD), const2), pl.BlockSpec((1, HEAD), const2),
                  pl.BlockSpec((tm, HEAD), row), pl.BlockSpec((tm, HEAD), row), pl.BlockSpec((tm, HEAD), row)],
        out_specs=(pl.BlockSpec((tm, d), row), pl.BlockSpec((tm, NORM_HEADS * HEAD), row), pl.BlockSpec((tm, wcols), row),
                   pl.BlockSpec((2 * HEAD, tm), lambda i: (0, i))),
        compiler_params=_params(_mb(56)),
    )(x, ct, sc, sh, wint, qg, kg, cos, sa, sb)


def _qkv_bwd_prep(dqa, dka, dva, dqb, dkb, dvb, h_norm, qg, kg, cos, sa, sb):
    na = h_norm.shape[0]
    wcols = 24 * HEAD
    n = na - CTX
    tm = CTX
    nlat = n // tm

    def body(dqa_ref, dka_ref, dva_ref, dqb_ref, dkb_ref, dvb_ref, h_ref, qg_ref, kg_ref, cos_ref, sa_ref, sb_ref,
             dh_ref, dg_ref):
        i = pl.program_id(0)

        @pl.when(i == 0)
        def _():
            dg_ref[...] = jnp.zeros_like(dg_ref)

        cos, sa, sb = cos_ref[...], sa_ref[...], sb_ref[...]
        is_lat = i < nlat
        for hd in range(24):
            kind = _KINDS[hd]
            if hd < 8:
                dt = jnp.where(is_lat, dqa_ref[:, hd * HEAD:(hd + 1) * HEAD], 0.0)
            elif hd < 10:
                dt = dka_ref[:, (hd - 8) * HEAD:(hd - 7) * HEAD]
            elif hd < 12:
                dt = dva_ref[:, (hd - 10) * HEAD:(hd - 9) * HEAD]
            elif hd < 20:
                dt = jnp.where(is_lat, dqb_ref[:, (hd - 12) * HEAD:(hd - 11) * HEAD], 0.0)
            elif hd < 22:
                dt = dkb_ref[:, (hd - 20) * HEAD:(hd - 19) * HEAD]
            else:
                dt = dvb_ref[:, (hd - 22) * HEAD:(hd - 21) * HEAD]
            if kind != "none":
                dt = _rope_t(dt, cos, sa, sb)
            if kind in ("qnorm", "knorm"):
                g_ref = qg_ref if kind == "qnorm" else kg_ref
                r0 = 0 if kind == "qnorm" else 1
                xv = h_ref[:, (hd - NORM_HEAD0) * HEAD:(hd - NORM_HEAD0 + 1) * HEAD]
                xn = xv * lax.rsqrt(_rowmean(xv * xv) + EPS)
                dg_ref[r0:r0 + 1, :] += _colsum(dt * xn)
                dxn = dt * g_ref[...]
                dt = lax.rsqrt(_rowmean(xv * xv) + EPS) * (dxn - xn * _rowmean(dxn * xn))
            dh_ref[:, hd * HEAD:(hd + 1) * HEAD] = dt.astype(BF16)

    lat = lambda i: (jnp.minimum(i, nlat - 1), 0)
    row = lambda i: (i, 0)
    const2 = lambda i: (0, 0)
    return _pallas(
        body, name="qkv_bwd_prep", grid=(nlat + 1,),
        out_shape=(jax.ShapeDtypeStruct((na, wcols), BF16), jax.ShapeDtypeStruct((8, HEAD), F32)),
        in_specs=[pl.BlockSpec((tm, 8 * HEAD), lat), pl.BlockSpec((tm, 2 * HEAD), row), pl.BlockSpec((tm, 2 * HEAD), row),
                  pl.BlockSpec((tm, 8 * HEAD), lat), pl.BlockSpec((tm, 2 * HEAD), row), pl.BlockSpec((tm, 2 * HEAD), row),
                  pl.BlockSpec((tm, NORM_HEADS * HEAD), row),
                  pl.BlockSpec((1, HEAD), const2), pl.BlockSpec((1, HEAD), const2),
                  pl.BlockSpec((tm, HEAD), row), pl.BlockSpec((tm, HEAD), row), pl.BlockSpec((tm, HEAD), row)],
        out_specs=(pl.BlockSpec((tm, wcols), row), pl.BlockSpec((8, HEAD), const2)),
        compiler_params=_params(_mb(40)),
    )(dqa, dka, dva, dqb, dkb, dvb, h_norm, qg, kg, cos, sa, sb)


def _window_keys(k_ref, v_ref, n, na):
    i = pl.program_id(1)
    tq = WINDOW
    start = pl.multiple_of(jnp.clip((i - 1) * tq, 0, n - 3 * tq), tq)
    kk = jnp.concatenate([k_ref[pl.ds(start, 3 * tq), :], k_ref[n:na, :]], axis=0)
    vv = jnp.concatenate([v_ref[pl.ds(start, 3 * tq), :], v_ref[n:na, :]], axis=0)
    return kk, vv, start


def _window_bias():
    tq = WINDOW
    r = (jnp.arange(4 * tq) % tq)[:, None]
    c = jnp.arange(3 * tq + CTX)[None, :]
    variants = []
    for back in (0, tq, 2 * tq):
        seen = (jnp.abs(back + r - c) <= WINDOW) | (c >= 3 * tq)
        variants.append(jnp.where(seen, 0.0, NEG).astype(F32))
    return jnp.stack(variants)


def _window_bias_spec(nq):
    return pl.BlockSpec((1, 4 * WINDOW, 3 * WINDOW + CTX),
                        lambda kv, i: (jnp.where(i == 0, 0, jnp.where(i == nq - 1, 2, 1)), 0, 0))


def _stack_heads(ref, width=HEAD):
    return jnp.concatenate([ref[:, g * HEAD:g * HEAD + width] for g in range(4)], axis=0)


def _sink_column(sink_ref, kv, tq):
    grp = lax.broadcasted_iota(jnp.int32, (4 * tq, 1), 0) // tq
    col = jnp.zeros((4 * tq, 1), F32)
    for g in range(4):
        col = jnp.where(grp == g, sink_ref[0, 4 * kv + g] * LOG2E, col)
    return col


def _attn_window_fwd(t_all, sink, bias, after):
    na = t_all.shape[0]
    n = na - CTX
    tq = WINDOW

    def body(sink_ref, q_ref, k_ref, v_ref, bias_ref, after_ref, o_ref, p_ref, linv_ref):
        kv = pl.program_id(0)
        kk, vv, _ = _window_keys(k_ref, v_ref, n, na)
        t = lax.dot_general(_stack_heads(q_ref), kk, _NT, preferred_element_type=F32) * QK_LOG2 + bias_ref[0]
        sk = _sink_column(sink_ref, kv, tq)
        m = jnp.maximum(jnp.max(t, axis=-1, keepdims=True), sk)
        p = jnp.exp2(t - m)
        p_sink = jnp.exp2(sk - m)
        linv = 1.0 / (jnp.sum(p, axis=-1, keepdims=True) + p_sink)
        pb = p.astype(BF16)
        o = jnp.dot(pb, vv, preferred_element_type=F32) * linv
        p_all = jnp.concatenate([pb, jnp.broadcast_to(p_sink, (4 * tq, WIN_P - WIN_KEYS)).astype(BF16)], axis=1)
        for g in range(4):
            o_ref[:, g * HEAD:(g + 1) * HEAD] = o[g * tq:(g + 1) * tq]
            p_ref[g] = p_all[g * tq:(g + 1) * tq]
            linv_ref[:, g * HEAD:(g + 1) * HEAD] = jnp.broadcast_to(linv[g * tq:(g + 1) * tq], (tq, HEAD))

    blk = pl.BlockSpec((tq, 4 * HEAD), lambda kv, i: (i, kv))
    return _pallas(
        body, name="attn_window_fwd", grid=(2, n // tq),
        out_shape=(jax.ShapeDtypeStruct((n, 16 * HEAD), F32), jax.ShapeDtypeStruct((8, n, WIN_P), BF16),
                   jax.ShapeDtypeStruct((n, 8 * HEAD), F32)),
        in_specs=[pl.BlockSpec(memory_space=pltpu.SMEM), blk,
                  pl.BlockSpec((na, HEAD), lambda kv, i: (0, 8 + kv)),
                  pl.BlockSpec((na, HEAD), lambda kv, i: (0, 10 + kv)), _window_bias_spec(n // tq), _ANY],
        out_specs=(blk, pl.BlockSpec((4, tq, WIN_P), lambda kv, i: (kv, i, 0)), blk),
        compiler_params=_params(_mb(32)),
    )(sink, t_all, t_all, t_all, bias, after)


def _attn_global_fwd(t_all, o_part):
    na = t_all.shape[0]
    n = na - CTX
    tq = 256

    def body(q_ref, k_ref, v_ref, o_in_ref, o_ref, p_ref, linv_ref):
        kk, vv = k_ref[...], v_ref[...]
        for g in range(4):
            q = q_ref[:, g * HEAD:(g + 1) * HEAD]
            t = lax.dot_general(q, kk, _NT, preferred_element_type=F32) * QK_LOG2
            m = jnp.max(t, axis=-1, keepdims=True)
            p = jnp.exp2(t - m)
            linv = 1.0 / jnp.sum(p, axis=-1, keepdims=True)
            pb = p.astype(BF16)
            p_ref[g] = pb
            o_ref[:, g * HEAD:(g + 1) * HEAD] = jnp.dot(pb, vv, preferred_element_type=F32) * linv
            linv_ref[:, g * HEAD:(g + 1) * HEAD] = jnp.broadcast_to(linv, (tq, HEAD))

    return _pallas(
        body, name="attn_global_fwd", grid=(2, n // tq),
        out_shape=(jax.ShapeDtypeStruct((n, 16 * HEAD), F32), jax.ShapeDtypeStruct((8, n, na), BF16),
                   jax.ShapeDtypeStruct((n, 8 * HEAD), F32)),
        in_specs=[pl.BlockSpec((tq, 4 * HEAD), lambda kv, i: (i, 3 + kv)),
                  pl.BlockSpec((na, HEAD), lambda kv, i: (0, 20 + kv)),
                  pl.BlockSpec((na, HEAD), lambda kv, i: (0, 22 + kv)), _ANY],
        out_specs=(pl.BlockSpec((tq, 4 * HEAD), lambda kv, i: (i, 2 + kv)),
                   pl.BlockSpec((4, tq, na), lambda kv, i: (kv, i, 0)),
                   pl.BlockSpec((tq, 4 * HEAD), lambda kv, i: (i, kv))),
        input_output_aliases={3: 0},
        compiler_params=_params(_mb(56)),
    )(t_all, t_all, t_all, o_part)


def _attn_window_bwd(t_all, o, do, p_all, linv):
    na = t_all.shape[0]
    n = na - CTX
    tq = WINDOW

    def body(q_ref, k_ref, v_ref, o_ref, do_ref, p_ref, linv_ref, dq_ref, dk_ref, dv_ref, dsink_ref):
        @pl.when(pl.program_id(1) == 0)
        def _():
            dk_ref[...] = jnp.zeros_like(dk_ref)
            dv_ref[...] = jnp.zeros_like(dv_ref)
            dsink_ref[...] = jnp.zeros_like(dsink_ref)

        kk, vv, start = _window_keys(k_ref, v_ref, n, na)
        q = _stack_heads(q_ref)
        p_full = jnp.concatenate([p_ref[g] for g in range(4)], axis=0).astype(F32) * _stack_heads(linv_ref, 1)
        p = p_full[:, :WIN_KEYS]
        dof = _stack_heads(do_ref)
        delta = jnp.sum(dof * _stack_heads(o_ref), axis=-1, keepdims=True)
        dob = dof.astype(BF16)
        dv_acc = lax.dot_general(p.astype(BF16), dob, _TN, preferred_element_type=F32)
        dp = lax.dot_general(dob, vv, _NT, preferred_element_type=F32)
        ds = (p * (dp - delta) * SCALE).astype(BF16)
        dq = jnp.dot(ds, kk, preferred_element_type=F32)
        dk_acc = lax.dot_general(ds, q, _TN, preferred_element_type=F32)
        dsk = -(p_full[:, WIN_KEYS:WIN_KEYS + 1] * delta)
        for g in range(4):
            dq_ref[:, g * HEAD:(g + 1) * HEAD] = dq[g * tq:(g + 1) * tq]
            dsink_ref[0, g:g + 1, :] += jnp.broadcast_to(_colsum(dsk[g * tq:(g + 1) * tq]), (1, HEAD))
        dk_ref[pl.ds(start, 3 * tq), :] += dk_acc[:3 * tq]
        dv_ref[pl.ds(start, 3 * tq), :] += dv_acc[:3 * tq]
        dk_ref[n:na, :] += dk_acc[3 * tq:]
        dv_ref[n:na, :] += dv_acc[3 * tq:]

    blk = pl.BlockSpec((tq, 4 * HEAD), lambda kv, i: (i, kv))
    kvout = pl.BlockSpec((na, HEAD), lambda kv, i: (0, kv))
    return _pallas(
        body, name="attn_window_bwd", grid=(2, n // tq),
        out_shape=(jax.ShapeDtypeStruct((n, 8 * HEAD), F32), jax.ShapeDtypeStruct((na, 2 * HEAD), F32),
                   jax.ShapeDtypeStruct((na, 2 * HEAD), F32), jax.ShapeDtypeStruct((2, 8, HEAD), F32)),
        in_specs=[blk,
                  pl.BlockSpec((na, HEAD), lambda kv, i: (0, 8 + kv)),
                  pl.BlockSpec((na, HEAD), lambda kv, i: (0, 10 + kv)),
                  blk, blk, pl.BlockSpec((4, tq, WIN_P), lambda kv, i: (kv, i, 0)), blk],
        out_specs=(blk, kvout, kvout, pl.BlockSpec((1, 8, HEAD), lambda kv, i: (kv, 0, 0))),
        compiler_params=_params(_mb(40)),
    )(t_all, t_all, t_all, o, do, p_all, linv)


def _attn_global_bwd(t_all, kt, o, do, p_all, linv):
    na = t_all.shape[0]
    n = na - CTX
    tq = 256

    def body(q_ref, v_ref, kt_ref, o_ref, do_ref, p_ref, linv_ref, dq_ref, dk_ref, dv_ref, dkt_acc, dvt_acc):
        i = pl.program_id(1)

        @pl.when(i == 0)
        def _():
            dkt_acc[...] = jnp.zeros_like(dkt_acc)
            dvt_acc[...] = jnp.zeros_like(dvt_acc)

        vv, kt_v = v_ref[...], kt_ref[...]
        dkt = jnp.zeros((HEAD, na), F32)
        dvt = jnp.zeros((HEAD, na), F32)
        def probs(g):
            return p_ref[g].astype(F32) * linv_ref[:, g * HEAD:g * HEAD + 1]

        def dprobs(g):
            dob = do_ref[:, g * HEAD:(g + 1) * HEAD].astype(BF16)
            return dob, lax.dot_general(dob, vv, _NT, preferred_element_type=F32)

        nxt = dprobs(0)
        for g in range(4):
            q = q_ref[:, g * HEAD:(g + 1) * HEAD]
            p = probs(g)
            dob, dp = nxt
            if g < 3:
                nxt = dprobs(g + 1)
            delta = jnp.sum(do_ref[:, g * HEAD:(g + 1) * HEAD] * o_ref[:, g * HEAD:(g + 1) * HEAD], axis=-1,
                            keepdims=True)
            dvt = dvt + lax.dot_general(dob, p.astype(BF16), _TN, preferred_element_type=F32)
            ds = (p * (dp - delta) * SCALE).astype(BF16)
            dq_ref[:, g * HEAD:(g + 1) * HEAD] = lax.dot_general(kt_v, ds, _NT, preferred_element_type=F32).T
            dkt = dkt + lax.dot_general(q, ds, _TN, preferred_element_type=F32)
        dkt_acc[...] += dkt
        dvt_acc[...] += dvt

        @pl.when(i == pl.num_programs(1) - 1)
        def _():
            dk_ref[...] = dkt_acc[...].T
            dv_ref[...] = dvt_acc[...].T

    ospec = pl.BlockSpec((tq, 4 * HEAD), lambda kv, i: (i, 2 + kv))
    lspec = pl.BlockSpec((tq, 4 * HEAD), lambda kv, i: (i, kv))
    kvout = pl.BlockSpec((na, HEAD), lambda kv, i: (0, kv))
    return _pallas(
        body, name="attn_global_bwd", grid=(2, n // tq),
        out_shape=(jax.ShapeDtypeStruct((n, 8 * HEAD), F32), jax.ShapeDtypeStruct((na, 2 * HEAD), F32),
                   jax.ShapeDtypeStruct((na, 2 * HEAD), F32)),
        in_specs=[pl.BlockSpec((tq, 4 * HEAD), lambda kv, i: (i, 3 + kv)),
                  pl.BlockSpec((na, HEAD), lambda kv, i: (0, 22 + kv)),
                  pl.BlockSpec((HEAD, na), lambda kv, i: (kv, 0)),
                  ospec, ospec, pl.BlockSpec((4, tq, na), lambda kv, i: (kv, i, 0)), lspec],
        out_specs=(lspec, kvout, kvout),
        scratch_shapes=[pltpu.VMEM((HEAD, na), F32), pltpu.VMEM((HEAD, na), F32)],
        compiler_params=_params(_mb(56)),
    )(t_all, t_all, kt, o, do, p_all, linv)


def _outproj_ln1(o, wout, x, g1, lg, lb, sc2, sh2, after):
    n, d = x.shape
    tm = 256

    def body(o_ref, w_ref, x_ref, g1_ref, lg_ref, lb_ref, sc_ref, sh_ref, after_ref, a_ref, xh_ref, rs_ref, u_ref):
        a1 = jnp.dot(o_ref[...].astype(BF16), w_ref[...], preferred_element_type=F32)
        a_ref[...] = a1.astype(BF16)
        r = ALPHA * x_ref[...] + g1_ref[...] * a1
        dlt = r - _rowmean(r)
        rstd = lax.rsqrt(_rowmean(dlt * dlt) + EPS)
        xh = dlt * rstd
        xh_ref[...] = xh
        rs_ref[...] = rstd
        x1 = xh * lg_ref[...] + lb_ref[...]
        u_ref[...] = (x1 * (1.0 + sc_ref[...]) + sh_ref[...]).astype(BF16)

    row = lambda i: (i, 0)
    const2 = lambda i: (0, 0)
    vec = pl.BlockSpec((1, d), const2)
    big = pl.BlockSpec((tm, d), row)
    return _pallas(
        body, name="outproj_ln1", grid=(n // tm,),
        out_shape=(jax.ShapeDtypeStruct((n, d), BF16), jax.ShapeDtypeStruct((n, d), F32),
                   jax.ShapeDtypeStruct((n, 1), F32), jax.ShapeDtypeStruct((n, d), BF16)),
        in_specs=[big, pl.BlockSpec((d, d), const2), big, vec, vec, vec, vec, vec, _ANY],
        out_specs=(big, big, pl.BlockSpec((tm, 1), row), big),
        compiler_params=_params(_mb(56)),
    )(o, wout, x, g1, lg, lb, sc2, sh2, after)


def _ffn_up(u2, wgt, wut, after):
    n, d = u2.shape
    f = wgt.shape[0]
    tm = min(1024, n)

    def body(u_ref, wg_ref, wu_ref, after_ref, sa_ref, sb_ref, hf_ref):
        u = u_ref[...]
        gv = lax.dot_general(u, wg_ref[...], _NT, preferred_element_type=F32)
        pv = lax.dot_general(u, wu_ref[...], _NT, preferred_element_type=F32)
        sg = _sigmoid(gv)
        silu = gv * sg
        sa_ref[...] = silu.astype(BF16)
        sb_ref[...] = (pv * (sg * (1.0 + gv * (1.0 - sg)))).astype(BF16)
        hf_ref[...] = (silu * pv).astype(BF16)

    tile = pl.BlockSpec((tm, FFN_TILE), lambda i, j: (i, j))
    wspec = pl.BlockSpec((FFN_TILE, d), lambda i, j: (j, 0))
    sds = jax.ShapeDtypeStruct((n, f), BF16)
    return _pallas(
        body, name="ffn_up", grid=(n // tm, f // FFN_TILE),
        out_shape=(sds, sds, sds),
        in_specs=[pl.BlockSpec((tm, d), lambda i, j: (i, 0)), wspec, wspec, _ANY],
        out_specs=(tile, tile, tile),
        compiler_params=_params(_mb(48)),
    )(u2, wgt, wut, after)


def _ffn_down(hf, wd):
    n, f = hf.shape
    d = wd.shape[1]
    tm, tn = min(1024, n), 512

    def body(h_ref, w_ref, o_ref):
        o_ref[...] = jnp.dot(h_ref[...], w_ref[...], preferred_element_type=F32)

    return _pallas(
        body, name="ffn_down", grid=(n // tm, d // tn),
        out_shape=jax.ShapeDtypeStruct((n, d), F32),
        in_specs=[pl.BlockSpec((tm, f), lambda i, j: (i, 0)), pl.BlockSpec((f, tn), lambda i, j: (0, j))],
        out_specs=pl.BlockSpec((tm, tn), lambda i, j: (i, j)),
        compiler_params=_params(_mb(56)),
    )(hf, wd)


def _ln2_loss(xh1, ffn, tgt, lg1, lb1, g2, lg2, lb2):
    n, d = xh1.shape
    tm = 256

    def body(xh_ref, f_ref, t_ref, lg1_ref, lb1_ref, g2_ref, lg2_ref, lb2_ref, dr_ref, df_ref, loss_ref, acc_ref):
        @pl.when(pl.program_id(0) == 0)
        def _():
            loss_ref[...] = jnp.zeros_like(loss_ref)
            acc_ref[...] = jnp.zeros_like(acc_ref)

        x1 = xh_ref[...] * lg1_ref[...] + lb1_ref[...]
        fv = f_ref[...]
        r = ALPHA * x1 + g2_ref[...] * fv
        dlt = r - _rowmean(r)
        rstd = lax.rsqrt(_rowmean(dlt * dlt) + EPS)
        xh2 = dlt * rstd
        err = xh2 * lg2_ref[...] + lb2_ref[...] - t_ref[...]
        loss_ref[...] += 0.5 * jnp.sum(_rowmean(err * err))
        dy = err * (1.0 / d)
        dyg = dy * lg2_ref[...]
        dr = rstd * (dyg - _rowmean(dyg) - xh2 * _rowmean(dyg * xh2))
        dr_ref[...] = dr
        df_ref[...] = (g2_ref[...] * dr).astype(BF16)
        acc_ref[0:1, :] += _colsum(dy * xh2)
        acc_ref[1:2, :] += _colsum(dy)
        acc_ref[2:3, :] += _colsum(dr * fv)

    row = lambda i: (i, 0)
    const2 = lambda i: (0, 0)
    vec = pl.BlockSpec((1, d), const2)
    big = pl.BlockSpec((tm, d), row)
    return _pallas(
        body, name="ln2_loss", grid=(n // tm,),
        out_shape=(jax.ShapeDtypeStruct((n, d), F32), jax.ShapeDtypeStruct((n, d), BF16),
                   jax.ShapeDtypeStruct((8, HEAD), F32), jax.ShapeDtypeStruct((8, d), F32)),
        in_specs=[big, big, big, vec, vec, vec, vec, vec],
        out_specs=(big, big, pl.BlockSpec((8, HEAD), const2), pl.BlockSpec((8, d), const2)),
        compiler_params=_params(_mb(48)),
    )(xh1, ffn, tgt, lg1, lb1, g2, lg2, lb2)


def _ffn_dhf(df, wd, sa, sb):
    n, d = df.shape
    f = sa.shape[1]
    tm = min(2048, n)

    def body(df_ref, w_ref, sa_ref, sb_ref, dgp_ref):
        dhf = lax.dot_general(df_ref[...], w_ref[...], _NT, preferred_element_type=F32)
        dgp_ref[:, :FFN_TILE] = (dhf * sb_ref[...].astype(F32)).astype(BF16)
        dgp_ref[:, FFN_TILE:] = (dhf * sa_ref[...].astype(F32)).astype(BF16)

    tile = pl.BlockSpec((tm, FFN_TILE), lambda i, j: (i, j))
    return _pallas(
        body, name="ffn_dhf", grid=(n // tm, f // FFN_TILE),
        out_shape=jax.ShapeDtypeStruct((n, 2 * f), BF16),
        in_specs=[pl.BlockSpec((tm, d), lambda i, j: (i, 0)), pl.BlockSpec((FFN_TILE, d), lambda i, j: (j, 0)),
                  tile, tile],
        out_specs=pl.BlockSpec((tm, 2 * FFN_TILE), lambda i, j: (i, j)),
        compiler_params=_params(_mb(48)),
    )(df, wd, sa, sb)


def _ffn_du2(dgp, wgt, wut, after):
    n = dgp.shape[0]
    f, d = wgt.shape
    tm = min(1024, n)

    def body(dgp_ref, wg_ref, wu_ref, after_ref, o_ref):
        w = jnp.concatenate([wg_ref[...], wu_ref[...]], axis=0)
        part = jnp.dot(dgp_ref[...], w, preferred_element_type=F32)

        @pl.when(pl.program_id(1) == 0)
        def _():
            o_ref[...] = part

        @pl.when(pl.program_id(1) > 0)
        def _():
            o_ref[...] += part

    wspec = pl.BlockSpec((FFN_TILE, d), lambda i, j: (j, 0))
    return _pallas(
        body, name="ffn_du2", grid=(n // tm, f // FFN_TILE),
        out_shape=jax.ShapeDtypeStruct((n, d), F32),
        in_specs=[pl.BlockSpec((tm, 2 * FFN_TILE), lambda i, j: (i, j)), wspec, wspec, _ANY],
        out_specs=pl.BlockSpec((tm, d), lambda i, j: (i, 0)),
        compiler_params=_params(_mb(48)),
    )(dgp, wgt, wut, after)


def _dw_gate_up(dgp, u2, after):
    n, d = u2.shape
    f = dgp.shape[1] // 2
    tm = min(2048, n)

    def body(a_ref, b_ref, after_ref, og_ref, ou_ref, acc_ref):
        part = lax.dot_general(a_ref[...], b_ref[...], _TN, preferred_element_type=F32)
        i = pl.program_id(1)

        @pl.when(i == 0)
        def _():
            acc_ref[...] = part

        @pl.when(i > 0)
        def _():
            acc_ref[...] += part

        @pl.when(i == pl.num_programs(1) - 1)
        def _():
            og_ref[...] = acc_ref[:FFN_TILE].astype(BF16)
            ou_ref[...] = acc_ref[FFN_TILE:].astype(BF16)

    out = pl.BlockSpec((FFN_TILE, d), lambda j, i: (j, 0))
    sds = jax.ShapeDtypeStruct((f, d), BF16)
    return _pallas(
        body, name="dw_gate_up", grid=(f // FFN_TILE, n // tm),
        out_shape=(sds, sds),
        in_specs=[pl.BlockSpec((tm, 2 * FFN_TILE), lambda j, i: (i, j)), pl.BlockSpec((tm, d), lambda j, i: (i, 0)), _ANY],
        out_specs=(out, out),
        scratch_shapes=[pltpu.VMEM((2 * FFN_TILE, d), F32)],
        compiler_params=_params(_mb(56)),
    )(dgp, u2, after)


def _ln1_bwd(du2, dr2, xh1, rs1, a1, lg1, lb1, sc2, g1):
    n, d = du2.shape
    tm = 256

    def body(du_ref, dr2_ref, xh_ref, rs_ref, a_ref, lg_ref, lb_ref, sc_ref, g1_ref, dr1_ref, da_ref, acc_ref):
        @pl.when(pl.program_id(0) == 0)
        def _():
            acc_ref[...] = jnp.zeros_like(acc_ref)

        du = du_ref[...]
        xh = xh_ref[...]
        x1 = xh * lg_ref[...] + lb_ref[...]
        dx1 = ALPHA * dr2_ref[...] + du * (1.0 + sc_ref[...])
        dxg = dx1 * lg_ref[...]
        dr1 = rs_ref[...] * (dxg - _rowmean(dxg) - xh * _rowmean(dxg * xh))
        dr1_ref[...] = dr1
        da_ref[...] = (g1_ref[...] * dr1).astype(BF16)
        acc_ref[0:1, :] += _colsum(du * x1)
        acc_ref[1:2, :] += _colsum(du)
        acc_ref[2:3, :] += _colsum(dx1 * xh)
        acc_ref[3:4, :] += _colsum(dx1)
        acc_ref[4:5, :] += _colsum(dr1 * a_ref[...].astype(F32))

    row = lambda i: (i, 0)
    const2 = lambda i: (0, 0)
    vec = pl.BlockSpec((1, d), const2)
    big = pl.BlockSpec((tm, d), row)
    return _pallas(
        body, name="ln1_bwd", grid=(n // tm,),
        out_shape=(jax.ShapeDtypeStruct((n, d), F32), jax.ShapeDtypeStruct((n, d), BF16),
                   jax.ShapeDtypeStruct((8, d), F32)),
        in_specs=[big, big, big, pl.BlockSpec((tm, 1), row), big, vec, vec, vec, vec],
        out_specs=(big, big, pl.BlockSpec((8, d), const2)),
        compiler_params=_params(_mb(48)),
    )(du2, dr2, xh1, rs1, a1, lg1, lb1, sc2, g1)


def _dw_rows(a, b, nblk, bw, tm, after, name):
    m = a.shape[0]
    nn = b.shape[1]

    def body(a_ref, b_ref, after_ref, o_ref, acc_ref):
        part = lax.dot_general(a_ref[...].astype(BF16), b_ref[...], _TN, preferred_element_type=F32)
        i = pl.program_id(1)

        @pl.when(i == 0)
        def _():
            acc_ref[...] = part

        @pl.when(i > 0)
        def _():
            acc_ref[...] += part

        @pl.when(i == pl.num_programs(1) - 1)
        def _():
            o_ref[0] = acc_ref[...].astype(BF16)

    return _pallas(
        body, name=name, grid=(nblk, m // tm),
        out_shape=jax.ShapeDtypeStruct((nblk, bw, nn), BF16),
        in_specs=[pl.BlockSpec((tm, bw), lambda j, i: (i, j)), pl.BlockSpec((tm, nn), lambda j, i: (i, 0)), _ANY],
        out_specs=pl.BlockSpec((1, bw, nn), lambda j, i: (j, 0, 0)),
        scratch_shapes=[pltpu.VMEM((bw, nn), F32)],
        compiler_params=_params(_mb(56)),
    )(a, b, after)


def _outproj_bwd(da1, wout, after):
    n, d = da1.shape
    tm = 512

    def body(a_ref, w_ref, after_ref, o_ref):
        o_ref[...] = lax.dot_general(a_ref[...], w_ref[...], _NT, preferred_element_type=F32)

    return _pallas(
        body, name="outproj_bwd", grid=(n // tm,),
        out_shape=jax.ShapeDtypeStruct((n, d), F32),
        in_specs=[pl.BlockSpec((tm, d), lambda i: (i, 0)), pl.BlockSpec((d, d), lambda i: (0, 0)), _ANY],
        out_specs=pl.BlockSpec((tm, d), lambda i: (i, 0)),
        compiler_params=_params(_mb(48)),
    )(da1, wout, after)


def _qkv_bwd(dh, wint, x, ct, dr1, sc):
    na, wcols = dh.shape
    n, d = x.shape
    tm = CTX
    nlat = n // tm

    def body(dh_ref, w_ref, x_ref, ct_ref, dr_ref, sc_ref, gx_ref, acc_ref):
        i = pl.program_id(0)

        @pl.when(i == 0)
        def _():
            acc_ref[...] = jnp.zeros_like(acc_ref)

        du = jnp.dot(dh_ref[...], w_ref[...], preferred_element_type=F32)

        @pl.when(i < nlat)
        def _():
            gx_ref[...] = ALPHA * dr_ref[...] + du * (1.0 + sc_ref[0])
            acc_ref[0:1, :] += _colsum(du)
            acc_ref[1:2, :] += _colsum(du * x_ref[...])

        @pl.when(i == nlat)
        def _():
            acc_ref[2:3, :] += _colsum(du)
            acc_ref[3:4, :] += _colsum(du * ct_ref[...])

    lat = lambda i: (jnp.minimum(i, nlat - 1), 0)
    const2 = lambda i: (0, 0)
    return _pallas(
        body, name="qkv_bwd", grid=(nlat + 1,),
        out_shape=(jax.ShapeDtypeStruct((n, d), F32), jax.ShapeDtypeStruct((8, d), F32)),
        in_specs=[pl.BlockSpec((tm, wcols), lambda i: (i, 0)), pl.BlockSpec((wcols, d), const2),
                  pl.BlockSpec((tm, d), lat), pl.BlockSpec((tm, d), const2), pl.BlockSpec((tm, d), lat),
                  pl.BlockSpec((1, 1, d), lambda i: (0, 0, 0))],
        out_specs=(pl.BlockSpec((tm, d), lat), pl.BlockSpec((8, d), const2)),
        compiler_params=_params(_mb(56)),
    )(dh, wint, x, ct, dr1, sc)


def _adam_math(w, g, m, v):
    m2 = ADAM_B1 * m + (1.0 - ADAM_B1) * g
    v2 = ADAM_B2 * v + (1.0 - ADAM_B2) * (g * g)
    m_hat = m2 * (1.0 / (1.0 - ADAM_B1 ** ADAM_STEP))
    v_hat = v2 * (1.0 / (1.0 - ADAM_B2 ** ADAM_STEP))
    delta = -ADAM_LR * (m_hat / (jnp.sqrt(v_hat) + ADAM_EPS) + ADAM_WD * w)
    return delta, m2, v2


def _adamw(w, gsrc, m, v, name, after=None):
    r, c = w.shape
    parts = gsrc.ndim == 3
    after = w if after is None else after
    tr = r
    while tr * c * 4 > _mb(1) and tr % 32 == 0:
        tr //= 2

    def body(w_ref, g_ref, m_ref, v_ref, after_ref, go_ref, d_ref, mo_ref, vo_ref):
        if parts:
            g = g_ref[0].astype(F32)
            for s in range(1, NDEV):
                g = g + g_ref[s].astype(F32)
        else:
            g = g_ref[...]
        delta, m2, v2 = _adam_math(w_ref[...], g, m_ref[...], v_ref[...])
        go_ref[...] = g
        d_ref[...] = delta
        mo_ref[...] = m2
        vo_ref[...] = v2

    tile = pl.BlockSpec((tr, c), lambda i: (i, 0))
    gspec = pl.BlockSpec((NDEV, tr, c), lambda i: (0, i, 0)) if parts else tile
    sds = jax.ShapeDtypeStruct((r, c), F32)
    return _pallas(
        body, name=name, grid=(r // tr,),
        out_shape=(sds, sds, sds, sds),
        in_specs=[tile, gspec, tile, tile, _ANY],
        out_specs=(tile, tile, tile, tile),
        compiler_params=_params(_mb(48)),
    )(w, gsrc, m, v, after)


def _adamw_t(w, gsrc_t, m, v, name):
    r, c = w.shape
    tr = 256

    def body(w_ref, g_ref, m_ref, v_ref, go_ref, d_ref, mo_ref, vo_ref):
        gt = g_ref[0].astype(F32)
        for s in range(1, NDEV):
            gt = gt + g_ref[s].astype(F32)
        g = gt.T
        delta, m2, v2 = _adam_math(w_ref[...], g, m_ref[...], v_ref[...])
        go_ref[...] = g
        d_ref[...] = delta
        mo_ref[...] = m2
        vo_ref[...] = v2

    tile = pl.BlockSpec((tr, c), lambda i: (i, 0))
    sds = jax.ShapeDtypeStruct((r, c), F32)
    return _pallas(
        body, name=name, grid=(r // tr,),
        out_shape=(sds, sds, sds, sds),
        in_specs=[tile, pl.BlockSpec((NDEV, c, tr), lambda i: (0, 0, i)), tile, tile],
        out_specs=(tile, tile, tile, tile),
        compiler_params=_params(_mb(48)),
    )(w, gsrc_t, m, v)


def _small_update(gath, dcc, cc, w_s, m_s, v_s):
    d = w_s.shape[1]

    def body(g_ref, dcc_ref, cc_ref, w_ref, m_ref, v_ref, go_ref, d_ref, mo_ref, vo_ref):
        s = g_ref[0]
        for b in range(1, NDEV):
            s = s + g_ref[b]
        dsl = dcc_ref[0, 8:9, :]
        for b in range(1, NDEV):
            dsl = dsl + dcc_ref[b, 8:9, :]
        cv = cc_ref[...]
        sg = _sigmoid(cv)
        go_ref[...] = jnp.zeros_like(go_ref)
        go_ref[0:1, :] = dsl * (sg * (1.0 + cv * (1.0 - sg)))
        go_ref[1:3, :] = s[0:2] + s[6:8]
        go_ref[3:7, :] = s[2:6]
        go_ref[7:12, :] = s[8:13]
        delta, m2, v2 = _adam_math(w_ref[...], go_ref[...], m_ref[...], v_ref[...])
        d_ref[...] = delta
        mo_ref[...] = m2
        vo_ref[...] = v2

    full = pl.BlockSpec((16, d), lambda: (0, 0))
    g3 = pl.BlockSpec((NDEV, 16, d), lambda: (0, 0, 0))
    sds = jax.ShapeDtypeStruct((16, d), F32)
    return _pallas(
        body, name="small_update",
        out_shape=(sds, sds, sds, sds),
        in_specs=[g3, g3, pl.BlockSpec((1, d), lambda: (0, 0)), full, full, full],
        out_specs=(full, full, full, full),
        compiler_params=_params(_mb(24)),
    )(gath, dcc, cc, w_s, m_s, v_s)


def _rope_tables(n):
    rows = n // GRID_W
    row_ids = jnp.repeat(jnp.arange(rows, dtype=F32), GRID_W)
    col_ids = jnp.tile(jnp.arange(GRID_W, dtype=F32), rows)
    axis_dim = HEAD // 2
    inv_freq = jnp.power(ROPE_THETA, -jnp.arange(0, axis_dim, 2, dtype=F32) / axis_dim)
    ang_r = row_ids[:, None] * inv_freq
    ang_c = col_ids[:, None] * inv_freq
    ang = jnp.concatenate([ang_r, ang_r, ang_c, ang_c], axis=-1)
    cos, sin = jnp.cos(ang), jnp.sin(ang)
    first = (jnp.arange(HEAD) % (HEAD // 2)) < HEAD // 4
    sa = jnp.where(first, -sin, 0.0)
    sb = jnp.where(first, 0.0, sin)
    ones = jnp.ones((CTX, HEAD), F32)
    zeros = jnp.zeros((CTX, HEAD), F32)
    return (jnp.concatenate([cos, ones], 0), jnp.concatenate([sa, zeros], 0), jnp.concatenate([sb, zeros], 0))


def _pad_cols(a, width):
    return jnp.pad(a, ((0, 0), (0, width - a.shape[1])))


def _pad_rows(a, rows):
    return jnp.pad(a, ((0, rows - a.shape[0]), (0, 0)))


def _pack_small(c_ctx, b_ada, ln1_g, ln1_b, ln2_g, ln2_b, qg, kg, sink, d):
    misc = _pad_cols(jnp.concatenate([qg, kg, sink], axis=1), d)
    rows = jnp.concatenate([c_ctx.reshape(1, d), b_ada.reshape(6, d), ln1_g, ln1_b, ln2_g, ln2_b, misc], axis=0)
    return _pad_rows(rows, 16)


def _unpack_small(p, d):
    return dict(c_ctx=p[0], b_ada=p[1:7].reshape(1, 6 * d), ln1_g=p[7:8], ln1_b=p[8:9], ln2_g=p[9:10], ln2_b=p[10:11],
                q_norm_g=p[11:12, 0:HEAD], k_norm_g=p[11:12, HEAD:2 * HEAD], sink_logit=p[11:12, 2 * HEAD:2 * HEAD + 8])


def kernel(x, c, ctx, c_ctx, w_ada, b_ada, w_in, q_norm_g, k_norm_g, sink_logit, w_out, ln1_g, ln1_b, w_gate, w_up, w_down, ln2_g, ln2_b, loss_target, m_c_ctx, m_w_ada, m_b_ada, m_w_in, m_q_norm_g, m_k_norm_g, m_sink_logit, m_w_out, m_ln1_g, m_ln1_b, m_w_gate, m_w_up, m_w_down, m_ln2_g, m_ln2_b, v_c_ctx, v_w_ada, v_b_ada, v_w_in, v_q_norm_g, v_k_norm_g, v_sink_logit, v_w_out, v_ln1_g, v_ln1_b, v_w_gate, v_w_up, v_w_down, v_ln2_g, v_ln2_b):
    xs, cts, tgt = x[0], ctx[0], loss_target[0]
    n, d = xs.shape
    assert cts.shape == (CTX, d) and w_in.shape[2] == IN_SHARD and w_gate.shape[2] == FFN_SHARD
    me = 4 * lax.axis_index("x") + 2 * lax.axis_index("y") + lax.axis_index("c")
    e_sh = w_ada.shape[2]

    c_g = _exchange(_pad_rows(c, 8), False, "gather_c")
    c_all = jnp.concatenate([c_g[:, 0, :], _pad_rows(c_ctx.reshape(1, d), 8)], axis=0)
    bias_sh = lax.dynamic_slice(b_ada, (0, me * e_sh), (1, e_sh))
    mods_g = _exchange(_ada_fwd(c_all, w_ada[0], bias_sh), False, "gather_mods")
    mods = jnp.transpose(mods_g, (1, 0, 2)).reshape(16, NDEV * e_sh)
    mine = lax.dynamic_slice(mods, (me, 0), (1, 6 * d))
    sh1, sc1, g1, sh2, sc2, g2 = [mine[:, k * d:(k + 1) * d] for k in range(6)]
    csh1, csc1 = mods[8:9, 0:d], mods[8:9, d:2 * d]
    sc_pair = jnp.stack([sc1, csc1])
    sh_pair = jnp.stack([sh1, csh1])

    h_win, tok = _exchange_start(w_in[0].T.astype(BF16), "chip", mods, "gather_w_in_start")
    tok, (wo_l, wg_l, wu_l, wd_l) = lax.optimization_barrier((tok, (w_out, w_gate, w_up, w_down)))
    h_wout, tok = _exchange_start(wo_l[0].astype(BF16), "gather", tok, "gather_w_out_start")
    h_wg, tok = _exchange_start(wg_l[0].T.astype(BF16), "chip", tok, "gather_w_gate_start")
    h_wu, tok = _exchange_start(wu_l[0].T.astype(BF16), "chip", tok, "gather_w_up_start")
    h_wd, tok = _exchange_start(wd_l[0].astype(BF16), "chip", tok, "gather_w_down_start")

    cos, sa, sb = _rope_tables(n)
    f_win, tok = _forward_start(_exchange_wait(h_win, "chip", tok, "gather_w_in_wait"), tok, "forward_w_in_start")
    win_g = _forward_wait(f_win, tok, "forward_w_in_wait").reshape(NDEV * IN_SHARD, d)
    u_all, h_all, t_all, kt_b = _qkv_fwd(xs, cts, sc_pair, sh_pair, win_g, q_norm_g, k_norm_g, cos, sa, sb)
    o_a, p_a, linv_a = _attn_window_fwd(t_all, sink_logit, _window_bias(), t_all)
    o, p_b, linv_b = _attn_global_fwd(t_all, o_a)
    f_wg, tok = _forward_start(_exchange_wait(h_wg, "chip", o, "gather_w_gate_wait"), o, "forward_w_gate_start")
    f_wu, tok = _forward_start(_exchange_wait(h_wu, "chip", tok, "gather_w_up_wait"), tok, "forward_w_up_start")
    wout_g = _exchange_wait(h_wout, "gather", tok, "gather_w_out_wait").reshape(d, d)
    a1, xh1, rs1, u2 = _outproj_ln1(o, wout_g, xs, g1, ln1_g, ln1_b, sc2, sh2, tok)
    f_wd, tok = _forward_start(_exchange_wait(h_wd, "chip", rs1, "gather_w_down_wait"), rs1, "forward_w_down_start")
    ffn_w = (NDEV * FFN_SHARD, d)
    wg_g = _forward_wait(f_wg, tok, "forward_w_gate_wait").reshape(ffn_w)
    wu_g = _forward_wait(f_wu, tok, "forward_w_up_wait").reshape(ffn_w)
    sa_f, sb_f, hf = _ffn_up(u2, wg_g, wu_g, tok)
    wd_g = _forward_wait(f_wd, hf, "forward_w_down_wait").reshape(ffn_w)
    ffn = _ffn_down(hf, wd_g)
    dr2, df, loss_p, acc2 = _ln2_loss(xh1, ffn, tgt, ln1_g, ln1_b, g2, ln2_g, ln2_b)
    loss = lax.psum(loss_p[0, 0], ("x", "y", "c"))

    parts = (NDEV, FFN_SHARD, d)
    dgp = _ffn_dhf(df, wd_g, sa_f, sb_f)
    dwd_p = _dw_rows(hf, df, hf.shape[1] // FFN_TILE, FFN_TILE, min(n, 2048), loss_p, "dw_down").reshape(parts)
    h_dwd, tok = _exchange_start(dwd_p, "scatter", loss.reshape(1, 1), "scatter_dw_down_start")
    dwg_t, dwu_t = _dw_gate_up(dgp, u2, tok)
    h_dwg, tok = _exchange_start(dwg_t.reshape(parts), "scatter", tok, "scatter_dw_gate_start")
    h_dwu, tok = _exchange_start(dwu_t.reshape(parts), "scatter", tok, "scatter_dw_up_start")
    du2 = _ffn_du2(dgp, wg_g, wu_g, tok)
    dr1, da1, acc1 = _ln1_bwd(du2, dr2, xh1, rs1, a1, ln1_g, ln1_b, sc2, g1)
    dwo_p = _dw_rows(o, da1, 2, 8 * HEAD, min(n, 1024), loss_p, "dw_out").reshape(NDEV, 2 * HEAD, d)
    h_dwo, tok = _exchange_start(dwo_p, "scatter", loss_p, "scatter_dw_out_start")
    do = _outproj_bwd(da1, wout_g, tok)
    dqa, dka, dva, dsink = _attn_window_bwd(t_all, o, do, p_a, linv_a)
    dqb, dkb, dvb = _attn_global_bwd(t_all, kt_b, o, do, p_b, linv_b)
    dh_all, dnorm = _qkv_bwd_prep(dqa, dka, dva, dqb, dkb, dvb, h_all, q_norm_g, k_norm_g, cos, sa, sb)
    grad_x, acc0 = _qkv_bwd(dh_all, win_g, xs, cts, dr1, sc_pair)

    misc = _pad_cols(jnp.concatenate([dnorm[0:1], dnorm[1:2], dsink[:, 0:4, 0].reshape(1, 8)], axis=1), d)
    part = jnp.concatenate([
        acc0[0:2], acc1[4:5], acc1[1:2], acc1[0:1], acc2[2:3],
        acc0[2:4],
        acc1[2:4], acc2[0:2],
        misc, jnp.zeros((3, d), F32)], axis=0)
    gath = _exchange(part, False, "gather_small")
    dm_batch = gath[:, 0:6, :].reshape(NDEV, 6 * d)
    dm_ctx = _pad_cols(gath[:, 6:8, :].reshape(NDEV, 2 * d), 6 * d)
    dm16 = lax.dynamic_slice(jnp.concatenate([dm_batch, dm_ctx], axis=0), (0, me * e_sh), (16, e_sh))
    dw_ada, drow = _ada_bwd(dm16, c_all, w_ada[0])
    dcc = _exchange(drow, False, "gather_dcc")
    dwi_p = _dw_rows(dh_all, u_all, NDEV // 2, 2 * IN_SHARD, (n + CTX) // 2, dcc, "dw_in")
    dwi_p = dwi_p.reshape(NDEV, IN_SHARD, d)
    h_dwi, tok = _exchange_start(dwi_p, "scatter", dcc, "scatter_dw_in_start")

    w_s = _pack_small(c_ctx, b_ada, ln1_g, ln1_b, ln2_g, ln2_b, q_norm_g, k_norm_g, sink_logit, d)
    m_s = _pack_small(m_c_ctx, m_b_ada, m_ln1_g, m_ln1_b, m_ln2_g, m_ln2_b, m_q_norm_g, m_k_norm_g, m_sink_logit, d)
    v_s = _pack_small(v_c_ctx, v_b_ada, v_ln1_g, v_ln1_b, v_ln2_g, v_ln2_b, v_q_norm_g, v_k_norm_g, v_sink_logit, d)
    small = [_unpack_small(p, d) for p in _small_update(gath, dcc, c_ctx.reshape(1, d), w_s, m_s, v_s)]

    big = {}
    big["w_ada"] = _adamw(w_ada[0], dw_ada, m_w_ada[0], v_w_ada[0], "adamw_w_ada", after=tok)
    big["w_down"] = _adamw(w_down[0], _exchange_wait(h_dwd, "scatter", big["w_ada"][1], "scatter_dw_down_wait"),
                           m_w_down[0], v_w_down[0], "adamw_w_down")
    late = big["w_down"][1]
    for nm, wt, mt, vt, hd in (("w_gate", w_gate, m_w_gate, v_w_gate, h_dwg), ("w_up", w_up, m_w_up, v_w_up, h_dwu)):
        res = _adamw(wt[0].T, _exchange_wait(hd, "scatter", late, "scatter_d" + nm + "_wait"), mt[0].T, vt[0].T,
                     "adamw_" + nm)
        big[nm] = [r.T for r in res]
        late = res[1]
    big["w_out"] = _adamw(w_out[0], _exchange_wait(h_dwo, "scatter", late, "scatter_dw_out_wait"), m_w_out[0], v_w_out[0],
                          "adamw_w_out")
    big["w_in"] = _adamw_t(w_in[0], _exchange_wait(h_dwi, "scatter", big["w_out"][1], "scatter_dw_in_wait"), m_w_in[0],
                           v_w_in[0], "adamw_w_in")

    names = ["c_ctx", "w_ada", "b_ada", "w_in", "q_norm_g", "k_norm_g", "sink_logit", "w_out", "ln1_g", "ln1_b",
             "w_gate", "w_up", "w_down", "ln2_g", "ln2_b"]
    outs = [loss, grad_x[None]]
    for k in range(4):
        for nm in names:
            outs.append(big[nm][k][None] if nm in big else small[k][nm])
    return tuple(outs)
```
